```python
import jax, jax.numpy as jnp
from jax import lax
import numpy as np

D_MODEL = 1024
BATCH = 16
SEQ = 2048
DEPTH = 1

CTX_LEN = 256
GRID_W = 64
CHUNK = 128
RET_HEADS = 4
RET_DK = 256
RET_DV = 256
RET_W = RET_HEADS * RET_DK
ML_HEADS = 4
ML_DK = 256
ML_DV = 256
ML_W = ML_HEADS * ML_DK
CONV_W = 3
N_ML_GATES = 4 * ML_HEADS
N_GROUPS = 4
EXP_PER_GROUP = 8
N_EXPERTS = N_GROUPS * EXP_PER_GROUP
TOP_K = 2
D_EXPERT = 512
MOE_BLOCK = 128
DN_ALPHA = (2 * DEPTH) ** 0.25
DN_BETA = (8 * DEPTH) ** -0.25
ROPE_BASE = 10000.0
LN_EPS = 1e-5
NEG_INF = -1e30
IN_SIZES = [RET_W, RET_W, RET_W, RET_W, ML_W, ML_W, ML_W, ML_W, N_ML_GATES, D_MODEL, D_MODEL]
IN_COLS = int(sum(IN_SIZES))
IN_SPLITS = np.cumsum(IN_SIZES)[:-1].tolist()

kernel_name = "hybrid_retention_mlstm_hmoe_dit"

F32 = jnp.float32


def layer_norm(x):
    xf = x.astype(F32)
    mu = xf.mean(-1, keepdims=True)
    var = jnp.mean(jnp.square(xf - mu), -1, keepdims=True)
    return ((xf - mu) * lax.rsqrt(var + LN_EPS)).astype(x.dtype)


def post_norm(x, g, b):
    return layer_norm(x) * g + b


def modulate(h, shift, scale):
    return h * (1.0 + scale) + shift


def heads(t, n_heads):
    B, T, _ = t.shape
    return t.reshape(B, T, n_heads, -1).transpose(0, 2, 1, 3)


def merge_heads(t):
    B, H, T, d = t.shape
    return t.transpose(0, 2, 1, 3).reshape(B, T, H * d)


def flip(t):
    return jnp.flip(t, axis=2)


def rotary_2d(t, pos_row, pos_col):
    half = t.shape[-1] // 2
    quarter = half // 2
    freqs = ROPE_BASE ** (-jnp.arange(quarter, dtype=F32) / quarter)

    def rot(u, pos):
        ang = pos.astype(F32)[:, None] * freqs[None, :]
        cos, sin = jnp.cos(ang), jnp.sin(ang)
        u1, u2 = u[..., :quarter], u[..., quarter:]
        return jnp.concatenate([u1 * cos - u2 * sin, u1 * sin + u2 * cos], -1)

    out = jnp.concatenate([rot(t[..., :half], pos_row), rot(t[..., half:], pos_col)], -1)
    return out.astype(t.dtype)


def centred_conv(x, w, b):
    pad = CONV_W // 2
    T = x.shape[1]
    xp = jnp.pad(x, ((0, 0), (pad, pad), (0, 0)))
    return sum(xp[:, j:j + T, :] * w[j] for j in range(CONV_W)) + b


def to_chunks(t):
    B, H, T = t.shape[:3]
    t = t.reshape((B, H, T // CHUNK, CHUNK) + t.shape[3:])
    return jnp.moveaxis(t, 2, 0)


def from_chunks(t):
    t = jnp.moveaxis(t, 0, 2)
    B, H, N, L = t.shape[:4]
    return t.reshape((B, H, N * L) + t.shape[4:])


def intra_mask(inclusive):
    idx = jnp.arange(CHUNK)
    rel = idx[:, None] - idx[None, :]
    return (rel >= 0) if inclusive else (rel > 0), rel.astype(F32)


def retention_scan(q, k, v, log_gamma, s0, inclusive):
    q, k, v = q.astype(F32), k.astype(F32), v.astype(F32)
    mask, rel = intra_mask(inclusive)
    idx = jnp.arange(CHUNK, dtype=F32)
    lg = log_gamma[:, None, None]
    intra_decay = jnp.where(mask, jnp.exp(jnp.where(mask, rel, 0.0) * lg), 0.0)
    q_decay = jnp.exp((idx + 1.0)[None, :] * log_gamma[:, None])
    k_decay = jnp.exp((CHUNK - 1.0 - idx)[None, :] * log_gamma[:, None])
    chunk_decay = jnp.exp(CHUNK * log_gamma)

    def step(s, qkv):
        qc, kc, vc = qkv
        att = jnp.einsum('bhld,bhmd->bhlm', qc, kc) * intra_decay
        out = (jnp.einsum('bhlm,bhme->bhle', att, vc)
               + jnp.einsum('bhld,bhde->bhle', qc, s) * q_decay[None, :, :, None])
        s = (s * chunk_decay[None, :, None, None]
             + jnp.einsum('bhld,bhle->bhde', kc * k_decay[None, :, :, None], vc))
        return s, out

    s, out = lax.scan(step, s0, (to_chunks(q), to_chunks(k), to_chunks(v)))
    return from_chunks(out), s


def retention_state(k, v, log_gamma):
    T = k.shape[2]
    w = jnp.exp((T - 1.0 - jnp.arange(T, dtype=F32))[None, :] * log_gamma[:, None])
    return jnp.einsum('bhtd,bhte->bhde', k.astype(F32) * w[None, :, :, None], v.astype(F32))


def mlstm_scan(q, k, v, log_i, log_f, state0, inclusive):
    q, k, v = q.astype(F32), k.astype(F32), v.astype(F32)
    mask, _ = intra_mask(inclusive)

    def step(carry, inp):
        C, n, m = carry
        qc, kc, vc, ic, fc = inp
        b = jnp.cumsum(fc, axis=-1)
        d = jnp.where(mask, b[..., :, None] - b[..., None, :] + ic[..., None, :], NEG_INF)
        inter = b + m[..., None]
        m_row = jnp.maximum(d.max(-1), inter)
        w = jnp.exp(d - m_row[..., None])
        a = jnp.exp(inter - m_row)
        att = jnp.einsum('bhld,bhmd->bhlm', qc, kc) * w
        num = jnp.einsum('bhlm,bhme->bhle', att, vc) + a[..., None] * jnp.einsum('bhld,bhde->bhle', qc, C)
        den = att.sum(-1) + a * jnp.einsum('bhld,bhd->bhl', qc, n)
        h = num / jnp.maximum(jnp.abs(den), jnp.exp(-m_row))[..., None]
        b_last = b[..., -1]
        g = b_last[..., None] - b + ic
        m_new = jnp.maximum(b_last + m, g.max(-1))
        kw = kc * jnp.exp(g - m_new[..., None])[..., None]
        decay = jnp.exp(b_last + m - m_new)
        C = decay[..., None, None] * C + jnp.einsum('bhld,bhle->bhde', kw, vc)
        n = decay[..., None] * n + kw.sum(2)
        return (C, n, m_new), h

    state, h = lax.scan(step, state0, (to_chunks(q), to_chunks(k), to_chunks(v),
                                       to_chunks(log_i), to_chunks(log_f)))
    return from_chunks(h), state


def mlstm_state(k, v, log_i, log_f):
    k, v = k.astype(F32), v.astype(F32)
    F = jnp.cumsum(log_f, axis=-1)
    FT = F[..., -1]
    g = FT[..., None] - F + log_i
    m = jnp.maximum(FT, g.max(-1))
    kw = k * jnp.exp(g - m[..., None])[..., None]
    return (jnp.einsum('bhtd,bhte->bhde', kw, v), kw.sum(2), m)


def project_stream(u, w_in, b_mgate, conv_w, conv_b, pos):
    B, T, _ = u.shape
    rq, rk, rv, rg, mq, mk, mv, mo, mg, gate_r, gate_m = jnp.split(u @ w_in, IN_SPLITS, axis=-1)
    rq, rk, rv = heads(rq, RET_HEADS), heads(rk, RET_HEADS) * RET_DK ** -0.5, heads(rv, RET_HEADS)
    if pos is not None:
        rq, rk = rotary_2d(rq, *pos), rotary_2d(rk, *pos)
    mq = jax.nn.silu(centred_conv(mq, conv_w[:, :ML_W], conv_b[:ML_W]))
    mk = jax.nn.silu(centred_conv(mk, conv_w[:, ML_W:], conv_b[ML_W:]))
    mq, mk, mv = heads(mq, ML_HEADS), heads(mk, ML_HEADS) * ML_DK ** -0.5, heads(mv, ML_HEADS)
    g = (mg + b_mgate).astype(F32).reshape(B, T, 4, ML_HEADS).transpose(2, 0, 3, 1)
    ml_gates = (g[0], jax.nn.log_sigmoid(g[1]), g[2], jax.nn.log_sigmoid(g[3]))
    return (rq, rk, rv, rg), (mq, mk, mv, mo), ml_gates, (gate_r, gate_m)


def merge_branches(ret, mls, rg, mo, gate_r, gate_m, w_ret_branch, w_ml_branch, w_out):
    r = merge_heads(layer_norm(ret)).astype(rg.dtype) * jax.nn.silu(rg)
    m = merge_heads(layer_norm(mls)).astype(mo.dtype) * jax.nn.sigmoid(mo)
    y = jax.nn.sigmoid(gate_r) * (r @ w_ret_branch) + jax.nn.sigmoid(gate_m) * (m @ w_ml_branch)
    return y @ w_out


def token_mixer(u_lat, u_ctx, pos, w_in, b_mgate, conv_w, conv_b, decay_logit,
                w_ret_branch, w_ml_branch, w_out, need_ctx_out):
    B = u_lat.shape[0]
    lg = jax.nn.log_sigmoid(decay_logit.astype(F32))
    (crq, crk, crv, crg), (cmq, cmk, cmv, cmo), (ci_f, clf_f, ci_b, clf_b), (cgr, cgm) = \
        project_stream(u_ctx, w_in, b_mgate, conv_w, conv_b, None)
    (lrq, lrk, lrv, lrg), (lmq, lmk, lmv, lmo), (li_f, llf_f, li_b, llf_b), (lgr, lgm) = \
        project_stream(u_lat, w_in, b_mgate, conv_w, conv_b, pos)
    merge = lambda ret, mls, rg, mo, gr, gm: merge_branches(ret, mls, rg, mo, gr, gm,
                                                           w_ret_branch, w_ml_branch, w_out)
    if need_ctx_out:
        zr = jnp.zeros((B, RET_HEADS, RET_DK, RET_DV), F32)
        zm = (jnp.zeros((B, ML_HEADS, ML_DK, ML_DV), F32), jnp.zeros((B, ML_HEADS, ML_DK), F32),
              jnp.zeros((B, ML_HEADS), F32))
        cr_f, s_f = retention_scan(crq, crk, crv, lg[0], zr, True)
        cr_b, s_b = retention_scan(flip(crq), flip(crk), flip(crv), lg[1], zr, False)
        cm_f, m_f = mlstm_scan(cmq, cmk, cmv, ci_f, clf_f, zm, True)
        cm_b, m_b = mlstm_scan(flip(cmq), flip(cmk), flip(cmv), flip(ci_b), flip(clf_b), zm, False)
        y_ctx = merge(cr_f + flip(cr_b), cm_f + flip(cm_b), crg, cmo, cgr, cgm)
    else:
        s_f = retention_state(crk, crv, lg[0])
        s_b = retention_state(flip(crk), flip(crv), lg[1])
        m_f = mlstm_state(cmk, cmv, ci_f, clf_f)
        m_b = mlstm_state(flip(cmk), flip(cmv), flip(ci_b), flip(clf_b))
        y_ctx = None
    lr_f, _ = retention_scan(lrq, lrk, lrv, lg[0], s_f, True)
    lr_b, _ = retention_scan(flip(lrq), flip(lrk), flip(lrv), lg[1], s_b, False)
    lm_f, _ = mlstm_scan(lmq, lmk, lmv, li_f, llf_f, m_f, True)
    lm_b, _ = mlstm_scan(flip(lmq), flip(lmk), flip(lmv), flip(li_b), flip(llf_b), m_b, False)
    y_lat = merge(lr_f + flip(lr_b), lm_f + flip(lm_b), lrg, lmo, lgr, lgm)
    return y_lat, y_ctx


def hier_moe(u, w_rg, b_rg, w_re, b_re, w_e1, w_e3, w_e2):
    N, D = u.shape
    p_g = jax.nn.softmax((u @ w_rg + b_rg).astype(F32), axis=-1)
    pg_top, g_idx = lax.top_k(p_g, 1)
    le = (u @ w_re + b_re).astype(F32).reshape(N, N_GROUPS, EXP_PER_GROUP)
    le = jnp.take_along_axis(le, g_idx[:, :, None], axis=1)[:, 0]
    pe_top, e_idx = lax.top_k(jax.nn.softmax(le, axis=-1), TOP_K)
    w = pg_top * pe_top / pe_top.sum(-1, keepdims=True)
    e_flat = (g_idx * EXP_PER_GROUP + e_idx).reshape(-1)
    w_flat = w.reshape(-1)
    t_flat = jnp.repeat(jnp.arange(N, dtype=jnp.int32), TOP_K)
    n_assign = N * TOP_K
    n_blocks = -(-n_assign // MOE_BLOCK) + N_EXPERTS
    n_slots = n_blocks * MOE_BLOCK
    order = jnp.argsort(e_flat)
    se, st, sw = e_flat[order], t_flat[order], w_flat[order]
    counts = jnp.zeros((N_EXPERTS,), jnp.int32).at[e_flat].add(1)
    offsets = jnp.cumsum(counts) - counts
    padded = (counts + MOE_BLOCK - 1) // MOE_BLOCK * MOE_BLOCK
    pad_end = jnp.cumsum(padded)
    pad_off = pad_end - padded
    dest = pad_off[se] + jnp.arange(n_assign, dtype=jnp.int32) - offsets[se]
    slot_tok = jnp.zeros((n_slots,), jnp.int32).at[dest].set(st)
    slot_w = jnp.zeros((n_slots,), F32).at[dest].set(sw)
    block_start = jnp.arange(n_blocks, dtype=jnp.int32) * MOE_BLOCK
    block_exp = jnp.minimum((block_start[:, None] >= pad_end[None, :]).sum(1), N_EXPERTS - 1)
    xs = u[slot_tok].reshape(n_blocks, MOE_BLOCK, D)

    def expert_block(args):
        xb, e = args
        return (jax.nn.silu(xb @ w_e1[e]) * (xb @ w_e3[e])) @ w_e2[e]

    ys = lax.map(expert_block, (xs, block_exp)).reshape(n_slots, D)
    out = jnp.zeros((N, D), F32).at[slot_tok].add(slot_w[:, None] * ys)
    return out.astype(u.dtype)


def setup_inputs(seed: int = 0) -> dict:
    key = jax.random.key(seed)
    ks = iter(jax.random.split(key, 32))
    nrm = lambda shape, scale: scale * jax.random.normal(next(ks), shape, F32)
    D = D_MODEL
    lin = jnp.linspace(3.0, 6.0, ML_HEADS, dtype=F32)
    zh = jnp.zeros((ML_HEADS,), F32)
    gate_base = jnp.concatenate([zh, lin, zh, lin])[None, :]
    decay_base = jnp.log(2.0 ** (5.0 + jnp.arange(RET_HEADS, dtype=F32)) - 1.0)
    return {
        "x": nrm((BATCH, SEQ, D), 1.0),
        "c": nrm((BATCH, D), 1.0),
        "ctx": nrm((BATCH, CTX_LEN, D), 1.0),
        "c_ctx": nrm((D,), 1.0),
        "w_ada": nrm((DEPTH, D, 6 * D), D ** -0.5),
        "b_ada": nrm((DEPTH, 6 * D), 0.02),
        "w_in": nrm((DEPTH, D, IN_COLS), D ** -0.5),
        "b_mgate": gate_base + nrm((DEPTH, N_ML_GATES), 0.1),
        "ml_conv_w": nrm((DEPTH, CONV_W, 2 * ML_W), CONV_W ** -0.5),
        "ml_conv_b": nrm((DEPTH, 2 * ML_W), 0.02),
        "ret_decay_logit": decay_base[None, None, :] + nrm((DEPTH, 2, RET_HEADS), 0.01),
        "w_ret_branch": nrm((DEPTH, RET_HEADS * RET_DV, D), (RET_HEADS * RET_DV) ** -0.5),
        "w_ml_branch": nrm((DEPTH, ML_HEADS * ML_DV, D), (ML_HEADS * ML_DV) ** -0.5),
        "w_out": nrm((DEPTH, D, D), DN_BETA * D ** -0.5),
        "ln1_g": 1.0 + nrm((DEPTH, D), 0.02),
        "ln1_b": nrm((DEPTH, D), 0.02),
        "w_rg": nrm((DEPTH, D, N_GROUPS), D ** -0.5),
        "b_rg": nrm((DEPTH, N_GROUPS), 0.01),
        "w_re": nrm((DEPTH, D, N_EXPERTS), D ** -0.5),
        "b_re": nrm((DEPTH, N_EXPERTS), 0.01),
        "w_e1": nrm((DEPTH, N_EXPERTS, D, D_EXPERT), D ** -0.5),
        "w_e3": nrm((DEPTH, N_EXPERTS, D, D_EXPERT), D ** -0.5),
        "w_e2": nrm((DEPTH, N_EXPERTS, D_EXPERT, D), DN_BETA * D_EXPERT ** -0.5),
        "ln2_g": 1.0 + nrm((DEPTH, D), 0.02),
        "ln2_b": nrm((DEPTH, D), 0.02),
    }


def reference(x, c, ctx, c_ctx, w_ada, b_ada, w_in, b_mgate, ml_conv_w, ml_conv_b, ret_decay_logit,
              w_ret_branch, w_ml_branch, w_out, ln1_g, ln1_b, w_rg, b_rg, w_re, b_re,
              w_e1, w_e3, w_e2, ln2_g, ln2_b):
    B, T, D = x.shape
    ROWS = T // GRID_W
    pos = (jnp.repeat(jnp.arange(ROWS), GRID_W), jnp.tile(jnp.arange(GRID_W), ROWS))
    h_ctx = ctx
    for l in range(DEPTH):
        last = l == DEPTH - 1
        mod = jax.nn.silu(c) @ w_ada[l] + b_ada[l]
        sh1, sc1, g1, sh2, sc2, g2 = jnp.split(mod[:, None, :], 6, axis=-1)
        cmod = jax.nn.silu(c_ctx) @ w_ada[l] + b_ada[l]
        csh1, csc1, cg1, csh2, csc2, cg2 = jnp.split(cmod[None, None, :], 6, axis=-1)
        u_lat = modulate(layer_norm(x), sh1, sc1)
        u_ctx = modulate(layer_norm(h_ctx), csh1, csc1)
        y_lat, y_ctx = token_mixer(u_lat, u_ctx, pos, w_in[l], b_mgate[l], ml_conv_w[l], ml_conv_b[l],
                                   ret_decay_logit[l], w_ret_branch[l], w_ml_branch[l], w_out[l],
                                   not last)
        x = post_norm(DN_ALPHA * x + g1 * y_lat, ln1_g[l], ln1_b[l])
        moe_p = (w_rg[l], b_rg[l], w_re[l], b_re[l], w_e1[l], w_e3[l], w_e2[l])
        u2 = modulate(layer_norm(x), sh2, sc2)
        if last:
            f_lat = hier_moe(u2.reshape(-1, D), *moe_p).reshape(x.shape)
        else:
            h_ctx = post_norm(DN_ALPHA * h_ctx + cg1 * y_ctx, ln1_g[l], ln1_b[l])
            u2c = modulate(layer_norm(h_ctx), csh2, csc2)
            f_all = hier_moe(jnp.concatenate([u2.reshape(-1, D), u2c.reshape(-1, D)], axis=0), *moe_p)
            f_lat = f_all[:B * T].reshape(x.shape)
            h_ctx = post_norm(DN_ALPHA * h_ctx + cg2 * f_all[B * T:].reshape(h_ctx.shape),
                              ln2_g[l], ln2_b[l])
        x = post_norm(DN_ALPHA * x + g2 * f_lat, ln2_g[l], ln2_b[l])
    return x
```

```python
import functools

import jax
import jax.numpy as jnp
from jax import lax
from jax.experimental import pallas as pl
from jax.experimental.pallas import tpu as pltpu
from jax.experimental.pallas import tpu_sc as plsc

F32 = jnp.float32
BF16 = jnp.bfloat16
U32 = jnp.uint32
I32 = jnp.int32
HIGHEST = lax.Precision.HIGHEST

HEADS = 4
HEAD_W = 256
BRANCH_W = HEADS * HEAD_W
CHUNK = 128
GRID_W = 64
ROPE_BASE = 10000.0
N_GATES = 16
N_GROUPS = 4
EXP_PER_GROUP = 8
N_EXPERTS = N_GROUPS * EXP_PER_GROUP
LN_EPS = 1e-5
NEG_INF = -1e30
KEY_SCALE = HEAD_W ** -0.5

PROJ_TM = 1024
MERGE_TM = 512
ROUTE_TM = 512
MOE_BLK = 256
SC_WIN = 128
PACK_W = 256
ROUTE_ROWS = 64
VMEM_LIMIT = 48 * 1024 * 1024

NT_DIMS = (((1,), (1,)), ((), ()))
TN_DIMS = (((0,), (0,)), ((), ()))


def _cparams(*sem):
    return pltpu.CompilerParams(dimension_semantics=sem, vmem_limit_bytes=VMEM_LIMIT)


def _layer_norm(x):
    mu = jnp.mean(x, axis=-1, keepdims=True)
    xc = x - mu
    var = jnp.mean(xc * xc, axis=-1, keepdims=True)
    return xc * lax.rsqrt(var + LN_EPS)


def _log_sigmoid(x):
    return jnp.minimum(x, 0.0) - jnp.log1p(jnp.exp(-jnp.abs(x)))


def _silu(x):
    return x * jax.nn.sigmoid(x)


def _pack_pairs(hi, lo):
    hb = lax.bitcast_convert_type(hi.astype(BF16).astype(F32), U32)
    lb = lax.bitcast_convert_type(lo.astype(BF16).astype(F32), U32)
    return (hb & jnp.uint32(0xFFFF0000)) | (lb >> 16)


def _unpack_pairs(p):
    hi = lax.bitcast_convert_type(p & jnp.uint32(0xFFFF0000), F32)
    lo = lax.bitcast_convert_type(p << 16, F32)
    return hi, lo


def _ada_kernel(c_ref, w_ref, b_ref, o_ref):
    s = _silu(c_ref[...])
    o_ref[...] = jnp.dot(s, w_ref[...], precision=HIGHEST, preferred_element_type=F32) + b_ref[...]


def _ada(cs, w, b):
    rows, d = cs.shape
    cols = w.shape[1]
    tn = 1024
    return pl.pallas_call(
        _ada_kernel,
        out_shape=jax.ShapeDtypeStruct((rows, cols), F32),
        grid=(cols // tn,),
        in_specs=[pl.BlockSpec((rows, d), lambda j: (0, 0)),
                  pl.BlockSpec((d, tn), lambda j: (0, j)),
                  pl.BlockSpec((1, tn), lambda j: (0, j))],
        out_specs=pl.BlockSpec((rows, tn), lambda j: (0, j)),
        compiler_params=_cparams("parallel"),
        name="ada",
    )(cs, w, b)


def _proj_kernel(kinds, x_ref, sh_ref, sc_ref, w_ref, wg_ref, bg_ref, *rest):
    if "rot" in kinds or "rot_scale" in kinds:
        cos_ref, sin_ref, o_ref, gt_ref, u_ref = rest
    else:
        o_ref, gt_ref, u_ref = rest
    j = pl.program_id(2)

    @pl.when(j == 0)
    def _():
        u = _layer_norm(x_ref[0]) * (1.0 + sc_ref[0]) + sh_ref[0]
        ub = u.astype(BF16)
        u_ref[...] = ub
        gt_ref[0] = lax.dot_general(wg_ref[...], ub, NT_DIMS, preferred_element_type=F32) + bg_ref[...]

    acc = jnp.dot(u_ref[...], w_ref[...], preferred_element_type=F32)

    def rotary(scale):
        for s in range(acc.shape[1] // 128):
            a = acc[:, s * 128:(s + 1) * 128]
            half = s % 2
            cs = cos_ref[:, half * 128:(half + 1) * 128]
            sn = sin_ref[:, half * 128:(half + 1) * 128]
            r = a * cs + pltpu.roll(a, 64, 1) * sn
            if scale != 1.0:
                r = r * scale
            o_ref[0, :, s * 128:(s + 1) * 128] = r.astype(BF16)

    by_kind = {}
    for s, kind in enumerate(kinds):
        by_kind.setdefault(kind, []).append(s)
    for kind, secs in by_kind.items():
        cond = functools.reduce(jnp.logical_or, [j == s for s in secs])

        @pl.when(cond)
        def _(kind=kind):
            if kind == "rot":
                rotary(1.0)
            elif kind == "rot_scale":
                rotary(KEY_SCALE)
            elif kind == "scale":
                o_ref[0] = (acc * KEY_SCALE).astype(BF16)
            else:
                o_ref[0] = acc.astype(BF16)


def _proj(x, sh, sc, w_main, w_gate_t, b_gate, kinds, tables=None):
    B, T, D = x.shape
    n_sec = len(kinds)
    tm = min(PROJ_TM, T)
    tn = BRANCH_W
    in_specs = [
        pl.BlockSpec((1, tm, D), lambda i, b, j: (b, i, 0)),
        pl.BlockSpec((1, 1, D), lambda i, b, j: (b, 0, 0)),
        pl.BlockSpec((1, 1, D), lambda i, b, j: (b, 0, 0)),
        pl.BlockSpec((D, tn), lambda i, b, j: (0, j)),
        pl.BlockSpec((N_GATES, D), lambda i, b, j: (0, 0)),
        pl.BlockSpec((N_GATES, 1), lambda i, b, j: (0, 0)),
    ]
    args = [x, sh, sc, w_main, w_gate_t, b_gate]
    if tables is not None:
        in_specs += [pl.BlockSpec((tm, HEAD_W), lambda i, b, j: (i, 0))] * 2
        args += list(tables)
    return pl.pallas_call(
        functools.partial(_proj_kernel, kinds),
        out_shape=(jax.ShapeDtypeStruct((B, T, n_sec * tn), BF16),
                   jax.ShapeDtypeStruct((B, N_GATES, T), F32)),
        grid=(T // tm, B, n_sec),
        in_specs=in_specs,
        out_specs=(pl.BlockSpec((1, tm, tn), lambda i, b, j: (b, i, j)),
                   pl.BlockSpec((1, N_GATES, tm), lambda i, b, j: (b, 0, i))),
        scratch_shapes=[pltpu.VMEM((tm, D), BF16)],
        compiler_params=_cparams("parallel", "parallel", "arbitrary"),
        name="proj_lat" if tables is not None else "proj_ctx",
    )(*args)


def _ret_kernel(dl_ref, q_ref, k_ref, v_ref, rg_ref, ck_ref, cv_ref, o_ref,
                acc_ref, sf_ref, sb_ref, dec_ref, d_ref):
    h = pl.program_id(1)
    L = CHUNK
    n_chunks = q_ref.shape[1] // L
    n_ctx_chunks = ck_ref.shape[1] // L
    lgf = _log_sigmoid(jnp.full((1, 1), dl_ref[0, h], F32))
    lgb = _log_sigmoid(jnp.full((1, 1), dl_ref[1, h], F32))

    ri = lax.broadcasted_iota(I32, (L, L), 0)
    ci = lax.broadcasted_iota(I32, (L, L), 1)
    rel = (ri - ci).astype(F32)
    d_ref[...] = jnp.where(rel >= 0.0, jnp.exp(jnp.maximum(rel, 0.0) * lgf),
                           jnp.exp(jnp.maximum(-rel, 0.0) * lgb))
    row = lax.broadcasted_iota(I32, (L, HEAD_W), 0).astype(F32)
    dec_ref[0] = jnp.exp((row + 1.0) * lgf)
    dec_ref[1] = jnp.exp((L - 1.0 - row) * lgf)
    dec_ref[2] = jnp.exp((L - row) * lgb)
    dec_ref[3] = jnp.exp(row * lgb)
    cdf = jnp.exp(L * lgf)
    cdb = jnp.exp(L * lgb)

    def update(s_ref, kc, vc, kd, cd):
        kdec = (kc.astype(F32) * kd).astype(BF16)
        s_ref[...] = s_ref[...] * cd + lax.dot_general(kdec, vc, TN_DIMS, preferred_element_type=F32)

    sf_ref[...] = jnp.zeros_like(sf_ref)
    sb_ref[...] = jnp.zeros_like(sb_ref)
    for c in range(n_ctx_chunks):
        update(sf_ref, ck_ref[0, c * L:(c + 1) * L, :], cv_ref[0, c * L:(c + 1) * L, :], dec_ref[1], cdf)
    for c in reversed(range(n_ctx_chunks)):
        update(sb_ref, ck_ref[0, c * L:(c + 1) * L, :], cv_ref[0, c * L:(c + 1) * L, :], dec_ref[3], cdb)

    def fwd(c, carry):
        r0 = pl.multiple_of(c * L, L)
        q = q_ref[0, pl.ds(r0, L), :]
        k = k_ref[0, pl.ds(r0, L), :]
        v = v_ref[0, pl.ds(r0, L), :]
        s = lax.dot_general(q, k, NT_DIMS, preferred_element_type=F32)
        att = (s * d_ref[...]).astype(BF16)
        o = jnp.dot(att, v, preferred_element_type=F32)
        o = o + jnp.dot(q, sf_ref[...].astype(BF16), preferred_element_type=F32) * dec_ref[0]
        acc_ref[pl.ds(r0, L), :] = o
        update(sf_ref, k, v, dec_ref[1], cdf)
        return carry

    lax.fori_loop(0, n_chunks, fwd, 0)

    def bwd(i, carry):
        r0 = pl.multiple_of((n_chunks - 1 - i) * L, L)
        q = q_ref[0, pl.ds(r0, L), :]
        k = k_ref[0, pl.ds(r0, L), :]
        v = v_ref[0, pl.ds(r0, L), :]
        o = acc_ref[pl.ds(r0, L), :]
        o = o + jnp.dot(q, sb_ref[...].astype(BF16), preferred_element_type=F32) * dec_ref[2]
        rg = rg_ref[0, pl.ds(r0, L), :].astype(F32)
        o_ref[0, pl.ds(r0, L), :] = (_layer_norm(o) * _silu(rg)).astype(BF16)
        update(sb_ref, k, v, dec_ref[3], cdb)
        return carry

    lax.fori_loop(0, n_chunks, bwd, 0)


def _retention(decay_logit, p_lat, p_ctx, sec_lat, sec_ctx):
    B, T, _ = p_lat.shape
    Tc = p_ctx.shape[1]

    def lat(sec):
        return pl.BlockSpec((1, T, HEAD_W), lambda b, h: (b, 0, sec * HEADS + h))

    def cx(sec):
        return pl.BlockSpec((1, Tc, HEAD_W), lambda b, h: (b, 0, sec * HEADS + h))

    return pl.pallas_call(
        _ret_kernel,
        out_shape=jax.ShapeDtypeStruct((B, T, BRANCH_W), BF16),
        grid=(B, HEADS),
        in_specs=[pl.BlockSpec(memory_space=pltpu.SMEM)]
        + [lat(s) for s in sec_lat] + [cx(s) for s in sec_ctx],
        out_specs=pl.BlockSpec((1, T, HEAD_W), lambda b, h: (b, 0, h)),
        scratch_shapes=[pltpu.VMEM((T, HEAD_W), F32),
                        pltpu.VMEM((HEAD_W, HEAD_W), F32),
                        pltpu.VMEM((HEAD_W, HEAD_W), F32),
                        pltpu.VMEM((4, CHUNK, HEAD_W), F32),
                        pltpu.VMEM((CHUNK, CHUNK), F32)],
        compiler_params=_cparams("parallel", "parallel"),
        name="retention",
    )(decay_logit, p_lat, p_lat, p_lat, p_lat, p_ctx, p_ctx)


def _mlstm_kernel(q_ref, k_ref, v_ref, mo_ref, ck_ref, cv_ref, g_ref, gt_ref, cg_ref, cgt_ref,
                  wq_ref, bq_ref, wk_ref, bk_ref, o_ref,
                  xf_ref, qs_ref, ks_ref, cks_ref, acc_ref, c_ref, n_ref, m_ref):
    L = CHUNK
    T = q_ref.shape[1]
    Tc = ck_ref.shape[1]
    n_chunks = T // L
    n_ctx_chunks = Tc // L

    def conv_silu(src_ref, w_ref, b_ref, dst_ref, t_len, scale):
        xf_ref[pl.ds(0, 8), :] = jnp.zeros((8, HEAD_W), F32)
        xf_ref[pl.ds(8 + t_len, 8), :] = jnp.zeros((8, HEAD_W), F32)
        xf_ref[pl.ds(8, t_len), :] = src_ref[0].astype(F32)
        w = w_ref[...]
        b = b_ref[...]

        def body(c, carry):
            r0 = pl.multiple_of(c * L, L)
            win = xf_ref[pl.ds(r0, L + 16), :]
            prev = pltpu.roll(win, 1, 0)[8:8 + L, :]
            cur = win[8:8 + L, :]
            nxt = pltpu.roll(win, L + 15, 0)[8:8 + L, :]
            y = prev * w[0:1, :] + cur * w[1:2, :] + nxt * w[2:3, :] + b
            y = _silu(y)
            if scale != 1.0:
                y = y * scale
            dst_ref[pl.ds(r0, L), :] = y.astype(BF16)
            return carry

        lax.fori_loop(0, t_len // L, body, 0)

    conv_silu(q_ref, wq_ref, bq_ref, qs_ref, T, 1.0)
    conv_silu(k_ref, wk_ref, bk_ref, ks_ref, T, KEY_SCALE)
    conv_silu(ck_ref, wk_ref, bk_ref, cks_ref, Tc, KEY_SCALE)

    ri = lax.broadcasted_iota(I32, (L, L), 0)
    ci = lax.broadcasted_iota(I32, (L, L), 1)
    tri_l = (ci <= ri).astype(F32)
    tri_u = (ci >= ri).astype(F32)

    def gates(gc, gtc, backward):
        lane = lax.broadcasted_iota(I32, gc.shape, 1)
        gl = jnp.where(lane % 2 == 1, _log_sigmoid(gc), gc)
        sub = lax.broadcasted_iota(I32, gtc.shape, 0)
        gtl = jnp.where(sub % 2 == 1, _log_sigmoid(gtc), gtc)
        if not backward:
            cs_col = jnp.dot(tri_l, gl, precision=HIGHEST, preferred_element_type=F32)
            cs_row = jnp.dot(gtl, tri_u, precision=HIGHEST, preferred_element_type=F32)
            return (gl[:, 0:1], gtl[0:1, :], cs_col[:, 1:2], cs_row[1:2, :], cs_col[L - 1:L, 1:2])
        cs_col = jnp.dot(tri_u, gl, precision=HIGHEST, preferred_element_type=F32)
        cs_row = jnp.dot(gtl, tri_l, precision=HIGHEST, preferred_element_type=F32)
        return (gl[:, 2:3], gtl[2:3, :], cs_col[:, 3:4], cs_row[3:4, :], cs_col[0:1, 3:4])

    def step(q, k, v, gate_vecs, mask, want_out):
        i_col, i_row, b_col, b_row, b_last = gate_vecs
        m = m_ref[...]
        h = None
        if want_out:
            d = jnp.where(mask, b_col - b_row + i_row, NEG_INF)
            inter = b_col + m
            m_row = jnp.maximum(jnp.max(d, axis=-1, keepdims=True), inter)
            w = jnp.exp(d - m_row)
            a = jnp.exp(inter - m_row)
            att = lax.dot_general(q, k, NT_DIMS, preferred_element_type=F32) * w
            num = jnp.dot(att.astype(BF16), v, preferred_element_type=F32)
            num = num + a * jnp.dot(q, c_ref[...].astype(BF16), preferred_element_type=F32)
            qn = jnp.sum(q.astype(F32) * n_ref[...], axis=-1, keepdims=True)
            den = jnp.sum(att, axis=-1, keepdims=True) + a * qn
            h = num * (1.0 / jnp.maximum(jnp.abs(den), jnp.exp(-m_row)))
        g_col = b_last - b_col + i_col
        g_row = b_last - b_row + i_row
        m_new = jnp.maximum(b_last + m, jnp.max(g_row, axis=-1, keepdims=True))
        kw = k.astype(F32) * jnp.exp(g_col - m_new)
        decay = jnp.exp(b_last + m - m_new)
        c_ref[...] = decay * c_ref[...] + lax.dot_general(kw.astype(BF16), v, TN_DIMS,
                                                          preferred_element_type=F32)
        n_ref[...] = decay * n_ref[...] + jnp.sum(kw, axis=0, keepdims=True)
        m_ref[...] = m_new
        return h

    def reset_state():
        c_ref[...] = jnp.zeros_like(c_ref)
        n_ref[...] = jnp.zeros_like(n_ref)
        m_ref[...] = jnp.zeros_like(m_ref)

    def ctx_step(c, backward):
        gv = gates(cg_ref[0, 0, c * L:(c + 1) * L, :], cgt_ref[0, 0, :, c * L:(c + 1) * L], backward)
        step(None, cks_ref[c * L:(c + 1) * L, :], cv_ref[0, c * L:(c + 1) * L, :], gv, None, False)

    def lat_step(r0, backward, mask):
        gv = gates(g_ref[0, 0, pl.ds(r0, L), :], gt_ref[0, 0, :, pl.ds(r0, L)], backward)
        return step(qs_ref[pl.ds(r0, L), :], ks_ref[pl.ds(r0, L), :], v_ref[0, pl.ds(r0, L), :],
                    gv, mask, True)

    reset_state()
    for c in range(n_ctx_chunks):
        ctx_step(c, False)

    def fwd(c, carry):
        r0 = pl.multiple_of(c * L, L)
        acc_ref[pl.ds(r0, L), :] = lat_step(r0, False, ci <= ri)
        return carry

    lax.fori_loop(0, n_chunks, fwd, 0)

    reset_state()
    for c in reversed(range(n_ctx_chunks)):
        ctx_step(c, True)

    def bwd(i, carry):
        r0 = pl.multiple_of((n_chunks - 1 - i) * L, L)
        tot = acc_ref[pl.ds(r0, L), :] + lat_step(r0, True, ci > ri)
        mo = mo_ref[0, pl.ds(r0, L), :].astype(F32)
        o_ref[0, pl.ds(r0, L), :] = (_layer_norm(tot) * jax.nn.sigmoid(mo)).astype(BF16)
        return carry

    lax.fori_loop(0, n_chunks, bwd, 0)


def _mlstm(p_lat, p_ctx, g, gt, cg, cgt, conv_w, conv_b, sec_lat, sec_ctx):
    B, T, _ = p_lat.shape
    Tc = p_ctx.shape[1]

    def lat(sec):
        return pl.BlockSpec((1, T, HEAD_W), lambda b, h: (b, 0, sec * HEADS + h))

    def cx(sec):
        return pl.BlockSpec((1, Tc, HEAD_W), lambda b, h: (b, 0, sec * HEADS + h))

    n_gk = N_GATES // HEADS
    in_specs = [lat(s) for s in sec_lat] + [cx(s) for s in sec_ctx] + [
        pl.BlockSpec((1, 1, T, n_gk), lambda b, h: (b, h, 0, 0)),
        pl.BlockSpec((1, 1, n_gk, T), lambda b, h: (b, h, 0, 0)),
        pl.BlockSpec((1, 1, Tc, n_gk), lambda b, h: (b, h, 0, 0)),
        pl.BlockSpec((1, 1, n_gk, Tc), lambda b, h: (b, h, 0, 0)),
        pl.BlockSpec((3, HEAD_W), lambda b, h: (0, h)),
        pl.BlockSpec((1, HEAD_W), lambda b, h: (0, h)),
        pl.BlockSpec((3, HEAD_W), lambda b, h: (0, HEADS + h)),
        pl.BlockSpec((1, HEAD_W), lambda b, h: (0, HEADS + h)),
    ]
    return pl.pallas_call(
        _mlstm_kernel,
        out_shape=jax.ShapeDtypeStruct((B, T, BRANCH_W), BF16),
        grid=(B, HEADS),
        in_specs=in_specs,
        out_specs=pl.BlockSpec((1, T, HEAD_W), lambda b, h: (b, 0, h)),
        scratch_shapes=[pltpu.VMEM((T + 16, HEAD_W), F32),
                        pltpu.VMEM((T, HEAD_W), BF16),
                        pltpu.VMEM((T, HEAD_W), BF16),
                        pltpu.VMEM((Tc, HEAD_W), BF16),
                        pltpu.VMEM((T, HEAD_W), F32),
                        pltpu.VMEM((HEAD_W, HEAD_W), F32),
                        pltpu.VMEM((1, HEAD_W), F32),
                        pltpu.VMEM((1, 1), F32)],
        compiler_params=_cparams("parallel", "parallel"),
        name="mlstm",
    )(p_lat, p_lat, p_lat, p_lat, p_ctx, p_ctx, g, gt, cg, cgt, conv_w, conv_b, conv_w, conv_b)


def _merge_kernel(alpha, r_ref, m_ref, gr_ref, gm_ref, x_ref, g1_ref, sh2_ref, sc2_ref,
                  lng_ref, lnb_ref, wr_ref, wm_ref, wo_ref, wrt_ref, brt_ref,
                  x1_ref, ua_ref, ub_ref, lt_ref):
    yr = jnp.dot(r_ref[0], wr_ref[...], preferred_element_type=F32)
    ym = jnp.dot(m_ref[0], wm_ref[...], preferred_element_type=F32)
    y = jax.nn.sigmoid(gr_ref[0].astype(F32)) * yr + jax.nn.sigmoid(gm_ref[0].astype(F32)) * ym
    yo = jnp.dot(y.astype(BF16), wo_ref[...], preferred_element_type=F32)
    x1 = _layer_norm(alpha * x_ref[0] + g1_ref[0] * yo) * lng_ref[...] + lnb_ref[...]
    x1_ref[0] = x1
    u2 = _layer_norm(x1) * (1.0 + sc2_ref[0]) + sh2_ref[0]
    ub16 = u2.astype(BF16)
    lt_ref[0] = lax.dot_general(wrt_ref[...], ub16, NT_DIMS, preferred_element_type=F32) + brt_ref[...]
    ua_ref[0] = _pack_pairs(u2[:, 0:PACK_W], u2[:, PACK_W:2 * PACK_W])
    ub_ref[0] = _pack_pairs(u2[:, 2 * PACK_W:3 * PACK_W], u2[:, 3 * PACK_W:4 * PACK_W])


def _merge(alpha, r, m, p_lat, sec_gates, x, g1, sh2, sc2, lng, lnb, wr, wm, wo, wrt, brt):
    B, T, D = x.shape
    tm = MERGE_TM

    def tile(w):
        return pl.BlockSpec((1, tm, w), lambda b, i: (b, i, 0))

    def sec(s):
        return pl.BlockSpec((1, tm, BRANCH_W), lambda b, i: (b, i, s))

    def mod():
        return pl.BlockSpec((1, 1, D), lambda b, i: (b, 0, 0))

    def const(shape):
        return pl.BlockSpec(shape, lambda b, i: (0,) * len(shape))

    return pl.pallas_call(
        functools.partial(_merge_kernel, alpha),
        out_shape=(jax.ShapeDtypeStruct((B, T, D), F32),
                   jax.ShapeDtypeStruct((B, T, PACK_W), U32),
                   jax.ShapeDtypeStruct((B, T, PACK_W), U32),
                   jax.ShapeDtypeStruct((B, ROUTE_ROWS, T), F32)),
        grid=(B, T // tm),
        in_specs=[tile(BRANCH_W), tile(BRANCH_W), sec(sec_gates[0]), sec(sec_gates[1]), tile(D),
                  mod(), mod(), mod(), const((1, D)), const((1, D)),
                  const((BRANCH_W, D)), const((BRANCH_W, D)), const((D, D)),
                  const((ROUTE_ROWS, D)), const((ROUTE_ROWS, 1))],
        out_specs=(tile(D), tile(PACK_W), tile(PACK_W),
                   pl.BlockSpec((1, ROUTE_ROWS, tm), lambda b, i: (b, 0, i))),
        compiler_params=_cparams("parallel", "parallel"),
        name="merge",
    )(r, m, p_lat, p_lat, x, g1, sh2, sc2, lng, lnb, wr, wm, wo, wrt, brt)


def _route_kernel(lt_ref, ri_ref, rw_ref, cnt_ref, carry_ref, u_ref):
    i = pl.program_id(0)
    tm = lt_ref.shape[2]

    @pl.when(i == 0)
    def _():
        carry_ref[...] = jnp.zeros_like(carry_ref)
        r = lax.broadcasted_iota(I32, (tm, tm), 0)
        c = lax.broadcasted_iota(I32, (tm, tm), 1)
        u_ref[...] = (r < c).astype(BF16)

    lt = lt_ref[0]
    lg = lt[0:N_GROUPS, :]
    eg = jnp.exp(lg - jnp.max(lg, axis=0, keepdims=True))
    pg = eg / jnp.sum(eg, axis=0, keepdims=True)
    pg_top = jnp.max(pg, axis=0, keepdims=True)
    rows_g = lax.broadcasted_iota(I32, pg.shape, 0)
    g_idx = jnp.min(jnp.where(pg == pg_top, rows_g, N_GROUPS), axis=0, keepdims=True)

    le = jnp.zeros((EXP_PER_GROUP, tm), F32)
    for g in range(N_GROUPS):
        lo = 8 + g * EXP_PER_GROUP
        le = jnp.where(g_idx == g, lt[lo:lo + EXP_PER_GROUP, :], le)
    ee = jnp.exp(le - jnp.max(le, axis=0, keepdims=True))
    pe = ee / jnp.sum(ee, axis=0, keepdims=True)
    rows_e = lax.broadcasted_iota(I32, pe.shape, 0)
    v1 = jnp.max(pe, axis=0, keepdims=True)
    i1 = jnp.min(jnp.where(pe == v1, rows_e, EXP_PER_GROUP), axis=0, keepdims=True)
    pe2 = jnp.where(rows_e == i1, -1.0, pe)
    v2 = jnp.max(pe2, axis=0, keepdims=True)
    i2 = jnp.min(jnp.where(pe2 == v2, rows_e, EXP_PER_GROUP), axis=0, keepdims=True)
    den = v1 + v2
    rw_ref[0:1, :] = pg_top * v1 / den
    rw_ref[1:2, :] = pg_top * v2 / den
    e1 = g_idx * EXP_PER_GROUP + i1
    e2 = g_idx * EXP_PER_GROUP + i2

    rows_x = lax.broadcasted_iota(I32, (N_EXPERTS, tm), 0)
    oh1 = (rows_x == e1).astype(F32)
    oh2 = (rows_x == e2).astype(F32)
    both = oh1 + oh2
    before = carry_ref[:, 0:1] + jnp.dot(both.astype(BF16), u_ref[...], preferred_element_type=F32)
    ri_ref[0:1, :] = e1
    ri_ref[1:2, :] = e2
    ri_ref[2:3, :] = jnp.sum(oh1 * before, axis=0, keepdims=True).astype(I32)
    ri_ref[3:4, :] = jnp.sum(oh2 * before, axis=0, keepdims=True).astype(I32)
    carry_ref[...] = carry_ref[...] + jnp.sum(both, axis=1, keepdims=True)
    cnt_ref[...] = carry_ref[...].astype(I32)


def _route(lt):
    B, _, T = lt.shape
    tm = ROUTE_TM
    per_b = T // tm
    n = B * T
    return pl.pallas_call(
        _route_kernel,
        out_shape=(jax.ShapeDtypeStruct((4, n), I32),
                   jax.ShapeDtypeStruct((2, n), F32),
                   jax.ShapeDtypeStruct((N_EXPERTS, 128), I32)),
        grid=(n // tm,),
        in_specs=[pl.BlockSpec((1, ROUTE_ROWS, tm), lambda i: (i // per_b, 0, i % per_b))],
        out_specs=(pl.BlockSpec((4, tm), lambda i: (0, i)),
                   pl.BlockSpec((2, tm), lambda i: (0, i)),
                   pl.BlockSpec((N_EXPERTS, 128), lambda i: (0, 0))),
        scratch_shapes=[pltpu.VMEM((N_EXPERTS, 128), F32), pltpu.VMEM((tm, tm), BF16)],
        compiler_params=_cparams("arbitrary"),
        name="route",
    )(lt)


def _sc_mesh():
    return plsc.VectorSubcoreMesh(core_axis_name="c", subcore_axis_name="s")


def _sc_scatter2(rows, idx0, idx1, n_out):
    m, w = rows.shape

    @functools.partial(pl.kernel, out_type=jax.ShapeDtypeStruct((n_out, w), rows.dtype),
                       mesh=_sc_mesh(), scratch_types=[])
    def k(x_hbm, i0_hbm, i1_hbm, o_hbm):
        def body(x_vmem, i0_vmem, i1_vmem):
            pltpu.sync_copy(x_vmem, o_hbm.at[i0_vmem.at[0]])
            pltpu.sync_copy(x_vmem, o_hbm.at[i1_vmem.at[0]])

        pltpu.emit_pipeline(
            body,
            grid=(m // SC_WIN,),
            in_specs=[pl.BlockSpec((SC_WIN, w), lambda i: (i, 0)),
                      pl.BlockSpec((1, SC_WIN), lambda i: (0, i)),
                      pl.BlockSpec((1, SC_WIN), lambda i: (0, i))],
            out_specs=[],
            core_axis_name=("c", "s"),
            dimension_semantics=(pltpu.PARALLEL,),
        )(x_hbm, i0_hbm, i1_hbm)

    return k(rows, idx0.reshape(1, m), idx1.reshape(1, m))


def _sc_gather(table, idx):
    m = idx.shape[0]
    w = table.shape[1]

    @functools.partial(pl.kernel, out_type=jax.ShapeDtypeStruct((m, w), table.dtype),
                       mesh=_sc_mesh(), scratch_types=[])
    def k(t_hbm, i_hbm, o_hbm):
        def body(i_vmem, o_vmem):
            pltpu.sync_copy(t_hbm.at[i_vmem.at[0]], o_vmem)

        pltpu.emit_pipeline(
            body,
            grid=(m // SC_WIN,),
            in_specs=[pl.BlockSpec((1, SC_WIN), lambda i: (0, i))],
            out_specs=[pl.BlockSpec((SC_WIN, w), lambda i: (i, 0))],
            core_axis_name=("c", "s"),
            dimension_semantics=(pltpu.PARALLEL,),
        )(i_hbm, o_hbm)

    return k(table, idx.reshape(1, m))


def _expert_kernel(be_ref, nv_ref, xa_ref, xb_ref, w1_ref, w3_ref, w2_ref, ya_ref, yb_ref):
    j = pl.program_id(0)
    nv = nv_ref[j]

    @pl.when(nv > 0)
    def _():
        valid = lax.broadcasted_iota(I32, xa_ref.shape, 0) < nv
        zero = jnp.zeros(xa_ref.shape, U32)
        parts = _unpack_pairs(jnp.where(valid, xa_ref[...], zero)) + \
            _unpack_pairs(jnp.where(valid, xb_ref[...], zero))
        h1 = None
        h3 = None
        for p, part in enumerate(parts):
            xb16 = part.astype(BF16)
            lo = p * PACK_W
            d1 = jnp.dot(xb16, w1_ref[0, lo:lo + PACK_W, :], preferred_element_type=F32)
            d3 = jnp.dot(xb16, w3_ref[0, lo:lo + PACK_W, :], preferred_element_type=F32)
            h1 = d1 if h1 is None else h1 + d1
            h3 = d3 if h3 is None else h3 + d3
        hh = (_silu(h1) * h3).astype(BF16)
        y = jnp.dot(hh, w2_ref[0], preferred_element_type=F32)
        ya_ref[...] = _pack_pairs(y[:, 0:PACK_W], y[:, PACK_W:2 * PACK_W])
        yb_ref[...] = _pack_pairs(y[:, 2 * PACK_W:3 * PACK_W], y[:, 3 * PACK_W:4 * PACK_W])

    @pl.when(nv == 0)
    def _():
        ya_ref[...] = jnp.zeros_like(ya_ref)
        yb_ref[...] = jnp.zeros_like(yb_ref)


def _experts(block_exp, n_valid, xa, xb, w1, w3, w2):
    n_slots = xa.shape[0]
    n_blocks = n_slots // MOE_BLK
    d, de = w1.shape[1], w1.shape[2]
    slot = pl.BlockSpec((MOE_BLK, PACK_W), lambda j, be, nv: (j, 0))
    grid_spec = pltpu.PrefetchScalarGridSpec(
        num_scalar_prefetch=2,
        grid=(n_blocks,),
        in_specs=[slot, slot,
                  pl.BlockSpec((1, d, de), lambda j, be, nv: (be[j], 0, 0)),
                  pl.BlockSpec((1, d, de), lambda j, be, nv: (be[j], 0, 0)),
                  pl.BlockSpec((1, de, d), lambda j, be, nv: (be[j], 0, 0))],
        out_specs=(slot, slot),
    )
    return pl.pallas_call(
        _expert_kernel,
        out_shape=(jax.ShapeDtypeStruct((n_slots, PACK_W), U32),
                   jax.ShapeDtypeStruct((n_slots, PACK_W), U32)),
        grid_spec=grid_spec,
        compiler_params=_cparams("parallel"),
        name="experts",
    )(block_exp, n_valid, xa, xb, w1, w3, w2)


def _final_kernel(alpha, x1_ref, a0_ref, b0_ref, a1_ref, b1_ref, w_ref, g2_ref, lng_ref, lnb_ref, o_ref):
    w = w_ref[...]
    w0 = w[:, 0:1]
    w1 = w[:, 1:2]
    parts0 = _unpack_pairs(a0_ref[...]) + _unpack_pairs(b0_ref[...])
    parts1 = _unpack_pairs(a1_ref[...]) + _unpack_pairs(b1_ref[...])
    f = jnp.concatenate([w0 * p0 + w1 * p1 for p0, p1 in zip(parts0, parts1)], axis=1)
    o_ref[0] = _layer_norm(alpha * x1_ref[0] + g2_ref[0] * f) * lng_ref[...] + lnb_ref[...]


def _final(alpha, x1, ya, yb, w, g2, lng, lnb):
    B, T, D = x1.shape
    tm = MERGE_TM
    per_b = T // tm
    n_tiles = B * per_b

    def rows(k):
        return pl.BlockSpec((tm, PACK_W), lambda b, i: (k * n_tiles + b * per_b + i, 0))

    return pl.pallas_call(
        functools.partial(_final_kernel, alpha),
        out_shape=jax.ShapeDtypeStruct((B, T, D), F32),
        grid=(B, per_b),
        in_specs=[pl.BlockSpec((1, tm, D), lambda b, i: (b, i, 0)),
                  rows(0), rows(0), rows(1), rows(1),
                  pl.BlockSpec((tm, 2), lambda b, i: (b * per_b + i, 0)),
                  pl.BlockSpec((1, 1, D), lambda b, i: (b, 0, 0)),
                  pl.BlockSpec((1, D), lambda b, i: (0, 0)),
                  pl.BlockSpec((1, D), lambda b, i: (0, 0))],
        out_specs=pl.BlockSpec((1, tm, D), lambda b, i: (b, i, 0)),
        compiler_params=_cparams("parallel", "parallel"),
        name="final",
    )(x1, ya, yb, ya, yb, w, g2, lng, lnb)


def _rotary_tables(T):
    quarter = HEAD_W // 4
    freqs = ROPE_BASE ** (-jnp.arange(quarter, dtype=F32) / quarter)
    t = jnp.arange(T)
    ang_r = (t // GRID_W).astype(F32)[:, None] * freqs[None, :]
    ang_c = (t % GRID_W).astype(F32)[:, None] * freqs[None, :]
    cos = jnp.concatenate([jnp.cos(ang_r)] * 2 + [jnp.cos(ang_c)] * 2, axis=1)
    sin = jnp.concatenate([-jnp.sin(ang_r), jnp.sin(ang_r), -jnp.sin(ang_c), jnp.sin(ang_c)], axis=1)
    return cos, sin


def _per_head_gates(gt):
    B, _, T = gt.shape
    gth = gt.reshape(B, N_GATES // HEADS, HEADS, T).transpose(0, 2, 1, 3)
    return gth.transpose(0, 1, 3, 2), gth


def kernel(x, c, ctx, c_ctx, w_ada, b_ada, w_in, b_mgate, ml_conv_w, ml_conv_b, ret_decay_logit, w_ret_branch, w_ml_branch, w_out, ln1_g, ln1_b, w_rg, b_rg, w_re, b_re, w_e1, w_e3, w_e2, ln2_g, ln2_b):
    B, T, D = x.shape
    depth = w_ada.shape[0]
    assert depth == 1 and D == BRANCH_W and T % PROJ_TM == 0 and T % GRID_W == 0
    alpha = (2 * depth) ** 0.25
    n_tok = B * T

    n_rows = -(-(B + 1) // 8) * 8
    cs = jnp.zeros((n_rows, D), F32).at[:B].set(c).at[B].set(c_ctx)
    mod = _ada(cs, w_ada[0], b_ada[0][None, :])
    sh1, sc1, g1, sh2, sc2, g2 = [mod[:B, None, i * D:(i + 1) * D] for i in range(6)]
    csh1 = jnp.broadcast_to(mod[B, 0 * D:1 * D], (B, 1, D))
    csc1 = jnp.broadcast_to(mod[B, 1 * D:2 * D], (B, 1, D))

    w = w_in[0]
    sec_w = [w[:, s * BRANCH_W:(s + 1) * BRANCH_W] for s in range(8)]
    g_lo = 8 * BRANCH_W
    w_gate_t = w[:, g_lo:g_lo + N_GATES].T.astype(BF16)
    b_gate = b_mgate[0][:, None]
    sec_w += [w[:, g_lo + N_GATES:g_lo + N_GATES + D], w[:, g_lo + N_GATES + D:]]
    w_lat = jnp.concatenate(sec_w, axis=1).astype(BF16)
    w_ctx = jnp.concatenate([sec_w[1], sec_w[2], sec_w[5], sec_w[6]], axis=1).astype(BF16)
    kinds_lat = ("rot", "rot_scale") + ("plain",) * 8
    kinds_ctx = ("scale", "plain", "plain", "plain")
    p_lat, gt_lat = _proj(x, sh1, sc1, w_lat, w_gate_t, b_gate, kinds_lat, _rotary_tables(T))
    p_ctx, gt_ctx = _proj(ctx, csh1, csc1, w_ctx, w_gate_t, b_gate, kinds_ctx)

    ret = _retention(ret_decay_logit[0], p_lat, p_ctx, (0, 1, 2, 3), (0, 1))
    g, gt = _per_head_gates(gt_lat)
    cg, cgt = _per_head_gates(gt_ctx)
    mls = _mlstm(p_lat, p_ctx, g, gt, cg, cgt, ml_conv_w[0], ml_conv_b[0][None, :],
                 (4, 5, 6, 7), (2, 3))

    wrt = jnp.zeros((ROUTE_ROWS, D), F32).at[:N_GROUPS].set(w_rg[0].T).at[8:8 + N_EXPERTS].set(w_re[0].T)
    brt = jnp.zeros((ROUTE_ROWS, 1), F32).at[:N_GROUPS, 0].set(b_rg[0]).at[8:8 + N_EXPERTS, 0].set(b_re[0])
    x1, ua, ub, lt = _merge(alpha, ret, mls, p_lat, (8, 9), x, g1, sh2, sc2,
                            ln1_g[0][None, :], ln1_b[0][None, :],
                            w_ret_branch[0].astype(BF16), w_ml_branch[0].astype(BF16),
                            w_out[0].astype(BF16), wrt.astype(BF16), brt)

    ri, rw, cnt = _route(lt)

    counts = cnt[:, 0]
    padded = (counts + MOE_BLK - 1) // MOE_BLK * MOE_BLK
    pad_end = jnp.cumsum(padded)
    pad_off = pad_end - padded
    dest = pad_off[ri[0:2]] + ri[2:4]
    n_blocks = (2 * n_tok) // MOE_BLK + N_EXPERTS
    n_slots = n_blocks * MOE_BLK
    block_start = jnp.arange(n_blocks, dtype=I32) * MOE_BLK
    block_exp = jnp.minimum((block_start[:, None] >= pad_end[None, :]).sum(1), N_EXPERTS - 1).astype(I32)
    n_valid = jnp.clip(counts[block_exp] - (block_start - pad_off[block_exp]), 0, MOE_BLK).astype(I32)

    xa = _sc_scatter2(ua.reshape(n_tok, PACK_W), dest[0], dest[1], n_slots)
    xb = _sc_scatter2(ub.reshape(n_tok, PACK_W), dest[0], dest[1], n_slots)
    ya, yb = _experts(block_exp, n_valid, xa, xb,
                      w_e1[0].astype(BF16), w_e3[0].astype(BF16), w_e2[0].astype(BF16))
    dflat = dest.reshape(2 * n_tok)
    ga = _sc_gather(ya, dflat)
    gb = _sc_gather(yb, dflat)
    return _final(alpha, x1, ga, gb, rw.T, g2, ln2_g[0][None, :], ln2_b[0][None, :])
```

```python
import functools

import jax
import jax.numpy as jnp
from jax import lax
from jax.experimental import pallas as pl
from jax.experimental.pallas import tpu as pltpu
from jax.experimental.pallas import tpu_sc as plsc

F32 = jnp.float32
BF16 = jnp.bfloat16
U32 = jnp.uint32
I32 = jnp.int32
HIGHEST = lax.Precision.HIGHEST

HEADS = 4
HEAD_W = 256
BRANCH_W = HEADS * HEAD_W
GRID_W = 64
ROPE_BASE = 10000.0
N_GATES = 16
N_GK = N_GATES // HEADS
N_GROUPS = 4
EXP_PER_GROUP = 8
N_EXPERTS = N_GROUPS * EXP_PER_GROUP
LN_EPS = 1e-5
NEG_INF = -1e30
KEY_SCALE = HEAD_W ** -0.5

SCAN_L = 256
PROJ_TM = 1024
MERGE_TM = 512
ROUTE_TM = 512
MOE_BLK = 256
SC_WIN = 128
PACK_W = 256
ROUTE_ROWS = 64
VMEM_LIMIT = 48 * 1024 * 1024

NT_DIMS = (((1,), (1,)), ((), ()))
TN_DIMS = (((0,), (0,)), ((), ()))


def _cparams(*sem):
    return pltpu.CompilerParams(dimension_semantics=sem, vmem_limit_bytes=VMEM_LIMIT)


def _layer_norm(x):
    mu = jnp.mean(x, axis=-1, keepdims=True)
    xc = x - mu
    var = jnp.mean(xc * xc, axis=-1, keepdims=True)
    return xc * lax.rsqrt(var + LN_EPS)


def _log_sigmoid(x):
    return jnp.minimum(x, 0.0) - jnp.log1p(jnp.exp(-jnp.abs(x)))


def _silu(x):
    return x * jax.nn.sigmoid(x)


def _pack_pairs(hi, lo):
    hb = lax.bitcast_convert_type(hi.astype(BF16).astype(F32), U32)
    lb = lax.bitcast_convert_type(lo.astype(BF16).astype(F32), U32)
    return (hb & jnp.uint32(0xFFFF0000)) | (lb >> 16)


def _unpack_pairs(p):
    hi = lax.bitcast_convert_type(p & jnp.uint32(0xFFFF0000), F32)
    lo = lax.bitcast_convert_type(p << 16, F32)
    return hi, lo


def _ada_kernel(c_ref, w_ref, b_ref, o_ref):
    s = _silu(c_ref[...])
    o_ref[...] = jnp.dot(s, w_ref[...], precision=HIGHEST, preferred_element_type=F32) + b_ref[...]


def _ada(cs, w, b):
    rows, d = cs.shape
    cols = w.shape[1]
    tn = 1024
    return pl.pallas_call(
        _ada_kernel,
        out_shape=jax.ShapeDtypeStruct((rows, cols), F32),
        grid=(cols // tn,),
        in_specs=[pl.BlockSpec((rows, d), lambda j: (0, 0)),
                  pl.BlockSpec((d, tn), lambda j: (0, j)),
                  pl.BlockSpec((1, tn), lambda j: (0, j))],
        out_specs=pl.BlockSpec((rows, tn), lambda j: (0, j)),
        compiler_params=_cparams("parallel"),
        name="ada",
    )(cs, w, b)


def _proj_kernel(kinds, x_ref, sh_ref, sc_ref, w_ref, wg_ref, bg_ref, *rest):
    if "rot" in kinds or "rot_scale" in kinds:
        cos_ref, sin_ref, o_ref, gt_ref, u_ref = rest
    else:
        o_ref, gt_ref, u_ref = rest
    j = pl.program_id(2)

    @pl.when(j == 0)
    def _():
        u = _layer_norm(x_ref[0]) * (1.0 + sc_ref[0]) + sh_ref[0]
        ub = u.astype(BF16)
        u_ref[...] = ub
        gt_ref[0] = lax.dot_general(wg_ref[...], ub, NT_DIMS, preferred_element_type=F32) + bg_ref[...]

    acc = jnp.dot(u_ref[...], w_ref[...], preferred_element_type=F32)

    def rotary(scale):
        for s in range(acc.shape[1] // 128):
            a = acc[:, s * 128:(s + 1) * 128]
            half = s % 2
            cs = cos_ref[:, half * 128:(half + 1) * 128]
            sn = sin_ref[:, half * 128:(half + 1) * 128]
            r = a * cs + pltpu.roll(a, 64, 1) * sn
            if scale != 1.0:
                r = r * scale
            o_ref[0, :, s * 128:(s + 1) * 128] = r.astype(BF16)

    by_kind = {}
    for s, kind in enumerate(kinds):
        by_kind.setdefault(kind, []).append(s)
    for kind, secs in by_kind.items():
        cond = functools.reduce(jnp.logical_or, [j == s for s in secs])

        @pl.when(cond)
        def _(kind=kind):
            if kind == "rot":
                rotary(1.0)
            elif kind == "rot_scale":
                rotary(KEY_SCALE)
            elif kind == "scale":
                o_ref[0] = (acc * KEY_SCALE).astype(BF16)
            else:
                o_ref[0] = acc.astype(BF16)


def _proj(x, sh, sc, w_main, w_gate_t, b_gate, kinds, tables=None):
    B, T, D = x.shape
    n_sec = len(kinds)
    tm = min(PROJ_TM, T)
    tn = BRANCH_W
    in_specs = [
        pl.BlockSpec((1, tm, D), lambda i, b, j: (b, i, 0)),
        pl.BlockSpec((1, 1, D), lambda i, b, j: (b, 0, 0)),
        pl.BlockSpec((1, 1, D), lambda i, b, j: (b, 0, 0)),
        pl.BlockSpec((D, tn), lambda i, b, j: (0, j)),
        pl.BlockSpec((N_GATES, D), lambda i, b, j: (0, 0)),
        pl.BlockSpec((N_GATES, 1), lambda i, b, j: (0, 0)),
    ]
    args = [x, sh, sc, w_main, w_gate_t, b_gate]
    if tables is not None:
        in_specs += [pl.BlockSpec((tm, HEAD_W), lambda i, b, j: (i, 0))] * 2
        args += list(tables)
    return pl.pallas_call(
        functools.partial(_proj_kernel, kinds),
        out_shape=(jax.ShapeDtypeStruct((B, T, n_sec * tn), BF16),
                   jax.ShapeDtypeStruct((B, N_GATES, T), F32)),
        grid=(T // tm, B, n_sec),
        in_specs=in_specs,
        out_specs=(pl.BlockSpec((1, tm, tn), lambda i, b, j: (b, i, j)),
                   pl.BlockSpec((1, N_GATES, tm), lambda i, b, j: (b, 0, i))),
        scratch_shapes=[pltpu.VMEM((tm, D), BF16)],
        compiler_params=_cparams("parallel", "parallel", "arbitrary"),
        name="proj_lat" if tables is not None else "proj_ctx",
    )(*args)


def _ret_kernel(dl_ref, q_ref, k_ref, v_ref, rg_ref, ck_ref, cv_ref, o_ref,
                sf_ref, sb_ref, fs_ref, bs_ref, dec_ref, d_ref):
    h = pl.program_id(1)
    L = SCAN_L
    n_chunks = q_ref.shape[1] // L
    n_ctx_chunks = ck_ref.shape[1] // L
    lgf = _log_sigmoid(jnp.full((1, 1), dl_ref[0, h], F32))
    lgb = _log_sigmoid(jnp.full((1, 1), dl_ref[1, h], F32))

    ri = lax.broadcasted_iota(I32, (L, L), 0)
    ci = lax.broadcasted_iota(I32, (L, L), 1)
    rel = (ri - ci).astype(F32)
    d_ref[...] = jnp.where(rel >= 0.0, jnp.exp(jnp.maximum(rel, 0.0) * lgf),
                           jnp.exp(jnp.maximum(-rel, 0.0) * lgb))
    row = lax.broadcasted_iota(I32, (L, HEAD_W), 0).astype(F32)
    dec_ref[0] = jnp.exp((row + 1.0) * lgf)
    dec_ref[1] = jnp.exp((L - 1.0 - row) * lgf)
    dec_ref[2] = jnp.exp((L - row) * lgb)
    dec_ref[3] = jnp.exp(row * lgb)
    cdf = jnp.exp(L * lgf)
    cdb = jnp.exp(L * lgb)

    def update(s_ref, kc, vc, kd, cd):
        kdec = (kc.astype(F32) * kd).astype(BF16)
        s_ref[...] = s_ref[...] * cd + lax.dot_general(kdec, vc, TN_DIMS, preferred_element_type=F32)

    sf_ref[...] = jnp.zeros_like(sf_ref)
    sb_ref[...] = jnp.zeros_like(sb_ref)
    for c in range(n_ctx_chunks):
        update(sf_ref, ck_ref[0, c * L:(c + 1) * L, :], cv_ref[0, c * L:(c + 1) * L, :], dec_ref[1], cdf)
    for c in reversed(range(n_ctx_chunks)):
        update(sb_ref, ck_ref[0, c * L:(c + 1) * L, :], cv_ref[0, c * L:(c + 1) * L, :], dec_ref[3], cdb)

    def state_pass(i, carry):
        cb = n_chunks - 1 - i
        rf = pl.multiple_of(i * L, L)
        rb = pl.multiple_of(cb * L, L)
        fs_ref[i] = sf_ref[...].astype(BF16)
        bs_ref[cb] = sb_ref[...].astype(BF16)
        update(sf_ref, k_ref[0, pl.ds(rf, L), :], v_ref[0, pl.ds(rf, L), :], dec_ref[1], cdf)
        update(sb_ref, k_ref[0, pl.ds(rb, L), :], v_ref[0, pl.ds(rb, L), :], dec_ref[3], cdb)
        return carry

    lax.fori_loop(0, n_chunks - 1, state_pass, 0)
    fs_ref[n_chunks - 1] = sf_ref[...].astype(BF16)
    bs_ref[0] = sb_ref[...].astype(BF16)

    def out_chunk(c):
        r0 = pl.multiple_of(c * L, L)
        q = q_ref[0, pl.ds(r0, L), :]
        k = k_ref[0, pl.ds(r0, L), :]
        v = v_ref[0, pl.ds(r0, L), :]
        s = lax.dot_general(q, k, NT_DIMS, preferred_element_type=F32)
        att = (s * d_ref[...]).astype(BF16)
        o = jnp.dot(att, v, preferred_element_type=F32)
        o = o + jnp.dot(q, fs_ref[c], preferred_element_type=F32) * dec_ref[0]
        o = o + jnp.dot(q, bs_ref[c], preferred_element_type=F32) * dec_ref[2]
        rg = rg_ref[0, pl.ds(r0, L), :].astype(F32)
        o_ref[0, pl.ds(r0, L), :] = (_layer_norm(o) * _silu(rg)).astype(BF16)

    def out_pass(i, carry):
        out_chunk(2 * i)
        out_chunk(2 * i + 1)
        return carry

    lax.fori_loop(0, n_chunks // 2, out_pass, 0)


def _retention(decay_logit, p_lat, p_ctx, sec_lat, sec_ctx):
    B, T, _ = p_lat.shape
    Tc = p_ctx.shape[1]
    assert T % (2 * SCAN_L) == 0 and Tc % SCAN_L == 0
    n_chunks = T // SCAN_L

    def lat(sec):
        return pl.BlockSpec((1, T, HEAD_W), lambda b, h: (b, 0, sec * HEADS + h))

    def cx(sec):
        return pl.BlockSpec((1, Tc, HEAD_W), lambda b, h: (b, 0, sec * HEADS + h))

    return pl.pallas_call(
        _ret_kernel,
        out_shape=jax.ShapeDtypeStruct((B, T, BRANCH_W), BF16),
        grid=(B, HEADS),
        in_specs=[pl.BlockSpec(memory_space=pltpu.SMEM)]
        + [lat(s) for s in sec_lat] + [cx(s) for s in sec_ctx],
        out_specs=pl.BlockSpec((1, T, HEAD_W), lambda b, h: (b, 0, h)),
        scratch_shapes=[pltpu.VMEM((HEAD_W, HEAD_W), F32),
                        pltpu.VMEM((HEAD_W, HEAD_W), F32),
                        pltpu.VMEM((n_chunks, HEAD_W, HEAD_W), BF16),
                        pltpu.VMEM((n_chunks, HEAD_W, HEAD_W), BF16),
                        pltpu.VMEM((4, SCAN_L, HEAD_W), F32),
                        pltpu.VMEM((SCAN_L, SCAN_L), F32)],
        compiler_params=_cparams("parallel", "parallel"),
        name="retention",
    )(decay_logit, p_lat, p_lat, p_lat, p_lat, p_ctx, p_ctx)


def _mlstm_kernel_old(q_ref, k_ref, v_ref, mo_ref, ck_ref, cv_ref, g_ref, gt_ref, cg_ref, cgt_ref,
                  wq_ref, bq_ref, wk_ref, bk_ref, o_ref,
                  xf_ref, qs_ref, ks_ref, cks_ref,
                  glc_ref, csc_ref, rcc_ref, glr_ref, csr_ref, rcr_ref,
                  cf_ref, nf_ref, mf_ref, cb_ref, nb_ref, mb_ref,
                  cfs_ref, nfs_ref, mfs_ref, cbs_ref, nbs_ref, mbs_ref):
    L = SCAN_L
    T = q_ref.shape[1]
    Tc = ck_ref.shape[1]
    n_chunks = T // L
    n_ctx_chunks = Tc // L
    CV = 128

    def conv_silu(src_ref, w_ref, b_ref, dst_ref, t_len, scale):
        xf_ref[pl.ds(0, 8), :] = jnp.zeros((8, HEAD_W), F32)
        xf_ref[pl.ds(8 + t_len, 8), :] = jnp.zeros((8, HEAD_W), F32)
        xf_ref[pl.ds(8, t_len), :] = src_ref[0].astype(F32)
        w = w_ref[...]
        b = b_ref[...]

        def body(c, carry):
            r0 = pl.multiple_of(c * CV, CV)
            win = xf_ref[pl.ds(r0, CV + 16), :]
            prev = pltpu.roll(win, 1, 0)[8:8 + CV, :]
            cur = win[8:8 + CV, :]
            nxt = pltpu.roll(win, CV + 15, 0)[8:8 + CV, :]
            y = prev * w[0:1, :] + cur * w[1:2, :] + nxt * w[2:3, :] + b
            y = _silu(y)
            if scale != 1.0:
                y = y * scale
            dst_ref[pl.ds(r0, CV), :] = y.astype(BF16)
            return carry

        lax.fori_loop(0, t_len // CV, body, 0)

    conv_silu(q_ref, wq_ref, bq_ref, qs_ref, T, 1.0)
    conv_silu(k_ref, wk_ref, bk_ref, ks_ref, T, KEY_SCALE)
    conv_silu(ck_ref, wk_ref, bk_ref, cks_ref, Tc, KEY_SCALE)

    ri = lax.broadcasted_iota(I32, (L, L), 0)
    ci = lax.broadcasted_iota(I32, (L, L), 1)
    tri_l = (ci <= ri).astype(F32)
    tri_u = (ci >= ri).astype(F32)

    def gate_tables(gc, gtc):
        lane = lax.broadcasted_iota(I32, gc.shape, 1)
        gl = jnp.where(lane % 2 == 1, _log_sigmoid(gc), gc)
        sub = lax.broadcasted_iota(I32, gtc.shape, 0)
        gtl = jnp.where(sub % 2 == 1, _log_sigmoid(gtc), gtc)
        cs_col = jnp.dot(tri_l, gl, precision=HIGHEST, preferred_element_type=F32)
        rc_col = jnp.dot(tri_u, gl, precision=HIGHEST, preferred_element_type=F32)
        cs_row = jnp.dot(gtl, tri_u, precision=HIGHEST, preferred_element_type=F32)
        rc_row = jnp.dot(gtl, tri_l, precision=HIGHEST, preferred_element_type=F32)
        return gl, cs_col, rc_col, gtl, cs_row, rc_row

    def pick(tables, backward):
        gl, cs_col, rc_col, gtl, cs_row, rc_row = tables
        if not backward:
            return gl[:, 0:1], gtl[0:1, :], cs_col[:, 1:2], cs_row[1:2, :], cs_row[1:2, L - 1:L]
        return gl[:, 2:3], gtl[2:3, :], rc_col[:, 3:4], rc_row[3:4, :], rc_row[3:4, 0:1]

    def lat_tables(r0):
        return (glc_ref[pl.ds(r0, L), :], csc_ref[pl.ds(r0, L), :], rcc_ref[pl.ds(r0, L), :],
                glr_ref[:, pl.ds(r0, L)], csr_ref[:, pl.ds(r0, L)], rcr_ref[:, pl.ds(r0, L)])

    def table_pass(c, carry):
        r0 = pl.multiple_of(c * L, L)
        gl, cs_col, rc_col, gtl, cs_row, rc_row = gate_tables(
            g_ref[0, 0, pl.ds(r0, L), :], gt_ref[0, 0, :, pl.ds(r0, L)])
        glc_ref[pl.ds(r0, L), :] = gl
        csc_ref[pl.ds(r0, L), :] = cs_col
        rcc_ref[pl.ds(r0, L), :] = rc_col
        glr_ref[:, pl.ds(r0, L)] = gtl
        csr_ref[:, pl.ds(r0, L)] = cs_row
        rcr_ref[:, pl.ds(r0, L)] = rc_row
        return carry

    lax.fori_loop(0, n_chunks, table_pass, 0)

    def advance(k, v, gate_vecs, c_ref, n_ref, m_ref):
        i_col, i_row, b_col, b_row, b_last = gate_vecs
        m = m_ref[...]
        g_col = b_last - b_col + i_col
        g_row = b_last - b_row + i_row
        m_new = jnp.maximum(b_last + m, jnp.max(g_row, axis=-1, keepdims=True))
        kw = k.astype(F32) * jnp.exp(g_col - m_new)
        decay = jnp.exp(b_last + m - m_new)
        c_ref[...] = decay * c_ref[...] + lax.dot_general(kw.astype(BF16), v, TN_DIMS,
                                                          preferred_element_type=F32)
        n_ref[...] = decay * n_ref[...] + jnp.sum(kw, axis=0, keepdims=True)
        m_ref[...] = m_new

    for refs in ((cf_ref, nf_ref, mf_ref), (cb_ref, nb_ref, mb_ref)):
        for r in refs:
            r[...] = jnp.zeros_like(r)
    ctx_tabs = [gate_tables(cg_ref[0, 0, c * L:(c + 1) * L, :], cgt_ref[0, 0, :, c * L:(c + 1) * L])
                for c in range(n_ctx_chunks)]
    for c in range(n_ctx_chunks):
        advance(cks_ref[c * L:(c + 1) * L, :], cv_ref[0, c * L:(c + 1) * L, :],
                pick(ctx_tabs[c], False), cf_ref, nf_ref, mf_ref)
    for c in reversed(range(n_ctx_chunks)):
        advance(cks_ref[c * L:(c + 1) * L, :], cv_ref[0, c * L:(c + 1) * L, :],
                pick(ctx_tabs[c], True), cb_ref, nb_ref, mb_ref)

    def snapshot(c, src, dst):
        dst[0][c] = src[0][...].astype(BF16)
        dst[1][c] = src[1][...]
        dst[2][c] = src[2][...]

    fwd_run, fwd_snap = (cf_ref, nf_ref, mf_ref), (cfs_ref, nfs_ref, mfs_ref)
    bwd_run, bwd_snap = (cb_ref, nb_ref, mb_ref), (cbs_ref, nbs_ref, mbs_ref)

    def state_pass(i, carry):
        cb = n_chunks - 1 - i
        rf = pl.multiple_of(i * L, L)
        rb = pl.multiple_of(cb * L, L)
        snapshot(i, fwd_run, fwd_snap)
        snapshot(cb, bwd_run, bwd_snap)
        advance(ks_ref[pl.ds(rf, L), :], v_ref[0, pl.ds(rf, L), :], pick(lat_tables(rf), False), *fwd_run)
        advance(ks_ref[pl.ds(rb, L), :], v_ref[0, pl.ds(rb, L), :], pick(lat_tables(rb), True), *bwd_run)
        return carry

    lax.fori_loop(0, n_chunks - 1, state_pass, 0)
    snapshot(n_chunks - 1, fwd_run, fwd_snap)
    snapshot(0, bwd_run, bwd_snap)

    def direction(q, v, s, gate_vecs, mask, c_in, n_in, m_in):
        _, i_row, b_col, b_row, _ = gate_vecs
        d = jnp.where(mask, b_col - b_row + i_row, NEG_INF)
        inter = b_col + m_in
        m_row = jnp.maximum(jnp.max(d, axis=-1, keepdims=True), inter)
        a = jnp.exp(inter - m_row)
        att = s * jnp.exp(d - m_row)
        num = jnp.dot(att.astype(BF16), v, preferred_element_type=F32)
        num = num + a * jnp.dot(q, c_in, preferred_element_type=F32)
        qn = jnp.sum(q.astype(F32) * n_in, axis=-1, keepdims=True)
        den = jnp.sum(att, axis=-1, keepdims=True) + a * qn
        return num * (1.0 / jnp.maximum(jnp.abs(den), jnp.exp(-m_row)))

    def out_pass(c, carry):
        r0 = pl.multiple_of(c * L, L)
        q = qs_ref[pl.ds(r0, L), :]
        k = ks_ref[pl.ds(r0, L), :]
        v = v_ref[0, pl.ds(r0, L), :]
        s = lax.dot_general(q, k, NT_DIMS, preferred_element_type=F32)
        tabs = lat_tables(r0)
        tot = direction(q, v, s, pick(tabs, False), ci <= ri, cfs_ref[c], nfs_ref[c], mfs_ref[c])
        tot = tot + direction(q, v, s, pick(tabs, True), ci > ri, cbs_ref[c], nbs_ref[c], mbs_ref[c])
        mo = mo_ref[0, pl.ds(r0, L), :].astype(F32)
        o_ref[0, pl.ds(r0, L), :] = (_layer_norm(tot) * jax.nn.sigmoid(mo)).astype(BF16)
        return carry

    lax.fori_loop(0, n_chunks, out_pass, 0)


def _mlstm_old(p_lat, p_ctx, g, gt, cg, cgt, conv_w, conv_b, sec_lat, sec_ctx):
    B, T, _ = p_lat.shape
    Tc = p_ctx.shape[1]
    assert T % SCAN_L == 0 and Tc % SCAN_L == 0
    n_chunks = T // SCAN_L

    def lat(sec):
        return pl.BlockSpec((1, T, HEAD_W), lambda b, h: (b, 0, sec * HEADS + h))

    def cx(sec):
        return pl.BlockSpec((1, Tc, HEAD_W), lambda b, h: (b, 0, sec * HEADS + h))

    in_specs = [lat(s) for s in sec_lat] + [cx(s) for s in sec_ctx] + [
        pl.BlockSpec((1, 1, T, N_GK), lambda b, h: (b, h, 0, 0)),
        pl.BlockSpec((1, 1, N_GK, T), lambda b, h: (b, h, 0, 0)),
        pl.BlockSpec((1, 1, Tc, N_GK), lambda b, h: (b, h, 0, 0)),
        pl.BlockSpec((1, 1, N_GK, Tc), lambda b, h: (b, h, 0, 0)),
        pl.BlockSpec((3, HEAD_W), lambda b, h: (0, h)),
        pl.BlockSpec((1, HEAD_W), lambda b, h: (0, h)),
        pl.BlockSpec((3, HEAD_W), lambda b, h: (0, HEADS + h)),
        pl.BlockSpec((1, HEAD_W), lambda b, h: (0, HEADS + h)),
    ]
    state = [pltpu.VMEM((HEAD_W, HEAD_W), F32), pltpu.VMEM((1, HEAD_W), F32), pltpu.VMEM((1, 1), F32)]
    snaps = [pltpu.VMEM((n_chunks, HEAD_W, HEAD_W), BF16), pltpu.VMEM((n_chunks, 1, HEAD_W), F32),
             pltpu.VMEM((n_chunks, 1, 1), F32)]
    return pl.pallas_call(
        _mlstm_kernel,
        out_shape=jax.ShapeDtypeStruct((B, T, BRANCH_W), BF16),
        grid=(B, HEADS),
        in_specs=in_specs,
        out_specs=pl.BlockSpec((1, T, HEAD_W), lambda b, h: (b, 0, h)),
        scratch_shapes=[pltpu.VMEM((T + 16, HEAD_W), F32),
                        pltpu.VMEM((T, HEAD_W), BF16),
                        pltpu.VMEM((T, HEAD_W), BF16),
                        pltpu.VMEM((Tc, HEAD_W), BF16)]
        + [pltpu.VMEM((T, N_GK), F32)] * 3 + [pltpu.VMEM((N_GK, T), F32)] * 3
        + state + state + snaps + snaps,
        compiler_params=_cparams("parallel", "parallel"),
        name="mlstm",
    )(p_lat, p_lat, p_lat, p_lat, p_ctx, p_ctx, g, gt, cg, cgt, conv_w, conv_b, conv_w, conv_b)


N_TAB = 6
AUG_W = HEAD_W + 128


def _split3(x):
    hi = x.astype(BF16).astype(F32)
    r1 = x - hi
    mid = r1.astype(BF16).astype(F32)
    lo = (r1 - mid).astype(BF16).astype(F32)
    return jnp.concatenate([hi, mid, lo], axis=0).astype(BF16)


def _mlstm_kernel(q_ref, k_ref, v_ref, mo_ref, ck_ref, cv_ref, gt_ref, cgt_ref,
                  wq_ref, bq_ref, wk_ref, bk_ref, o_ref,
                  xf_ref, qs_ref, ks_ref, cks_ref, tab_ref, row_ref,
                  cf_ref, mf_ref, cb_ref, mb_ref, cfs_ref, mfs_ref, cbs_ref, mbs_ref):
    L = SCAN_L
    T = q_ref.shape[1]
    Tc = ck_ref.shape[1]
    n_chunks = T // L
    n_ctx_chunks = Tc // L
    CV = 128

    def conv_silu(src_ref, w_ref, b_ref, dst_ref, t_len, scale):
        xf_ref[pl.ds(0, 8), :] = jnp.zeros((8, HEAD_W), F32)
        xf_ref[pl.ds(8 + t_len, 8), :] = jnp.zeros((8, HEAD_W), F32)
        xf_ref[pl.ds(8, t_len), :] = src_ref[0].astype(F32)
        w = w_ref[...]
        b = b_ref[...]

        def body(c, carry):
            r0 = pl.multiple_of(c * CV, CV)
            win = xf_ref[pl.ds(r0, CV + 16), :]
            prev = pltpu.roll(win, 1, 0)[8:8 + CV, :]
            cur = win[8:8 + CV, :]
            nxt = pltpu.roll(win, CV + 15, 0)[8:8 + CV, :]
            y = prev * w[0:1, :] + cur * w[1:2, :] + nxt * w[2:3, :] + b
            y = _silu(y)
            if scale != 1.0:
                y = y * scale
            dst_ref[pl.ds(r0, CV), :] = y.astype(BF16)
            return carry

        lax.fori_loop(0, t_len // CV, body, 0)

    conv_silu(q_ref, wq_ref, bq_ref, qs_ref, T, 1.0)
    conv_silu(k_ref, wk_ref, bk_ref, ks_ref, T, KEY_SCALE)
    conv_silu(ck_ref, wk_ref, bk_ref, cks_ref, Tc, KEY_SCALE)

    ri = lax.broadcasted_iota(I32, (L, L), 0)
    ci = lax.broadcasted_iota(I32, (L, L), 1)
    tri_u = (ri <= ci).astype(BF16)
    lane8 = lax.broadcasted_iota(I32, (8, L), 1)
    sub8 = lax.broadcasted_iota(I32, (8, L), 0)
    sel_r = lax.broadcasted_iota(I32, (24, 8 * 128), 0) % 8
    sel_c = lax.broadcasted_iota(I32, (24, 8 * 128), 1) // 128
    sel3 = (sel_r == sel_c).astype(BF16)
    ones_cols = jnp.ones((L, AUG_W - HEAD_W), BF16)

    def chunk_tables(g8):
        i_f, i_b = g8[0], g8[2]
        lf_f, lf_b = _log_sigmoid(g8[1]), _log_sigmoid(g8[3])
        cs3 = jnp.dot(_split3(jnp.concatenate([lf_f, lf_b], axis=0)), tri_u,
                      preferred_element_type=F32)
        cs = cs3[0:16] + cs3[16:32] + cs3[32:48]
        b_f = cs[0:8]
        b_b = cs[8:16, L - 1:L] - cs[8:16] + lf_b
        z_f = i_f - b_f
        z_b = i_b - b_b
        g_f = b_f[:, L - 1:L] - b_f + i_f
        g_b = b_b[:, 0:1] - b_b + i_b
        mf, mb = z_f, z_b
        s = 1
        while s < L:
            mf = jnp.maximum(mf, jnp.where(lane8 >= s, pltpu.roll(mf, s, 1), NEG_INF))
            mb = jnp.maximum(mb, jnp.where(lane8 < L - s, pltpu.roll(mb, L - s, 1), NEG_INF))
            s *= 2
        mb = jnp.where(lane8 < L - 1, pltpu.roll(mb, L - 1, 1), NEG_INF)
        reps = [lax.dot_general(_split3(t), sel3, TN_DIMS, preferred_element_type=F32)
                for t in (mf, b_f, g_f, mb, b_b, g_b)]

        def rows_of(c):
            out = jnp.zeros((8, L), F32)
            for r, val in enumerate((z_f, z_b, g_f, g_b, b_f, b_b)):
                out = jnp.where(sub8 == r, val[c:c + 1], out)
            return out

        return rows_of, reps

    lat_rows, lat_reps = chunk_tables(gt_ref[0, 0])
    for c in range(n_chunks):
        row_ref[c] = lat_rows(c)
        for t in range(N_TAB):
            tab_ref[t, c * L:(c + 1) * L, :] = lat_reps[t][:, c * 128:(c + 1) * 128]

    def lanes2(x):
        return jnp.concatenate([x, x], axis=1)

    def advance(k, v, g_rep, g_row, b_last, c_ref, m_ref):
        m = m_ref[...]
        m_new = jnp.maximum(b_last + m, jnp.max(g_row, axis=-1, keepdims=True))
        kw = (k.astype(F32) * jnp.exp(lanes2(g_rep) - m_new)).astype(BF16)
        v_aug = jnp.concatenate([v, ones_cols], axis=1)
        c_ref[...] = jnp.exp(b_last + m - m_new) * c_ref[...] + lax.dot_general(
            kw, v_aug, TN_DIMS, preferred_element_type=F32)
        m_ref[...] = m_new

    for r in (cf_ref, mf_ref, cb_ref, mb_ref):
        r[...] = jnp.zeros_like(r)
    ctx_rows, ctx_reps = chunk_tables(cgt_ref[0, 0])
    for c in range(n_ctx_chunks):
        rows = ctx_rows(c)
        advance(cks_ref[c * L:(c + 1) * L, :], cv_ref[0, c * L:(c + 1) * L, :],
                ctx_reps[2][:, c * 128:(c + 1) * 128], rows[2:3], rows[4:5, L - 1:L], cf_ref, mf_ref)
    for c in reversed(range(n_ctx_chunks)):
        rows = ctx_rows(c)
        advance(cks_ref[c * L:(c + 1) * L, :], cv_ref[0, c * L:(c + 1) * L, :],
                ctx_reps[5][:, c * 128:(c + 1) * 128], rows[3:4], rows[5:6, 0:1], cb_ref, mb_ref)

    def state_pass(i, carry):
        cb = n_chunks - 1 - i
        rf = pl.multiple_of(i * L, L)
        rb = pl.multiple_of(cb * L, L)
        cfs_ref[i] = cf_ref[...].astype(BF16)
        mfs_ref[i] = mf_ref[...]
        cbs_ref[cb] = cb_ref[...].astype(BF16)
        mbs_ref[cb] = mb_ref[...]
        rows_f = row_ref[i]
        rows_b = row_ref[cb]
        advance(ks_ref[pl.ds(rf, L), :], v_ref[0, pl.ds(rf, L), :], tab_ref[2, pl.ds(rf, L), :],
                rows_f[2:3], rows_f[4:5, L - 1:L], cf_ref, mf_ref)
        advance(ks_ref[pl.ds(rb, L), :], v_ref[0, pl.ds(rb, L), :], tab_ref[5, pl.ds(rb, L), :],
                rows_b[3:4], rows_b[5:6, 0:1], cb_ref, mb_ref)
        return carry

    lax.fori_loop(0, n_chunks - 1, state_pass, 0)
    cfs_ref[n_chunks - 1] = cf_ref[...].astype(BF16)
    mfs_ref[n_chunks - 1] = mf_ref[...]
    cbs_ref[0] = cb_ref[...].astype(BF16)
    mbs_ref[0] = mb_ref[...]

    def direction(q, v_aug, s, z_row, zmax_rep, b_rep, mask, c_in, m_in):
        mx = jnp.maximum(zmax_rep, m_in)
        att = s * jnp.exp(jnp.where(mask, z_row - lanes2(mx), NEG_INF))
        na = jnp.dot(att.astype(BF16), v_aug, preferred_element_type=F32)
        qa = jnp.dot(q, c_in, preferred_element_type=F32)
        a = jnp.exp(m_in - mx)
        num = na[:, 0:HEAD_W] + lanes2(a) * qa[:, 0:HEAD_W]
        den = na[:, HEAD_W:] + a * qa[:, HEAD_W:]
        scale = 1.0 / jnp.maximum(jnp.abs(den), jnp.exp(-(b_rep + mx)))
        return num * lanes2(scale)

    def out_pass(c, carry):
        r0 = pl.multiple_of(c * L, L)
        q = qs_ref[pl.ds(r0, L), :]
        k = ks_ref[pl.ds(r0, L), :]
        v_aug = jnp.concatenate([v_ref[0, pl.ds(r0, L), :], ones_cols], axis=1)
        s = lax.dot_general(q, k, NT_DIMS, preferred_element_type=F32)
        rows = row_ref[c]
        tot = direction(q, v_aug, s, rows[0:1], tab_ref[0, pl.ds(r0, L), :], tab_ref[1, pl.ds(r0, L), :],
                        ci <= ri, cfs_ref[c], mfs_ref[c])
        tot = tot + direction(q, v_aug, s, rows[1:2], tab_ref[3, pl.ds(r0, L), :],
                              tab_ref[4, pl.ds(r0, L), :], ci > ri, cbs_ref[c], mbs_ref[c])
        mo = mo_ref[0, pl.ds(r0, L), :].astype(F32)
        o_ref[0, pl.ds(r0, L), :] = (_layer_norm(tot) * jax.nn.sigmoid(mo)).astype(BF16)
        return carry

    lax.fori_loop(0, n_chunks, out_pass, 0)


def _mlstm(p_lat, p_ctx, gt, cgt, conv_w, conv_b, sec_lat, sec_ctx):
    B, T, _ = p_lat.shape
    Tc = p_ctx.shape[1]
    assert T % SCAN_L == 0 and Tc % SCAN_L == 0 and T // SCAN_L <= 8
    n_chunks = T // SCAN_L

    def lat(sec):
        return pl.BlockSpec((1, T, HEAD_W), lambda b, h: (b, 0, sec * HEADS + h))

    def cx(sec):
        return pl.BlockSpec((1, Tc, HEAD_W), lambda b, h: (b, 0, sec * HEADS + h))

    in_specs = [lat(s) for s in sec_lat] + [cx(s) for s in sec_ctx] + [
        pl.BlockSpec((1, 1, N_GK, 8, SCAN_L), lambda b, h: (b, h, 0, 0, 0)),
        pl.BlockSpec((1, 1, N_GK, 8, SCAN_L), lambda b, h: (b, h, 0, 0, 0)),
        pl.BlockSpec((3, HEAD_W), lambda b, h: (0, h)),
        pl.BlockSpec((1, HEAD_W), lambda b, h: (0, h)),
        pl.BlockSpec((3, HEAD_W), lambda b, h: (0, HEADS + h)),
        pl.BlockSpec((1, HEAD_W), lambda b, h: (0, HEADS + h)),
    ]
    state = [pltpu.VMEM((HEAD_W, AUG_W), F32), pltpu.VMEM((1, 1), F32)]
    snaps = [pltpu.VMEM((n_chunks, HEAD_W, AUG_W), BF16), pltpu.VMEM((n_chunks, 1, 1), F32)]
    return pl.pallas_call(
        _mlstm_kernel,
        out_shape=jax.ShapeDtypeStruct((B, T, BRANCH_W), BF16),
        grid=(B, HEADS),
        in_specs=in_specs,
        out_specs=pl.BlockSpec((1, T, HEAD_W), lambda b, h: (b, 0, h)),
        scratch_shapes=[pltpu.VMEM((T + 16, HEAD_W), F32),
                        pltpu.VMEM((T, HEAD_W), BF16),
                        pltpu.VMEM((T, HEAD_W), BF16),
                        pltpu.VMEM((Tc, HEAD_W), BF16),
                        pltpu.VMEM((N_TAB, T, 128), F32),
                        pltpu.VMEM((n_chunks, 8, SCAN_L), F32)]
        + state + state + snaps + snaps,
        compiler_params=_cparams("parallel", "parallel"),
        name="mlstm",
    )(p_lat, p_lat, p_lat, p_lat, p_ctx, p_ctx, gt, cgt, conv_w, conv_b, conv_w, conv_b)


def _merge_kernel(alpha, r_ref, m_ref, gr_ref, gm_ref, x_ref, g1_ref, sh2_ref, sc2_ref,
                  lng_ref, lnb_ref, wr_ref, wm_ref, wo_ref, wrt_ref, brt_ref,
                  x1_ref, ua_ref, ub_ref, lt_ref):
    yr = jnp.dot(r_ref[0], wr_ref[...], preferred_element_type=F32)
    ym = jnp.dot(m_ref[0], wm_ref[...], preferred_element_type=F32)
    y = jax.nn.sigmoid(gr_ref[0].astype(F32)) * yr + jax.nn.sigmoid(gm_ref[0].astype(F32)) * ym
    yo = jnp.dot(y.astype(BF16), wo_ref[...], preferred_element_type=F32)
    x1 = _layer_norm(alpha * x_ref[0] + g1_ref[0] * yo) * lng_ref[...] + lnb_ref[...]
    x1_ref[0] = x1
    u2 = _layer_norm(x1) * (1.0 + sc2_ref[0]) + sh2_ref[0]
    ub16 = u2.astype(BF16)
    lt_ref[0] = lax.dot_general(wrt_ref[...], ub16, NT_DIMS, preferred_element_type=F32) + brt_ref[...]
    ua_ref[0] = _pack_pairs(u2[:, 0:PACK_W], u2[:, PACK_W:2 * PACK_W])
    ub_ref[0] = _pack_pairs(u2[:, 2 * PACK_W:3 * PACK_W], u2[:, 3 * PACK_W:4 * PACK_W])


def _merge(alpha, r, m, p_lat, sec_gates, x, g1, sh2, sc2, lng, lnb, wr, wm, wo, wrt, brt):
    B, T, D = x.shape
    tm = MERGE_TM

    def tile(w):
        return pl.BlockSpec((1, tm, w), lambda b, i: (b, i, 0))

    def sec(s):
        return pl.BlockSpec((1, tm, BRANCH_W), lambda b, i: (b, i, s))

    def mod():
        return pl.BlockSpec((1, 1, D), lambda b, i: (b, 0, 0))

    def const(shape):
        return pl.BlockSpec(shape, lambda b, i: (0,) * len(shape))

    return pl.pallas_call(
        functools.partial(_merge_kernel, alpha),
        out_shape=(jax.ShapeDtypeStruct((B, T, D), F32),
                   jax.ShapeDtypeStruct((B, T, PACK_W), U32),
                   jax.ShapeDtypeStruct((B, T, PACK_W), U32),
                   jax.ShapeDtypeStruct((B, ROUTE_ROWS, T), F32)),
        grid=(B, T // tm),
        in_specs=[tile(BRANCH_W), tile(BRANCH_W), sec(sec_gates[0]), sec(sec_gates[1]), tile(D),
                  mod(), mod(), mod(), const((1, D)), const((1, D)),
                  const((BRANCH_W, D)), const((BRANCH_W, D)), const((D, D)),
                  const((ROUTE_ROWS, D)), const((ROUTE_ROWS, 1))],
        out_specs=(tile(D), tile(PACK_W), tile(PACK_W),
                   pl.BlockSpec((1, ROUTE_ROWS, tm), lambda b, i: (b, 0, i))),
        compiler_params=_cparams("parallel", "parallel"),
        name="merge",
    )(r, m, p_lat, p_lat, x, g1, sh2, sc2, lng, lnb, wr, wm, wo, wrt, brt)


def _route_kernel(lt_ref, ri_ref, rw_ref, cnt_ref, carry_ref, u_ref):
    i = pl.program_id(0)
    tm = lt_ref.shape[2]

    @pl.when(i == 0)
    def _():
        carry_ref[...] = jnp.zeros_like(carry_ref)
        r = lax.broadcasted_iota(I32, (tm, tm), 0)
        c = lax.broadcasted_iota(I32, (tm, tm), 1)
        u_ref[...] = (r < c).astype(BF16)

    lt = lt_ref[0]
    lg = lt[0:N_GROUPS, :]
    eg = jnp.exp(lg - jnp.max(lg, axis=0, keepdims=True))
    pg = eg / jnp.sum(eg, axis=0, keepdims=True)
    pg_top = jnp.max(pg, axis=0, keepdims=True)
    rows_g = lax.broadcasted_iota(I32, pg.shape, 0)
    g_idx = jnp.min(jnp.where(pg == pg_top, rows_g, N_GROUPS), axis=0, keepdims=True)

    le = jnp.zeros((EXP_PER_GROUP, tm), F32)
    for g in range(N_GROUPS):
        lo = 8 + g * EXP_PER_GROUP
        le = jnp.where(g_idx == g, lt[lo:lo + EXP_PER_GROUP, :], le)
    ee = jnp.exp(le - jnp.max(le, axis=0, keepdims=True))
    pe = ee / jnp.sum(ee, axis=0, keepdims=True)
    rows_e = lax.broadcasted_iota(I32, pe.shape, 0)
    v1 = jnp.max(pe, axis=0, keepdims=True)
    i1 = jnp.min(jnp.where(pe == v1, rows_e, EXP_PER_GROUP), axis=0, keepdims=True)
    pe2 = jnp.where(rows_e == i1, -1.0, pe)
    v2 = jnp.max(pe2, axis=0, keepdims=True)
    i2 = jnp.min(jnp.where(pe2 == v2, rows_e, EXP_PER_GROUP), axis=0, keepdims=True)
    den = v1 + v2
    rw_ref[0:1, :] = pg_top * v1 / den
    rw_ref[1:2, :] = pg_top * v2 / den
    e1 = g_idx * EXP_PER_GROUP + i1
    e2 = g_idx * EXP_PER_GROUP + i2

    rows_x = lax.broadcasted_iota(I32, (N_EXPERTS, tm), 0)
    oh1 = (rows_x == e1).astype(F32)
    oh2 = (rows_x == e2).astype(F32)
    both = oh1 + oh2
    before = carry_ref[:, 0:1] + jnp.dot(both.astype(BF16), u_ref[...], preferred_element_type=F32)
    ri_ref[0:1, :] = e1
    ri_ref[1:2, :] = e2
    ri_ref[2:3, :] = jnp.sum(oh1 * before, axis=0, keepdims=True).astype(I32)
    ri_ref[3:4, :] = jnp.sum(oh2 * before, axis=0, keepdims=True).astype(I32)
    carry_ref[...] = carry_ref[...] + jnp.sum(both, axis=1, keepdims=True)
    cnt_ref[...] = carry_ref[...].astype(I32)


def _route(lt):
    B, _, T = lt.shape
    tm = ROUTE_TM
    per_b = T // tm
    n = B * T
    return pl.pallas_call(
        _route_kernel,
        out_shape=(jax.ShapeDtypeStruct((4, n), I32),
                   jax.ShapeDtypeStruct((2, n), F32),
                   jax.ShapeDtypeStruct((N_EXPERTS, 128), I32)),
        grid=(n // tm,),
        in_specs=[pl.BlockSpec((1, ROUTE_ROWS, tm), lambda i: (i // per_b, 0, i % per_b))],
        out_specs=(pl.BlockSpec((4, tm), lambda i: (0, i)),
                   pl.BlockSpec((2, tm), lambda i: (0, i)),
                   pl.BlockSpec((N_EXPERTS, 128), lambda i: (0, 0))),
        scratch_shapes=[pltpu.VMEM((N_EXPERTS, 128), F32), pltpu.VMEM((tm, tm), BF16)],
        compiler_params=_cparams("arbitrary"),
        name="route",
    )(lt)


def _sc_mesh():
    return plsc.VectorSubcoreMesh(core_axis_name="c", subcore_axis_name="s")


def _sc_scatter2(rows, idx0, idx1, n_out):
    m, w = rows.shape

    @functools.partial(pl.kernel, out_type=jax.ShapeDtypeStruct((n_out, w), rows.dtype),
                       mesh=_sc_mesh(), scratch_types=[])
    def k(x_hbm, i0_hbm, i1_hbm, o_hbm):
        def body(x_vmem, i0_vmem, i1_vmem):
            pltpu.sync_copy(x_vmem, o_hbm.at[i0_vmem.at[0]])
            pltpu.sync_copy(x_vmem, o_hbm.at[i1_vmem.at[0]])

        pltpu.emit_pipeline(
            body,
            grid=(m // SC_WIN,),
            in_specs=[pl.BlockSpec((SC_WIN, w), lambda i: (i, 0)),
                      pl.BlockSpec((1, SC_WIN), lambda i: (0, i)),
                      pl.BlockSpec((1, SC_WIN), lambda i: (0, i))],
            out_specs=[],
            core_axis_name=("c", "s"),
            dimension_semantics=(pltpu.PARALLEL,),
        )(x_hbm, i0_hbm, i1_hbm)

    return k(rows, idx0.reshape(1, m), idx1.reshape(1, m))


def _sc_gather(table, idx):
    m = idx.shape[0]
    w = table.shape[1]

    @functools.partial(pl.kernel, out_type=jax.ShapeDtypeStruct((m, w), table.dtype),
                       mesh=_sc_mesh(), scratch_types=[])
    def k(t_hbm, i_hbm, o_hbm):
        def body(i_vmem, o_vmem):
            pltpu.sync_copy(t_hbm.at[i_vmem.at[0]], o_vmem)

        pltpu.emit_pipeline(
            body,
            grid=(m // SC_WIN,),
            in_specs=[pl.BlockSpec((1, SC_WIN), lambda i: (0, i))],
            out_specs=[pl.BlockSpec((SC_WIN, w), lambda i: (i, 0))],
            core_axis_name=("c", "s"),
            dimension_semantics=(pltpu.PARALLEL,),
        )(i_hbm, o_hbm)

    return k(table, idx.reshape(1, m))


def _expert_kernel(be_ref, nv_ref, xa_ref, xb_ref, w1_ref, w3_ref, w2_ref, ya_ref, yb_ref):
    j = pl.program_id(0)
    nv = nv_ref[j]

    @pl.when(nv > 0)
    def _():
        valid = lax.broadcasted_iota(I32, xa_ref.shape, 0) < nv
        zero = jnp.zeros(xa_ref.shape, U32)
        parts = _unpack_pairs(jnp.where(valid, xa_ref[...], zero)) + \
            _unpack_pairs(jnp.where(valid, xb_ref[...], zero))
        h1 = None
        h3 = None
        for p, part in enumerate(parts):
            xb16 = part.astype(BF16)
            lo = p * PACK_W
            d1 = jnp.dot(xb16, w1_ref[0, lo:lo + PACK_W, :], preferred_element_type=F32)
            d3 = jnp.dot(xb16, w3_ref[0, lo:lo + PACK_W, :], preferred_element_type=F32)
            h1 = d1 if h1 is None else h1 + d1
            h3 = d3 if h3 is None else h3 + d3
        hh = (_silu(h1) * h3).astype(BF16)
        y = jnp.dot(hh, w2_ref[0], preferred_element_type=F32)
        ya_ref[...] = _pack_pairs(y[:, 0:PACK_W], y[:, PACK_W:2 * PACK_W])
        yb_ref[...] = _pack_pairs(y[:, 2 * PACK_W:3 * PACK_W], y[:, 3 * PACK_W:4 * PACK_W])

    @pl.when(nv == 0)
    def _():
        ya_ref[...] = jnp.zeros_like(ya_ref)
        yb_ref[...] = jnp.zeros_like(yb_ref)


def _experts(block_exp, n_valid, xa, xb, w1, w3, w2):
    n_slots = xa.shape[0]
    n_blocks = n_slots // MOE_BLK
    d, de = w1.shape[1], w1.shape[2]
    slot = pl.BlockSpec((MOE_BLK, PACK_W), lambda j, be, nv: (j, 0))
    grid_spec = pltpu.PrefetchScalarGridSpec(
        num_scalar_prefetch=2,
        grid=(n_blocks,),
        in_specs=[slot, slot,
                  pl.BlockSpec((1, d, de), lambda j, be, nv: (be[j], 0, 0)),
                  pl.BlockSpec((1, d, de), lambda j, be, nv: (be[j], 0, 0)),
                  pl.BlockSpec((1, de, d), lambda j, be, nv: (be[j], 0, 0))],
        out_specs=(slot, slot),
    )
    return pl.pallas_call(
        _expert_kernel,
        out_shape=(jax.ShapeDtypeStruct((n_slots, PACK_W), U32),
                   jax.ShapeDtypeStruct((n_slots, PACK_W), U32)),
        grid_spec=grid_spec,
        compiler_params=_cparams("parallel"),
        name="experts",
    )(block_exp, n_valid, xa, xb, w1, w3, w2)


def _final_kernel(alpha, x1_ref, a0_ref, b0_ref, a1_ref, b1_ref, w_ref, g2_ref, lng_ref, lnb_ref, o_ref):
    w = w_ref[...]
    w0 = w[:, 0:1]
    w1 = w[:, 1:2]
    parts0 = _unpack_pairs(a0_ref[...]) + _unpack_pairs(b0_ref[...])
    parts1 = _unpack_pairs(a1_ref[...]) + _unpack_pairs(b1_ref[...])
    f = jnp.concatenate([w0 * p0 + w1 * p1 for p0, p1 in zip(parts0, parts1)], axis=1)
    o_ref[0] = _layer_norm(alpha * x1_ref[0] + g2_ref[0] * f) * lng_ref[...] + lnb_ref[...]


def _final(alpha, x1, ya, yb, w, g2, lng, lnb):
    B, T, D = x1.shape
    tm = MERGE_TM
    per_b = T // tm
    n_tiles = B * per_b

    def rows(k):
        return pl.BlockSpec((tm, PACK_W), lambda b, i: (k * n_tiles + b * per_b + i, 0))

    return pl.pallas_call(
        functools.partial(_final_kernel, alpha),
        out_shape=jax.ShapeDtypeStruct((B, T, D), F32),
        grid=(B, per_b),
        in_specs=[pl.BlockSpec((1, tm, D), lambda b, i: (b, i, 0)),
                  rows(0), rows(0), rows(1), rows(1),
                  pl.BlockSpec((tm, 2), lambda b, i: (b * per_b + i, 0)),
                  pl.BlockSpec((1, 1, D), lambda b, i: (b, 0, 0)),
                  pl.BlockSpec((1, D), lambda b, i: (0, 0)),
                  pl.BlockSpec((1, D), lambda b, i: (0, 0))],
        out_specs=pl.BlockSpec((1, tm, D), lambda b, i: (b, i, 0)),
        compiler_params=_cparams("parallel", "parallel"),
        name="final",
    )(x1, ya, yb, ya, yb, w, g2, lng, lnb)


def _rotary_tables(T):
    quarter = HEAD_W // 4
    freqs = ROPE_BASE ** (-jnp.arange(quarter, dtype=F32) / quarter)
    t = jnp.arange(T)
    ang_r = (t // GRID_W).astype(F32)[:, None] * freqs[None, :]
    ang_c = (t % GRID_W).astype(F32)[:, None] * freqs[None, :]
    cos = jnp.concatenate([jnp.cos(ang_r)] * 2 + [jnp.cos(ang_c)] * 2, axis=1)
    sin = jnp.concatenate([-jnp.sin(ang_r), jnp.sin(ang_r), -jnp.sin(ang_c), jnp.sin(ang_c)], axis=1)
    return cos, sin


def _per_head_gates(gt):
    B, _, T = gt.shape
    n_chunks = T // SCAN_L
    gth = gt.reshape(B, N_GK, HEADS, n_chunks, SCAN_L).transpose(0, 2, 1, 3, 4)
    return jnp.pad(gth, ((0, 0), (0, 0), (0, 0), (0, 8 - n_chunks), (0, 0)))


def _table_lookup(table, idx):
    sel = idx[..., None] == jnp.arange(table.shape[0], dtype=idx.dtype)
    return jnp.sum(jnp.where(sel, table, 0), axis=-1)


def kernel(x, c, ctx, c_ctx, w_ada, b_ada, w_in, b_mgate, ml_conv_w, ml_conv_b, ret_decay_logit, w_ret_branch, w_ml_branch, w_out, ln1_g, ln1_b, w_rg, b_rg, w_re, b_re, w_e1, w_e3, w_e2, ln2_g, ln2_b):
    B, T, D = x.shape
    depth = w_ada.shape[0]
    assert depth == 1 and D == BRANCH_W and T % PROJ_TM == 0 and T % GRID_W == 0
    alpha = (2 * depth) ** 0.25
    n_tok = B * T

    n_rows = -(-(B + 1) // 8) * 8
    cs = jnp.zeros((n_rows, D), F32).at[:B].set(c).at[B].set(c_ctx)
    mod = _ada(cs, w_ada[0], b_ada[0][None, :])
    sh1, sc1, g1, sh2, sc2, g2 = [mod[:B, None, i * D:(i + 1) * D] for i in range(6)]
    csh1 = jnp.broadcast_to(mod[B, 0 * D:1 * D], (B, 1, D))
    csc1 = jnp.broadcast_to(mod[B, 1 * D:2 * D], (B, 1, D))

    w = w_in[0]
    sec_w = [w[:, s * BRANCH_W:(s + 1) * BRANCH_W] for s in range(8)]
    g_lo = 8 * BRANCH_W
    w_gate_t = w[:, g_lo:g_lo + N_GATES].T.astype(BF16)
    b_gate = b_mgate[0][:, None]
    sec_w += [w[:, g_lo + N_GATES:g_lo + N_GATES + D], w[:, g_lo + N_GATES + D:]]
    w_lat = jnp.concatenate(sec_w, axis=1).astype(BF16)
    w_ctx = jnp.concatenate([sec_w[1], sec_w[2], sec_w[5], sec_w[6]], axis=1).astype(BF16)
    kinds_lat = ("rot", "rot_scale") + ("plain",) * 8
    kinds_ctx = ("scale", "plain", "plain", "plain")
    p_lat, gt_lat = _proj(x, sh1, sc1, w_lat, w_gate_t, b_gate, kinds_lat, _rotary_tables(T))
    p_ctx, gt_ctx = _proj(ctx, csh1, csc1, w_ctx, w_gate_t, b_gate, kinds_ctx)

    ret = _retention(ret_decay_logit[0], p_lat, p_ctx, (0, 1, 2, 3), (0, 1))
    mls = _mlstm(p_lat, p_ctx, _per_head_gates(gt_lat), _per_head_gates(gt_ctx),
                 ml_conv_w[0], ml_conv_b[0][None, :], (4, 5, 6, 7), (2, 3))

    wrt = jnp.zeros((ROUTE_ROWS, D), F32).at[:N_GROUPS].set(w_rg[0].T).at[8:8 + N_EXPERTS].set(w_re[0].T)
    brt = jnp.zeros((ROUTE_ROWS, 1), F32).at[:N_GROUPS, 0].set(b_rg[0]).at[8:8 + N_EXPERTS, 0].set(b_re[0])
    x1, ua, ub, lt = _merge(alpha, ret, mls, p_lat, (8, 9), x, g1, sh2, sc2,
                            ln1_g[0][None, :], ln1_b[0][None, :],
                            w_ret_branch[0].astype(BF16), w_ml_branch[0].astype(BF16),
                            w_out[0].astype(BF16), wrt.astype(BF16), brt)

    ri, rw, cnt = _route(lt)

    counts = cnt[:, 0]
    padded = (counts + MOE_BLK - 1) // MOE_BLK * MOE_BLK
    pad_end = jnp.cumsum(padded)
    pad_off = pad_end - padded
    dest = _table_lookup(pad_off, ri[0:2]) + ri[2:4]
    n_blocks = (2 * n_tok) // MOE_BLK + N_EXPERTS
    n_slots = n_blocks * MOE_BLK
    block_start = jnp.arange(n_blocks, dtype=I32) * MOE_BLK
    block_exp = jnp.minimum((block_start[:, None] >= pad_end[None, :]).sum(1), N_EXPERTS - 1).astype(I32)
    n_valid = jnp.clip(_table_lookup(counts, block_exp) - (block_start - _table_lookup(pad_off, block_exp)),
                       0, MOE_BLK).astype(I32)

    xa = _sc_scatter2(ua.reshape(n_tok, PACK_W), dest[0], dest[1], n_slots)
    xb = _sc_scatter2(ub.reshape(n_tok, PACK_W), dest[0], dest[1], n_slots)
    ya, yb = _experts(block_exp, n_valid, xa, xb,
                      w_e1[0].astype(BF16), w_e3[0].astype(BF16), w_e2[0].astype(BF16))
    dflat = dest.reshape(2 * n_tok)
    ga = _sc_gather(ya, dflat)
    gb = _sc_gather(yb, dflat)
    return _final(alpha, x1, ga, gb, rw.T, g2, ln2_g[0][None, :], ln2_b[0][None, :])
```

```python
import functools

import jax
import jax.numpy as jnp
from jax import lax
from jax.experimental import pallas as pl
from jax.experimental.pallas import tpu as pltpu
from jax.experimental.pallas import tpu_sc as plsc

F32 = jnp.float32
BF16 = jnp.bfloat16
U32 = jnp.uint32
I32 = jnp.int32
HIGHEST = lax.Precision.HIGHEST

HEADS = 4
HEAD_W = 256
BRANCH_W = HEADS * HEAD_W
GRID_W = 64
ROPE_BASE = 10000.0
N_GATES = 16
N_GK = N_GATES // HEADS
N_GROUPS = 4
EXP_PER_GROUP = 8
N_EXPERTS = N_GROUPS * EXP_PER_GROUP
LN_EPS = 1e-5
NEG_INF = -1e30
KEY_SCALE = HEAD_W ** -0.5

SCAN_L = 256
PROJ_TM = 1024
PROJ_SUB = 256
MERGE_TM = 512
MERGE_SUB = 512
MOE_SUB = 256
ROUTE_TM = 512
MOE_BLK = 256
SC_WIN = 128
PACK_W = 256
ROUTE_ROWS = 64
VMEM_LIMIT = 48 * 1024 * 1024

NT_DIMS = (((1,), (1,)), ((), ()))
TN_DIMS = (((0,), (0,)), ((), ()))


def _cparams(*sem):
    return pltpu.CompilerParams(dimension_semantics=sem, vmem_limit_bytes=VMEM_LIMIT)


def _layer_norm(x):
    mu = jnp.mean(x, axis=-1, keepdims=True)
    xc = x - mu
    var = jnp.mean(xc * xc, axis=-1, keepdims=True)
    return xc * lax.rsqrt(var + LN_EPS)


def _log_sigmoid(x):
    return jnp.minimum(x, 0.0) - jnp.log1p(jnp.exp(-jnp.abs(x)))


def _silu(x):
    return x * jax.nn.sigmoid(x)


def _pack_pairs(hi, lo):
    hb = lax.bitcast_convert_type(hi.astype(BF16).astype(F32), U32)
    lb = lax.bitcast_convert_type(lo.astype(BF16).astype(F32), U32)
    return (hb & jnp.uint32(0xFFFF0000)) | (lb >> 16)


def _unpack_pairs(p):
    hi = lax.bitcast_convert_type(p & jnp.uint32(0xFFFF0000), F32)
    lo = lax.bitcast_convert_type(p << 16, F32)
    return hi, lo


def _ada_kernel(c_ref, w_ref, b_ref, o_ref):
    s = _silu(c_ref[...])
    o_ref[...] = jnp.dot(s, w_ref[...], precision=HIGHEST, preferred_element_type=F32) + b_ref[...]


def _ada(cs, w, b):
    rows, d = cs.shape
    cols = w.shape[1]
    tn = 1024
    return pl.pallas_call(
        _ada_kernel,
        out_shape=jax.ShapeDtypeStruct((rows, cols), F32),
        grid=(cols // tn,),
        in_specs=[pl.BlockSpec((rows, d), lambda j: (0, 0)),
                  pl.BlockSpec((d, tn), lambda j: (0, j)),
                  pl.BlockSpec((1, tn), lambda j: (0, j))],
        out_specs=pl.BlockSpec((rows, tn), lambda j: (0, j)),
        compiler_params=_cparams("parallel"),
        name="ada",
    )(cs, w, b)


def _proj_kernel(kinds, x_ref, sh_ref, sc_ref, w_ref, wg_ref, bg_ref, *rest):
    if "rot" in kinds or "rot_scale" in kinds:
        cos_ref, sin_ref, o_ref, gt_ref, u_ref = rest
    else:
        o_ref, gt_ref, u_ref = rest
    j = pl.program_id(2)
    tm = x_ref.shape[1]
    sub = min(PROJ_SUB, tm)

    def rotary(acc, rows, scale):
        for s in range(acc.shape[1] // 128):
            a = acc[:, s * 128:(s + 1) * 128]
            half = s % 2
            cs = cos_ref[rows, half * 128:(half + 1) * 128]
            sn = sin_ref[rows, half * 128:(half + 1) * 128]
            r = a * cs + pltpu.roll(a, 64, 1) * sn
            if scale != 1.0:
                r = r * scale
            o_ref[0, rows, s * 128:(s + 1) * 128] = r.astype(BF16)

    def section(kind, first):
        for r in range(tm // sub):
            rows = slice(r * sub, (r + 1) * sub)
            if first:
                u = _layer_norm(x_ref[0, rows, :]) * (1.0 + sc_ref[0]) + sh_ref[0]
                ub = u.astype(BF16)
                u_ref[rows, :] = ub
                gt_ref[0, :, rows] = lax.dot_general(wg_ref[...], ub, NT_DIMS,
                                                     preferred_element_type=F32) + bg_ref[...]
            else:
                ub = u_ref[rows, :]
            acc = jnp.dot(ub, w_ref[...], preferred_element_type=F32)
            if kind == "rot":
                rotary(acc, rows, 1.0)
            elif kind == "rot_scale":
                rotary(acc, rows, KEY_SCALE)
            elif kind == "scale":
                o_ref[0, rows, :] = (acc * KEY_SCALE).astype(BF16)
            else:
                o_ref[0, rows, :] = acc.astype(BF16)

    variants = {}
    for s, kind in enumerate(kinds):
        variants.setdefault((kind, s == 0), []).append(s)
    for (kind, first), secs in variants.items():
        cond = functools.reduce(jnp.logical_or, [j == s for s in secs])

        @pl.when(cond)
        def _(kind=kind, first=first):
            section(kind, first)


def _proj(x, sh, sc, w_main, w_gate_t, b_gate, kinds, tables=None):
    B, T, D = x.shape
    n_sec = len(kinds)
    tm = min(PROJ_TM, T)
    tn = BRANCH_W
    in_specs = [
        pl.BlockSpec((1, tm, D), lambda i, b, j: (b, i, 0)),
        pl.BlockSpec((1, 1, D), lambda i, b, j: (b, 0, 0)),
        pl.BlockSpec((1, 1, D), lambda i, b, j: (b, 0, 0)),
        pl.BlockSpec((D, tn), lambda i, b, j: (0, j)),
        pl.BlockSpec((N_GATES, D), lambda i, b, j: (0, 0)),
        pl.BlockSpec((N_GATES, 1), lambda i, b, j: (0, 0)),
    ]
    args = [x, sh, sc, w_main, w_gate_t, b_gate]
    if tables is not None:
        in_specs += [pl.BlockSpec((tm, HEAD_W), lambda i, b, j: (i, 0))] * 2
        args += list(tables)
    return pl.pallas_call(
        functools.partial(_proj_kernel, kinds),
        out_shape=(jax.ShapeDtypeStruct((B, T, n_sec * tn), BF16),
                   jax.ShapeDtypeStruct((B, N_GATES, T), F32)),
        grid=(T // tm, B, n_sec),
        in_specs=in_specs,
        out_specs=(pl.BlockSpec((1, tm, tn), lambda i, b, j: (b, i, j)),
                   pl.BlockSpec((1, N_GATES, tm), lambda i, b, j: (b, 0, i))),
        scratch_shapes=[pltpu.VMEM((tm, D), BF16)],
        compiler_params=_cparams("parallel", "parallel", "arbitrary"),
        name="proj_lat" if tables is not None else "proj_ctx",
    )(*args)


def _ret_kernel(dl_ref, q_ref, k_ref, v_ref, rg_ref, ck_ref, cv_ref, o_ref,
                sf_ref, sb_ref, fs_ref, bs_ref, dec_ref, d_ref):
    h = pl.program_id(1)
    L = SCAN_L
    n_chunks = q_ref.shape[1] // L
    n_ctx_chunks = ck_ref.shape[1] // L
    lgf = _log_sigmoid(jnp.full((1, 1), dl_ref[0, h], F32))
    lgb = _log_sigmoid(jnp.full((1, 1), dl_ref[1, h], F32))

    ri = lax.broadcasted_iota(I32, (L, L), 0)
    ci = lax.broadcasted_iota(I32, (L, L), 1)
    rel = (ri - ci).astype(F32)
    d_ref[...] = jnp.where(rel >= 0.0, jnp.exp(jnp.maximum(rel, 0.0) * lgf),
                           jnp.exp(jnp.maximum(-rel, 0.0) * lgb))
    row = lax.broadcasted_iota(I32, (L, HEAD_W), 0).astype(F32)
    dec_ref[0] = jnp.exp((row + 1.0) * lgf)
    dec_ref[1] = jnp.exp((L - 1.0 - row) * lgf)
    dec_ref[2] = jnp.exp((L - row) * lgb)
    dec_ref[3] = jnp.exp(row * lgb)
    cdf = jnp.exp(L * lgf)
    cdb = jnp.exp(L * lgb)

    def update(s_ref, kc, vc, kd, cd):
        kdec = (kc.astype(F32) * kd).astype(BF16)
        s_ref[...] = s_ref[...] * cd + lax.dot_general(kdec, vc, TN_DIMS, preferred_element_type=F32)

    sf_ref[...] = jnp.zeros_like(sf_ref)
    sb_ref[...] = jnp.zeros_like(sb_ref)
    for c in range(n_ctx_chunks):
        update(sf_ref, ck_ref[0, c * L:(c + 1) * L, :], cv_ref[0, c * L:(c + 1) * L, :], dec_ref[1], cdf)
    for c in reversed(range(n_ctx_chunks)):
        update(sb_ref, ck_ref[0, c * L:(c + 1) * L, :], cv_ref[0, c * L:(c + 1) * L, :], dec_ref[3], cdb)

    def state_pass(i, carry):
        cb = n_chunks - 1 - i
        rf = pl.multiple_of(i * L, L)
        rb = pl.multiple_of(cb * L, L)
        fs_ref[i] = sf_ref[...].astype(BF16)
        bs_ref[cb] = sb_ref[...].astype(BF16)
        update(sf_ref, k_ref[0, pl.ds(rf, L), :], v_ref[0, pl.ds(rf, L), :], dec_ref[1], cdf)
        update(sb_ref, k_ref[0, pl.ds(rb, L), :], v_ref[0, pl.ds(rb, L), :], dec_ref[3], cdb)
        return carry

    lax.fori_loop(0, n_chunks - 1, state_pass, 0)
    fs_ref[n_chunks - 1] = sf_ref[...].astype(BF16)
    bs_ref[0] = sb_ref[...].astype(BF16)

    def out_chunk(c):
        r0 = pl.multiple_of(c * L, L)
        q = q_ref[0, pl.ds(r0, L), :]
        k = k_ref[0, pl.ds(r0, L), :]
        v = v_ref[0, pl.ds(r0, L), :]
        s = lax.dot_general(q, k, NT_DIMS, preferred_element_type=F32)
        att = (s * d_ref[...]).astype(BF16)
        o = jnp.dot(att, v, preferred_element_type=F32)
        o = o + jnp.dot(q, fs_ref[c], preferred_element_type=F32) * dec_ref[0]
        o = o + jnp.dot(q, bs_ref[c], preferred_element_type=F32) * dec_ref[2]
        rg = rg_ref[0, pl.ds(r0, L), :].astype(F32)
        o_ref[0, pl.ds(r0, L), :] = (_layer_norm(o) * _silu(rg)).astype(BF16)

    def out_pass(i, carry):
        out_chunk(2 * i)
        out_chunk(2 * i + 1)
        return carry

    lax.fori_loop(0, n_chunks // 2, out_pass, 0)


def _retention(decay_logit, p_lat, p_ctx, sec_lat, sec_ctx):
    B, T, _ = p_lat.shape
    Tc = p_ctx.shape[1]
    assert T % (2 * SCAN_L) == 0 and Tc % SCAN_L == 0
    n_chunks = T // SCAN_L

    def lat(sec):
        return pl.BlockSpec((1, T, HEAD_W), lambda b, h: (b, 0, sec * HEADS + h))

    def cx(sec):
        return pl.BlockSpec((1, Tc, HEAD_W), lambda b, h: (b, 0, sec * HEADS + h))

    return pl.pallas_call(
        _ret_kernel,
        out_shape=jax.ShapeDtypeStruct((B, T, BRANCH_W), BF16),
        grid=(B, HEADS),
        in_specs=[pl.BlockSpec(memory_space=pltpu.SMEM)]
        + [lat(s) for s in sec_lat] + [cx(s) for s in sec_ctx],
        out_specs=pl.BlockSpec((1, T, HEAD_W), lambda b, h: (b, 0, h)),
        scratch_shapes=[pltpu.VMEM((HEAD_W, HEAD_W), F32),
                        pltpu.VMEM((HEAD_W, HEAD_W), F32),
                        pltpu.VMEM((n_chunks, HEAD_W, HEAD_W), BF16),
                        pltpu.VMEM((n_chunks, HEAD_W, HEAD_W), BF16),
                        pltpu.VMEM((4, SCAN_L, HEAD_W), F32),
                        pltpu.VMEM((SCAN_L, SCAN_L), F32)],
        compiler_params=_cparams("parallel", "parallel"),
        name="retention",
    )(decay_logit, p_lat, p_lat, p_lat, p_lat, p_ctx, p_ctx)


def _mlstm_kernel_old(q_ref, k_ref, v_ref, mo_ref, ck_ref, cv_ref, g_ref, gt_ref, cg_ref, cgt_ref,
                  wq_ref, bq_ref, wk_ref, bk_ref, o_ref,
                  xf_ref, qs_ref, ks_ref, cks_ref,
                  glc_ref, csc_ref, rcc_ref, glr_ref, csr_ref, rcr_ref,
                  cf_ref, nf_ref, mf_ref, cb_ref, nb_ref, mb_ref,
                  cfs_ref, nfs_ref, mfs_ref, cbs_ref, nbs_ref, mbs_ref):
    L = SCAN_L
    T = q_ref.shape[1]
    Tc = ck_ref.shape[1]
    n_chunks = T // L
    n_ctx_chunks = Tc // L
    CV = 128

    def conv_silu(src_ref, w_ref, b_ref, dst_ref, t_len, scale):
        xf_ref[pl.ds(0, 8), :] = jnp.zeros((8, HEAD_W), F32)
        xf_ref[pl.ds(8 + t_len, 8), :] = jnp.zeros((8, HEAD_W), F32)
        xf_ref[pl.ds(8, t_len), :] = src_ref[0].astype(F32)
        w = w_ref[...]
        b = b_ref[...]

        def body(c, carry):
            r0 = pl.multiple_of(c * CV, CV)
            win = xf_ref[pl.ds(r0, CV + 16), :]
            prev = pltpu.roll(win, 1, 0)[8:8 + CV, :]
            cur = win[8:8 + CV, :]
            nxt = pltpu.roll(win, CV + 15, 0)[8:8 + CV, :]
            y = prev * w[0:1, :] + cur * w[1:2, :] + nxt * w[2:3, :] + b
            y = _silu(y)
            if scale != 1.0:
                y = y * scale
            dst_ref[pl.ds(r0, CV), :] = y.astype(BF16)
            return carry

        lax.fori_loop(0, t_len // CV, body, 0)

    conv_silu(q_ref, wq_ref, bq_ref, qs_ref, T, 1.0)
    conv_silu(k_ref, wk_ref, bk_ref, ks_ref, T, KEY_SCALE)
    conv_silu(ck_ref, wk_ref, bk_ref, cks_ref, Tc, KEY_SCALE)

    ri = lax.broadcasted_iota(I32, (L, L), 0)
    ci = lax.broadcasted_iota(I32, (L, L), 1)
    tri_l = (ci <= ri).astype(F32)
    tri_u = (ci >= ri).astype(F32)

    def gate_tables(gc, gtc):
        lane = lax.broadcasted_iota(I32, gc.shape, 1)
        gl = jnp.where(lane % 2 == 1, _log_sigmoid(gc), gc)
        sub = lax.broadcasted_iota(I32, gtc.shape, 0)
        gtl = jnp.where(sub % 2 == 1, _log_sigmoid(gtc), gtc)
        cs_col = jnp.dot(tri_l, gl, precision=HIGHEST, preferred_element_type=F32)
        rc_col = jnp.dot(tri_u, gl, precision=HIGHEST, preferred_element_type=F32)
        cs_row = jnp.dot(gtl, tri_u, precision=HIGHEST, preferred_element_type=F32)
        rc_row = jnp.dot(gtl, tri_l, precision=HIGHEST, preferred_element_type=F32)
        return gl, cs_col, rc_col, gtl, cs_row, rc_row

    def pick(tables, backward):
        gl, cs_col, rc_col, gtl, cs_row, rc_row = tables
        if not backward:
            return gl[:, 0:1], gtl[0:1, :], cs_col[:, 1:2], cs_row[1:2, :], cs_row[1:2, L - 1:L]
        return gl[:, 2:3], gtl[2:3, :], rc_col[:, 3:4], rc_row[3:4, :], rc_row[3:4, 0:1]

    def lat_tables(r0):
        return (glc_ref[pl.ds(r0, L), :], csc_ref[pl.ds(r0, L), :], rcc_ref[pl.ds(r0, L), :],
                glr_ref[:, pl.ds(r0, L)], csr_ref[:, pl.ds(r0, L)], rcr_ref[:, pl.ds(r0, L)])

    def table_pass(c, carry):
        r0 = pl.multiple_of(c * L, L)
        gl, cs_col, rc_col, gtl, cs_row, rc_row = gate_tables(
            g_ref[0, 0, pl.ds(r0, L), :], gt_ref[0, 0, :, pl.ds(r0, L)])
        glc_ref[pl.ds(r0, L), :] = gl
        csc_ref[pl.ds(r0, L), :] = cs_col
        rcc_ref[pl.ds(r0, L), :] = rc_col
        glr_ref[:, pl.ds(r0, L)] = gtl
        csr_ref[:, pl.ds(r0, L)] = cs_row
        rcr_ref[:, pl.ds(r0, L)] = rc_row
        return carry

    lax.fori_loop(0, n_chunks, table_pass, 0)

    def advance(k, v, gate_vecs, c_ref, n_ref, m_ref):
        i_col, i_row, b_col, b_row, b_last = gate_vecs
        m = m_ref[...]
        g_col = b_last - b_col + i_col
        g_row = b_last - b_row + i_row
        m_new = jnp.maximum(b_last + m, jnp.max(g_row, axis=-1, keepdims=True))
        kw = k.astype(F32) * jnp.exp(g_col - m_new)
        decay = jnp.exp(b_last + m - m_new)
        c_ref[...] = decay * c_ref[...] + lax.dot_general(kw.astype(BF16), v, TN_DIMS,
                                                          preferred_element_type=F32)
        n_ref[...] = decay * n_ref[...] + jnp.sum(kw, axis=0, keepdims=True)
        m_ref[...] = m_new

    for refs in ((cf_ref, nf_ref, mf_ref), (cb_ref, nb_ref, mb_ref)):
        for r in refs:
            r[...] = jnp.zeros_like(r)
    ctx_tabs = [gate_tables(cg_ref[0, 0, c * L:(c + 1) * L, :], cgt_ref[0, 0, :, c * L:(c + 1) * L])
                for c in range(n_ctx_chunks)]
    for c in range(n_ctx_chunks):
        advance(cks_ref[c * L:(c + 1) * L, :], cv_ref[0, c * L:(c + 1) * L, :],
                pick(ctx_tabs[c], False), cf_ref, nf_ref, mf_ref)
    for c in reversed(range(n_ctx_chunks)):
        advance(cks_ref[c * L:(c + 1) * L, :], cv_ref[0, c * L:(c + 1) * L, :],
                pick(ctx_tabs[c], True), cb_ref, nb_ref, mb_ref)

    def snapshot(c, src, dst):
        dst[0][c] = src[0][...].astype(BF16)
        dst[1][c] = src[1][...]
        dst[2][c] = src[2][...]

    fwd_run, fwd_snap = (cf_ref, nf_ref, mf_ref), (cfs_ref, nfs_ref, mfs_ref)
    bwd_run, bwd_snap = (cb_ref, nb_ref, mb_ref), (cbs_ref, nbs_ref, mbs_ref)

    def state_pass(i, carry):
        cb = n_chunks - 1 - i
        rf = pl.multiple_of(i * L, L)
        rb = pl.multiple_of(cb * L, L)
        snapshot(i, fwd_run, fwd_snap)
        snapshot(cb, bwd_run, bwd_snap)
        advance(ks_ref[pl.ds(rf, L), :], v_ref[0, pl.ds(rf, L), :], pick(lat_tables(rf), False), *fwd_run)
        advance(ks_ref[pl.ds(rb, L), :], v_ref[0, pl.ds(rb, L), :], pick(lat_tables(rb), True), *bwd_run)
        return carry

    lax.fori_loop(0, n_chunks - 1, state_pass, 0)
    snapshot(n_chunks - 1, fwd_run, fwd_snap)
    snapshot(0, bwd_run, bwd_snap)

    def direction(q, v, s, gate_vecs, mask, c_in, n_in, m_in):
        _, i_row, b_col, b_row, _ = gate_vecs
        d = jnp.where(mask, b_col - b_row + i_row, NEG_INF)
        inter = b_col + m_in
        m_row = jnp.maximum(jnp.max(d, axis=-1, keepdims=True), inter)
        a = jnp.exp(inter - m_row)
        att = s * jnp.exp(d - m_row)
        num = jnp.dot(att.astype(BF16), v, preferred_element_type=F32)
        num = num + a * jnp.dot(q, c_in, preferred_element_type=F32)
        qn = jnp.sum(q.astype(F32) * n_in, axis=-1, keepdims=True)
        den = jnp.sum(att, axis=-1, keepdims=True) + a * qn
        return num * (1.0 / jnp.maximum(jnp.abs(den), jnp.exp(-m_row)))

    def out_pass(c, carry):
        r0 = pl.multiple_of(c * L, L)
        q = qs_ref[pl.ds(r0, L), :]
        k = ks_ref[pl.ds(r0, L), :]
        v = v_ref[0, pl.ds(r0, L), :]
        s = lax.dot_general(q, k, NT_DIMS, preferred_element_type=F32)
        tabs = lat_tables(r0)
        tot = direction(q, v, s, pick(tabs, False), ci <= ri, cfs_ref[c], nfs_ref[c], mfs_ref[c])
        tot = tot + direction(q, v, s, pick(tabs, True), ci > ri, cbs_ref[c], nbs_ref[c], mbs_ref[c])
        mo = mo_ref[0, pl.ds(r0, L), :].astype(F32)
        o_ref[0, pl.ds(r0, L), :] = (_layer_norm(tot) * jax.nn.sigmoid(mo)).astype(BF16)
        return carry

    lax.fori_loop(0, n_chunks, out_pass, 0)


def _mlstm_old(p_lat, p_ctx, g, gt, cg, cgt, conv_w, conv_b, sec_lat, sec_ctx):
    B, T, _ = p_lat.shape
    Tc = p_ctx.shape[1]
    assert T % SCAN_L == 0 and Tc % SCAN_L == 0
    n_chunks = T // SCAN_L

    def lat(sec):
        return pl.BlockSpec((1, T, HEAD_W), lambda b, h: (b, 0, sec * HEADS + h))

    def cx(sec):
        return pl.BlockSpec((1, Tc, HEAD_W), lambda b, h: (b, 0, sec * HEADS + h))

    in_specs = [lat(s) for s in sec_lat] + [cx(s) for s in sec_ctx] + [
        pl.BlockSpec((1, 1, T, N_GK), lambda b, h: (b, h, 0, 0)),
        pl.BlockSpec((1, 1, N_GK, T), lambda b, h: (b, h, 0, 0)),
        pl.BlockSpec((1, 1, Tc, N_GK), lambda b, h: (b, h, 0, 0)),
        pl.BlockSpec((1, 1, N_GK, Tc), lambda b, h: (b, h, 0, 0)),
        pl.BlockSpec((3, HEAD_W), lambda b, h: (0, h)),
        pl.BlockSpec((1, HEAD_W), lambda b, h: (0, h)),
        pl.BlockSpec((3, HEAD_W), lambda b, h: (0, HEADS + h)),
        pl.BlockSpec((1, HEAD_W), lambda b, h: (0, HEADS + h)),
    ]
    state = [pltpu.VMEM((HEAD_W, HEAD_W), F32), pltpu.VMEM((1, HEAD_W), F32), pltpu.VMEM((1, 1), F32)]
    snaps = [pltpu.VMEM((n_chunks, HEAD_W, HEAD_W), BF16), pltpu.VMEM((n_chunks, 1, HEAD_W), F32),
             pltpu.VMEM((n_chunks, 1, 1), F32)]
    return pl.pallas_call(
        _mlstm_kernel,
        out_shape=jax.ShapeDtypeStruct((B, T, BRANCH_W), BF16),
        grid=(B, HEADS),
        in_specs=in_specs,
        out_specs=pl.BlockSpec((1, T, HEAD_W), lambda b, h: (b, 0, h)),
        scratch_shapes=[pltpu.VMEM((T + 16, HEAD_W), F32),
                        pltpu.VMEM((T, HEAD_W), BF16),
                        pltpu.VMEM((T, HEAD_W), BF16),
                        pltpu.VMEM((Tc, HEAD_W), BF16)]
        + [pltpu.VMEM((T, N_GK), F32)] * 3 + [pltpu.VMEM((N_GK, T), F32)] * 3
        + state + state + snaps + snaps,
        compiler_params=_cparams("parallel", "parallel"),
        name="mlstm",
    )(p_lat, p_lat, p_lat, p_lat, p_ctx, p_ctx, g, gt, cg, cgt, conv_w, conv_b, conv_w, conv_b)


N_TAB = 6
AUG_W = HEAD_W + 128


def _split3(x):
    hi = x.astype(BF16).astype(F32)
    r1 = x - hi
    mid = r1.astype(BF16).astype(F32)
    lo = (r1 - mid).astype(BF16).astype(F32)
    return jnp.concatenate([hi, mid, lo], axis=0).astype(BF16)


def _mlstm_kernel(q_ref, k_ref, v_ref, mo_ref, ck_ref, cv_ref, gt_ref, cgt_ref,
                  wq_ref, bq_ref, wk_ref, bk_ref, o_ref,
                  xf_ref, qs_ref, ks_ref, cks_ref, tab_ref, row_ref,
                  cf_ref, mf_ref, cb_ref, mb_ref, cfs_ref, mfs_ref, cbs_ref, mbs_ref, mask_ref):
    L = SCAN_L
    T = q_ref.shape[1]
    Tc = ck_ref.shape[1]
    n_chunks = T // L
    n_ctx_chunks = Tc // L
    CV = 128

    def conv_silu(src_ref, w_ref, b_ref, dst_ref, t_len, scale):
        xf_ref[pl.ds(0, 8), :] = jnp.zeros((8, HEAD_W), F32)
        xf_ref[pl.ds(8 + t_len, 8), :] = jnp.zeros((8, HEAD_W), F32)
        xf_ref[pl.ds(8, t_len), :] = src_ref[0].astype(F32)
        w = w_ref[...]
        b = b_ref[...]

        def body(c, carry):
            r0 = pl.multiple_of(c * CV, CV)
            win = xf_ref[pl.ds(r0, CV + 16), :]
            prev = pltpu.roll(win, 1, 0)[8:8 + CV, :]
            cur = win[8:8 + CV, :]
            nxt = pltpu.roll(win, CV + 15, 0)[8:8 + CV, :]
            y = prev * w[0:1, :] + cur * w[1:2, :] + nxt * w[2:3, :] + b
            y = _silu(y)
            if scale != 1.0:
                y = y * scale
            dst_ref[pl.ds(r0, CV), :] = y.astype(BF16)
            return carry

        lax.fori_loop(0, t_len // CV, body, 0)

    conv_silu(q_ref, wq_ref, bq_ref, qs_ref, T, 1.0)
    conv_silu(k_ref, wk_ref, bk_ref, ks_ref, T, KEY_SCALE)
    conv_silu(ck_ref, wk_ref, bk_ref, cks_ref, Tc, KEY_SCALE)

    ri = lax.broadcasted_iota(I32, (L, L), 0)
    ci = lax.broadcasted_iota(I32, (L, L), 1)
    tri_u = (ri <= ci).astype(BF16)
    lane8 = lax.broadcasted_iota(I32, (8, L), 1)
    sub8 = lax.broadcasted_iota(I32, (8, L), 0)
    sel_r = lax.broadcasted_iota(I32, (24, 8 * 128), 0) % 8
    sel_c = lax.broadcasted_iota(I32, (24, 8 * 128), 1) // 128
    sel3 = (sel_r == sel_c).astype(BF16)
    ones_cols = jnp.ones((L, AUG_W - HEAD_W), BF16)

    def chunk_tables(g8):
        i_f, i_b = g8[0], g8[2]
        lf_f, lf_b = _log_sigmoid(g8[1]), _log_sigmoid(g8[3])
        cs3 = jnp.dot(_split3(jnp.concatenate([lf_f, lf_b], axis=0)), tri_u,
                      preferred_element_type=F32)
        cs = cs3[0:16] + cs3[16:32] + cs3[32:48]
        b_f = cs[0:8]
        b_b = cs[8:16, L - 1:L] - cs[8:16] + lf_b
        z_f = i_f - b_f
        z_b = i_b - b_b
        g_f = b_f[:, L - 1:L] - b_f + i_f
        g_b = b_b[:, 0:1] - b_b + i_b
        mf, mb = z_f, z_b
        s = 1
        while s < L:
            mf = jnp.maximum(mf, jnp.where(lane8 >= s, pltpu.roll(mf, s, 1), NEG_INF))
            mb = jnp.maximum(mb, jnp.where(lane8 < L - s, pltpu.roll(mb, L - s, 1), NEG_INF))
            s *= 2
        mb = jnp.where(lane8 < L - 1, pltpu.roll(mb, L - 1, 1), NEG_INF)
        reps = [lax.dot_general(_split3(t), sel3, TN_DIMS, preferred_element_type=F32)
                for t in (mf, b_f, g_f, mb, b_b, g_b)]

        def rows_of(c):
            out = jnp.zeros((8, L), F32)
            for r, val in enumerate((z_f, z_b, g_f, g_b, b_f, b_b)):
                out = jnp.where(sub8 == r, val[c:c + 1], out)
            return out

        return rows_of, reps

    lat_rows, lat_reps = chunk_tables(gt_ref[0, 0])
    for c in range(n_chunks):
        row_ref[c] = lat_rows(c)
        for t in range(N_TAB):
            tab_ref[t, c * L:(c + 1) * L, :] = lat_reps[t][:, c * 128:(c + 1) * 128]

    def lanes2(x):
        return jnp.concatenate([x, x], axis=1)

    def advance(k, v, g_rep, g_row, b_last, c_ref, m_ref):
        m = m_ref[...]
        m_new = jnp.maximum(b_last + m, jnp.max(g_row, axis=-1, keepdims=True))
        kw = (k.astype(F32) * jnp.exp(lanes2(g_rep) - m_new)).astype(BF16)
        v_aug = jnp.concatenate([v, ones_cols], axis=1)
        c_ref[...] = jnp.exp(b_last + m - m_new) * c_ref[...] + lax.dot_general(
            kw, v_aug, TN_DIMS, preferred_element_type=F32)
        m_ref[...] = m_new

    for r in (cf_ref, mf_ref, cb_ref, mb_ref):
        r[...] = jnp.zeros_like(r)
    ctx_rows, ctx_reps = chunk_tables(cgt_ref[0, 0])
    for c in range(n_ctx_chunks):
        rows = ctx_rows(c)
        advance(cks_ref[c * L:(c + 1) * L, :], cv_ref[0, c * L:(c + 1) * L, :],
                ctx_reps[2][:, c * 128:(c + 1) * 128], rows[2:3], rows[4:5, L - 1:L], cf_ref, mf_ref)
    for c in reversed(range(n_ctx_chunks)):
        rows = ctx_rows(c)
        advance(cks_ref[c * L:(c + 1) * L, :], cv_ref[0, c * L:(c + 1) * L, :],
                ctx_reps[5][:, c * 128:(c + 1) * 128], rows[3:4], rows[5:6, 0:1], cb_ref, mb_ref)

    def state_pass(i, carry):
        cb = n_chunks - 1 - i
        rf = pl.multiple_of(i * L, L)
        rb = pl.multiple_of(cb * L, L)
        cfs_ref[i] = cf_ref[...].astype(BF16)
        mfs_ref[i] = mf_ref[...]
        cbs_ref[cb] = cb_ref[...].astype(BF16)
        mbs_ref[cb] = mb_ref[...]
        rows_f = row_ref[i]
        rows_b = row_ref[cb]
        advance(ks_ref[pl.ds(rf, L), :], v_ref[0, pl.ds(rf, L), :], tab_ref[2, pl.ds(rf, L), :],
                rows_f[2:3], rows_f[4:5, L - 1:L], cf_ref, mf_ref)
        advance(ks_ref[pl.ds(rb, L), :], v_ref[0, pl.ds(rb, L), :], tab_ref[5, pl.ds(rb, L), :],
                rows_b[3:4], rows_b[5:6, 0:1], cb_ref, mb_ref)
        return carry

    lax.fori_loop(0, n_chunks - 1, state_pass, 0)
    cfs_ref[n_chunks - 1] = cf_ref[...].astype(BF16)
    mfs_ref[n_chunks - 1] = mf_ref[...]
    cbs_ref[0] = cb_ref[...].astype(BF16)
    mbs_ref[0] = mb_ref[...]

    def direction(q, v_aug, s, z_row, zmax_rep, b_rep, mask, c_in, m_in):
        mx = jnp.maximum(zmax_rep, m_in)
        att = s * jnp.exp((z_row - lanes2(mx)) + mask)
        na = jnp.dot(att.astype(BF16), v_aug, preferred_element_type=F32)
        qa = jnp.dot(q, c_in, preferred_element_type=F32)
        a = jnp.exp(m_in - mx)
        num = na[:, 0:HEAD_W] + lanes2(a) * qa[:, 0:HEAD_W]
        den = na[:, HEAD_W:] + a * qa[:, HEAD_W:]
        scale = 1.0 / jnp.maximum(jnp.abs(den), jnp.exp(-(b_rep + mx)))
        return num * lanes2(scale)

    mask_ref[0] = jnp.where(ci <= ri, 0.0, NEG_INF)
    mask_ref[1] = jnp.where(ci > ri, 0.0, NEG_INF)

    def out_pass(c, carry):
        r0 = pl.multiple_of(c * L, L)
        q = qs_ref[pl.ds(r0, L), :]
        k = ks_ref[pl.ds(r0, L), :]
        v_aug = jnp.concatenate([v_ref[0, pl.ds(r0, L), :], ones_cols], axis=1)
        s = lax.dot_general(q, k, NT_DIMS, preferred_element_type=F32)
        rows = row_ref[c]
        tot = direction(q, v_aug, s, rows[0:1], tab_ref[0, pl.ds(r0, L), :], tab_ref[1, pl.ds(r0, L), :],
                        mask_ref[0], cfs_ref[c], mfs_ref[c])
        tot = tot + direction(q, v_aug, s, rows[1:2], tab_ref[3, pl.ds(r0, L), :],
                              tab_ref[4, pl.ds(r0, L), :], mask_ref[1], cbs_ref[c], mbs_ref[c])
        mo = mo_ref[0, pl.ds(r0, L), :].astype(F32)
        o_ref[0, pl.ds(r0, L), :] = (_layer_norm(tot) * jax.nn.sigmoid(mo)).astype(BF16)
        return carry

    lax.fori_loop(0, n_chunks, out_pass, 0)


def _mlstm(p_lat, p_ctx, gt, cgt, conv_w, conv_b, sec_lat, sec_ctx):
    B, T, _ = p_lat.shape
    Tc = p_ctx.shape[1]
    assert T % SCAN_L == 0 and Tc % SCAN_L == 0 and T // SCAN_L <= 8
    n_chunks = T // SCAN_L

    def lat(sec):
        return pl.BlockSpec((1, T, HEAD_W), lambda b, h: (b, 0, sec * HEADS + h))

    def cx(sec):
        return pl.BlockSpec((1, Tc, HEAD_W), lambda b, h: (b, 0, sec * HEADS + h))

    in_specs = [lat(s) for s in sec_lat] + [cx(s) for s in sec_ctx] + [
        pl.BlockSpec((1, 1, N_GK, 8, SCAN_L), lambda b, h: (b, h, 0, 0, 0)),
        pl.BlockSpec((1, 1, N_GK, 8, SCAN_L), lambda b, h: (b, h, 0, 0, 0)),
        pl.BlockSpec((3, HEAD_W), lambda b, h: (0, h)),
        pl.BlockSpec((1, HEAD_W), lambda b, h: (0, h)),
        pl.BlockSpec((3, HEAD_W), lambda b, h: (0, HEADS + h)),
        pl.BlockSpec((1, HEAD_W), lambda b, h: (0, HEADS + h)),
    ]
    state = [pltpu.VMEM((HEAD_W, AUG_W), F32), pltpu.VMEM((1, 1), F32)]
    snaps = [pltpu.VMEM((n_chunks, HEAD_W, AUG_W), BF16), pltpu.VMEM((n_chunks, 1, 1), F32)]
    return pl.pallas_call(
        _mlstm_kernel,
        out_shape=jax.ShapeDtypeStruct((B, T, BRANCH_W), BF16),
        grid=(B, HEADS),
        in_specs=in_specs,
        out_specs=pl.BlockSpec((1, T, HEAD_W), lambda b, h: (b, 0, h)),
        scratch_shapes=[pltpu.VMEM((T + 16, HEAD_W), F32),
                        pltpu.VMEM((T, HEAD_W), BF16),
                        pltpu.VMEM((T, HEAD_W), BF16),
                        pltpu.VMEM((Tc, HEAD_W), BF16),
                        pltpu.VMEM((N_TAB, T, 128), F32),
                        pltpu.VMEM((n_chunks, 8, SCAN_L), F32)]
        + state + state + snaps + snaps + [pltpu.VMEM((2, SCAN_L, SCAN_L), F32)],
        compiler_params=_cparams("parallel", "parallel"),
        name="mlstm",
    )(p_lat, p_lat, p_lat, p_lat, p_ctx, p_ctx, gt, cgt, conv_w, conv_b, conv_w, conv_b)


def _merge_kernel(alpha, r_ref, m_ref, gr_ref, gm_ref, x_ref, g1_ref, sh2_ref, sc2_ref,
                  lng_ref, lnb_ref, wr_ref, wm_ref, wo_ref, wrt_ref, brt_ref,
                  x1_ref, ua_ref, ub_ref, lt_ref):
    tm = x_ref.shape[1]
    for s in range(tm // MERGE_SUB):
        rows = slice(s * MERGE_SUB, (s + 1) * MERGE_SUB)
        yr = jnp.dot(r_ref[0, rows, :], wr_ref[...], preferred_element_type=F32)
        ym = jnp.dot(m_ref[0, rows, :], wm_ref[...], preferred_element_type=F32)
        y = (jax.nn.sigmoid(gr_ref[0, rows, :].astype(F32)) * yr
             + jax.nn.sigmoid(gm_ref[0, rows, :].astype(F32)) * ym)
        yo = jnp.dot(y.astype(BF16), wo_ref[...], preferred_element_type=F32)
        x1 = _layer_norm(alpha * x_ref[0, rows, :] + g1_ref[0] * yo) * lng_ref[...] + lnb_ref[...]
        x1_ref[0, rows, :] = x1
        u2 = _layer_norm(x1) * (1.0 + sc2_ref[0]) + sh2_ref[0]
        lt_ref[0, :, rows] = lax.dot_general(wrt_ref[...], u2.astype(BF16), NT_DIMS,
                                             preferred_element_type=F32) + brt_ref[...]
        ua_ref[0, rows, :] = _pack_pairs(u2[:, 0:PACK_W], u2[:, PACK_W:2 * PACK_W])
        ub_ref[0, rows, :] = _pack_pairs(u2[:, 2 * PACK_W:3 * PACK_W], u2[:, 3 * PACK_W:4 * PACK_W])


def _merge(alpha, r, m, p_lat, sec_gates, x, g1, sh2, sc2, lng, lnb, wr, wm, wo, wrt, brt):
    B, T, D = x.shape
    tm = MERGE_TM

    def tile(w):
        return pl.BlockSpec((1, tm, w), lambda b, i: (b, i, 0))

    def sec(s):
        return pl.BlockSpec((1, tm, BRANCH_W), lambda b, i: (b, i, s))

    def mod():
        return pl.BlockSpec((1, 1, D), lambda b, i: (b, 0, 0))

    def const(shape):
        return pl.BlockSpec(shape, lambda b, i: (0,) * len(shape))

    return pl.pallas_call(
        functools.partial(_merge_kernel, alpha),
        out_shape=(jax.ShapeDtypeStruct((B, T, D), F32),
                   jax.ShapeDtypeStruct((B, T, PACK_W), U32),
                   jax.ShapeDtypeStruct((B, T, PACK_W), U32),
                   jax.ShapeDtypeStruct((B, ROUTE_ROWS, T), F32)),
        grid=(B, T // tm),
        in_specs=[tile(BRANCH_W), tile(BRANCH_W), sec(sec_gates[0]), sec(sec_gates[1]), tile(D),
                  mod(), mod(), mod(), const((1, D)), const((1, D)),
                  const((BRANCH_W, D)), const((BRANCH_W, D)), const((D, D)),
                  const((ROUTE_ROWS, D)), const((ROUTE_ROWS, 1))],
        out_specs=(tile(D), tile(PACK_W), tile(PACK_W),
                   pl.BlockSpec((1, ROUTE_ROWS, tm), lambda b, i: (b, 0, i))),
        compiler_params=_cparams("parallel", "parallel"),
        name="merge",
    )(r, m, p_lat, p_lat, x, g1, sh2, sc2, lng, lnb, wr, wm, wo, wrt, brt)


def _route_kernel(lt_ref, ri_ref, rw_ref, cnt_ref, carry_ref, u_ref):
    i = pl.program_id(0)
    tm = lt_ref.shape[2]

    @pl.when(i == 0)
    def _():
        carry_ref[...] = jnp.zeros_like(carry_ref)
        r = lax.broadcasted_iota(I32, (tm, tm), 0)
        c = lax.broadcasted_iota(I32, (tm, tm), 1)
        u_ref[...] = (r < c).astype(BF16)

    lt = lt_ref[0]
    lg = lt[0:N_GROUPS, :]
    eg = jnp.exp(lg - jnp.max(lg, axis=0, keepdims=True))
    pg = eg / jnp.sum(eg, axis=0, keepdims=True)
    pg_top = jnp.max(pg, axis=0, keepdims=True)
    rows_g = lax.broadcasted_iota(I32, pg.shape, 0)
    g_idx = jnp.min(jnp.where(pg == pg_top, rows_g, N_GROUPS), axis=0, keepdims=True)

    le = jnp.zeros((EXP_PER_GROUP, tm), F32)
    for g in range(N_GROUPS):
        lo = 8 + g * EXP_PER_GROUP
        le = jnp.where(g_idx == g, lt[lo:lo + EXP_PER_GROUP, :], le)
    ee = jnp.exp(le - jnp.max(le, axis=0, keepdims=True))
    pe = ee / jnp.sum(ee, axis=0, keepdims=True)
    rows_e = lax.broadcasted_iota(I32, pe.shape, 0)
    v1 = jnp.max(pe, axis=0, keepdims=True)
    i1 = jnp.min(jnp.where(pe == v1, rows_e, EXP_PER_GROUP), axis=0, keepdims=True)
    pe2 = jnp.where(rows_e == i1, -1.0, pe)
    v2 = jnp.max(pe2, axis=0, keepdims=True)
    i2 = jnp.min(jnp.where(pe2 == v2, rows_e, EXP_PER_GROUP), axis=0, keepdims=True)
    den = v1 + v2
    rw_ref[0:1, :] = pg_top * v1 / den
    rw_ref[1:2, :] = pg_top * v2 / den
    e1 = g_idx * EXP_PER_GROUP + i1
    e2 = g_idx * EXP_PER_GROUP + i2

    rows_x = lax.broadcasted_iota(I32, (N_EXPERTS, tm), 0)
    oh1 = (rows_x == e1).astype(F32)
    oh2 = (rows_x == e2).astype(F32)
    both = oh1 + oh2
    before = carry_ref[:, 0:1] + jnp.dot(both.astype(BF16), u_ref[...], preferred_element_type=F32)
    ri_ref[0:1, :] = e1
    ri_ref[1:2, :] = e2
    ri_ref[2:3, :] = jnp.sum(oh1 * before, axis=0, keepdims=True).astype(I32)
    ri_ref[3:4, :] = jnp.sum(oh2 * before, axis=0, keepdims=True).astype(I32)
    carry_ref[...] = carry_ref[...] + jnp.sum(both, axis=1, keepdims=True)
    cnt_ref[...] = carry_ref[...].astype(I32)


def _route(lt):
    B, _, T = lt.shape
    tm = ROUTE_TM
    per_b = T // tm
    n = B * T
    return pl.pallas_call(
        _route_kernel,
        out_shape=(jax.ShapeDtypeStruct((4, n), I32),
                   jax.ShapeDtypeStruct((2, n), F32),
                   jax.ShapeDtypeStruct((N_EXPERTS, 128), I32)),
        grid=(n // tm,),
        in_specs=[pl.BlockSpec((1, ROUTE_ROWS, tm), lambda i: (i // per_b, 0, i % per_b))],
        out_specs=(pl.BlockSpec((4, tm), lambda i: (0, i)),
                   pl.BlockSpec((2, tm), lambda i: (0, i)),
                   pl.BlockSpec((N_EXPERTS, 128), lambda i: (0, 0))),
        scratch_shapes=[pltpu.VMEM((N_EXPERTS, 128), F32), pltpu.VMEM((tm, tm), BF16)],
        compiler_params=_cparams("arbitrary"),
        name="route",
    )(lt)


def _sc_mesh():
    return plsc.VectorSubcoreMesh(core_axis_name="c", subcore_axis_name="s")


def _sc_scatter2(rows, idx0, idx1, n_out):
    m, w = rows.shape

    @functools.partial(pl.kernel, out_type=jax.ShapeDtypeStruct((n_out, w), rows.dtype),
                       mesh=_sc_mesh(), scratch_types=[])
    def k(x_hbm, i0_hbm, i1_hbm, o_hbm):
        def body(x_vmem, i0_vmem, i1_vmem):
            pltpu.sync_copy(x_vmem, o_hbm.at[i0_vmem.at[0]])
            pltpu.sync_copy(x_vmem, o_hbm.at[i1_vmem.at[0]])

        pltpu.emit_pipeline(
            body,
            grid=(m // SC_WIN,),
            in_specs=[pl.BlockSpec((SC_WIN, w), lambda i: (i, 0)),
                      pl.BlockSpec((1, SC_WIN), lambda i: (0, i)),
                      pl.BlockSpec((1, SC_WIN), lambda i: (0, i))],
            out_specs=[],
            core_axis_name=("c", "s"),
            dimension_semantics=(pltpu.PARALLEL,),
        )(x_hbm, i0_hbm, i1_hbm)

    return k(rows, idx0.reshape(1, m), idx1.reshape(1, m))


def _sc_gather(table, idx):
    m = idx.shape[0]
    w = table.shape[1]

    @functools.partial(pl.kernel, out_type=jax.ShapeDtypeStruct((m, w), table.dtype),
                       mesh=_sc_mesh(), scratch_types=[])
    def k(t_hbm, i_hbm, o_hbm):
        def body(i_vmem, o_vmem):
            pltpu.sync_copy(t_hbm.at[i_vmem.at[0]], o_vmem)

        pltpu.emit_pipeline(
            body,
            grid=(m // SC_WIN,),
            in_specs=[pl.BlockSpec((1, SC_WIN), lambda i: (0, i))],
            out_specs=[pl.BlockSpec((SC_WIN, w), lambda i: (i, 0))],
            core_axis_name=("c", "s"),
            dimension_semantics=(pltpu.PARALLEL,),
        )(i_hbm, o_hbm)

    return k(table, idx.reshape(1, m))


def _expert_kernel(be_ref, nv_ref, xa_ref, xb_ref, w1f_ref, w3f_ref, w2f_ref, ya_ref, yb_ref,
                   w1_ref, w3_ref, w2_ref):
    j = pl.program_id(0)
    nv = nv_ref[j]

    @pl.when(jnp.logical_or(j == 0, be_ref[j] != be_ref[jnp.maximum(j - 1, 0)]))
    def _():
        w1_ref[0] = w1f_ref[0].astype(BF16)
        w3_ref[0] = w3f_ref[0].astype(BF16)
        w2_ref[0] = w2f_ref[0].astype(BF16)

    @pl.when(nv > 0)
    def _():
        for s in range(xa_ref.shape[0] // MOE_SUB):
            rows = slice(s * MOE_SUB, (s + 1) * MOE_SUB)
            valid = lax.broadcasted_iota(I32, (MOE_SUB, PACK_W), 0) + s * MOE_SUB < nv
            zero = jnp.zeros((MOE_SUB, PACK_W), U32)
            parts = _unpack_pairs(jnp.where(valid, xa_ref[rows, :], zero)) + \
                _unpack_pairs(jnp.where(valid, xb_ref[rows, :], zero))
            x = jnp.concatenate([p.astype(BF16) for p in parts], axis=1)
            h1 = jnp.dot(x, w1_ref[0], preferred_element_type=F32)
            h3 = jnp.dot(x, w3_ref[0], preferred_element_type=F32)
            y = jnp.dot((_silu(h1) * h3).astype(BF16), w2_ref[0], preferred_element_type=F32)
            ya_ref[rows, :] = _pack_pairs(y[:, 0:PACK_W], y[:, PACK_W:2 * PACK_W])
            yb_ref[rows, :] = _pack_pairs(y[:, 2 * PACK_W:3 * PACK_W], y[:, 3 * PACK_W:4 * PACK_W])

    @pl.when(nv == 0)
    def _():
        ya_ref[...] = jnp.zeros_like(ya_ref)
        yb_ref[...] = jnp.zeros_like(yb_ref)


def _experts(block_exp, n_valid, xa, xb, w1, w3, w2):
    n_slots = xa.shape[0]
    n_blocks = n_slots // MOE_BLK
    d, de = w1.shape[1], w1.shape[2]
    slot = pl.BlockSpec((MOE_BLK, PACK_W), lambda j, be, nv: (j, 0))
    grid_spec = pltpu.PrefetchScalarGridSpec(
        num_scalar_prefetch=2,
        grid=(n_blocks,),
        in_specs=[slot, slot,
                  pl.BlockSpec((1, d, de), lambda j, be, nv: (be[j], 0, 0)),
                  pl.BlockSpec((1, d, de), lambda j, be, nv: (be[j], 0, 0)),
                  pl.BlockSpec((1, de, d), lambda j, be, nv: (be[j], 0, 0))],
        out_specs=(slot, slot),
        scratch_shapes=[pltpu.VMEM((1, d, de), BF16), pltpu.VMEM((1, d, de), BF16),
                        pltpu.VMEM((1, de, d), BF16)],
    )
    return pl.pallas_call(
        _expert_kernel,
        out_shape=(jax.ShapeDtypeStruct((n_slots, PACK_W), U32),
                   jax.ShapeDtypeStruct((n_slots, PACK_W), U32)),
        grid_spec=grid_spec,
        compiler_params=_cparams("arbitrary"),
        name="experts",
    )(block_exp, n_valid, xa, xb, w1, w3, w2)


def _final_kernel(alpha, x1_ref, a0_ref, b0_ref, a1_ref, b1_ref, w_ref, g2_ref, lng_ref, lnb_ref, o_ref):
    w = w_ref[...]
    w0 = w[:, 0:1]
    w1 = w[:, 1:2]
    parts0 = _unpack_pairs(a0_ref[...]) + _unpack_pairs(b0_ref[...])
    parts1 = _unpack_pairs(a1_ref[...]) + _unpack_pairs(b1_ref[...])
    f = jnp.concatenate([w0 * p0 + w1 * p1 for p0, p1 in zip(parts0, parts1)], axis=1)
    o_ref[0] = _layer_norm(alpha * x1_ref[0] + g2_ref[0] * f) * lng_ref[...] + lnb_ref[...]


def _final(alpha, x1, ya, yb, w, g2, lng, lnb):
    B, T, D = x1.shape
    tm = MERGE_TM
    per_b = T // tm
    n_tiles = B * per_b

    def rows(k):
        return pl.BlockSpec((tm, PACK_W), lambda b, i: (k * n_tiles + b * per_b + i, 0))

    return pl.pallas_call(
        functools.partial(_final_kernel, alpha),
        out_shape=jax.ShapeDtypeStruct((B, T, D), F32),
        grid=(B, per_b),
        in_specs=[pl.BlockSpec((1, tm, D), lambda b, i: (b, i, 0)),
                  rows(0), rows(0), rows(1), rows(1),
                  pl.BlockSpec((tm, 2), lambda b, i: (b * per_b + i, 0)),
                  pl.BlockSpec((1, 1, D), lambda b, i: (b, 0, 0)),
                  pl.BlockSpec((1, D), lambda b, i: (0, 0)),
                  pl.BlockSpec((1, D), lambda b, i: (0, 0))],
        out_specs=pl.BlockSpec((1, tm, D), lambda b, i: (b, i, 0)),
        compiler_params=_cparams("parallel", "parallel"),
        name="final",
    )(x1, ya, yb, ya, yb, w, g2, lng, lnb)


def _rotary_tables(T):
    quarter = HEAD_W // 4
    freqs = ROPE_BASE ** (-jnp.arange(quarter, dtype=F32) / quarter)
    t = jnp.arange(T)
    ang_r = (t // GRID_W).astype(F32)[:, None] * freqs[None, :]
    ang_c = (t % GRID_W).astype(F32)[:, None] * freqs[None, :]
    cos = jnp.concatenate([jnp.cos(ang_r)] * 2 + [jnp.cos(ang_c)] * 2, axis=1)
    sin = jnp.concatenate([-jnp.sin(ang_r), jnp.sin(ang_r), -jnp.sin(ang_c), jnp.sin(ang_c)], axis=1)
    return cos, sin


def _per_head_gates(gt):
    B, _, T = gt.shape
    n_chunks = T // SCAN_L
    gth = gt.reshape(B, N_GK, HEADS, n_chunks, SCAN_L).transpose(0, 2, 1, 3, 4)
    return jnp.pad(gth, ((0, 0), (0, 0), (0, 0), (0, 8 - n_chunks), (0, 0)))


def _table_lookup(table, idx):
    sel = idx[..., None] == jnp.arange(table.shape[0], dtype=idx.dtype)
    return jnp.sum(jnp.where(sel, table, 0), axis=-1)


def kernel(x, c, ctx, c_ctx, w_ada, b_ada, w_in, b_mgate, ml_conv_w, ml_conv_b, ret_decay_logit, w_ret_branch, w_ml_branch, w_out, ln1_g, ln1_b, w_rg, b_rg, w_re, b_re, w_e1, w_e3, w_e2, ln2_g, ln2_b):
    B, T, D = x.shape
    depth = w_ada.shape[0]
    assert depth == 1 and D == BRANCH_W and T % PROJ_TM == 0 and T % GRID_W == 0
    alpha = (2 * depth) ** 0.25
    n_tok = B * T

    n_rows = -(-(B + 1) // 8) * 8
    cs = jnp.zeros((n_rows, D), F32).at[:B].set(c).at[B].set(c_ctx)
    mod = _ada(cs, w_ada[0], b_ada[0][None, :])
    sh1, sc1, g1, sh2, sc2, g2 = [mod[:B, None, i * D:(i + 1) * D] for i in range(6)]
    csh1 = mod[B, 0 * D:1 * D].reshape(1, 1, D)
    csc1 = mod[B, 1 * D:2 * D].reshape(1, 1, D)

    w = w_in[0]
    sec_w = [w[:, s * BRANCH_W:(s + 1) * BRANCH_W] for s in range(8)]
    g_lo = 8 * BRANCH_W
    w_gate_t = w[:, g_lo:g_lo + N_GATES].T.astype(BF16)
    b_gate = b_mgate[0][:, None]
    sec_w += [w[:, g_lo + N_GATES:g_lo + N_GATES + D], w[:, g_lo + N_GATES + D:]]
    w_lat = jnp.concatenate(sec_w, axis=1).astype(BF16)
    w_ctx = jnp.concatenate([sec_w[1], sec_w[2], sec_w[5], sec_w[6]], axis=1).astype(BF16)
    kinds_lat = ("rot", "rot_scale") + ("plain",) * 8
    kinds_ctx = ("scale", "plain", "plain", "plain")
    p_lat, gt_lat = _proj(x, sh1, sc1, w_lat, w_gate_t, b_gate, kinds_lat, _rotary_tables(T))
    Tc = ctx.shape[1]
    p_ctx, gt_ctx = _proj(ctx.reshape(1, B * Tc, D), csh1, csc1, w_ctx, w_gate_t, b_gate, kinds_ctx)
    p_ctx = p_ctx.reshape(B, Tc, -1)
    gt_ctx = gt_ctx.reshape(N_GATES, B, Tc).transpose(1, 0, 2)

    ret = _retention(ret_decay_logit[0], p_lat, p_ctx, (0, 1, 2, 3), (0, 1))
    mls = _mlstm(p_lat, p_ctx, _per_head_gates(gt_lat), _per_head_gates(gt_ctx),
                 ml_conv_w[0], ml_conv_b[0][None, :], (4, 5, 6, 7), (2, 3))

    wrt = jnp.zeros((ROUTE_ROWS, D), F32).at[:N_GROUPS].set(w_rg[0].T).at[8:8 + N_EXPERTS].set(w_re[0].T)
    brt = jnp.zeros((ROUTE_ROWS, 1), F32).at[:N_GROUPS, 0].set(b_rg[0]).at[8:8 + N_EXPERTS, 0].set(b_re[0])
    x1, ua, ub, lt = _merge(alpha, ret, mls, p_lat, (8, 9), x, g1, sh2, sc2,
                            ln1_g[0][None, :], ln1_b[0][None, :],
                            w_ret_branch[0].astype(BF16), w_ml_branch[0].astype(BF16),
                            w_out[0].astype(BF16), wrt.astype(BF16), brt)

    ri, rw, cnt = _route(lt)

    counts = cnt[:, 0]
    padded = (counts + MOE_BLK - 1) // MOE_BLK * MOE_BLK
    pad_end = jnp.cumsum(padded)
    pad_off = pad_end - padded
    dest = _table_lookup(pad_off, ri[0:2]) + ri[2:4]
    n_blocks = (2 * n_tok) // MOE_BLK + N_EXPERTS
    n_slots = n_blocks * MOE_BLK
    block_start = jnp.arange(n_blocks, dtype=I32) * MOE_BLK
    block_exp = jnp.minimum((block_start[:, None] >= pad_end[None, :]).sum(1), N_EXPERTS - 1).astype(I32)
    n_valid = jnp.clip(_table_lookup(counts, block_exp) - (block_start - _table_lookup(pad_off, block_exp)),
                       0, MOE_BLK).astype(I32)

    xa = _sc_scatter2(ua.reshape(n_tok, PACK_W), dest[0], dest[1], n_slots)
    xb = _sc_scatter2(ub.reshape(n_tok, PACK_W), dest[0], dest[1], n_slots)
    ya, yb = _experts(block_exp, n_valid, xa, xb, w_e1[0], w_e3[0], w_e2[0])
    dflat = dest.reshape(2 * n_tok)
    ga = _sc_gather(ya, dflat)
    gb = _sc_gather(yb, dflat)
    return _final(alpha, x1, ga, gb, rw.T, g2, ln2_g[0][None, :], ln2_b[0][None, :])
```

```python
import functools

import jax
import jax.numpy as jnp
from jax import lax
from jax.experimental import pallas as pl
from jax.experimental.pallas import tpu as pltpu
from jax.experimental.pallas import tpu_sc as plsc

F32 = jnp.float32
BF16 = jnp.bfloat16
U32 = jnp.uint32
I32 = jnp.int32
HIGHEST = lax.Precision.HIGHEST

HEADS = 4
HEAD_W = 256
BRANCH_W = HEADS * HEAD_W
GRID_W = 64
ROPE_BASE = 10000.0
N_GATES = 16
N_GK = N_GATES // HEADS
N_GROUPS = 4
EXP_PER_GROUP = 8
N_EXPERTS = N_GROUPS * EXP_PER_GROUP
LN_EPS = 1e-5
NEG_INF = -1e30
KEY_SCALE = HEAD_W ** -0.5

SCAN_L = 256
PROJ_TM = 2048
PROJ_SUB = 256
MERGE_TM = 512
MERGE_SUB = 512
MOE_SUB = 256
ROUTE_TM = 512
MOE_BLK = 512
SC_WIN = 128
PACK_W = 256
ROUTE_ROWS = 64
VMEM_LIMIT = 48 * 1024 * 1024

NT_DIMS = (((1,), (1,)), ((), ()))
TN_DIMS = (((0,), (0,)), ((), ()))


def _cparams(*sem):
    return pltpu.CompilerParams(dimension_semantics=sem, vmem_limit_bytes=VMEM_LIMIT)


def _layer_norm(x):
    mu = jnp.mean(x, axis=-1, keepdims=True)
    xc = x - mu
    var = jnp.mean(xc * xc, axis=-1, keepdims=True)
    return xc * lax.rsqrt(var + LN_EPS)


def _log_sigmoid(x):
    return jnp.minimum(x, 0.0) - jnp.log1p(jnp.exp(-jnp.abs(x)))


def _silu(x):
    return x * jax.nn.sigmoid(x)


def _pack_pairs(hi, lo):
    hb = lax.bitcast_convert_type(hi.astype(BF16).astype(F32), U32)
    lb = lax.bitcast_convert_type(lo.astype(BF16).astype(F32), U32)
    return (hb & jnp.uint32(0xFFFF0000)) | (lb >> 16)


def _unpack_pairs(p):
    hi = lax.bitcast_convert_type(p & jnp.uint32(0xFFFF0000), F32)
    lo = lax.bitcast_convert_type(p << 16, F32)
    return hi, lo


def _ada_kernel(c_ref, w_ref, b_ref, o_ref):
    s = _silu(c_ref[...])
    o_ref[...] = jnp.dot(s, w_ref[...], precision=HIGHEST, preferred_element_type=F32) + b_ref[...]


def _ada(cs, w, b):
    rows, d = cs.shape
    cols = w.shape[1]
    tn = 1024
    return pl.pallas_call(
        _ada_kernel,
        out_shape=jax.ShapeDtypeStruct((rows, cols), F32),
        grid=(cols // tn,),
        in_specs=[pl.BlockSpec((rows, d), lambda j: (0, 0)),
                  pl.BlockSpec((d, tn), lambda j: (0, j)),
                  pl.BlockSpec((1, tn), lambda j: (0, j))],
        out_specs=pl.BlockSpec((rows, tn), lambda j: (0, j)),
        compiler_params=_cparams("parallel"),
        name="ada",
    )(cs, w, b)


def _proj_kernel(kinds, x_ref, sh_ref, sc_ref, w_ref, wg_ref, bg_ref, *rest):
    if "rot" in kinds or "rot_scale" in kinds:
        cos_ref, sin_ref, o_ref, gt_ref, u_ref = rest
    else:
        o_ref, gt_ref, u_ref = rest
    j = pl.program_id(2)
    tm = x_ref.shape[1]
    sub = min(PROJ_SUB, tm)

    def rotary(acc, rows, scale):
        for s in range(acc.shape[1] // 128):
            a = acc[:, s * 128:(s + 1) * 128]
            half = s % 2
            cs = cos_ref[rows, half * 128:(half + 1) * 128]
            sn = sin_ref[rows, half * 128:(half + 1) * 128]
            r = a * cs + pltpu.roll(a, 64, 1) * sn
            if scale != 1.0:
                r = r * scale
            o_ref[0, rows, s * 128:(s + 1) * 128] = r.astype(BF16)

    def section(kind, first):
        for r in range(tm // sub):
            rows = slice(r * sub, (r + 1) * sub)
            if first:
                u = _layer_norm(x_ref[0, rows, :]) * (1.0 + sc_ref[0]) + sh_ref[0]
                ub = u.astype(BF16)
                u_ref[rows, :] = ub
                gt_ref[0, :, rows] = lax.dot_general(wg_ref[...], ub, NT_DIMS,
                                                     preferred_element_type=F32) + bg_ref[...]
            else:
                ub = u_ref[rows, :]
            acc = jnp.dot(ub, w_ref[...], preferred_element_type=F32)
            if kind == "rot":
                rotary(acc, rows, 1.0)
            elif kind == "rot_scale":
                rotary(acc, rows, KEY_SCALE)
            elif kind == "scale":
                o_ref[0, rows, :] = (acc * KEY_SCALE).astype(BF16)
            else:
                o_ref[0, rows, :] = acc.astype(BF16)

    variants = {}
    for s, kind in enumerate(kinds):
        variants.setdefault((kind, s == 0), []).append(s)
    for (kind, first), secs in variants.items():
        cond = functools.reduce(jnp.logical_or, [j == s for s in secs])

        @pl.when(cond)
        def _(kind=kind, first=first):
            section(kind, first)


def _proj(x, sh, sc, w_main, w_gate_t, b_gate, kinds, tables=None):
    B, T, D = x.shape
    n_sec = len(kinds)
    tm = min(PROJ_TM, T)
    tn = BRANCH_W
    in_specs = [
        pl.BlockSpec((1, tm, D), lambda i, b, j: (b, i, 0)),
        pl.BlockSpec((1, 1, D), lambda i, b, j: (b, 0, 0)),
        pl.BlockSpec((1, 1, D), lambda i, b, j: (b, 0, 0)),
        pl.BlockSpec((D, tn), lambda i, b, j: (0, j)),
        pl.BlockSpec((N_GATES, D), lambda i, b, j: (0, 0)),
        pl.BlockSpec((N_GATES, 1), lambda i, b, j: (0, 0)),
    ]
    args = [x, sh, sc, w_main, w_gate_t, b_gate]
    if tables is not None:
        in_specs += [pl.BlockSpec((tm, HEAD_W), lambda i, b, j: (i, 0))] * 2
        args += list(tables)
    return pl.pallas_call(
        functools.partial(_proj_kernel, kinds),
        out_shape=(jax.ShapeDtypeStruct((B, T, n_sec * tn), BF16),
                   jax.ShapeDtypeStruct((B, N_GATES, T), F32)),
        grid=(T // tm, B, n_sec),
        in_specs=in_specs,
        out_specs=(pl.BlockSpec((1, tm, tn), lambda i, b, j: (b, i, j)),
                   pl.BlockSpec((1, N_GATES, tm), lambda i, b, j: (b, 0, i))),
        scratch_shapes=[pltpu.VMEM((tm, D), BF16)],
        compiler_params=_cparams("parallel", "parallel", "arbitrary"),
        name="proj_lat" if tables is not None else "proj_ctx",
    )(*args)


def _ret_kernel(dl_ref, q_ref, k_ref, v_ref, rg_ref, ck_ref, cv_ref, o_ref,
                sf_ref, sb_ref, fs_ref, bs_ref, dec_ref, d_ref):
    h = pl.program_id(1)
    L = SCAN_L
    n_chunks = q_ref.shape[1] // L
    n_ctx_chunks = ck_ref.shape[1] // L
    lgf = _log_sigmoid(jnp.full((1, 1), dl_ref[0, h], F32))
    lgb = _log_sigmoid(jnp.full((1, 1), dl_ref[1, h], F32))

    ri = lax.broadcasted_iota(I32, (L, L), 0)
    ci = lax.broadcasted_iota(I32, (L, L), 1)
    rel = (ri - ci).astype(F32)
    d_ref[...] = jnp.where(rel >= 0.0, jnp.exp(jnp.maximum(rel, 0.0) * lgf),
                           jnp.exp(jnp.maximum(-rel, 0.0) * lgb))
    row = lax.broadcasted_iota(I32, (L, HEAD_W), 0).astype(F32)
    dec_ref[0] = jnp.exp((row + 1.0) * lgf)
    dec_ref[1] = jnp.exp((L - 1.0 - row) * lgf)
    dec_ref[2] = jnp.exp((L - row) * lgb)
    dec_ref[3] = jnp.exp(row * lgb)
    cdf = jnp.exp(L * lgf)
    cdb = jnp.exp(L * lgb)

    def update(s_ref, kc, vc, kd, cd):
        kdec = (kc.astype(F32) * kd).astype(BF16)
        s_ref[...] = s_ref[...] * cd + lax.dot_general(kdec, vc, TN_DIMS, preferred_element_type=F32)

    sf_ref[...] = jnp.zeros_like(sf_ref)
    sb_ref[...] = jnp.zeros_like(sb_ref)
    for c in range(n_ctx_chunks):
        update(sf_ref, ck_ref[0, c * L:(c + 1) * L, :], cv_ref[0, c * L:(c + 1) * L, :], dec_ref[1], cdf)
    for c in reversed(range(n_ctx_chunks)):
        update(sb_ref, ck_ref[0, c * L:(c + 1) * L, :], cv_ref[0, c * L:(c + 1) * L, :], dec_ref[3], cdb)

    def state_pass(i, carry):
        cb = n_chunks - 1 - i
        rf = pl.multiple_of(i * L, L)
        rb = pl.multiple_of(cb * L, L)
        fs_ref[i] = sf_ref[...].astype(BF16)
        bs_ref[cb] = sb_ref[...].astype(BF16)
        update(sf_ref, k_ref[0, pl.ds(rf, L), :], v_ref[0, pl.ds(rf, L), :], dec_ref[1], cdf)
        update(sb_ref, k_ref[0, pl.ds(rb, L), :], v_ref[0, pl.ds(rb, L), :], dec_ref[3], cdb)
        return carry

    lax.fori_loop(0, n_chunks - 1, state_pass, 0)
    fs_ref[n_chunks - 1] = sf_ref[...].astype(BF16)
    bs_ref[0] = sb_ref[...].astype(BF16)

    def out_chunk(c):
        r0 = pl.multiple_of(c * L, L)
        q = q_ref[0, pl.ds(r0, L), :]
        k = k_ref[0, pl.ds(r0, L), :]
        v = v_ref[0, pl.ds(r0, L), :]
        s = lax.dot_general(q, k, NT_DIMS, preferred_element_type=F32)
        att = (s * d_ref[...]).astype(BF16)
        o = jnp.dot(att, v, preferred_element_type=F32)
        o = o + jnp.dot(q, fs_ref[c], preferred_element_type=F32) * dec_ref[0]
        o = o + jnp.dot(q, bs_ref[c], preferred_element_type=F32) * dec_ref[2]
        rg = rg_ref[0, pl.ds(r0, L), :].astype(F32)
        o_ref[0, pl.ds(r0, L), :] = (_layer_norm(o) * _silu(rg)).astype(BF16)

    def out_pass(i, carry):
        out_chunk(2 * i)
        out_chunk(2 * i + 1)
        return carry

    lax.fori_loop(0, n_chunks // 2, out_pass, 0)


def _retention(decay_logit, p_lat, p_ctx, sec_lat, sec_ctx):
    B, T, _ = p_lat.shape
    Tc = p_ctx.shape[1]
    assert T % (2 * SCAN_L) == 0 and Tc % SCAN_L == 0
    n_chunks = T // SCAN_L

    def lat(sec):
        return pl.BlockSpec((1, T, HEAD_W), lambda b, h: (b, 0, sec * HEADS + h))

    def cx(sec):
        return pl.BlockSpec((1, Tc, HEAD_W), lambda b, h: (b, 0, sec * HEADS + h))

    return pl.pallas_call(
        _ret_kernel,
        out_shape=jax.ShapeDtypeStruct((B, T, BRANCH_W), BF16),
        grid=(B, HEADS),
        in_specs=[pl.BlockSpec(memory_space=pltpu.SMEM)]
        + [lat(s) for s in sec_lat] + [cx(s) for s in sec_ctx],
        out_specs=pl.BlockSpec((1, T, HEAD_W), lambda b, h: (b, 0, h)),
        scratch_shapes=[pltpu.VMEM((HEAD_W, HEAD_W), F32),
                        pltpu.VMEM((HEAD_W, HEAD_W), F32),
                        pltpu.VMEM((n_chunks, HEAD_W, HEAD_W), BF16),
                        pltpu.VMEM((n_chunks, HEAD_W, HEAD_W), BF16),
                        pltpu.VMEM((4, SCAN_L, HEAD_W), F32),
                        pltpu.VMEM((SCAN_L, SCAN_L), F32)],
        compiler_params=_cparams("parallel", "parallel"),
        name="retention",
    )(decay_logit, p_lat, p_lat, p_lat, p_lat, p_ctx, p_ctx)


def _mlstm_kernel_old(q_ref, k_ref, v_ref, mo_ref, ck_ref, cv_ref, g_ref, gt_ref, cg_ref, cgt_ref,
                  wq_ref, bq_ref, wk_ref, bk_ref, o_ref,
                  xf_ref, qs_ref, ks_ref, cks_ref,
                  glc_ref, csc_ref, rcc_ref, glr_ref, csr_ref, rcr_ref,
                  cf_ref, nf_ref, mf_ref, cb_ref, nb_ref, mb_ref,
                  cfs_ref, nfs_ref, mfs_ref, cbs_ref, nbs_ref, mbs_ref):
    L = SCAN_L
    T = q_ref.shape[1]
    Tc = ck_ref.shape[1]
    n_chunks = T // L
    n_ctx_chunks = Tc // L
    CV = 128

    def conv_silu(src_ref, w_ref, b_ref, dst_ref, t_len, scale):
        xf_ref[pl.ds(0, 8), :] = jnp.zeros((8, HEAD_W), F32)
        xf_ref[pl.ds(8 + t_len, 8), :] = jnp.zeros((8, HEAD_W), F32)
        xf_ref[pl.ds(8, t_len), :] = src_ref[0].astype(F32)
        w = w_ref[...]
        b = b_ref[...]

        def body(c, carry):
            r0 = pl.multiple_of(c * CV, CV)
            win = xf_ref[pl.ds(r0, CV + 16), :]
            prev = pltpu.roll(win, 1, 0)[8:8 + CV, :]
            cur = win[8:8 + CV, :]
            nxt = pltpu.roll(win, CV + 15, 0)[8:8 + CV, :]
            y = prev * w[0:1, :] + cur * w[1:2, :] + nxt * w[2:3, :] + b
            y = _silu(y)
            if scale != 1.0:
                y = y * scale
            dst_ref[pl.ds(r0, CV), :] = y.astype(BF16)
            return carry

        lax.fori_loop(0, t_len // CV, body, 0)

    conv_silu(q_ref, wq_ref, bq_ref, qs_ref, T, 1.0)
    conv_silu(k_ref, wk_ref, bk_ref, ks_ref, T, KEY_SCALE)
    conv_silu(ck_ref, wk_ref, bk_ref, cks_ref, Tc, KEY_SCALE)

    ri = lax.broadcasted_iota(I32, (L, L), 0)
    ci = lax.broadcasted_iota(I32, (L, L), 1)
    tri_l = (ci <= ri).astype(F32)
    tri_u = (ci >= ri).astype(F32)

    def gate_tables(gc, gtc):
        lane = lax.broadcasted_iota(I32, gc.shape, 1)
        gl = jnp.where(lane % 2 == 1, _log_sigmoid(gc), gc)
        sub = lax.broadcasted_iota(I32, gtc.shape, 0)
        gtl = jnp.where(sub % 2 == 1, _log_sigmoid(gtc), gtc)
        cs_col = jnp.dot(tri_l, gl, precision=HIGHEST, preferred_element_type=F32)
        rc_col = jnp.dot(tri_u, gl, precision=HIGHEST, preferred_element_type=F32)
        cs_row = jnp.dot(gtl, tri_u, precision=HIGHEST, preferred_element_type=F32)
        rc_row = jnp.dot(gtl, tri_l, precision=HIGHEST, preferred_element_type=F32)
        return gl, cs_col, rc_col, gtl, cs_row, rc_row

    def pick(tables, backward):
        gl, cs_col, rc_col, gtl, cs_row, rc_row = tables
        if not backward:
            return gl[:, 0:1], gtl[0:1, :], cs_col[:, 1:2], cs_row[1:2, :], cs_row[1:2, L - 1:L]
        return gl[:, 2:3], gtl[2:3, :], rc_col[:, 3:4], rc_row[3:4, :], rc_row[3:4, 0:1]

    def lat_tables(r0):
        return (glc_ref[pl.ds(r0, L), :], csc_ref[pl.ds(r0, L), :], rcc_ref[pl.ds(r0, L), :],
                glr_ref[:, pl.ds(r0, L)], csr_ref[:, pl.ds(r0, L)], rcr_ref[:, pl.ds(r0, L)])

    def table_pass(c, carry):
        r0 = pl.multiple_of(c * L, L)
        gl, cs_col, rc_col, gtl, cs_row, rc_row = gate_tables(
            g_ref[0, 0, pl.ds(r0, L), :], gt_ref[0, 0, :, pl.ds(r0, L)])
        glc_ref[pl.ds(r0, L), :] = gl
        csc_ref[pl.ds(r0, L), :] = cs_col
        rcc_ref[pl.ds(r0, L), :] = rc_col
        glr_ref[:, pl.ds(r0, L)] = gtl
        csr_ref[:, pl.ds(r0, L)] = cs_row
        rcr_ref[:, pl.ds(r0, L)] = rc_row
        return carry

    lax.fori_loop(0, n_chunks, table_pass, 0)

    def advance(k, v, gate_vecs, c_ref, n_ref, m_ref):
        i_col, i_row, b_col, b_row, b_last = gate_vecs
        m = m_ref[...]
        g_col = b_last - b_col + i_col
        g_row = b_last - b_row + i_row
        m_new = jnp.maximum(b_last + m, jnp.max(g_row, axis=-1, keepdims=True))
        kw = k.astype(F32) * jnp.exp(g_col - m_new)
        decay = jnp.exp(b_last + m - m_new)
        c_ref[...] = decay * c_ref[...] + lax.dot_general(kw.astype(BF16), v, TN_DIMS,
                                                          preferred_element_type=F32)
        n_ref[...] = decay * n_ref[...] + jnp.sum(kw, axis=0, keepdims=True)
        m_ref[...] = m_new

    for refs in ((cf_ref, nf_ref, mf_ref), (cb_ref, nb_ref, mb_ref)):
        for r in refs:
            r[...] = jnp.zeros_like(r)
    ctx_tabs = [gate_tables(cg_ref[0, 0, c * L:(c + 1) * L, :], cgt_ref[0, 0, :, c * L:(c + 1) * L])
                for c in range(n_ctx_chunks)]
    for c in range(n_ctx_chunks):
        advance(cks_ref[c * L:(c + 1) * L, :], cv_ref[0, c * L:(c + 1) * L, :],
                pick(ctx_tabs[c], False), cf_ref, nf_ref, mf_ref)
    for c in reversed(range(n_ctx_chunks)):
        advance(cks_ref[c * L:(c + 1) * L, :], cv_ref[0, c * L:(c + 1) * L, :],
                pick(ctx_tabs[c], True), cb_ref, nb_ref, mb_ref)

    def snapshot(c, src, dst):
        dst[0][c] = src[0][...].astype(BF16)
        dst[1][c] = src[1][...]
        dst[2][c] = src[2][...]

    fwd_run, fwd_snap = (cf_ref, nf_ref, mf_ref), (cfs_ref, nfs_ref, mfs_ref)
    bwd_run, bwd_snap = (cb_ref, nb_ref, mb_ref), (cbs_ref, nbs_ref, mbs_ref)

    def state_pass(i, carry):
        cb = n_chunks - 1 - i
        rf = pl.multiple_of(i * L, L)
        rb = pl.multiple_of(cb * L, L)
        snapshot(i, fwd_run, fwd_snap)
        snapshot(cb, bwd_run, bwd_snap)
        advance(ks_ref[pl.ds(rf, L), :], v_ref[0, pl.ds(rf, L), :], pick(lat_tables(rf), False), *fwd_run)
        advance(ks_ref[pl.ds(rb, L), :], v_ref[0, pl.ds(rb, L), :], pick(lat_tables(rb), True), *bwd_run)
        return carry

    lax.fori_loop(0, n_chunks - 1, state_pass, 0)
    snapshot(n_chunks - 1, fwd_run, fwd_snap)
    snapshot(0, bwd_run, bwd_snap)

    def direction(q, v, s, gate_vecs, mask, c_in, n_in, m_in):
        _, i_row, b_col, b_row, _ = gate_vecs
        d = jnp.where(mask, b_col - b_row + i_row, NEG_INF)
        inter = b_col + m_in
        m_row = jnp.maximum(jnp.max(d, axis=-1, keepdims=True), inter)
        a = jnp.exp(inter - m_row)
        att = s * jnp.exp(d - m_row)
        num = jnp.dot(att.astype(BF16), v, preferred_element_type=F32)
        num = num + a * jnp.dot(q, c_in, preferred_element_type=F32)
        qn = jnp.sum(q.astype(F32) * n_in, axis=-1, keepdims=True)
        den = jnp.sum(att, axis=-1, keepdims=True) + a * qn
        return num * (1.0 / jnp.maximum(jnp.abs(den), jnp.exp(-m_row)))

    def out_pass(c, carry):
        r0 = pl.multiple_of(c * L, L)
        q = qs_ref[pl.ds(r0, L), :]
        k = ks_ref[pl.ds(r0, L), :]
        v = v_ref[0, pl.ds(r0, L), :]
        s = lax.dot_general(q, k, NT_DIMS, preferred_element_type=F32)
        tabs = lat_tables(r0)
        tot = direction(q, v, s, pick(tabs, False), ci <= ri, cfs_ref[c], nfs_ref[c], mfs_ref[c])
        tot = tot + direction(q, v, s, pick(tabs, True), ci > ri, cbs_ref[c], nbs_ref[c], mbs_ref[c])
        mo = mo_ref[0, pl.ds(r0, L), :].astype(F32)
        o_ref[0, pl.ds(r0, L), :] = (_layer_norm(tot) * jax.nn.sigmoid(mo)).astype(BF16)
        return carry

    lax.fori_loop(0, n_chunks, out_pass, 0)


def _mlstm_old(p_lat, p_ctx, g, gt, cg, cgt, conv_w, conv_b, sec_lat, sec_ctx):
    B, T, _ = p_lat.shape
    Tc = p_ctx.shape[1]
    assert T % SCAN_L == 0 and Tc % SCAN_L == 0
    n_chunks = T // SCAN_L

    def lat(sec):
        return pl.BlockSpec((1, T, HEAD_W), lambda b, h: (b, 0, sec * HEADS + h))

    def cx(sec):
        return pl.BlockSpec((1, Tc, HEAD_W), lambda b, h: (b, 0, sec * HEADS + h))

    in_specs = [lat(s) for s in sec_lat] + [cx(s) for s in sec_ctx] + [
        pl.BlockSpec((1, 1, T, N_GK), lambda b, h: (b, h, 0, 0)),
        pl.BlockSpec((1, 1, N_GK, T), lambda b, h: (b, h, 0, 0)),
        pl.BlockSpec((1, 1, Tc, N_GK), lambda b, h: (b, h, 0, 0)),
        pl.BlockSpec((1, 1, N_GK, Tc), lambda b, h: (b, h, 0, 0)),
        pl.BlockSpec((3, HEAD_W), lambda b, h: (0, h)),
        pl.BlockSpec((1, HEAD_W), lambda b, h: (0, h)),
        pl.BlockSpec((3, HEAD_W), lambda b, h: (0, HEADS + h)),
        pl.BlockSpec((1, HEAD_W), lambda b, h: (0, HEADS + h)),
    ]
    state = [pltpu.VMEM((HEAD_W, HEAD_W), F32), pltpu.VMEM((1, HEAD_W), F32), pltpu.VMEM((1, 1), F32)]
    snaps = [pltpu.VMEM((n_chunks, HEAD_W, HEAD_W), BF16), pltpu.VMEM((n_chunks, 1, HEAD_W), F32),
             pltpu.VMEM((n_chunks, 1, 1), F32)]
    return pl.pallas_call(
        _mlstm_kernel,
        out_shape=jax.ShapeDtypeStruct((B, T, BRANCH_W), BF16),
        grid=(B, HEADS),
        in_specs=in_specs,
        out_specs=pl.BlockSpec((1, T, HEAD_W), lambda b, h: (b, 0, h)),
        scratch_shapes=[pltpu.VMEM((T + 16, HEAD_W), F32),
                        pltpu.VMEM((T, HEAD_W), BF16),
                        pltpu.VMEM((T, HEAD_W), BF16),
                        pltpu.VMEM((Tc, HEAD_W), BF16)]
        + [pltpu.VMEM((T, N_GK), F32)] * 3 + [pltpu.VMEM((N_GK, T), F32)] * 3
        + state + state + snaps + snaps,
        compiler_params=_cparams("parallel", "parallel"),
        name="mlstm",
    )(p_lat, p_lat, p_lat, p_lat, p_ctx, p_ctx, g, gt, cg, cgt, conv_w, conv_b, conv_w, conv_b)


N_TAB = 6
AUG_W = HEAD_W + 128


def _split3(x):
    hi = x.astype(BF16).astype(F32)
    r1 = x - hi
    mid = r1.astype(BF16).astype(F32)
    lo = (r1 - mid).astype(BF16).astype(F32)
    return jnp.concatenate([hi, mid, lo], axis=0).astype(BF16)


def _mlstm_kernel(q_ref, k_ref, v_ref, mo_ref, ck_ref, cv_ref, gt_ref, cgt_ref,
                  wq_ref, bq_ref, wk_ref, bk_ref, o_ref,
                  xf_ref, qs_ref, ks_ref, cks_ref, tab_ref, row_ref,
                  cf_ref, mf_ref, cb_ref, mb_ref, cfs_ref, mfs_ref, cbs_ref, mbs_ref, mask_ref):
    L = SCAN_L
    T = q_ref.shape[1]
    Tc = ck_ref.shape[1]
    n_chunks = T // L
    n_ctx_chunks = Tc // L
    CV = 128

    def conv_silu(src_ref, w_ref, b_ref, dst_ref, t_len, scale):
        xf_ref[pl.ds(0, 8), :] = jnp.zeros((8, HEAD_W), F32)
        xf_ref[pl.ds(8 + t_len, 8), :] = jnp.zeros((8, HEAD_W), F32)
        xf_ref[pl.ds(8, t_len), :] = src_ref[0].astype(F32)
        w = w_ref[...]
        b = b_ref[...]

        def body(c, carry):
            r0 = pl.multiple_of(c * CV, CV)
            win = xf_ref[pl.ds(r0, CV + 16), :]
            prev = pltpu.roll(win, 1, 0)[8:8 + CV, :]
            cur = win[8:8 + CV, :]
            nxt = pltpu.roll(win, CV + 15, 0)[8:8 + CV, :]
            y = prev * w[0:1, :] + cur * w[1:2, :] + nxt * w[2:3, :] + b
            y = _silu(y)
            if scale != 1.0:
                y = y * scale
            dst_ref[pl.ds(r0, CV), :] = y.astype(BF16)
            return carry

        lax.fori_loop(0, t_len // CV, body, 0)

    conv_silu(q_ref, wq_ref, bq_ref, qs_ref, T, 1.0)
    conv_silu(k_ref, wk_ref, bk_ref, ks_ref, T, KEY_SCALE)
    conv_silu(ck_ref, wk_ref, bk_ref, cks_ref, Tc, KEY_SCALE)

    ri = lax.broadcasted_iota(I32, (L, L), 0)
    ci = lax.broadcasted_iota(I32, (L, L), 1)
    tri_u = (ri <= ci).astype(BF16)
    lane8 = lax.broadcasted_iota(I32, (8, L), 1)
    sub8 = lax.broadcasted_iota(I32, (8, L), 0)
    sel_r = lax.broadcasted_iota(I32, (24, 8 * 128), 0) % 8
    sel_c = lax.broadcasted_iota(I32, (24, 8 * 128), 1) // 128
    sel3 = (sel_r == sel_c).astype(BF16)
    ones_cols = jnp.ones((L, AUG_W - HEAD_W), BF16)

    def chunk_tables(g8):
        i_f, i_b = g8[0], g8[2]
        lf_f, lf_b = _log_sigmoid(g8[1]), _log_sigmoid(g8[3])
        cs3 = jnp.dot(_split3(jnp.concatenate([lf_f, lf_b], axis=0)), tri_u,
                      preferred_element_type=F32)
        cs = cs3[0:16] + cs3[16:32] + cs3[32:48]
        b_f = cs[0:8]
        b_b = cs[8:16, L - 1:L] - cs[8:16] + lf_b
        z_f = i_f - b_f
        z_b = i_b - b_b
        g_f = b_f[:, L - 1:L] - b_f + i_f
        g_b = b_b[:, 0:1] - b_b + i_b
        mf, mb = z_f, z_b
        s = 1
        while s < L:
            mf = jnp.maximum(mf, jnp.where(lane8 >= s, pltpu.roll(mf, s, 1), NEG_INF))
            mb = jnp.maximum(mb, jnp.where(lane8 < L - s, pltpu.roll(mb, L - s, 1), NEG_INF))
            s *= 2
        mb = jnp.where(lane8 < L - 1, pltpu.roll(mb, L - 1, 1), NEG_INF)
        reps = [lax.dot_general(_split3(t), sel3, TN_DIMS, preferred_element_type=F32)
                for t in (mf, b_f, g_f, mb, b_b, g_b)]

        def rows_of(c):
            out = jnp.zeros((8, L), F32)
            for r, val in enumerate((z_f, z_b, g_f, g_b, b_f, b_b)):
                out = jnp.where(sub8 == r, val[c:c + 1], out)
            return out

        return rows_of, reps

    lat_rows, lat_reps = chunk_tables(gt_ref[0, 0])
    for c in range(n_chunks):
        row_ref[c] = lat_rows(c)
        for t in range(N_TAB):
            tab_ref[t, c * L:(c + 1) * L, :] = lat_reps[t][:, c * 128:(c + 1) * 128]

    def lanes2(x):
        return jnp.concatenate([x, x], axis=1)

    def advance(k, v, g_rep, g_row, b_last, c_ref, m_ref):
        m = m_ref[...]
        m_new = jnp.maximum(b_last + m, jnp.max(g_row, axis=-1, keepdims=True))
        kw = (k.astype(F32) * jnp.exp(lanes2(g_rep) - m_new)).astype(BF16)
        v_aug = jnp.concatenate([v, ones_cols], axis=1)
        c_ref[...] = jnp.exp(b_last + m - m_new) * c_ref[...] + lax.dot_general(
            kw, v_aug, TN_DIMS, preferred_element_type=F32)
        m_ref[...] = m_new

    for r in (cf_ref, mf_ref, cb_ref, mb_ref):
        r[...] = jnp.zeros_like(r)
    ctx_rows, ctx_reps = chunk_tables(cgt_ref[0, 0])
    for c in range(n_ctx_chunks):
        rows = ctx_rows(c)
        advance(cks_ref[c * L:(c + 1) * L, :], cv_ref[0, c * L:(c + 1) * L, :],
                ctx_reps[2][:, c * 128:(c + 1) * 128], rows[2:3], rows[4:5, L - 1:L], cf_ref, mf_ref)
    for c in reversed(range(n_ctx_chunks)):
        rows = ctx_rows(c)
        advance(cks_ref[c * L:(c + 1) * L, :], cv_ref[0, c * L:(c + 1) * L, :],
                ctx_reps[5][:, c * 128:(c + 1) * 128], rows[3:4], rows[5:6, 0:1], cb_ref, mb_ref)

    def state_pass(i, carry):
        cb = n_chunks - 1 - i
        rf = pl.multiple_of(i * L, L)
        rb = pl.multiple_of(cb * L, L)
        cfs_ref[i] = cf_ref[...].astype(BF16)
        mfs_ref[i] = mf_ref[...]
        cbs_ref[cb] = cb_ref[...].astype(BF16)
        mbs_ref[cb] = mb_ref[...]
        rows_f = row_ref[i]
        rows_b = row_ref[cb]
        advance(ks_ref[pl.ds(rf, L), :], v_ref[0, pl.ds(rf, L), :], tab_ref[2, pl.ds(rf, L), :],
                rows_f[2:3], rows_f[4:5, L - 1:L], cf_ref, mf_ref)
        advance(ks_ref[pl.ds(rb, L), :], v_ref[0, pl.ds(rb, L), :], tab_ref[5, pl.ds(rb, L), :],
                rows_b[3:4], rows_b[5:6, 0:1], cb_ref, mb_ref)
        return carry

    lax.fori_loop(0, n_chunks - 1, state_pass, 0)
    cfs_ref[n_chunks - 1] = cf_ref[...].astype(BF16)
    mfs_ref[n_chunks - 1] = mf_ref[...]
    cbs_ref[0] = cb_ref[...].astype(BF16)
    mbs_ref[0] = mb_ref[...]

    def direction(q, v_aug, s, z_row, zmax_rep, b_rep, mask, c_in, m_in):
        mx = jnp.maximum(zmax_rep, m_in)
        att = s * jnp.exp((z_row - lanes2(mx)) + mask)
        na = jnp.dot(att.astype(BF16), v_aug, preferred_element_type=F32)
        qa = jnp.dot(q, c_in, preferred_element_type=F32)
        a = jnp.exp(m_in - mx)
        num = na[:, 0:HEAD_W] + lanes2(a) * qa[:, 0:HEAD_W]
        den = na[:, HEAD_W:] + a * qa[:, HEAD_W:]
        scale = 1.0 / jnp.maximum(jnp.abs(den), jnp.exp(-(b_rep + mx)))
        return num * lanes2(scale)

    mask_ref[0] = jnp.where(ci <= ri, 0.0, NEG_INF)
    mask_ref[1] = jnp.where(ci > ri, 0.0, NEG_INF)

    def out_chunk(c):
        r0 = pl.multiple_of(c * L, L)
        q = qs_ref[pl.ds(r0, L), :]
        k = ks_ref[pl.ds(r0, L), :]
        v_aug = jnp.concatenate([v_ref[0, pl.ds(r0, L), :], ones_cols], axis=1)
        s = lax.dot_general(q, k, NT_DIMS, preferred_element_type=F32)
        rows = row_ref[c]
        tot = direction(q, v_aug, s, rows[0:1], tab_ref[0, pl.ds(r0, L), :], tab_ref[1, pl.ds(r0, L), :],
                        mask_ref[0], cfs_ref[c], mfs_ref[c])
        tot = tot + direction(q, v_aug, s, rows[1:2], tab_ref[3, pl.ds(r0, L), :],
                              tab_ref[4, pl.ds(r0, L), :], mask_ref[1], cbs_ref[c], mbs_ref[c])
        mo = mo_ref[0, pl.ds(r0, L), :].astype(F32)
        o_ref[0, pl.ds(r0, L), :] = (_layer_norm(tot) * jax.nn.sigmoid(mo)).astype(BF16)

    def out_pass(i, carry):
        out_chunk(2 * i)
        out_chunk(2 * i + 1)
        return carry

    lax.fori_loop(0, n_chunks // 2, out_pass, 0)


def _mlstm(p_lat, p_ctx, gt, cgt, conv_w, conv_b, sec_lat, sec_ctx):
    B, T, _ = p_lat.shape
    Tc = p_ctx.shape[1]
    assert T % (2 * SCAN_L) == 0 and Tc % SCAN_L == 0 and T // SCAN_L <= 8
    n_chunks = T // SCAN_L

    def lat(sec):
        return pl.BlockSpec((1, T, HEAD_W), lambda b, h: (b, 0, sec * HEADS + h))

    def cx(sec):
        return pl.BlockSpec((1, Tc, HEAD_W), lambda b, h: (b, 0, sec * HEADS + h))

    in_specs = [lat(s) for s in sec_lat] + [cx(s) for s in sec_ctx] + [
        pl.BlockSpec((1, 1, N_GK, 8, SCAN_L), lambda b, h: (b, h, 0, 0, 0)),
        pl.BlockSpec((1, 1, N_GK, 8, SCAN_L), lambda b, h: (b, h, 0, 0, 0)),
        pl.BlockSpec((3, HEAD_W), lambda b, h: (0, h)),
        pl.BlockSpec((1, HEAD_W), lambda b, h: (0, h)),
        pl.BlockSpec((3, HEAD_W), lambda b, h: (0, HEADS + h)),
        pl.BlockSpec((1, HEAD_W), lambda b, h: (0, HEADS + h)),
    ]
    state = [pltpu.VMEM((HEAD_W, AUG_W), F32), pltpu.VMEM((1, 1), F32)]
    snaps = [pltpu.VMEM((n_chunks, HEAD_W, AUG_W), BF16), pltpu.VMEM((n_chunks, 1, 1), F32)]
    return pl.pallas_call(
        _mlstm_kernel,
        out_shape=jax.ShapeDtypeStruct((B, T, BRANCH_W), BF16),
        grid=(B, HEADS),
        in_specs=in_specs,
        out_specs=pl.BlockSpec((1, T, HEAD_W), lambda b, h: (b, 0, h)),
        scratch_shapes=[pltpu.VMEM((T + 16, HEAD_W), F32),
                        pltpu.VMEM((T, HEAD_W), BF16),
                        pltpu.VMEM((T, HEAD_W), BF16),
                        pltpu.VMEM((Tc, HEAD_W), BF16),
                        pltpu.VMEM((N_TAB, T, 128), F32),
                        pltpu.VMEM((n_chunks, 8, SCAN_L), F32)]
        + state + state + snaps + snaps + [pltpu.VMEM((2, SCAN_L, SCAN_L), F32)],
        compiler_params=_cparams("parallel", "parallel"),
        name="mlstm",
    )(p_lat, p_lat, p_lat, p_lat, p_ctx, p_ctx, gt, cgt, conv_w, conv_b, conv_w, conv_b)


def _merge_kernel(alpha, r_ref, m_ref, gr_ref, gm_ref, x_ref, g1_ref, sh2_ref, sc2_ref,
                  lng_ref, lnb_ref, wr_ref, wm_ref, wo_ref, wrt_ref, brt_ref,
                  x1_ref, ua_ref, ub_ref, lt_ref):
    tm = x_ref.shape[1]
    for s in range(tm // MERGE_SUB):
        rows = slice(s * MERGE_SUB, (s + 1) * MERGE_SUB)
        yr = jnp.dot(r_ref[0, rows, :], wr_ref[...], preferred_element_type=F32)
        ym = jnp.dot(m_ref[0, rows, :], wm_ref[...], preferred_element_type=F32)
        y = (jax.nn.sigmoid(gr_ref[0, rows, :].astype(F32)) * yr
             + jax.nn.sigmoid(gm_ref[0, rows, :].astype(F32)) * ym)
        yo = jnp.dot(y.astype(BF16), wo_ref[...], preferred_element_type=F32)
        x1 = _layer_norm(alpha * x_ref[0, rows, :] + g1_ref[0] * yo) * lng_ref[...] + lnb_ref[...]
        x1_ref[0, rows, :] = x1
        u2 = _layer_norm(x1) * (1.0 + sc2_ref[0]) + sh2_ref[0]
        lt_ref[0, :, rows] = lax.dot_general(wrt_ref[...], u2.astype(BF16), NT_DIMS,
                                             preferred_element_type=F32) + brt_ref[...]
        ua_ref[0, rows, :] = _pack_pairs(u2[:, 0:PACK_W], u2[:, PACK_W:2 * PACK_W])
        ub_ref[0, rows, :] = _pack_pairs(u2[:, 2 * PACK_W:3 * PACK_W], u2[:, 3 * PACK_W:4 * PACK_W])


def _merge(alpha, r, m, p_lat, sec_gates, x, g1, sh2, sc2, lng, lnb, wr, wm, wo, wrt, brt):
    B, T, D = x.shape
    tm = MERGE_TM

    def tile(w):
        return pl.BlockSpec((1, tm, w), lambda b, i: (b, i, 0))

    def sec(s):
        return pl.BlockSpec((1, tm, BRANCH_W), lambda b, i: (b, i, s))

    def mod():
        return pl.BlockSpec((1, 1, D), lambda b, i: (b, 0, 0))

    def const(shape):
        return pl.BlockSpec(shape, lambda b, i: (0,) * len(shape))

    return pl.pallas_call(
        functools.partial(_merge_kernel, alpha),
        out_shape=(jax.ShapeDtypeStruct((B, T, D), F32),
                   jax.ShapeDtypeStruct((B, T, PACK_W), U32),
                   jax.ShapeDtypeStruct((B, T, PACK_W), U32),
                   jax.ShapeDtypeStruct((B, ROUTE_ROWS, T), F32)),
        grid=(B, T // tm),
        in_specs=[tile(BRANCH_W), tile(BRANCH_W), sec(sec_gates[0]), sec(sec_gates[1]), tile(D),
                  mod(), mod(), mod(), const((1, D)), const((1, D)),
                  const((BRANCH_W, D)), const((BRANCH_W, D)), const((D, D)),
                  const((ROUTE_ROWS, D)), const((ROUTE_ROWS, 1))],
        out_specs=(tile(D), tile(PACK_W), tile(PACK_W),
                   pl.BlockSpec((1, ROUTE_ROWS, tm), lambda b, i: (b, 0, i))),
        compiler_params=_cparams("parallel", "parallel"),
        name="merge",
    )(r, m, p_lat, p_lat, x, g1, sh2, sc2, lng, lnb, wr, wm, wo, wrt, brt)


def _route_kernel(lt_ref, ri_ref, rw_ref, cnt_ref, carry_ref, u_ref):
    i = pl.program_id(0)
    tm = lt_ref.shape[2]

    @pl.when(i == 0)
    def _():
        carry_ref[...] = jnp.zeros_like(carry_ref)
        r = lax.broadcasted_iota(I32, (tm, tm), 0)
        c = lax.broadcasted_iota(I32, (tm, tm), 1)
        u_ref[...] = (r < c).astype(BF16)

    lt = lt_ref[0]
    lg = lt[0:N_GROUPS, :]
    eg = jnp.exp(lg - jnp.max(lg, axis=0, keepdims=True))
    pg = eg / jnp.sum(eg, axis=0, keepdims=True)
    pg_top = jnp.max(pg, axis=0, keepdims=True)
    rows_g = lax.broadcasted_iota(I32, pg.shape, 0)
    g_idx = jnp.min(jnp.where(pg == pg_top, rows_g, N_GROUPS), axis=0, keepdims=True)

    le = jnp.zeros((EXP_PER_GROUP, tm), F32)
    for g in range(N_GROUPS):
        lo = 8 + g * EXP_PER_GROUP
        le = jnp.where(g_idx == g, lt[lo:lo + EXP_PER_GROUP, :], le)
    ee = jnp.exp(le - jnp.max(le, axis=0, keepdims=True))
    pe = ee / jnp.sum(ee, axis=0, keepdims=True)
    rows_e = lax.broadcasted_iota(I32, pe.shape, 0)
    v1 = jnp.max(pe, axis=0, keepdims=True)
    i1 = jnp.min(jnp.where(pe == v1, rows_e, EXP_PER_GROUP), axis=0, keepdims=True)
    pe2 = jnp.where(rows_e == i1, -1.0, pe)
    v2 = jnp.max(pe2, axis=0, keepdims=True)
    i2 = jnp.min(jnp.where(pe2 == v2, rows_e, EXP_PER_GROUP), axis=0, keepdims=True)
    den = v1 + v2
    rw_ref[0:1, :] = pg_top * v1 / den
    rw_ref[1:2, :] = pg_top * v2 / den
    e1 = g_idx * EXP_PER_GROUP + i1
    e2 = g_idx * EXP_PER_GROUP + i2

    rows_x = lax.broadcasted_iota(I32, (N_EXPERTS, tm), 0)
    oh1 = (rows_x == e1).astype(F32)
    oh2 = (rows_x == e2).astype(F32)
    both = oh1 + oh2
    before = carry_ref[:, 0:1] + jnp.dot(both.astype(BF16), u_ref[...], preferred_element_type=F32)
    ri_ref[0:1, :] = e1
    ri_ref[1:2, :] = e2
    ri_ref[2:3, :] = jnp.sum(oh1 * before, axis=0, keepdims=True).astype(I32)
    ri_ref[3:4, :] = jnp.sum(oh2 * before, axis=0, keepdims=True).astype(I32)
    carry_ref[...] = carry_ref[...] + jnp.sum(both, axis=1, keepdims=True)
    cnt_ref[...] = carry_ref[...].astype(I32)


def _route(lt):
    B, _, T = lt.shape
    tm = ROUTE_TM
    per_b = T // tm
    n = B * T
    return pl.pallas_call(
        _route_kernel,
        out_shape=(jax.ShapeDtypeStruct((4, n), I32),
                   jax.ShapeDtypeStruct((2, n), F32),
                   jax.ShapeDtypeStruct((N_EXPERTS, 128), I32)),
        grid=(n // tm,),
        in_specs=[pl.BlockSpec((1, ROUTE_ROWS, tm), lambda i: (i // per_b, 0, i % per_b))],
        out_specs=(pl.BlockSpec((4, tm), lambda i: (0, i)),
                   pl.BlockSpec((2, tm), lambda i: (0, i)),
                   pl.BlockSpec((N_EXPERTS, 128), lambda i: (0, 0))),
        scratch_shapes=[pltpu.VMEM((N_EXPERTS, 128), F32), pltpu.VMEM((tm, tm), BF16)],
        compiler_params=_cparams("arbitrary"),
        name="route",
    )(lt)


def _sc_mesh():
    return plsc.VectorSubcoreMesh(core_axis_name="c", subcore_axis_name="s")


def _sc_scatter2(rows, idx0, idx1, n_out):
    m, w = rows.shape

    @functools.partial(pl.kernel, out_type=jax.ShapeDtypeStruct((n_out, w), rows.dtype),
                       mesh=_sc_mesh(), scratch_types=[])
    def k(x_hbm, i0_hbm, i1_hbm, o_hbm):
        def body(x_vmem, i0_vmem, i1_vmem):
            pltpu.sync_copy(x_vmem, o_hbm.at[i0_vmem.at[0]])
            pltpu.sync_copy(x_vmem, o_hbm.at[i1_vmem.at[0]])

        pltpu.emit_pipeline(
            body,
            grid=(m // SC_WIN,),
            in_specs=[pl.BlockSpec((SC_WIN, w), lambda i: (i, 0)),
                      pl.BlockSpec((1, SC_WIN), lambda i: (0, i)),
                      pl.BlockSpec((1, SC_WIN), lambda i: (0, i))],
            out_specs=[],
            core_axis_name=("c", "s"),
            dimension_semantics=(pltpu.PARALLEL,),
        )(x_hbm, i0_hbm, i1_hbm)

    return k(rows, idx0.reshape(1, m), idx1.reshape(1, m))


def _sc_gather(table, idx):
    m = idx.shape[0]
    w = table.shape[1]

    @functools.partial(pl.kernel, out_type=jax.ShapeDtypeStruct((m, w), table.dtype),
                       mesh=_sc_mesh(), scratch_types=[])
    def k(t_hbm, i_hbm, o_hbm):
        def body(i_vmem, o_vmem):
            pltpu.sync_copy(t_hbm.at[i_vmem.at[0]], o_vmem)

        pltpu.emit_pipeline(
            body,
            grid=(m // SC_WIN,),
            in_specs=[pl.BlockSpec((1, SC_WIN), lambda i: (0, i))],
            out_specs=[pl.BlockSpec((SC_WIN, w), lambda i: (i, 0))],
            core_axis_name=("c", "s"),
            dimension_semantics=(pltpu.PARALLEL,),
        )(i_hbm, o_hbm)

    return k(table, idx.reshape(1, m))


def _expert_kernel(be_ref, nv_ref, xa_ref, xb_ref, w1f_ref, w3f_ref, w2f_ref, ya_ref, yb_ref,
                   w1_ref, w3_ref, w2_ref):
    j = pl.program_id(0)
    nv = nv_ref[j]

    @pl.when(jnp.logical_or(j == 0, be_ref[j] != be_ref[jnp.maximum(j - 1, 0)]))
    def _():
        w1_ref[0] = w1f_ref[0].astype(BF16)
        w3_ref[0] = w3f_ref[0].astype(BF16)
        w2_ref[0] = w2f_ref[0].astype(BF16)

    @pl.when(nv > 0)
    def _():
        for s in range(xa_ref.shape[0] // MOE_SUB):
            rows = slice(s * MOE_SUB, (s + 1) * MOE_SUB)
            valid = lax.broadcasted_iota(I32, (MOE_SUB, PACK_W), 0) + s * MOE_SUB < nv
            zero = jnp.zeros((MOE_SUB, PACK_W), U32)
            parts = _unpack_pairs(jnp.where(valid, xa_ref[rows, :], zero)) + \
                _unpack_pairs(jnp.where(valid, xb_ref[rows, :], zero))
            x = jnp.concatenate([p.astype(BF16) for p in parts], axis=1)
            h1 = jnp.dot(x, w1_ref[0], preferred_element_type=F32)
            h3 = jnp.dot(x, w3_ref[0], preferred_element_type=F32)
            y = jnp.dot((_silu(h1) * h3).astype(BF16), w2_ref[0], preferred_element_type=F32)
            ya_ref[rows, :] = _pack_pairs(y[:, 0:PACK_W], y[:, PACK_W:2 * PACK_W])
            yb_ref[rows, :] = _pack_pairs(y[:, 2 * PACK_W:3 * PACK_W], y[:, 3 * PACK_W:4 * PACK_W])

    @pl.when(nv == 0)
    def _():
        ya_ref[...] = jnp.zeros_like(ya_ref)
        yb_ref[...] = jnp.zeros_like(yb_ref)


def _experts(block_exp, n_valid, xa, xb, w1, w3, w2):
    n_slots = xa.shape[0]
    n_blocks = n_slots // MOE_BLK
    d, de = w1.shape[1], w1.shape[2]
    slot = pl.BlockSpec((MOE_BLK, PACK_W), lambda j, be, nv: (j, 0))
    grid_spec = pltpu.PrefetchScalarGridSpec(
        num_scalar_prefetch=2,
        grid=(n_blocks,),
        in_specs=[slot, slot,
                  pl.BlockSpec((1, d, de), lambda j, be, nv: (be[j], 0, 0)),
                  pl.BlockSpec((1, d, de), lambda j, be, nv: (be[j], 0, 0)),
                  pl.BlockSpec((1, de, d), lambda j, be, nv: (be[j], 0, 0))],
        out_specs=(slot, slot),
        scratch_shapes=[pltpu.VMEM((1, d, de), BF16), pltpu.VMEM((1, d, de), BF16),
                        pltpu.VMEM((1, de, d), BF16)],
    )
    return pl.pallas_call(
        _expert_kernel,
        out_shape=(jax.ShapeDtypeStruct((n_slots, PACK_W), U32),
                   jax.ShapeDtypeStruct((n_slots, PACK_W), U32)),
        grid_spec=grid_spec,
        compiler_params=_cparams("arbitrary"),
        name="experts",
    )(block_exp, n_valid, xa, xb, w1, w3, w2)


def _final_kernel(alpha, x1_ref, a0_ref, b0_ref, a1_ref, b1_ref, w_ref, g2_ref, lng_ref, lnb_ref, o_ref):
    w = w_ref[...]
    w0 = w[:, 0:1]
    w1 = w[:, 1:2]
    parts0 = _unpack_pairs(a0_ref[...]) + _unpack_pairs(b0_ref[...])
    parts1 = _unpack_pairs(a1_ref[...]) + _unpack_pairs(b1_ref[...])
    f = jnp.concatenate([w0 * p0 + w1 * p1 for p0, p1 in zip(parts0, parts1)], axis=1)
    o_ref[0] = _layer_norm(alpha * x1_ref[0] + g2_ref[0] * f) * lng_ref[...] + lnb_ref[...]


def _final(alpha, x1, ya, yb, w, g2, lng, lnb):
    B, T, D = x1.shape
    tm = MERGE_TM
    per_b = T // tm
    n_tiles = B * per_b

    def rows(k):
        return pl.BlockSpec((tm, PACK_W), lambda b, i: (k * n_tiles + b * per_b + i, 0))

    return pl.pallas_call(
        functools.partial(_final_kernel, alpha),
        out_shape=jax.ShapeDtypeStruct((B, T, D), F32),
        grid=(B, per_b),
        in_specs=[pl.BlockSpec((1, tm, D), lambda b, i: (b, i, 0)),
                  rows(0), rows(0), rows(1), rows(1),
                  pl.BlockSpec((tm, 2), lambda b, i: (b * per_b + i, 0)),
                  pl.BlockSpec((1, 1, D), lambda b, i: (b, 0, 0)),
                  pl.BlockSpec((1, D), lambda b, i: (0, 0)),
                  pl.BlockSpec((1, D), lambda b, i: (0, 0))],
        out_specs=pl.BlockSpec((1, tm, D), lambda b, i: (b, i, 0)),
        compiler_params=_cparams("parallel", "parallel"),
        name="final",
    )(x1, ya, yb, ya, yb, w, g2, lng, lnb)


def _rotary_tables(T):
    quarter = HEAD_W // 4
    freqs = ROPE_BASE ** (-jnp.arange(quarter, dtype=F32) / quarter)
    t = jnp.arange(T)
    ang_r = (t // GRID_W).astype(F32)[:, None] * freqs[None, :]
    ang_c = (t % GRID_W).astype(F32)[:, None] * freqs[None, :]
    cos = jnp.concatenate([jnp.cos(ang_r)] * 2 + [jnp.cos(ang_c)] * 2, axis=1)
    sin = jnp.concatenate([-jnp.sin(ang_r), jnp.sin(ang_r), -jnp.sin(ang_c), jnp.sin(ang_c)], axis=1)
    return cos, sin


def _per_head_gates(gt):
    B, _, T = gt.shape
    n_chunks = T // SCAN_L
    gth = gt.reshape(B, N_GK, HEADS, n_chunks, SCAN_L).transpose(0, 2, 1, 3, 4)
    return jnp.pad(gth, ((0, 0), (0, 0), (0, 0), (0, 8 - n_chunks), (0, 0)))


def _table_lookup(table, idx):
    sel = idx[..., None] == jnp.arange(table.shape[0], dtype=idx.dtype)
    return jnp.sum(jnp.where(sel, table, 0), axis=-1)


def kernel(x, c, ctx, c_ctx, w_ada, b_ada, w_in, b_mgate, ml_conv_w, ml_conv_b, ret_decay_logit, w_ret_branch, w_ml_branch, w_out, ln1_g, ln1_b, w_rg, b_rg, w_re, b_re, w_e1, w_e3, w_e2, ln2_g, ln2_b):
    B, T, D = x.shape
    depth = w_ada.shape[0]
    assert depth == 1 and D == BRANCH_W and T % min(PROJ_TM, T) == 0 and T % GRID_W == 0
    alpha = (2 * depth) ** 0.25
    n_tok = B * T

    n_rows = -(-(B + 1) // 8) * 8
    cs = jnp.zeros((n_rows, D), F32).at[:B].set(c).at[B].set(c_ctx)
    mod = _ada(cs, w_ada[0], b_ada[0][None, :])
    sh1, sc1, g1, sh2, sc2, g2 = [mod[:B, None, i * D:(i + 1) * D] for i in range(6)]
    csh1 = mod[B, 0 * D:1 * D].reshape(1, 1, D)
    csc1 = mod[B, 1 * D:2 * D].reshape(1, 1, D)

    w = w_in[0]
    sec_w = [w[:, s * BRANCH_W:(s + 1) * BRANCH_W] for s in range(8)]
    g_lo = 8 * BRANCH_W
    w_gate_t = w[:, g_lo:g_lo + N_GATES].T.astype(BF16)
    b_gate = b_mgate[0][:, None]
    sec_w += [w[:, g_lo + N_GATES:g_lo + N_GATES + D], w[:, g_lo + N_GATES + D:]]
    w_lat = jnp.concatenate(sec_w, axis=1).astype(BF16)
    w_ctx = jnp.concatenate([sec_w[1], sec_w[2], sec_w[5], sec_w[6]], axis=1).astype(BF16)
    kinds_lat = ("rot", "rot_scale") + ("plain",) * 8
    kinds_ctx = ("scale", "plain", "plain", "plain")
    p_lat, gt_lat = _proj(x, sh1, sc1, w_lat, w_gate_t, b_gate, kinds_lat, _rotary_tables(T))
    Tc = ctx.shape[1]
    p_ctx, gt_ctx = _proj(ctx.reshape(1, B * Tc, D), csh1, csc1, w_ctx, w_gate_t, b_gate, kinds_ctx)
    p_ctx = p_ctx.reshape(B, Tc, -1)
    gt_ctx = gt_ctx.reshape(N_GATES, B, Tc).transpose(1, 0, 2)

    ret = _retention(ret_decay_logit[0], p_lat, p_ctx, (0, 1, 2, 3), (0, 1))
    mls = _mlstm(p_lat, p_ctx, _per_head_gates(gt_lat), _per_head_gates(gt_ctx),
                 ml_conv_w[0], ml_conv_b[0][None, :], (4, 5, 6, 7), (2, 3))

    wrt = jnp.zeros((ROUTE_ROWS, D), F32).at[:N_GROUPS].set(w_rg[0].T).at[8:8 + N_EXPERTS].set(w_re[0].T)
    brt = jnp.zeros((ROUTE_ROWS, 1), F32).at[:N_GROUPS, 0].set(b_rg[0]).at[8:8 + N_EXPERTS, 0].set(b_re[0])
    x1, ua, ub, lt = _merge(alpha, ret, mls, p_lat, (8, 9), x, g1, sh2, sc2,
                            ln1_g[0][None, :], ln1_b[0][None, :],
                            w_ret_branch[0].astype(BF16), w_ml_branch[0].astype(BF16),
                            w_out[0].astype(BF16), wrt.astype(BF16), brt)

    ri, rw, cnt = _route(lt)

    counts = cnt[:, 0]
    padded = (counts + MOE_BLK - 1) // MOE_BLK * MOE_BLK
    pad_end = jnp.cumsum(padded)
    pad_off = pad_end - padded
    dest = _table_lookup(pad_off, ri[0:2]) + ri[2:4]
    n_blocks = (2 * n_tok) // MOE_BLK + N_EXPERTS
    n_slots = n_blocks * MOE_BLK
    block_start = jnp.arange(n_blocks, dtype=I32) * MOE_BLK
    block_exp = jnp.minimum((block_start[:, None] >= pad_end[None, :]).sum(1), N_EXPERTS - 1).astype(I32)
    n_valid = jnp.clip(_table_lookup(counts, block_exp) - (block_start - _table_lookup(pad_off, block_exp)),
                       0, MOE_BLK).astype(I32)

    xa = _sc_scatter2(ua.reshape(n_tok, PACK_W), dest[0], dest[1], n_slots)
    xb = _sc_scatter2(ub.reshape(n_tok, PACK_W), dest[0], dest[1], n_slots)
    ya, yb = _experts(block_exp, n_valid, xa, xb, w_e1[0], w_e3[0], w_e2[0])
    dflat = dest.reshape(2 * n_tok)
    ga = _sc_gather(ya, dflat)
    gb = _sc_gather(yb, dflat)
    return _final(alpha, x1, ga, gb, rw.T, g2, ln2_g[0][None, :], ln2_b[0][None, :])
```

```python
import functools

import jax
import jax.numpy as jnp
from jax import lax
from jax.experimental import pallas as pl
from jax.experimental.pallas import tpu as pltpu
from jax.experimental.pallas import tpu_sc as plsc

F32 = jnp.float32
BF16 = jnp.bfloat16
U32 = jnp.uint32
I32 = jnp.int32
HIGHEST = lax.Precision.HIGHEST

HEADS = 4
HEAD_W = 256
BRANCH_W = HEADS * HEAD_W
GRID_W = 64
ROPE_BASE = 10000.0
N_GATES = 16
N_GK = N_GATES // HEADS
N_GROUPS = 4
EXP_PER_GROUP = 8
N_EXPERTS = N_GROUPS * EXP_PER_GROUP
LN_EPS = 1e-5
NEG_INF = -1e30
KEY_SCALE = HEAD_W ** -0.5

SCAN_L = 256
PROJ_TM = 2048
PROJ_SUB = 256
MERGE_TM = 512
MERGE_SUB = 512
MOE_SUB = 256
ROUTE_TM = 512
MOE_BLK = 512
SC_WIN = 128
PACK_W = 256
ROUTE_ROWS = 64
VMEM_LIMIT = 48 * 1024 * 1024

NT_DIMS = (((1,), (1,)), ((), ()))
TN_DIMS = (((0,), (0,)), ((), ()))


def _cparams(*sem):
    return pltpu.CompilerParams(dimension_semantics=sem, vmem_limit_bytes=VMEM_LIMIT)


def _layer_norm(x):
    mu = jnp.mean(x, axis=-1, keepdims=True)
    xc = x - mu
    var = jnp.mean(xc * xc, axis=-1, keepdims=True)
    return xc * lax.rsqrt(var + LN_EPS)


def _log_sigmoid(x):
    return jnp.minimum(x, 0.0) - jnp.log1p(jnp.exp(-jnp.abs(x)))


def _silu(x):
    return x * jax.nn.sigmoid(x)


def _pack_pairs(hi, lo):
    hb = lax.bitcast_convert_type(hi.astype(BF16).astype(F32), U32)
    lb = lax.bitcast_convert_type(lo.astype(BF16).astype(F32), U32)
    return (hb & jnp.uint32(0xFFFF0000)) | (lb >> 16)


def _unpack_pairs(p):
    hi = lax.bitcast_convert_type(p & jnp.uint32(0xFFFF0000), F32)
    lo = lax.bitcast_convert_type(p << 16, F32)
    return hi, lo


def _ada_kernel(c_ref, w_ref, b_ref, o_ref):
    s = _silu(c_ref[...])
    o_ref[...] = jnp.dot(s, w_ref[...], precision=HIGHEST, preferred_element_type=F32) + b_ref[...]


def _ada(cs, w, b):
    rows, d = cs.shape
    cols = w.shape[1]
    tn = 1024
    return pl.pallas_call(
        _ada_kernel,
        out_shape=jax.ShapeDtypeStruct((rows, cols), F32),
        grid=(cols // tn,),
        in_specs=[pl.BlockSpec((rows, d), lambda j: (0, 0)),
                  pl.BlockSpec((d, tn), lambda j: (0, j)),
                  pl.BlockSpec((1, tn), lambda j: (0, j))],
        out_specs=pl.BlockSpec((rows, tn), lambda j: (0, j)),
        compiler_params=_cparams("parallel"),
        name="ada",
    )(cs, w, b)


def _proj_kernel(kinds, seq_len, x_ref, sh_ref, sc_ref, w_ref, wg_ref, bg_ref, cw_ref, cb_ref, *rest):
    if "rot" in kinds or "rot_scale" in kinds:
        cos_ref, sin_ref, o_ref, gt_ref, u_ref = rest
    else:
        o_ref, gt_ref, u_ref = rest
    j = pl.program_id(2)
    tm = x_ref.shape[1]
    sub = min(PROJ_SUB, tm)
    n_sub = tm // sub
    assert kinds[0] not in ("conv", "conv_scale") and seq_len % sub == 0

    def conv_section(kind):
        c0 = 0 if kind == "conv" else BRANCH_W
        w = cw_ref[:, c0:c0 + BRANCH_W]
        b = cb_ref[:, c0:c0 + BRANCH_W]
        sub8 = lax.broadcasted_iota(I32, (8, BRANCH_W), 0)
        zero_row = jnp.zeros((1, BRANCH_W), F32)
        accs = []

        def finish(r):
            a = accs[r]
            seq_start = (r * sub) % seq_len == 0
            seq_end = ((r + 1) * sub) % seq_len == 0
            before = zero_row if seq_start else accs[r - 1][sub - 1:sub, :]
            after = zero_row if seq_end else accs[r + 1][0:1, :]
            prev = pltpu.roll(a, 1, 0)
            prev = jnp.concatenate([jnp.where(sub8 == 0, before, prev[0:8, :]), prev[8:, :]], axis=0)
            nxt = pltpu.roll(a, sub - 1, 0)
            nxt = jnp.concatenate([nxt[:sub - 8, :], jnp.where(sub8 == 7, after, nxt[sub - 8:, :])], axis=0)
            y = _silu(prev * w[0:1, :] + a * w[1:2, :] + nxt * w[2:3, :] + b)
            if kind == "conv_scale":
                y = y * KEY_SCALE
            o_ref[0, r * sub:(r + 1) * sub, :] = y.astype(BF16)

        for r in range(n_sub):
            accs.append(jnp.dot(u_ref[r * sub:(r + 1) * sub, :], w_ref[...], preferred_element_type=F32))
            if r >= 1:
                finish(r - 1)
        finish(n_sub - 1)

    def rotary(acc, rows, scale):
        for s in range(acc.shape[1] // 128):
            a = acc[:, s * 128:(s + 1) * 128]
            half = s % 2
            cs = cos_ref[rows, half * 128:(half + 1) * 128]
            sn = sin_ref[rows, half * 128:(half + 1) * 128]
            r = a * cs + pltpu.roll(a, 64, 1) * sn
            if scale != 1.0:
                r = r * scale
            o_ref[0, rows, s * 128:(s + 1) * 128] = r.astype(BF16)

    def section(kind, first):
        if kind in ("conv", "conv_scale"):
            conv_section(kind)
            return
        for r in range(n_sub):
            rows = slice(r * sub, (r + 1) * sub)
            if first:
                u = _layer_norm(x_ref[0, rows, :]) * (1.0 + sc_ref[0]) + sh_ref[0]
                ub = u.astype(BF16)
                u_ref[rows, :] = ub
                gt_ref[0, :, rows] = lax.dot_general(wg_ref[...], ub, NT_DIMS,
                                                     preferred_element_type=F32) + bg_ref[...]
            else:
                ub = u_ref[rows, :]
            acc = jnp.dot(ub, w_ref[...], preferred_element_type=F32)
            if kind == "rot":
                rotary(acc, rows, 1.0)
            elif kind == "rot_scale":
                rotary(acc, rows, KEY_SCALE)
            elif kind == "scale":
                o_ref[0, rows, :] = (acc * KEY_SCALE).astype(BF16)
            else:
                o_ref[0, rows, :] = acc.astype(BF16)

    variants = {}
    for s, kind in enumerate(kinds):
        variants.setdefault((kind, s == 0), []).append(s)
    for (kind, first), secs in variants.items():
        cond = functools.reduce(jnp.logical_or, [j == s for s in secs])

        @pl.when(cond)
        def _(kind=kind, first=first):
            section(kind, first)


def _proj(x, sh, sc, w_main, w_gate_t, b_gate, conv_w, conv_b, kinds, seq_len, tables=None):
    B, T, D = x.shape
    n_sec = len(kinds)
    tm = min(PROJ_TM, T)
    tn = BRANCH_W
    assert tm % seq_len == 0 and T % tm == 0
    in_specs = [
        pl.BlockSpec((1, tm, D), lambda i, b, j: (b, i, 0)),
        pl.BlockSpec((1, 1, D), lambda i, b, j: (b, 0, 0)),
        pl.BlockSpec((1, 1, D), lambda i, b, j: (b, 0, 0)),
        pl.BlockSpec((D, tn), lambda i, b, j: (0, j)),
        pl.BlockSpec((N_GATES, D), lambda i, b, j: (0, 0)),
        pl.BlockSpec((N_GATES, 1), lambda i, b, j: (0, 0)),
        pl.BlockSpec(conv_w.shape, lambda i, b, j: (0, 0)),
        pl.BlockSpec(conv_b.shape, lambda i, b, j: (0, 0)),
    ]
    args = [x, sh, sc, w_main, w_gate_t, b_gate, conv_w, conv_b]
    if tables is not None:
        in_specs += [pl.BlockSpec((tm, HEAD_W), lambda i, b, j: (i, 0))] * 2
        args += list(tables)
    return pl.pallas_call(
        functools.partial(_proj_kernel, kinds, seq_len),
        out_shape=(jax.ShapeDtypeStruct((B, T, n_sec * tn), BF16),
                   jax.ShapeDtypeStruct((B, N_GATES, T), F32)),
        grid=(T // tm, B, n_sec),
        in_specs=in_specs,
        out_specs=(pl.BlockSpec((1, tm, tn), lambda i, b, j: (b, i, j)),
                   pl.BlockSpec((1, N_GATES, tm), lambda i, b, j: (b, 0, i))),
        scratch_shapes=[pltpu.VMEM((tm, D), BF16)],
        compiler_params=_cparams("parallel", "parallel", "arbitrary"),
        name="proj_lat" if tables is not None else "proj_ctx",
    )(*args)


def _ret_kernel(dl_ref, q_ref, k_ref, v_ref, rg_ref, ck_ref, cv_ref, o_ref,
                sf_ref, sb_ref, fs_ref, bs_ref, dec_ref, d_ref):
    h = pl.program_id(1)
    L = SCAN_L
    n_chunks = q_ref.shape[1] // L
    n_ctx_chunks = ck_ref.shape[1] // L
    lgf = _log_sigmoid(jnp.full((1, 1), dl_ref[0, h], F32))
    lgb = _log_sigmoid(jnp.full((1, 1), dl_ref[1, h], F32))

    ri = lax.broadcasted_iota(I32, (L, L), 0)
    ci = lax.broadcasted_iota(I32, (L, L), 1)
    rel = (ri - ci).astype(F32)
    d_ref[...] = jnp.where(rel >= 0.0, jnp.exp(jnp.maximum(rel, 0.0) * lgf),
                           jnp.exp(jnp.maximum(-rel, 0.0) * lgb))
    row = lax.broadcasted_iota(I32, (L, HEAD_W), 0).astype(F32)
    dec_ref[0] = jnp.exp((row + 1.0) * lgf)
    dec_ref[1] = jnp.exp((L - 1.0 - row) * lgf)
    dec_ref[2] = jnp.exp((L - row) * lgb)
    dec_ref[3] = jnp.exp(row * lgb)
    cdf = jnp.exp(L * lgf)
    cdb = jnp.exp(L * lgb)

    def update(s_ref, kc, vc, kd, cd):
        kdec = (kc.astype(F32) * kd).astype(BF16)
        s_ref[...] = s_ref[...] * cd + lax.dot_general(kdec, vc, TN_DIMS, preferred_element_type=F32)

    sf_ref[...] = jnp.zeros_like(sf_ref)
    sb_ref[...] = jnp.zeros_like(sb_ref)
    for c in range(n_ctx_chunks):
        update(sf_ref, ck_ref[0, c * L:(c + 1) * L, :], cv_ref[0, c * L:(c + 1) * L, :], dec_ref[1], cdf)
    for c in reversed(range(n_ctx_chunks)):
        update(sb_ref, ck_ref[0, c * L:(c + 1) * L, :], cv_ref[0, c * L:(c + 1) * L, :], dec_ref[3], cdb)

    def state_pass(i, carry):
        cb = n_chunks - 1 - i
        rf = pl.multiple_of(i * L, L)
        rb = pl.multiple_of(cb * L, L)
        fs_ref[i] = sf_ref[...].astype(BF16)
        bs_ref[cb] = sb_ref[...].astype(BF16)
        update(sf_ref, k_ref[0, pl.ds(rf, L), :], v_ref[0, pl.ds(rf, L), :], dec_ref[1], cdf)
        update(sb_ref, k_ref[0, pl.ds(rb, L), :], v_ref[0, pl.ds(rb, L), :], dec_ref[3], cdb)
        return carry

    lax.fori_loop(0, n_chunks - 1, state_pass, 0)
    fs_ref[n_chunks - 1] = sf_ref[...].astype(BF16)
    bs_ref[0] = sb_ref[...].astype(BF16)

    def out_chunk(c):
        r0 = pl.multiple_of(c * L, L)
        q = q_ref[0, pl.ds(r0, L), :]
        k = k_ref[0, pl.ds(r0, L), :]
        v = v_ref[0, pl.ds(r0, L), :]
        s = lax.dot_general(q, k, NT_DIMS, preferred_element_type=F32)
        att = (s * d_ref[...]).astype(BF16)
        o = jnp.dot(att, v, preferred_element_type=F32)
        o = o + jnp.dot(q, fs_ref[c], preferred_element_type=F32) * dec_ref[0]
        o = o + jnp.dot(q, bs_ref[c], preferred_element_type=F32) * dec_ref[2]
        rg = rg_ref[0, pl.ds(r0, L), :].astype(F32)
        o_ref[0, pl.ds(r0, L), :] = (_layer_norm(o) * _silu(rg)).astype(BF16)

    def out_pass(i, carry):
        out_chunk(2 * i)
        out_chunk(2 * i + 1)
        return carry

    lax.fori_loop(0, n_chunks // 2, out_pass, 0)


def _retention(decay_logit, p_lat, p_ctx, sec_lat, sec_ctx):
    B, T, _ = p_lat.shape
    Tc = p_ctx.shape[1]
    assert T % (2 * SCAN_L) == 0 and Tc % SCAN_L == 0
    n_chunks = T // SCAN_L

    def lat(sec):
        return pl.BlockSpec((1, T, HEAD_W), lambda b, h: (b, 0, sec * HEADS + h))

    def cx(sec):
        return pl.BlockSpec((1, Tc, HEAD_W), lambda b, h: (b, 0, sec * HEADS + h))

    return pl.pallas_call(
        _ret_kernel,
        out_shape=jax.ShapeDtypeStruct((B, T, BRANCH_W), BF16),
        grid=(B, HEADS),
        in_specs=[pl.BlockSpec(memory_space=pltpu.SMEM)]
        + [lat(s) for s in sec_lat] + [cx(s) for s in sec_ctx],
        out_specs=pl.BlockSpec((1, T, HEAD_W), lambda b, h: (b, 0, h)),
        scratch_shapes=[pltpu.VMEM((HEAD_W, HEAD_W), F32),
                        pltpu.VMEM((HEAD_W, HEAD_W), F32),
                        pltpu.VMEM((n_chunks, HEAD_W, HEAD_W), BF16),
                        pltpu.VMEM((n_chunks, HEAD_W, HEAD_W), BF16),
                        pltpu.VMEM((4, SCAN_L, HEAD_W), F32),
                        pltpu.VMEM((SCAN_L, SCAN_L), F32)],
        compiler_params=_cparams("parallel", "parallel"),
        name="retention",
    )(decay_logit, p_lat, p_lat, p_lat, p_lat, p_ctx, p_ctx)


def _mlstm_kernel_old(q_ref, k_ref, v_ref, mo_ref, ck_ref, cv_ref, g_ref, gt_ref, cg_ref, cgt_ref,
                  wq_ref, bq_ref, wk_ref, bk_ref, o_ref,
                  xf_ref, qs_ref, ks_ref, cks_ref,
                  glc_ref, csc_ref, rcc_ref, glr_ref, csr_ref, rcr_ref,
                  cf_ref, nf_ref, mf_ref, cb_ref, nb_ref, mb_ref,
                  cfs_ref, nfs_ref, mfs_ref, cbs_ref, nbs_ref, mbs_ref):
    L = SCAN_L
    T = q_ref.shape[1]
    Tc = ck_ref.shape[1]
    n_chunks = T // L
    n_ctx_chunks = Tc // L
    CV = 128

    def conv_silu(src_ref, w_ref, b_ref, dst_ref, t_len, scale):
        xf_ref[pl.ds(0, 8), :] = jnp.zeros((8, HEAD_W), F32)
        xf_ref[pl.ds(8 + t_len, 8), :] = jnp.zeros((8, HEAD_W), F32)
        xf_ref[pl.ds(8, t_len), :] = src_ref[0].astype(F32)
        w = w_ref[...]
        b = b_ref[...]

        def body(c, carry):
            r0 = pl.multiple_of(c * CV, CV)
            win = xf_ref[pl.ds(r0, CV + 16), :]
            prev = pltpu.roll(win, 1, 0)[8:8 + CV, :]
            cur = win[8:8 + CV, :]
            nxt = pltpu.roll(win, CV + 15, 0)[8:8 + CV, :]
            y = prev * w[0:1, :] + cur * w[1:2, :] + nxt * w[2:3, :] + b
            y = _silu(y)
            if scale != 1.0:
                y = y * scale
            dst_ref[pl.ds(r0, CV), :] = y.astype(BF16)
            return carry

        lax.fori_loop(0, t_len // CV, body, 0)

    conv_silu(q_ref, wq_ref, bq_ref, qs_ref, T, 1.0)
    conv_silu(k_ref, wk_ref, bk_ref, ks_ref, T, KEY_SCALE)
    conv_silu(ck_ref, wk_ref, bk_ref, cks_ref, Tc, KEY_SCALE)

    ri = lax.broadcasted_iota(I32, (L, L), 0)
    ci = lax.broadcasted_iota(I32, (L, L), 1)
    tri_l = (ci <= ri).astype(F32)
    tri_u = (ci >= ri).astype(F32)

    def gate_tables(gc, gtc):
        lane = lax.broadcasted_iota(I32, gc.shape, 1)
        gl = jnp.where(lane % 2 == 1, _log_sigmoid(gc), gc)
        sub = lax.broadcasted_iota(I32, gtc.shape, 0)
        gtl = jnp.where(sub % 2 == 1, _log_sigmoid(gtc), gtc)
        cs_col = jnp.dot(tri_l, gl, precision=HIGHEST, preferred_element_type=F32)
        rc_col = jnp.dot(tri_u, gl, precision=HIGHEST, preferred_element_type=F32)
        cs_row = jnp.dot(gtl, tri_u, precision=HIGHEST, preferred_element_type=F32)
        rc_row = jnp.dot(gtl, tri_l, precision=HIGHEST, preferred_element_type=F32)
        return gl, cs_col, rc_col, gtl, cs_row, rc_row

    def pick(tables, backward):
        gl, cs_col, rc_col, gtl, cs_row, rc_row = tables
        if not backward:
            return gl[:, 0:1], gtl[0:1, :], cs_col[:, 1:2], cs_row[1:2, :], cs_row[1:2, L - 1:L]
        return gl[:, 2:3], gtl[2:3, :], rc_col[:, 3:4], rc_row[3:4, :], rc_row[3:4, 0:1]

    def lat_tables(r0):
        return (glc_ref[pl.ds(r0, L), :], csc_ref[pl.ds(r0, L), :], rcc_ref[pl.ds(r0, L), :],
                glr_ref[:, pl.ds(r0, L)], csr_ref[:, pl.ds(r0, L)], rcr_ref[:, pl.ds(r0, L)])

    def table_pass(c, carry):
        r0 = pl.multiple_of(c * L, L)
        gl, cs_col, rc_col, gtl, cs_row, rc_row = gate_tables(
            g_ref[0, 0, pl.ds(r0, L), :], gt_ref[0, 0, :, pl.ds(r0, L)])
        glc_ref[pl.ds(r0, L), :] = gl
        csc_ref[pl.ds(r0, L), :] = cs_col
        rcc_ref[pl.ds(r0, L), :] = rc_col
        glr_ref[:, pl.ds(r0, L)] = gtl
        csr_ref[:, pl.ds(r0, L)] = cs_row
        rcr_ref[:, pl.ds(r0, L)] = rc_row
        return carry

    lax.fori_loop(0, n_chunks, table_pass, 0)

    def advance(k, v, gate_vecs, c_ref, n_ref, m_ref):
        i_col, i_row, b_col, b_row, b_last = gate_vecs
        m = m_ref[...]
        g_col = b_last - b_col + i_col
        g_row = b_last - b_row + i_row
        m_new = jnp.maximum(b_last + m, jnp.max(g_row, axis=-1, keepdims=True))
        kw = k.astype(F32) * jnp.exp(g_col - m_new)
        decay = jnp.exp(b_last + m - m_new)
        c_ref[...] = decay * c_ref[...] + lax.dot_general(kw.astype(BF16), v, TN_DIMS,
                                                          preferred_element_type=F32)
        n_ref[...] = decay * n_ref[...] + jnp.sum(kw, axis=0, keepdims=True)
        m_ref[...] = m_new

    for refs in ((cf_ref, nf_ref, mf_ref), (cb_ref, nb_ref, mb_ref)):
        for r in refs:
            r[...] = jnp.zeros_like(r)
    ctx_tabs = [gate_tables(cg_ref[0, 0, c * L:(c + 1) * L, :], cgt_ref[0, 0, :, c * L:(c + 1) * L])
                for c in range(n_ctx_chunks)]
    for c in range(n_ctx_chunks):
        advance(cks_ref[c * L:(c + 1) * L, :], cv_ref[0, c * L:(c + 1) * L, :],
                pick(ctx_tabs[c], False), cf_ref, nf_ref, mf_ref)
    for c in reversed(range(n_ctx_chunks)):
        advance(cks_ref[c * L:(c + 1) * L, :], cv_ref[0, c * L:(c + 1) * L, :],
                pick(ctx_tabs[c], True), cb_ref, nb_ref, mb_ref)

    def snapshot(c, src, dst):
        dst[0][c] = src[0][...].astype(BF16)
        dst[1][c] = src[1][...]
        dst[2][c] = src[2][...]

    fwd_run, fwd_snap = (cf_ref, nf_ref, mf_ref), (cfs_ref, nfs_ref, mfs_ref)
    bwd_run, bwd_snap = (cb_ref, nb_ref, mb_ref), (cbs_ref, nbs_ref, mbs_ref)

    def state_pass(i, carry):
        cb = n_chunks - 1 - i
        rf = pl.multiple_of(i * L, L)
        rb = pl.multiple_of(cb * L, L)
        snapshot(i, fwd_run, fwd_snap)
        snapshot(cb, bwd_run, bwd_snap)
        advance(ks_ref[pl.ds(rf, L), :], v_ref[0, pl.ds(rf, L), :], pick(lat_tables(rf), False), *fwd_run)
        advance(ks_ref[pl.ds(rb, L), :], v_ref[0, pl.ds(rb, L), :], pick(lat_tables(rb), True), *bwd_run)
        return carry

    lax.fori_loop(0, n_chunks - 1, state_pass, 0)
    snapshot(n_chunks - 1, fwd_run, fwd_snap)
    snapshot(0, bwd_run, bwd_snap)

    def direction(q, v, s, gate_vecs, mask, c_in, n_in, m_in):
        _, i_row, b_col, b_row, _ = gate_vecs
        d = jnp.where(mask, b_col - b_row + i_row, NEG_INF)
        inter = b_col + m_in
        m_row = jnp.maximum(jnp.max(d, axis=-1, keepdims=True), inter)
        a = jnp.exp(inter - m_row)
        att = s * jnp.exp(d - m_row)
        num = jnp.dot(att.astype(BF16), v, preferred_element_type=F32)
        num = num + a * jnp.dot(q, c_in, preferred_element_type=F32)
        qn = jnp.sum(q.astype(F32) * n_in, axis=-1, keepdims=True)
        den = jnp.sum(att, axis=-1, keepdims=True) + a * qn
        return num * (1.0 / jnp.maximum(jnp.abs(den), jnp.exp(-m_row)))

    def out_pass(c, carry):
        r0 = pl.multiple_of(c * L, L)
        q = qs_ref[pl.ds(r0, L), :]
        k = ks_ref[pl.ds(r0, L), :]
        v = v_ref[0, pl.ds(r0, L), :]
        s = lax.dot_general(q, k, NT_DIMS, preferred_element_type=F32)
        tabs = lat_tables(r0)
        tot = direction(q, v, s, pick(tabs, False), ci <= ri, cfs_ref[c], nfs_ref[c], mfs_ref[c])
        tot = tot + direction(q, v, s, pick(tabs, True), ci > ri, cbs_ref[c], nbs_ref[c], mbs_ref[c])
        mo = mo_ref[0, pl.ds(r0, L), :].astype(F32)
        o_ref[0, pl.ds(r0, L), :] = (_layer_norm(tot) * jax.nn.sigmoid(mo)).astype(BF16)
        return carry

    lax.fori_loop(0, n_chunks, out_pass, 0)


def _mlstm_old(p_lat, p_ctx, g, gt, cg, cgt, conv_w, conv_b, sec_lat, sec_ctx):
    B, T, _ = p_lat.shape
    Tc = p_ctx.shape[1]
    assert T % SCAN_L == 0 and Tc % SCAN_L == 0
    n_chunks = T // SCAN_L

    def lat(sec):
        return pl.BlockSpec((1, T, HEAD_W), lambda b, h: (b, 0, sec * HEADS + h))

    def cx(sec):
        return pl.BlockSpec((1, Tc, HEAD_W), lambda b, h: (b, 0, sec * HEADS + h))

    in_specs = [lat(s) for s in sec_lat] + [cx(s) for s in sec_ctx] + [
        pl.BlockSpec((1, 1, T, N_GK), lambda b, h: (b, h, 0, 0)),
        pl.BlockSpec((1, 1, N_GK, T), lambda b, h: (b, h, 0, 0)),
        pl.BlockSpec((1, 1, Tc, N_GK), lambda b, h: (b, h, 0, 0)),
        pl.BlockSpec((1, 1, N_GK, Tc), lambda b, h: (b, h, 0, 0)),
        pl.BlockSpec((3, HEAD_W), lambda b, h: (0, h)),
        pl.BlockSpec((1, HEAD_W), lambda b, h: (0, h)),
        pl.BlockSpec((3, HEAD_W), lambda b, h: (0, HEADS + h)),
        pl.BlockSpec((1, HEAD_W), lambda b, h: (0, HEADS + h)),
    ]
    state = [pltpu.VMEM((HEAD_W, HEAD_W), F32), pltpu.VMEM((1, HEAD_W), F32), pltpu.VMEM((1, 1), F32)]
    snaps = [pltpu.VMEM((n_chunks, HEAD_W, HEAD_W), BF16), pltpu.VMEM((n_chunks, 1, HEAD_W), F32),
             pltpu.VMEM((n_chunks, 1, 1), F32)]
    return pl.pallas_call(
        _mlstm_kernel,
        out_shape=jax.ShapeDtypeStruct((B, T, BRANCH_W), BF16),
        grid=(B, HEADS),
        in_specs=in_specs,
        out_specs=pl.BlockSpec((1, T, HEAD_W), lambda b, h: (b, 0, h)),
        scratch_shapes=[pltpu.VMEM((T + 16, HEAD_W), F32),
                        pltpu.VMEM((T, HEAD_W), BF16),
                        pltpu.VMEM((T, HEAD_W), BF16),
                        pltpu.VMEM((Tc, HEAD_W), BF16)]
        + [pltpu.VMEM((T, N_GK), F32)] * 3 + [pltpu.VMEM((N_GK, T), F32)] * 3
        + state + state + snaps + snaps,
        compiler_params=_cparams("parallel", "parallel"),
        name="mlstm",
    )(p_lat, p_lat, p_lat, p_lat, p_ctx, p_ctx, g, gt, cg, cgt, conv_w, conv_b, conv_w, conv_b)


N_TAB = 6
AUG_W = HEAD_W + 128


def _split3(x):
    hi = x.astype(BF16).astype(F32)
    r1 = x - hi
    mid = r1.astype(BF16).astype(F32)
    lo = (r1 - mid).astype(BF16).astype(F32)
    return jnp.concatenate([hi, mid, lo], axis=0).astype(BF16)


def _mlstm_kernel(q_ref, k_ref, v_ref, mo_ref, ck_ref, cv_ref, gt_ref, cgt_ref, o_ref,
                  tab_ref, row_ref,
                  cf_ref, mf_ref, cb_ref, mb_ref, cfs_ref, mfs_ref, cbs_ref, mbs_ref, mask_ref):
    L = SCAN_L
    T = q_ref.shape[1]
    Tc = ck_ref.shape[1]
    n_chunks = T // L
    n_ctx_chunks = Tc // L

    ri = lax.broadcasted_iota(I32, (L, L), 0)
    ci = lax.broadcasted_iota(I32, (L, L), 1)
    tri_u = (ri <= ci).astype(BF16)
    lane8 = lax.broadcasted_iota(I32, (8, L), 1)
    sub8 = lax.broadcasted_iota(I32, (8, L), 0)
    sel_r = lax.broadcasted_iota(I32, (24, 8 * 128), 0) % 8
    sel_c = lax.broadcasted_iota(I32, (24, 8 * 128), 1) // 128
    sel3 = (sel_r == sel_c).astype(BF16)
    ones_cols = jnp.ones((L, AUG_W - HEAD_W), BF16)

    def chunk_tables(g8):
        i_f, i_b = g8[0], g8[2]
        lf_f, lf_b = _log_sigmoid(g8[1]), _log_sigmoid(g8[3])
        cs3 = jnp.dot(_split3(jnp.concatenate([lf_f, lf_b], axis=0)), tri_u,
                      preferred_element_type=F32)
        cs = cs3[0:16] + cs3[16:32] + cs3[32:48]
        b_f = cs[0:8]
        b_b = cs[8:16, L - 1:L] - cs[8:16] + lf_b
        z_f = i_f - b_f
        z_b = i_b - b_b
        g_f = b_f[:, L - 1:L] - b_f + i_f
        g_b = b_b[:, 0:1] - b_b + i_b
        mf, mb = z_f, z_b
        s = 1
        while s < L:
            mf = jnp.maximum(mf, jnp.where(lane8 >= s, pltpu.roll(mf, s, 1), NEG_INF))
            mb = jnp.maximum(mb, jnp.where(lane8 < L - s, pltpu.roll(mb, L - s, 1), NEG_INF))
            s *= 2
        mb = jnp.where(lane8 < L - 1, pltpu.roll(mb, L - 1, 1), NEG_INF)
        reps = [lax.dot_general(_split3(t), sel3, TN_DIMS, preferred_element_type=F32)
                for t in (mf, b_f, g_f, mb, b_b, g_b)]

        def rows_of(c):
            out = jnp.zeros((8, L), F32)
            for r, val in enumerate((z_f, z_b, g_f, g_b, b_f, b_b)):
                out = jnp.where(sub8 == r, val[c:c + 1], out)
            return out

        return rows_of, reps

    lat_rows, lat_reps = chunk_tables(gt_ref[0, 0])
    for c in range(n_chunks):
        row_ref[c] = lat_rows(c)
        for t in range(N_TAB):
            tab_ref[t, c * L:(c + 1) * L, :] = lat_reps[t][:, c * 128:(c + 1) * 128]

    def lanes2(x):
        return jnp.concatenate([x, x], axis=1)

    def advance(k, v, g_rep, g_row, b_last, c_ref, m_ref):
        m = m_ref[...]
        m_new = jnp.maximum(b_last + m, jnp.max(g_row, axis=-1, keepdims=True))
        kw = (k.astype(F32) * jnp.exp(lanes2(g_rep) - m_new)).astype(BF16)
        v_aug = jnp.concatenate([v, ones_cols], axis=1)
        c_ref[...] = jnp.exp(b_last + m - m_new) * c_ref[...] + lax.dot_general(
            kw, v_aug, TN_DIMS, preferred_element_type=F32)
        m_ref[...] = m_new

    for r in (cf_ref, mf_ref, cb_ref, mb_ref):
        r[...] = jnp.zeros_like(r)
    ctx_rows, ctx_reps = chunk_tables(cgt_ref[0, 0])
    for c in range(n_ctx_chunks):
        rows = ctx_rows(c)
        advance(ck_ref[0, c * L:(c + 1) * L, :], cv_ref[0, c * L:(c + 1) * L, :],
                ctx_reps[2][:, c * 128:(c + 1) * 128], rows[2:3], rows[4:5, L - 1:L], cf_ref, mf_ref)
    for c in reversed(range(n_ctx_chunks)):
        rows = ctx_rows(c)
        advance(ck_ref[0, c * L:(c + 1) * L, :], cv_ref[0, c * L:(c + 1) * L, :],
                ctx_reps[5][:, c * 128:(c + 1) * 128], rows[3:4], rows[5:6, 0:1], cb_ref, mb_ref)

    def state_pass(i, carry):
        cb = n_chunks - 1 - i
        rf = pl.multiple_of(i * L, L)
        rb = pl.multiple_of(cb * L, L)
        cfs_ref[i] = cf_ref[...].astype(BF16)
        mfs_ref[i] = mf_ref[...]
        cbs_ref[cb] = cb_ref[...].astype(BF16)
        mbs_ref[cb] = mb_ref[...]
        rows_f = row_ref[i]
        rows_b = row_ref[cb]
        advance(k_ref[0, pl.ds(rf, L), :], v_ref[0, pl.ds(rf, L), :], tab_ref[2, pl.ds(rf, L), :],
                rows_f[2:3], rows_f[4:5, L - 1:L], cf_ref, mf_ref)
        advance(k_ref[0, pl.ds(rb, L), :], v_ref[0, pl.ds(rb, L), :], tab_ref[5, pl.ds(rb, L), :],
                rows_b[3:4], rows_b[5:6, 0:1], cb_ref, mb_ref)
        return carry

    lax.fori_loop(0, n_chunks - 1, state_pass, 0)
    cfs_ref[n_chunks - 1] = cf_ref[...].astype(BF16)
    mfs_ref[n_chunks - 1] = mf_ref[...]
    cbs_ref[0] = cb_ref[...].astype(BF16)
    mbs_ref[0] = mb_ref[...]

    def direction(q, v_aug, s, z_row, zmax_rep, b_rep, mask, c_in, m_in):
        mx = jnp.maximum(zmax_rep, m_in)
        att = s * jnp.exp((z_row - lanes2(mx)) + mask)
        na = jnp.dot(att.astype(BF16), v_aug, preferred_element_type=F32)
        qa = jnp.dot(q, c_in, preferred_element_type=F32)
        a = jnp.exp(m_in - mx)
        num = na[:, 0:HEAD_W] + lanes2(a) * qa[:, 0:HEAD_W]
        den = na[:, HEAD_W:] + a * qa[:, HEAD_W:]
        scale = 1.0 / jnp.maximum(jnp.abs(den), jnp.exp(-(b_rep + mx)))
        return num * lanes2(scale)

    mask_ref[0] = jnp.where(ci <= ri, 0.0, NEG_INF)
    mask_ref[1] = jnp.where(ci > ri, 0.0, NEG_INF)

    def out_chunk(c):
        r0 = pl.multiple_of(c * L, L)
        q = q_ref[0, pl.ds(r0, L), :]
        k = k_ref[0, pl.ds(r0, L), :]
        v_aug = jnp.concatenate([v_ref[0, pl.ds(r0, L), :], ones_cols], axis=1)
        s = lax.dot_general(q, k, NT_DIMS, preferred_element_type=F32)
        rows = row_ref[c]
        tot = direction(q, v_aug, s, rows[0:1], tab_ref[0, pl.ds(r0, L), :], tab_ref[1, pl.ds(r0, L), :],
                        mask_ref[0], cfs_ref[c], mfs_ref[c])
        tot = tot + direction(q, v_aug, s, rows[1:2], tab_ref[3, pl.ds(r0, L), :],
                              tab_ref[4, pl.ds(r0, L), :], mask_ref[1], cbs_ref[c], mbs_ref[c])
        mo = mo_ref[0, pl.ds(r0, L), :].astype(F32)
        o_ref[0, pl.ds(r0, L), :] = (_layer_norm(tot) * jax.nn.sigmoid(mo)).astype(BF16)

    def out_pass(i, carry):
        out_chunk(2 * i)
        out_chunk(2 * i + 1)
        return carry

    lax.fori_loop(0, n_chunks // 2, out_pass, 0)


def _mlstm(p_lat, p_ctx, gt, cgt, sec_lat, sec_ctx):
    B, T, _ = p_lat.shape
    Tc = p_ctx.shape[1]
    assert T % (2 * SCAN_L) == 0 and Tc % SCAN_L == 0 and T // SCAN_L <= 8
    n_chunks = T // SCAN_L

    def lat(sec):
        return pl.BlockSpec((1, T, HEAD_W), lambda b, h: (b, 0, sec * HEADS + h))

    def cx(sec):
        return pl.BlockSpec((1, Tc, HEAD_W), lambda b, h: (b, 0, sec * HEADS + h))

    in_specs = [lat(s) for s in sec_lat] + [cx(s) for s in sec_ctx] + [
        pl.BlockSpec((1, 1, N_GK, 8, SCAN_L), lambda b, h: (b, h, 0, 0, 0)),
        pl.BlockSpec((1, 1, N_GK, 8, SCAN_L), lambda b, h: (b, h, 0, 0, 0)),
    ]
    state = [pltpu.VMEM((HEAD_W, AUG_W), F32), pltpu.VMEM((1, 1), F32)]
    snaps = [pltpu.VMEM((n_chunks, HEAD_W, AUG_W), BF16), pltpu.VMEM((n_chunks, 1, 1), F32)]
    return pl.pallas_call(
        _mlstm_kernel,
        out_shape=jax.ShapeDtypeStruct((B, T, BRANCH_W), BF16),
        grid=(B, HEADS),
        in_specs=in_specs,
        out_specs=pl.BlockSpec((1, T, HEAD_W), lambda b, h: (b, 0, h)),
        scratch_shapes=[pltpu.VMEM((N_TAB, T, 128), F32),
                        pltpu.VMEM((n_chunks, 8, SCAN_L), F32)]
        + state + state + snaps + snaps + [pltpu.VMEM((2, SCAN_L, SCAN_L), F32)],
        compiler_params=_cparams("parallel", "parallel"),
        name="mlstm",
    )(p_lat, p_lat, p_lat, p_lat, p_ctx, p_ctx, gt, cgt)


def _merge_kernel(alpha, r_ref, m_ref, gr_ref, gm_ref, x_ref, g1_ref, sh2_ref, sc2_ref,
                  lng_ref, lnb_ref, wr_ref, wm_ref, wo_ref, wrt_ref, brt_ref,
                  x1_ref, ua_ref, ub_ref, lt_ref):
    tm = x_ref.shape[1]
    for s in range(tm // MERGE_SUB):
        rows = slice(s * MERGE_SUB, (s + 1) * MERGE_SUB)
        yr = jnp.dot(r_ref[0, rows, :], wr_ref[...], preferred_element_type=F32)
        ym = jnp.dot(m_ref[0, rows, :], wm_ref[...], preferred_element_type=F32)
        y = (jax.nn.sigmoid(gr_ref[0, rows, :].astype(F32)) * yr
             + jax.nn.sigmoid(gm_ref[0, rows, :].astype(F32)) * ym)
        yo = jnp.dot(y.astype(BF16), wo_ref[...], preferred_element_type=F32)
        x1 = _layer_norm(alpha * x_ref[0, rows, :] + g1_ref[0] * yo) * lng_ref[...] + lnb_ref[...]
        x1_ref[0, rows, :] = x1
        u2 = _layer_norm(x1) * (1.0 + sc2_ref[0]) + sh2_ref[0]
        lt_ref[0, :, rows] = lax.dot_general(wrt_ref[...], u2.astype(BF16), NT_DIMS,
                                             preferred_element_type=F32) + brt_ref[...]
        ua_ref[0, rows, :] = _pack_pairs(u2[:, 0:PACK_W], u2[:, PACK_W:2 * PACK_W])
        ub_ref[0, rows, :] = _pack_pairs(u2[:, 2 * PACK_W:3 * PACK_W], u2[:, 3 * PACK_W:4 * PACK_W])


def _merge(alpha, r, m, p_lat, sec_gates, x, g1, sh2, sc2, lng, lnb, wr, wm, wo, wrt, brt):
    B, T, D = x.shape
    tm = MERGE_TM

    def tile(w):
        return pl.BlockSpec((1, tm, w), lambda b, i: (b, i, 0))

    def sec(s):
        return pl.BlockSpec((1, tm, BRANCH_W), lambda b, i: (b, i, s))

    def mod():
        return pl.BlockSpec((1, 1, D), lambda b, i: (b, 0, 0))

    def const(shape):
        return pl.BlockSpec(shape, lambda b, i: (0,) * len(shape))

    return pl.pallas_call(
        functools.partial(_merge_kernel, alpha),
        out_shape=(jax.ShapeDtypeStruct((B, T, D), F32),
                   jax.ShapeDtypeStruct((B, T, PACK_W), U32),
                   jax.ShapeDtypeStruct((B, T, PACK_W), U32),
                   jax.ShapeDtypeStruct((B, ROUTE_ROWS, T), F32)),
        grid=(B, T // tm),
        in_specs=[tile(BRANCH_W), tile(BRANCH_W), sec(sec_gates[0]), sec(sec_gates[1]), tile(D),
                  mod(), mod(), mod(), const((1, D)), const((1, D)),
                  const((BRANCH_W, D)), const((BRANCH_W, D)), const((D, D)),
                  const((ROUTE_ROWS, D)), const((ROUTE_ROWS, 1))],
        out_specs=(tile(D), tile(PACK_W), tile(PACK_W),
                   pl.BlockSpec((1, ROUTE_ROWS, tm), lambda b, i: (b, 0, i))),
        compiler_params=_cparams("parallel", "parallel"),
        name="merge",
    )(r, m, p_lat, p_lat, x, g1, sh2, sc2, lng, lnb, wr, wm, wo, wrt, brt)


def _route_kernel(lt_ref, ri_ref, rw_ref, cnt_ref, carry_ref, u_ref):
    i = pl.program_id(0)
    tm = lt_ref.shape[2]

    @pl.when(i == 0)
    def _():
        carry_ref[...] = jnp.zeros_like(carry_ref)
        r = lax.broadcasted_iota(I32, (tm, tm), 0)
        c = lax.broadcasted_iota(I32, (tm, tm), 1)
        u_ref[...] = (r < c).astype(BF16)

    lt = lt_ref[0]
    lg = lt[0:N_GROUPS, :]
    eg = jnp.exp(lg - jnp.max(lg, axis=0, keepdims=True))
    pg = eg / jnp.sum(eg, axis=0, keepdims=True)
    pg_top = jnp.max(pg, axis=0, keepdims=True)
    rows_g = lax.broadcasted_iota(I32, pg.shape, 0)
    g_idx = jnp.min(jnp.where(pg == pg_top, rows_g, N_GROUPS), axis=0, keepdims=True)

    le = jnp.zeros((EXP_PER_GROUP, tm), F32)
    for g in range(N_GROUPS):
        lo = 8 + g * EXP_PER_GROUP
        le = jnp.where(g_idx == g, lt[lo:lo + EXP_PER_GROUP, :], le)
    ee = jnp.exp(le - jnp.max(le, axis=0, keepdims=True))
    pe = ee / jnp.sum(ee, axis=0, keepdims=True)
    rows_e = lax.broadcasted_iota(I32, pe.shape, 0)
    v1 = jnp.max(pe, axis=0, keepdims=True)
    i1 = jnp.min(jnp.where(pe == v1, rows_e, EXP_PER_GROUP), axis=0, keepdims=True)
    pe2 = jnp.where(rows_e == i1, -1.0, pe)
    v2 = jnp.max(pe2, axis=0, keepdims=True)
    i2 = jnp.min(jnp.where(pe2 == v2, rows_e, EXP_PER_GROUP), axis=0, keepdims=True)
    den = v1 + v2
    rw_ref[0:1, :] = pg_top * v1 / den
    rw_ref[1:2, :] = pg_top * v2 / den
    e1 = g_idx * EXP_PER_GROUP + i1
    e2 = g_idx * EXP_PER_GROUP + i2

    rows_x = lax.broadcasted_iota(I32, (N_EXPERTS, tm), 0)
    oh1 = (rows_x == e1).astype(F32)
    oh2 = (rows_x == e2).astype(F32)
    both = oh1 + oh2
    before = carry_ref[:, 0:1] + jnp.dot(both.astype(BF16), u_ref[...], preferred_element_type=F32)
    ri_ref[0:1, :] = e1
    ri_ref[1:2, :] = e2
    ri_ref[2:3, :] = jnp.sum(oh1 * before, axis=0, keepdims=True).astype(I32)
    ri_ref[3:4, :] = jnp.sum(oh2 * before, axis=0, keepdims=True).astype(I32)
    carry_ref[...] = carry_ref[...] + jnp.sum(both, axis=1, keepdims=True)
    cnt_ref[...] = carry_ref[...].astype(I32)


def _route(lt):
    B, _, T = lt.shape
    tm = ROUTE_TM
    per_b = T // tm
    n = B * T
    return pl.pallas_call(
        _route_kernel,
        out_shape=(jax.ShapeDtypeStruct((4, n), I32),
                   jax.ShapeDtypeStruct((2, n), F32),
                   jax.ShapeDtypeStruct((N_EXPERTS, 128), I32)),
        grid=(n // tm,),
        in_specs=[pl.BlockSpec((1, ROUTE_ROWS, tm), lambda i: (i // per_b, 0, i % per_b))],
        out_specs=(pl.BlockSpec((4, tm), lambda i: (0, i)),
                   pl.BlockSpec((2, tm), lambda i: (0, i)),
                   pl.BlockSpec((N_EXPERTS, 128), lambda i: (0, 0))),
        scratch_shapes=[pltpu.VMEM((N_EXPERTS, 128), F32), pltpu.VMEM((tm, tm), BF16)],
        compiler_params=_cparams("arbitrary"),
        name="route",
    )(lt)


def _sc_mesh():
    return plsc.VectorSubcoreMesh(core_axis_name="c", subcore_axis_name="s")


def _sc_scatter2(rows, idx0, idx1, n_out):
    m, w = rows.shape

    @functools.partial(pl.kernel, out_type=jax.ShapeDtypeStruct((n_out, w), rows.dtype),
                       mesh=_sc_mesh(), scratch_types=[])
    def k(x_hbm, i0_hbm, i1_hbm, o_hbm):
        def body(x_vmem, i0_vmem, i1_vmem):
            pltpu.sync_copy(x_vmem, o_hbm.at[i0_vmem.at[0]])
            pltpu.sync_copy(x_vmem, o_hbm.at[i1_vmem.at[0]])

        pltpu.emit_pipeline(
            body,
            grid=(m // SC_WIN,),
            in_specs=[pl.BlockSpec((SC_WIN, w), lambda i: (i, 0)),
                      pl.BlockSpec((1, SC_WIN), lambda i: (0, i)),
                      pl.BlockSpec((1, SC_WIN), lambda i: (0, i))],
            out_specs=[],
            core_axis_name=("c", "s"),
            dimension_semantics=(pltpu.PARALLEL,),
        )(x_hbm, i0_hbm, i1_hbm)

    return k(rows, idx0.reshape(1, m), idx1.reshape(1, m))


def _sc_gather(table, idx):
    m = idx.shape[0]
    w = table.shape[1]

    @functools.partial(pl.kernel, out_type=jax.ShapeDtypeStruct((m, w), table.dtype),
                       mesh=_sc_mesh(), scratch_types=[])
    def k(t_hbm, i_hbm, o_hbm):
        def body(i_vmem, o_vmem):
            pltpu.sync_copy(t_hbm.at[i_vmem.at[0]], o_vmem)

        pltpu.emit_pipeline(
            body,
            grid=(m // SC_WIN,),
            in_specs=[pl.BlockSpec((1, SC_WIN), lambda i: (0, i))],
            out_specs=[pl.BlockSpec((SC_WIN, w), lambda i: (i, 0))],
            core_axis_name=("c", "s"),
            dimension_semantics=(pltpu.PARALLEL,),
        )(i_hbm, o_hbm)

    return k(table, idx.reshape(1, m))


def _expert_kernel(be_ref, nv_ref, xa_ref, xb_ref, w1f_ref, w3f_ref, w2f_ref, ya_ref, yb_ref,
                   w1_ref, w3_ref, w2_ref):
    j = pl.program_id(0)
    nv = nv_ref[j]

    @pl.when(jnp.logical_or(j == 0, be_ref[j] != be_ref[jnp.maximum(j - 1, 0)]))
    def _():
        w1_ref[0] = w1f_ref[0].astype(BF16)
        w3_ref[0] = w3f_ref[0].astype(BF16)
        w2_ref[0] = w2f_ref[0].astype(BF16)

    @pl.when(nv > 0)
    def _():
        for s in range(xa_ref.shape[0] // MOE_SUB):
            rows = slice(s * MOE_SUB, (s + 1) * MOE_SUB)
            valid = lax.broadcasted_iota(I32, (MOE_SUB, PACK_W), 0) + s * MOE_SUB < nv
            zero = jnp.zeros((MOE_SUB, PACK_W), U32)
            parts = _unpack_pairs(jnp.where(valid, xa_ref[rows, :], zero)) + \
                _unpack_pairs(jnp.where(valid, xb_ref[rows, :], zero))
            x = jnp.concatenate([p.astype(BF16) for p in parts], axis=1)
            h1 = jnp.dot(x, w1_ref[0], preferred_element_type=F32)
            h3 = jnp.dot(x, w3_ref[0], preferred_element_type=F32)
            y = jnp.dot((_silu(h1) * h3).astype(BF16), w2_ref[0], preferred_element_type=F32)
            ya_ref[rows, :] = _pack_pairs(y[:, 0:PACK_W], y[:, PACK_W:2 * PACK_W])
            yb_ref[rows, :] = _pack_pairs(y[:, 2 * PACK_W:3 * PACK_W], y[:, 3 * PACK_W:4 * PACK_W])

    @pl.when(nv == 0)
    def _():
        ya_ref[...] = jnp.zeros_like(ya_ref)
        yb_ref[...] = jnp.zeros_like(yb_ref)


def _experts(block_exp, n_valid, xa, xb, w1, w3, w2):
    n_slots = xa.shape[0]
    n_blocks = n_slots // MOE_BLK
    d, de = w1.shape[1], w1.shape[2]
    slot = pl.BlockSpec((MOE_BLK, PACK_W), lambda j, be, nv: (j, 0))
    grid_spec = pltpu.PrefetchScalarGridSpec(
        num_scalar_prefetch=2,
        grid=(n_blocks,),
        in_specs=[slot, slot,
                  pl.BlockSpec((1, d, de), lambda j, be, nv: (be[j], 0, 0)),
                  pl.BlockSpec((1, d, de), lambda j, be, nv: (be[j], 0, 0)),
                  pl.BlockSpec((1, de, d), lambda j, be, nv: (be[j], 0, 0))],
        out_specs=(slot, slot),
        scratch_shapes=[pltpu.VMEM((1, d, de), BF16), pltpu.VMEM((1, d, de), BF16),
                        pltpu.VMEM((1, de, d), BF16)],
    )
    return pl.pallas_call(
        _expert_kernel,
        out_shape=(jax.ShapeDtypeStruct((n_slots, PACK_W), U32),
                   jax.ShapeDtypeStruct((n_slots, PACK_W), U32)),
        grid_spec=grid_spec,
        compiler_params=_cparams("arbitrary"),
        name="experts",
    )(block_exp, n_valid, xa, xb, w1, w3, w2)


def _final_kernel(alpha, x1_ref, a0_ref, b0_ref, a1_ref, b1_ref, w_ref, g2_ref, lng_ref, lnb_ref, o_ref):
    w = w_ref[...]
    w0 = w[:, 0:1]
    w1 = w[:, 1:2]
    parts0 = _unpack_pairs(a0_ref[...]) + _unpack_pairs(b0_ref[...])
    parts1 = _unpack_pairs(a1_ref[...]) + _unpack_pairs(b1_ref[...])
    f = jnp.concatenate([w0 * p0 + w1 * p1 for p0, p1 in zip(parts0, parts1)], axis=1)
    o_ref[0] = _layer_norm(alpha * x1_ref[0] + g2_ref[0] * f) * lng_ref[...] + lnb_ref[...]


def _final(alpha, x1, ya, yb, w, g2, lng, lnb):
    B, T, D = x1.shape
    tm = MERGE_TM
    per_b = T // tm
    n_tiles = B * per_b

    def rows(k):
        return pl.BlockSpec((tm, PACK_W), lambda b, i: (k * n_tiles + b * per_b + i, 0))

    return pl.pallas_call(
        functools.partial(_final_kernel, alpha),
        out_shape=jax.ShapeDtypeStruct((B, T, D), F32),
        grid=(B, per_b),
        in_specs=[pl.BlockSpec((1, tm, D), lambda b, i: (b, i, 0)),
                  rows(0), rows(0), rows(1), rows(1),
                  pl.BlockSpec((tm, 2), lambda b, i: (b * per_b + i, 0)),
                  pl.BlockSpec((1, 1, D), lambda b, i: (b, 0, 0)),
                  pl.BlockSpec((1, D), lambda b, i: (0, 0)),
                  pl.BlockSpec((1, D), lambda b, i: (0, 0))],
        out_specs=pl.BlockSpec((1, tm, D), lambda b, i: (b, i, 0)),
        compiler_params=_cparams("parallel", "parallel"),
        name="final",
    )(x1, ya, yb, ya, yb, w, g2, lng, lnb)


def _rotary_tables(T):
    quarter = HEAD_W // 4
    freqs = ROPE_BASE ** (-jnp.arange(quarter, dtype=F32) / quarter)
    t = jnp.arange(T)
    ang_r = (t // GRID_W).astype(F32)[:, None] * freqs[None, :]
    ang_c = (t % GRID_W).astype(F32)[:, None] * freqs[None, :]
    cos = jnp.concatenate([jnp.cos(ang_r)] * 2 + [jnp.cos(ang_c)] * 2, axis=1)
    sin = jnp.concatenate([-jnp.sin(ang_r), jnp.sin(ang_r), -jnp.sin(ang_c), jnp.sin(ang_c)], axis=1)
    return cos, sin


def _per_head_gates(gt):
    B, _, T = gt.shape
    n_chunks = T // SCAN_L
    gth = gt.reshape(B, N_GK, HEADS, n_chunks, SCAN_L).transpose(0, 2, 1, 3, 4)
    return jnp.pad(gth, ((0, 0), (0, 0), (0, 0), (0, 8 - n_chunks), (0, 0)))


def _table_lookup(table, idx):
    sel = idx[..., None] == jnp.arange(table.shape[0], dtype=idx.dtype)
    return jnp.sum(jnp.where(sel, table, 0), axis=-1)


def kernel(x, c, ctx, c_ctx, w_ada, b_ada, w_in, b_mgate, ml_conv_w, ml_conv_b, ret_decay_logit, w_ret_branch, w_ml_branch, w_out, ln1_g, ln1_b, w_rg, b_rg, w_re, b_re, w_e1, w_e3, w_e2, ln2_g, ln2_b):
    B, T, D = x.shape
    depth = w_ada.shape[0]
    assert depth == 1 and D == BRANCH_W and T % min(PROJ_TM, T) == 0 and T % GRID_W == 0
    alpha = (2 * depth) ** 0.25
    n_tok = B * T

    n_rows = -(-(B + 1) // 8) * 8
    cs = jnp.zeros((n_rows, D), F32).at[:B].set(c).at[B].set(c_ctx)
    mod = _ada(cs, w_ada[0], b_ada[0][None, :])
    sh1, sc1, g1, sh2, sc2, g2 = [mod[:B, None, i * D:(i + 1) * D] for i in range(6)]
    csh1 = mod[B, 0 * D:1 * D].reshape(1, 1, D)
    csc1 = mod[B, 1 * D:2 * D].reshape(1, 1, D)

    w = w_in[0]
    sec_w = [w[:, s * BRANCH_W:(s + 1) * BRANCH_W] for s in range(8)]
    g_lo = 8 * BRANCH_W
    w_gate_t = w[:, g_lo:g_lo + N_GATES].T.astype(BF16)
    b_gate = b_mgate[0][:, None]
    sec_w += [w[:, g_lo + N_GATES:g_lo + N_GATES + D], w[:, g_lo + N_GATES + D:]]
    w_lat = jnp.concatenate(sec_w, axis=1).astype(BF16)
    w_ctx = jnp.concatenate([sec_w[1], sec_w[2], sec_w[5], sec_w[6]], axis=1).astype(BF16)
    kinds_lat = ("rot", "rot_scale", "plain", "plain", "conv", "conv_scale") + ("plain",) * 4
    kinds_ctx = ("scale", "plain", "conv_scale", "plain")
    conv_w, conv_b = ml_conv_w[0], ml_conv_b[0][None, :]
    p_lat, gt_lat = _proj(x, sh1, sc1, w_lat, w_gate_t, b_gate, conv_w, conv_b, kinds_lat, T,
                          _rotary_tables(T))
    Tc = ctx.shape[1]
    p_ctx, gt_ctx = _proj(ctx.reshape(1, B * Tc, D), csh1, csc1, w_ctx, w_gate_t, b_gate,
                          conv_w, conv_b, kinds_ctx, Tc)
    p_ctx = p_ctx.reshape(B, Tc, -1)
    gt_ctx = gt_ctx.reshape(N_GATES, B, Tc).transpose(1, 0, 2)

    ret = _retention(ret_decay_logit[0], p_lat, p_ctx, (0, 1, 2, 3), (0, 1))
    mls = _mlstm(p_lat, p_ctx, _per_head_gates(gt_lat), _per_head_gates(gt_ctx),
                 (4, 5, 6, 7), (2, 3))

    wrt = jnp.zeros((ROUTE_ROWS, D), F32).at[:N_GROUPS].set(w_rg[0].T).at[8:8 + N_EXPERTS].set(w_re[0].T)
    brt = jnp.zeros((ROUTE_ROWS, 1), F32).at[:N_GROUPS, 0].set(b_rg[0]).at[8:8 + N_EXPERTS, 0].set(b_re[0])
    x1, ua, ub, lt = _merge(alpha, ret, mls, p_lat, (8, 9), x, g1, sh2, sc2,
                            ln1_g[0][None, :], ln1_b[0][None, :],
                            w_ret_branch[0].astype(BF16), w_ml_branch[0].astype(BF16),
                            w_out[0].astype(BF16), wrt.astype(BF16), brt)

    ri, rw, cnt = _route(lt)

    counts = cnt[:, 0]
    padded = (counts + MOE_BLK - 1) // MOE_BLK * MOE_BLK
    pad_end = jnp.cumsum(padded)
    pad_off = pad_end - padded
    dest = _table_lookup(pad_off, ri[0:2]) + ri[2:4]
    n_blocks = (2 * n_tok) // MOE_BLK + N_EXPERTS
    n_slots = n_blocks * MOE_BLK
    block_start = jnp.arange(n_blocks, dtype=I32) * MOE_BLK
    block_exp = jnp.minimum((block_start[:, None] >= pad_end[None, :]).sum(1), N_EXPERTS - 1).astype(I32)
    n_valid = jnp.clip(_table_lookup(counts, block_exp) - (block_start - _table_lookup(pad_off, block_exp)),
                       0, MOE_BLK).astype(I32)

    xa = _sc_scatter2(ua.reshape(n_tok, PACK_W), dest[0], dest[1], n_slots)
    xb = _sc_scatter2(ub.reshape(n_tok, PACK_W), dest[0], dest[1], n_slots)
    ya, yb = _experts(block_exp, n_valid, xa, xb, w_e1[0], w_e3[0], w_e2[0])
    dflat = dest.reshape(2 * n_tok)
    ga = _sc_gather(ya, dflat)
    gb = _sc_gather(yb, dflat)
    return _final(alpha, x1, ga, gb, rw.T, g2, ln2_g[0][None, :], ln2_b[0][None, :])
```

```python
import functools

import jax
import jax.numpy as jnp
from jax import lax
from jax.experimental import pallas as pl
from jax.experimental.pallas import tpu as pltpu
from jax.experimental.pallas import tpu_sc as plsc

F32 = jnp.float32
BF16 = jnp.bfloat16
U32 = jnp.uint32
I32 = jnp.int32
HIGHEST = lax.Precision.HIGHEST

HEADS = 4
HEAD_W = 256
BRANCH_W = HEADS * HEAD_W
GRID_W = 64
ROPE_BASE = 10000.0
N_GATES = 16
N_GK = N_GATES // HEADS
N_GROUPS = 4
EXP_PER_GROUP = 8
N_EXPERTS = N_GROUPS * EXP_PER_GROUP
LN_EPS = 1e-5
NEG_INF = -1e30
KEY_SCALE = HEAD_W ** -0.5

SCAN_L = 256
PROJ_TM = 2048
PROJ_SUB = 256
MERGE_TM = 512
MERGE_SUB = 512
MOE_SUB = 256
ROUTE_TM = 512
MOE_BLK = 512
MOE_SPLIT = 2
SC_WIN = 128
PACK_W = 256
ROUTE_ROWS = 64
VMEM_LIMIT = 48 * 1024 * 1024

NT_DIMS = (((1,), (1,)), ((), ()))
TN_DIMS = (((0,), (0,)), ((), ()))


def _cparams(*sem):
    return pltpu.CompilerParams(dimension_semantics=sem, vmem_limit_bytes=VMEM_LIMIT)


def _layer_norm(x):
    mu = jnp.mean(x, axis=-1, keepdims=True)
    xc = x - mu
    var = jnp.mean(xc * xc, axis=-1, keepdims=True)
    return xc * lax.rsqrt(var + LN_EPS)


def _log_sigmoid(x):
    return jnp.minimum(x, 0.0) - jnp.log1p(jnp.exp(-jnp.abs(x)))


def _silu(x):
    return x * jax.nn.sigmoid(x)


def _pack_pairs(hi, lo):
    hb = lax.bitcast_convert_type(hi.astype(BF16).astype(F32), U32)
    lb = lax.bitcast_convert_type(lo.astype(BF16).astype(F32), U32)
    return (hb & jnp.uint32(0xFFFF0000)) | (lb >> 16)


def _unpack_pairs(p):
    hi = lax.bitcast_convert_type(p & jnp.uint32(0xFFFF0000), F32)
    lo = lax.bitcast_convert_type(p << 16, F32)
    return hi, lo


def _ada_kernel(c_ref, w_ref, b_ref, o_ref):
    s = _silu(c_ref[...])
    o_ref[...] = jnp.dot(s, w_ref[...], precision=HIGHEST, preferred_element_type=F32) + b_ref[...]


def _ada(cs, w, b):
    rows, d = cs.shape
    cols = w.shape[1]
    tn = 1024
    return pl.pallas_call(
        _ada_kernel,
        out_shape=jax.ShapeDtypeStruct((rows, cols), F32),
        grid=(cols // tn,),
        in_specs=[pl.BlockSpec((rows, d), lambda j: (0, 0)),
                  pl.BlockSpec((d, tn), lambda j: (0, j)),
                  pl.BlockSpec((1, tn), lambda j: (0, j))],
        out_specs=pl.BlockSpec((rows, tn), lambda j: (0, j)),
        compiler_params=_cparams("parallel"),
        name="ada",
    )(cs, w, b)


def _proj_kernel(kinds, seq_len, x_ref, sh_ref, sc_ref, w_ref, wg_ref, bg_ref, cw_ref, cb_ref, *rest):
    if "rot" in kinds or "rot_scale" in kinds:
        cos_ref, sin_ref, o_ref, gt_ref, u_ref = rest
    else:
        o_ref, gt_ref, u_ref = rest
    j = pl.program_id(2)
    tm = x_ref.shape[1]
    sub = min(PROJ_SUB, tm)
    n_sub = tm // sub
    assert kinds[0] not in ("conv", "conv_scale") and seq_len % sub == 0

    def conv_section(kind):
        c0 = 0 if kind == "conv" else BRANCH_W
        w = cw_ref[:, c0:c0 + BRANCH_W]
        b = cb_ref[:, c0:c0 + BRANCH_W]
        sub8 = lax.broadcasted_iota(I32, (8, BRANCH_W), 0)
        zero_row = jnp.zeros((1, BRANCH_W), F32)
        accs = []

        def finish(r):
            a = accs[r]
            seq_start = (r * sub) % seq_len == 0
            seq_end = ((r + 1) * sub) % seq_len == 0
            before = zero_row if seq_start else accs[r - 1][sub - 1:sub, :]
            after = zero_row if seq_end else accs[r + 1][0:1, :]
            prev = pltpu.roll(a, 1, 0)
            prev = jnp.concatenate([jnp.where(sub8 == 0, before, prev[0:8, :]), prev[8:, :]], axis=0)
            nxt = pltpu.roll(a, sub - 1, 0)
            nxt = jnp.concatenate([nxt[:sub - 8, :], jnp.where(sub8 == 7, after, nxt[sub - 8:, :])], axis=0)
            y = _silu(prev * w[0:1, :] + a * w[1:2, :] + nxt * w[2:3, :] + b)
            if kind == "conv_scale":
                y = y * KEY_SCALE
            o_ref[0, r * sub:(r + 1) * sub, :] = y.astype(BF16)

        for r in range(n_sub):
            accs.append(jnp.dot(u_ref[r * sub:(r + 1) * sub, :], w_ref[...], preferred_element_type=F32))
            if r >= 1:
                finish(r - 1)
        finish(n_sub - 1)

    def rotary(acc, rows, scale):
        for s in range(acc.shape[1] // 128):
            a = acc[:, s * 128:(s + 1) * 128]
            half = s % 2
            cs = cos_ref[rows, half * 128:(half + 1) * 128]
            sn = sin_ref[rows, half * 128:(half + 1) * 128]
            r = a * cs + pltpu.roll(a, 64, 1) * sn
            if scale != 1.0:
                r = r * scale
            o_ref[0, rows, s * 128:(s + 1) * 128] = r.astype(BF16)

    def section(kind, first):
        if kind in ("conv", "conv_scale"):
            conv_section(kind)
            return
        for r in range(n_sub):
            rows = slice(r * sub, (r + 1) * sub)
            if first:
                u = _layer_norm(x_ref[0, rows, :]) * (1.0 + sc_ref[0]) + sh_ref[0]
                ub = u.astype(BF16)
                u_ref[rows, :] = ub
                gt_ref[0, :, rows] = lax.dot_general(wg_ref[...], ub, NT_DIMS,
                                                     preferred_element_type=F32) + bg_ref[...]
            else:
                ub = u_ref[rows, :]
            acc = jnp.dot(ub, w_ref[...], preferred_element_type=F32)
            if kind == "rot":
                rotary(acc, rows, 1.0)
            elif kind == "rot_scale":
                rotary(acc, rows, KEY_SCALE)
            elif kind == "scale":
                o_ref[0, rows, :] = (acc * KEY_SCALE).astype(BF16)
            else:
                o_ref[0, rows, :] = acc.astype(BF16)

    variants = {}
    for s, kind in enumerate(kinds):
        variants.setdefault((kind, s == 0), []).append(s)
    for (kind, first), secs in variants.items():
        cond = functools.reduce(jnp.logical_or, [j == s for s in secs])

        @pl.when(cond)
        def _(kind=kind, first=first):
            section(kind, first)


def _proj(x, sh, sc, w_main, w_gate_t, b_gate, conv_w, conv_b, kinds, seq_len, tables=None):
    B, T, D = x.shape
    n_sec = len(kinds)
    tm = min(PROJ_TM, T)
    tn = BRANCH_W
    assert tm % seq_len == 0 and T % tm == 0
    in_specs = [
        pl.BlockSpec((1, tm, D), lambda i, b, j: (b, i, 0)),
        pl.BlockSpec((1, 1, D), lambda i, b, j: (b, 0, 0)),
        pl.BlockSpec((1, 1, D), lambda i, b, j: (b, 0, 0)),
        pl.BlockSpec((D, tn), lambda i, b, j: (0, j)),
        pl.BlockSpec((N_GATES, D), lambda i, b, j: (0, 0)),
        pl.BlockSpec((N_GATES, 1), lambda i, b, j: (0, 0)),
        pl.BlockSpec(conv_w.shape, lambda i, b, j: (0, 0)),
        pl.BlockSpec(conv_b.shape, lambda i, b, j: (0, 0)),
    ]
    args = [x, sh, sc, w_main, w_gate_t, b_gate, conv_w, conv_b]
    if tables is not None:
        in_specs += [pl.BlockSpec((tm, HEAD_W), lambda i, b, j: (i, 0))] * 2
        args += list(tables)
    return pl.pallas_call(
        functools.partial(_proj_kernel, kinds, seq_len),
        out_shape=(jax.ShapeDtypeStruct((B, T, n_sec * tn), BF16),
                   jax.ShapeDtypeStruct((B, N_GATES, T), F32)),
        grid=(T // tm, B, n_sec),
        in_specs=in_specs,
        out_specs=(pl.BlockSpec((1, tm, tn), lambda i, b, j: (b, i, j)),
                   pl.BlockSpec((1, N_GATES, tm), lambda i, b, j: (b, 0, i))),
        scratch_shapes=[pltpu.VMEM((tm, D), BF16)],
        compiler_params=_cparams("parallel", "parallel", "arbitrary"),
        name="proj_lat" if tables is not None else "proj_ctx",
    )(*args)


def _ret_kernel(dl_ref, q_ref, k_ref, v_ref, rg_ref, ck_ref, cv_ref, o_ref,
                sf_ref, sb_ref, fs_ref, bs_ref, dec_ref, d_ref):
    h = pl.program_id(1)
    L = SCAN_L
    n_chunks = q_ref.shape[1] // L
    n_ctx_chunks = ck_ref.shape[1] // L
    lgf = _log_sigmoid(jnp.full((1, 1), dl_ref[0, h], F32))
    lgb = _log_sigmoid(jnp.full((1, 1), dl_ref[1, h], F32))

    ri = lax.broadcasted_iota(I32, (L, L), 0)
    ci = lax.broadcasted_iota(I32, (L, L), 1)
    rel = (ri - ci).astype(F32)
    d_ref[...] = jnp.where(rel >= 0.0, jnp.exp(jnp.maximum(rel, 0.0) * lgf),
                           jnp.exp(jnp.maximum(-rel, 0.0) * lgb))
    row = lax.broadcasted_iota(I32, (L, HEAD_W), 0).astype(F32)
    dec_ref[0] = jnp.exp((row + 1.0) * lgf)
    dec_ref[1] = jnp.exp((L - 1.0 - row) * lgf)
    dec_ref[2] = jnp.exp((L - row) * lgb)
    dec_ref[3] = jnp.exp(row * lgb)
    cdf = jnp.exp(L * lgf)
    cdb = jnp.exp(L * lgb)

    def update(s_ref, kc, vc, kd, cd):
        kdec = (kc.astype(F32) * kd).astype(BF16)
        s_ref[...] = s_ref[...] * cd + lax.dot_general(kdec, vc, TN_DIMS, preferred_element_type=F32)

    sf_ref[...] = jnp.zeros_like(sf_ref)
    sb_ref[...] = jnp.zeros_like(sb_ref)
    for c in range(n_ctx_chunks):
        update(sf_ref, ck_ref[0, c * L:(c + 1) * L, :], cv_ref[0, c * L:(c + 1) * L, :], dec_ref[1], cdf)
    for c in reversed(range(n_ctx_chunks)):
        update(sb_ref, ck_ref[0, c * L:(c + 1) * L, :], cv_ref[0, c * L:(c + 1) * L, :], dec_ref[3], cdb)

    def state_pass(i, carry):
        cb = n_chunks - 1 - i
        rf = pl.multiple_of(i * L, L)
        rb = pl.multiple_of(cb * L, L)
        fs_ref[i] = sf_ref[...].astype(BF16)
        bs_ref[cb] = sb_ref[...].astype(BF16)
        update(sf_ref, k_ref[0, pl.ds(rf, L), :], v_ref[0, pl.ds(rf, L), :], dec_ref[1], cdf)
        update(sb_ref, k_ref[0, pl.ds(rb, L), :], v_ref[0, pl.ds(rb, L), :], dec_ref[3], cdb)
        return carry

    lax.fori_loop(0, n_chunks - 1, state_pass, 0)
    fs_ref[n_chunks - 1] = sf_ref[...].astype(BF16)
    bs_ref[0] = sb_ref[...].astype(BF16)

    def out_chunk(c):
        r0 = pl.multiple_of(c * L, L)
        q = q_ref[0, pl.ds(r0, L), :]
        k = k_ref[0, pl.ds(r0, L), :]
        v = v_ref[0, pl.ds(r0, L), :]
        s = lax.dot_general(q, k, NT_DIMS, preferred_element_type=F32)
        att = (s * d_ref[...]).astype(BF16)
        o = jnp.dot(att, v, preferred_element_type=F32)
        o = o + jnp.dot(q, fs_ref[c], preferred_element_type=F32) * dec_ref[0]
        o = o + jnp.dot(q, bs_ref[c], preferred_element_type=F32) * dec_ref[2]
        rg = rg_ref[0, pl.ds(r0, L), :].astype(F32)
        o_ref[0, pl.ds(r0, L), :] = (_layer_norm(o) * _silu(rg)).astype(BF16)

    def out_pass(i, carry):
        out_chunk(2 * i)
        out_chunk(2 * i + 1)
        return carry

    lax.fori_loop(0, n_chunks // 2, out_pass, 0)


def _retention(decay_logit, p_lat, p_ctx, sec_lat, sec_ctx):
    B, T, _ = p_lat.shape
    Tc = p_ctx.shape[1]
    assert T % (2 * SCAN_L) == 0 and Tc % SCAN_L == 0
    n_chunks = T // SCAN_L

    def lat(sec):
        return pl.BlockSpec((1, T, HEAD_W), lambda b, h: (b, 0, sec * HEADS + h))

    def cx(sec):
        return pl.BlockSpec((1, Tc, HEAD_W), lambda b, h: (b, 0, sec * HEADS + h))

    return pl.pallas_call(
        _ret_kernel,
        out_shape=jax.ShapeDtypeStruct((B, T, BRANCH_W), BF16),
        grid=(B, HEADS),
        in_specs=[pl.BlockSpec(memory_space=pltpu.SMEM)]
        + [lat(s) for s in sec_lat] + [cx(s) for s in sec_ctx],
        out_specs=pl.BlockSpec((1, T, HEAD_W), lambda b, h: (b, 0, h)),
        scratch_shapes=[pltpu.VMEM((HEAD_W, HEAD_W), F32),
                        pltpu.VMEM((HEAD_W, HEAD_W), F32),
                        pltpu.VMEM((n_chunks, HEAD_W, HEAD_W), BF16),
                        pltpu.VMEM((n_chunks, HEAD_W, HEAD_W), BF16),
                        pltpu.VMEM((4, SCAN_L, HEAD_W), F32),
                        pltpu.VMEM((SCAN_L, SCAN_L), F32)],
        compiler_params=_cparams("parallel", "parallel"),
        name="retention",
    )(decay_logit, p_lat, p_lat, p_lat, p_lat, p_ctx, p_ctx)


def _mlstm_kernel_old(q_ref, k_ref, v_ref, mo_ref, ck_ref, cv_ref, g_ref, gt_ref, cg_ref, cgt_ref,
                  wq_ref, bq_ref, wk_ref, bk_ref, o_ref,
                  xf_ref, qs_ref, ks_ref, cks_ref,
                  glc_ref, csc_ref, rcc_ref, glr_ref, csr_ref, rcr_ref,
                  cf_ref, nf_ref, mf_ref, cb_ref, nb_ref, mb_ref,
                  cfs_ref, nfs_ref, mfs_ref, cbs_ref, nbs_ref, mbs_ref):
    L = SCAN_L
    T = q_ref.shape[1]
    Tc = ck_ref.shape[1]
    n_chunks = T // L
    n_ctx_chunks = Tc // L
    CV = 128

    def conv_silu(src_ref, w_ref, b_ref, dst_ref, t_len, scale):
        xf_ref[pl.ds(0, 8), :] = jnp.zeros((8, HEAD_W), F32)
        xf_ref[pl.ds(8 + t_len, 8), :] = jnp.zeros((8, HEAD_W), F32)
        xf_ref[pl.ds(8, t_len), :] = src_ref[0].astype(F32)
        w = w_ref[...]
        b = b_ref[...]

        def body(c, carry):
            r0 = pl.multiple_of(c * CV, CV)
            win = xf_ref[pl.ds(r0, CV + 16), :]
            prev = pltpu.roll(win, 1, 0)[8:8 + CV, :]
            cur = win[8:8 + CV, :]
            nxt = pltpu.roll(win, CV + 15, 0)[8:8 + CV, :]
            y = prev * w[0:1, :] + cur * w[1:2, :] + nxt * w[2:3, :] + b
            y = _silu(y)
            if scale != 1.0:
                y = y * scale
            dst_ref[pl.ds(r0, CV), :] = y.astype(BF16)
            return carry

        lax.fori_loop(0, t_len // CV, body, 0)

    conv_silu(q_ref, wq_ref, bq_ref, qs_ref, T, 1.0)
    conv_silu(k_ref, wk_ref, bk_ref, ks_ref, T, KEY_SCALE)
    conv_silu(ck_ref, wk_ref, bk_ref, cks_ref, Tc, KEY_SCALE)

    ri = lax.broadcasted_iota(I32, (L, L), 0)
    ci = lax.broadcasted_iota(I32, (L, L), 1)
    tri_l = (ci <= ri).astype(F32)
    tri_u = (ci >= ri).astype(F32)

    def gate_tables(gc, gtc):
        lane = lax.broadcasted_iota(I32, gc.shape, 1)
        gl = jnp.where(lane % 2 == 1, _log_sigmoid(gc), gc)
        sub = lax.broadcasted_iota(I32, gtc.shape, 0)
        gtl = jnp.where(sub % 2 == 1, _log_sigmoid(gtc), gtc)
        cs_col = jnp.dot(tri_l, gl, precision=HIGHEST, preferred_element_type=F32)
        rc_col = jnp.dot(tri_u, gl, precision=HIGHEST, preferred_element_type=F32)
        cs_row = jnp.dot(gtl, tri_u, precision=HIGHEST, preferred_element_type=F32)
        rc_row = jnp.dot(gtl, tri_l, precision=HIGHEST, preferred_element_type=F32)
        return gl, cs_col, rc_col, gtl, cs_row, rc_row

    def pick(tables, backward):
        gl, cs_col, rc_col, gtl, cs_row, rc_row = tables
        if not backward:
            return gl[:, 0:1], gtl[0:1, :], cs_col[:, 1:2], cs_row[1:2, :], cs_row[1:2, L - 1:L]
        return gl[:, 2:3], gtl[2:3, :], rc_col[:, 3:4], rc_row[3:4, :], rc_row[3:4, 0:1]

    def lat_tables(r0):
        return (glc_ref[pl.ds(r0, L), :], csc_ref[pl.ds(r0, L), :], rcc_ref[pl.ds(r0, L), :],
                glr_ref[:, pl.ds(r0, L)], csr_ref[:, pl.ds(r0, L)], rcr_ref[:, pl.ds(r0, L)])

    def table_pass(c, carry):
        r0 = pl.multiple_of(c * L, L)
        gl, cs_col, rc_col, gtl, cs_row, rc_row = gate_tables(
            g_ref[0, 0, pl.ds(r0, L), :], gt_ref[0, 0, :, pl.ds(r0, L)])
        glc_ref[pl.ds(r0, L), :] = gl
        csc_ref[pl.ds(r0, L), :] = cs_col
        rcc_ref[pl.ds(r0, L), :] = rc_col
        glr_ref[:, pl.ds(r0, L)] = gtl
        csr_ref[:, pl.ds(r0, L)] = cs_row
        rcr_ref[:, pl.ds(r0, L)] = rc_row
        return carry

    lax.fori_loop(0, n_chunks, table_pass, 0)

    def advance(k, v, gate_vecs, c_ref, n_ref, m_ref):
        i_col, i_row, b_col, b_row, b_last = gate_vecs
        m = m_ref[...]
        g_col = b_last - b_col + i_col
        g_row = b_last - b_row + i_row
        m_new = jnp.maximum(b_last + m, jnp.max(g_row, axis=-1, keepdims=True))
        kw = k.astype(F32) * jnp.exp(g_col - m_new)
        decay = jnp.exp(b_last + m - m_new)
        c_ref[...] = decay * c_ref[...] + lax.dot_general(kw.astype(BF16), v, TN_DIMS,
                                                          preferred_element_type=F32)
        n_ref[...] = decay * n_ref[...] + jnp.sum(kw, axis=0, keepdims=True)
        m_ref[...] = m_new

    for refs in ((cf_ref, nf_ref, mf_ref), (cb_ref, nb_ref, mb_ref)):
        for r in refs:
            r[...] = jnp.zeros_like(r)
    ctx_tabs = [gate_tables(cg_ref[0, 0, c * L:(c + 1) * L, :], cgt_ref[0, 0, :, c * L:(c + 1) * L])
                for c in range(n_ctx_chunks)]
    for c in range(n_ctx_chunks):
        advance(cks_ref[c * L:(c + 1) * L, :], cv_ref[0, c * L:(c + 1) * L, :],
                pick(ctx_tabs[c], False), cf_ref, nf_ref, mf_ref)
    for c in reversed(range(n_ctx_chunks)):
        advance(cks_ref[c * L:(c + 1) * L, :], cv_ref[0, c * L:(c + 1) * L, :],
                pick(ctx_tabs[c], True), cb_ref, nb_ref, mb_ref)

    def snapshot(c, src, dst):
        dst[0][c] = src[0][...].astype(BF16)
        dst[1][c] = src[1][...]
        dst[2][c] = src[2][...]

    fwd_run, fwd_snap = (cf_ref, nf_ref, mf_ref), (cfs_ref, nfs_ref, mfs_ref)
    bwd_run, bwd_snap = (cb_ref, nb_ref, mb_ref), (cbs_ref, nbs_ref, mbs_ref)

    def state_pass(i, carry):
        cb = n_chunks - 1 - i
        rf = pl.multiple_of(i * L, L)
        rb = pl.multiple_of(cb * L, L)
        snapshot(i, fwd_run, fwd_snap)
        snapshot(cb, bwd_run, bwd_snap)
        advance(ks_ref[pl.ds(rf, L), :], v_ref[0, pl.ds(rf, L), :], pick(lat_tables(rf), False), *fwd_run)
        advance(ks_ref[pl.ds(rb, L), :], v_ref[0, pl.ds(rb, L), :], pick(lat_tables(rb), True), *bwd_run)
        return carry

    lax.fori_loop(0, n_chunks - 1, state_pass, 0)
    snapshot(n_chunks - 1, fwd_run, fwd_snap)
    snapshot(0, bwd_run, bwd_snap)

    def direction(q, v, s, gate_vecs, mask, c_in, n_in, m_in):
        _, i_row, b_col, b_row, _ = gate_vecs
        d = jnp.where(mask, b_col - b_row + i_row, NEG_INF)
        inter = b_col + m_in
        m_row = jnp.maximum(jnp.max(d, axis=-1, keepdims=True), inter)
        a = jnp.exp(inter - m_row)
        att = s * jnp.exp(d - m_row)
        num = jnp.dot(att.astype(BF16), v, preferred_element_type=F32)
        num = num + a * jnp.dot(q, c_in, preferred_element_type=F32)
        qn = jnp.sum(q.astype(F32) * n_in, axis=-1, keepdims=True)
        den = jnp.sum(att, axis=-1, keepdims=True) + a * qn
        return num * (1.0 / jnp.maximum(jnp.abs(den), jnp.exp(-m_row)))

    def out_pass(c, carry):
        r0 = pl.multiple_of(c * L, L)
        q = qs_ref[pl.ds(r0, L), :]
        k = ks_ref[pl.ds(r0, L), :]
        v = v_ref[0, pl.ds(r0, L), :]
        s = lax.dot_general(q, k, NT_DIMS, preferred_element_type=F32)
        tabs = lat_tables(r0)
        tot = direction(q, v, s, pick(tabs, False), ci <= ri, cfs_ref[c], nfs_ref[c], mfs_ref[c])
        tot = tot + direction(q, v, s, pick(tabs, True), ci > ri, cbs_ref[c], nbs_ref[c], mbs_ref[c])
        mo = mo_ref[0, pl.ds(r0, L), :].astype(F32)
        o_ref[0, pl.ds(r0, L), :] = (_layer_norm(tot) * jax.nn.sigmoid(mo)).astype(BF16)
        return carry

    lax.fori_loop(0, n_chunks, out_pass, 0)


def _mlstm_old(p_lat, p_ctx, g, gt, cg, cgt, conv_w, conv_b, sec_lat, sec_ctx):
    B, T, _ = p_lat.shape
    Tc = p_ctx.shape[1]
    assert T % SCAN_L == 0 and Tc % SCAN_L == 0
    n_chunks = T // SCAN_L

    def lat(sec):
        return pl.BlockSpec((1, T, HEAD_W), lambda b, h: (b, 0, sec * HEADS + h))

    def cx(sec):
        return pl.BlockSpec((1, Tc, HEAD_W), lambda b, h: (b, 0, sec * HEADS + h))

    in_specs = [lat(s) for s in sec_lat] + [cx(s) for s in sec_ctx] + [
        pl.BlockSpec((1, 1, T, N_GK), lambda b, h: (b, h, 0, 0)),
        pl.BlockSpec((1, 1, N_GK, T), lambda b, h: (b, h, 0, 0)),
        pl.BlockSpec((1, 1, Tc, N_GK), lambda b, h: (b, h, 0, 0)),
        pl.BlockSpec((1, 1, N_GK, Tc), lambda b, h: (b, h, 0, 0)),
        pl.BlockSpec((3, HEAD_W), lambda b, h: (0, h)),
        pl.BlockSpec((1, HEAD_W), lambda b, h: (0, h)),
        pl.BlockSpec((3, HEAD_W), lambda b, h: (0, HEADS + h)),
        pl.BlockSpec((1, HEAD_W), lambda b, h: (0, HEADS + h)),
    ]
    state = [pltpu.VMEM((HEAD_W, HEAD_W), F32), pltpu.VMEM((1, HEAD_W), F32), pltpu.VMEM((1, 1), F32)]
    snaps = [pltpu.VMEM((n_chunks, HEAD_W, HEAD_W), BF16), pltpu.VMEM((n_chunks, 1, HEAD_W), F32),
             pltpu.VMEM((n_chunks, 1, 1), F32)]
    return pl.pallas_call(
        _mlstm_kernel,
        out_shape=jax.ShapeDtypeStruct((B, T, BRANCH_W), BF16),
        grid=(B, HEADS),
        in_specs=in_specs,
        out_specs=pl.BlockSpec((1, T, HEAD_W), lambda b, h: (b, 0, h)),
        scratch_shapes=[pltpu.VMEM((T + 16, HEAD_W), F32),
                        pltpu.VMEM((T, HEAD_W), BF16),
                        pltpu.VMEM((T, HEAD_W), BF16),
                        pltpu.VMEM((Tc, HEAD_W), BF16)]
        + [pltpu.VMEM((T, N_GK), F32)] * 3 + [pltpu.VMEM((N_GK, T), F32)] * 3
        + state + state + snaps + snaps,
        compiler_params=_cparams("parallel", "parallel"),
        name="mlstm",
    )(p_lat, p_lat, p_lat, p_lat, p_ctx, p_ctx, g, gt, cg, cgt, conv_w, conv_b, conv_w, conv_b)


N_TAB = 6
AUG_W = HEAD_W + 128


def _split3(x):
    hi = x.astype(BF16).astype(F32)
    r1 = x - hi
    mid = r1.astype(BF16).astype(F32)
    lo = (r1 - mid).astype(BF16).astype(F32)
    return jnp.concatenate([hi, mid, lo], axis=0).astype(BF16)


def _mlstm_kernel(q_ref, k_ref, v_ref, mo_ref, ck_ref, cv_ref, gt_ref, cgt_ref, o_ref,
                  tab_ref, row_ref,
                  cf_ref, mf_ref, cb_ref, mb_ref, cfs_ref, mfs_ref, cbs_ref, mbs_ref, mask_ref):
    L = SCAN_L
    T = q_ref.shape[1]
    Tc = ck_ref.shape[1]
    n_chunks = T // L
    n_ctx_chunks = Tc // L

    ri = lax.broadcasted_iota(I32, (L, L), 0)
    ci = lax.broadcasted_iota(I32, (L, L), 1)
    tri_u = (ri <= ci).astype(BF16)
    lane8 = lax.broadcasted_iota(I32, (8, L), 1)
    sub8 = lax.broadcasted_iota(I32, (8, L), 0)
    sel_r = lax.broadcasted_iota(I32, (24, 8 * 128), 0) % 8
    sel_c = lax.broadcasted_iota(I32, (24, 8 * 128), 1) // 128
    sel3 = (sel_r == sel_c).astype(BF16)
    ones_cols = jnp.ones((L, AUG_W - HEAD_W), BF16)

    def chunk_tables(g8, n_used, state_only):
        i_f, i_b = g8[0], g8[2]
        lf_f, lf_b = _log_sigmoid(g8[1]), _log_sigmoid(g8[3])
        cs3 = jnp.dot(_split3(jnp.concatenate([lf_f, lf_b], axis=0)), tri_u,
                      preferred_element_type=F32)
        cs = cs3[0:16] + cs3[16:32] + cs3[32:48]
        b_f = cs[0:8]
        b_b = cs[8:16, L - 1:L] - cs[8:16] + lf_b
        z_f = i_f - b_f
        z_b = i_b - b_b
        g_f = b_f[:, L - 1:L] - b_f + i_f
        g_b = b_b[:, 0:1] - b_b + i_b
        mf, mb = z_f, z_b
        s = 1
        while s < L:
            mf = jnp.maximum(mf, jnp.where(lane8 >= s, pltpu.roll(mf, s, 1), NEG_INF))
            mb = jnp.maximum(mb, jnp.where(lane8 < L - s, pltpu.roll(mb, L - s, 1), NEG_INF))
            s *= 2
        mb = jnp.where(lane8 < L - 1, pltpu.roll(mb, L - 1, 1), NEG_INF)
        reps = [None if state_only and t not in (2, 5) else
                lax.dot_general(_split3(val), sel3[:, 0:n_used * 128], TN_DIMS, preferred_element_type=F32)
                for t, val in enumerate((mf, b_f, g_f, mb, b_b, g_b))]

        def rows_of(c):
            out = jnp.zeros((8, L), F32)
            for r, val in enumerate((z_f, z_b, g_f, g_b, b_f, b_b)):
                out = jnp.where(sub8 == r, val[c:c + 1], out)
            return out

        return rows_of, reps

    lat_rows, lat_reps = chunk_tables(gt_ref[0, 0], n_chunks, False)
    for c in range(n_chunks):
        row_ref[c] = lat_rows(c)
        for t in range(N_TAB):
            tab_ref[t, c * L:(c + 1) * L, :] = lat_reps[t][:, c * 128:(c + 1) * 128]

    def lanes2(x):
        return jnp.concatenate([x, x], axis=1)

    def advance(k, v, g_rep, g_row, b_last, c_ref, m_ref):
        m = m_ref[...]
        m_new = jnp.maximum(b_last + m, jnp.max(g_row, axis=-1, keepdims=True))
        kw = (k.astype(F32) * jnp.exp(lanes2(g_rep) - m_new)).astype(BF16)
        v_aug = jnp.concatenate([v, ones_cols], axis=1)
        c_ref[...] = jnp.exp(b_last + m - m_new) * c_ref[...] + lax.dot_general(
            kw, v_aug, TN_DIMS, preferred_element_type=F32)
        m_ref[...] = m_new

    for r in (cf_ref, mf_ref, cb_ref, mb_ref):
        r[...] = jnp.zeros_like(r)
    ctx_rows, ctx_reps = chunk_tables(cgt_ref[0, 0], n_ctx_chunks, True)
    for c in range(n_ctx_chunks):
        rows = ctx_rows(c)
        advance(ck_ref[0, c * L:(c + 1) * L, :], cv_ref[0, c * L:(c + 1) * L, :],
                ctx_reps[2][:, c * 128:(c + 1) * 128], rows[2:3], rows[4:5, L - 1:L], cf_ref, mf_ref)
    for c in reversed(range(n_ctx_chunks)):
        rows = ctx_rows(c)
        advance(ck_ref[0, c * L:(c + 1) * L, :], cv_ref[0, c * L:(c + 1) * L, :],
                ctx_reps[5][:, c * 128:(c + 1) * 128], rows[3:4], rows[5:6, 0:1], cb_ref, mb_ref)

    def state_pass(i, carry):
        cb = n_chunks - 1 - i
        rf = pl.multiple_of(i * L, L)
        rb = pl.multiple_of(cb * L, L)
        cfs_ref[i] = cf_ref[...].astype(BF16)
        mfs_ref[i] = mf_ref[...]
        cbs_ref[cb] = cb_ref[...].astype(BF16)
        mbs_ref[cb] = mb_ref[...]
        rows_f = row_ref[i]
        rows_b = row_ref[cb]
        advance(k_ref[0, pl.ds(rf, L), :], v_ref[0, pl.ds(rf, L), :], tab_ref[2, pl.ds(rf, L), :],
                rows_f[2:3], rows_f[4:5, L - 1:L], cf_ref, mf_ref)
        advance(k_ref[0, pl.ds(rb, L), :], v_ref[0, pl.ds(rb, L), :], tab_ref[5, pl.ds(rb, L), :],
                rows_b[3:4], rows_b[5:6, 0:1], cb_ref, mb_ref)
        return carry

    lax.fori_loop(0, n_chunks - 1, state_pass, 0)
    cfs_ref[n_chunks - 1] = cf_ref[...].astype(BF16)
    mfs_ref[n_chunks - 1] = mf_ref[...]
    cbs_ref[0] = cb_ref[...].astype(BF16)
    mbs_ref[0] = mb_ref[...]

    def direction(q, v_aug, s, z_row, zmax_rep, b_rep, mask, c_in, m_in):
        mx = jnp.maximum(zmax_rep, m_in)
        att = s * jnp.exp((z_row - lanes2(mx)) + mask)
        na = jnp.dot(att.astype(BF16), v_aug, preferred_element_type=F32)
        qa = jnp.dot(q, c_in, preferred_element_type=F32)
        a = jnp.exp(m_in - mx)
        num = na[:, 0:HEAD_W] + lanes2(a) * qa[:, 0:HEAD_W]
        den = na[:, HEAD_W:] + a * qa[:, HEAD_W:]
        scale = 1.0 / jnp.maximum(jnp.abs(den), jnp.exp(-(b_rep + mx)))
        return num * lanes2(scale)

    mask_ref[0] = jnp.where(ci <= ri, 0.0, NEG_INF)
    mask_ref[1] = jnp.where(ci > ri, 0.0, NEG_INF)

    def out_chunk(c):
        r0 = pl.multiple_of(c * L, L)
        q = q_ref[0, pl.ds(r0, L), :]
        k = k_ref[0, pl.ds(r0, L), :]
        v_aug = jnp.concatenate([v_ref[0, pl.ds(r0, L), :], ones_cols], axis=1)
        s = lax.dot_general(q, k, NT_DIMS, preferred_element_type=F32)
        rows = row_ref[c]
        tot = direction(q, v_aug, s, rows[0:1], tab_ref[0, pl.ds(r0, L), :], tab_ref[1, pl.ds(r0, L), :],
                        mask_ref[0], cfs_ref[c], mfs_ref[c])
        tot = tot + direction(q, v_aug, s, rows[1:2], tab_ref[3, pl.ds(r0, L), :],
                              tab_ref[4, pl.ds(r0, L), :], mask_ref[1], cbs_ref[c], mbs_ref[c])
        mo = mo_ref[0, pl.ds(r0, L), :].astype(F32)
        o_ref[0, pl.ds(r0, L), :] = (_layer_norm(tot) * jax.nn.sigmoid(mo)).astype(BF16)

    def out_pass(i, carry):
        out_chunk(2 * i)
        out_chunk(2 * i + 1)
        return carry

    lax.fori_loop(0, n_chunks // 2, out_pass, 0)


def _mlstm(p_lat, p_ctx, gt, cgt, sec_lat, sec_ctx):
    B, T, _ = p_lat.shape
    Tc = p_ctx.shape[1]
    assert T % (2 * SCAN_L) == 0 and Tc % SCAN_L == 0 and T // SCAN_L <= 8
    n_chunks = T // SCAN_L

    def lat(sec):
        return pl.BlockSpec((1, T, HEAD_W), lambda b, h: (b, 0, sec * HEADS + h))

    def cx(sec):
        return pl.BlockSpec((1, Tc, HEAD_W), lambda b, h: (b, 0, sec * HEADS + h))

    in_specs = [lat(s) for s in sec_lat] + [cx(s) for s in sec_ctx] + [
        pl.BlockSpec((1, 1, N_GK, 8, SCAN_L), lambda b, h: (b, h, 0, 0, 0)),
        pl.BlockSpec((1, 1, N_GK, 8, SCAN_L), lambda b, h: (b, h, 0, 0, 0)),
    ]
    state = [pltpu.VMEM((HEAD_W, AUG_W), F32), pltpu.VMEM((1, 1), F32)]
    snaps = [pltpu.VMEM((n_chunks, HEAD_W, AUG_W), BF16), pltpu.VMEM((n_chunks, 1, 1), F32)]
    return pl.pallas_call(
        _mlstm_kernel,
        out_shape=jax.ShapeDtypeStruct((B, T, BRANCH_W), BF16),
        grid=(B, HEADS),
        in_specs=in_specs,
        out_specs=pl.BlockSpec((1, T, HEAD_W), lambda b, h: (b, 0, h)),
        scratch_shapes=[pltpu.VMEM((N_TAB, T, 128), F32),
                        pltpu.VMEM((n_chunks, 8, SCAN_L), F32)]
        + state + state + snaps + snaps + [pltpu.VMEM((2, SCAN_L, SCAN_L), F32)],
        compiler_params=_cparams("parallel", "parallel"),
        name="mlstm",
    )(p_lat, p_lat, p_lat, p_lat, p_ctx, p_ctx, gt, cgt)


def _merge_kernel(alpha, r_ref, m_ref, gr_ref, gm_ref, x_ref, g1_ref, sh2_ref, sc2_ref,
                  lng_ref, lnb_ref, wr_ref, wm_ref, wo_ref, wrt_ref, brt_ref,
                  x1_ref, ua_ref, ub_ref, lt_ref):
    tm = x_ref.shape[1]
    for s in range(tm // MERGE_SUB):
        rows = slice(s * MERGE_SUB, (s + 1) * MERGE_SUB)
        yr = jnp.dot(r_ref[0, rows, :], wr_ref[...], preferred_element_type=F32)
        ym = jnp.dot(m_ref[0, rows, :], wm_ref[...], preferred_element_type=F32)
        y = (jax.nn.sigmoid(gr_ref[0, rows, :].astype(F32)) * yr
             + jax.nn.sigmoid(gm_ref[0, rows, :].astype(F32)) * ym)
        yo = jnp.dot(y.astype(BF16), wo_ref[...], preferred_element_type=F32)
        x1 = _layer_norm(alpha * x_ref[0, rows, :] + g1_ref[0] * yo) * lng_ref[...] + lnb_ref[...]
        x1_ref[0, rows, :] = x1
        u2 = _layer_norm(x1) * (1.0 + sc2_ref[0]) + sh2_ref[0]
        lt_ref[0, :, rows] = lax.dot_general(wrt_ref[...], u2.astype(BF16), NT_DIMS,
                                             preferred_element_type=F32) + brt_ref[...]
        ua_ref[0, rows, :] = _pack_pairs(u2[:, 0:PACK_W], u2[:, PACK_W:2 * PACK_W])
        ub_ref[0, rows, :] = _pack_pairs(u2[:, 2 * PACK_W:3 * PACK_W], u2[:, 3 * PACK_W:4 * PACK_W])


def _merge(alpha, b0, nb, r, m, p_lat, sec_gates, x, g1, sh2, sc2, lng, lnb, wr, wm, wo, wrt, brt):
    _, T, D = x.shape
    tm = MERGE_TM

    def tile(w):
        return pl.BlockSpec((1, tm, w), lambda b, i: (b + b0, i, 0))

    def out_tile(w):
        return pl.BlockSpec((1, tm, w), lambda b, i: (b, i, 0))

    def sec(s):
        return pl.BlockSpec((1, tm, BRANCH_W), lambda b, i: (b + b0, i, s))

    def mod():
        return pl.BlockSpec((1, 1, D), lambda b, i: (b + b0, 0, 0))

    def const(shape):
        return pl.BlockSpec(shape, lambda b, i: (0,) * len(shape))

    return pl.pallas_call(
        functools.partial(_merge_kernel, alpha),
        out_shape=(jax.ShapeDtypeStruct((nb, T, D), F32),
                   jax.ShapeDtypeStruct((nb, T, PACK_W), U32),
                   jax.ShapeDtypeStruct((nb, T, PACK_W), U32),
                   jax.ShapeDtypeStruct((nb, ROUTE_ROWS, T), F32)),
        grid=(nb, T // tm),
        in_specs=[tile(BRANCH_W), tile(BRANCH_W), sec(sec_gates[0]), sec(sec_gates[1]), tile(D),
                  mod(), mod(), mod(), const((1, D)), const((1, D)),
                  const((BRANCH_W, D)), const((BRANCH_W, D)), const((D, D)),
                  const((ROUTE_ROWS, D)), const((ROUTE_ROWS, 1))],
        out_specs=(out_tile(D), out_tile(PACK_W), out_tile(PACK_W),
                   pl.BlockSpec((1, ROUTE_ROWS, tm), lambda b, i: (b, 0, i))),
        compiler_params=_cparams("parallel", "parallel"),
        name="merge",
    )(r, m, p_lat, p_lat, x, g1, sh2, sc2, lng, lnb, wr, wm, wo, wrt, brt)


def _route_kernel(lt_ref, ri_ref, rw_ref, cnt_ref, carry_ref, u_ref):
    i = pl.program_id(0)
    tm = lt_ref.shape[2]

    @pl.when(i == 0)
    def _():
        carry_ref[...] = jnp.zeros_like(carry_ref)
        r = lax.broadcasted_iota(I32, (tm, tm), 0)
        c = lax.broadcasted_iota(I32, (tm, tm), 1)
        u_ref[...] = (r < c).astype(BF16)

    lt = lt_ref[0]
    lg = lt[0:N_GROUPS, :]
    eg = jnp.exp(lg - jnp.max(lg, axis=0, keepdims=True))
    pg = eg / jnp.sum(eg, axis=0, keepdims=True)
    pg_top = jnp.max(pg, axis=0, keepdims=True)
    rows_g = lax.broadcasted_iota(I32, pg.shape, 0)
    g_idx = jnp.min(jnp.where(pg == pg_top, rows_g, N_GROUPS), axis=0, keepdims=True)

    le = jnp.zeros((EXP_PER_GROUP, tm), F32)
    for g in range(N_GROUPS):
        lo = 8 + g * EXP_PER_GROUP
        le = jnp.where(g_idx == g, lt[lo:lo + EXP_PER_GROUP, :], le)
    ee = jnp.exp(le - jnp.max(le, axis=0, keepdims=True))
    pe = ee / jnp.sum(ee, axis=0, keepdims=True)
    rows_e = lax.broadcasted_iota(I32, pe.shape, 0)
    v1 = jnp.max(pe, axis=0, keepdims=True)
    i1 = jnp.min(jnp.where(pe == v1, rows_e, EXP_PER_GROUP), axis=0, keepdims=True)
    pe2 = jnp.where(rows_e == i1, -1.0, pe)
    v2 = jnp.max(pe2, axis=0, keepdims=True)
    i2 = jnp.min(jnp.where(pe2 == v2, rows_e, EXP_PER_GROUP), axis=0, keepdims=True)
    den = v1 + v2
    rw_ref[0:1, :] = pg_top * v1 / den
    rw_ref[1:2, :] = pg_top * v2 / den
    e1 = g_idx * EXP_PER_GROUP + i1
    e2 = g_idx * EXP_PER_GROUP + i2

    rows_x = lax.broadcasted_iota(I32, (N_EXPERTS, tm), 0)
    oh1 = (rows_x == e1).astype(F32)
    oh2 = (rows_x == e2).astype(F32)
    both = oh1 + oh2
    before = carry_ref[:, 0:1] + jnp.dot(both.astype(BF16), u_ref[...], preferred_element_type=F32)
    ri_ref[0:1, :] = e1
    ri_ref[1:2, :] = e2
    ri_ref[2:3, :] = jnp.sum(oh1 * before, axis=0, keepdims=True).astype(I32)
    ri_ref[3:4, :] = jnp.sum(oh2 * before, axis=0, keepdims=True).astype(I32)
    carry_ref[...] = carry_ref[...] + jnp.sum(both, axis=1, keepdims=True)
    cnt_ref[...] = carry_ref[...].astype(I32)


def _route(lt):
    B, _, T = lt.shape
    tm = ROUTE_TM
    per_b = T // tm
    n = B * T
    return pl.pallas_call(
        _route_kernel,
        out_shape=(jax.ShapeDtypeStruct((4, n), I32),
                   jax.ShapeDtypeStruct((2, n), F32),
                   jax.ShapeDtypeStruct((N_EXPERTS, 128), I32)),
        grid=(n // tm,),
        in_specs=[pl.BlockSpec((1, ROUTE_ROWS, tm), lambda i: (i // per_b, 0, i % per_b))],
        out_specs=(pl.BlockSpec((4, tm), lambda i: (0, i)),
                   pl.BlockSpec((2, tm), lambda i: (0, i)),
                   pl.BlockSpec((N_EXPERTS, 128), lambda i: (0, 0))),
        scratch_shapes=[pltpu.VMEM((N_EXPERTS, 128), F32), pltpu.VMEM((tm, tm), BF16)],
        compiler_params=_cparams("arbitrary"),
        name="route",
    )(lt)


def _sc_mesh():
    return plsc.VectorSubcoreMesh(core_axis_name="c", subcore_axis_name="s")


def _sc_scatter2(rows, idx0, idx1, n_out):
    m, w = rows.shape

    @functools.partial(pl.kernel, out_type=jax.ShapeDtypeStruct((n_out, w), rows.dtype),
                       mesh=_sc_mesh(), scratch_types=[])
    def k(x_hbm, i0_hbm, i1_hbm, o_hbm):
        def body(x_vmem, i0_vmem, i1_vmem):
            pltpu.sync_copy(x_vmem, o_hbm.at[i0_vmem.at[0]])
            pltpu.sync_copy(x_vmem, o_hbm.at[i1_vmem.at[0]])

        pltpu.emit_pipeline(
            body,
            grid=(m // SC_WIN,),
            in_specs=[pl.BlockSpec((SC_WIN, w), lambda i: (i, 0)),
                      pl.BlockSpec((1, SC_WIN), lambda i: (0, i)),
                      pl.BlockSpec((1, SC_WIN), lambda i: (0, i))],
            out_specs=[],
            core_axis_name=("c", "s"),
            dimension_semantics=(pltpu.PARALLEL,),
        )(x_hbm, i0_hbm, i1_hbm)

    return k(rows, idx0.reshape(1, m), idx1.reshape(1, m))


def _sc_gather(table, idx):
    m = idx.shape[0]
    w = table.shape[1]

    @functools.partial(pl.kernel, out_type=jax.ShapeDtypeStruct((m, w), table.dtype),
                       mesh=_sc_mesh(), scratch_types=[])
    def k(t_hbm, i_hbm, o_hbm):
        def body(i_vmem, o_vmem):
            pltpu.sync_copy(t_hbm.at[i_vmem.at[0]], o_vmem)

        pltpu.emit_pipeline(
            body,
            grid=(m // SC_WIN,),
            in_specs=[pl.BlockSpec((1, SC_WIN), lambda i: (0, i))],
            out_specs=[pl.BlockSpec((SC_WIN, w), lambda i: (i, 0))],
            core_axis_name=("c", "s"),
            dimension_semantics=(pltpu.PARALLEL,),
        )(i_hbm, o_hbm)

    return k(table, idx.reshape(1, m))


def _expert_kernel(be_ref, nv_ref, xa_ref, xb_ref, w1f_ref, w3f_ref, w2f_ref, ya_ref, yb_ref,
                   w1_ref, w3_ref, w2_ref):
    j = pl.program_id(0)
    nv = nv_ref[j]

    @pl.when(jnp.logical_or(j == 0, be_ref[j] != be_ref[jnp.maximum(j - 1, 0)]))
    def _():
        w1_ref[0] = w1f_ref[0].astype(BF16)
        w3_ref[0] = w3f_ref[0].astype(BF16)
        w2_ref[0] = w2f_ref[0].astype(BF16)

    @pl.when(nv > 0)
    def _():
        for s in range(xa_ref.shape[0] // MOE_SUB):
            rows = slice(s * MOE_SUB, (s + 1) * MOE_SUB)
            valid = lax.broadcasted_iota(I32, (MOE_SUB, PACK_W), 0) + s * MOE_SUB < nv
            zero = jnp.zeros((MOE_SUB, PACK_W), U32)
            parts = _unpack_pairs(jnp.where(valid, xa_ref[rows, :], zero)) + \
                _unpack_pairs(jnp.where(valid, xb_ref[rows, :], zero))
            x = jnp.concatenate([p.astype(BF16) for p in parts], axis=1)
            h1 = jnp.dot(x, w1_ref[0], preferred_element_type=F32)
            h3 = jnp.dot(x, w3_ref[0], preferred_element_type=F32)
            y = jnp.dot((_silu(h1) * h3).astype(BF16), w2_ref[0], preferred_element_type=F32)
            ya_ref[rows, :] = _pack_pairs(y[:, 0:PACK_W], y[:, PACK_W:2 * PACK_W])
            yb_ref[rows, :] = _pack_pairs(y[:, 2 * PACK_W:3 * PACK_W], y[:, 3 * PACK_W:4 * PACK_W])

    @pl.when(nv == 0)
    def _():
        ya_ref[...] = jnp.zeros_like(ya_ref)
        yb_ref[...] = jnp.zeros_like(yb_ref)


def _experts(block_exp, n_valid, xa, xb, w1, w3, w2):
    n_slots = xa.shape[0]
    n_blocks = n_slots // MOE_BLK
    d, de = w1.shape[1], w1.shape[2]
    slot = pl.BlockSpec((MOE_BLK, PACK_W), lambda j, be, nv: (j, 0))
    grid_spec = pltpu.PrefetchScalarGridSpec(
        num_scalar_prefetch=2,
        grid=(n_blocks,),
        in_specs=[slot, slot,
                  pl.BlockSpec((1, d, de), lambda j, be, nv: (be[j], 0, 0)),
                  pl.BlockSpec((1, d, de), lambda j, be, nv: (be[j], 0, 0)),
                  pl.BlockSpec((1, de, d), lambda j, be, nv: (be[j], 0, 0))],
        out_specs=(slot, slot),
        scratch_shapes=[pltpu.VMEM((1, d, de), BF16), pltpu.VMEM((1, d, de), BF16),
                        pltpu.VMEM((1, de, d), BF16)],
    )
    return pl.pallas_call(
        _expert_kernel,
        out_shape=(jax.ShapeDtypeStruct((n_slots, PACK_W), U32),
                   jax.ShapeDtypeStruct((n_slots, PACK_W), U32)),
        grid_spec=grid_spec,
        compiler_params=_cparams("arbitrary"),
        name="experts",
    )(block_exp, n_valid, xa, xb, w1, w3, w2)


def _final_kernel(alpha, x1_ref, a0_ref, b0_ref, a1_ref, b1_ref, w_ref, g2_ref, lng_ref, lnb_ref,
                  *rest):
    o_ref = rest[-1]
    w = w_ref[...]
    w0 = w[:, 0:1]
    w1 = w[:, 1:2]
    parts0 = _unpack_pairs(a0_ref[...]) + _unpack_pairs(b0_ref[...])
    parts1 = _unpack_pairs(a1_ref[...]) + _unpack_pairs(b1_ref[...])
    f = jnp.concatenate([w0 * p0 + w1 * p1 for p0, p1 in zip(parts0, parts1)], axis=1)
    o_ref[0] = _layer_norm(alpha * x1_ref[0] + g2_ref[0] * f) * lng_ref[...] + lnb_ref[...]


def _final(alpha, b0, n_batch, x1, ya, yb, w, g2, lng, lnb, out_prev):
    nb, T, D = x1.shape
    tm = MERGE_TM
    per_b = T // tm
    n_tiles = nb * per_b

    def rows(k):
        return pl.BlockSpec((tm, PACK_W), lambda b, i: (k * n_tiles + b * per_b + i, 0))

    in_specs = [pl.BlockSpec((1, tm, D), lambda b, i: (b, i, 0)),
                rows(0), rows(0), rows(1), rows(1),
                pl.BlockSpec((tm, 2), lambda b, i: (b * per_b + i, 0)),
                pl.BlockSpec((1, 1, D), lambda b, i: (b + b0, 0, 0)),
                pl.BlockSpec((1, D), lambda b, i: (0, 0)),
                pl.BlockSpec((1, D), lambda b, i: (0, 0))]
    args = [x1, ya, yb, ya, yb, w, g2, lng, lnb]
    aliases = {}
    if out_prev is not None:
        in_specs.append(pl.BlockSpec(memory_space=pl.ANY))
        args.append(out_prev)
        aliases = {len(args) - 1: 0}
    return pl.pallas_call(
        functools.partial(_final_kernel, alpha),
        out_shape=jax.ShapeDtypeStruct((n_batch, T, D), F32),
        grid=(nb, per_b),
        in_specs=in_specs,
        out_specs=pl.BlockSpec((1, tm, D), lambda b, i: (b + b0, i, 0)),
        input_output_aliases=aliases,
        compiler_params=_cparams("parallel", "parallel"),
        name="final",
    )(*args)


def _rotary_tables(T):
    quarter = HEAD_W // 4
    freqs = ROPE_BASE ** (-jnp.arange(quarter, dtype=F32) / quarter)
    t = jnp.arange(T)
    ang_r = (t // GRID_W).astype(F32)[:, None] * freqs[None, :]
    ang_c = (t % GRID_W).astype(F32)[:, None] * freqs[None, :]
    cos = jnp.concatenate([jnp.cos(ang_r)] * 2 + [jnp.cos(ang_c)] * 2, axis=1)
    sin = jnp.concatenate([-jnp.sin(ang_r), jnp.sin(ang_r), -jnp.sin(ang_c), jnp.sin(ang_c)], axis=1)
    return cos, sin


def _per_head_gates(gt):
    B, _, T = gt.shape
    n_chunks = T // SCAN_L
    gth = gt.reshape(B, N_GK, HEADS, n_chunks, SCAN_L).transpose(0, 2, 1, 3, 4)
    return jnp.pad(gth, ((0, 0), (0, 0), (0, 0), (0, 8 - n_chunks), (0, 0)))


def _table_lookup(table, idx):
    sel = idx[..., None] == jnp.arange(table.shape[0], dtype=idx.dtype)
    return jnp.sum(jnp.where(sel, table, 0), axis=-1)


def kernel(x, c, ctx, c_ctx, w_ada, b_ada, w_in, b_mgate, ml_conv_w, ml_conv_b, ret_decay_logit, w_ret_branch, w_ml_branch, w_out, ln1_g, ln1_b, w_rg, b_rg, w_re, b_re, w_e1, w_e3, w_e2, ln2_g, ln2_b):
    B, T, D = x.shape
    depth = w_ada.shape[0]
    assert depth == 1 and D == BRANCH_W and T % min(PROJ_TM, T) == 0 and T % GRID_W == 0
    alpha = (2 * depth) ** 0.25

    n_rows = -(-(B + 1) // 8) * 8
    cs = jnp.zeros((n_rows, D), F32).at[:B].set(c).at[B].set(c_ctx)
    mod = _ada(cs, w_ada[0], b_ada[0][None, :])
    sh1, sc1, g1, sh2, sc2, g2 = [mod[:B, None, i * D:(i + 1) * D] for i in range(6)]
    csh1 = mod[B, 0 * D:1 * D].reshape(1, 1, D)
    csc1 = mod[B, 1 * D:2 * D].reshape(1, 1, D)

    w = w_in[0]
    sec_w = [w[:, s * BRANCH_W:(s + 1) * BRANCH_W] for s in range(8)]
    g_lo = 8 * BRANCH_W
    w_gate_t = w[:, g_lo:g_lo + N_GATES].T.astype(BF16)
    b_gate = b_mgate[0][:, None]
    sec_w += [w[:, g_lo + N_GATES:g_lo + N_GATES + D], w[:, g_lo + N_GATES + D:]]
    w_lat = jnp.concatenate(sec_w, axis=1).astype(BF16)
    w_ctx = jnp.concatenate([sec_w[1], sec_w[2], sec_w[5], sec_w[6]], axis=1).astype(BF16)
    kinds_lat = ("rot", "rot_scale", "plain", "plain", "conv", "conv_scale") + ("plain",) * 4
    kinds_ctx = ("scale", "plain", "conv_scale", "plain")
    conv_w, conv_b = ml_conv_w[0], ml_conv_b[0][None, :]
    p_lat, gt_lat = _proj(x, sh1, sc1, w_lat, w_gate_t, b_gate, conv_w, conv_b, kinds_lat, T,
                          _rotary_tables(T))
    Tc = ctx.shape[1]
    p_ctx, gt_ctx = _proj(ctx.reshape(1, B * Tc, D), csh1, csc1, w_ctx, w_gate_t, b_gate,
                          conv_w, conv_b, kinds_ctx, Tc)
    p_ctx = p_ctx.reshape(B, Tc, -1)
    gt_ctx = gt_ctx.reshape(N_GATES, B, Tc).transpose(1, 0, 2)

    ret = _retention(ret_decay_logit[0], p_lat, p_ctx, (0, 1, 2, 3), (0, 1))
    mls = _mlstm(p_lat, p_ctx, _per_head_gates(gt_lat), _per_head_gates(gt_ctx),
                 (4, 5, 6, 7), (2, 3))

    wrt = jnp.zeros((ROUTE_ROWS, D), F32).at[:N_GROUPS].set(w_rg[0].T).at[8:8 + N_EXPERTS].set(w_re[0].T)
    brt = jnp.zeros((ROUTE_ROWS, 1), F32).at[:N_GROUPS, 0].set(b_rg[0]).at[8:8 + N_EXPERTS, 0].set(b_re[0])
    merge_w = (ln1_g[0][None, :], ln1_b[0][None, :], w_ret_branch[0].astype(BF16),
               w_ml_branch[0].astype(BF16), w_out[0].astype(BF16), wrt.astype(BF16), brt)

    n_split = MOE_SPLIT if B % MOE_SPLIT == 0 else 1
    nb = B // n_split
    n_tok = nb * T
    n_blocks = (2 * n_tok) // MOE_BLK + N_EXPERTS
    n_slots = n_blocks * MOE_BLK
    block_start = jnp.arange(n_blocks, dtype=I32) * MOE_BLK

    def dispatch(b0):
        x1, ua, ub, lt = _merge(alpha, b0, nb, ret, mls, p_lat, (8, 9), x, g1, sh2, sc2, *merge_w)
        ri, rw, cnt = _route(lt)
        counts = cnt[:, 0]
        padded = (counts + MOE_BLK - 1) // MOE_BLK * MOE_BLK
        pad_end = jnp.cumsum(padded)
        pad_off = pad_end - padded
        dest = _table_lookup(pad_off, ri[0:2]) + ri[2:4]
        block_exp = jnp.minimum((block_start[:, None] >= pad_end[None, :]).sum(1),
                                N_EXPERTS - 1).astype(I32)
        n_valid = jnp.clip(_table_lookup(counts, block_exp)
                           - (block_start - _table_lookup(pad_off, block_exp)), 0, MOE_BLK).astype(I32)
        xa = _sc_scatter2(ua.reshape(n_tok, PACK_W), dest[0], dest[1], n_slots)
        xb = _sc_scatter2(ub.reshape(n_tok, PACK_W), dest[0], dest[1], n_slots)
        return x1, rw, dest, block_exp, n_valid, xa, xb

    def combine(d):
        x1, rw, dest, block_exp, n_valid, xa, xb = d
        ya, yb = _experts(block_exp, n_valid, xa, xb, w_e1[0], w_e3[0], w_e2[0])
        dflat = dest.reshape(2 * n_tok)
        return x1, rw, _sc_gather(ya, dflat), _sc_gather(yb, dflat)

    dispatched = [dispatch(s * nb) for s in range(n_split)]
    combined = [combine(d) for d in dispatched]
    out = None
    for s, (x1, rw, ga, gb) in enumerate(combined):
        out = _final(alpha, s * nb, B, x1, ga, gb, rw.T, g2, ln2_g[0][None, :], ln2_b[0][None, :], out)
    return out
```

```python
import functools

import jax
import jax.numpy as jnp
from jax import lax
from jax.experimental import pallas as pl
from jax.experimental.pallas import tpu as pltpu
from jax.experimental.pallas import tpu_sc as plsc

F32 = jnp.float32
BF16 = jnp.bfloat16
U32 = jnp.uint32
I32 = jnp.int32
HIGHEST = lax.Precision.HIGHEST

HEADS = 4
HEAD_W = 256
BRANCH_W = HEADS * HEAD_W
GRID_W = 64
ROPE_BASE = 10000.0
N_GATES = 16
N_GK = N_GATES // HEADS
N_GROUPS = 4
EXP_PER_GROUP = 8
N_EXPERTS = N_GROUPS * EXP_PER_GROUP
LN_EPS = 1e-5
NEG_INF = -1e30
KEY_SCALE = HEAD_W ** -0.5

SCAN_L = 256
PROJ_TM = 2048
PROJ_SUB = 256
MERGE_TM = 512
MERGE_SUB = 512
MOE_SUB = 256
ROUTE_TM = 512
MOE_BLK = 512
MOE_SPLIT = 1
SC_WIN = 128
PACK_W = 256
ROUTE_ROWS = 64
VMEM_LIMIT = 48 * 1024 * 1024

NT_DIMS = (((1,), (1,)), ((), ()))
TN_DIMS = (((0,), (0,)), ((), ()))


def _cparams(*sem):
    return pltpu.CompilerParams(dimension_semantics=sem, vmem_limit_bytes=VMEM_LIMIT)


def _layer_norm(x):
    mu = jnp.mean(x, axis=-1, keepdims=True)
    xc = x - mu
    var = jnp.mean(xc * xc, axis=-1, keepdims=True)
    return xc * lax.rsqrt(var + LN_EPS)


def _log_sigmoid(x):
    return jnp.minimum(x, 0.0) - jnp.log1p(jnp.exp(-jnp.abs(x)))


def _silu(x):
    return x * jax.nn.sigmoid(x)


def _pack_pairs(hi, lo):
    hb = lax.bitcast_convert_type(hi.astype(BF16).astype(F32), U32)
    lb = lax.bitcast_convert_type(lo.astype(BF16).astype(F32), U32)
    return (hb & jnp.uint32(0xFFFF0000)) | (lb >> 16)


def _unpack_pairs(p):
    hi = lax.bitcast_convert_type(p & jnp.uint32(0xFFFF0000), F32)
    lo = lax.bitcast_convert_type(p << 16, F32)
    return hi, lo


def _ada_kernel(c_ref, w_ref, b_ref, o_ref):
    s = _silu(c_ref[...])
    o_ref[...] = jnp.dot(s, w_ref[...], precision=HIGHEST, preferred_element_type=F32) + b_ref[...]


def _ada(cs, w, b):
    rows, d = cs.shape
    cols = w.shape[1]
    tn = 1024
    return pl.pallas_call(
        _ada_kernel,
        out_shape=jax.ShapeDtypeStruct((rows, cols), F32),
        grid=(cols // tn,),
        in_specs=[pl.BlockSpec((rows, d), lambda j: (0, 0)),
                  pl.BlockSpec((d, tn), lambda j: (0, j)),
                  pl.BlockSpec((1, tn), lambda j: (0, j))],
        out_specs=pl.BlockSpec((rows, tn), lambda j: (0, j)),
        compiler_params=_cparams("parallel"),
        name="ada",
    )(cs, w, b)


def _proj_kernel(kinds, seq_len, x_ref, sh_ref, sc_ref, w_ref, wg_ref, bg_ref, cw_ref, cb_ref, *rest):
    if "rot" in kinds or "rot_scale" in kinds:
        cos_ref, sin_ref, o_ref, gt_ref, u_ref = rest
    else:
        o_ref, gt_ref, u_ref = rest
    j = pl.program_id(2)
    tm = x_ref.shape[1]
    sub = min(PROJ_SUB, tm)
    n_sub = tm // sub
    assert kinds[0] not in ("conv", "conv_scale") and seq_len % sub == 0

    def conv_section(kind):
        c0 = 0 if kind == "conv" else BRANCH_W
        w = cw_ref[:, c0:c0 + BRANCH_W]
        b = cb_ref[:, c0:c0 + BRANCH_W]
        sub8 = lax.broadcasted_iota(I32, (8, BRANCH_W), 0)
        zero_row = jnp.zeros((1, BRANCH_W), F32)
        accs = []

        def finish(r):
            a = accs[r]
            seq_start = (r * sub) % seq_len == 0
            seq_end = ((r + 1) * sub) % seq_len == 0
            before = zero_row if seq_start else accs[r - 1][sub - 1:sub, :]
            after = zero_row if seq_end else accs[r + 1][0:1, :]
            prev = pltpu.roll(a, 1, 0)
            prev = jnp.concatenate([jnp.where(sub8 == 0, before, prev[0:8, :]), prev[8:, :]], axis=0)
            nxt = pltpu.roll(a, sub - 1, 0)
            nxt = jnp.concatenate([nxt[:sub - 8, :], jnp.where(sub8 == 7, after, nxt[sub - 8:, :])], axis=0)
            y = _silu(prev * w[0:1, :] + a * w[1:2, :] + nxt * w[2:3, :] + b)
            if kind == "conv_scale":
                y = y * KEY_SCALE
            o_ref[0, r * sub:(r + 1) * sub, :] = y.astype(BF16)

        for r in range(n_sub):
            accs.append(jnp.dot(u_ref[r * sub:(r + 1) * sub, :], w_ref[...], preferred_element_type=F32))
            if r >= 1:
                finish(r - 1)
        finish(n_sub - 1)

    def rotary(acc, rows, scale):
        for s in range(acc.shape[1] // 128):
            a = acc[:, s * 128:(s + 1) * 128]
            half = s % 2
            cs = cos_ref[rows, half * 128:(half + 1) * 128]
            sn = sin_ref[rows, half * 128:(half + 1) * 128]
            r = a * cs + pltpu.roll(a, 64, 1) * sn
            if scale != 1.0:
                r = r * scale
            o_ref[0, rows, s * 128:(s + 1) * 128] = r.astype(BF16)

    def section(kind, first):
        if kind in ("conv", "conv_scale"):
            conv_section(kind)
            return
        for r in range(n_sub):
            rows = slice(r * sub, (r + 1) * sub)
            if first:
                u = _layer_norm(x_ref[0, rows, :]) * (1.0 + sc_ref[0]) + sh_ref[0]
                ub = u.astype(BF16)
                u_ref[rows, :] = ub
                gt_ref[0, :, rows] = lax.dot_general(wg_ref[...], ub, NT_DIMS,
                                                     preferred_element_type=F32) + bg_ref[...]
            else:
                ub = u_ref[rows, :]
            acc = jnp.dot(ub, w_ref[...], preferred_element_type=F32)
            if kind == "rot":
                rotary(acc, rows, 1.0)
            elif kind == "rot_scale":
                rotary(acc, rows, KEY_SCALE)
            elif kind == "scale":
                o_ref[0, rows, :] = (acc * KEY_SCALE).astype(BF16)
            else:
                o_ref[0, rows, :] = acc.astype(BF16)

    variants = {}
    for s, kind in enumerate(kinds):
        variants.setdefault((kind, s == 0), []).append(s)
    for (kind, first), secs in variants.items():
        cond = functools.reduce(jnp.logical_or, [j == s for s in secs])

        @pl.when(cond)
        def _(kind=kind, first=first):
            section(kind, first)


def _proj(x, sh, sc, w_main, w_gate_t, b_gate, conv_w, conv_b, kinds, seq_len, tables=None):
    B, T, D = x.shape
    n_sec = len(kinds)
    tm = min(PROJ_TM, T)
    tn = BRANCH_W
    assert tm % seq_len == 0 and T % tm == 0
    in_specs = [
        pl.BlockSpec((1, tm, D), lambda i, b, j: (b, i, 0)),
        pl.BlockSpec((1, 1, D), lambda i, b, j: (b, 0, 0)),
        pl.BlockSpec((1, 1, D), lambda i, b, j: (b, 0, 0)),
        pl.BlockSpec((D, tn), lambda i, b, j: (0, j)),
        pl.BlockSpec((N_GATES, D), lambda i, b, j: (0, 0)),
        pl.BlockSpec((N_GATES, 1), lambda i, b, j: (0, 0)),
        pl.BlockSpec(conv_w.shape, lambda i, b, j: (0, 0)),
        pl.BlockSpec(conv_b.shape, lambda i, b, j: (0, 0)),
    ]
    args = [x, sh, sc, w_main, w_gate_t, b_gate, conv_w, conv_b]
    if tables is not None:
        in_specs += [pl.BlockSpec((tm, HEAD_W), lambda i, b, j: (i, 0))] * 2
        args += list(tables)
    return pl.pallas_call(
        functools.partial(_proj_kernel, kinds, seq_len),
        out_shape=(jax.ShapeDtypeStruct((B, T, n_sec * tn), BF16),
                   jax.ShapeDtypeStruct((B, N_GATES, T), F32)),
        grid=(T // tm, B, n_sec),
        in_specs=in_specs,
        out_specs=(pl.BlockSpec((1, tm, tn), lambda i, b, j: (b, i, j)),
                   pl.BlockSpec((1, N_GATES, tm), lambda i, b, j: (b, 0, i))),
        scratch_shapes=[pltpu.VMEM((tm, D), BF16)],
        compiler_params=_cparams("parallel", "parallel", "arbitrary"),
        name="proj_lat" if tables is not None else "proj_ctx",
    )(*args)


def _ret_build(dl_ref, q_ref, k_ref, v_ref, rg_ref, ck_ref, cv_ref, o_ref,
               sf_ref, sb_ref, fs_ref, bs_ref, dec_ref, d_ref):
    h = pl.program_id(1)
    L = SCAN_L
    n_chunks = q_ref.shape[1] // L
    n_ctx_chunks = ck_ref.shape[1] // L
    lgf = _log_sigmoid(jnp.full((1, 1), dl_ref[0, h], F32))
    lgb = _log_sigmoid(jnp.full((1, 1), dl_ref[1, h], F32))

    ri = lax.broadcasted_iota(I32, (L, L), 0)
    ci = lax.broadcasted_iota(I32, (L, L), 1)
    rel = (ri - ci).astype(F32)
    d_ref[...] = jnp.where(rel >= 0.0, jnp.exp(jnp.maximum(rel, 0.0) * lgf),
                           jnp.exp(jnp.maximum(-rel, 0.0) * lgb))
    row = lax.broadcasted_iota(I32, (L, HEAD_W), 0).astype(F32)
    dec_ref[0] = jnp.exp((row + 1.0) * lgf)
    dec_ref[1] = jnp.exp((L - 1.0 - row) * lgf)
    dec_ref[2] = jnp.exp((L - row) * lgb)
    dec_ref[3] = jnp.exp(row * lgb)
    cdf = jnp.exp(L * lgf)
    cdb = jnp.exp(L * lgb)

    def update(s_ref, kc, vc, kd, cd):
        kdec = (kc.astype(F32) * kd).astype(BF16)
        s_ref[...] = s_ref[...] * cd + lax.dot_general(kdec, vc, TN_DIMS, preferred_element_type=F32)

    sf_ref[...] = jnp.zeros_like(sf_ref)
    sb_ref[...] = jnp.zeros_like(sb_ref)
    for c in range(n_ctx_chunks):
        update(sf_ref, ck_ref[0, c * L:(c + 1) * L, :], cv_ref[0, c * L:(c + 1) * L, :], dec_ref[1], cdf)
    for c in reversed(range(n_ctx_chunks)):
        update(sb_ref, ck_ref[0, c * L:(c + 1) * L, :], cv_ref[0, c * L:(c + 1) * L, :], dec_ref[3], cdb)

    def state_pass(i, carry):
        cb = n_chunks - 1 - i
        rf = pl.multiple_of(i * L, L)
        rb = pl.multiple_of(cb * L, L)
        fs_ref[i] = sf_ref[...].astype(BF16)
        bs_ref[cb] = sb_ref[...].astype(BF16)
        update(sf_ref, k_ref[0, pl.ds(rf, L), :], v_ref[0, pl.ds(rf, L), :], dec_ref[1], cdf)
        update(sb_ref, k_ref[0, pl.ds(rb, L), :], v_ref[0, pl.ds(rb, L), :], dec_ref[3], cdb)
        return carry

    def finish_states():
        fs_ref[n_chunks - 1] = sf_ref[...].astype(BF16)
        bs_ref[0] = sb_ref[...].astype(BF16)

    def out_chunk(c):
        r0 = pl.multiple_of(c * L, L)
        q = q_ref[0, pl.ds(r0, L), :]
        k = k_ref[0, pl.ds(r0, L), :]
        v = v_ref[0, pl.ds(r0, L), :]
        s = lax.dot_general(q, k, NT_DIMS, preferred_element_type=F32)
        att = (s * d_ref[...]).astype(BF16)
        o = jnp.dot(att, v, preferred_element_type=F32)
        o = o + jnp.dot(q, fs_ref[c], preferred_element_type=F32) * dec_ref[0]
        o = o + jnp.dot(q, bs_ref[c], preferred_element_type=F32) * dec_ref[2]
        rg = rg_ref[0, pl.ds(r0, L), :].astype(F32)
        o_ref[0, pl.ds(r0, L), :] = (_layer_norm(o) * _silu(rg)).astype(BF16)

    return state_pass, finish_states, out_chunk


def _retention_unused(decay_logit, p_lat, p_ctx, sec_lat, sec_ctx):
    B, T, _ = p_lat.shape
    Tc = p_ctx.shape[1]
    assert T % (2 * SCAN_L) == 0 and Tc % SCAN_L == 0
    n_chunks = T // SCAN_L

    def lat(sec):
        return pl.BlockSpec((1, T, HEAD_W), lambda b, h: (b, 0, sec * HEADS + h))

    def cx(sec):
        return pl.BlockSpec((1, Tc, HEAD_W), lambda b, h: (b, 0, sec * HEADS + h))

    return pl.pallas_call(
        _ret_kernel,
        out_shape=jax.ShapeDtypeStruct((B, T, BRANCH_W), BF16),
        grid=(B, HEADS),
        in_specs=[pl.BlockSpec(memory_space=pltpu.SMEM)]
        + [lat(s) for s in sec_lat] + [cx(s) for s in sec_ctx],
        out_specs=pl.BlockSpec((1, T, HEAD_W), lambda b, h: (b, 0, h)),
        scratch_shapes=[pltpu.VMEM((HEAD_W, HEAD_W), F32),
                        pltpu.VMEM((HEAD_W, HEAD_W), F32),
                        pltpu.VMEM((n_chunks, HEAD_W, HEAD_W), BF16),
                        pltpu.VMEM((n_chunks, HEAD_W, HEAD_W), BF16),
                        pltpu.VMEM((4, SCAN_L, HEAD_W), F32),
                        pltpu.VMEM((SCAN_L, SCAN_L), F32)],
        compiler_params=_cparams("parallel", "parallel"),
        name="retention",
    )(decay_logit, p_lat, p_lat, p_lat, p_lat, p_ctx, p_ctx)


def _mlstm_kernel_old(q_ref, k_ref, v_ref, mo_ref, ck_ref, cv_ref, g_ref, gt_ref, cg_ref, cgt_ref,
                  wq_ref, bq_ref, wk_ref, bk_ref, o_ref,
                  xf_ref, qs_ref, ks_ref, cks_ref,
                  glc_ref, csc_ref, rcc_ref, glr_ref, csr_ref, rcr_ref,
                  cf_ref, nf_ref, mf_ref, cb_ref, nb_ref, mb_ref,
                  cfs_ref, nfs_ref, mfs_ref, cbs_ref, nbs_ref, mbs_ref):
    L = SCAN_L
    T = q_ref.shape[1]
    Tc = ck_ref.shape[1]
    n_chunks = T // L
    n_ctx_chunks = Tc // L
    CV = 128

    def conv_silu(src_ref, w_ref, b_ref, dst_ref, t_len, scale):
        xf_ref[pl.ds(0, 8), :] = jnp.zeros((8, HEAD_W), F32)
        xf_ref[pl.ds(8 + t_len, 8), :] = jnp.zeros((8, HEAD_W), F32)
        xf_ref[pl.ds(8, t_len), :] = src_ref[0].astype(F32)
        w = w_ref[...]
        b = b_ref[...]

        def body(c, carry):
            r0 = pl.multiple_of(c * CV, CV)
            win = xf_ref[pl.ds(r0, CV + 16), :]
            prev = pltpu.roll(win, 1, 0)[8:8 + CV, :]
            cur = win[8:8 + CV, :]
            nxt = pltpu.roll(win, CV + 15, 0)[8:8 + CV, :]
            y = prev * w[0:1, :] + cur * w[1:2, :] + nxt * w[2:3, :] + b
            y = _silu(y)
            if scale != 1.0:
                y = y * scale
            dst_ref[pl.ds(r0, CV), :] = y.astype(BF16)
            return carry

        lax.fori_loop(0, t_len // CV, body, 0)

    conv_silu(q_ref, wq_ref, bq_ref, qs_ref, T, 1.0)
    conv_silu(k_ref, wk_ref, bk_ref, ks_ref, T, KEY_SCALE)
    conv_silu(ck_ref, wk_ref, bk_ref, cks_ref, Tc, KEY_SCALE)

    ri = lax.broadcasted_iota(I32, (L, L), 0)
    ci = lax.broadcasted_iota(I32, (L, L), 1)
    tri_l = (ci <= ri).astype(F32)
    tri_u = (ci >= ri).astype(F32)

    def gate_tables(gc, gtc):
        lane = lax.broadcasted_iota(I32, gc.shape, 1)
        gl = jnp.where(lane % 2 == 1, _log_sigmoid(gc), gc)
        sub = lax.broadcasted_iota(I32, gtc.shape, 0)
        gtl = jnp.where(sub % 2 == 1, _log_sigmoid(gtc), gtc)
        cs_col = jnp.dot(tri_l, gl, precision=HIGHEST, preferred_element_type=F32)
        rc_col = jnp.dot(tri_u, gl, precision=HIGHEST, preferred_element_type=F32)
        cs_row = jnp.dot(gtl, tri_u, precision=HIGHEST, preferred_element_type=F32)
        rc_row = jnp.dot(gtl, tri_l, precision=HIGHEST, preferred_element_type=F32)
        return gl, cs_col, rc_col, gtl, cs_row, rc_row

    def pick(tables, backward):
        gl, cs_col, rc_col, gtl, cs_row, rc_row = tables
        if not backward:
            return gl[:, 0:1], gtl[0:1, :], cs_col[:, 1:2], cs_row[1:2, :], cs_row[1:2, L - 1:L]
        return gl[:, 2:3], gtl[2:3, :], rc_col[:, 3:4], rc_row[3:4, :], rc_row[3:4, 0:1]

    def lat_tables(r0):
        return (glc_ref[pl.ds(r0, L), :], csc_ref[pl.ds(r0, L), :], rcc_ref[pl.ds(r0, L), :],
                glr_ref[:, pl.ds(r0, L)], csr_ref[:, pl.ds(r0, L)], rcr_ref[:, pl.ds(r0, L)])

    def table_pass(c, carry):
        r0 = pl.multiple_of(c * L, L)
        gl, cs_col, rc_col, gtl, cs_row, rc_row = gate_tables(
            g_ref[0, 0, pl.ds(r0, L), :], gt_ref[0, 0, :, pl.ds(r0, L)])
        glc_ref[pl.ds(r0, L), :] = gl
        csc_ref[pl.ds(r0, L), :] = cs_col
        rcc_ref[pl.ds(r0, L), :] = rc_col
        glr_ref[:, pl.ds(r0, L)] = gtl
        csr_ref[:, pl.ds(r0, L)] = cs_row
        rcr_ref[:, pl.ds(r0, L)] = rc_row
        return carry

    lax.fori_loop(0, n_chunks, table_pass, 0)

    def advance(k, v, gate_vecs, c_ref, n_ref, m_ref):
        i_col, i_row, b_col, b_row, b_last = gate_vecs
        m = m_ref[...]
        g_col = b_last - b_col + i_col
        g_row = b_last - b_row + i_row
        m_new = jnp.maximum(b_last + m, jnp.max(g_row, axis=-1, keepdims=True))
        kw = k.astype(F32) * jnp.exp(g_col - m_new)
        decay = jnp.exp(b_last + m - m_new)
        c_ref[...] = decay * c_ref[...] + lax.dot_general(kw.astype(BF16), v, TN_DIMS,
                                                          preferred_element_type=F32)
        n_ref[...] = decay * n_ref[...] + jnp.sum(kw, axis=0, keepdims=True)
        m_ref[...] = m_new

    for refs in ((cf_ref, nf_ref, mf_ref), (cb_ref, nb_ref, mb_ref)):
        for r in refs:
            r[...] = jnp.zeros_like(r)
    ctx_tabs = [gate_tables(cg_ref[0, 0, c * L:(c + 1) * L, :], cgt_ref[0, 0, :, c * L:(c + 1) * L])
                for c in range(n_ctx_chunks)]
    for c in range(n_ctx_chunks):
        advance(cks_ref[c * L:(c + 1) * L, :], cv_ref[0, c * L:(c + 1) * L, :],
                pick(ctx_tabs[c], False), cf_ref, nf_ref, mf_ref)
    for c in reversed(range(n_ctx_chunks)):
        advance(cks_ref[c * L:(c + 1) * L, :], cv_ref[0, c * L:(c + 1) * L, :],
                pick(ctx_tabs[c], True), cb_ref, nb_ref, mb_ref)

    def snapshot(c, src, dst):
        dst[0][c] = src[0][...].astype(BF16)
        dst[1][c] = src[1][...]
        dst[2][c] = src[2][...]

    fwd_run, fwd_snap = (cf_ref, nf_ref, mf_ref), (cfs_ref, nfs_ref, mfs_ref)
    bwd_run, bwd_snap = (cb_ref, nb_ref, mb_ref), (cbs_ref, nbs_ref, mbs_ref)

    def state_pass(i, carry):
        cb = n_chunks - 1 - i
        rf = pl.multiple_of(i * L, L)
        rb = pl.multiple_of(cb * L, L)
        snapshot(i, fwd_run, fwd_snap)
        snapshot(cb, bwd_run, bwd_snap)
        advance(ks_ref[pl.ds(rf, L), :], v_ref[0, pl.ds(rf, L), :], pick(lat_tables(rf), False), *fwd_run)
        advance(ks_ref[pl.ds(rb, L), :], v_ref[0, pl.ds(rb, L), :], pick(lat_tables(rb), True), *bwd_run)
        return carry

    lax.fori_loop(0, n_chunks - 1, state_pass, 0)
    snapshot(n_chunks - 1, fwd_run, fwd_snap)
    snapshot(0, bwd_run, bwd_snap)

    def direction(q, v, s, gate_vecs, mask, c_in, n_in, m_in):
        _, i_row, b_col, b_row, _ = gate_vecs
        d = jnp.where(mask, b_col - b_row + i_row, NEG_INF)
        inter = b_col + m_in
        m_row = jnp.maximum(jnp.max(d, axis=-1, keepdims=True), inter)
        a = jnp.exp(inter - m_row)
        att = s * jnp.exp(d - m_row)
        num = jnp.dot(att.astype(BF16), v, preferred_element_type=F32)
        num = num + a * jnp.dot(q, c_in, preferred_element_type=F32)
        qn = jnp.sum(q.astype(F32) * n_in, axis=-1, keepdims=True)
        den = jnp.sum(att, axis=-1, keepdims=True) + a * qn
        return num * (1.0 / jnp.maximum(jnp.abs(den), jnp.exp(-m_row)))

    def out_pass(c, carry):
        r0 = pl.multiple_of(c * L, L)
        q = qs_ref[pl.ds(r0, L), :]
        k = ks_ref[pl.ds(r0, L), :]
        v = v_ref[0, pl.ds(r0, L), :]
        s = lax.dot_general(q, k, NT_DIMS, preferred_element_type=F32)
        tabs = lat_tables(r0)
        tot = direction(q, v, s, pick(tabs, False), ci <= ri, cfs_ref[c], nfs_ref[c], mfs_ref[c])
        tot = tot + direction(q, v, s, pick(tabs, True), ci > ri, cbs_ref[c], nbs_ref[c], mbs_ref[c])
        mo = mo_ref[0, pl.ds(r0, L), :].astype(F32)
        o_ref[0, pl.ds(r0, L), :] = (_layer_norm(tot) * jax.nn.sigmoid(mo)).astype(BF16)
        return carry

    lax.fori_loop(0, n_chunks, out_pass, 0)


def _mlstm_old(p_lat, p_ctx, g, gt, cg, cgt, conv_w, conv_b, sec_lat, sec_ctx):
    B, T, _ = p_lat.shape
    Tc = p_ctx.shape[1]
    assert T % SCAN_L == 0 and Tc % SCAN_L == 0
    n_chunks = T // SCAN_L

    def lat(sec):
        return pl.BlockSpec((1, T, HEAD_W), lambda b, h: (b, 0, sec * HEADS + h))

    def cx(sec):
        return pl.BlockSpec((1, Tc, HEAD_W), lambda b, h: (b, 0, sec * HEADS + h))

    in_specs = [lat(s) for s in sec_lat] + [cx(s) for s in sec_ctx] + [
        pl.BlockSpec((1, 1, T, N_GK), lambda b, h: (b, h, 0, 0)),
        pl.BlockSpec((1, 1, N_GK, T), lambda b, h: (b, h, 0, 0)),
        pl.BlockSpec((1, 1, Tc, N_GK), lambda b, h: (b, h, 0, 0)),
        pl.BlockSpec((1, 1, N_GK, Tc), lambda b, h: (b, h, 0, 0)),
        pl.BlockSpec((3, HEAD_W), lambda b, h: (0, h)),
        pl.BlockSpec((1, HEAD_W), lambda b, h: (0, h)),
        pl.BlockSpec((3, HEAD_W), lambda b, h: (0, HEADS + h)),
        pl.BlockSpec((1, HEAD_W), lambda b, h: (0, HEADS + h)),
    ]
    state = [pltpu.VMEM((HEAD_W, HEAD_W), F32), pltpu.VMEM((1, HEAD_W), F32), pltpu.VMEM((1, 1), F32)]
    snaps = [pltpu.VMEM((n_chunks, HEAD_W, HEAD_W), BF16), pltpu.VMEM((n_chunks, 1, HEAD_W), F32),
             pltpu.VMEM((n_chunks, 1, 1), F32)]
    return pl.pallas_call(
        _mlstm_kernel,
        out_shape=jax.ShapeDtypeStruct((B, T, BRANCH_W), BF16),
        grid=(B, HEADS),
        in_specs=in_specs,
        out_specs=pl.BlockSpec((1, T, HEAD_W), lambda b, h: (b, 0, h)),
        scratch_shapes=[pltpu.VMEM((T + 16, HEAD_W), F32),
                        pltpu.VMEM((T, HEAD_W), BF16),
                        pltpu.VMEM((T, HEAD_W), BF16),
                        pltpu.VMEM((Tc, HEAD_W), BF16)]
        + [pltpu.VMEM((T, N_GK), F32)] * 3 + [pltpu.VMEM((N_GK, T), F32)] * 3
        + state + state + snaps + snaps,
        compiler_params=_cparams("parallel", "parallel"),
        name="mlstm",
    )(p_lat, p_lat, p_lat, p_lat, p_ctx, p_ctx, g, gt, cg, cgt, conv_w, conv_b, conv_w, conv_b)


N_TAB = 6
AUG_W = HEAD_W + 128


def _split3(x):
    hi = x.astype(BF16).astype(F32)
    r1 = x - hi
    mid = r1.astype(BF16).astype(F32)
    lo = (r1 - mid).astype(BF16).astype(F32)
    return jnp.concatenate([hi, mid, lo], axis=0).astype(BF16)


def _mlstm_build(q_ref, k_ref, v_ref, mo_ref, ck_ref, cv_ref, gt_ref, cgt_ref, o_ref,
                 tab_ref, row_ref,
                 cf_ref, mf_ref, cb_ref, mb_ref, cfs_ref, mfs_ref, cbs_ref, mbs_ref, mask_ref):
    L = SCAN_L
    T = q_ref.shape[1]
    Tc = ck_ref.shape[1]
    n_chunks = T // L
    n_ctx_chunks = Tc // L

    ri = lax.broadcasted_iota(I32, (L, L), 0)
    ci = lax.broadcasted_iota(I32, (L, L), 1)
    tri_u = (ri <= ci).astype(BF16)
    lane8 = lax.broadcasted_iota(I32, (8, L), 1)
    sub8 = lax.broadcasted_iota(I32, (8, L), 0)
    sel_r = lax.broadcasted_iota(I32, (24, 8 * 128), 0) % 8
    sel_c = lax.broadcasted_iota(I32, (24, 8 * 128), 1) // 128
    sel3 = (sel_r == sel_c).astype(BF16)
    ones_cols = jnp.ones((L, AUG_W - HEAD_W), BF16)

    def chunk_tables(g8, n_used, state_only):
        i_f, i_b = g8[0], g8[2]
        lf_f, lf_b = _log_sigmoid(g8[1]), _log_sigmoid(g8[3])
        cs3 = jnp.dot(_split3(jnp.concatenate([lf_f, lf_b], axis=0)), tri_u,
                      preferred_element_type=F32)
        cs = cs3[0:16] + cs3[16:32] + cs3[32:48]
        b_f = cs[0:8]
        b_b = cs[8:16, L - 1:L] - cs[8:16] + lf_b
        z_f = i_f - b_f
        z_b = i_b - b_b
        g_f = b_f[:, L - 1:L] - b_f + i_f
        g_b = b_b[:, 0:1] - b_b + i_b
        mf, mb = z_f, z_b
        s = 1
        while s < L:
            mf = jnp.maximum(mf, jnp.where(lane8 >= s, pltpu.roll(mf, s, 1), NEG_INF))
            mb = jnp.maximum(mb, jnp.where(lane8 < L - s, pltpu.roll(mb, L - s, 1), NEG_INF))
            s *= 2
        mb = jnp.where(lane8 < L - 1, pltpu.roll(mb, L - 1, 1), NEG_INF)
        reps = [None if state_only and t not in (2, 5) else
                lax.dot_general(_split3(val), sel3[:, 0:n_used * 128], TN_DIMS, preferred_element_type=F32)
                for t, val in enumerate((mf, b_f, g_f, mb, b_b, g_b))]

        def rows_of(c):
            out = jnp.zeros((8, L), F32)
            for r, val in enumerate((z_f, z_b, g_f, g_b, b_f, b_b)):
                out = jnp.where(sub8 == r, val[c:c + 1], out)
            return out

        return rows_of, reps

    lat_rows, lat_reps = chunk_tables(gt_ref[0, 0], n_chunks, False)
    for c in range(n_chunks):
        row_ref[c] = lat_rows(c)
        for t in range(N_TAB):
            tab_ref[t, c * L:(c + 1) * L, :] = lat_reps[t][:, c * 128:(c + 1) * 128]

    def lanes2(x):
        return jnp.concatenate([x, x], axis=1)

    def advance(k, v, g_rep, g_row, b_last, c_ref, m_ref):
        m = m_ref[...]
        m_new = jnp.maximum(b_last + m, jnp.max(g_row, axis=-1, keepdims=True))
        kw = (k.astype(F32) * jnp.exp(lanes2(g_rep) - m_new)).astype(BF16)
        v_aug = jnp.concatenate([v, ones_cols], axis=1)
        c_ref[...] = jnp.exp(b_last + m - m_new) * c_ref[...] + lax.dot_general(
            kw, v_aug, TN_DIMS, preferred_element_type=F32)
        m_ref[...] = m_new

    for r in (cf_ref, mf_ref, cb_ref, mb_ref):
        r[...] = jnp.zeros_like(r)
    ctx_rows, ctx_reps = chunk_tables(cgt_ref[0, 0], n_ctx_chunks, True)
    for c in range(n_ctx_chunks):
        rows = ctx_rows(c)
        advance(ck_ref[0, c * L:(c + 1) * L, :], cv_ref[0, c * L:(c + 1) * L, :],
                ctx_reps[2][:, c * 128:(c + 1) * 128], rows[2:3], rows[4:5, L - 1:L], cf_ref, mf_ref)
    for c in reversed(range(n_ctx_chunks)):
        rows = ctx_rows(c)
        advance(ck_ref[0, c * L:(c + 1) * L, :], cv_ref[0, c * L:(c + 1) * L, :],
                ctx_reps[5][:, c * 128:(c + 1) * 128], rows[3:4], rows[5:6, 0:1], cb_ref, mb_ref)

    def state_pass(i, carry):
        cb = n_chunks - 1 - i
        rf = pl.multiple_of(i * L, L)
        rb = pl.multiple_of(cb * L, L)
        cfs_ref[i] = cf_ref[...].astype(BF16)
        mfs_ref[i] = mf_ref[...]
        cbs_ref[cb] = cb_ref[...].astype(BF16)
        mbs_ref[cb] = mb_ref[...]
        rows_f = row_ref[i]
        rows_b = row_ref[cb]
        advance(k_ref[0, pl.ds(rf, L), :], v_ref[0, pl.ds(rf, L), :], tab_ref[2, pl.ds(rf, L), :],
                rows_f[2:3], rows_f[4:5, L - 1:L], cf_ref, mf_ref)
        advance(k_ref[0, pl.ds(rb, L), :], v_ref[0, pl.ds(rb, L), :], tab_ref[5, pl.ds(rb, L), :],
                rows_b[3:4], rows_b[5:6, 0:1], cb_ref, mb_ref)
        return carry

    def finish_states():
        cfs_ref[n_chunks - 1] = cf_ref[...].astype(BF16)
        mfs_ref[n_chunks - 1] = mf_ref[...]
        cbs_ref[0] = cb_ref[...].astype(BF16)
        mbs_ref[0] = mb_ref[...]

    def direction(q, v_aug, s, z_row, zmax_rep, b_rep, mask, c_in, m_in):
        mx = jnp.maximum(zmax_rep, m_in)
        att = s * jnp.exp((z_row - lanes2(mx)) + mask)
        na = jnp.dot(att.astype(BF16), v_aug, preferred_element_type=F32)
        qa = jnp.dot(q, c_in, preferred_element_type=F32)
        a = jnp.exp(m_in - mx)
        num = na[:, 0:HEAD_W] + lanes2(a) * qa[:, 0:HEAD_W]
        den = na[:, HEAD_W:] + a * qa[:, HEAD_W:]
        scale = 1.0 / jnp.maximum(jnp.abs(den), jnp.exp(-(b_rep + mx)))
        return num * lanes2(scale)

    mask_ref[0] = jnp.where(ci <= ri, 0.0, NEG_INF)
    mask_ref[1] = jnp.where(ci > ri, 0.0, NEG_INF)

    def out_chunk(c):
        r0 = pl.multiple_of(c * L, L)
        q = q_ref[0, pl.ds(r0, L), :]
        k = k_ref[0, pl.ds(r0, L), :]
        v_aug = jnp.concatenate([v_ref[0, pl.ds(r0, L), :], ones_cols], axis=1)
        s = lax.dot_general(q, k, NT_DIMS, preferred_element_type=F32)
        rows = row_ref[c]
        tot = direction(q, v_aug, s, rows[0:1], tab_ref[0, pl.ds(r0, L), :], tab_ref[1, pl.ds(r0, L), :],
                        mask_ref[0], cfs_ref[c], mfs_ref[c])
        tot = tot + direction(q, v_aug, s, rows[1:2], tab_ref[3, pl.ds(r0, L), :],
                              tab_ref[4, pl.ds(r0, L), :], mask_ref[1], cbs_ref[c], mbs_ref[c])
        mo = mo_ref[0, pl.ds(r0, L), :].astype(F32)
        o_ref[0, pl.ds(r0, L), :] = (_layer_norm(tot) * jax.nn.sigmoid(mo)).astype(BF16)

    return state_pass, finish_states, out_chunk


def _scan_kernel(dl_ref, rq_ref, rk_ref, rv_ref, rg_ref, rck_ref, rcv_ref,
                 mq_ref, mk_ref, mv_ref, mo_ref, mck_ref, mcv_ref, gt_ref, cgt_ref,
                 r_ref, m_ref, *scratch):
    n_chunks = rq_ref.shape[1] // SCAN_L
    ret = _ret_build(dl_ref, rq_ref, rk_ref, rv_ref, rg_ref, rck_ref, rcv_ref, r_ref,
                     *scratch[:N_RET_SCRATCH])
    mls = _mlstm_build(mq_ref, mk_ref, mv_ref, mo_ref, mck_ref, mcv_ref, gt_ref, cgt_ref, m_ref,
                       *scratch[N_RET_SCRATCH:])

    def state_pass(i, carry):
        ret[0](i, carry)
        mls[0](i, carry)
        return carry

    lax.fori_loop(0, n_chunks - 1, state_pass, 0)
    ret[1]()
    mls[1]()

    def out_pass(i, carry):
        for c in (2 * i, 2 * i + 1):
            ret[2](c)
            mls[2](c)
        return carry

    lax.fori_loop(0, n_chunks // 2, out_pass, 0)


N_RET_SCRATCH = 6


def _scans(decay_logit, p_lat, p_ctx, gt, cgt, ret_lat, ret_ctx, ml_lat, ml_ctx):
    B, T, _ = p_lat.shape
    Tc = p_ctx.shape[1]
    assert T % (2 * SCAN_L) == 0 and Tc % SCAN_L == 0 and T // SCAN_L <= 8
    n_chunks = T // SCAN_L

    def lat(sec):
        return pl.BlockSpec((1, T, HEAD_W), lambda b, h: (b, 0, sec * HEADS + h))

    def cx(sec):
        return pl.BlockSpec((1, Tc, HEAD_W), lambda b, h: (b, 0, sec * HEADS + h))

    gates = pl.BlockSpec((1, 1, N_GK, 8, SCAN_L), lambda b, h: (b, h, 0, 0, 0))
    out = pl.BlockSpec((1, T, HEAD_W), lambda b, h: (b, 0, h))
    ret_scratch = [pltpu.VMEM((HEAD_W, HEAD_W), F32),
                   pltpu.VMEM((HEAD_W, HEAD_W), F32),
                   pltpu.VMEM((n_chunks, HEAD_W, HEAD_W), BF16),
                   pltpu.VMEM((n_chunks, HEAD_W, HEAD_W), BF16),
                   pltpu.VMEM((4, SCAN_L, HEAD_W), F32),
                   pltpu.VMEM((SCAN_L, SCAN_L), F32)]
    assert len(ret_scratch) == N_RET_SCRATCH
    state = [pltpu.VMEM((HEAD_W, AUG_W), F32), pltpu.VMEM((1, 1), F32)]
    snaps = [pltpu.VMEM((n_chunks, HEAD_W, AUG_W), BF16), pltpu.VMEM((n_chunks, 1, 1), F32)]
    ml_scratch = [pltpu.VMEM((N_TAB, T, 128), F32), pltpu.VMEM((n_chunks, 8, SCAN_L), F32)] \
        + state + state + snaps + snaps + [pltpu.VMEM((2, SCAN_L, SCAN_L), F32)]
    return pl.pallas_call(
        _scan_kernel,
        out_shape=(jax.ShapeDtypeStruct((B, T, BRANCH_W), BF16),
                   jax.ShapeDtypeStruct((B, T, BRANCH_W), BF16)),
        grid=(B, HEADS),
        in_specs=[pl.BlockSpec(memory_space=pltpu.SMEM)]
        + [lat(s) for s in ret_lat] + [cx(s) for s in ret_ctx]
        + [lat(s) for s in ml_lat] + [cx(s) for s in ml_ctx] + [gates, gates],
        out_specs=(out, out),
        scratch_shapes=ret_scratch + ml_scratch,
        compiler_params=_cparams("parallel", "parallel"),
        name="scans",
    )(decay_logit, *([p_lat] * 4), *([p_ctx] * 2), *([p_lat] * 4), *([p_ctx] * 2), gt, cgt)


def _mlstm_unused(p_lat, p_ctx, gt, cgt, sec_lat, sec_ctx):
    B, T, _ = p_lat.shape
    Tc = p_ctx.shape[1]
    assert T % (2 * SCAN_L) == 0 and Tc % SCAN_L == 0 and T // SCAN_L <= 8
    n_chunks = T // SCAN_L

    def lat(sec):
        return pl.BlockSpec((1, T, HEAD_W), lambda b, h: (b, 0, sec * HEADS + h))

    def cx(sec):
        return pl.BlockSpec((1, Tc, HEAD_W), lambda b, h: (b, 0, sec * HEADS + h))

    in_specs = [lat(s) for s in sec_lat] + [cx(s) for s in sec_ctx] + [
        pl.BlockSpec((1, 1, N_GK, 8, SCAN_L), lambda b, h: (b, h, 0, 0, 0)),
        pl.BlockSpec((1, 1, N_GK, 8, SCAN_L), lambda b, h: (b, h, 0, 0, 0)),
    ]
    state = [pltpu.VMEM((HEAD_W, AUG_W), F32), pltpu.VMEM((1, 1), F32)]
    snaps = [pltpu.VMEM((n_chunks, HEAD_W, AUG_W), BF16), pltpu.VMEM((n_chunks, 1, 1), F32)]
    return pl.pallas_call(
        _mlstm_kernel,
        out_shape=jax.ShapeDtypeStruct((B, T, BRANCH_W), BF16),
        grid=(B, HEADS),
        in_specs=in_specs,
        out_specs=pl.BlockSpec((1, T, HEAD_W), lambda b, h: (b, 0, h)),
        scratch_shapes=[pltpu.VMEM((N_TAB, T, 128), F32),
                        pltpu.VMEM((n_chunks, 8, SCAN_L), F32)]
        + state + state + snaps + snaps + [pltpu.VMEM((2, SCAN_L, SCAN_L), F32)],
        compiler_params=_cparams("parallel", "parallel"),
        name="mlstm",
    )(p_lat, p_lat, p_lat, p_lat, p_ctx, p_ctx, gt, cgt)


def _merge_kernel(alpha, r_ref, m_ref, gr_ref, gm_ref, x_ref, g1_ref, sh2_ref, sc2_ref,
                  lng_ref, lnb_ref, wr_ref, wm_ref, wo_ref, wrt_ref, brt_ref,
                  x1_ref, ua_ref, ub_ref, lt_ref):
    tm = x_ref.shape[1]
    for s in range(tm // MERGE_SUB):
        rows = slice(s * MERGE_SUB, (s + 1) * MERGE_SUB)
        yr = jnp.dot(r_ref[0, rows, :], wr_ref[...], preferred_element_type=F32)
        ym = jnp.dot(m_ref[0, rows, :], wm_ref[...], preferred_element_type=F32)
        y = (jax.nn.sigmoid(gr_ref[0, rows, :].astype(F32)) * yr
             + jax.nn.sigmoid(gm_ref[0, rows, :].astype(F32)) * ym)
        yo = jnp.dot(y.astype(BF16), wo_ref[...], preferred_element_type=F32)
        x1 = _layer_norm(alpha * x_ref[0, rows, :] + g1_ref[0] * yo) * lng_ref[...] + lnb_ref[...]
        x1_ref[0, rows, :] = x1
        u2 = _layer_norm(x1) * (1.0 + sc2_ref[0]) + sh2_ref[0]
        lt_ref[0, :, rows] = lax.dot_general(wrt_ref[...], u2.astype(BF16), NT_DIMS,
                                             preferred_element_type=F32) + brt_ref[...]
        ua_ref[0, rows, :] = _pack_pairs(u2[:, 0:PACK_W], u2[:, PACK_W:2 * PACK_W])
        ub_ref[0, rows, :] = _pack_pairs(u2[:, 2 * PACK_W:3 * PACK_W], u2[:, 3 * PACK_W:4 * PACK_W])


def _merge(alpha, b0, nb, r, m, p_lat, sec_gates, x, g1, sh2, sc2, lng, lnb, wr, wm, wo, wrt, brt):
    _, T, D = x.shape
    tm = MERGE_TM

    def tile(w):
        return pl.BlockSpec((1, tm, w), lambda b, i: (b + b0, i, 0))

    def out_tile(w):
        return pl.BlockSpec((1, tm, w), lambda b, i: (b, i, 0))

    def sec(s):
        return pl.BlockSpec((1, tm, BRANCH_W), lambda b, i: (b + b0, i, s))

    def mod():
        return pl.BlockSpec((1, 1, D), lambda b, i: (b + b0, 0, 0))

    def const(shape):
        return pl.BlockSpec(shape, lambda b, i: (0,) * len(shape))

    return pl.pallas_call(
        functools.partial(_merge_kernel, alpha),
        out_shape=(jax.ShapeDtypeStruct((nb, T, D), F32),
                   jax.ShapeDtypeStruct((nb, T, PACK_W), U32),
                   jax.ShapeDtypeStruct((nb, T, PACK_W), U32),
                   jax.ShapeDtypeStruct((nb, ROUTE_ROWS, T), F32)),
        grid=(nb, T // tm),
        in_specs=[tile(BRANCH_W), tile(BRANCH_W), sec(sec_gates[0]), sec(sec_gates[1]), tile(D),
                  mod(), mod(), mod(), const((1, D)), const((1, D)),
                  const((BRANCH_W, D)), const((BRANCH_W, D)), const((D, D)),
                  const((ROUTE_ROWS, D)), const((ROUTE_ROWS, 1))],
        out_specs=(out_tile(D), out_tile(PACK_W), out_tile(PACK_W),
                   pl.BlockSpec((1, ROUTE_ROWS, tm), lambda b, i: (b, 0, i))),
        compiler_params=_cparams("parallel", "parallel"),
        name="merge",
    )(r, m, p_lat, p_lat, x, g1, sh2, sc2, lng, lnb, wr, wm, wo, wrt, brt)


def _route_kernel(lt_ref, ri_ref, rw_ref, cnt_ref, carry_ref, u_ref):
    i = pl.program_id(0)
    tm = lt_ref.shape[2]

    @pl.when(i == 0)
    def _():
        carry_ref[...] = jnp.zeros_like(carry_ref)
        r = lax.broadcasted_iota(I32, (tm, tm), 0)
        c = lax.broadcasted_iota(I32, (tm, tm), 1)
        u_ref[...] = (r < c).astype(BF16)

    lt = lt_ref[0]
    lg = lt[0:N_GROUPS, :]
    eg = jnp.exp(lg - jnp.max(lg, axis=0, keepdims=True))
    pg = eg / jnp.sum(eg, axis=0, keepdims=True)
    pg_top = jnp.max(pg, axis=0, keepdims=True)
    rows_g = lax.broadcasted_iota(I32, pg.shape, 0)
    g_idx = jnp.min(jnp.where(pg == pg_top, rows_g, N_GROUPS), axis=0, keepdims=True)

    le = jnp.zeros((EXP_PER_GROUP, tm), F32)
    for g in range(N_GROUPS):
        lo = 8 + g * EXP_PER_GROUP
        le = jnp.where(g_idx == g, lt[lo:lo + EXP_PER_GROUP, :], le)
    ee = jnp.exp(le - jnp.max(le, axis=0, keepdims=True))
    pe = ee / jnp.sum(ee, axis=0, keepdims=True)
    rows_e = lax.broadcasted_iota(I32, pe.shape, 0)
    v1 = jnp.max(pe, axis=0, keepdims=True)
    i1 = jnp.min(jnp.where(pe == v1, rows_e, EXP_PER_GROUP), axis=0, keepdims=True)
    pe2 = jnp.where(rows_e == i1, -1.0, pe)
    v2 = jnp.max(pe2, axis=0, keepdims=True)
    i2 = jnp.min(jnp.where(pe2 == v2, rows_e, EXP_PER_GROUP), axis=0, keepdims=True)
    den = v1 + v2
    rw_ref[0:1, :] = pg_top * v1 / den
    rw_ref[1:2, :] = pg_top * v2 / den
    e1 = g_idx * EXP_PER_GROUP + i1
    e2 = g_idx * EXP_PER_GROUP + i2

    rows_x = lax.broadcasted_iota(I32, (N_EXPERTS, tm), 0)
    oh1 = (rows_x == e1).astype(F32)
    oh2 = (rows_x == e2).astype(F32)
    both = oh1 + oh2
    before = carry_ref[:, 0:1] + jnp.dot(both.astype(BF16), u_ref[...], preferred_element_type=F32)
    ri_ref[0:1, :] = e1
    ri_ref[1:2, :] = e2
    ri_ref[2:3, :] = jnp.sum(oh1 * before, axis=0, keepdims=True).astype(I32)
    ri_ref[3:4, :] = jnp.sum(oh2 * before, axis=0, keepdims=True).astype(I32)
    carry_ref[...] = carry_ref[...] + jnp.sum(both, axis=1, keepdims=True)
    cnt_ref[...] = carry_ref[...].astype(I32)


def _route(lt):
    B, _, T = lt.shape
    tm = ROUTE_TM
    per_b = T // tm
    n = B * T
    return pl.pallas_call(
        _route_kernel,
        out_shape=(jax.ShapeDtypeStruct((4, n), I32),
                   jax.ShapeDtypeStruct((2, n), F32),
                   jax.ShapeDtypeStruct((N_EXPERTS, 128), I32)),
        grid=(n // tm,),
        in_specs=[pl.BlockSpec((1, ROUTE_ROWS, tm), lambda i: (i // per_b, 0, i % per_b))],
        out_specs=(pl.BlockSpec((4, tm), lambda i: (0, i)),
                   pl.BlockSpec((2, tm), lambda i: (0, i)),
                   pl.BlockSpec((N_EXPERTS, 128), lambda i: (0, 0))),
        scratch_shapes=[pltpu.VMEM((N_EXPERTS, 128), F32), pltpu.VMEM((tm, tm), BF16)],
        compiler_params=_cparams("arbitrary"),
        name="route",
    )(lt)


def _sc_mesh():
    return plsc.VectorSubcoreMesh(core_axis_name="c", subcore_axis_name="s")


def _sc_scatter2(rows, idx0, idx1, n_out):
    m, w = rows.shape

    @functools.partial(pl.kernel, out_type=jax.ShapeDtypeStruct((n_out, w), rows.dtype),
                       mesh=_sc_mesh(), scratch_types=[])
    def k(x_hbm, i0_hbm, i1_hbm, o_hbm):
        def body(x_vmem, i0_vmem, i1_vmem):
            pltpu.sync_copy(x_vmem, o_hbm.at[i0_vmem.at[0]])
            pltpu.sync_copy(x_vmem, o_hbm.at[i1_vmem.at[0]])

        pltpu.emit_pipeline(
            body,
            grid=(m // SC_WIN,),
            in_specs=[pl.BlockSpec((SC_WIN, w), lambda i: (i, 0)),
                      pl.BlockSpec((1, SC_WIN), lambda i: (0, i)),
                      pl.BlockSpec((1, SC_WIN), lambda i: (0, i))],
            out_specs=[],
            core_axis_name=("c", "s"),
            dimension_semantics=(pltpu.PARALLEL,),
        )(x_hbm, i0_hbm, i1_hbm)

    return k(rows, idx0.reshape(1, m), idx1.reshape(1, m))


def _sc_gather(table, idx):
    m = idx.shape[0]
    w = table.shape[1]

    @functools.partial(pl.kernel, out_type=jax.ShapeDtypeStruct((m, w), table.dtype),
                       mesh=_sc_mesh(), scratch_types=[])
    def k(t_hbm, i_hbm, o_hbm):
        def body(i_vmem, o_vmem):
            pltpu.sync_copy(t_hbm.at[i_vmem.at[0]], o_vmem)

        pltpu.emit_pipeline(
            body,
            grid=(m // SC_WIN,),
            in_specs=[pl.BlockSpec((1, SC_WIN), lambda i: (0, i))],
            out_specs=[pl.BlockSpec((SC_WIN, w), lambda i: (i, 0))],
            core_axis_name=("c", "s"),
            dimension_semantics=(pltpu.PARALLEL,),
        )(i_hbm, o_hbm)

    return k(table, idx.reshape(1, m))


def _expert_kernel(be_ref, nv_ref, xa_ref, xb_ref, w1f_ref, w3f_ref, w2f_ref, ya_ref, yb_ref,
                   w1_ref, w3_ref, w2_ref):
    j = pl.program_id(0)
    nv = nv_ref[j]

    @pl.when(jnp.logical_or(j == 0, be_ref[j] != be_ref[jnp.maximum(j - 1, 0)]))
    def _():
        w1_ref[0] = w1f_ref[0].astype(BF16)
        w3_ref[0] = w3f_ref[0].astype(BF16)
        w2_ref[0] = w2f_ref[0].astype(BF16)

    @pl.when(nv > 0)
    def _():
        for s in range(xa_ref.shape[0] // MOE_SUB):
            rows = slice(s * MOE_SUB, (s + 1) * MOE_SUB)
            valid = lax.broadcasted_iota(I32, (MOE_SUB, PACK_W), 0) + s * MOE_SUB < nv
            zero = jnp.zeros((MOE_SUB, PACK_W), U32)
            parts = _unpack_pairs(jnp.where(valid, xa_ref[rows, :], zero)) + \
                _unpack_pairs(jnp.where(valid, xb_ref[rows, :], zero))
            x = jnp.concatenate([p.astype(BF16) for p in parts], axis=1)
            h1 = jnp.dot(x, w1_ref[0], preferred_element_type=F32)
            h3 = jnp.dot(x, w3_ref[0], preferred_element_type=F32)
            y = jnp.dot((_silu(h1) * h3).astype(BF16), w2_ref[0], preferred_element_type=F32)
            ya_ref[rows, :] = _pack_pairs(y[:, 0:PACK_W], y[:, PACK_W:2 * PACK_W])
            yb_ref[rows, :] = _pack_pairs(y[:, 2 * PACK_W:3 * PACK_W], y[:, 3 * PACK_W:4 * PACK_W])

    @pl.when(nv == 0)
    def _():
        ya_ref[...] = jnp.zeros_like(ya_ref)
        yb_ref[...] = jnp.zeros_like(yb_ref)


def _experts(block_exp, n_valid, xa, xb, w1, w3, w2):
    n_slots = xa.shape[0]
    n_blocks = n_slots // MOE_BLK
    d, de = w1.shape[1], w1.shape[2]
    slot = pl.BlockSpec((MOE_BLK, PACK_W), lambda j, be, nv: (j, 0))
    grid_spec = pltpu.PrefetchScalarGridSpec(
        num_scalar_prefetch=2,
        grid=(n_blocks,),
        in_specs=[slot, slot,
                  pl.BlockSpec((1, d, de), lambda j, be, nv: (be[j], 0, 0)),
                  pl.BlockSpec((1, d, de), lambda j, be, nv: (be[j], 0, 0)),
                  pl.BlockSpec((1, de, d), lambda j, be, nv: (be[j], 0, 0))],
        out_specs=(slot, slot),
        scratch_shapes=[pltpu.VMEM((1, d, de), BF16), pltpu.VMEM((1, d, de), BF16),
                        pltpu.VMEM((1, de, d), BF16)],
    )
    return pl.pallas_call(
        _expert_kernel,
        out_shape=(jax.ShapeDtypeStruct((n_slots, PACK_W), U32),
                   jax.ShapeDtypeStruct((n_slots, PACK_W), U32)),
        grid_spec=grid_spec,
        compiler_params=_cparams("arbitrary"),
        name="experts",
    )(block_exp, n_valid, xa, xb, w1, w3, w2)


def _final_kernel(alpha, x1_ref, a0_ref, b0_ref, a1_ref, b1_ref, w_ref, g2_ref, lng_ref, lnb_ref,
                  *rest):
    o_ref = rest[-1]
    w = w_ref[...]
    w0 = w[:, 0:1]
    w1 = w[:, 1:2]
    parts0 = _unpack_pairs(a0_ref[...]) + _unpack_pairs(b0_ref[...])
    parts1 = _unpack_pairs(a1_ref[...]) + _unpack_pairs(b1_ref[...])
    f = jnp.concatenate([w0 * p0 + w1 * p1 for p0, p1 in zip(parts0, parts1)], axis=1)
    o_ref[0] = _layer_norm(alpha * x1_ref[0] + g2_ref[0] * f) * lng_ref[...] + lnb_ref[...]


def _final(alpha, b0, n_batch, x1, ya, yb, w, g2, lng, lnb, out_prev):
    nb, T, D = x1.shape
    tm = MERGE_TM
    per_b = T // tm
    n_tiles = nb * per_b

    def rows(k):
        return pl.BlockSpec((tm, PACK_W), lambda b, i: (k * n_tiles + b * per_b + i, 0))

    in_specs = [pl.BlockSpec((1, tm, D), lambda b, i: (b, i, 0)),
                rows(0), rows(0), rows(1), rows(1),
                pl.BlockSpec((tm, 2), lambda b, i: (b * per_b + i, 0)),
                pl.BlockSpec((1, 1, D), lambda b, i: (b + b0, 0, 0)),
                pl.BlockSpec((1, D), lambda b, i: (0, 0)),
                pl.BlockSpec((1, D), lambda b, i: (0, 0))]
    args = [x1, ya, yb, ya, yb, w, g2, lng, lnb]
    aliases = {}
    if out_prev is not None:
        in_specs.append(pl.BlockSpec(memory_space=pl.ANY))
        args.append(out_prev)
        aliases = {len(args) - 1: 0}
    return pl.pallas_call(
        functools.partial(_final_kernel, alpha),
        out_shape=jax.ShapeDtypeStruct((n_batch, T, D), F32),
        grid=(nb, per_b),
        in_specs=in_specs,
        out_specs=pl.BlockSpec((1, tm, D), lambda b, i: (b + b0, i, 0)),
        input_output_aliases=aliases,
        compiler_params=_cparams("parallel", "parallel"),
        name="final",
    )(*args)


def _rotary_tables(T):
    quarter = HEAD_W // 4
    freqs = ROPE_BASE ** (-jnp.arange(quarter, dtype=F32) / quarter)
    t = jnp.arange(T)
    ang_r = (t // GRID_W).astype(F32)[:, None] * freqs[None, :]
    ang_c = (t % GRID_W).astype(F32)[:, None] * freqs[None, :]
    cos = jnp.concatenate([jnp.cos(ang_r)] * 2 + [jnp.cos(ang_c)] * 2, axis=1)
    sin = jnp.concatenate([-jnp.sin(ang_r), jnp.sin(ang_r), -jnp.sin(ang_c), jnp.sin(ang_c)], axis=1)
    return cos, sin


def _per_head_gates(gt):
    B, _, T = gt.shape
    n_chunks = T // SCAN_L
    gth = gt.reshape(B, N_GK, HEADS, n_chunks, SCAN_L).transpose(0, 2, 1, 3, 4)
    return jnp.pad(gth, ((0, 0), (0, 0), (0, 0), (0, 8 - n_chunks), (0, 0)))


def _table_lookup(table, idx):
    sel = idx[..., None] == jnp.arange(table.shape[0], dtype=idx.dtype)
    return jnp.sum(jnp.where(sel, table, 0), axis=-1)


def kernel(x, c, ctx, c_ctx, w_ada, b_ada, w_in, b_mgate, ml_conv_w, ml_conv_b, ret_decay_logit, w_ret_branch, w_ml_branch, w_out, ln1_g, ln1_b, w_rg, b_rg, w_re, b_re, w_e1, w_e3, w_e2, ln2_g, ln2_b):
    B, T, D = x.shape
    depth = w_ada.shape[0]
    assert depth == 1 and D == BRANCH_W and T % min(PROJ_TM, T) == 0 and T % GRID_W == 0
    alpha = (2 * depth) ** 0.25

    n_rows = -(-(B + 1) // 8) * 8
    cs = jnp.zeros((n_rows, D), F32).at[:B].set(c).at[B].set(c_ctx)
    mod = _ada(cs, w_ada[0], b_ada[0][None, :])
    sh1, sc1, g1, sh2, sc2, g2 = [mod[:B, None, i * D:(i + 1) * D] for i in range(6)]
    csh1 = mod[B, 0 * D:1 * D].reshape(1, 1, D)
    csc1 = mod[B, 1 * D:2 * D].reshape(1, 1, D)

    w = w_in[0]
    sec_w = [w[:, s * BRANCH_W:(s + 1) * BRANCH_W] for s in range(8)]
    g_lo = 8 * BRANCH_W
    w_gate_t = w[:, g_lo:g_lo + N_GATES].T.astype(BF16)
    b_gate = b_mgate[0][:, None]
    sec_w += [w[:, g_lo + N_GATES:g_lo + N_GATES + D], w[:, g_lo + N_GATES + D:]]
    w_lat = jnp.concatenate(sec_w, axis=1).astype(BF16)
    w_ctx = jnp.concatenate([sec_w[1], sec_w[2], sec_w[5], sec_w[6]], axis=1).astype(BF16)
    kinds_lat = ("rot", "rot_scale", "plain", "plain", "conv", "conv_scale") + ("plain",) * 4
    kinds_ctx = ("scale", "plain", "conv_scale", "plain")
    conv_w, conv_b = ml_conv_w[0], ml_conv_b[0][None, :]
    p_lat, gt_lat = _proj(x, sh1, sc1, w_lat, w_gate_t, b_gate, conv_w, conv_b, kinds_lat, T,
                          _rotary_tables(T))
    Tc = ctx.shape[1]
    p_ctx, gt_ctx = _proj(ctx.reshape(1, B * Tc, D), csh1, csc1, w_ctx, w_gate_t, b_gate,
                          conv_w, conv_b, kinds_ctx, Tc)
    p_ctx = p_ctx.reshape(B, Tc, -1)
    gt_ctx = gt_ctx.reshape(N_GATES, B, Tc).transpose(1, 0, 2)

    ret, mls = _scans(ret_decay_logit[0], p_lat, p_ctx, _per_head_gates(gt_lat), _per_head_gates(gt_ctx),
                      (0, 1, 2, 3), (0, 1), (4, 5, 6, 7), (2, 3))

    wrt = jnp.zeros((ROUTE_ROWS, D), F32).at[:N_GROUPS].set(w_rg[0].T).at[8:8 + N_EXPERTS].set(w_re[0].T)
    brt = jnp.zeros((ROUTE_ROWS, 1), F32).at[:N_GROUPS, 0].set(b_rg[0]).at[8:8 + N_EXPERTS, 0].set(b_re[0])
    merge_w = (ln1_g[0][None, :], ln1_b[0][None, :], w_ret_branch[0].astype(BF16),
               w_ml_branch[0].astype(BF16), w_out[0].astype(BF16), wrt.astype(BF16), brt)

    n_split = MOE_SPLIT if B % MOE_SPLIT == 0 else 1
    nb = B // n_split
    n_tok = nb * T
    n_blocks = (2 * n_tok) // MOE_BLK + N_EXPERTS
    n_slots = n_blocks * MOE_BLK
    block_start = jnp.arange(n_blocks, dtype=I32) * MOE_BLK

    def dispatch(b0):
        x1, ua, ub, lt = _merge(alpha, b0, nb, ret, mls, p_lat, (8, 9), x, g1, sh2, sc2, *merge_w)
        ri, rw, cnt = _route(lt)
        counts = cnt[:, 0]
        padded = (counts + MOE_BLK - 1) // MOE_BLK * MOE_BLK
        pad_end = jnp.cumsum(padded)
        pad_off = pad_end - padded
        dest = _table_lookup(pad_off, ri[0:2]) + ri[2:4]
        block_exp = jnp.minimum((block_start[:, None] >= pad_end[None, :]).sum(1),
                                N_EXPERTS - 1).astype(I32)
        n_valid = jnp.clip(_table_lookup(counts, block_exp)
                           - (block_start - _table_lookup(pad_off, block_exp)), 0, MOE_BLK).astype(I32)
        xa = _sc_scatter2(ua.reshape(n_tok, PACK_W), dest[0], dest[1], n_slots)
        xb = _sc_scatter2(ub.reshape(n_tok, PACK_W), dest[0], dest[1], n_slots)
        return x1, rw, dest, block_exp, n_valid, xa, xb

    def combine(d):
        x1, rw, dest, block_exp, n_valid, xa, xb = d
        ya, yb = _experts(block_exp, n_valid, xa, xb, w_e1[0], w_e3[0], w_e2[0])
        dflat = dest.reshape(2 * n_tok)
        return x1, rw, _sc_gather(ya, dflat), _sc_gather(yb, dflat)

    dispatched = [dispatch(s * nb) for s in range(n_split)]
    combined = [combine(d) for d in dispatched]
    out = None
    for s, (x1, rw, ga, gb) in enumerate(combined):
        out = _final(alpha, s * nb, B, x1, ga, gb, rw.T, g2, ln2_g[0][None, :], ln2_b[0][None, :], out)
    return out
```

```python
import functools

import jax
import jax.numpy as jnp
from jax import lax
from jax.experimental import pallas as pl
from jax.experimental.pallas import tpu as pltpu
from jax.experimental.pallas import tpu_sc as plsc

F32 = jnp.float32
BF16 = jnp.bfloat16
U32 = jnp.uint32
I32 = jnp.int32
HIGHEST = lax.Precision.HIGHEST

HEADS = 4
HEAD_W = 256
BRANCH_W = HEADS * HEAD_W
GRID_W = 64
ROPE_BASE = 10000.0
N_GATES = 16
N_GK = N_GATES // HEADS
N_GROUPS = 4
EXP_PER_GROUP = 8
N_EXPERTS = N_GROUPS * EXP_PER_GROUP
LN_EPS = 1e-5
NEG_INF = -1e30
KEY_SCALE = HEAD_W ** -0.5

SCAN_L = 256
PROJ_TM = 2048
PROJ_SUB = 256
MERGE_TM = 512
MERGE_SUB = 512
MOE_SUB = 256
ROUTE_TM = 512
MOE_BLK = 512
MOE_SPLIT = 1
SC_WIN = 128
PACK_W = 256
ROUTE_ROWS = 64
VMEM_LIMIT = 48 * 1024 * 1024

NT_DIMS = (((1,), (1,)), ((), ()))
TN_DIMS = (((0,), (0,)), ((), ()))


def _cparams(*sem):
    return pltpu.CompilerParams(dimension_semantics=sem, vmem_limit_bytes=VMEM_LIMIT)


def _layer_norm(x):
    mu = jnp.mean(x, axis=-1, keepdims=True)
    xc = x - mu
    var = jnp.mean(xc * xc, axis=-1, keepdims=True)
    return xc * lax.rsqrt(var + LN_EPS)


def _log_sigmoid(x):
    return jnp.minimum(x, 0.0) - jnp.log1p(jnp.exp(-jnp.abs(x)))


def _silu(x):
    return x * jax.nn.sigmoid(x)


def _pack_pairs(hi, lo):
    hb = lax.bitcast_convert_type(hi.astype(BF16).astype(F32), U32)
    lb = lax.bitcast_convert_type(lo.astype(BF16).astype(F32), U32)
    return (hb & jnp.uint32(0xFFFF0000)) | (lb >> 16)


def _unpack_pairs(p):
    hi = lax.bitcast_convert_type(p & jnp.uint32(0xFFFF0000), F32)
    lo = lax.bitcast_convert_type(p << 16, F32)
    return hi, lo


def _ada_kernel(c_ref, w_ref, b_ref, o_ref):
    s = _silu(c_ref[...])
    o_ref[...] = jnp.dot(s, w_ref[...], precision=HIGHEST, preferred_element_type=F32) + b_ref[...]


def _ada(cs, w, b):
    rows, d = cs.shape
    cols = w.shape[1]
    tn = 1024
    return pl.pallas_call(
        _ada_kernel,
        out_shape=jax.ShapeDtypeStruct((rows, cols), F32),
        grid=(cols // tn,),
        in_specs=[pl.BlockSpec((rows, d), lambda j: (0, 0)),
                  pl.BlockSpec((d, tn), lambda j: (0, j)),
                  pl.BlockSpec((1, tn), lambda j: (0, j))],
        out_specs=pl.BlockSpec((rows, tn), lambda j: (0, j)),
        compiler_params=_cparams("parallel"),
        name="ada",
    )(cs, w, b)


def _proj_kernel(kinds, seq_len, x_ref, sh_ref, sc_ref, w_ref, wg_ref, bg_ref, cw_ref, cb_ref, *rest):
    if "rot" in kinds or "rot_scale" in kinds:
        cos_ref, sin_ref, o_ref, gt_ref, u_ref = rest
    else:
        o_ref, gt_ref, u_ref = rest
    j = pl.program_id(2)
    tm = x_ref.shape[1]
    sub = min(PROJ_SUB, tm)
    n_sub = tm // sub
    assert kinds[0] not in ("conv", "conv_scale") and seq_len % sub == 0

    def conv_section(kind):
        c0 = 0 if kind == "conv" else BRANCH_W
        w = cw_ref[:, c0:c0 + BRANCH_W]
        b = cb_ref[:, c0:c0 + BRANCH_W]
        sub8 = lax.broadcasted_iota(I32, (8, BRANCH_W), 0)
        zero_row = jnp.zeros((1, BRANCH_W), F32)
        accs = []

        def finish(r):
            a = accs[r]
            seq_start = (r * sub) % seq_len == 0
            seq_end = ((r + 1) * sub) % seq_len == 0
            before = zero_row if seq_start else accs[r - 1][sub - 1:sub, :]
            after = zero_row if seq_end else accs[r + 1][0:1, :]
            prev = pltpu.roll(a, 1, 0)
            prev = jnp.concatenate([jnp.where(sub8 == 0, before, prev[0:8, :]), prev[8:, :]], axis=0)
            nxt = pltpu.roll(a, sub - 1, 0)
            nxt = jnp.concatenate([nxt[:sub - 8, :], jnp.where(sub8 == 7, after, nxt[sub - 8:, :])], axis=0)
            y = _silu(prev * w[0:1, :] + a * w[1:2, :] + nxt * w[2:3, :] + b)
            if kind == "conv_scale":
                y = y * KEY_SCALE
            o_ref[0, r * sub:(r + 1) * sub, :] = y.astype(BF16)

        for r in range(n_sub):
            accs.append(jnp.dot(u_ref[r * sub:(r + 1) * sub, :], w_ref[...], preferred_element_type=F32))
            if r >= 1:
                finish(r - 1)
        finish(n_sub - 1)

    def rotary(acc, rows, scale):
        for s in range(acc.shape[1] // 128):
            a = acc[:, s * 128:(s + 1) * 128]
            half = s % 2
            cs = cos_ref[rows, half * 128:(half + 1) * 128]
            sn = sin_ref[rows, half * 128:(half + 1) * 128]
            r = a * cs + pltpu.roll(a, 64, 1) * sn
            if scale != 1.0:
                r = r * scale
            o_ref[0, rows, s * 128:(s + 1) * 128] = r.astype(BF16)

    def section(kind, first):
        if kind in ("conv", "conv_scale"):
            conv_section(kind)
            return
        for r in range(n_sub):
            rows = slice(r * sub, (r + 1) * sub)
            if first:
                u = _layer_norm(x_ref[0, rows, :]) * (1.0 + sc_ref[0]) + sh_ref[0]
                ub = u.astype(BF16)
                u_ref[rows, :] = ub
                gt_ref[0, :, rows] = lax.dot_general(wg_ref[...], ub, NT_DIMS,
                                                     preferred_element_type=F32) + bg_ref[...]
            else:
                ub = u_ref[rows, :]
            acc = jnp.dot(ub, w_ref[...], preferred_element_type=F32)
            if kind == "rot":
                rotary(acc, rows, 1.0)
            elif kind == "rot_scale":
                rotary(acc, rows, KEY_SCALE)
            elif kind == "scale":
                o_ref[0, rows, :] = (acc * KEY_SCALE).astype(BF16)
            else:
                o_ref[0, rows, :] = acc.astype(BF16)

    variants = {}
    for s, kind in enumerate(kinds):
        variants.setdefault((kind, s == 0), []).append(s)
    for (kind, first), secs in variants.items():
        cond = functools.reduce(jnp.logical_or, [j == s for s in secs])

        @pl.when(cond)
        def _(kind=kind, first=first):
            section(kind, first)


def _proj(x, sh, sc, w_main, w_gate_t, b_gate, conv_w, conv_b, kinds, seq_len, tables=None):
    B, T, D = x.shape
    n_sec = len(kinds)
    tm = min(PROJ_TM, T)
    tn = BRANCH_W
    assert tm % seq_len == 0 and T % tm == 0
    in_specs = [
        pl.BlockSpec((1, tm, D), lambda i, b, j: (b, i, 0)),
        pl.BlockSpec((1, 1, D), lambda i, b, j: (b, 0, 0)),
        pl.BlockSpec((1, 1, D), lambda i, b, j: (b, 0, 0)),
        pl.BlockSpec((D, tn), lambda i, b, j: (0, j)),
        pl.BlockSpec((N_GATES, D), lambda i, b, j: (0, 0)),
        pl.BlockSpec((N_GATES, 1), lambda i, b, j: (0, 0)),
        pl.BlockSpec(conv_w.shape, lambda i, b, j: (0, 0)),
        pl.BlockSpec(conv_b.shape, lambda i, b, j: (0, 0)),
    ]
    args = [x, sh, sc, w_main, w_gate_t, b_gate, conv_w, conv_b]
    if tables is not None:
        in_specs += [pl.BlockSpec((tm, HEAD_W), lambda i, b, j: (i, 0))] * 2
        args += list(tables)
    return pl.pallas_call(
        functools.partial(_proj_kernel, kinds, seq_len),
        out_shape=(jax.ShapeDtypeStruct((B, T, n_sec * tn), BF16),
                   jax.ShapeDtypeStruct((B, N_GATES, T), F32)),
        grid=(T // tm, B, n_sec),
        in_specs=in_specs,
        out_specs=(pl.BlockSpec((1, tm, tn), lambda i, b, j: (b, i, j)),
                   pl.BlockSpec((1, N_GATES, tm), lambda i, b, j: (b, 0, i))),
        scratch_shapes=[pltpu.VMEM((tm, D), BF16)],
        compiler_params=_cparams("parallel", "parallel", "arbitrary"),
        name="proj_lat" if tables is not None else "proj_ctx",
    )(*args)


def _ret_build(dl_ref, q_ref, k_ref, v_ref, rg_ref, ck_ref, cv_ref, o_ref,
               sf_ref, sb_ref, fs_ref, bs_ref, dec_ref, d_ref):
    h = pl.program_id(1)
    L = SCAN_L
    n_chunks = q_ref.shape[1] // L
    n_ctx_chunks = ck_ref.shape[1] // L
    lgf = _log_sigmoid(jnp.full((1, 1), dl_ref[0, h], F32))
    lgb = _log_sigmoid(jnp.full((1, 1), dl_ref[1, h], F32))

    ri = lax.broadcasted_iota(I32, (L, L), 0)
    ci = lax.broadcasted_iota(I32, (L, L), 1)
    rel = (ri - ci).astype(F32)
    d_ref[...] = jnp.where(rel >= 0.0, jnp.exp(jnp.maximum(rel, 0.0) * lgf),
                           jnp.exp(jnp.maximum(-rel, 0.0) * lgb))
    row = lax.broadcasted_iota(I32, (L, HEAD_W), 0).astype(F32)
    dec_ref[0] = jnp.exp((row + 1.0) * lgf)
    dec_ref[1] = jnp.exp((L - 1.0 - row) * lgf)
    dec_ref[2] = jnp.exp((L - row) * lgb)
    dec_ref[3] = jnp.exp(row * lgb)
    cdf = jnp.exp(L * lgf)
    cdb = jnp.exp(L * lgb)

    def update(s_ref, kc, vc, kd, cd):
        kdec = (kc.astype(F32) * kd).astype(BF16)
        s_ref[...] = s_ref[...] * cd + lax.dot_general(kdec, vc, TN_DIMS, preferred_element_type=F32)

    sf_ref[...] = jnp.zeros_like(sf_ref)
    sb_ref[...] = jnp.zeros_like(sb_ref)
    for c in range(n_ctx_chunks):
        update(sf_ref, ck_ref[0, c * L:(c + 1) * L, :], cv_ref[0, c * L:(c + 1) * L, :], dec_ref[1], cdf)
    for c in reversed(range(n_ctx_chunks)):
        update(sb_ref, ck_ref[0, c * L:(c + 1) * L, :], cv_ref[0, c * L:(c + 1) * L, :], dec_ref[3], cdb)

    def state_pass(i, carry):
        cb = n_chunks - 1 - i
        rf = pl.multiple_of(i * L, L)
        rb = pl.multiple_of(cb * L, L)
        fs_ref[i] = sf_ref[...].astype(BF16)
        bs_ref[cb] = sb_ref[...].astype(BF16)
        update(sf_ref, k_ref[0, pl.ds(rf, L), :], v_ref[0, pl.ds(rf, L), :], dec_ref[1], cdf)
        update(sb_ref, k_ref[0, pl.ds(rb, L), :], v_ref[0, pl.ds(rb, L), :], dec_ref[3], cdb)
        return carry

    def finish_states():
        fs_ref[n_chunks - 1] = sf_ref[...].astype(BF16)
        bs_ref[0] = sb_ref[...].astype(BF16)

    def out_chunk(c):
        r0 = pl.multiple_of(c * L, L)
        q = q_ref[0, pl.ds(r0, L), :]
        k = k_ref[0, pl.ds(r0, L), :]
        v = v_ref[0, pl.ds(r0, L), :]
        s = lax.dot_general(q, k, NT_DIMS, preferred_element_type=F32)
        att = (s * d_ref[...]).astype(BF16)
        o = jnp.dot(att, v, preferred_element_type=F32)
        o = o + jnp.dot(q, fs_ref[c], preferred_element_type=F32) * dec_ref[0]
        o = o + jnp.dot(q, bs_ref[c], preferred_element_type=F32) * dec_ref[2]
        rg = rg_ref[0, pl.ds(r0, L), :].astype(F32)
        o_ref[0, pl.ds(r0, L), :] = (_layer_norm(o) * _silu(rg)).astype(BF16)

    return state_pass, finish_states, out_chunk


def _retention_unused(decay_logit, p_lat, p_ctx, sec_lat, sec_ctx):
    B, T, _ = p_lat.shape
    Tc = p_ctx.shape[1]
    assert T % (2 * SCAN_L) == 0 and Tc % SCAN_L == 0
    n_chunks = T // SCAN_L

    def lat(sec):
        return pl.BlockSpec((1, T, HEAD_W), lambda b, h: (b, 0, sec * HEADS + h))

    def cx(sec):
        return pl.BlockSpec((1, Tc, HEAD_W), lambda b, h: (b, 0, sec * HEADS + h))

    return pl.pallas_call(
        _ret_kernel,
        out_shape=jax.ShapeDtypeStruct((B, T, BRANCH_W), BF16),
        grid=(B, HEADS),
        in_specs=[pl.BlockSpec(memory_space=pltpu.SMEM)]
        + [lat(s) for s in sec_lat] + [cx(s) for s in sec_ctx],
        out_specs=pl.BlockSpec((1, T, HEAD_W), lambda b, h: (b, 0, h)),
        scratch_shapes=[pltpu.VMEM((HEAD_W, HEAD_W), F32),
                        pltpu.VMEM((HEAD_W, HEAD_W), F32),
                        pltpu.VMEM((n_chunks, HEAD_W, HEAD_W), BF16),
                        pltpu.VMEM((n_chunks, HEAD_W, HEAD_W), BF16),
                        pltpu.VMEM((4, SCAN_L, HEAD_W), F32),
                        pltpu.VMEM((SCAN_L, SCAN_L), F32)],
        compiler_params=_cparams("parallel", "parallel"),
        name="retention",
    )(decay_logit, p_lat, p_lat, p_lat, p_lat, p_ctx, p_ctx)


def _mlstm_kernel_old(q_ref, k_ref, v_ref, mo_ref, ck_ref, cv_ref, g_ref, gt_ref, cg_ref, cgt_ref,
                  wq_ref, bq_ref, wk_ref, bk_ref, o_ref,
                  xf_ref, qs_ref, ks_ref, cks_ref,
                  glc_ref, csc_ref, rcc_ref, glr_ref, csr_ref, rcr_ref,
                  cf_ref, nf_ref, mf_ref, cb_ref, nb_ref, mb_ref,
                  cfs_ref, nfs_ref, mfs_ref, cbs_ref, nbs_ref, mbs_ref):
    L = SCAN_L
    T = q_ref.shape[1]
    Tc = ck_ref.shape[1]
    n_chunks = T // L
    n_ctx_chunks = Tc // L
    CV = 128

    def conv_silu(src_ref, w_ref, b_ref, dst_ref, t_len, scale):
        xf_ref[pl.ds(0, 8), :] = jnp.zeros((8, HEAD_W), F32)
        xf_ref[pl.ds(8 + t_len, 8), :] = jnp.zeros((8, HEAD_W), F32)
        xf_ref[pl.ds(8, t_len), :] = src_ref[0].astype(F32)
        w = w_ref[...]
        b = b_ref[...]

        def body(c, carry):
            r0 = pl.multiple_of(c * CV, CV)
            win = xf_ref[pl.ds(r0, CV + 16), :]
            prev = pltpu.roll(win, 1, 0)[8:8 + CV, :]
            cur = win[8:8 + CV, :]
            nxt = pltpu.roll(win, CV + 15, 0)[8:8 + CV, :]
            y = prev * w[0:1, :] + cur * w[1:2, :] + nxt * w[2:3, :] + b
            y = _silu(y)
            if scale != 1.0:
                y = y * scale
            dst_ref[pl.ds(r0, CV), :] = y.astype(BF16)
            return carry

        lax.fori_loop(0, t_len // CV, body, 0)

    conv_silu(q_ref, wq_ref, bq_ref, qs_ref, T, 1.0)
    conv_silu(k_ref, wk_ref, bk_ref, ks_ref, T, KEY_SCALE)
    conv_silu(ck_ref, wk_ref, bk_ref, cks_ref, Tc, KEY_SCALE)

    ri = lax.broadcasted_iota(I32, (L, L), 0)
    ci = lax.broadcasted_iota(I32, (L, L), 1)
    tri_l = (ci <= ri).astype(F32)
    tri_u = (ci >= ri).astype(F32)

    def gate_tables(gc, gtc):
        lane = lax.broadcasted_iota(I32, gc.shape, 1)
        gl = jnp.where(lane % 2 == 1, _log_sigmoid(gc), gc)
        sub = lax.broadcasted_iota(I32, gtc.shape, 0)
        gtl = jnp.where(sub % 2 == 1, _log_sigmoid(gtc), gtc)
        cs_col = jnp.dot(tri_l, gl, precision=HIGHEST, preferred_element_type=F32)
        rc_col = jnp.dot(tri_u, gl, precision=HIGHEST, preferred_element_type=F32)
        cs_row = jnp.dot(gtl, tri_u, precision=HIGHEST, preferred_element_type=F32)
        rc_row = jnp.dot(gtl, tri_l, precision=HIGHEST, preferred_element_type=F32)
        return gl, cs_col, rc_col, gtl, cs_row, rc_row

    def pick(tables, backward):
        gl, cs_col, rc_col, gtl, cs_row, rc_row = tables
        if not backward:
            return gl[:, 0:1], gtl[0:1, :], cs_col[:, 1:2], cs_row[1:2, :], cs_row[1:2, L - 1:L]
        return gl[:, 2:3], gtl[2:3, :], rc_col[:, 3:4], rc_row[3:4, :], rc_row[3:4, 0:1]

    def lat_tables(r0):
        return (glc_ref[pl.ds(r0, L), :], csc_ref[pl.ds(r0, L), :], rcc_ref[pl.ds(r0, L), :],
                glr_ref[:, pl.ds(r0, L)], csr_ref[:, pl.ds(r0, L)], rcr_ref[:, pl.ds(r0, L)])

    def table_pass(c, carry):
        r0 = pl.multiple_of(c * L, L)
        gl, cs_col, rc_col, gtl, cs_row, rc_row = gate_tables(
            g_ref[0, 0, pl.ds(r0, L), :], gt_ref[0, 0, :, pl.ds(r0, L)])
        glc_ref[pl.ds(r0, L), :] = gl
        csc_ref[pl.ds(r0, L), :] = cs_col
        rcc_ref[pl.ds(r0, L), :] = rc_col
        glr_ref[:, pl.ds(r0, L)] = gtl
        csr_ref[:, pl.ds(r0, L)] = cs_row
        rcr_ref[:, pl.ds(r0, L)] = rc_row
        return carry

    lax.fori_loop(0, n_chunks, table_pass, 0)

    def advance(k, v, gate_vecs, c_ref, n_ref, m_ref):
        i_col, i_row, b_col, b_row, b_last = gate_vecs
        m = m_ref[...]
        g_col = b_last - b_col + i_col
        g_row = b_last - b_row + i_row
        m_new = jnp.maximum(b_last + m, jnp.max(g_row, axis=-1, keepdims=True))
        kw = k.astype(F32) * jnp.exp(g_col - m_new)
        decay = jnp.exp(b_last + m - m_new)
        c_ref[...] = decay * c_ref[...] + lax.dot_general(kw.astype(BF16), v, TN_DIMS,
                                                          preferred_element_type=F32)
        n_ref[...] = decay * n_ref[...] + jnp.sum(kw, axis=0, keepdims=True)
        m_ref[...] = m_new

    for refs in ((cf_ref, nf_ref, mf_ref), (cb_ref, nb_ref, mb_ref)):
        for r in refs:
            r[...] = jnp.zeros_like(r)
    ctx_tabs = [gate_tables(cg_ref[0, 0, c * L:(c + 1) * L, :], cgt_ref[0, 0, :, c * L:(c + 1) * L])
                for c in range(n_ctx_chunks)]
    for c in range(n_ctx_chunks):
        advance(cks_ref[c * L:(c + 1) * L, :], cv_ref[0, c * L:(c + 1) * L, :],
                pick(ctx_tabs[c], False), cf_ref, nf_ref, mf_ref)
    for c in reversed(range(n_ctx_chunks)):
        advance(cks_ref[c * L:(c + 1) * L, :], cv_ref[0, c * L:(c + 1) * L, :],
                pick(ctx_tabs[c], True), cb_ref, nb_ref, mb_ref)

    def snapshot(c, src, dst):
        dst[0][c] = src[0][...].astype(BF16)
        dst[1][c] = src[1][...]
        dst[2][c] = src[2][...]

    fwd_run, fwd_snap = (cf_ref, nf_ref, mf_ref), (cfs_ref, nfs_ref, mfs_ref)
    bwd_run, bwd_snap = (cb_ref, nb_ref, mb_ref), (cbs_ref, nbs_ref, mbs_ref)

    def state_pass(i, carry):
        cb = n_chunks - 1 - i
        rf = pl.multiple_of(i * L, L)
        rb = pl.multiple_of(cb * L, L)
        snapshot(i, fwd_run, fwd_snap)
        snapshot(cb, bwd_run, bwd_snap)
        advance(ks_ref[pl.ds(rf, L), :], v_ref[0, pl.ds(rf, L), :], pick(lat_tables(rf), False), *fwd_run)
        advance(ks_ref[pl.ds(rb, L), :], v_ref[0, pl.ds(rb, L), :], pick(lat_tables(rb), True), *bwd_run)
        return carry

    lax.fori_loop(0, n_chunks - 1, state_pass, 0)
    snapshot(n_chunks - 1, fwd_run, fwd_snap)
    snapshot(0, bwd_run, bwd_snap)

    def direction(q, v, s, gate_vecs, mask, c_in, n_in, m_in):
        _, i_row, b_col, b_row, _ = gate_vecs
        d = jnp.where(mask, b_col - b_row + i_row, NEG_INF)
        inter = b_col + m_in
        m_row = jnp.maximum(jnp.max(d, axis=-1, keepdims=True), inter)
        a = jnp.exp(inter - m_row)
        att = s * jnp.exp(d - m_row)
        num = jnp.dot(att.astype(BF16), v, preferred_element_type=F32)
        num = num + a * jnp.dot(q, c_in, preferred_element_type=F32)
        qn = jnp.sum(q.astype(F32) * n_in, axis=-1, keepdims=True)
        den = jnp.sum(att, axis=-1, keepdims=True) + a * qn
        return num * (1.0 / jnp.maximum(jnp.abs(den), jnp.exp(-m_row)))

    def out_pass(c, carry):
        r0 = pl.multiple_of(c * L, L)
        q = qs_ref[pl.ds(r0, L), :]
        k = ks_ref[pl.ds(r0, L), :]
        v = v_ref[0, pl.ds(r0, L), :]
        s = lax.dot_general(q, k, NT_DIMS, preferred_element_type=F32)
        tabs = lat_tables(r0)
        tot = direction(q, v, s, pick(tabs, False), ci <= ri, cfs_ref[c], nfs_ref[c], mfs_ref[c])
        tot = tot + direction(q, v, s, pick(tabs, True), ci > ri, cbs_ref[c], nbs_ref[c], mbs_ref[c])
        mo = mo_ref[0, pl.ds(r0, L), :].astype(F32)
        o_ref[0, pl.ds(r0, L), :] = (_layer_norm(tot) * jax.nn.sigmoid(mo)).astype(BF16)
        return carry

    lax.fori_loop(0, n_chunks, out_pass, 0)


def _mlstm_old(p_lat, p_ctx, g, gt, cg, cgt, conv_w, conv_b, sec_lat, sec_ctx):
    B, T, _ = p_lat.shape
    Tc = p_ctx.shape[1]
    assert T % SCAN_L == 0 and Tc % SCAN_L == 0
    n_chunks = T // SCAN_L

    def lat(sec):
        return pl.BlockSpec((1, T, HEAD_W), lambda b, h: (b, 0, sec * HEADS + h))

    def cx(sec):
        return pl.BlockSpec((1, Tc, HEAD_W), lambda b, h: (b, 0, sec * HEADS + h))

    in_specs = [lat(s) for s in sec_lat] + [cx(s) for s in sec_ctx] + [
        pl.BlockSpec((1, 1, T, N_GK), lambda b, h: (b, h, 0, 0)),
        pl.BlockSpec((1, 1, N_GK, T), lambda b, h: (b, h, 0, 0)),
        pl.BlockSpec((1, 1, Tc, N_GK), lambda b, h: (b, h, 0, 0)),
        pl.BlockSpec((1, 1, N_GK, Tc), lambda b, h: (b, h, 0, 0)),
        pl.BlockSpec((3, HEAD_W), lambda b, h: (0, h)),
        pl.BlockSpec((1, HEAD_W), lambda b, h: (0, h)),
        pl.BlockSpec((3, HEAD_W), lambda b, h: (0, HEADS + h)),
        pl.BlockSpec((1, HEAD_W), lambda b, h: (0, HEADS + h)),
    ]
    state = [pltpu.VMEM((HEAD_W, HEAD_W), F32), pltpu.VMEM((1, HEAD_W), F32), pltpu.VMEM((1, 1), F32)]
    snaps = [pltpu.VMEM((n_chunks, HEAD_W, HEAD_W), BF16), pltpu.VMEM((n_chunks, 1, HEAD_W), F32),
             pltpu.VMEM((n_chunks, 1, 1), F32)]
    return pl.pallas_call(
        _mlstm_kernel,
        out_shape=jax.ShapeDtypeStruct((B, T, BRANCH_W), BF16),
        grid=(B, HEADS),
        in_specs=in_specs,
        out_specs=pl.BlockSpec((1, T, HEAD_W), lambda b, h: (b, 0, h)),
        scratch_shapes=[pltpu.VMEM((T + 16, HEAD_W), F32),
                        pltpu.VMEM((T, HEAD_W), BF16),
                        pltpu.VMEM((T, HEAD_W), BF16),
                        pltpu.VMEM((Tc, HEAD_W), BF16)]
        + [pltpu.VMEM((T, N_GK), F32)] * 3 + [pltpu.VMEM((N_GK, T), F32)] * 3
        + state + state + snaps + snaps,
        compiler_params=_cparams("parallel", "parallel"),
        name="mlstm",
    )(p_lat, p_lat, p_lat, p_lat, p_ctx, p_ctx, g, gt, cg, cgt, conv_w, conv_b, conv_w, conv_b)


N_TAB = 6
AUG_W = HEAD_W + 128


def _split3(x):
    hi = x.astype(BF16).astype(F32)
    r1 = x - hi
    mid = r1.astype(BF16).astype(F32)
    lo = (r1 - mid).astype(BF16).astype(F32)
    return jnp.concatenate([hi, mid, lo], axis=0).astype(BF16)


def _mlstm_build(qp_ref, kp_ref, v_ref, mo_ref, ckp_ref, cv_ref, gt_ref, cgt_ref,
                 wq_ref, bq_ref, wk_ref, bk_ref, o_ref,
                 tab_ref, row_ref,
                 cf_ref, mf_ref, cb_ref, mb_ref, cfs_ref, mfs_ref, cbs_ref, mbs_ref, mask_ref,
                 xf_ref, q_ref, k_ref, ck_ref):
    L = SCAN_L
    T = qp_ref.shape[1]
    Tc = ckp_ref.shape[1]
    n_chunks = T // L
    n_ctx_chunks = Tc // L
    CV = 128

    def conv_stage(src_ref, t_len):
        xf_ref[pl.ds(0, 8), :] = jnp.zeros((8, HEAD_W), F32)
        xf_ref[pl.ds(8 + t_len, 8), :] = jnp.zeros((8, HEAD_W), F32)
        xf_ref[pl.ds(8, t_len), :] = src_ref[0].astype(F32)

    def conv_chunk(c, w, b, dst_ref, scale):
        r0 = pl.multiple_of(c * CV, CV)
        win = xf_ref[pl.ds(r0, CV + 16), :]
        prev = pltpu.roll(win, 1, 0)[8:8 + CV, :]
        cur = win[8:8 + CV, :]
        nxt = pltpu.roll(win, CV + 15, 0)[8:8 + CV, :]
        y = _silu(prev * w[0:1, :] + cur * w[1:2, :] + nxt * w[2:3, :] + b)
        if scale != 1.0:
            y = y * scale
        dst_ref[pl.ds(r0, CV), :] = y.astype(BF16)

    def conv_all(src_ref, w_ref, b_ref, dst_ref, t_len, scale):
        conv_stage(src_ref, t_len)
        w = w_ref[...]
        b = b_ref[...]

        def body(c, carry):
            conv_chunk(c, w, b, dst_ref, scale)
            return carry

        lax.fori_loop(0, t_len // CV, body, 0)

    conv_all(kp_ref, wk_ref, bk_ref, k_ref, T, KEY_SCALE)
    conv_all(ckp_ref, wk_ref, bk_ref, ck_ref, Tc, KEY_SCALE)
    conv_stage(qp_ref, T)
    q_per_chunk = L // CV

    def conv_q(c):
        for u in range(q_per_chunk):
            conv_chunk(c * q_per_chunk + u, wq_ref[...], bq_ref[...], q_ref, 1.0)

    ri = lax.broadcasted_iota(I32, (L, L), 0)
    ci = lax.broadcasted_iota(I32, (L, L), 1)
    tri_u = (ri <= ci).astype(BF16)
    lane8 = lax.broadcasted_iota(I32, (8, L), 1)
    sub8 = lax.broadcasted_iota(I32, (8, L), 0)
    sel_r = lax.broadcasted_iota(I32, (24, 8 * 128), 0) % 8
    sel_c = lax.broadcasted_iota(I32, (24, 8 * 128), 1) // 128
    sel3 = (sel_r == sel_c).astype(BF16)
    ones_cols = jnp.ones((L, AUG_W - HEAD_W), BF16)

    def chunk_tables(g8, n_used, state_only):
        i_f, i_b = g8[0], g8[2]
        lf_f, lf_b = _log_sigmoid(g8[1]), _log_sigmoid(g8[3])
        cs3 = jnp.dot(_split3(jnp.concatenate([lf_f, lf_b], axis=0)), tri_u,
                      preferred_element_type=F32)
        cs = cs3[0:16] + cs3[16:32] + cs3[32:48]
        b_f = cs[0:8]
        b_b = cs[8:16, L - 1:L] - cs[8:16] + lf_b
        z_f = i_f - b_f
        z_b = i_b - b_b
        g_f = b_f[:, L - 1:L] - b_f + i_f
        g_b = b_b[:, 0:1] - b_b + i_b
        mf, mb = z_f, z_b
        s = 1
        while s < L:
            mf = jnp.maximum(mf, jnp.where(lane8 >= s, pltpu.roll(mf, s, 1), NEG_INF))
            mb = jnp.maximum(mb, jnp.where(lane8 < L - s, pltpu.roll(mb, L - s, 1), NEG_INF))
            s *= 2
        mb = jnp.where(lane8 < L - 1, pltpu.roll(mb, L - 1, 1), NEG_INF)
        reps = [None if state_only and t not in (2, 5) else
                lax.dot_general(_split3(val), sel3[:, 0:n_used * 128], TN_DIMS, preferred_element_type=F32)
                for t, val in enumerate((mf, b_f, g_f, mb, b_b, g_b))]

        def rows_of(c):
            out = jnp.zeros((8, L), F32)
            for r, val in enumerate((z_f, z_b, g_f, g_b, b_f, b_b)):
                out = jnp.where(sub8 == r, val[c:c + 1], out)
            return out

        return rows_of, reps

    lat_rows, lat_reps = chunk_tables(gt_ref[0, 0], n_chunks, False)
    for c in range(n_chunks):
        row_ref[c] = lat_rows(c)
        for t in range(N_TAB):
            tab_ref[t, c * L:(c + 1) * L, :] = lat_reps[t][:, c * 128:(c + 1) * 128]

    def lanes2(x):
        return jnp.concatenate([x, x], axis=1)

    def advance(k, v, g_rep, g_row, b_last, c_ref, m_ref):
        m = m_ref[...]
        m_new = jnp.maximum(b_last + m, jnp.max(g_row, axis=-1, keepdims=True))
        kw = (k.astype(F32) * jnp.exp(lanes2(g_rep) - m_new)).astype(BF16)
        v_aug = jnp.concatenate([v, ones_cols], axis=1)
        c_ref[...] = jnp.exp(b_last + m - m_new) * c_ref[...] + lax.dot_general(
            kw, v_aug, TN_DIMS, preferred_element_type=F32)
        m_ref[...] = m_new

    for r in (cf_ref, mf_ref, cb_ref, mb_ref):
        r[...] = jnp.zeros_like(r)
    ctx_rows, ctx_reps = chunk_tables(cgt_ref[0, 0], n_ctx_chunks, True)
    for c in range(n_ctx_chunks):
        rows = ctx_rows(c)
        advance(ck_ref[c * L:(c + 1) * L, :], cv_ref[0, c * L:(c + 1) * L, :],
                ctx_reps[2][:, c * 128:(c + 1) * 128], rows[2:3], rows[4:5, L - 1:L], cf_ref, mf_ref)
    for c in reversed(range(n_ctx_chunks)):
        rows = ctx_rows(c)
        advance(ck_ref[c * L:(c + 1) * L, :], cv_ref[0, c * L:(c + 1) * L, :],
                ctx_reps[5][:, c * 128:(c + 1) * 128], rows[3:4], rows[5:6, 0:1], cb_ref, mb_ref)

    def state_pass(i, carry):
        cb = n_chunks - 1 - i
        rf = pl.multiple_of(i * L, L)
        rb = pl.multiple_of(cb * L, L)
        cfs_ref[i] = cf_ref[...].astype(BF16)
        mfs_ref[i] = mf_ref[...]
        cbs_ref[cb] = cb_ref[...].astype(BF16)
        mbs_ref[cb] = mb_ref[...]
        rows_f = row_ref[i]
        rows_b = row_ref[cb]
        advance(k_ref[pl.ds(rf, L), :], v_ref[0, pl.ds(rf, L), :], tab_ref[2, pl.ds(rf, L), :],
                rows_f[2:3], rows_f[4:5, L - 1:L], cf_ref, mf_ref)
        advance(k_ref[pl.ds(rb, L), :], v_ref[0, pl.ds(rb, L), :], tab_ref[5, pl.ds(rb, L), :],
                rows_b[3:4], rows_b[5:6, 0:1], cb_ref, mb_ref)
        conv_q(i)
        return carry

    def finish_states():
        conv_q(n_chunks - 1)
        cfs_ref[n_chunks - 1] = cf_ref[...].astype(BF16)
        mfs_ref[n_chunks - 1] = mf_ref[...]
        cbs_ref[0] = cb_ref[...].astype(BF16)
        mbs_ref[0] = mb_ref[...]

    def direction(q, v_aug, s, z_row, zmax_rep, b_rep, mask, c_in, m_in):
        mx = jnp.maximum(zmax_rep, m_in)
        att = s * jnp.exp((z_row - lanes2(mx)) + mask)
        na = jnp.dot(att.astype(BF16), v_aug, preferred_element_type=F32)
        qa = jnp.dot(q, c_in, preferred_element_type=F32)
        a = jnp.exp(m_in - mx)
        num = na[:, 0:HEAD_W] + lanes2(a) * qa[:, 0:HEAD_W]
        den = na[:, HEAD_W:] + a * qa[:, HEAD_W:]
        scale = 1.0 / jnp.maximum(jnp.abs(den), jnp.exp(-(b_rep + mx)))
        return num * lanes2(scale)

    mask_ref[0] = jnp.where(ci <= ri, 0.0, NEG_INF)
    mask_ref[1] = jnp.where(ci > ri, 0.0, NEG_INF)

    def out_chunk(c):
        r0 = pl.multiple_of(c * L, L)
        q = q_ref[pl.ds(r0, L), :]
        k = k_ref[pl.ds(r0, L), :]
        v_aug = jnp.concatenate([v_ref[0, pl.ds(r0, L), :], ones_cols], axis=1)
        s = lax.dot_general(q, k, NT_DIMS, preferred_element_type=F32)
        rows = row_ref[c]
        tot = direction(q, v_aug, s, rows[0:1], tab_ref[0, pl.ds(r0, L), :], tab_ref[1, pl.ds(r0, L), :],
                        mask_ref[0], cfs_ref[c], mfs_ref[c])
        tot = tot + direction(q, v_aug, s, rows[1:2], tab_ref[3, pl.ds(r0, L), :],
                              tab_ref[4, pl.ds(r0, L), :], mask_ref[1], cbs_ref[c], mbs_ref[c])
        mo = mo_ref[0, pl.ds(r0, L), :].astype(F32)
        o_ref[0, pl.ds(r0, L), :] = (_layer_norm(tot) * jax.nn.sigmoid(mo)).astype(BF16)

    return state_pass, finish_states, out_chunk


def _scan_kernel(dl_ref, rq_ref, rk_ref, rv_ref, rg_ref, rck_ref, rcv_ref,
                 mq_ref, mk_ref, mv_ref, mo_ref, mck_ref, mcv_ref, gt_ref, cgt_ref,
                 wq_ref, bq_ref, wk_ref, bk_ref, r_ref, m_ref, *scratch):
    n_chunks = rq_ref.shape[1] // SCAN_L
    ret = _ret_build(dl_ref, rq_ref, rk_ref, rv_ref, rg_ref, rck_ref, rcv_ref, r_ref,
                     *scratch[:N_RET_SCRATCH])
    mls = _mlstm_build(mq_ref, mk_ref, mv_ref, mo_ref, mck_ref, mcv_ref, gt_ref, cgt_ref,
                       wq_ref, bq_ref, wk_ref, bk_ref, m_ref, *scratch[N_RET_SCRATCH:])

    def state_pass(i, carry):
        ret[0](i, carry)
        mls[0](i, carry)
        return carry

    lax.fori_loop(0, n_chunks - 1, state_pass, 0)
    ret[1]()
    mls[1]()

    def out_pass(i, carry):
        for c in (2 * i, 2 * i + 1):
            ret[2](c)
            mls[2](c)
        return carry

    lax.fori_loop(0, n_chunks // 2, out_pass, 0)


N_RET_SCRATCH = 6


def _scans(decay_logit, p_lat, p_ctx, gt, cgt, conv_w, conv_b, ret_lat, ret_ctx, ml_lat, ml_ctx):
    B, T, _ = p_lat.shape
    Tc = p_ctx.shape[1]
    assert T % (2 * SCAN_L) == 0 and Tc % SCAN_L == 0 and T // SCAN_L <= 8
    n_chunks = T // SCAN_L

    def lat(sec):
        return pl.BlockSpec((1, T, HEAD_W), lambda b, h: (b, 0, sec * HEADS + h))

    def cx(sec):
        return pl.BlockSpec((1, Tc, HEAD_W), lambda b, h: (b, 0, sec * HEADS + h))

    gates = pl.BlockSpec((1, 1, N_GK, 8, SCAN_L), lambda b, h: (b, h, 0, 0, 0))
    out = pl.BlockSpec((1, T, HEAD_W), lambda b, h: (b, 0, h))
    ret_scratch = [pltpu.VMEM((HEAD_W, HEAD_W), F32),
                   pltpu.VMEM((HEAD_W, HEAD_W), F32),
                   pltpu.VMEM((n_chunks, HEAD_W, HEAD_W), BF16),
                   pltpu.VMEM((n_chunks, HEAD_W, HEAD_W), BF16),
                   pltpu.VMEM((4, SCAN_L, HEAD_W), F32),
                   pltpu.VMEM((SCAN_L, SCAN_L), F32)]
    assert len(ret_scratch) == N_RET_SCRATCH
    state = [pltpu.VMEM((HEAD_W, AUG_W), F32), pltpu.VMEM((1, 1), F32)]
    snaps = [pltpu.VMEM((n_chunks, HEAD_W, AUG_W), BF16), pltpu.VMEM((n_chunks, 1, 1), F32)]
    ml_scratch = [pltpu.VMEM((N_TAB, T, 128), F32), pltpu.VMEM((n_chunks, 8, SCAN_L), F32)] \
        + state + state + snaps + snaps + [pltpu.VMEM((2, SCAN_L, SCAN_L), F32)] \
        + [pltpu.VMEM((T + 16, HEAD_W), F32), pltpu.VMEM((T, HEAD_W), BF16),
           pltpu.VMEM((T, HEAD_W), BF16), pltpu.VMEM((Tc, HEAD_W), BF16)]
    conv_specs = [pl.BlockSpec((3, HEAD_W), lambda b, h: (0, h)),
                  pl.BlockSpec((1, HEAD_W), lambda b, h: (0, h)),
                  pl.BlockSpec((3, HEAD_W), lambda b, h: (0, HEADS + h)),
                  pl.BlockSpec((1, HEAD_W), lambda b, h: (0, HEADS + h))]
    return pl.pallas_call(
        _scan_kernel,
        out_shape=(jax.ShapeDtypeStruct((B, T, BRANCH_W), BF16),
                   jax.ShapeDtypeStruct((B, T, BRANCH_W), BF16)),
        grid=(B, HEADS),
        in_specs=[pl.BlockSpec(memory_space=pltpu.SMEM)]
        + [lat(s) for s in ret_lat] + [cx(s) for s in ret_ctx]
        + [lat(s) for s in ml_lat] + [cx(s) for s in ml_ctx] + [gates, gates] + conv_specs,
        out_specs=(out, out),
        scratch_shapes=ret_scratch + ml_scratch,
        compiler_params=_cparams("parallel", "parallel"),
        name="scans",
    )(decay_logit, *([p_lat] * 4), *([p_ctx] * 2), *([p_lat] * 4), *([p_ctx] * 2), gt, cgt,
      conv_w, conv_b, conv_w, conv_b)


def _mlstm_unused(p_lat, p_ctx, gt, cgt, sec_lat, sec_ctx):
    B, T, _ = p_lat.shape
    Tc = p_ctx.shape[1]
    assert T % (2 * SCAN_L) == 0 and Tc % SCAN_L == 0 and T // SCAN_L <= 8
    n_chunks = T // SCAN_L

    def lat(sec):
        return pl.BlockSpec((1, T, HEAD_W), lambda b, h: (b, 0, sec * HEADS + h))

    def cx(sec):
        return pl.BlockSpec((1, Tc, HEAD_W), lambda b, h: (b, 0, sec * HEADS + h))

    in_specs = [lat(s) for s in sec_lat] + [cx(s) for s in sec_ctx] + [
        pl.BlockSpec((1, 1, N_GK, 8, SCAN_L), lambda b, h: (b, h, 0, 0, 0)),
        pl.BlockSpec((1, 1, N_GK, 8, SCAN_L), lambda b, h: (b, h, 0, 0, 0)),
    ]
    state = [pltpu.VMEM((HEAD_W, AUG_W), F32), pltpu.VMEM((1, 1), F32)]
    snaps = [pltpu.VMEM((n_chunks, HEAD_W, AUG_W), BF16), pltpu.VMEM((n_chunks, 1, 1), F32)]
    return pl.pallas_call(
        _mlstm_kernel,
        out_shape=jax.ShapeDtypeStruct((B, T, BRANCH_W), BF16),
        grid=(B, HEADS),
        in_specs=in_specs,
        out_specs=pl.BlockSpec((1, T, HEAD_W), lambda b, h: (b, 0, h)),
        scratch_shapes=[pltpu.VMEM((N_TAB, T, 128), F32),
                        pltpu.VMEM((n_chunks, 8, SCAN_L), F32)]
        + state + state + snaps + snaps + [pltpu.VMEM((2, SCAN_L, SCAN_L), F32)],
        compiler_params=_cparams("parallel", "parallel"),
        name="mlstm",
    )(p_lat, p_lat, p_lat, p_lat, p_ctx, p_ctx, gt, cgt)


def _merge_kernel(alpha, r_ref, m_ref, gr_ref, gm_ref, x_ref, g1_ref, sh2_ref, sc2_ref,
                  lng_ref, lnb_ref, wr_ref, wm_ref, wo_ref, wrt_ref, brt_ref,
                  x1_ref, ua_ref, ub_ref, lt_ref):
    tm = x_ref.shape[1]
    for s in range(tm // MERGE_SUB):
        rows = slice(s * MERGE_SUB, (s + 1) * MERGE_SUB)
        yr = jnp.dot(r_ref[0, rows, :], wr_ref[...], preferred_element_type=F32)
        ym = jnp.dot(m_ref[0, rows, :], wm_ref[...], preferred_element_type=F32)
        y = (jax.nn.sigmoid(gr_ref[0, rows, :].astype(F32)) * yr
             + jax.nn.sigmoid(gm_ref[0, rows, :].astype(F32)) * ym)
        yo = jnp.dot(y.astype(BF16), wo_ref[...], preferred_element_type=F32)
        x1 = _layer_norm(alpha * x_ref[0, rows, :] + g1_ref[0] * yo) * lng_ref[...] + lnb_ref[...]
        x1_ref[0, rows, :] = x1
        u2 = _layer_norm(x1) * (1.0 + sc2_ref[0]) + sh2_ref[0]
        lt_ref[0, :, rows] = lax.dot_general(wrt_ref[...], u2.astype(BF16), NT_DIMS,
                                             preferred_element_type=F32) + brt_ref[...]
        ua_ref[0, rows, :] = _pack_pairs(u2[:, 0:PACK_W], u2[:, PACK_W:2 * PACK_W])
        ub_ref[0, rows, :] = _pack_pairs(u2[:, 2 * PACK_W:3 * PACK_W], u2[:, 3 * PACK_W:4 * PACK_W])


def _merge(alpha, b0, nb, r, m, p_lat, sec_gates, x, g1, sh2, sc2, lng, lnb, wr, wm, wo, wrt, brt):
    _, T, D = x.shape
    tm = MERGE_TM

    def tile(w):
        return pl.BlockSpec((1, tm, w), lambda b, i: (b + b0, i, 0))

    def out_tile(w):
        return pl.BlockSpec((1, tm, w), lambda b, i: (b, i, 0))

    def sec(s):
        return pl.BlockSpec((1, tm, BRANCH_W), lambda b, i: (b + b0, i, s))

    def mod():
        return pl.BlockSpec((1, 1, D), lambda b, i: (b + b0, 0, 0))

    def const(shape):
        return pl.BlockSpec(shape, lambda b, i: (0,) * len(shape))

    return pl.pallas_call(
        functools.partial(_merge_kernel, alpha),
        out_shape=(jax.ShapeDtypeStruct((nb, T, D), F32),
                   jax.ShapeDtypeStruct((nb, T, PACK_W), U32),
                   jax.ShapeDtypeStruct((nb, T, PACK_W), U32),
                   jax.ShapeDtypeStruct((nb, ROUTE_ROWS, T), F32)),
        grid=(nb, T // tm),
        in_specs=[tile(BRANCH_W), tile(BRANCH_W), sec(sec_gates[0]), sec(sec_gates[1]), tile(D),
                  mod(), mod(), mod(), const((1, D)), const((1, D)),
                  const((BRANCH_W, D)), const((BRANCH_W, D)), const((D, D)),
                  const((ROUTE_ROWS, D)), const((ROUTE_ROWS, 1))],
        out_specs=(out_tile(D), out_tile(PACK_W), out_tile(PACK_W),
                   pl.BlockSpec((1, ROUTE_ROWS, tm), lambda b, i: (b, 0, i))),
        compiler_params=_cparams("parallel", "parallel"),
        name="merge",
    )(r, m, p_lat, p_lat, x, g1, sh2, sc2, lng, lnb, wr, wm, wo, wrt, brt)


def _route_kernel(lt_ref, ri_ref, rw_ref, cnt_ref, carry_ref, u_ref):
    i = pl.program_id(0)
    tm = lt_ref.shape[2]

    @pl.when(i == 0)
    def _():
        carry_ref[...] = jnp.zeros_like(carry_ref)
        r = lax.broadcasted_iota(I32, (tm, tm), 0)
        c = lax.broadcasted_iota(I32, (tm, tm), 1)
        u_ref[...] = (r < c).astype(BF16)

    lt = lt_ref[0]
    lg = lt[0:N_GROUPS, :]
    eg = jnp.exp(lg - jnp.max(lg, axis=0, keepdims=True))
    pg = eg / jnp.sum(eg, axis=0, keepdims=True)
    pg_top = jnp.max(pg, axis=0, keepdims=True)
    rows_g = lax.broadcasted_iota(I32, pg.shape, 0)
    g_idx = jnp.min(jnp.where(pg == pg_top, rows_g, N_GROUPS), axis=0, keepdims=True)

    le = jnp.zeros((EXP_PER_GROUP, tm), F32)
    for g in range(N_GROUPS):
        lo = 8 + g * EXP_PER_GROUP
        le = jnp.where(g_idx == g, lt[lo:lo + EXP_PER_GROUP, :], le)
    ee = jnp.exp(le - jnp.max(le, axis=0, keepdims=True))
    pe = ee / jnp.sum(ee, axis=0, keepdims=True)
    rows_e = lax.broadcasted_iota(I32, pe.shape, 0)
    v1 = jnp.max(pe, axis=0, keepdims=True)
    i1 = jnp.min(jnp.where(pe == v1, rows_e, EXP_PER_GROUP), axis=0, keepdims=True)
    pe2 = jnp.where(rows_e == i1, -1.0, pe)
    v2 = jnp.max(pe2, axis=0, keepdims=True)
    i2 = jnp.min(jnp.where(pe2 == v2, rows_e, EXP_PER_GROUP), axis=0, keepdims=True)
    den = v1 + v2
    rw_ref[0:1, :] = pg_top * v1 / den
    rw_ref[1:2, :] = pg_top * v2 / den
    e1 = g_idx * EXP_PER_GROUP + i1
    e2 = g_idx * EXP_PER_GROUP + i2

    rows_x = lax.broadcasted_iota(I32, (N_EXPERTS, tm), 0)
    oh1 = (rows_x == e1).astype(F32)
    oh2 = (rows_x == e2).astype(F32)
    both = oh1 + oh2
    before = carry_ref[:, 0:1] + jnp.dot(both.astype(BF16), u_ref[...], preferred_element_type=F32)
    ri_ref[0:1, :] = e1
    ri_ref[1:2, :] = e2
    ri_ref[2:3, :] = jnp.sum(oh1 * before, axis=0, keepdims=True).astype(I32)
    ri_ref[3:4, :] = jnp.sum(oh2 * before, axis=0, keepdims=True).astype(I32)
    carry_ref[...] = carry_ref[...] + jnp.sum(both, axis=1, keepdims=True)
    cnt_ref[...] = carry_ref[...].astype(I32)


def _route(lt):
    B, _, T = lt.shape
    tm = ROUTE_TM
    per_b = T // tm
    n = B * T
    return pl.pallas_call(
        _route_kernel,
        out_shape=(jax.ShapeDtypeStruct((4, n), I32),
                   jax.ShapeDtypeStruct((2, n), F32),
                   jax.ShapeDtypeStruct((N_EXPERTS, 128), I32)),
        grid=(n // tm,),
        in_specs=[pl.BlockSpec((1, ROUTE_ROWS, tm), lambda i: (i // per_b, 0, i % per_b))],
        out_specs=(pl.BlockSpec((4, tm), lambda i: (0, i)),
                   pl.BlockSpec((2, tm), lambda i: (0, i)),
                   pl.BlockSpec((N_EXPERTS, 128), lambda i: (0, 0))),
        scratch_shapes=[pltpu.VMEM((N_EXPERTS, 128), F32), pltpu.VMEM((tm, tm), BF16)],
        compiler_params=_cparams("arbitrary"),
        name="route",
    )(lt)


def _sc_mesh():
    return plsc.VectorSubcoreMesh(core_axis_name="c", subcore_axis_name="s")


def _sc_scatter2(rows, idx0, idx1, n_out):
    m, w = rows.shape

    @functools.partial(pl.kernel, out_type=jax.ShapeDtypeStruct((n_out, w), rows.dtype),
                       mesh=_sc_mesh(), scratch_types=[])
    def k(x_hbm, i0_hbm, i1_hbm, o_hbm):
        def body(x_vmem, i0_vmem, i1_vmem):
            pltpu.sync_copy(x_vmem, o_hbm.at[i0_vmem.at[0]])
            pltpu.sync_copy(x_vmem, o_hbm.at[i1_vmem.at[0]])

        pltpu.emit_pipeline(
            body,
            grid=(m // SC_WIN,),
            in_specs=[pl.BlockSpec((SC_WIN, w), lambda i: (i, 0)),
                      pl.BlockSpec((1, SC_WIN), lambda i: (0, i)),
                      pl.BlockSpec((1, SC_WIN), lambda i: (0, i))],
            out_specs=[],
            core_axis_name=("c", "s"),
            dimension_semantics=(pltpu.PARALLEL,),
        )(x_hbm, i0_hbm, i1_hbm)

    return k(rows, idx0.reshape(1, m), idx1.reshape(1, m))


def _sc_gather(table, idx):
    m = idx.shape[0]
    w = table.shape[1]

    @functools.partial(pl.kernel, out_type=jax.ShapeDtypeStruct((m, w), table.dtype),
                       mesh=_sc_mesh(), scratch_types=[])
    def k(t_hbm, i_hbm, o_hbm):
        def body(i_vmem, o_vmem):
            pltpu.sync_copy(t_hbm.at[i_vmem.at[0]], o_vmem)

        pltpu.emit_pipeline(
            body,
            grid=(m // SC_WIN,),
            in_specs=[pl.BlockSpec((1, SC_WIN), lambda i: (0, i))],
            out_specs=[pl.BlockSpec((SC_WIN, w), lambda i: (i, 0))],
            core_axis_name=("c", "s"),
            dimension_semantics=(pltpu.PARALLEL,),
        )(i_hbm, o_hbm)

    return k(table, idx.reshape(1, m))


def _expert_kernel(be_ref, nv_ref, xa_ref, xb_ref, w1f_ref, w3f_ref, w2f_ref, ya_ref, yb_ref,
                   w1_ref, w3_ref, w2_ref):
    j = pl.program_id(0)
    nv = nv_ref[j]

    @pl.when(jnp.logical_or(j == 0, be_ref[j] != be_ref[jnp.maximum(j - 1, 0)]))
    def _():
        w1_ref[0] = w1f_ref[0].astype(BF16)
        w3_ref[0] = w3f_ref[0].astype(BF16)
        w2_ref[0] = w2f_ref[0].astype(BF16)

    @pl.when(nv > 0)
    def _():
        for s in range(xa_ref.shape[0] // MOE_SUB):
            rows = slice(s * MOE_SUB, (s + 1) * MOE_SUB)
            valid = lax.broadcasted_iota(I32, (MOE_SUB, PACK_W), 0) + s * MOE_SUB < nv
            zero = jnp.zeros((MOE_SUB, PACK_W), U32)
            parts = _unpack_pairs(jnp.where(valid, xa_ref[rows, :], zero)) + \
                _unpack_pairs(jnp.where(valid, xb_ref[rows, :], zero))
            x = jnp.concatenate([p.astype(BF16) for p in parts], axis=1)
            h1 = jnp.dot(x, w1_ref[0], preferred_element_type=F32)
            h3 = jnp.dot(x, w3_ref[0], preferred_element_type=F32)
            y = jnp.dot((_silu(h1) * h3).astype(BF16), w2_ref[0], preferred_element_type=F32)
            ya_ref[rows, :] = _pack_pairs(y[:, 0:PACK_W], y[:, PACK_W:2 * PACK_W])
            yb_ref[rows, :] = _pack_pairs(y[:, 2 * PACK_W:3 * PACK_W], y[:, 3 * PACK_W:4 * PACK_W])

    @pl.when(nv == 0)
    def _():
        ya_ref[...] = jnp.zeros_like(ya_ref)
        yb_ref[...] = jnp.zeros_like(yb_ref)


def _experts(block_exp, n_valid, xa, xb, w1, w3, w2):
    n_slots = xa.shape[0]
    n_blocks = n_slots // MOE_BLK
    d, de = w1.shape[1], w1.shape[2]
    slot = pl.BlockSpec((MOE_BLK, PACK_W), lambda j, be, nv: (j, 0))
    grid_spec = pltpu.PrefetchScalarGridSpec(
        num_scalar_prefetch=2,
        grid=(n_blocks,),
        in_specs=[slot, slot,
                  pl.BlockSpec((1, d, de), lambda j, be, nv: (be[j], 0, 0)),
                  pl.BlockSpec((1, d, de), lambda j, be, nv: (be[j], 0, 0)),
                  pl.BlockSpec((1, de, d), lambda j, be, nv: (be[j], 0, 0))],
        out_specs=(slot, slot),
        scratch_shapes=[pltpu.VMEM((1, d, de), BF16), pltpu.VMEM((1, d, de), BF16),
                        pltpu.VMEM((1, de, d), BF16)],
    )
    return pl.pallas_call(
        _expert_kernel,
        out_shape=(jax.ShapeDtypeStruct((n_slots, PACK_W), U32),
                   jax.ShapeDtypeStruct((n_slots, PACK_W), U32)),
        grid_spec=grid_spec,
        compiler_params=_cparams("arbitrary"),
        name="experts",
    )(block_exp, n_valid, xa, xb, w1, w3, w2)


def _final_kernel(alpha, x1_ref, a0_ref, b0_ref, a1_ref, b1_ref, w_ref, g2_ref, lng_ref, lnb_ref,
                  *rest):
    o_ref = rest[-1]
    w = w_ref[...]
    w0 = w[:, 0:1]
    w1 = w[:, 1:2]
    parts0 = _unpack_pairs(a0_ref[...]) + _unpack_pairs(b0_ref[...])
    parts1 = _unpack_pairs(a1_ref[...]) + _unpack_pairs(b1_ref[...])
    f = jnp.concatenate([w0 * p0 + w1 * p1 for p0, p1 in zip(parts0, parts1)], axis=1)
    o_ref[0] = _layer_norm(alpha * x1_ref[0] + g2_ref[0] * f) * lng_ref[...] + lnb_ref[...]


def _final(alpha, b0, n_batch, x1, ya, yb, w, g2, lng, lnb, out_prev):
    nb, T, D = x1.shape
    tm = MERGE_TM
    per_b = T // tm
    n_tiles = nb * per_b

    def rows(k):
        return pl.BlockSpec((tm, PACK_W), lambda b, i: (k * n_tiles + b * per_b + i, 0))

    in_specs = [pl.BlockSpec((1, tm, D), lambda b, i: (b, i, 0)),
                rows(0), rows(0), rows(1), rows(1),
                pl.BlockSpec((tm, 2), lambda b, i: (b * per_b + i, 0)),
                pl.BlockSpec((1, 1, D), lambda b, i: (b + b0, 0, 0)),
                pl.BlockSpec((1, D), lambda b, i: (0, 0)),
                pl.BlockSpec((1, D), lambda b, i: (0, 0))]
    args = [x1, ya, yb, ya, yb, w, g2, lng, lnb]
    aliases = {}
    if out_prev is not None:
        in_specs.append(pl.BlockSpec(memory_space=pl.ANY))
        args.append(out_prev)
        aliases = {len(args) - 1: 0}
    return pl.pallas_call(
        functools.partial(_final_kernel, alpha),
        out_shape=jax.ShapeDtypeStruct((n_batch, T, D), F32),
        grid=(nb, per_b),
        in_specs=in_specs,
        out_specs=pl.BlockSpec((1, tm, D), lambda b, i: (b + b0, i, 0)),
        input_output_aliases=aliases,
        compiler_params=_cparams("parallel", "parallel"),
        name="final",
    )(*args)


def _rotary_tables(T):
    quarter = HEAD_W // 4
    freqs = ROPE_BASE ** (-jnp.arange(quarter, dtype=F32) / quarter)
    t = jnp.arange(T)
    ang_r = (t // GRID_W).astype(F32)[:, None] * freqs[None, :]
    ang_c = (t % GRID_W).astype(F32)[:, None] * freqs[None, :]
    cos = jnp.concatenate([jnp.cos(ang_r)] * 2 + [jnp.cos(ang_c)] * 2, axis=1)
    sin = jnp.concatenate([-jnp.sin(ang_r), jnp.sin(ang_r), -jnp.sin(ang_c), jnp.sin(ang_c)], axis=1)
    return cos, sin


def _per_head_gates(gt):
    B, _, T = gt.shape
    n_chunks = T // SCAN_L
    gth = gt.reshape(B, N_GK, HEADS, n_chunks, SCAN_L).transpose(0, 2, 1, 3, 4)
    return jnp.pad(gth, ((0, 0), (0, 0), (0, 0), (0, 8 - n_chunks), (0, 0)))


def _table_lookup(table, idx):
    sel = idx[..., None] == jnp.arange(table.shape[0], dtype=idx.dtype)
    return jnp.sum(jnp.where(sel, table, 0), axis=-1)


def kernel(x, c, ctx, c_ctx, w_ada, b_ada, w_in, b_mgate, ml_conv_w, ml_conv_b, ret_decay_logit, w_ret_branch, w_ml_branch, w_out, ln1_g, ln1_b, w_rg, b_rg, w_re, b_re, w_e1, w_e3, w_e2, ln2_g, ln2_b):
    B, T, D = x.shape
    depth = w_ada.shape[0]
    assert depth == 1 and D == BRANCH_W and T % min(PROJ_TM, T) == 0 and T % GRID_W == 0
    alpha = (2 * depth) ** 0.25

    n_rows = -(-(B + 1) // 8) * 8
    cs = jnp.zeros((n_rows, D), F32).at[:B].set(c).at[B].set(c_ctx)
    mod = _ada(cs, w_ada[0], b_ada[0][None, :])
    sh1, sc1, g1, sh2, sc2, g2 = [mod[:B, None, i * D:(i + 1) * D] for i in range(6)]
    csh1 = mod[B, 0 * D:1 * D].reshape(1, 1, D)
    csc1 = mod[B, 1 * D:2 * D].reshape(1, 1, D)

    w = w_in[0]
    sec_w = [w[:, s * BRANCH_W:(s + 1) * BRANCH_W] for s in range(8)]
    g_lo = 8 * BRANCH_W
    w_gate_t = w[:, g_lo:g_lo + N_GATES].T.astype(BF16)
    b_gate = b_mgate[0][:, None]
    sec_w += [w[:, g_lo + N_GATES:g_lo + N_GATES + D], w[:, g_lo + N_GATES + D:]]
    w_lat = jnp.concatenate(sec_w, axis=1).astype(BF16)
    w_ctx = jnp.concatenate([sec_w[1], sec_w[2], sec_w[5], sec_w[6]], axis=1).astype(BF16)
    kinds_lat = ("rot", "rot_scale") + ("plain",) * 8
    kinds_ctx = ("scale", "plain", "plain", "plain")
    conv_w, conv_b = ml_conv_w[0], ml_conv_b[0][None, :]
    p_lat, gt_lat = _proj(x, sh1, sc1, w_lat, w_gate_t, b_gate, conv_w, conv_b, kinds_lat, T,
                          _rotary_tables(T))
    Tc = ctx.shape[1]
    p_ctx, gt_ctx = _proj(ctx.reshape(1, B * Tc, D), csh1, csc1, w_ctx, w_gate_t, b_gate,
                          conv_w, conv_b, kinds_ctx, Tc)
    p_ctx = p_ctx.reshape(B, Tc, -1)
    gt_ctx = gt_ctx.reshape(N_GATES, B, Tc).transpose(1, 0, 2)

    ret, mls = _scans(ret_decay_logit[0], p_lat, p_ctx, _per_head_gates(gt_lat), _per_head_gates(gt_ctx),
                      conv_w, conv_b, (0, 1, 2, 3), (0, 1), (4, 5, 6, 7), (2, 3))

    wrt = jnp.zeros((ROUTE_ROWS, D), F32).at[:N_GROUPS].set(w_rg[0].T).at[8:8 + N_EXPERTS].set(w_re[0].T)
    brt = jnp.zeros((ROUTE_ROWS, 1), F32).at[:N_GROUPS, 0].set(b_rg[0]).at[8:8 + N_EXPERTS, 0].set(b_re[0])
    merge_w = (ln1_g[0][None, :], ln1_b[0][None, :], w_ret_branch[0].astype(BF16),
               w_ml_branch[0].astype(BF16), w_out[0].astype(BF16), wrt.astype(BF16), brt)

    n_split = MOE_SPLIT if B % MOE_SPLIT == 0 else 1
    nb = B // n_split
    n_tok = nb * T
    n_blocks = (2 * n_tok) // MOE_BLK + N_EXPERTS
    n_slots = n_blocks * MOE_BLK
    block_start = jnp.arange(n_blocks, dtype=I32) * MOE_BLK

    def dispatch(b0):
        x1, ua, ub, lt = _merge(alpha, b0, nb, ret, mls, p_lat, (8, 9), x, g1, sh2, sc2, *merge_w)
        ri, rw, cnt = _route(lt)
        counts = cnt[:, 0]
        padded = (counts + MOE_BLK - 1) // MOE_BLK * MOE_BLK
        pad_end = jnp.cumsum(padded)
        pad_off = pad_end - padded
        dest = _table_lookup(pad_off, ri[0:2]) + ri[2:4]
        block_exp = jnp.minimum((block_start[:, None] >= pad_end[None, :]).sum(1),
                                N_EXPERTS - 1).astype(I32)
        n_valid = jnp.clip(_table_lookup(counts, block_exp)
                           - (block_start - _table_lookup(pad_off, block_exp)), 0, MOE_BLK).astype(I32)
        xa = _sc_scatter2(ua.reshape(n_tok, PACK_W), dest[0], dest[1], n_slots)
        xb = _sc_scatter2(ub.reshape(n_tok, PACK_W), dest[0], dest[1], n_slots)
        return x1, rw, dest, block_exp, n_valid, xa, xb

    def combine(d):
        x1, rw, dest, block_exp, n_valid, xa, xb = d
        ya, yb = _experts(block_exp, n_valid, xa, xb, w_e1[0], w_e3[0], w_e2[0])
        dflat = dest.reshape(2 * n_tok)
        return x1, rw, _sc_gather(ya, dflat), _sc_gather(yb, dflat)

    dispatched = [dispatch(s * nb) for s in range(n_split)]
    combined = [combine(d) for d in dispatched]
    out = None
    for s, (x1, rw, ga, gb) in enumerate(combined):
        out = _final(alpha, s * nb, B, x1, ga, gb, rw.T, g2, ln2_g[0][None, :], ln2_b[0][None, :], out)
    return out
```

```python
import functools

import jax
import jax.numpy as jnp
from jax import lax
from jax.experimental import pallas as pl
from jax.experimental.pallas import tpu as pltpu
from jax.experimental.pallas import tpu_sc as plsc

F32 = jnp.float32
BF16 = jnp.bfloat16
U32 = jnp.uint32
I32 = jnp.int32
HIGHEST = lax.Precision.HIGHEST

HEADS = 4
HEAD_W = 256
BRANCH_W = HEADS * HEAD_W
GRID_W = 64
ROPE_BASE = 10000.0
N_GATES = 16
N_GK = N_GATES // HEADS
N_GROUPS = 4
EXP_PER_GROUP = 8
N_EXPERTS = N_GROUPS * EXP_PER_GROUP
LN_EPS = 1e-5
NEG_INF = -1e30
KEY_SCALE = HEAD_W ** -0.5

SCAN_L = 256
CONV_ROWS = 128
PROJ_TM = 2048
PROJ_SUB = 256
MERGE_TM = 512
ROUTE_TM = 512
MOE_BLK = 512
SC_WIN = 128
PACK_W = 256
ROUTE_ROWS = 64
N_TAB = 6
AUG_W = HEAD_W + 128
VMEM_LIMIT = 48 * 1024 * 1024
PROJ_VMEM_LIMIT = 60000 * 1024

NT_DIMS = (((1,), (1,)), ((), ()))
TN_DIMS = (((0,), (0,)), ((), ()))


def _cparams(*sem, vmem=VMEM_LIMIT):
    return pltpu.CompilerParams(dimension_semantics=sem, vmem_limit_bytes=vmem)


def _layer_norm(x):
    mu = jnp.mean(x, axis=-1, keepdims=True)
    xc = x - mu
    var = jnp.mean(xc * xc, axis=-1, keepdims=True)
    return xc * lax.rsqrt(var + LN_EPS)


def _log_sigmoid(x):
    return jnp.minimum(x, 0.0) - jnp.log1p(jnp.exp(-jnp.abs(x)))


def _silu(x):
    return x * jax.nn.sigmoid(x)


def _pack_pairs(hi, lo):
    hb = lax.bitcast_convert_type(hi.astype(BF16).astype(F32), U32)
    lb = lax.bitcast_convert_type(lo.astype(BF16).astype(F32), U32)
    return (hb & jnp.uint32(0xFFFF0000)) | (lb >> 16)


def _unpack_pairs(p):
    hi = lax.bitcast_convert_type(p & jnp.uint32(0xFFFF0000), F32)
    lo = lax.bitcast_convert_type(p << 16, F32)
    return hi, lo


def _split3(x):
    hi = x.astype(BF16).astype(F32)
    r1 = x - hi
    mid = r1.astype(BF16).astype(F32)
    lo = (r1 - mid).astype(BF16).astype(F32)
    return jnp.concatenate([hi, mid, lo], axis=0).astype(BF16)


def _ada_kernel(c_ref, w_ref, b_ref, o_ref):
    s = _silu(c_ref[...])
    o_ref[...] = jnp.dot(s, w_ref[...], precision=HIGHEST, preferred_element_type=F32) + b_ref[...]


def _ada(cs, w, b):
    rows, d = cs.shape
    cols = w.shape[1]
    tn = 1024
    return pl.pallas_call(
        _ada_kernel,
        out_shape=jax.ShapeDtypeStruct((rows, cols), F32),
        grid=(cols // tn,),
        in_specs=[pl.BlockSpec((rows, d), lambda j: (0, 0)),
                  pl.BlockSpec((d, tn), lambda j: (0, j)),
                  pl.BlockSpec((1, tn), lambda j: (0, j))],
        out_specs=pl.BlockSpec((rows, tn), lambda j: (0, j)),
        compiler_params=_cparams("parallel"),
        name="ada",
    )(cs, w, b)


def _proj_kernel(kinds, srcs, hb_ref, tb_ref, x_ref, sh_ref, sc_ref, wh_ref, wt_ref, wg_ref, bg_ref, *rest):
    if "rot" in kinds or "rot_scale" in kinds:
        cos_ref, sin_ref, o_ref, gt_ref, u_ref = rest
    else:
        o_ref, gt_ref, u_ref = rest
    j = pl.program_id(2)
    tm = x_ref.shape[1]
    sub = min(PROJ_SUB, tm)

    def rotary(acc, rows, scale):
        for s in range(acc.shape[1] // 128):
            a = acc[:, s * 128:(s + 1) * 128]
            half = s % 2
            cs = cos_ref[rows, half * 128:(half + 1) * 128]
            sn = sin_ref[rows, half * 128:(half + 1) * 128]
            r = a * cs + pltpu.roll(a, 64, 1) * sn
            if scale != 1.0:
                r = r * scale
            o_ref[0, rows, s * 128:(s + 1) * 128] = r.astype(BF16)

    def section(kind, first, src):
        w = (wh_ref if src == "h" else wt_ref)[...].astype(BF16)
        for r in range(tm // sub):
            rows = slice(r * sub, (r + 1) * sub)
            if first:
                u = _layer_norm(x_ref[0, rows, :]) * (1.0 + sc_ref[0]) + sh_ref[0]
                ub = u.astype(BF16)
                u_ref[rows, :] = ub
                gt_ref[0, :, rows] = lax.dot_general(wg_ref[...], ub, NT_DIMS,
                                                     preferred_element_type=F32) + bg_ref[...]
            else:
                ub = u_ref[rows, :]
            acc = jnp.dot(ub, w, preferred_element_type=F32)
            if kind == "rot":
                rotary(acc, rows, 1.0)
            elif kind == "rot_scale":
                rotary(acc, rows, KEY_SCALE)
            elif kind == "scale":
                o_ref[0, rows, :] = (acc * KEY_SCALE).astype(BF16)
            else:
                o_ref[0, rows, :] = acc.astype(BF16)

    variants = {}
    for s, key in enumerate(zip(kinds, srcs)):
        variants.setdefault(key + (s == 0,), []).append(s)
    for (kind, src, first), secs in variants.items():
        cond = functools.reduce(jnp.logical_or, [j == s for s in secs])

        @pl.when(cond)
        def _(kind=kind, first=first, src=src):
            section(kind, first, src)


def _proj(x, sh, sc, w_head, w_tail, sections, w_gate_t, b_gate, kinds, tables=None):
    B, T, D = x.shape
    n_sec = len(kinds)
    tm = min(PROJ_TM, T)
    tn = BRANCH_W
    assert T % tm == 0
    srcs = tuple(src for src, _ in sections)
    hb, tb, h_last, t_last = [], [], 0, 0
    for src, blk in sections:
        h_last, t_last = (blk, t_last) if src == "h" else (h_last, blk)
        hb.append(h_last)
        tb.append(t_last)
    in_specs = [
        pl.BlockSpec((1, tm, D), lambda i, b, j, hb, tb: (b, i, 0)),
        pl.BlockSpec((1, 1, D), lambda i, b, j, hb, tb: (b, 0, 0)),
        pl.BlockSpec((1, 1, D), lambda i, b, j, hb, tb: (b, 0, 0)),
        pl.BlockSpec((D, tn), lambda i, b, j, hb, tb: (0, hb[j])),
        pl.BlockSpec((D, tn), lambda i, b, j, hb, tb: (0, tb[j])),
        pl.BlockSpec((N_GATES, D), lambda i, b, j, hb, tb: (0, 0)),
        pl.BlockSpec((N_GATES, 1), lambda i, b, j, hb, tb: (0, 0)),
    ]
    args = [x, sh, sc, w_head, w_tail, w_gate_t, b_gate]
    if tables is not None:
        in_specs += [pl.BlockSpec((tm, HEAD_W), lambda i, b, j, hb, tb: (i, 0),
                                  pipeline_mode=pl.Buffered(1))] * 2
        args += list(tables)
    grid_spec = pltpu.PrefetchScalarGridSpec(
        num_scalar_prefetch=2,
        grid=(T // tm, B, n_sec),
        in_specs=in_specs,
        out_specs=(pl.BlockSpec((1, tm, tn), lambda i, b, j, hb, tb: (b, i, j)),
                   pl.BlockSpec((1, N_GATES, tm), lambda i, b, j, hb, tb: (b, 0, i))),
        scratch_shapes=[pltpu.VMEM((tm, D), BF16)],
    )
    return pl.pallas_call(
        functools.partial(_proj_kernel, kinds, srcs),
        out_shape=(jax.ShapeDtypeStruct((B, T, n_sec * tn), BF16),
                   jax.ShapeDtypeStruct((B, N_GATES, T), F32)),
        grid_spec=grid_spec,
        compiler_params=_cparams("parallel", "parallel", "arbitrary", vmem=PROJ_VMEM_LIMIT),
        name="proj_lat" if tables is not None else "proj_ctx",
    )(jnp.asarray(hb, I32), jnp.asarray(tb, I32), *args)


def _ret_build(dl_ref, q_ref, k_ref, v_ref, rg_ref, ck_ref, cv_ref, o_ref,
               sf_ref, sb_ref, fs_ref, bs_ref, dec_ref, d_ref):
    h = pl.program_id(1)
    L = SCAN_L
    n_chunks = q_ref.shape[1] // L
    n_ctx_chunks = ck_ref.shape[1] // L
    lgf = _log_sigmoid(jnp.full((1, 1), dl_ref[0, h], F32))
    lgb = _log_sigmoid(jnp.full((1, 1), dl_ref[1, h], F32))

    ri = lax.broadcasted_iota(I32, (L, L), 0)
    ci = lax.broadcasted_iota(I32, (L, L), 1)
    rel = (ri - ci).astype(F32)
    d_ref[...] = jnp.where(rel >= 0.0, jnp.exp(jnp.maximum(rel, 0.0) * lgf),
                           jnp.exp(jnp.maximum(-rel, 0.0) * lgb))
    row = lax.broadcasted_iota(I32, (L, HEAD_W), 0).astype(F32)
    dec_ref[0] = jnp.exp((row + 1.0) * lgf)
    dec_ref[1] = jnp.exp((L - 1.0 - row) * lgf)
    dec_ref[2] = jnp.exp((L - row) * lgb)
    dec_ref[3] = jnp.exp(row * lgb)
    cdf = jnp.exp(L * lgf)
    cdb = jnp.exp(L * lgb)

    def update(s_ref, kc, vc, kd, cd):
        kdec = (kc.astype(F32) * kd).astype(BF16)
        s_ref[...] = s_ref[...] * cd + lax.dot_general(kdec, vc, TN_DIMS, preferred_element_type=F32)

    sf_ref[...] = jnp.zeros_like(sf_ref)
    sb_ref[...] = jnp.zeros_like(sb_ref)
    for c in range(n_ctx_chunks):
        update(sf_ref, ck_ref[0, c * L:(c + 1) * L, :], cv_ref[0, c * L:(c + 1) * L, :], dec_ref[1], cdf)
    for c in reversed(range(n_ctx_chunks)):
        update(sb_ref, ck_ref[0, c * L:(c + 1) * L, :], cv_ref[0, c * L:(c + 1) * L, :], dec_ref[3], cdb)

    def state_pass(i, carry):
        cb = n_chunks - 1 - i
        rf = pl.multiple_of(i * L, L)
        rb = pl.multiple_of(cb * L, L)
        fs_ref[i] = sf_ref[...].astype(BF16)
        bs_ref[cb] = sb_ref[...].astype(BF16)
        update(sf_ref, k_ref[0, pl.ds(rf, L), :], v_ref[0, pl.ds(rf, L), :], dec_ref[1], cdf)
        update(sb_ref, k_ref[0, pl.ds(rb, L), :], v_ref[0, pl.ds(rb, L), :], dec_ref[3], cdb)
        return carry

    def finish_states():
        fs_ref[n_chunks - 1] = sf_ref[...].astype(BF16)
        bs_ref[0] = sb_ref[...].astype(BF16)

    def out_chunk(c):
        r0 = pl.multiple_of(c * L, L)
        q = q_ref[0, pl.ds(r0, L), :]
        k = k_ref[0, pl.ds(r0, L), :]
        v = v_ref[0, pl.ds(r0, L), :]
        s = lax.dot_general(q, k, NT_DIMS, preferred_element_type=F32)
        att = (s * d_ref[...]).astype(BF16)
        o = jnp.dot(att, v, preferred_element_type=F32)
        o = o + jnp.dot(q, fs_ref[c], preferred_element_type=F32) * dec_ref[0]
        o = o + jnp.dot(q, bs_ref[c], preferred_element_type=F32) * dec_ref[2]
        rg = rg_ref[0, pl.ds(r0, L), :].astype(F32)
        o_ref[0, pl.ds(r0, L), :] = (_layer_norm(o) * _silu(rg)).astype(BF16)

    return state_pass, finish_states, out_chunk


def _ret_scratch(n_chunks):
    return [pltpu.VMEM((HEAD_W, HEAD_W), F32),
            pltpu.VMEM((HEAD_W, HEAD_W), F32),
            pltpu.VMEM((n_chunks, HEAD_W, HEAD_W), BF16),
            pltpu.VMEM((n_chunks, HEAD_W, HEAD_W), BF16),
            pltpu.VMEM((4, SCAN_L, HEAD_W), F32),
            pltpu.VMEM((SCAN_L, SCAN_L), F32)]


def _mlstm_build(qp_ref, kp_ref, v_ref, mo_ref, ckp_ref, cv_ref, gt_ref, cgt_ref,
                 wq_ref, bq_ref, wk_ref, bk_ref, o_ref,
                 tab_ref, row_ref,
                 cf_ref, mf_ref, cb_ref, mb_ref, cfs_ref, mfs_ref, cbs_ref, mbs_ref, mask_ref,
                 xf_ref, q_ref, k_ref, ck_ref):
    L = SCAN_L
    T = qp_ref.shape[1]
    Tc = ckp_ref.shape[1]
    n_chunks = T // L
    n_ctx_chunks = Tc // L
    CV = CONV_ROWS

    def conv_stage(src_ref, t_len):
        xf_ref[pl.ds(0, 8), :] = jnp.zeros((8, HEAD_W), F32)
        xf_ref[pl.ds(8 + t_len, 8), :] = jnp.zeros((8, HEAD_W), F32)
        xf_ref[pl.ds(8, t_len), :] = src_ref[0].astype(F32)

    def conv_chunk(c, w, b, dst_ref, scale):
        r0 = pl.multiple_of(c * CV, CV)
        win = xf_ref[pl.ds(r0, CV + 16), :]
        prev = pltpu.roll(win, 1, 0)[8:8 + CV, :]
        cur = win[8:8 + CV, :]
        nxt = pltpu.roll(win, CV + 15, 0)[8:8 + CV, :]
        y = _silu(prev * w[0:1, :] + cur * w[1:2, :] + nxt * w[2:3, :] + b)
        if scale != 1.0:
            y = y * scale
        dst_ref[pl.ds(r0, CV), :] = y.astype(BF16)

    def conv_all(src_ref, w_ref, b_ref, dst_ref, t_len, scale):
        conv_stage(src_ref, t_len)
        w = w_ref[...]
        b = b_ref[...]

        def body(c, carry):
            conv_chunk(c, w, b, dst_ref, scale)
            return carry

        lax.fori_loop(0, t_len // CV, body, 0)

    conv_all(kp_ref, wk_ref, bk_ref, k_ref, T, KEY_SCALE)
    conv_all(ckp_ref, wk_ref, bk_ref, ck_ref, Tc, KEY_SCALE)
    conv_stage(qp_ref, T)
    q_per_chunk = L // CV

    def conv_q(c):
        for u in range(q_per_chunk):
            conv_chunk(c * q_per_chunk + u, wq_ref[...], bq_ref[...], q_ref, 1.0)

    ri = lax.broadcasted_iota(I32, (L, L), 0)
    ci = lax.broadcasted_iota(I32, (L, L), 1)
    tri_u = (ri <= ci).astype(BF16)
    lane8 = lax.broadcasted_iota(I32, (8, L), 1)
    sub8 = lax.broadcasted_iota(I32, (8, L), 0)
    sel_r = lax.broadcasted_iota(I32, (24, 8 * 128), 0) % 8
    sel_c = lax.broadcasted_iota(I32, (24, 8 * 128), 1) // 128
    sel3 = (sel_r == sel_c).astype(BF16)
    ones_cols = jnp.ones((L, AUG_W - HEAD_W), BF16)

    def chunk_tables(g8, n_used, state_only):
        i_f, i_b = g8[0], g8[2]
        lf_f, lf_b = _log_sigmoid(g8[1]), _log_sigmoid(g8[3])
        cs3 = jnp.dot(_split3(jnp.concatenate([lf_f, lf_b], axis=0)), tri_u,
                      preferred_element_type=F32)
        cs = cs3[0:16] + cs3[16:32] + cs3[32:48]
        b_f = cs[0:8]
        b_b = cs[8:16, L - 1:L] - cs[8:16] + lf_b
        z_f = i_f - b_f
        z_b = i_b - b_b
        g_f = b_f[:, L - 1:L] - b_f + i_f
        g_b = b_b[:, 0:1] - b_b + i_b
        mf, mb = z_f, z_b
        s = 1
        while s < L:
            mf = jnp.maximum(mf, jnp.where(lane8 >= s, pltpu.roll(mf, s, 1), NEG_INF))
            mb = jnp.maximum(mb, jnp.where(lane8 < L - s, pltpu.roll(mb, L - s, 1), NEG_INF))
            s *= 2
        mb = jnp.where(lane8 < L - 1, pltpu.roll(mb, L - 1, 1), NEG_INF)
        reps = [None if state_only and t not in (2, 5) else
                lax.dot_general(_split3(val), sel3[:, 0:n_used * 128], TN_DIMS, preferred_element_type=F32)
                for t, val in enumerate((mf, b_f, g_f, mb, b_b, g_b))]

        def rows_of(c):
            out = jnp.zeros((8, L), F32)
            for r, val in enumerate((z_f, z_b, g_f, g_b, b_f, b_b)):
                out = jnp.where(sub8 == r, val[c:c + 1], out)
            return out

        return rows_of, reps

    lat_rows, lat_reps = chunk_tables(gt_ref[0, 0], n_chunks, False)
    for c in range(n_chunks):
        row_ref[c] = lat_rows(c)
        for t in range(N_TAB):
            tab_ref[t, c * L:(c + 1) * L, :] = lat_reps[t][:, c * 128:(c + 1) * 128]

    def lanes2(x):
        return jnp.concatenate([x, x], axis=1)

    def advance(k, v, g_rep, g_row, b_last, c_ref, m_ref):
        m = m_ref[...]
        m_new = jnp.maximum(b_last + m, jnp.max(g_row, axis=-1, keepdims=True))
        kw = (k.astype(F32) * jnp.exp(lanes2(g_rep) - m_new)).astype(BF16)
        v_aug = jnp.concatenate([v, ones_cols], axis=1)
        c_ref[...] = jnp.exp(b_last + m - m_new) * c_ref[...] + lax.dot_general(
            kw, v_aug, TN_DIMS, preferred_element_type=F32)
        m_ref[...] = m_new

    for r in (cf_ref, mf_ref, cb_ref, mb_ref):
        r[...] = jnp.zeros_like(r)
    ctx_rows, ctx_reps = chunk_tables(cgt_ref[0, 0], n_ctx_chunks, True)
    for c in range(n_ctx_chunks):
        rows = ctx_rows(c)
        advance(ck_ref[c * L:(c + 1) * L, :], cv_ref[0, c * L:(c + 1) * L, :],
                ctx_reps[2][:, c * 128:(c + 1) * 128], rows[2:3], rows[4:5, L - 1:L], cf_ref, mf_ref)
    for c in reversed(range(n_ctx_chunks)):
        rows = ctx_rows(c)
        advance(ck_ref[c * L:(c + 1) * L, :], cv_ref[0, c * L:(c + 1) * L, :],
                ctx_reps[5][:, c * 128:(c + 1) * 128], rows[3:4], rows[5:6, 0:1], cb_ref, mb_ref)

    def state_pass(i, carry):
        cb = n_chunks - 1 - i
        rf = pl.multiple_of(i * L, L)
        rb = pl.multiple_of(cb * L, L)
        cfs_ref[i] = cf_ref[...].astype(BF16)
        mfs_ref[i] = mf_ref[...]
        cbs_ref[cb] = cb_ref[...].astype(BF16)
        mbs_ref[cb] = mb_ref[...]
        rows_f = row_ref[i]
        rows_b = row_ref[cb]
        advance(k_ref[pl.ds(rf, L), :], v_ref[0, pl.ds(rf, L), :], tab_ref[2, pl.ds(rf, L), :],
                rows_f[2:3], rows_f[4:5, L - 1:L], cf_ref, mf_ref)
        advance(k_ref[pl.ds(rb, L), :], v_ref[0, pl.ds(rb, L), :], tab_ref[5, pl.ds(rb, L), :],
                rows_b[3:4], rows_b[5:6, 0:1], cb_ref, mb_ref)
        conv_q(i)
        return carry

    def finish_states():
        conv_q(n_chunks - 1)
        cfs_ref[n_chunks - 1] = cf_ref[...].astype(BF16)
        mfs_ref[n_chunks - 1] = mf_ref[...]
        cbs_ref[0] = cb_ref[...].astype(BF16)
        mbs_ref[0] = mb_ref[...]

    def direction(q, v_aug, s, z_row, zmax_rep, b_rep, mask, c_in, m_in):
        mx = jnp.maximum(zmax_rep, m_in)
        att = s * jnp.exp((z_row - lanes2(mx)) + mask)
        na = jnp.dot(att.astype(BF16), v_aug, preferred_element_type=F32)
        qa = jnp.dot(q, c_in, preferred_element_type=F32)
        a = jnp.exp(m_in - mx)
        num = na[:, 0:HEAD_W] + lanes2(a) * qa[:, 0:HEAD_W]
        den = na[:, HEAD_W:] + a * qa[:, HEAD_W:]
        scale = 1.0 / jnp.maximum(jnp.abs(den), jnp.exp(-(b_rep + mx)))
        return num * lanes2(scale)

    mask_ref[0] = jnp.where(ci <= ri, 0.0, NEG_INF)
    mask_ref[1] = jnp.where(ci > ri, 0.0, NEG_INF)

    def out_chunk(c):
        r0 = pl.multiple_of(c * L, L)
        q = q_ref[pl.ds(r0, L), :]
        k = k_ref[pl.ds(r0, L), :]
        v_aug = jnp.concatenate([v_ref[0, pl.ds(r0, L), :], ones_cols], axis=1)
        s = lax.dot_general(q, k, NT_DIMS, preferred_element_type=F32)
        rows = row_ref[c]
        tot = direction(q, v_aug, s, rows[0:1], tab_ref[0, pl.ds(r0, L), :], tab_ref[1, pl.ds(r0, L), :],
                        mask_ref[0], cfs_ref[c], mfs_ref[c])
        tot = tot + direction(q, v_aug, s, rows[1:2], tab_ref[3, pl.ds(r0, L), :],
                              tab_ref[4, pl.ds(r0, L), :], mask_ref[1], cbs_ref[c], mbs_ref[c])
        mo = mo_ref[0, pl.ds(r0, L), :].astype(F32)
        o_ref[0, pl.ds(r0, L), :] = (_layer_norm(tot) * jax.nn.sigmoid(mo)).astype(BF16)

    return state_pass, finish_states, out_chunk


def _mlstm_scratch(T, Tc, n_chunks):
    state = [pltpu.VMEM((HEAD_W, AUG_W), F32), pltpu.VMEM((1, 1), F32)]
    snaps = [pltpu.VMEM((n_chunks, HEAD_W, AUG_W), BF16), pltpu.VMEM((n_chunks, 1, 1), F32)]
    return [pltpu.VMEM((N_TAB, T, 128), F32), pltpu.VMEM((n_chunks, 8, SCAN_L), F32)] \
        + state + state + snaps + snaps + [pltpu.VMEM((2, SCAN_L, SCAN_L), F32)] \
        + [pltpu.VMEM((T + 16, HEAD_W), F32), pltpu.VMEM((T, HEAD_W), BF16),
           pltpu.VMEM((T, HEAD_W), BF16), pltpu.VMEM((Tc, HEAD_W), BF16)]


def _scan_kernel(n_ret_scratch, dl_ref, rq_ref, rk_ref, rv_ref, rg_ref, rck_ref, rcv_ref,
                 mq_ref, mk_ref, mv_ref, mo_ref, mck_ref, mcv_ref, gt_ref, cgt_ref,
                 wq_ref, bq_ref, wk_ref, bk_ref, r_ref, m_ref, *scratch):
    n_chunks = rq_ref.shape[1] // SCAN_L
    ret = _ret_build(dl_ref, rq_ref, rk_ref, rv_ref, rg_ref, rck_ref, rcv_ref, r_ref,
                     *scratch[:n_ret_scratch])
    mls = _mlstm_build(mq_ref, mk_ref, mv_ref, mo_ref, mck_ref, mcv_ref, gt_ref, cgt_ref,
                       wq_ref, bq_ref, wk_ref, bk_ref, m_ref, *scratch[n_ret_scratch:])

    def state_pass(i, carry):
        ret[0](i, carry)
        mls[0](i, carry)
        return carry

    lax.fori_loop(0, n_chunks - 1, state_pass, 0)
    ret[1]()
    mls[1]()

    def out_pass(i, carry):
        for c in (2 * i, 2 * i + 1):
            ret[2](c)
            mls[2](c)
        return carry

    lax.fori_loop(0, n_chunks // 2, out_pass, 0)


def _scans(decay_logit, p_lat, p_ctx, gt, cgt, conv_w, conv_b, ret_lat, ret_ctx, ml_lat, ml_ctx):
    B, T, _ = p_lat.shape
    Tc = p_ctx.shape[1]
    assert T % (2 * SCAN_L) == 0 and Tc % SCAN_L == 0 and T // SCAN_L <= 8
    n_chunks = T // SCAN_L

    def lat(sec):
        return pl.BlockSpec((1, T, HEAD_W), lambda b, h: (b, 0, sec * HEADS + h))

    def cx(sec):
        return pl.BlockSpec((1, Tc, HEAD_W), lambda b, h: (b, 0, sec * HEADS + h))

    gates = pl.BlockSpec((1, 1, N_GK, 8, SCAN_L), lambda b, h: (b, h, 0, 0, 0))
    out = pl.BlockSpec((1, T, HEAD_W), lambda b, h: (b, 0, h))
    conv_specs = [pl.BlockSpec((3, HEAD_W), lambda b, h: (0, h)),
                  pl.BlockSpec((1, HEAD_W), lambda b, h: (0, h)),
                  pl.BlockSpec((3, HEAD_W), lambda b, h: (0, HEADS + h)),
                  pl.BlockSpec((1, HEAD_W), lambda b, h: (0, HEADS + h))]
    ret_scratch = _ret_scratch(n_chunks)
    return pl.pallas_call(
        functools.partial(_scan_kernel, len(ret_scratch)),
        out_shape=(jax.ShapeDtypeStruct((B, T, BRANCH_W), BF16),
                   jax.ShapeDtypeStruct((B, T, BRANCH_W), BF16)),
        grid=(B, HEADS),
        in_specs=[pl.BlockSpec(memory_space=pltpu.SMEM)]
        + [lat(s) for s in ret_lat] + [cx(s) for s in ret_ctx]
        + [lat(s) for s in ml_lat] + [cx(s) for s in ml_ctx] + [gates, gates] + conv_specs,
        out_specs=(out, out),
        scratch_shapes=ret_scratch + _mlstm_scratch(T, Tc, n_chunks),
        compiler_params=_cparams("parallel", "parallel"),
        name="scans",
    )(decay_logit, *([p_lat] * 4), *([p_ctx] * 2), *([p_lat] * 4), *([p_ctx] * 2), gt, cgt,
      conv_w, conv_b, conv_w, conv_b)


def _merge_kernel(alpha, r_ref, m_ref, gr_ref, gm_ref, x_ref, g1_ref, sh2_ref, sc2_ref,
                  lng_ref, lnb_ref, wr_ref, wm_ref, wo_ref, wrt_ref, brt_ref,
                  x1_ref, ua_ref, ub_ref, lt_ref):
    yr = jnp.dot(r_ref[0], wr_ref[...], preferred_element_type=F32)
    ym = jnp.dot(m_ref[0], wm_ref[...], preferred_element_type=F32)
    y = jax.nn.sigmoid(gr_ref[0].astype(F32)) * yr + jax.nn.sigmoid(gm_ref[0].astype(F32)) * ym
    yo = jnp.dot(y.astype(BF16), wo_ref[...], preferred_element_type=F32)
    x1 = _layer_norm(alpha * x_ref[0] + g1_ref[0] * yo) * lng_ref[...] + lnb_ref[...]
    x1_ref[0] = x1
    u2 = _layer_norm(x1) * (1.0 + sc2_ref[0]) + sh2_ref[0]
    lt_ref[0] = lax.dot_general(wrt_ref[...], u2.astype(BF16), NT_DIMS,
                                preferred_element_type=F32) + brt_ref[...]
    ua_ref[0] = _pack_pairs(u2[:, 0:PACK_W], u2[:, PACK_W:2 * PACK_W])
    ub_ref[0] = _pack_pairs(u2[:, 2 * PACK_W:3 * PACK_W], u2[:, 3 * PACK_W:4 * PACK_W])


def _merge(alpha, r, m, p_lat, sec_gates, x, g1, sh2, sc2, lng, lnb, wr, wm, wo, wrt, brt):
    B, T, D = x.shape
    tm = MERGE_TM

    def tile(w):
        return pl.BlockSpec((1, tm, w), lambda b, i: (b, i, 0))

    def sec(s):
        return pl.BlockSpec((1, tm, BRANCH_W), lambda b, i: (b, i, s))

    def mod():
        return pl.BlockSpec((1, 1, D), lambda b, i: (b, 0, 0))

    def const(shape):
        return pl.BlockSpec(shape, lambda b, i: (0,) * len(shape))

    return pl.pallas_call(
        functools.partial(_merge_kernel, alpha),
        out_shape=(jax.ShapeDtypeStruct((B, T, D), F32),
                   jax.ShapeDtypeStruct((B, T, PACK_W), U32),
                   jax.ShapeDtypeStruct((B, T, PACK_W), U32),
                   jax.ShapeDtypeStruct((B, ROUTE_ROWS, T), F32)),
        grid=(B, T // tm),
        in_specs=[tile(BRANCH_W), tile(BRANCH_W), sec(sec_gates[0]), sec(sec_gates[1]), tile(D),
                  mod(), mod(), mod(), const((1, D)), const((1, D)),
                  const((BRANCH_W, D)), const((BRANCH_W, D)), const((D, D)),
                  const((ROUTE_ROWS, D)), const((ROUTE_ROWS, 1))],
        out_specs=(tile(D), tile(PACK_W), tile(PACK_W),
                   pl.BlockSpec((1, ROUTE_ROWS, tm), lambda b, i: (b, 0, i))),
        compiler_params=_cparams("parallel", "parallel"),
        name="merge",
    )(r, m, p_lat, p_lat, x, g1, sh2, sc2, lng, lnb, wr, wm, wo, wrt, brt)


def _route_kernel(lt_ref, ri_ref, rw_ref, cnt_ref, carry_ref, u_ref):
    i = pl.program_id(0)
    tm = lt_ref.shape[2]

    @pl.when(i == 0)
    def _():
        carry_ref[...] = jnp.zeros_like(carry_ref)
        r = lax.broadcasted_iota(I32, (tm, tm), 0)
        c = lax.broadcasted_iota(I32, (tm, tm), 1)
        u_ref[...] = (r < c).astype(BF16)

    lt = lt_ref[0]
    lg = lt[0:N_GROUPS, :]
    eg = jnp.exp(lg - jnp.max(lg, axis=0, keepdims=True))
    pg = eg / jnp.sum(eg, axis=0, keepdims=True)
    pg_top = jnp.max(pg, axis=0, keepdims=True)
    rows_g = lax.broadcasted_iota(I32, pg.shape, 0)
    g_idx = jnp.min(jnp.where(pg == pg_top, rows_g, N_GROUPS), axis=0, keepdims=True)

    le = jnp.zeros((EXP_PER_GROUP, tm), F32)
    for g in range(N_GROUPS):
        lo = 8 + g * EXP_PER_GROUP
        le = jnp.where(g_idx == g, lt[lo:lo + EXP_PER_GROUP, :], le)
    ee = jnp.exp(le - jnp.max(le, axis=0, keepdims=True))
    pe = ee / jnp.sum(ee, axis=0, keepdims=True)
    rows_e = lax.broadcasted_iota(I32, pe.shape, 0)
    v1 = jnp.max(pe, axis=0, keepdims=True)
    i1 = jnp.min(jnp.where(pe == v1, rows_e, EXP_PER_GROUP), axis=0, keepdims=True)
    pe2 = jnp.where(rows_e == i1, -1.0, pe)
    v2 = jnp.max(pe2, axis=0, keepdims=True)
    i2 = jnp.min(jnp.where(pe2 == v2, rows_e, EXP_PER_GROUP), axis=0, keepdims=True)
    den = v1 + v2
    rw_ref[0:1, :] = pg_top * v1 / den
    rw_ref[1:2, :] = pg_top * v2 / den
    e1 = g_idx * EXP_PER_GROUP + i1
    e2 = g_idx * EXP_PER_GROUP + i2

    rows_x = lax.broadcasted_iota(I32, (N_EXPERTS, tm), 0)
    oh1 = (rows_x == e1).astype(F32)
    oh2 = (rows_x == e2).astype(F32)
    both = oh1 + oh2
    before = carry_ref[:, 0:1] + jnp.dot(both.astype(BF16), u_ref[...], preferred_element_type=F32)
    ri_ref[0:1, :] = e1
    ri_ref[1:2, :] = e2
    ri_ref[2:3, :] = jnp.sum(oh1 * before, axis=0, keepdims=True).astype(I32)
    ri_ref[3:4, :] = jnp.sum(oh2 * before, axis=0, keepdims=True).astype(I32)
    carry_ref[...] = carry_ref[...] + jnp.sum(both, axis=1, keepdims=True)
    cnt_ref[...] = carry_ref[...].astype(I32)


def _route(lt):
    B, _, T = lt.shape
    tm = ROUTE_TM
    per_b = T // tm
    n = B * T
    return pl.pallas_call(
        _route_kernel,
        out_shape=(jax.ShapeDtypeStruct((4, n), I32),
                   jax.ShapeDtypeStruct((2, n), F32),
                   jax.ShapeDtypeStruct((N_EXPERTS, 128), I32)),
        grid=(n // tm,),
        in_specs=[pl.BlockSpec((1, ROUTE_ROWS, tm), lambda i: (i // per_b, 0, i % per_b))],
        out_specs=(pl.BlockSpec((4, tm), lambda i: (0, i)),
                   pl.BlockSpec((2, tm), lambda i: (0, i)),
                   pl.BlockSpec((N_EXPERTS, 128), lambda i: (0, 0))),
        scratch_shapes=[pltpu.VMEM((N_EXPERTS, 128), F32), pltpu.VMEM((tm, tm), BF16)],
        compiler_params=_cparams("arbitrary"),
        name="route",
    )(lt)


def _sc_mesh():
    return plsc.VectorSubcoreMesh(core_axis_name="c", subcore_axis_name="s")


def _sc_scatter2(rows, idx0, idx1, n_out):
    m, w = rows.shape

    @functools.partial(pl.kernel, out_type=jax.ShapeDtypeStruct((n_out, w), rows.dtype),
                       mesh=_sc_mesh(), scratch_types=[])
    def k(x_hbm, i0_hbm, i1_hbm, o_hbm):
        def body(x_vmem, i0_vmem, i1_vmem):
            pltpu.sync_copy(x_vmem, o_hbm.at[i0_vmem.at[0]])
            pltpu.sync_copy(x_vmem, o_hbm.at[i1_vmem.at[0]])

        pltpu.emit_pipeline(
            body,
            grid=(m // SC_WIN,),
            in_specs=[pl.BlockSpec((SC_WIN, w), lambda i: (i, 0)),
                      pl.BlockSpec((1, SC_WIN), lambda i: (0, i)),
                      pl.BlockSpec((1, SC_WIN), lambda i: (0, i))],
            out_specs=[],
            core_axis_name=("c", "s"),
            dimension_semantics=(pltpu.PARALLEL,),
        )(x_hbm, i0_hbm, i1_hbm)

    return k(rows, idx0.reshape(1, m), idx1.reshape(1, m))


def _sc_gather(table, idx):
    m = idx.shape[0]
    w = table.shape[1]

    @functools.partial(pl.kernel, out_type=jax.ShapeDtypeStruct((m, w), table.dtype),
                       mesh=_sc_mesh(), scratch_types=[])
    def k(t_hbm, i_hbm, o_hbm):
        def body(i_vmem, o_vmem):
            pltpu.sync_copy(t_hbm.at[i_vmem.at[0]], o_vmem)

        pltpu.emit_pipeline(
            body,
            grid=(m // SC_WIN,),
            in_specs=[pl.BlockSpec((1, SC_WIN), lambda i: (0, i))],
            out_specs=[pl.BlockSpec((SC_WIN, w), lambda i: (i, 0))],
            core_axis_name=("c", "s"),
            dimension_semantics=(pltpu.PARALLEL,),
        )(i_hbm, o_hbm)

    return k(table, idx.reshape(1, m))


def _expert_kernel(be_ref, nv_ref, xa_ref, xb_ref, w1f_ref, w3f_ref, w2f_ref, ya_ref, yb_ref,
                   w1_ref, w3_ref, w2_ref):
    j = pl.program_id(0)
    nv = nv_ref[j]

    @pl.when(jnp.logical_or(j == 0, be_ref[j] != be_ref[jnp.maximum(j - 1, 0)]))
    def _():
        w1_ref[0] = w1f_ref[0].astype(BF16)
        w3_ref[0] = w3f_ref[0].astype(BF16)
        w2_ref[0] = w2f_ref[0].astype(BF16)

    @pl.when(nv > 0)
    def _():
        valid = lax.broadcasted_iota(I32, xa_ref.shape, 0) < nv
        zero = jnp.zeros(xa_ref.shape, U32)
        parts = _unpack_pairs(jnp.where(valid, xa_ref[...], zero)) + \
            _unpack_pairs(jnp.where(valid, xb_ref[...], zero))
        x = jnp.concatenate([p.astype(BF16) for p in parts], axis=1)
        h1 = jnp.dot(x, w1_ref[0], preferred_element_type=F32)
        h3 = jnp.dot(x, w3_ref[0], preferred_element_type=F32)
        y = jnp.dot((_silu(h1) * h3).astype(BF16), w2_ref[0], preferred_element_type=F32)
        ya_ref[...] = _pack_pairs(y[:, 0:PACK_W], y[:, PACK_W:2 * PACK_W])
        yb_ref[...] = _pack_pairs(y[:, 2 * PACK_W:3 * PACK_W], y[:, 3 * PACK_W:4 * PACK_W])

    @pl.when(nv == 0)
    def _():
        ya_ref[...] = jnp.zeros_like(ya_ref)
        yb_ref[...] = jnp.zeros_like(yb_ref)


def _experts(block_exp, n_valid, xa, xb, w1, w3, w2):
    n_slots = xa.shape[0]
    n_blocks = n_slots // MOE_BLK
    d, de = w1.shape[1], w1.shape[2]
    slot = pl.BlockSpec((MOE_BLK, PACK_W), lambda j, be, nv: (j, 0))
    grid_spec = pltpu.PrefetchScalarGridSpec(
        num_scalar_prefetch=2,
        grid=(n_blocks,),
        in_specs=[slot, slot,
                  pl.BlockSpec((1, d, de), lambda j, be, nv: (be[j], 0, 0)),
                  pl.BlockSpec((1, d, de), lambda j, be, nv: (be[j], 0, 0)),
                  pl.BlockSpec((1, de, d), lambda j, be, nv: (be[j], 0, 0))],
        out_specs=(slot, slot),
        scratch_shapes=[pltpu.VMEM((1, d, de), BF16), pltpu.VMEM((1, d, de), BF16),
                        pltpu.VMEM((1, de, d), BF16)],
    )
    return pl.pallas_call(
        _expert_kernel,
        out_shape=(jax.ShapeDtypeStruct((n_slots, PACK_W), U32),
                   jax.ShapeDtypeStruct((n_slots, PACK_W), U32)),
        grid_spec=grid_spec,
        compiler_params=_cparams("arbitrary"),
        name="experts",
    )(block_exp, n_valid, xa, xb, w1, w3, w2)


def _final_kernel(alpha, x1_ref, a0_ref, b0_ref, a1_ref, b1_ref, w_ref, g2_ref, lng_ref, lnb_ref, o_ref):
    w = w_ref[...]
    w0 = w[:, 0:1]
    w1 = w[:, 1:2]
    parts0 = _unpack_pairs(a0_ref[...]) + _unpack_pairs(b0_ref[...])
    parts1 = _unpack_pairs(a1_ref[...]) + _unpack_pairs(b1_ref[...])
    f = jnp.concatenate([w0 * p0 + w1 * p1 for p0, p1 in zip(parts0, parts1)], axis=1)
    o_ref[0] = _layer_norm(alpha * x1_ref[0] + g2_ref[0] * f) * lng_ref[...] + lnb_ref[...]


def _final(alpha, x1, ya, yb, w, g2, lng, lnb):
    B, T, D = x1.shape
    tm = MERGE_TM
    per_b = T // tm
    n_tiles = B * per_b

    def rows(k):
        return pl.BlockSpec((tm, PACK_W), lambda b, i: (k * n_tiles + b * per_b + i, 0))

    return pl.pallas_call(
        functools.partial(_final_kernel, alpha),
        out_shape=jax.ShapeDtypeStruct((B, T, D), F32),
        grid=(B, per_b),
        in_specs=[pl.BlockSpec((1, tm, D), lambda b, i: (b, i, 0)),
                  rows(0), rows(0), rows(1), rows(1),
                  pl.BlockSpec((tm, 2), lambda b, i: (b * per_b + i, 0)),
                  pl.BlockSpec((1, 1, D), lambda b, i: (b, 0, 0)),
                  pl.BlockSpec((1, D), lambda b, i: (0, 0)),
                  pl.BlockSpec((1, D), lambda b, i: (0, 0))],
        out_specs=pl.BlockSpec((1, tm, D), lambda b, i: (b, i, 0)),
        compiler_params=_cparams("parallel", "parallel"),
        name="final",
    )(x1, ya, yb, ya, yb, w, g2, lng, lnb)


def _rotary_tables(T):
    quarter = HEAD_W // 4
    freqs = ROPE_BASE ** (-jnp.arange(quarter, dtype=F32) / quarter)
    t = jnp.arange(T)
    ang_r = (t // GRID_W).astype(F32)[:, None] * freqs[None, :]
    ang_c = (t % GRID_W).astype(F32)[:, None] * freqs[None, :]
    cos = jnp.concatenate([jnp.cos(ang_r)] * 2 + [jnp.cos(ang_c)] * 2, axis=1)
    sin = jnp.concatenate([-jnp.sin(ang_r), jnp.sin(ang_r), -jnp.sin(ang_c), jnp.sin(ang_c)], axis=1)
    return cos, sin


def _per_head_gates(gt):
    B, _, T = gt.shape
    n_chunks = T // SCAN_L
    gth = gt.reshape(B, N_GK, HEADS, n_chunks, SCAN_L).transpose(0, 2, 1, 3, 4)
    return jnp.pad(gth, ((0, 0), (0, 0), (0, 0), (0, 8 - n_chunks), (0, 0)))


def _table_lookup(table, idx):
    sel = idx[..., None] == jnp.arange(table.shape[0], dtype=idx.dtype)
    return jnp.sum(jnp.where(sel, table, 0), axis=-1)


def kernel(x, c, ctx, c_ctx, w_ada, b_ada, w_in, b_mgate, ml_conv_w, ml_conv_b, ret_decay_logit, w_ret_branch, w_ml_branch, w_out, ln1_g, ln1_b, w_rg, b_rg, w_re, b_re, w_e1, w_e3, w_e2, ln2_g, ln2_b):
    B, T, D = x.shape
    depth = w_ada.shape[0]
    assert depth == 1 and D == BRANCH_W and T % GRID_W == 0
    alpha = (2 * depth) ** 0.25
    n_tok = B * T

    n_rows = -(-(B + 1) // 8) * 8
    cs = jnp.zeros((n_rows, D), F32).at[:B].set(c).at[B].set(c_ctx)
    mod = _ada(cs, w_ada[0], b_ada[0][None, :])
    sh1, sc1, g1, sh2, sc2, g2 = [mod[:B, None, i * D:(i + 1) * D] for i in range(6)]
    csh1 = mod[B, 0 * D:1 * D].reshape(1, 1, D)
    csc1 = mod[B, 1 * D:2 * D].reshape(1, 1, D)

    w = w_in[0]
    g_lo = 8 * BRANCH_W
    w_gate_t = w[:, g_lo:g_lo + N_GATES].T.astype(BF16)
    b_gate = b_mgate[0][:, None]
    w_tail = w[:, g_lo + N_GATES:]
    sec_lat = tuple(("h", s) for s in range(8)) + (("t", 0), ("t", 1))
    sec_ctx = (("h", 1), ("h", 2), ("h", 5), ("h", 6))
    kinds_lat = ("rot", "rot_scale") + ("plain",) * 8
    kinds_ctx = ("scale", "plain", "plain", "plain")
    p_lat, gt_lat = _proj(x, sh1, sc1, w, w_tail, sec_lat, w_gate_t, b_gate, kinds_lat,
                          _rotary_tables(T))
    Tc = ctx.shape[1]
    p_ctx, gt_ctx = _proj(ctx.reshape(1, B * Tc, D), csh1, csc1, w, w_tail, sec_ctx, w_gate_t, b_gate,
                          kinds_ctx)
    p_ctx = p_ctx.reshape(B, Tc, -1)
    gt_ctx = gt_ctx.reshape(N_GATES, B, Tc).transpose(1, 0, 2)

    ret, mls = _scans(ret_decay_logit[0], p_lat, p_ctx, _per_head_gates(gt_lat), _per_head_gates(gt_ctx),
                      ml_conv_w[0], ml_conv_b[0][None, :], (0, 1, 2, 3), (0, 1), (4, 5, 6, 7), (2, 3))

    wrt = jnp.zeros((ROUTE_ROWS, D), F32).at[:N_GROUPS].set(w_rg[0].T).at[8:8 + N_EXPERTS].set(w_re[0].T)
    brt = jnp.zeros((ROUTE_ROWS, 1), F32).at[:N_GROUPS, 0].set(b_rg[0]).at[8:8 + N_EXPERTS, 0].set(b_re[0])
    x1, ua, ub, lt = _merge(alpha, ret, mls, p_lat, (8, 9), x, g1, sh2, sc2,
                            ln1_g[0][None, :], ln1_b[0][None, :],
                            w_ret_branch[0].astype(BF16), w_ml_branch[0].astype(BF16),
                            w_out[0].astype(BF16), wrt.astype(BF16), brt)

    ri, rw, cnt = _route(lt)

    counts = cnt[:, 0]
    padded = (counts + MOE_BLK - 1) // MOE_BLK * MOE_BLK
    pad_end = jnp.cumsum(padded)
    pad_off = pad_end - padded
    dest = _table_lookup(pad_off, ri[0:2]) + ri[2:4]
    n_blocks = (2 * n_tok) // MOE_BLK + N_EXPERTS
    n_slots = n_blocks * MOE_BLK
    block_start = jnp.arange(n_blocks, dtype=I32) * MOE_BLK
    block_exp = jnp.minimum((block_start[:, None] >= pad_end[None, :]).sum(1), N_EXPERTS - 1).astype(I32)
    n_valid = jnp.clip(_table_lookup(counts, block_exp) - (block_start - _table_lookup(pad_off, block_exp)),
                       0, MOE_BLK).astype(I32)

    xa = _sc_scatter2(ua.reshape(n_tok, PACK_W), dest[0], dest[1], n_slots)
    xb = _sc_scatter2(ub.reshape(n_tok, PACK_W), dest[0], dest[1], n_slots)
    ya, yb = _experts(block_exp, n_valid, xa, xb, w_e1[0], w_e3[0], w_e2[0])
    dflat = dest.reshape(2 * n_tok)
    ga = _sc_gather(ya, dflat)
    gb = _sc_gather(yb, dflat)
    return _final(alpha, x1, ga, gb, rw.T, g2, ln2_g[0][None, :], ln2_b[0][None, :])
```

```python
import functools

import jax
import jax.numpy as jnp
from jax import lax
from jax.experimental import pallas as pl
from jax.experimental.pallas import tpu as pltpu
from jax.experimental.pallas import tpu_sc as plsc

F32 = jnp.float32
BF16 = jnp.bfloat16
U32 = jnp.uint32
I32 = jnp.int32
HIGHEST = lax.Precision.HIGHEST

HEADS = 4
HEAD_W = 256
BRANCH_W = HEADS * HEAD_W
GRID_W = 64
ROPE_BASE = 10000.0
N_GATES = 16
N_GK = N_GATES // HEADS
N_GROUPS = 4
EXP_PER_GROUP = 8
N_EXPERTS = N_GROUPS * EXP_PER_GROUP
LN_EPS = 1e-5
NEG_INF = -1e30
KEY_SCALE = HEAD_W ** -0.5

SCAN_L = 256
CONV_ROWS = 128
PROJ_TM = 2048
PROJ_SUB = 256
MERGE_TM = 512
ROUTE_TM = 512
MOE_BLK = 512
SC_WIN = 128
PACK_W = 256
ROUTE_ROWS = 64
N_TAB = 6
AUG_W = HEAD_W + 128
VMEM_LIMIT = 48 * 1024 * 1024

NT_DIMS = (((1,), (1,)), ((), ()))
TN_DIMS = (((0,), (0,)), ((), ()))


def _cparams(*sem):
    return pltpu.CompilerParams(dimension_semantics=sem, vmem_limit_bytes=VMEM_LIMIT)


def _layer_norm(x):
    mu = jnp.mean(x, axis=-1, keepdims=True)
    xc = x - mu
    var = jnp.mean(xc * xc, axis=-1, keepdims=True)
    return xc * lax.rsqrt(var + LN_EPS)


def _log_sigmoid(x):
    return jnp.minimum(x, 0.0) - jnp.log1p(jnp.exp(-jnp.abs(x)))


def _silu(x):
    return x * jax.nn.sigmoid(x)


def _pack_pairs(hi, lo):
    hb = lax.bitcast_convert_type(hi.astype(BF16).astype(F32), U32)
    lb = lax.bitcast_convert_type(lo.astype(BF16).astype(F32), U32)
    return (hb & jnp.uint32(0xFFFF0000)) | (lb >> 16)


def _unpack_pairs(p):
    hi = lax.bitcast_convert_type(p & jnp.uint32(0xFFFF0000), F32)
    lo = lax.bitcast_convert_type(p << 16, F32)
    return hi, lo


def _split3(x):
    hi = x.astype(BF16).astype(F32)
    r1 = x - hi
    mid = r1.astype(BF16).astype(F32)
    lo = (r1 - mid).astype(BF16).astype(F32)
    return jnp.concatenate([hi, mid, lo], axis=0).astype(BF16)


def _ada_kernel(c_ref, w_ref, b_ref, o_ref):
    s = _silu(c_ref[...])
    o_ref[...] = jnp.dot(s, w_ref[...], precision=HIGHEST, preferred_element_type=F32) + b_ref[...]


def _ada(cs, w, b):
    rows, d = cs.shape
    cols = w.shape[1]
    tn = 1024
    return pl.pallas_call(
        _ada_kernel,
        out_shape=jax.ShapeDtypeStruct((rows, cols), F32),
        grid=(cols // tn,),
        in_specs=[pl.BlockSpec((rows, d), lambda j: (0, 0)),
                  pl.BlockSpec((d, tn), lambda j: (0, j)),
                  pl.BlockSpec((1, tn), lambda j: (0, j))],
        out_specs=pl.BlockSpec((rows, tn), lambda j: (0, j)),
        compiler_params=_cparams("parallel"),
        name="ada",
    )(cs, w, b)


def _proj_kernel(kinds, x_ref, sh_ref, sc_ref, w_ref, wg_ref, bg_ref, *rest):
    if "rot" in kinds or "rot_scale" in kinds:
        cos_ref, sin_ref, o_ref, gt_ref, u_ref = rest
    else:
        o_ref, gt_ref, u_ref = rest
    j = pl.program_id(2)
    tm = x_ref.shape[1]
    sub = min(PROJ_SUB, tm)

    def rotary(acc, rows, scale):
        for s in range(acc.shape[1] // 128):
            a = acc[:, s * 128:(s + 1) * 128]
            half = s % 2
            cs = cos_ref[rows, half * 128:(half + 1) * 128]
            sn = sin_ref[rows, half * 128:(half + 1) * 128]
            r = a * cs + pltpu.roll(a, 64, 1) * sn
            if scale != 1.0:
                r = r * scale
            o_ref[0, rows, s * 128:(s + 1) * 128] = r.astype(BF16)

    def section(kind, first):
        for r in range(tm // sub):
            rows = slice(r * sub, (r + 1) * sub)
            if first:
                u = _layer_norm(x_ref[0, rows, :]) * (1.0 + sc_ref[0]) + sh_ref[0]
                ub = u.astype(BF16)
                u_ref[rows, :] = ub
                gt_ref[0, :, rows] = lax.dot_general(wg_ref[...], ub, NT_DIMS,
                                                     preferred_element_type=F32) + bg_ref[...]
            else:
                ub = u_ref[rows, :]
            acc = jnp.dot(ub, w_ref[...], preferred_element_type=F32)
            if kind == "rot":
                rotary(acc, rows, 1.0)
            elif kind == "rot_scale":
                rotary(acc, rows, KEY_SCALE)
            elif kind == "scale":
                o_ref[0, rows, :] = (acc * KEY_SCALE).astype(BF16)
            else:
                o_ref[0, rows, :] = acc.astype(BF16)

    variants = {}
    for s, kind in enumerate(kinds):
        variants.setdefault((kind, s == 0), []).append(s)
    for (kind, first), secs in variants.items():
        cond = functools.reduce(jnp.logical_or, [j == s for s in secs])

        @pl.when(cond)
        def _(kind=kind, first=first):
            section(kind, first)


def _proj(x, sh, sc, w_main, w_gate_t, b_gate, kinds, tables=None):
    B, T, D = x.shape
    n_sec = len(kinds)
    tm = min(PROJ_TM, T)
    tn = BRANCH_W
    assert T % tm == 0
    in_specs = [
        pl.BlockSpec((1, tm, D), lambda i, b, j: (b, i, 0)),
        pl.BlockSpec((1, 1, D), lambda i, b, j: (b, 0, 0)),
        pl.BlockSpec((1, 1, D), lambda i, b, j: (b, 0, 0)),
        pl.BlockSpec((D, tn), lambda i, b, j: (0, j)),
        pl.BlockSpec((N_GATES, D), lambda i, b, j: (0, 0)),
        pl.BlockSpec((N_GATES, 1), lambda i, b, j: (0, 0)),
    ]
    args = [x, sh, sc, w_main, w_gate_t, b_gate]
    if tables is not None:
        in_specs += [pl.BlockSpec((tm, HEAD_W), lambda i, b, j: (i, 0))] * 2
        args += list(tables)
    return pl.pallas_call(
        functools.partial(_proj_kernel, kinds),
        out_shape=(jax.ShapeDtypeStruct((B, T, n_sec * tn), BF16),
                   jax.ShapeDtypeStruct((B, N_GATES, T), F32)),
        grid=(T // tm, B, n_sec),
        in_specs=in_specs,
        out_specs=(pl.BlockSpec((1, tm, tn), lambda i, b, j: (b, i, j)),
                   pl.BlockSpec((1, N_GATES, tm), lambda i, b, j: (b, 0, i))),
        scratch_shapes=[pltpu.VMEM((tm, D), BF16)],
        compiler_params=_cparams("parallel", "parallel", "arbitrary"),
        name="proj_lat" if tables is not None else "proj_ctx",
    )(*args)


def _ret_build(dl_ref, q_ref, k_ref, v_ref, rg_ref, ck_ref, cv_ref, o_ref,
               sf_ref, sb_ref, fs_ref, bs_ref, dec_ref, d_ref):
    h = pl.program_id(1)
    L = SCAN_L
    n_chunks = q_ref.shape[1] // L
    n_ctx_chunks = ck_ref.shape[1] // L
    lgf = _log_sigmoid(jnp.full((1, 1), dl_ref[0, h], F32))
    lgb = _log_sigmoid(jnp.full((1, 1), dl_ref[1, h], F32))

    ri = lax.broadcasted_iota(I32, (L, L), 0)
    ci = lax.broadcasted_iota(I32, (L, L), 1)
    rel = (ri - ci).astype(F32)
    d_ref[...] = jnp.where(rel >= 0.0, jnp.exp(jnp.maximum(rel, 0.0) * lgf),
                           jnp.exp(jnp.maximum(-rel, 0.0) * lgb))
    row = lax.broadcasted_iota(I32, (L, HEAD_W), 0).astype(F32)
    dec_ref[0] = jnp.exp((row + 1.0) * lgf)
    dec_ref[1] = jnp.exp((L - 1.0 - row) * lgf)
    dec_ref[2] = jnp.exp((L - row) * lgb)
    dec_ref[3] = jnp.exp(row * lgb)
    cdf = jnp.exp(L * lgf)
    cdb = jnp.exp(L * lgb)

    def update(s_ref, kc, vc, kd, cd):
        kdec = (kc.astype(F32) * kd).astype(BF16)
        s_ref[...] = s_ref[...] * cd + lax.dot_general(kdec, vc, TN_DIMS, preferred_element_type=F32)

    sf_ref[...] = jnp.zeros_like(sf_ref)
    sb_ref[...] = jnp.zeros_like(sb_ref)
    for c in range(n_ctx_chunks):
        update(sf_ref, ck_ref[0, c * L:(c + 1) * L, :], cv_ref[0, c * L:(c + 1) * L, :], dec_ref[1], cdf)
    for c in reversed(range(n_ctx_chunks)):
        update(sb_ref, ck_ref[0, c * L:(c + 1) * L, :], cv_ref[0, c * L:(c + 1) * L, :], dec_ref[3], cdb)

    def state_pass(i, carry):
        cb = n_chunks - 1 - i
        rf = pl.multiple_of(i * L, L)
        rb = pl.multiple_of(cb * L, L)
        fs_ref[i] = sf_ref[...].astype(BF16)
        bs_ref[cb] = sb_ref[...].astype(BF16)
        update(sf_ref, k_ref[0, pl.ds(rf, L), :], v_ref[0, pl.ds(rf, L), :], dec_ref[1], cdf)
        update(sb_ref, k_ref[0, pl.ds(rb, L), :], v_ref[0, pl.ds(rb, L), :], dec_ref[3], cdb)
        return carry

    def finish_states():
        fs_ref[n_chunks - 1] = sf_ref[...].astype(BF16)
        bs_ref[0] = sb_ref[...].astype(BF16)

    def out_chunk(c):
        r0 = pl.multiple_of(c * L, L)
        q = q_ref[0, pl.ds(r0, L), :]
        k = k_ref[0, pl.ds(r0, L), :]
        v = v_ref[0, pl.ds(r0, L), :]
        s = lax.dot_general(q, k, NT_DIMS, preferred_element_type=F32)
        att = (s * d_ref[...]).astype(BF16)
        o = jnp.dot(att, v, preferred_element_type=F32)
        o = o + jnp.dot(q, fs_ref[c], preferred_element_type=F32) * dec_ref[0]
        o = o + jnp.dot(q, bs_ref[c], preferred_element_type=F32) * dec_ref[2]
        rg = rg_ref[0, pl.ds(r0, L), :].astype(F32)
        o_ref[0, pl.ds(r0, L), :] = (_layer_norm(o) * _silu(rg)).astype(BF16)

    return state_pass, finish_states, out_chunk


def _ret_scratch(n_chunks):
    return [pltpu.VMEM((HEAD_W, HEAD_W), F32),
            pltpu.VMEM((HEAD_W, HEAD_W), F32),
            pltpu.VMEM((n_chunks, HEAD_W, HEAD_W), BF16),
            pltpu.VMEM((n_chunks, HEAD_W, HEAD_W), BF16),
            pltpu.VMEM((4, SCAN_L, HEAD_W), F32),
            pltpu.VMEM((SCAN_L, SCAN_L), F32)]


def _mlstm_build(qp_ref, kp_ref, v_ref, mo_ref, ckp_ref, cv_ref, gt_ref, cgt_ref,
                 wq_ref, bq_ref, wk_ref, bk_ref, o_ref,
                 tab_ref, row_ref,
                 cf_ref, mf_ref, cb_ref, mb_ref, cfs_ref, mfs_ref, cbs_ref, mbs_ref, mask_ref,
                 xf_ref, q_ref, k_ref, ck_ref):
    L = SCAN_L
    T = qp_ref.shape[1]
    Tc = ckp_ref.shape[1]
    n_chunks = T // L
    n_ctx_chunks = Tc // L
    CV = CONV_ROWS

    def conv_stage(src_ref, t_len):
        xf_ref[pl.ds(0, 8), :] = jnp.zeros((8, HEAD_W), F32)
        xf_ref[pl.ds(8 + t_len, 8), :] = jnp.zeros((8, HEAD_W), F32)
        xf_ref[pl.ds(8, t_len), :] = src_ref[0].astype(F32)

    def conv_chunk(c, w, b, dst_ref, scale):
        r0 = pl.multiple_of(c * CV, CV)
        win = xf_ref[pl.ds(r0, CV + 16), :]
        prev = pltpu.roll(win, 1, 0)[8:8 + CV, :]
        cur = win[8:8 + CV, :]
        nxt = pltpu.roll(win, CV + 15, 0)[8:8 + CV, :]
        y = _silu(prev * w[0:1, :] + cur * w[1:2, :] + nxt * w[2:3, :] + b)
        if scale != 1.0:
            y = y * scale
        dst_ref[pl.ds(r0, CV), :] = y.astype(BF16)

    def conv_all(src_ref, w_ref, b_ref, dst_ref, t_len, scale):
        conv_stage(src_ref, t_len)
        w = w_ref[...]
        b = b_ref[...]

        def body(c, carry):
            conv_chunk(c, w, b, dst_ref, scale)
            return carry

        lax.fori_loop(0, t_len // CV, body, 0)

    conv_all(kp_ref, wk_ref, bk_ref, k_ref, T, KEY_SCALE)
    conv_all(ckp_ref, wk_ref, bk_ref, ck_ref, Tc, KEY_SCALE)
    conv_stage(qp_ref, T)
    q_per_chunk = L // CV

    def conv_q(c):
        for u in range(q_per_chunk):
            conv_chunk(c * q_per_chunk + u, wq_ref[...], bq_ref[...], q_ref, 1.0)

    ri = lax.broadcasted_iota(I32, (L, L), 0)
    ci = lax.broadcasted_iota(I32, (L, L), 1)
    tri_u = (ri <= ci).astype(BF16)
    lane8 = lax.broadcasted_iota(I32, (8, L), 1)
    sub8 = lax.broadcasted_iota(I32, (8, L), 0)
    sel_r = lax.broadcasted_iota(I32, (24, 8 * 128), 0) % 8
    sel_c = lax.broadcasted_iota(I32, (24, 8 * 128), 1) // 128
    sel3 = (sel_r == sel_c).astype(BF16)
    ones_cols = jnp.ones((L, AUG_W - HEAD_W), BF16)

    def chunk_tables(g8, n_used, state_only):
        i_f, i_b = g8[0], g8[2]
        lf_f, lf_b = _log_sigmoid(g8[1]), _log_sigmoid(g8[3])
        cs3 = jnp.dot(_split3(jnp.concatenate([lf_f, lf_b], axis=0)), tri_u,
                      preferred_element_type=F32)
        cs = cs3[0:16] + cs3[16:32] + cs3[32:48]
        b_f = cs[0:8]
        b_b = cs[8:16, L - 1:L] - cs[8:16] + lf_b
        z_f = i_f - b_f
        z_b = i_b - b_b
        g_f = b_f[:, L - 1:L] - b_f + i_f
        g_b = b_b[:, 0:1] - b_b + i_b
        mf, mb = z_f, z_b
        s = 1
        while s < L:
            mf = jnp.maximum(mf, jnp.where(lane8 >= s, pltpu.roll(mf, s, 1), NEG_INF))
            mb = jnp.maximum(mb, jnp.where(lane8 < L - s, pltpu.roll(mb, L - s, 1), NEG_INF))
            s *= 2
        mb = jnp.where(lane8 < L - 1, pltpu.roll(mb, L - 1, 1), NEG_INF)
        reps = [None if state_only and t not in (2, 5) else
                lax.dot_general(_split3(val), sel3[:, 0:n_used * 128], TN_DIMS, preferred_element_type=F32)
                for t, val in enumerate((mf, b_f, g_f, mb, b_b, g_b))]

        def rows_of(c):
            out = jnp.zeros((8, L), F32)
            for r, val in enumerate((z_f, z_b, g_f, g_b, b_f, b_b)):
                out = jnp.where(sub8 == r, val[c:c + 1], out)
            return out

        return rows_of, reps

    lat_rows, lat_reps = chunk_tables(gt_ref[0, 0], n_chunks, False)
    for c in range(n_chunks):
        row_ref[c] = lat_rows(c)
        for t in range(N_TAB):
            tab_ref[t, c * L:(c + 1) * L, :] = lat_reps[t][:, c * 128:(c + 1) * 128]

    def lanes2(x):
        return jnp.concatenate([x, x], axis=1)

    def advance(k, v, g_rep, g_row, b_last, c_ref, m_ref):
        m = m_ref[...]
        m_new = jnp.maximum(b_last + m, jnp.max(g_row, axis=-1, keepdims=True))
        kw = (k.astype(F32) * jnp.exp(lanes2(g_rep) - m_new)).astype(BF16)
        v_aug = jnp.concatenate([v, ones_cols], axis=1)
        c_ref[...] = jnp.exp(b_last + m - m_new) * c_ref[...] + lax.dot_general(
            kw, v_aug, TN_DIMS, preferred_element_type=F32)
        m_ref[...] = m_new

    for r in (cf_ref, mf_ref, cb_ref, mb_ref):
        r[...] = jnp.zeros_like(r)
    ctx_rows, ctx_reps = chunk_tables(cgt_ref[0, 0], n_ctx_chunks, True)
    for c in range(n_ctx_chunks):
        rows = ctx_rows(c)
        advance(ck_ref[c * L:(c + 1) * L, :], cv_ref[0, c * L:(c + 1) * L, :],
                ctx_reps[2][:, c * 128:(c + 1) * 128], rows[2:3], rows[4:5, L - 1:L], cf_ref, mf_ref)
    for c in reversed(range(n_ctx_chunks)):
        rows = ctx_rows(c)
        advance(ck_ref[c * L:(c + 1) * L, :], cv_ref[0, c * L:(c + 1) * L, :],
                ctx_reps[5][:, c * 128:(c + 1) * 128], rows[3:4], rows[5:6, 0:1], cb_ref, mb_ref)

    def state_pass(i, carry):
        cb = n_chunks - 1 - i
        rf = pl.multiple_of(i * L, L)
        rb = pl.multiple_of(cb * L, L)
        cfs_ref[i] = cf_ref[...].astype(BF16)
        mfs_ref[i] = mf_ref[...]
        cbs_ref[cb] = cb_ref[...].astype(BF16)
        mbs_ref[cb] = mb_ref[...]
        rows_f = row_ref[i]
        rows_b = row_ref[cb]
        advance(k_ref[pl.ds(rf, L), :], v_ref[0, pl.ds(rf, L), :], tab_ref[2, pl.ds(rf, L), :],
                rows_f[2:3], rows_f[4:5, L - 1:L], cf_ref, mf_ref)
        advance(k_ref[pl.ds(rb, L), :], v_ref[0, pl.ds(rb, L), :], tab_ref[5, pl.ds(rb, L), :],
                rows_b[3:4], rows_b[5:6, 0:1], cb_ref, mb_ref)
        conv_q(i)
        return carry

    def finish_states():
        conv_q(n_chunks - 1)
        cfs_ref[n_chunks - 1] = cf_ref[...].astype(BF16)
        mfs_ref[n_chunks - 1] = mf_ref[...]
        cbs_ref[0] = cb_ref[...].astype(BF16)
        mbs_ref[0] = mb_ref[...]

    def direction(q, v_aug, s, z_row, zmax_rep, b_rep, mask, c_in, m_in):
        mx = jnp.maximum(zmax_rep, m_in)
        att = s * jnp.exp((z_row - lanes2(mx)) + mask)
        na = jnp.dot(att.astype(BF16), v_aug, preferred_element_type=F32)
        qa = jnp.dot(q, c_in, preferred_element_type=F32)
        a = jnp.exp(m_in - mx)
        num = na[:, 0:HEAD_W] + lanes2(a) * qa[:, 0:HEAD_W]
        den = na[:, HEAD_W:] + a * qa[:, HEAD_W:]
        scale = 1.0 / jnp.maximum(jnp.abs(den), jnp.exp(-(b_rep + mx)))
        return num * lanes2(scale)

    mask_ref[0] = jnp.where(ci <= ri, 0.0, NEG_INF)
    mask_ref[1] = jnp.where(ci > ri, 0.0, NEG_INF)

    def out_chunk(c):
        r0 = pl.multiple_of(c * L, L)
        q = q_ref[pl.ds(r0, L), :]
        k = k_ref[pl.ds(r0, L), :]
        v_aug = jnp.concatenate([v_ref[0, pl.ds(r0, L), :], ones_cols], axis=1)
        s = lax.dot_general(q, k, NT_DIMS, preferred_element_type=F32)
        rows = row_ref[c]
        tot = direction(q, v_aug, s, rows[0:1], tab_ref[0, pl.ds(r0, L), :], tab_ref[1, pl.ds(r0, L), :],
                        mask_ref[0], cfs_ref[c], mfs_ref[c])
        tot = tot + direction(q, v_aug, s, rows[1:2], tab_ref[3, pl.ds(r0, L), :],
                              tab_ref[4, pl.ds(r0, L), :], mask_ref[1], cbs_ref[c], mbs_ref[c])
        mo = mo_ref[0, pl.ds(r0, L), :].astype(F32)
        o_ref[0, pl.ds(r0, L), :] = (_layer_norm(tot) * jax.nn.sigmoid(mo)).astype(BF16)

    return state_pass, finish_states, out_chunk


def _mlstm_scratch(T, Tc, n_chunks):
    state = [pltpu.VMEM((HEAD_W, AUG_W), F32), pltpu.VMEM((1, 1), F32)]
    snaps = [pltpu.VMEM((n_chunks, HEAD_W, AUG_W), BF16), pltpu.VMEM((n_chunks, 1, 1), F32)]
    return [pltpu.VMEM((N_TAB, T, 128), F32), pltpu.VMEM((n_chunks, 8, SCAN_L), F32)] \
        + state + state + snaps + snaps + [pltpu.VMEM((2, SCAN_L, SCAN_L), F32)] \
        + [pltpu.VMEM((T + 16, HEAD_W), F32), pltpu.VMEM((T, HEAD_W), BF16),
           pltpu.VMEM((T, HEAD_W), BF16), pltpu.VMEM((Tc, HEAD_W), BF16)]


def _scan_kernel(n_ret_scratch, dl_ref, rq_ref, rk_ref, rv_ref, rg_ref, rck_ref, rcv_ref,
                 mq_ref, mk_ref, mv_ref, mo_ref, mck_ref, mcv_ref, gt_ref, cgt_ref,
                 wq_ref, bq_ref, wk_ref, bk_ref, r_ref, m_ref, *scratch):
    n_chunks = rq_ref.shape[1] // SCAN_L
    ret = _ret_build(dl_ref, rq_ref, rk_ref, rv_ref, rg_ref, rck_ref, rcv_ref, r_ref,
                     *scratch[:n_ret_scratch])
    mls = _mlstm_build(mq_ref, mk_ref, mv_ref, mo_ref, mck_ref, mcv_ref, gt_ref, cgt_ref,
                       wq_ref, bq_ref, wk_ref, bk_ref, m_ref, *scratch[n_ret_scratch:])

    def state_pass(i, carry):
        ret[0](i, carry)
        mls[0](i, carry)
        return carry

    lax.fori_loop(0, n_chunks - 1, state_pass, 0)
    ret[1]()
    mls[1]()

    def out_pass(i, carry):
        for c in (2 * i, 2 * i + 1):
            ret[2](c)
            mls[2](c)
        return carry

    lax.fori_loop(0, n_chunks // 2, out_pass, 0)


def _scans(decay_logit, p_lat, p_ctx, gt, cgt, conv_w, conv_b, ret_lat, ret_ctx, ml_lat, ml_ctx):
    B, T, _ = p_lat.shape
    Tc = p_ctx.shape[1]
    assert T % (2 * SCAN_L) == 0 and Tc % SCAN_L == 0 and T // SCAN_L <= 8
    n_chunks = T // SCAN_L

    def lat(sec):
        return pl.BlockSpec((1, T, HEAD_W), lambda b, h: (b, 0, sec * HEADS + h))

    def cx(sec):
        return pl.BlockSpec((1, Tc, HEAD_W), lambda b, h: (b, 0, sec * HEADS + h))

    gates = pl.BlockSpec((1, 1, N_GK, 8, SCAN_L), lambda b, h: (b, h, 0, 0, 0))
    out = pl.BlockSpec((1, T, HEAD_W), lambda b, h: (b, 0, h))
    conv_specs = [pl.BlockSpec((3, HEAD_W), lambda b, h: (0, h)),
                  pl.BlockSpec((1, HEAD_W), lambda b, h: (0, h)),
                  pl.BlockSpec((3, HEAD_W), lambda b, h: (0, HEADS + h)),
                  pl.BlockSpec((1, HEAD_W), lambda b, h: (0, HEADS + h))]
    ret_scratch = _ret_scratch(n_chunks)
    return pl.pallas_call(
        functools.partial(_scan_kernel, len(ret_scratch)),
        out_shape=(jax.ShapeDtypeStruct((B, T, BRANCH_W), BF16),
                   jax.ShapeDtypeStruct((B, T, BRANCH_W), BF16)),
        grid=(B, HEADS),
        in_specs=[pl.BlockSpec(memory_space=pltpu.SMEM)]
        + [lat(s) for s in ret_lat] + [cx(s) for s in ret_ctx]
        + [lat(s) for s in ml_lat] + [cx(s) for s in ml_ctx] + [gates, gates] + conv_specs,
        out_specs=(out, out),
        scratch_shapes=ret_scratch + _mlstm_scratch(T, Tc, n_chunks),
        compiler_params=_cparams("parallel", "parallel"),
        name="scans",
    )(decay_logit, *([p_lat] * 4), *([p_ctx] * 2), *([p_lat] * 4), *([p_ctx] * 2), gt, cgt,
      conv_w, conv_b, conv_w, conv_b)


def _merge_kernel(alpha, r_ref, m_ref, gr_ref, gm_ref, x_ref, g1_ref, sh2_ref, sc2_ref,
                  lng_ref, lnb_ref, wr_ref, wm_ref, wo_ref, wrt_ref, brt_ref,
                  x1_ref, ua_ref, ub_ref, lt_ref):
    yr = jnp.dot(r_ref[0], wr_ref[...], preferred_element_type=F32)
    ym = jnp.dot(m_ref[0], wm_ref[...], preferred_element_type=F32)
    y = jax.nn.sigmoid(gr_ref[0].astype(F32)) * yr + jax.nn.sigmoid(gm_ref[0].astype(F32)) * ym
    yo = jnp.dot(y.astype(BF16), wo_ref[...], preferred_element_type=F32)
    x1 = _layer_norm(alpha * x_ref[0] + g1_ref[0] * yo) * lng_ref[...] + lnb_ref[...]
    x1_ref[0] = x1
    u2 = _layer_norm(x1) * (1.0 + sc2_ref[0]) + sh2_ref[0]
    lt_ref[0] = lax.dot_general(wrt_ref[...], u2.astype(BF16), NT_DIMS,
                                preferred_element_type=F32) + brt_ref[...]
    ua_ref[0] = _pack_pairs(u2[:, 0:PACK_W], u2[:, PACK_W:2 * PACK_W])
    ub_ref[0] = _pack_pairs(u2[:, 2 * PACK_W:3 * PACK_W], u2[:, 3 * PACK_W:4 * PACK_W])


def _merge(alpha, r, m, p_lat, sec_gates, x, g1, sh2, sc2, lng, lnb, wr, wm, wo, wrt, brt):
    B, T, D = x.shape
    tm = MERGE_TM

    def tile(w):
        return pl.BlockSpec((1, tm, w), lambda b, i: (b, i, 0))

    def sec(s):
        return pl.BlockSpec((1, tm, BRANCH_W), lambda b, i: (b, i, s))

    def mod():
        return pl.BlockSpec((1, 1, D), lambda b, i: (b, 0, 0))

    def const(shape):
        return pl.BlockSpec(shape, lambda b, i: (0,) * len(shape))

    return pl.pallas_call(
        functools.partial(_merge_kernel, alpha),
        out_shape=(jax.ShapeDtypeStruct((B, T, D), F32),
                   jax.ShapeDtypeStruct((B, T, PACK_W), U32),
                   jax.ShapeDtypeStruct((B, T, PACK_W), U32),
                   jax.ShapeDtypeStruct((B, ROUTE_ROWS, T), F32)),
        grid=(B, T // tm),
        in_specs=[tile(BRANCH_W), tile(BRANCH_W), sec(sec_gates[0]), sec(sec_gates[1]), tile(D),
                  mod(), mod(), mod(), const((1, D)), const((1, D)),
                  const((BRANCH_W, D)), const((BRANCH_W, D)), const((D, D)),
                  const((ROUTE_ROWS, D)), const((ROUTE_ROWS, 1))],
        out_specs=(tile(D), tile(PACK_W), tile(PACK_W),
                   pl.BlockSpec((1, ROUTE_ROWS, tm), lambda b, i: (b, 0, i))),
        compiler_params=_cparams("parallel", "parallel"),
        name="merge",
    )(r, m, p_lat, p_lat, x, g1, sh2, sc2, lng, lnb, wr, wm, wo, wrt, brt)


def _route_kernel(lt_ref, ri_ref, rw_ref, cnt_ref, carry_ref, u_ref):
    i = pl.program_id(0)
    tm = lt_ref.shape[2]

    @pl.when(i == 0)
    def _():
        carry_ref[...] = jnp.zeros_like(carry_ref)
        r = lax.broadcasted_iota(I32, (tm, tm), 0)
        c = lax.broadcasted_iota(I32, (tm, tm), 1)
        u_ref[...] = (r < c).astype(BF16)

    lt = lt_ref[0]
    lg = lt[0:N_GROUPS, :]
    eg = jnp.exp(lg - jnp.max(lg, axis=0, keepdims=True))
    pg = eg / jnp.sum(eg, axis=0, keepdims=True)
    pg_top = jnp.max(pg, axis=0, keepdims=True)
    rows_g = lax.broadcasted_iota(I32, pg.shape, 0)
    g_idx = jnp.min(jnp.where(pg == pg_top, rows_g, N_GROUPS), axis=0, keepdims=True)

    le = jnp.zeros((EXP_PER_GROUP, tm), F32)
    for g in range(N_GROUPS):
        lo = 8 + g * EXP_PER_GROUP
        le = jnp.where(g_idx == g, lt[lo:lo + EXP_PER_GROUP, :], le)
    ee = jnp.exp(le - jnp.max(le, axis=0, keepdims=True))
    pe = ee / jnp.sum(ee, axis=0, keepdims=True)
    rows_e = lax.broadcasted_iota(I32, pe.shape, 0)
    v1 = jnp.max(pe, axis=0, keepdims=True)
    i1 = jnp.min(jnp.where(pe == v1, rows_e, EXP_PER_GROUP), axis=0, keepdims=True)
    pe2 = jnp.where(rows_e == i1, -1.0, pe)
    v2 = jnp.max(pe2, axis=0, keepdims=True)
    i2 = jnp.min(jnp.where(pe2 == v2, rows_e, EXP_PER_GROUP), axis=0, keepdims=True)
    den = v1 + v2
    rw_ref[...] = jnp.zeros_like(rw_ref)
    rw_ref[0:1, :] = pg_top * v1 / den
    rw_ref[1:2, :] = pg_top * v2 / den
    e1 = g_idx * EXP_PER_GROUP + i1
    e2 = g_idx * EXP_PER_GROUP + i2

    rows_x = lax.broadcasted_iota(I32, (N_EXPERTS, tm), 0)
    oh1 = (rows_x == e1).astype(F32)
    oh2 = (rows_x == e2).astype(F32)
    both = oh1 + oh2
    before = carry_ref[:, 0:1] + jnp.dot(both.astype(BF16), u_ref[...], preferred_element_type=F32)
    ri_ref[0:1, :] = e1
    ri_ref[1:2, :] = e2
    ri_ref[2:3, :] = jnp.sum(oh1 * before, axis=0, keepdims=True).astype(I32)
    ri_ref[3:4, :] = jnp.sum(oh2 * before, axis=0, keepdims=True).astype(I32)
    carry_ref[...] = carry_ref[...] + jnp.sum(both, axis=1, keepdims=True)
    cnt_ref[...] = carry_ref[...].astype(I32)


def _route(lt):
    B, _, T = lt.shape
    tm = ROUTE_TM
    per_b = T // tm
    n = B * T
    return pl.pallas_call(
        _route_kernel,
        out_shape=(jax.ShapeDtypeStruct((4, n), I32),
                   jax.ShapeDtypeStruct((8, n), F32),
                   jax.ShapeDtypeStruct((N_EXPERTS, 128), I32)),
        grid=(n // tm,),
        in_specs=[pl.BlockSpec((1, ROUTE_ROWS, tm), lambda i: (i // per_b, 0, i % per_b))],
        out_specs=(pl.BlockSpec((4, tm), lambda i: (0, i)),
                   pl.BlockSpec((8, tm), lambda i: (0, i)),
                   pl.BlockSpec((N_EXPERTS, 128), lambda i: (0, 0))),
        scratch_shapes=[pltpu.VMEM((N_EXPERTS, 128), F32), pltpu.VMEM((tm, tm), BF16)],
        compiler_params=_cparams("arbitrary"),
        name="route",
    )(lt)


def _sc_mesh():
    return plsc.VectorSubcoreMesh(core_axis_name="c", subcore_axis_name="s")


def _sc_scatter2(rows, idx0, idx1, n_out):
    m, w = rows.shape

    @functools.partial(pl.kernel, out_type=jax.ShapeDtypeStruct((n_out, w), rows.dtype),
                       mesh=_sc_mesh(), scratch_types=[])
    def k(x_hbm, i0_hbm, i1_hbm, o_hbm):
        def body(x_vmem, i0_vmem, i1_vmem):
            pltpu.sync_copy(x_vmem, o_hbm.at[i0_vmem.at[0]])
            pltpu.sync_copy(x_vmem, o_hbm.at[i1_vmem.at[0]])

        pltpu.emit_pipeline(
            body,
            grid=(m // SC_WIN,),
            in_specs=[pl.BlockSpec((SC_WIN, w), lambda i: (i, 0)),
                      pl.BlockSpec((1, SC_WIN), lambda i: (0, i)),
                      pl.BlockSpec((1, SC_WIN), lambda i: (0, i))],
            out_specs=[],
            core_axis_name=("c", "s"),
            dimension_semantics=(pltpu.PARALLEL,),
        )(x_hbm, i0_hbm, i1_hbm)

    return k(rows, idx0.reshape(1, m), idx1.reshape(1, m))


def _sc_gather(table, idx):
    m = idx.shape[0]
    w = table.shape[1]

    @functools.partial(pl.kernel, out_type=jax.ShapeDtypeStruct((m, w), table.dtype),
                       mesh=_sc_mesh(), scratch_types=[])
    def k(t_hbm, i_hbm, o_hbm):
        def body(i_vmem, o_vmem):
            pltpu.sync_copy(t_hbm.at[i_vmem.at[0]], o_vmem)

        pltpu.emit_pipeline(
            body,
            grid=(m // SC_WIN,),
            in_specs=[pl.BlockSpec((1, SC_WIN), lambda i: (0, i))],
            out_specs=[pl.BlockSpec((SC_WIN, w), lambda i: (i, 0))],
            core_axis_name=("c", "s"),
            dimension_semantics=(pltpu.PARALLEL,),
        )(i_hbm, o_hbm)

    return k(table, idx.reshape(1, m))


def _expert_kernel(be_ref, nv_ref, xa_ref, xb_ref, w1f_ref, w3f_ref, w2f_ref, ya_ref, yb_ref,
                   w1_ref, w3_ref, w2_ref):
    j = pl.program_id(0)
    nv = nv_ref[j]

    @pl.when(jnp.logical_or(j == 0, be_ref[j] != be_ref[jnp.maximum(j - 1, 0)]))
    def _():
        w1_ref[0] = w1f_ref[0].astype(BF16)
        w3_ref[0] = w3f_ref[0].astype(BF16)
        w2_ref[0] = w2f_ref[0].astype(BF16)

    @pl.when(nv > 0)
    def _():
        valid = lax.broadcasted_iota(I32, xa_ref.shape, 0) < nv
        zero = jnp.zeros(xa_ref.shape, U32)
        parts = _unpack_pairs(jnp.where(valid, xa_ref[...], zero)) + \
            _unpack_pairs(jnp.where(valid, xb_ref[...], zero))
        x = jnp.concatenate([p.astype(BF16) for p in parts], axis=1)
        h1 = jnp.dot(x, w1_ref[0], preferred_element_type=F32)
        h3 = jnp.dot(x, w3_ref[0], preferred_element_type=F32)
        y = jnp.dot((_silu(h1) * h3).astype(BF16), w2_ref[0], preferred_element_type=F32)
        ya_ref[...] = _pack_pairs(y[:, 0:PACK_W], y[:, PACK_W:2 * PACK_W])
        yb_ref[...] = _pack_pairs(y[:, 2 * PACK_W:3 * PACK_W], y[:, 3 * PACK_W:4 * PACK_W])

    @pl.when(nv == 0)
    def _():
        ya_ref[...] = jnp.zeros_like(ya_ref)
        yb_ref[...] = jnp.zeros_like(yb_ref)


def _experts(block_exp, n_valid, xa, xb, w1, w3, w2):
    n_slots = xa.shape[0]
    n_blocks = n_slots // MOE_BLK
    d, de = w1.shape[1], w1.shape[2]
    slot = pl.BlockSpec((MOE_BLK, PACK_W), lambda j, be, nv: (j, 0))
    grid_spec = pltpu.PrefetchScalarGridSpec(
        num_scalar_prefetch=2,
        grid=(n_blocks,),
        in_specs=[slot, slot,
                  pl.BlockSpec((1, d, de), lambda j, be, nv: (be[j], 0, 0)),
                  pl.BlockSpec((1, d, de), lambda j, be, nv: (be[j], 0, 0)),
                  pl.BlockSpec((1, de, d), lambda j, be, nv: (be[j], 0, 0))],
        out_specs=(slot, slot),
        scratch_shapes=[pltpu.VMEM((1, d, de), BF16), pltpu.VMEM((1, d, de), BF16),
                        pltpu.VMEM((1, de, d), BF16)],
    )
    return pl.pallas_call(
        _expert_kernel,
        out_shape=(jax.ShapeDtypeStruct((n_slots, PACK_W), U32),
                   jax.ShapeDtypeStruct((n_slots, PACK_W), U32)),
        grid_spec=grid_spec,
        compiler_params=_cparams("arbitrary"),
        name="experts",
    )(block_exp, n_valid, xa, xb, w1, w3, w2)


def _final_kernel(alpha, x1_ref, a0_ref, b0_ref, a1_ref, b1_ref, w_ref, g2_ref, lng_ref, lnb_ref, o_ref):
    w = w_ref[...].T
    w0 = w[:, 0:1]
    w1 = w[:, 1:2]
    parts0 =_unpack_pairs(a0_ref[...]) + _unpack_pairs(b0_ref[...])
    parts1 = _unpack_pairs(a1_ref[...]) + _unpack_pairs(b1_ref[...])
    f = jnp.concatenate([w0 * p0 + w1 * p1 for p0, p1 in zip(parts0, parts1)], axis=1)
    o_ref[0] = _layer_norm(alpha * x1_ref[0] + g2_ref[0] * f) * lng_ref[...] + lnb_ref[...]


def _final(alpha, x1, ya, yb, w, g2, lng, lnb):
    B, T, D = x1.shape
    tm = MERGE_TM
    per_b = T // tm
    n_tiles = B * per_b

    def rows(k):
        return pl.BlockSpec((tm, PACK_W), lambda b, i: (k * n_tiles + b * per_b + i, 0))

    return pl.pallas_call(
        functools.partial(_final_kernel, alpha),
        out_shape=jax.ShapeDtypeStruct((B, T, D), F32),
        grid=(B, per_b),
        in_specs=[pl.BlockSpec((1, tm, D), lambda b, i: (b, i, 0)),
                  rows(0), rows(0), rows(1), rows(1),
                  pl.BlockSpec((8, tm), lambda b, i: (0, b * per_b + i)),
                  pl.BlockSpec((1, 1, D), lambda b, i: (b, 0, 0)),
                  pl.BlockSpec((1, D), lambda b, i: (0, 0)),
                  pl.BlockSpec((1, D), lambda b, i: (0, 0))],
        out_specs=pl.BlockSpec((1, tm, D), lambda b, i: (b, i, 0)),
        compiler_params=_cparams("parallel", "parallel"),
        name="final",
    )(x1, ya, yb, ya, yb, w, g2, lng, lnb)


def _rotary_tables(T):
    quarter = HEAD_W // 4
    freqs = ROPE_BASE ** (-jnp.arange(quarter, dtype=F32) / quarter)
    t = jnp.arange(T)
    ang_r = (t // GRID_W).astype(F32)[:, None] * freqs[None, :]
    ang_c = (t % GRID_W).astype(F32)[:, None] * freqs[None, :]
    cos = jnp.concatenate([jnp.cos(ang_r)] * 2 + [jnp.cos(ang_c)] * 2, axis=1)
    sin = jnp.concatenate([-jnp.sin(ang_r), jnp.sin(ang_r), -jnp.sin(ang_c), jnp.sin(ang_c)], axis=1)
    return cos, sin


def _per_head_gates(gt):
    B, _, T = gt.shape
    n_chunks = T // SCAN_L
    gth = gt.reshape(B, N_GK, HEADS, n_chunks, SCAN_L).transpose(0, 2, 1, 3, 4)
    return jnp.pad(gth, ((0, 0), (0, 0), (0, 0), (0, 8 - n_chunks), (0, 0)))


def _table_lookup(table, idx):
    sel = idx[..., None] == jnp.arange(table.shape[0], dtype=idx.dtype)
    return jnp.sum(jnp.where(sel, table, 0), axis=-1)


def kernel(x, c, ctx, c_ctx, w_ada, b_ada, w_in, b_mgate, ml_conv_w, ml_conv_b, ret_decay_logit, w_ret_branch, w_ml_branch, w_out, ln1_g, ln1_b, w_rg, b_rg, w_re, b_re, w_e1, w_e3, w_e2, ln2_g, ln2_b):
    B, T, D = x.shape
    depth = w_ada.shape[0]
    assert depth == 1 and D == BRANCH_W and T % GRID_W == 0
    alpha = (2 * depth) ** 0.25
    n_tok = B * T

    n_rows = -(-(B + 1) // 8) * 8
    cs = jnp.zeros((n_rows, D), F32).at[:B].set(c).at[B].set(c_ctx)
    mod = _ada(cs, w_ada[0], b_ada[0][None, :])
    sh1, sc1, g1, sh2, sc2, g2 = [mod[:B, None, i * D:(i + 1) * D] for i in range(6)]
    csh1 = mod[B, 0 * D:1 * D].reshape(1, 1, D)
    csc1 = mod[B, 1 * D:2 * D].reshape(1, 1, D)

    w = w_in[0]
    sec_w = [w[:, s * BRANCH_W:(s + 1) * BRANCH_W] for s in range(8)]
    g_lo = 8 * BRANCH_W
    w_gate_t = w[:, g_lo:g_lo + N_GATES].T.astype(BF16)
    b_gate = b_mgate[0][:, None]
    sec_w += [w[:, g_lo + N_GATES:g_lo + N_GATES + D], w[:, g_lo + N_GATES + D:]]
    w_lat = jnp.concatenate(sec_w, axis=1).astype(BF16)
    w_ctx = jnp.concatenate([sec_w[1], sec_w[2], sec_w[5], sec_w[6]], axis=1).astype(BF16)
    kinds_lat = ("rot", "rot_scale") + ("plain",) * 8
    kinds_ctx = ("scale", "plain", "plain", "plain")
    p_lat, gt_lat = _proj(x, sh1, sc1, w_lat, w_gate_t, b_gate, kinds_lat, _rotary_tables(T))
    Tc = ctx.shape[1]
    p_ctx, gt_ctx = _proj(ctx.reshape(1, B * Tc, D), csh1, csc1, w_ctx, w_gate_t, b_gate, kinds_ctx)
    p_ctx = p_ctx.reshape(B, Tc, -1)
    gt_ctx = gt_ctx.reshape(N_GATES, B, Tc).transpose(1, 0, 2)

    ret, mls = _scans(ret_decay_logit[0], p_lat, p_ctx, _per_head_gates(gt_lat), _per_head_gates(gt_ctx),
                      ml_conv_w[0], ml_conv_b[0][None, :], (0, 1, 2, 3), (0, 1), (4, 5, 6, 7), (2, 3))

    wrt = jnp.zeros((ROUTE_ROWS, D), F32).at[:N_GROUPS].set(w_rg[0].T).at[8:8 + N_EXPERTS].set(w_re[0].T)
    brt = jnp.zeros((ROUTE_ROWS, 1), F32).at[:N_GROUPS, 0].set(b_rg[0]).at[8:8 + N_EXPERTS, 0].set(b_re[0])
    x1, ua, ub, lt = _merge(alpha, ret, mls, p_lat, (8, 9), x, g1, sh2, sc2,
                            ln1_g[0][None, :], ln1_b[0][None, :],
                            w_ret_branch[0].astype(BF16), w_ml_branch[0].astype(BF16),
                            w_out[0].astype(BF16), wrt.astype(BF16), brt)

    ri, rw, cnt = _route(lt)

    counts = cnt[:, 0]
    padded = (counts + MOE_BLK - 1) // MOE_BLK * MOE_BLK
    pad_end = jnp.cumsum(padded)
    pad_off = pad_end - padded
    dest = _table_lookup(pad_off, ri[0:2]) + ri[2:4]
    n_blocks = (2 * n_tok) // MOE_BLK + N_EXPERTS
    n_slots = n_blocks * MOE_BLK
    block_start = jnp.arange(n_blocks, dtype=I32) * MOE_BLK
    block_exp = jnp.minimum((block_start[:, None] >= pad_end[None, :]).sum(1), N_EXPERTS - 1).astype(I32)
    n_valid = jnp.clip(_table_lookup(counts, block_exp) - (block_start - _table_lookup(pad_off, block_exp)),
                       0, MOE_BLK).astype(I32)

    xa = _sc_scatter2(ua.reshape(n_tok, PACK_W), dest[0], dest[1], n_slots)
    xb = _sc_scatter2(ub.reshape(n_tok, PACK_W), dest[0], dest[1], n_slots)
    ya, yb = _experts(block_exp, n_valid, xa, xb, w_e1[0], w_e3[0], w_e2[0])
    dflat = dest.reshape(2 * n_tok)
    ga = _sc_gather(ya, dflat)
    gb = _sc_gather(yb, dflat)
    return _final(alpha, x1, ga, gb, rw, g2, ln2_g[0][None, :], ln2_b[0][None, :])
```

```python
import functools

import jax
import jax.numpy as jnp
from jax import lax
from jax.experimental import pallas as pl
from jax.experimental.pallas import tpu as pltpu
from jax.experimental.pallas import tpu_sc as plsc

F32 = jnp.float32
BF16 = jnp.bfloat16
U32 = jnp.uint32
I32 = jnp.int32
HIGHEST = lax.Precision.HIGHEST

HEADS = 4
HEAD_W = 256
BRANCH_W = HEADS * HEAD_W
GRID_W = 64
ROPE_BASE = 10000.0
N_GATES = 16
N_GK = N_GATES // HEADS
N_GROUPS = 4
EXP_PER_GROUP = 8
N_EXPERTS = N_GROUPS * EXP_PER_GROUP
LN_EPS = 1e-5
NEG_INF = -1e30
KEY_SCALE = HEAD_W ** -0.5

SCAN_L = 256
CONV_ROWS = 128
PROJ_TM = 2048
PROJ_SUB = 256
MERGE_TM = 512
ROUTE_TM = 512
MOE_BLK = 512
COMBINE_SPLIT = 2
SC_WIN = 128
PACK_W = 256
ROUTE_ROWS = 64
N_TAB = 6
AUG_W = HEAD_W + 128
VMEM_LIMIT = 48 * 1024 * 1024

NT_DIMS = (((1,), (1,)), ((), ()))
TN_DIMS = (((0,), (0,)), ((), ()))


def _cparams(*sem):
    return pltpu.CompilerParams(dimension_semantics=sem, vmem_limit_bytes=VMEM_LIMIT)


def _layer_norm(x):
    mu = jnp.mean(x, axis=-1, keepdims=True)
    xc = x - mu
    var = jnp.mean(xc * xc, axis=-1, keepdims=True)
    return xc * lax.rsqrt(var + LN_EPS)


def _log_sigmoid(x):
    return jnp.minimum(x, 0.0) - jnp.log1p(jnp.exp(-jnp.abs(x)))


def _silu(x):
    return x * jax.nn.sigmoid(x)


def _pack_pairs(hi, lo):
    hb = lax.bitcast_convert_type(hi.astype(BF16).astype(F32), U32)
    lb = lax.bitcast_convert_type(lo.astype(BF16).astype(F32), U32)
    return (hb & jnp.uint32(0xFFFF0000)) | (lb >> 16)


def _unpack_pairs(p):
    hi = lax.bitcast_convert_type(p & jnp.uint32(0xFFFF0000), F32)
    lo = lax.bitcast_convert_type(p << 16, F32)
    return hi, lo


def _split3(x):
    hi = x.astype(BF16).astype(F32)
    r1 = x - hi
    mid = r1.astype(BF16).astype(F32)
    lo = (r1 - mid).astype(BF16).astype(F32)
    return jnp.concatenate([hi, mid, lo], axis=0).astype(BF16)


def _ada_kernel(c_ref, w_ref, b_ref, o_ref):
    s = _silu(c_ref[...])
    o_ref[...] = jnp.dot(s, w_ref[...], precision=HIGHEST, preferred_element_type=F32) + b_ref[...]


def _ada(cs, w, b):
    rows, d = cs.shape
    cols = w.shape[1]
    tn = 1024
    return pl.pallas_call(
        _ada_kernel,
        out_shape=jax.ShapeDtypeStruct((rows, cols), F32),
        grid=(cols // tn,),
        in_specs=[pl.BlockSpec((rows, d), lambda j: (0, 0)),
                  pl.BlockSpec((d, tn), lambda j: (0, j)),
                  pl.BlockSpec((1, tn), lambda j: (0, j))],
        out_specs=pl.BlockSpec((rows, tn), lambda j: (0, j)),
        compiler_params=_cparams("parallel"),
        name="ada",
    )(cs, w, b)


def _proj_kernel(kinds, x_ref, sh_ref, sc_ref, w_ref, wg_ref, bg_ref, *rest):
    if "rot" in kinds or "rot_scale" in kinds:
        cos_ref, sin_ref, o_ref, gt_ref, u_ref = rest
    else:
        o_ref, gt_ref, u_ref = rest
    j = pl.program_id(2)
    tm = x_ref.shape[1]
    sub = min(PROJ_SUB, tm)

    def rotary(acc, rows, scale):
        for s in range(acc.shape[1] // 128):
            a = acc[:, s * 128:(s + 1) * 128]
            half = s % 2
            cs = cos_ref[rows, half * 128:(half + 1) * 128]
            sn = sin_ref[rows, half * 128:(half + 1) * 128]
            r = a * cs + pltpu.roll(a, 64, 1) * sn
            if scale != 1.0:
                r = r * scale
            o_ref[0, rows, s * 128:(s + 1) * 128] = r.astype(BF16)

    def section(kind, first):
        for r in range(tm // sub):
            rows = slice(r * sub, (r + 1) * sub)
            if first:
                u = _layer_norm(x_ref[0, rows, :]) * (1.0 + sc_ref[0]) + sh_ref[0]
                ub = u.astype(BF16)
                u_ref[rows, :] = ub
                gt_ref[0, :, rows] = lax.dot_general(wg_ref[...], ub, NT_DIMS,
                                                     preferred_element_type=F32) + bg_ref[...]
            else:
                ub = u_ref[rows, :]
            acc = jnp.dot(ub, w_ref[...], preferred_element_type=F32)
            if kind == "rot":
                rotary(acc, rows, 1.0)
            elif kind == "rot_scale":
                rotary(acc, rows, KEY_SCALE)
            elif kind == "scale":
                o_ref[0, rows, :] = (acc * KEY_SCALE).astype(BF16)
            else:
                o_ref[0, rows, :] = acc.astype(BF16)

    variants = {}
    for s, kind in enumerate(kinds):
        variants.setdefault((kind, s == 0), []).append(s)
    for (kind, first), secs in variants.items():
        cond = functools.reduce(jnp.logical_or, [j == s for s in secs])

        @pl.when(cond)
        def _(kind=kind, first=first):
            section(kind, first)


def _proj(x, sh, sc, w_main, w_gate_t, b_gate, kinds, tables=None):
    B, T, D = x.shape
    n_sec = len(kinds)
    tm = min(PROJ_TM, T)
    tn = BRANCH_W
    assert T % tm == 0
    in_specs = [
        pl.BlockSpec((1, tm, D), lambda i, b, j: (b, i, 0)),
        pl.BlockSpec((1, 1, D), lambda i, b, j: (b, 0, 0)),
        pl.BlockSpec((1, 1, D), lambda i, b, j: (b, 0, 0)),
        pl.BlockSpec((D, tn), lambda i, b, j: (0, j)),
        pl.BlockSpec((N_GATES, D), lambda i, b, j: (0, 0)),
        pl.BlockSpec((N_GATES, 1), lambda i, b, j: (0, 0)),
    ]
    args = [x, sh, sc, w_main, w_gate_t, b_gate]
    if tables is not None:
        in_specs += [pl.BlockSpec((tm, HEAD_W), lambda i, b, j: (i, 0))] * 2
        args += list(tables)
    return pl.pallas_call(
        functools.partial(_proj_kernel, kinds),
        out_shape=(jax.ShapeDtypeStruct((B, T, n_sec * tn), BF16),
                   jax.ShapeDtypeStruct((B, N_GATES, T), F32)),
        grid=(T // tm, B, n_sec),
        in_specs=in_specs,
        out_specs=(pl.BlockSpec((1, tm, tn), lambda i, b, j: (b, i, j)),
                   pl.BlockSpec((1, N_GATES, tm), lambda i, b, j: (b, 0, i))),
        scratch_shapes=[pltpu.VMEM((tm, D), BF16)],
        compiler_params=_cparams("parallel", "parallel", "arbitrary"),
        name="proj_lat" if tables is not None else "proj_ctx",
    )(*args)


def _ret_build(dl_ref, q_ref, k_ref, v_ref, rg_ref, ck_ref, cv_ref, o_ref,
               sf_ref, sb_ref, fs_ref, bs_ref, dec_ref, d_ref):
    h = pl.program_id(1)
    L = SCAN_L
    n_chunks = q_ref.shape[1] // L
    n_ctx_chunks = ck_ref.shape[1] // L
    lgf = _log_sigmoid(jnp.full((1, 1), dl_ref[0, h], F32))
    lgb = _log_sigmoid(jnp.full((1, 1), dl_ref[1, h], F32))

    ri = lax.broadcasted_iota(I32, (L, L), 0)
    ci = lax.broadcasted_iota(I32, (L, L), 1)
    rel = (ri - ci).astype(F32)
    d_ref[...] = jnp.where(rel >= 0.0, jnp.exp(jnp.maximum(rel, 0.0) * lgf),
                           jnp.exp(jnp.maximum(-rel, 0.0) * lgb))
    row = lax.broadcasted_iota(I32, (L, HEAD_W), 0).astype(F32)
    dec_ref[0] = jnp.exp((row + 1.0) * lgf)
    dec_ref[1] = jnp.exp((L - 1.0 - row) * lgf)
    dec_ref[2] = jnp.exp((L - row) * lgb)
    dec_ref[3] = jnp.exp(row * lgb)
    cdf = jnp.exp(L * lgf)
    cdb = jnp.exp(L * lgb)

    def update(s_ref, kc, vc, kd, cd):
        kdec = (kc.astype(F32) * kd).astype(BF16)
        s_ref[...] = s_ref[...] * cd + lax.dot_general(kdec, vc, TN_DIMS, preferred_element_type=F32)

    sf_ref[...] = jnp.zeros_like(sf_ref)
    sb_ref[...] = jnp.zeros_like(sb_ref)
    for c in range(n_ctx_chunks):
        update(sf_ref, ck_ref[0, c * L:(c + 1) * L, :], cv_ref[0, c * L:(c + 1) * L, :], dec_ref[1], cdf)
    for c in reversed(range(n_ctx_chunks)):
        update(sb_ref, ck_ref[0, c * L:(c + 1) * L, :], cv_ref[0, c * L:(c + 1) * L, :], dec_ref[3], cdb)

    def state_pass(i, carry):
        cb = n_chunks - 1 - i
        rf = pl.multiple_of(i * L, L)
        rb = pl.multiple_of(cb * L, L)
        fs_ref[i] = sf_ref[...].astype(BF16)
        bs_ref[cb] = sb_ref[...].astype(BF16)
        update(sf_ref, k_ref[0, pl.ds(rf, L), :], v_ref[0, pl.ds(rf, L), :], dec_ref[1], cdf)
        update(sb_ref, k_ref[0, pl.ds(rb, L), :], v_ref[0, pl.ds(rb, L), :], dec_ref[3], cdb)
        return carry

    def finish_states():
        fs_ref[n_chunks - 1] = sf_ref[...].astype(BF16)
        bs_ref[0] = sb_ref[...].astype(BF16)

    def out_chunk(c):
        r0 = pl.multiple_of(c * L, L)
        q = q_ref[0, pl.ds(r0, L), :]
        k = k_ref[0, pl.ds(r0, L), :]
        v = v_ref[0, pl.ds(r0, L), :]
        s = lax.dot_general(q, k, NT_DIMS, preferred_element_type=F32)
        att = (s * d_ref[...]).astype(BF16)
        o = jnp.dot(att, v, preferred_element_type=F32)
        o = o + jnp.dot(q, fs_ref[c], preferred_element_type=F32) * dec_ref[0]
        o = o + jnp.dot(q, bs_ref[c], preferred_element_type=F32) * dec_ref[2]
        rg = rg_ref[0, pl.ds(r0, L), :].astype(F32)
        o_ref[0, pl.ds(r0, L), :] = (_layer_norm(o) * _silu(rg)).astype(BF16)

    return state_pass, finish_states, out_chunk


def _ret_scratch(n_chunks):
    return [pltpu.VMEM((HEAD_W, HEAD_W), F32),
            pltpu.VMEM((HEAD_W, HEAD_W), F32),
            pltpu.VMEM((n_chunks, HEAD_W, HEAD_W), BF16),
            pltpu.VMEM((n_chunks, HEAD_W, HEAD_W), BF16),
            pltpu.VMEM((4, SCAN_L, HEAD_W), F32),
            pltpu.VMEM((SCAN_L, SCAN_L), F32)]


def _mlstm_build(qp_ref, kp_ref, v_ref, mo_ref, ckp_ref, cv_ref, gt_ref, cgt_ref,
                 wq_ref, bq_ref, wk_ref, bk_ref, o_ref,
                 tab_ref, row_ref,
                 cf_ref, mf_ref, cb_ref, mb_ref, cfs_ref, mfs_ref, cbs_ref, mbs_ref, mask_ref,
                 xf_ref, q_ref, k_ref, ck_ref):
    L = SCAN_L
    T = qp_ref.shape[1]
    Tc = ckp_ref.shape[1]
    n_chunks = T // L
    n_ctx_chunks = Tc // L
    CV = CONV_ROWS

    def conv_stage(src_ref, t_len):
        xf_ref[pl.ds(0, 8), :] = jnp.zeros((8, HEAD_W), F32)
        xf_ref[pl.ds(8 + t_len, 8), :] = jnp.zeros((8, HEAD_W), F32)
        xf_ref[pl.ds(8, t_len), :] = src_ref[0].astype(F32)

    def conv_chunk(c, w, b, dst_ref, scale):
        r0 = pl.multiple_of(c * CV, CV)
        win = xf_ref[pl.ds(r0, CV + 16), :]
        prev = pltpu.roll(win, 1, 0)[8:8 + CV, :]
        cur = win[8:8 + CV, :]
        nxt = pltpu.roll(win, CV + 15, 0)[8:8 + CV, :]
        y = _silu(prev * w[0:1, :] + cur * w[1:2, :] + nxt * w[2:3, :] + b)
        if scale != 1.0:
            y = y * scale
        dst_ref[pl.ds(r0, CV), :] = y.astype(BF16)

    def conv_all(src_ref, w_ref, b_ref, dst_ref, t_len, scale):
        conv_stage(src_ref, t_len)
        w = w_ref[...]
        b = b_ref[...]

        def body(c, carry):
            conv_chunk(c, w, b, dst_ref, scale)
            return carry

        lax.fori_loop(0, t_len // CV, body, 0)

    conv_all(kp_ref, wk_ref, bk_ref, k_ref, T, KEY_SCALE)
    conv_all(ckp_ref, wk_ref, bk_ref, ck_ref, Tc, KEY_SCALE)
    conv_stage(qp_ref, T)
    q_per_chunk = L // CV

    def conv_q(c):
        for u in range(q_per_chunk):
            conv_chunk(c * q_per_chunk + u, wq_ref[...], bq_ref[...], q_ref, 1.0)

    ri = lax.broadcasted_iota(I32, (L, L), 0)
    ci = lax.broadcasted_iota(I32, (L, L), 1)
    tri_u = (ri <= ci).astype(BF16)
    lane8 = lax.broadcasted_iota(I32, (8, L), 1)
    sub8 = lax.broadcasted_iota(I32, (8, L), 0)
    sel_r = lax.broadcasted_iota(I32, (24, 8 * 128), 0) % 8
    sel_c = lax.broadcasted_iota(I32, (24, 8 * 128), 1) // 128
    sel3 = (sel_r == sel_c).astype(BF16)
    ones_cols = jnp.ones((L, AUG_W - HEAD_W), BF16)

    def chunk_tables(g8, n_used, state_only):
        i_f, i_b = g8[0], g8[2]
        lf_f, lf_b = _log_sigmoid(g8[1]), _log_sigmoid(g8[3])
        cs3 = jnp.dot(_split3(jnp.concatenate([lf_f, lf_b], axis=0)), tri_u,
                      preferred_element_type=F32)
        cs = cs3[0:16] + cs3[16:32] + cs3[32:48]
        b_f = cs[0:8]
        b_b = cs[8:16, L - 1:L] - cs[8:16] + lf_b
        z_f = i_f - b_f
        z_b = i_b - b_b
        g_f = b_f[:, L - 1:L] - b_f + i_f
        g_b = b_b[:, 0:1] - b_b + i_b
        mf, mb = z_f, z_b
        s = 1
        while s < L:
            mf = jnp.maximum(mf, jnp.where(lane8 >= s, pltpu.roll(mf, s, 1), NEG_INF))
            mb = jnp.maximum(mb, jnp.where(lane8 < L - s, pltpu.roll(mb, L - s, 1), NEG_INF))
            s *= 2
        mb = jnp.where(lane8 < L - 1, pltpu.roll(mb, L - 1, 1), NEG_INF)
        reps = [None if state_only and t not in (2, 5) else
                lax.dot_general(_split3(val), sel3[:, 0:n_used * 128], TN_DIMS, preferred_element_type=F32)
                for t, val in enumerate((mf, b_f, g_f, mb, b_b, g_b))]

        def rows_of(c):
            out = jnp.zeros((8, L), F32)
            for r, val in enumerate((z_f, z_b, g_f, g_b, b_f, b_b)):
                out = jnp.where(sub8 == r, val[c:c + 1], out)
            return out

        return rows_of, reps

    lat_rows, lat_reps = chunk_tables(gt_ref[0, 0], n_chunks, False)
    for c in range(n_chunks):
        row_ref[c] = lat_rows(c)
        for t in range(N_TAB):
            tab_ref[t, c * L:(c + 1) * L, :] = lat_reps[t][:, c * 128:(c + 1) * 128]

    def lanes2(x):
        return jnp.concatenate([x, x], axis=1)

    def advance(k, v, g_rep, g_row, b_last, c_ref, m_ref):
        m = m_ref[...]
        m_new = jnp.maximum(b_last + m, jnp.max(g_row, axis=-1, keepdims=True))
        kw = (k.astype(F32) * jnp.exp(lanes2(g_rep) - m_new)).astype(BF16)
        v_aug = jnp.concatenate([v, ones_cols], axis=1)
        c_ref[...] = jnp.exp(b_last + m - m_new) * c_ref[...] + lax.dot_general(
            kw, v_aug, TN_DIMS, preferred_element_type=F32)
        m_ref[...] = m_new

    for r in (cf_ref, mf_ref, cb_ref, mb_ref):
        r[...] = jnp.zeros_like(r)
    ctx_rows, ctx_reps = chunk_tables(cgt_ref[0, 0], n_ctx_chunks, True)
    for c in range(n_ctx_chunks):
        rows = ctx_rows(c)
        advance(ck_ref[c * L:(c + 1) * L, :], cv_ref[0, c * L:(c + 1) * L, :],
                ctx_reps[2][:, c * 128:(c + 1) * 128], rows[2:3], rows[4:5, L - 1:L], cf_ref, mf_ref)
    for c in reversed(range(n_ctx_chunks)):
        rows = ctx_rows(c)
        advance(ck_ref[c * L:(c + 1) * L, :], cv_ref[0, c * L:(c + 1) * L, :],
                ctx_reps[5][:, c * 128:(c + 1) * 128], rows[3:4], rows[5:6, 0:1], cb_ref, mb_ref)

    def state_pass(i, carry):
        cb = n_chunks - 1 - i
        rf = pl.multiple_of(i * L, L)
        rb = pl.multiple_of(cb * L, L)
        cfs_ref[i] = cf_ref[...].astype(BF16)
        mfs_ref[i] = mf_ref[...]
        cbs_ref[cb] = cb_ref[...].astype(BF16)
        mbs_ref[cb] = mb_ref[...]
        rows_f = row_ref[i]
        rows_b = row_ref[cb]
        advance(k_ref[pl.ds(rf, L), :], v_ref[0, pl.ds(rf, L), :], tab_ref[2, pl.ds(rf, L), :],
                rows_f[2:3], rows_f[4:5, L - 1:L], cf_ref, mf_ref)
        advance(k_ref[pl.ds(rb, L), :], v_ref[0, pl.ds(rb, L), :], tab_ref[5, pl.ds(rb, L), :],
                rows_b[3:4], rows_b[5:6, 0:1], cb_ref, mb_ref)
        conv_q(i)
        return carry

    def finish_states():
        conv_q(n_chunks - 1)
        cfs_ref[n_chunks - 1] = cf_ref[...].astype(BF16)
        mfs_ref[n_chunks - 1] = mf_ref[...]
        cbs_ref[0] = cb_ref[...].astype(BF16)
        mbs_ref[0] = mb_ref[...]

    def direction(q, v_aug, s, z_row, zmax_rep, b_rep, mask, c_in, m_in):
        mx = jnp.maximum(zmax_rep, m_in)
        att = s * jnp.exp((z_row - lanes2(mx)) + mask)
        na = jnp.dot(att.astype(BF16), v_aug, preferred_element_type=F32)
        qa = jnp.dot(q, c_in, preferred_element_type=F32)
        a = jnp.exp(m_in - mx)
        num = na[:, 0:HEAD_W] + lanes2(a) * qa[:, 0:HEAD_W]
        den = na[:, HEAD_W:] + a * qa[:, HEAD_W:]
        scale = 1.0 / jnp.maximum(jnp.abs(den), jnp.exp(-(b_rep + mx)))
        return num * lanes2(scale)

    mask_ref[0] = jnp.where(ci <= ri, 0.0, NEG_INF)
    mask_ref[1] = jnp.where(ci > ri, 0.0, NEG_INF)

    def out_chunk(c):
        r0 = pl.multiple_of(c * L, L)
        q = q_ref[pl.ds(r0, L), :]
        k = k_ref[pl.ds(r0, L), :]
        v_aug = jnp.concatenate([v_ref[0, pl.ds(r0, L), :], ones_cols], axis=1)
        s = lax.dot_general(q, k, NT_DIMS, preferred_element_type=F32)
        rows = row_ref[c]
        tot = direction(q, v_aug, s, rows[0:1], tab_ref[0, pl.ds(r0, L), :], tab_ref[1, pl.ds(r0, L), :],
                        mask_ref[0], cfs_ref[c], mfs_ref[c])
        tot = tot + direction(q, v_aug, s, rows[1:2], tab_ref[3, pl.ds(r0, L), :],
                              tab_ref[4, pl.ds(r0, L), :], mask_ref[1], cbs_ref[c], mbs_ref[c])
        mo = mo_ref[0, pl.ds(r0, L), :].astype(F32)
        o_ref[0, pl.ds(r0, L), :] = (_layer_norm(tot) * jax.nn.sigmoid(mo)).astype(BF16)

    return state_pass, finish_states, out_chunk


def _mlstm_scratch(T, Tc, n_chunks):
    state = [pltpu.VMEM((HEAD_W, AUG_W), F32), pltpu.VMEM((1, 1), F32)]
    snaps = [pltpu.VMEM((n_chunks, HEAD_W, AUG_W), BF16), pltpu.VMEM((n_chunks, 1, 1), F32)]
    return [pltpu.VMEM((N_TAB, T, 128), F32), pltpu.VMEM((n_chunks, 8, SCAN_L), F32)] \
        + state + state + snaps + snaps + [pltpu.VMEM((2, SCAN_L, SCAN_L), F32)] \
        + [pltpu.VMEM((T + 16, HEAD_W), F32), pltpu.VMEM((T, HEAD_W), BF16),
           pltpu.VMEM((T, HEAD_W), BF16), pltpu.VMEM((Tc, HEAD_W), BF16)]


def _scan_kernel(n_ret_scratch, dl_ref, rq_ref, rk_ref, rv_ref, rg_ref, rck_ref, rcv_ref,
                 mq_ref, mk_ref, mv_ref, mo_ref, mck_ref, mcv_ref, gt_ref, cgt_ref,
                 wq_ref, bq_ref, wk_ref, bk_ref, r_ref, m_ref, *scratch):
    n_chunks = rq_ref.shape[1] // SCAN_L
    ret = _ret_build(dl_ref, rq_ref, rk_ref, rv_ref, rg_ref, rck_ref, rcv_ref, r_ref,
                     *scratch[:n_ret_scratch])
    mls = _mlstm_build(mq_ref, mk_ref, mv_ref, mo_ref, mck_ref, mcv_ref, gt_ref, cgt_ref,
                       wq_ref, bq_ref, wk_ref, bk_ref, m_ref, *scratch[n_ret_scratch:])

    def state_pass(i, carry):
        ret[0](i, carry)
        mls[0](i, carry)
        return carry

    lax.fori_loop(0, n_chunks - 1, state_pass, 0)
    ret[1]()
    mls[1]()

    def out_pass(i, carry):
        for c in (2 * i, 2 * i + 1):
            ret[2](c)
            mls[2](c)
        return carry

    lax.fori_loop(0, n_chunks // 2, out_pass, 0)


def _scans(decay_logit, p_lat, p_ctx, gt, cgt, conv_w, conv_b, ret_lat, ret_ctx, ml_lat, ml_ctx):
    B, T, _ = p_lat.shape
    Tc = p_ctx.shape[1]
    assert T % (2 * SCAN_L) == 0 and Tc % SCAN_L == 0 and T // SCAN_L <= 8
    n_chunks = T // SCAN_L

    def lat(sec):
        return pl.BlockSpec((1, T, HEAD_W), lambda b, h: (b, 0, sec * HEADS + h))

    def cx(sec):
        return pl.BlockSpec((1, Tc, HEAD_W), lambda b, h: (b, 0, sec * HEADS + h))

    gates = pl.BlockSpec((1, 1, N_GK, 8, SCAN_L), lambda b, h: (b, h, 0, 0, 0))
    out = pl.BlockSpec((1, T, HEAD_W), lambda b, h: (b, 0, h))
    conv_specs = [pl.BlockSpec((3, HEAD_W), lambda b, h: (0, h)),
                  pl.BlockSpec((1, HEAD_W), lambda b, h: (0, h)),
                  pl.BlockSpec((3, HEAD_W), lambda b, h: (0, HEADS + h)),
                  pl.BlockSpec((1, HEAD_W), lambda b, h: (0, HEADS + h))]
    ret_scratch = _ret_scratch(n_chunks)
    return pl.pallas_call(
        functools.partial(_scan_kernel, len(ret_scratch)),
        out_shape=(jax.ShapeDtypeStruct((B, T, BRANCH_W), BF16),
                   jax.ShapeDtypeStruct((B, T, BRANCH_W), BF16)),
        grid=(B, HEADS),
        in_specs=[pl.BlockSpec(memory_space=pltpu.SMEM)]
        + [lat(s) for s in ret_lat] + [cx(s) for s in ret_ctx]
        + [lat(s) for s in ml_lat] + [cx(s) for s in ml_ctx] + [gates, gates] + conv_specs,
        out_specs=(out, out),
        scratch_shapes=ret_scratch + _mlstm_scratch(T, Tc, n_chunks),
        compiler_params=_cparams("parallel", "parallel"),
        name="scans",
    )(decay_logit, *([p_lat] * 4), *([p_ctx] * 2), *([p_lat] * 4), *([p_ctx] * 2), gt, cgt,
      conv_w, conv_b, conv_w, conv_b)


def _merge_kernel(alpha, r_ref, m_ref, gr_ref, gm_ref, x_ref, g1_ref, sh2_ref, sc2_ref,
                  lng_ref, lnb_ref, wr_ref, wm_ref, wo_ref, wrt_ref, brt_ref,
                  x1_ref, ua_ref, ub_ref, lt_ref):
    yr = jnp.dot(r_ref[0], wr_ref[...], preferred_element_type=F32)
    ym = jnp.dot(m_ref[0], wm_ref[...], preferred_element_type=F32)
    y = jax.nn.sigmoid(gr_ref[0].astype(F32)) * yr + jax.nn.sigmoid(gm_ref[0].astype(F32)) * ym
    yo = jnp.dot(y.astype(BF16), wo_ref[...], preferred_element_type=F32)
    x1 = _layer_norm(alpha * x_ref[0] + g1_ref[0] * yo) * lng_ref[...] + lnb_ref[...]
    x1_ref[0] = x1
    u2 = _layer_norm(x1) * (1.0 + sc2_ref[0]) + sh2_ref[0]
    lt_ref[0] = lax.dot_general(wrt_ref[...], u2.astype(BF16), NT_DIMS,
                                preferred_element_type=F32) + brt_ref[...]
    ua_ref[0] = _pack_pairs(u2[:, 0:PACK_W], u2[:, PACK_W:2 * PACK_W])
    ub_ref[0] = _pack_pairs(u2[:, 2 * PACK_W:3 * PACK_W], u2[:, 3 * PACK_W:4 * PACK_W])


def _merge(alpha, r, m, p_lat, sec_gates, x, g1, sh2, sc2, lng, lnb, wr, wm, wo, wrt, brt):
    B, T, D = x.shape
    tm = MERGE_TM

    def tile(w):
        return pl.BlockSpec((1, tm, w), lambda b, i: (b, i, 0))

    def sec(s):
        return pl.BlockSpec((1, tm, BRANCH_W), lambda b, i: (b, i, s))

    def mod():
        return pl.BlockSpec((1, 1, D), lambda b, i: (b, 0, 0))

    def const(shape):
        return pl.BlockSpec(shape, lambda b, i: (0,) * len(shape))

    return pl.pallas_call(
        functools.partial(_merge_kernel, alpha),
        out_shape=(jax.ShapeDtypeStruct((B, T, D), F32),
                   jax.ShapeDtypeStruct((B, T, PACK_W), U32),
                   jax.ShapeDtypeStruct((B, T, PACK_W), U32),
                   jax.ShapeDtypeStruct((B, ROUTE_ROWS, T), F32)),
        grid=(B, T // tm),
        in_specs=[tile(BRANCH_W), tile(BRANCH_W), sec(sec_gates[0]), sec(sec_gates[1]), tile(D),
                  mod(), mod(), mod(), const((1, D)), const((1, D)),
                  const((BRANCH_W, D)), const((BRANCH_W, D)), const((D, D)),
                  const((ROUTE_ROWS, D)), const((ROUTE_ROWS, 1))],
        out_specs=(tile(D), tile(PACK_W), tile(PACK_W),
                   pl.BlockSpec((1, ROUTE_ROWS, tm), lambda b, i: (b, 0, i))),
        compiler_params=_cparams("parallel", "parallel"),
        name="merge",
    )(r, m, p_lat, p_lat, x, g1, sh2, sc2, lng, lnb, wr, wm, wo, wrt, brt)


def _route_kernel(lt_ref, ri_ref, rw_ref, cnt_ref, carry_ref, u_ref):
    i = pl.program_id(0)
    tm = lt_ref.shape[2]

    @pl.when(i == 0)
    def _():
        carry_ref[...] = jnp.zeros_like(carry_ref)
        r = lax.broadcasted_iota(I32, (tm, tm), 0)
        c = lax.broadcasted_iota(I32, (tm, tm), 1)
        u_ref[...] = (r < c).astype(BF16)

    lt = lt_ref[0]
    lg = lt[0:N_GROUPS, :]
    eg = jnp.exp(lg - jnp.max(lg, axis=0, keepdims=True))
    pg = eg / jnp.sum(eg, axis=0, keepdims=True)
    pg_top = jnp.max(pg, axis=0, keepdims=True)
    rows_g = lax.broadcasted_iota(I32, pg.shape, 0)
    g_idx = jnp.min(jnp.where(pg == pg_top, rows_g, N_GROUPS), axis=0, keepdims=True)

    le = jnp.zeros((EXP_PER_GROUP, tm), F32)
    for g in range(N_GROUPS):
        lo = 8 + g * EXP_PER_GROUP
        le = jnp.where(g_idx == g, lt[lo:lo + EXP_PER_GROUP, :], le)
    ee = jnp.exp(le - jnp.max(le, axis=0, keepdims=True))
    pe = ee / jnp.sum(ee, axis=0, keepdims=True)
    rows_e = lax.broadcasted_iota(I32, pe.shape, 0)
    v1 = jnp.max(pe, axis=0, keepdims=True)
    i1 = jnp.min(jnp.where(pe == v1, rows_e, EXP_PER_GROUP), axis=0, keepdims=True)
    pe2 = jnp.where(rows_e == i1, -1.0, pe)
    v2 = jnp.max(pe2, axis=0, keepdims=True)
    i2 = jnp.min(jnp.where(pe2 == v2, rows_e, EXP_PER_GROUP), axis=0, keepdims=True)
    den = v1 + v2
    rw_ref[...] = jnp.zeros_like(rw_ref)
    rw_ref[0:1, :] = pg_top * v1 / den
    rw_ref[1:2, :] = pg_top * v2 / den
    e1 = g_idx * EXP_PER_GROUP + i1
    e2 = g_idx * EXP_PER_GROUP + i2

    rows_x = lax.broadcasted_iota(I32, (N_EXPERTS, tm), 0)
    oh1 = (rows_x == e1).astype(F32)
    oh2 = (rows_x == e2).astype(F32)
    both = oh1 + oh2
    before = carry_ref[:, 0:1] + jnp.dot(both.astype(BF16), u_ref[...], preferred_element_type=F32)
    ri_ref[0:1, :] = e1
    ri_ref[1:2, :] = e2
    ri_ref[2:3, :] = jnp.sum(oh1 * before, axis=0, keepdims=True).astype(I32)
    ri_ref[3:4, :] = jnp.sum(oh2 * before, axis=0, keepdims=True).astype(I32)
    carry_ref[...] = carry_ref[...] + jnp.sum(both, axis=1, keepdims=True)
    cnt_ref[...] = carry_ref[...].astype(I32)


def _route(lt):
    B, _, T = lt.shape
    tm = ROUTE_TM
    per_b = T // tm
    n = B * T
    return pl.pallas_call(
        _route_kernel,
        out_shape=(jax.ShapeDtypeStruct((4, n), I32),
                   jax.ShapeDtypeStruct((8, n), F32),
                   jax.ShapeDtypeStruct((N_EXPERTS, 128), I32)),
        grid=(n // tm,),
        in_specs=[pl.BlockSpec((1, ROUTE_ROWS, tm), lambda i: (i // per_b, 0, i % per_b))],
        out_specs=(pl.BlockSpec((4, tm), lambda i: (0, i)),
                   pl.BlockSpec((8, tm), lambda i: (0, i)),
                   pl.BlockSpec((N_EXPERTS, 128), lambda i: (0, 0))),
        scratch_shapes=[pltpu.VMEM((N_EXPERTS, 128), F32), pltpu.VMEM((tm, tm), BF16)],
        compiler_params=_cparams("arbitrary"),
        name="route",
    )(lt)


def _sc_mesh():
    return plsc.VectorSubcoreMesh(core_axis_name="c", subcore_axis_name="s")


def _sc_scatter2(rows, idx0, idx1, n_out):
    m, w = rows.shape

    @functools.partial(pl.kernel, out_type=jax.ShapeDtypeStruct((n_out, w), rows.dtype),
                       mesh=_sc_mesh(), scratch_types=[])
    def k(x_hbm, i0_hbm, i1_hbm, o_hbm):
        def body(x_vmem, i0_vmem, i1_vmem):
            pltpu.sync_copy(x_vmem, o_hbm.at[i0_vmem.at[0]])
            pltpu.sync_copy(x_vmem, o_hbm.at[i1_vmem.at[0]])

        pltpu.emit_pipeline(
            body,
            grid=(m // SC_WIN,),
            in_specs=[pl.BlockSpec((SC_WIN, w), lambda i: (i, 0)),
                      pl.BlockSpec((1, SC_WIN), lambda i: (0, i)),
                      pl.BlockSpec((1, SC_WIN), lambda i: (0, i))],
            out_specs=[],
            core_axis_name=("c", "s"),
            dimension_semantics=(pltpu.PARALLEL,),
        )(x_hbm, i0_hbm, i1_hbm)

    return k(rows, idx0.reshape(1, m), idx1.reshape(1, m))


def _sc_gather(table, idx):
    m = idx.shape[0]
    w = table.shape[1]

    @functools.partial(pl.kernel, out_type=jax.ShapeDtypeStruct((m, w), table.dtype),
                       mesh=_sc_mesh(), scratch_types=[])
    def k(t_hbm, i_hbm, o_hbm):
        def body(i_vmem, o_vmem):
            pltpu.sync_copy(t_hbm.at[i_vmem.at[0]], o_vmem)

        pltpu.emit_pipeline(
            body,
            grid=(m // SC_WIN,),
            in_specs=[pl.BlockSpec((1, SC_WIN), lambda i: (0, i))],
            out_specs=[pl.BlockSpec((SC_WIN, w), lambda i: (i, 0))],
            core_axis_name=("c", "s"),
            dimension_semantics=(pltpu.PARALLEL,),
        )(i_hbm, o_hbm)

    return k(table, idx.reshape(1, m))


def _expert_kernel(be_ref, nv_ref, xa_ref, xb_ref, w1f_ref, w3f_ref, w2f_ref, ya_ref, yb_ref,
                   w1_ref, w3_ref, w2_ref):
    j = pl.program_id(0)
    nv = nv_ref[j]

    @pl.when(jnp.logical_or(j == 0, be_ref[j] != be_ref[jnp.maximum(j - 1, 0)]))
    def _():
        w1_ref[0] = w1f_ref[0].astype(BF16)
        w3_ref[0] = w3f_ref[0].astype(BF16)
        w2_ref[0] = w2f_ref[0].astype(BF16)

    @pl.when(nv > 0)
    def _():
        valid = lax.broadcasted_iota(I32, xa_ref.shape, 0) < nv
        zero = jnp.zeros(xa_ref.shape, U32)
        parts = _unpack_pairs(jnp.where(valid, xa_ref[...], zero)) + \
            _unpack_pairs(jnp.where(valid, xb_ref[...], zero))
        x = jnp.concatenate([p.astype(BF16) for p in parts], axis=1)
        h1 = jnp.dot(x, w1_ref[0], preferred_element_type=F32)
        h3 = jnp.dot(x, w3_ref[0], preferred_element_type=F32)
        y = jnp.dot((_silu(h1) * h3).astype(BF16), w2_ref[0], preferred_element_type=F32)
        ya_ref[...] = _pack_pairs(y[:, 0:PACK_W], y[:, PACK_W:2 * PACK_W])
        yb_ref[...] = _pack_pairs(y[:, 2 * PACK_W:3 * PACK_W], y[:, 3 * PACK_W:4 * PACK_W])

    @pl.when(nv == 0)
    def _():
        ya_ref[...] = jnp.zeros_like(ya_ref)
        yb_ref[...] = jnp.zeros_like(yb_ref)


def _experts(block_exp, n_valid, xa, xb, w1, w3, w2):
    n_slots = xa.shape[0]
    n_blocks = n_slots // MOE_BLK
    d, de = w1.shape[1], w1.shape[2]
    slot = pl.BlockSpec((MOE_BLK, PACK_W), lambda j, be, nv: (j, 0))
    grid_spec = pltpu.PrefetchScalarGridSpec(
        num_scalar_prefetch=2,
        grid=(n_blocks,),
        in_specs=[slot, slot,
                  pl.BlockSpec((1, d, de), lambda j, be, nv: (be[j], 0, 0)),
                  pl.BlockSpec((1, d, de), lambda j, be, nv: (be[j], 0, 0)),
                  pl.BlockSpec((1, de, d), lambda j, be, nv: (be[j], 0, 0))],
        out_specs=(slot, slot),
        scratch_shapes=[pltpu.VMEM((1, d, de), BF16), pltpu.VMEM((1, d, de), BF16),
                        pltpu.VMEM((1, de, d), BF16)],
    )
    return pl.pallas_call(
        _expert_kernel,
        out_shape=(jax.ShapeDtypeStruct((n_slots, PACK_W), U32),
                   jax.ShapeDtypeStruct((n_slots, PACK_W), U32)),
        grid_spec=grid_spec,
        compiler_params=_cparams("arbitrary"),
        name="experts",
    )(block_exp, n_valid, xa, xb, w1, w3, w2)


def _final_kernel(alpha, x1_ref, a0_ref, b0_ref, a1_ref, b1_ref, w_ref, g2_ref, lng_ref, lnb_ref,
                  *rest):
    o_ref = rest[-1]
    w = w_ref[...].T
    w0 = w[:, 0:1]
    w1 = w[:, 1:2]
    parts0 = _unpack_pairs(a0_ref[...]) + _unpack_pairs(b0_ref[...])
    parts1 = _unpack_pairs(a1_ref[...]) + _unpack_pairs(b1_ref[...])
    f = jnp.concatenate([w0 * p0 + w1 * p1 for p0, p1 in zip(parts0, parts1)], axis=1)
    o_ref[0] = _layer_norm(alpha * x1_ref[0] + g2_ref[0] * f) * lng_ref[...] + lnb_ref[...]


def _final(alpha, b0, nb, x1, ya, yb, w, g2, lng, lnb, out_prev):
    B, T, D = x1.shape
    tm = MERGE_TM
    per_b = T // tm
    n_tiles = nb * per_b

    def rows(k):
        return pl.BlockSpec((tm, PACK_W), lambda b, i: (k * n_tiles + b * per_b + i, 0))

    in_specs = [pl.BlockSpec((1, tm, D), lambda b, i: (b + b0, i, 0)),
                rows(0), rows(0), rows(1), rows(1),
                pl.BlockSpec((8, tm), lambda b, i: (0, (b + b0) * per_b + i)),
                pl.BlockSpec((1, 1, D), lambda b, i: (b + b0, 0, 0)),
                pl.BlockSpec((1, D), lambda b, i: (0, 0)),
                pl.BlockSpec((1, D), lambda b, i: (0, 0))]
    args = [x1, ya, yb, ya, yb, w, g2, lng, lnb]
    aliases = {}
    if out_prev is not None:
        in_specs.append(pl.BlockSpec(memory_space=pl.ANY))
        args.append(out_prev)
        aliases = {len(args) - 1: 0}
    return pl.pallas_call(
        functools.partial(_final_kernel, alpha),
        out_shape=jax.ShapeDtypeStruct((B, T, D), F32),
        grid=(nb, per_b),
        in_specs=in_specs,
        out_specs=pl.BlockSpec((1, tm, D), lambda b, i: (b + b0, i, 0)),
        input_output_aliases=aliases,
        compiler_params=_cparams("parallel", "parallel"),
        name="final",
    )(*args)


def _rotary_tables(T):
    quarter = HEAD_W // 4
    freqs = ROPE_BASE ** (-jnp.arange(quarter, dtype=F32) / quarter)
    t = jnp.arange(T)
    ang_r = (t // GRID_W).astype(F32)[:, None] * freqs[None, :]
    ang_c = (t % GRID_W).astype(F32)[:, None] * freqs[None, :]
    cos = jnp.concatenate([jnp.cos(ang_r)] * 2 + [jnp.cos(ang_c)] * 2, axis=1)
    sin = jnp.concatenate([-jnp.sin(ang_r), jnp.sin(ang_r), -jnp.sin(ang_c), jnp.sin(ang_c)], axis=1)
    return cos, sin


def _per_head_gates(gt):
    B, _, T = gt.shape
    n_chunks = T // SCAN_L
    gth = gt.reshape(B, N_GK, HEADS, n_chunks, SCAN_L).transpose(0, 2, 1, 3, 4)
    return jnp.pad(gth, ((0, 0), (0, 0), (0, 0), (0, 8 - n_chunks), (0, 0)))


def _table_lookup(table, idx):
    sel = idx[..., None] == jnp.arange(table.shape[0], dtype=idx.dtype)
    return jnp.sum(jnp.where(sel, table, 0), axis=-1)


def kernel(x, c, ctx, c_ctx, w_ada, b_ada, w_in, b_mgate, ml_conv_w, ml_conv_b, ret_decay_logit, w_ret_branch, w_ml_branch, w_out, ln1_g, ln1_b, w_rg, b_rg, w_re, b_re, w_e1, w_e3, w_e2, ln2_g, ln2_b):
    B, T, D = x.shape
    depth = w_ada.shape[0]
    assert depth == 1 and D == BRANCH_W and T % GRID_W == 0
    alpha = (2 * depth) ** 0.25
    n_tok = B * T

    n_rows = -(-(B + 1) // 8) * 8
    cs = jnp.zeros((n_rows, D), F32).at[:B].set(c).at[B].set(c_ctx)
    mod = _ada(cs, w_ada[0], b_ada[0][None, :])
    sh1, sc1, g1, sh2, sc2, g2 = [mod[:B, None, i * D:(i + 1) * D] for i in range(6)]
    csh1 = mod[B, 0 * D:1 * D].reshape(1, 1, D)
    csc1 = mod[B, 1 * D:2 * D].reshape(1, 1, D)

    w = w_in[0]
    sec_w = [w[:, s * BRANCH_W:(s + 1) * BRANCH_W] for s in range(8)]
    g_lo = 8 * BRANCH_W
    w_gate_t = w[:, g_lo:g_lo + N_GATES].T.astype(BF16)
    b_gate = b_mgate[0][:, None]
    sec_w += [w[:, g_lo + N_GATES:g_lo + N_GATES + D], w[:, g_lo + N_GATES + D:]]
    w_lat = jnp.concatenate(sec_w, axis=1).astype(BF16)
    w_ctx = jnp.concatenate([sec_w[1], sec_w[2], sec_w[5], sec_w[6]], axis=1).astype(BF16)
    kinds_lat = ("rot", "rot_scale") + ("plain",) * 8
    kinds_ctx = ("scale", "plain", "plain", "plain")
    p_lat, gt_lat = _proj(x, sh1, sc1, w_lat, w_gate_t, b_gate, kinds_lat, _rotary_tables(T))
    Tc = ctx.shape[1]
    p_ctx, gt_ctx = _proj(ctx.reshape(1, B * Tc, D), csh1, csc1, w_ctx, w_gate_t, b_gate, kinds_ctx)
    p_ctx = p_ctx.reshape(B, Tc, -1)
    gt_ctx = gt_ctx.reshape(N_GATES, B, Tc).transpose(1, 0, 2)

    ret, mls = _scans(ret_decay_logit[0], p_lat, p_ctx, _per_head_gates(gt_lat), _per_head_gates(gt_ctx),
                      ml_conv_w[0], ml_conv_b[0][None, :], (0, 1, 2, 3), (0, 1), (4, 5, 6, 7), (2, 3))

    wrt = jnp.zeros((ROUTE_ROWS, D), F32).at[:N_GROUPS].set(w_rg[0].T).at[8:8 + N_EXPERTS].set(w_re[0].T)
    brt = jnp.zeros((ROUTE_ROWS, 1), F32).at[:N_GROUPS, 0].set(b_rg[0]).at[8:8 + N_EXPERTS, 0].set(b_re[0])
    x1, ua, ub, lt = _merge(alpha, ret, mls, p_lat, (8, 9), x, g1, sh2, sc2,
                            ln1_g[0][None, :], ln1_b[0][None, :],
                            w_ret_branch[0].astype(BF16), w_ml_branch[0].astype(BF16),
                            w_out[0].astype(BF16), wrt.astype(BF16), brt)

    ri, rw, cnt = _route(lt)

    counts = cnt[:, 0]
    padded = (counts + MOE_BLK - 1) // MOE_BLK * MOE_BLK
    pad_end = jnp.cumsum(padded)
    pad_off = pad_end - padded
    dest = _table_lookup(pad_off, ri[0:2]) + ri[2:4]
    n_blocks = (2 * n_tok) // MOE_BLK + N_EXPERTS
    n_slots = n_blocks * MOE_BLK
    block_start = jnp.arange(n_blocks, dtype=I32) * MOE_BLK
    block_exp = jnp.minimum((block_start[:, None] >= pad_end[None, :]).sum(1), N_EXPERTS - 1).astype(I32)
    n_valid = jnp.clip(_table_lookup(counts, block_exp) - (block_start - _table_lookup(pad_off, block_exp)),
                       0, MOE_BLK).astype(I32)

    xa = _sc_scatter2(ua.reshape(n_tok, PACK_W), dest[0], dest[1], n_slots)
    xb = _sc_scatter2(ub.reshape(n_tok, PACK_W), dest[0], dest[1], n_slots)
    ya, yb = _experts(block_exp, n_valid, xa, xb, w_e1[0], w_e3[0], w_e2[0])
    n_range = COMBINE_SPLIT if B % COMBINE_SPLIT == 0 else 1
    nb = B // n_range
    out = None
    for r in range(n_range):
        idx = dest[:, r * nb * T:(r + 1) * nb * T].reshape(2 * nb * T)
        out = _final(alpha, r * nb, nb, x1, _sc_gather(ya, idx), _sc_gather(yb, idx), rw, g2,
                     ln2_g[0][None, :], ln2_b[0][None, :], out)
    return out
```

```python
import functools

import jax
import jax.numpy as jnp
from jax import lax
from jax.experimental import pallas as pl
from jax.experimental.pallas import tpu as pltpu
from jax.experimental.pallas import tpu_sc as plsc

F32 = jnp.float32
BF16 = jnp.bfloat16
U32 = jnp.uint32
I32 = jnp.int32
HIGHEST = lax.Precision.HIGHEST

HEADS = 4
HEAD_W = 256
BRANCH_W = HEADS * HEAD_W
GRID_W = 64
ROPE_BASE = 10000.0
N_GATES = 16
N_GK = N_GATES // HEADS
N_GROUPS = 4
EXP_PER_GROUP = 8
N_EXPERTS = N_GROUPS * EXP_PER_GROUP
LN_EPS = 1e-5
NEG_INF = -1e30
KEY_SCALE = HEAD_W ** -0.5

SCAN_L = 256
CONV_ROWS = 128
PROJ_TM = 2048
PROJ_SUB = 256
MERGE_TM = 512
ROUTE_TM = 512
MOE_BLK = 512
SC_WIN = 128
PACK_W = 256
ROUTE_ROWS = 64
N_TAB = 6
AUG_W = HEAD_W + 128
VMEM_LIMIT = 48 * 1024 * 1024

NT_DIMS = (((1,), (1,)), ((), ()))
TN_DIMS = (((0,), (0,)), ((), ()))


def _cparams(*sem):
    return pltpu.CompilerParams(dimension_semantics=sem, vmem_limit_bytes=VMEM_LIMIT)


def _layer_norm(x):
    mu = jnp.mean(x, axis=-1, keepdims=True)
    xc = x - mu
    var = jnp.mean(xc * xc, axis=-1, keepdims=True)
    return xc * lax.rsqrt(var + LN_EPS)


def _log_sigmoid(x):
    return jnp.minimum(x, 0.0) - jnp.log1p(jnp.exp(-jnp.abs(x)))


def _silu(x):
    return x * jax.nn.sigmoid(x)


def _pack_pairs(hi, lo):
    hb = lax.bitcast_convert_type(hi.astype(BF16).astype(F32), U32)
    lb = lax.bitcast_convert_type(lo.astype(BF16).astype(F32), U32)
    return (hb & jnp.uint32(0xFFFF0000)) | (lb >> 16)


def _unpack_pairs(p):
    hi = lax.bitcast_convert_type(p & jnp.uint32(0xFFFF0000), F32)
    lo = lax.bitcast_convert_type(p << 16, F32)
    return hi, lo


def _split3(x):
    hi = x.astype(BF16).astype(F32)
    r1 = x - hi
    mid = r1.astype(BF16).astype(F32)
    lo = (r1 - mid).astype(BF16).astype(F32)
    return jnp.concatenate([hi, mid, lo], axis=0).astype(BF16)


def _ada_kernel(c_ref, w_ref, b_ref, o_ref):
    s = _silu(c_ref[...])
    o_ref[...] = jnp.dot(s, w_ref[...], precision=HIGHEST, preferred_element_type=F32) + b_ref[...]


def _ada(cs, w, b):
    rows, d = cs.shape
    cols = w.shape[1]
    tn = 1024
    return pl.pallas_call(
        _ada_kernel,
        out_shape=jax.ShapeDtypeStruct((rows, cols), F32),
        grid=(cols // tn,),
        in_specs=[pl.BlockSpec((rows, d), lambda j: (0, 0)),
                  pl.BlockSpec((d, tn), lambda j: (0, j)),
                  pl.BlockSpec((1, tn), lambda j: (0, j))],
        out_specs=pl.BlockSpec((rows, tn), lambda j: (0, j)),
        compiler_params=_cparams("parallel"),
        name="ada",
    )(cs, w, b)


def _proj_kernel(kinds, x_ref, sh_ref, sc_ref, w_ref, wg_ref, bg_ref, *rest):
    if "rot" in kinds or "rot_scale" in kinds:
        cos_ref, sin_ref, o_ref, gt_ref, u_ref = rest
    else:
        o_ref, gt_ref, u_ref = rest
    j = pl.program_id(2)
    tm = x_ref.shape[1]
    sub = min(PROJ_SUB, tm)

    def rotary(acc, rows, scale):
        for s in range(acc.shape[1] // 128):
            a = acc[:, s * 128:(s + 1) * 128]
            half = s % 2
            cs = cos_ref[rows, half * 128:(half + 1) * 128]
            sn = sin_ref[rows, half * 128:(half + 1) * 128]
            r = a * cs + pltpu.roll(a, 64, 1) * sn
            if scale != 1.0:
                r = r * scale
            o_ref[0, rows, s * 128:(s + 1) * 128] = r.astype(BF16)

    def section(kind, first):
        for r in range(tm // sub):
            rows = slice(r * sub, (r + 1) * sub)
            if first:
                u = _layer_norm(x_ref[0, rows, :]) * (1.0 + sc_ref[0]) + sh_ref[0]
                ub = u.astype(BF16)
                u_ref[rows, :] = ub
                gt_ref[0, :, rows] = lax.dot_general(wg_ref[...], ub, NT_DIMS,
                                                     preferred_element_type=F32) + bg_ref[...]
            else:
                ub = u_ref[rows, :]
            acc = jnp.dot(ub, w_ref[...], preferred_element_type=F32)
            if kind == "rot":
                rotary(acc, rows, 1.0)
            elif kind == "rot_scale":
                rotary(acc, rows, KEY_SCALE)
            elif kind == "scale":
                o_ref[0, rows, :] = (acc * KEY_SCALE).astype(BF16)
            else:
                o_ref[0, rows, :] = acc.astype(BF16)

    variants = {}
    for s, kind in enumerate(kinds):
        variants.setdefault((kind, s == 0), []).append(s)
    for (kind, first), secs in variants.items():
        cond = functools.reduce(jnp.logical_or, [j == s for s in secs])

        @pl.when(cond)
        def _(kind=kind, first=first):
            section(kind, first)


def _proj(x, sh, sc, w_main, w_gate_t, b_gate, kinds, tables=None):
    B, T, D = x.shape
    n_sec = len(kinds)
    tm = min(PROJ_TM, T)
    tn = BRANCH_W
    assert T % tm == 0
    in_specs = [
        pl.BlockSpec((1, tm, D), lambda i, b, j: (b, i, 0)),
        pl.BlockSpec((1, 1, D), lambda i, b, j: (b, 0, 0)),
        pl.BlockSpec((1, 1, D), lambda i, b, j: (b, 0, 0)),
        pl.BlockSpec((D, tn), lambda i, b, j: (0, j)),
        pl.BlockSpec((N_GATES, D), lambda i, b, j: (0, 0)),
        pl.BlockSpec((N_GATES, 1), lambda i, b, j: (0, 0)),
    ]
    args = [x, sh, sc, w_main, w_gate_t, b_gate]
    if tables is not None:
        in_specs += [pl.BlockSpec((tm, HEAD_W), lambda i, b, j: (i, 0))] * 2
        args += list(tables)
    return pl.pallas_call(
        functools.partial(_proj_kernel, kinds),
        out_shape=(jax.ShapeDtypeStruct((B, T, n_sec * tn), BF16),
                   jax.ShapeDtypeStruct((B, N_GATES, T), F32)),
        grid=(T // tm, B, n_sec),
        in_specs=in_specs,
        out_specs=(pl.BlockSpec((1, tm, tn), lambda i, b, j: (b, i, j)),
                   pl.BlockSpec((1, N_GATES, tm), lambda i, b, j: (b, 0, i))),
        scratch_shapes=[pltpu.VMEM((tm, D), BF16)],
        compiler_params=_cparams("parallel", "parallel", "arbitrary"),
        name="proj_lat" if tables is not None else "proj_ctx",
    )(*args)


def _ret_build(dl_ref, q_ref, k_ref, v_ref, rg_ref, ck_ref, cv_ref, o_ref,
               sf_ref, sb_ref, fs_ref, bs_ref, dec_ref, d_ref):
    h = pl.program_id(1)
    L = SCAN_L
    n_chunks = q_ref.shape[1] // L
    n_ctx_chunks = ck_ref.shape[1] // L
    lgf = _log_sigmoid(jnp.full((1, 1), dl_ref[0, h], F32))
    lgb = _log_sigmoid(jnp.full((1, 1), dl_ref[1, h], F32))

    ri = lax.broadcasted_iota(I32, (L, L), 0)
    ci = lax.broadcasted_iota(I32, (L, L), 1)
    rel = (ri - ci).astype(F32)
    d_ref[...] = jnp.where(rel >= 0.0, jnp.exp(jnp.maximum(rel, 0.0) * lgf),
                           jnp.exp(jnp.maximum(-rel, 0.0) * lgb))
    row = lax.broadcasted_iota(I32, (L, HEAD_W), 0).astype(F32)
    dec_ref[0] = jnp.exp((row + 1.0) * lgf)
    dec_ref[1] = jnp.exp((L - 1.0 - row) * lgf)
    dec_ref[2] = jnp.exp((L - row) * lgb)
    dec_ref[3] = jnp.exp(row * lgb)
    cdf = jnp.exp(L * lgf)
    cdb = jnp.exp(L * lgb)

    def update(s_ref, kc, vc, kd, cd):
        kdec = (kc.astype(F32) * kd).astype(BF16)
        s_ref[...] = s_ref[...] * cd + lax.dot_general(kdec, vc, TN_DIMS, preferred_element_type=F32)

    sf_ref[...] = jnp.zeros_like(sf_ref)
    sb_ref[...] = jnp.zeros_like(sb_ref)
    for c in range(n_ctx_chunks):
        update(sf_ref, ck_ref[0, c * L:(c + 1) * L, :], cv_ref[0, c * L:(c + 1) * L, :], dec_ref[1], cdf)
    for c in reversed(range(n_ctx_chunks)):
        update(sb_ref, ck_ref[0, c * L:(c + 1) * L, :], cv_ref[0, c * L:(c + 1) * L, :], dec_ref[3], cdb)

    def state_pass(i, carry):
        cb = n_chunks - 1 - i
        rf = pl.multiple_of(i * L, L)
        rb = pl.multiple_of(cb * L, L)
        fs_ref[i] = sf_ref[...].astype(BF16)
        bs_ref[cb] = sb_ref[...].astype(BF16)
        update(sf_ref, k_ref[0, pl.ds(rf, L), :], v_ref[0, pl.ds(rf, L), :], dec_ref[1], cdf)
        update(sb_ref, k_ref[0, pl.ds(rb, L), :], v_ref[0, pl.ds(rb, L), :], dec_ref[3], cdb)
        return carry

    def finish_states():
        fs_ref[n_chunks - 1] = sf_ref[...].astype(BF16)
        bs_ref[0] = sb_ref[...].astype(BF16)

    def out_chunk(c):
        r0 = pl.multiple_of(c * L, L)
        q = q_ref[0, pl.ds(r0, L), :]
        k = k_ref[0, pl.ds(r0, L), :]
        v = v_ref[0, pl.ds(r0, L), :]
        s = lax.dot_general(q, k, NT_DIMS, preferred_element_type=F32)
        att = (s * d_ref[...]).astype(BF16)
        o = jnp.dot(att, v, preferred_element_type=F32)
        o = o + jnp.dot(q, fs_ref[c], preferred_element_type=F32) * dec_ref[0]
        o = o + jnp.dot(q, bs_ref[c], preferred_element_type=F32) * dec_ref[2]
        rg = rg_ref[0, pl.ds(r0, L), :].astype(F32)
        o_ref[0, pl.ds(r0, L), :] = (_layer_norm(o) * _silu(rg)).astype(BF16)

    return state_pass, finish_states, out_chunk


def _ret_scratch(n_chunks):
    return [pltpu.VMEM((HEAD_W, HEAD_W), F32),
            pltpu.VMEM((HEAD_W, HEAD_W), F32),
            pltpu.VMEM((n_chunks, HEAD_W, HEAD_W), BF16),
            pltpu.VMEM((n_chunks, HEAD_W, HEAD_W), BF16),
            pltpu.VMEM((4, SCAN_L, HEAD_W), F32),
            pltpu.VMEM((SCAN_L, SCAN_L), F32)]


def _mlstm_build(qp_ref, kp_ref, v_ref, mo_ref, ckp_ref, cv_ref, gt_ref, cgt_ref,
                 wq_ref, bq_ref, wk_ref, bk_ref, o_ref,
                 tab_ref, row_ref,
                 cf_ref, mf_ref, cb_ref, mb_ref, cfs_ref, mfs_ref, cbs_ref, mbs_ref, mask_ref,
                 xf_ref, q_ref, k_ref, ck_ref):
    L = SCAN_L
    T = qp_ref.shape[1]
    Tc = ckp_ref.shape[1]
    n_chunks = T // L
    n_ctx_chunks = Tc // L
    CV = CONV_ROWS

    def conv_stage(src_ref, t_len):
        xf_ref[pl.ds(0, 8), :] = jnp.zeros((8, HEAD_W), F32)
        xf_ref[pl.ds(8 + t_len, 8), :] = jnp.zeros((8, HEAD_W), F32)
        xf_ref[pl.ds(8, t_len), :] = src_ref[0].astype(F32)

    def conv_chunk(c, w, b, dst_ref, scale):
        r0 = pl.multiple_of(c * CV, CV)
        win = xf_ref[pl.ds(r0, CV + 16), :]
        prev = pltpu.roll(win, 1, 0)[8:8 + CV, :]
        cur = win[8:8 + CV, :]
        nxt = pltpu.roll(win, CV + 15, 0)[8:8 + CV, :]
        y = _silu(prev * w[0:1, :] + cur * w[1:2, :] + nxt * w[2:3, :] + b)
        if scale != 1.0:
            y = y * scale
        dst_ref[pl.ds(r0, CV), :] = y.astype(BF16)

    def conv_all(src_ref, w_ref, b_ref, dst_ref, t_len, scale):
        conv_stage(src_ref, t_len)
        w = w_ref[...]
        b = b_ref[...]

        def body(c, carry):
            conv_chunk(c, w, b, dst_ref, scale)
            return carry

        lax.fori_loop(0, t_len // CV, body, 0)

    conv_all(kp_ref, wk_ref, bk_ref, k_ref, T, KEY_SCALE)
    conv_all(ckp_ref, wk_ref, bk_ref, ck_ref, Tc, KEY_SCALE)
    conv_stage(qp_ref, T)
    q_per_chunk = L // CV

    def conv_q(c):
        for u in range(q_per_chunk):
            conv_chunk(c * q_per_chunk + u, wq_ref[...], bq_ref[...], q_ref, 1.0)

    ri = lax.broadcasted_iota(I32, (L, L), 0)
    ci = lax.broadcasted_iota(I32, (L, L), 1)
    tri_u = (ri <= ci).astype(BF16)
    lane8 = lax.broadcasted_iota(I32, (8, L), 1)
    sub8 = lax.broadcasted_iota(I32, (8, L), 0)
    sel_r = lax.broadcasted_iota(I32, (24, 8 * 128), 0) % 8
    sel_c = lax.broadcasted_iota(I32, (24, 8 * 128), 1) // 128
    sel3 = (sel_r == sel_c).astype(BF16)
    ones_cols = jnp.ones((L, AUG_W - HEAD_W), BF16)

    def chunk_tables(g8, n_used, state_only):
        i_f, i_b = g8[0], g8[2]
        lf_f, lf_b = _log_sigmoid(g8[1]), _log_sigmoid(g8[3])
        cs3 = jnp.dot(_split3(jnp.concatenate([lf_f, lf_b], axis=0)), tri_u,
                      preferred_element_type=F32)
        cs = cs3[0:16] + cs3[16:32] + cs3[32:48]
        b_f = cs[0:8]
        b_b = cs[8:16, L - 1:L] - cs[8:16] + lf_b
        z_f = i_f - b_f
        z_b = i_b - b_b
        g_f = b_f[:, L - 1:L] - b_f + i_f
        g_b = b_b[:, 0:1] - b_b + i_b
        mf, mb = z_f, z_b
        s = 1
        while s < L:
            mf = jnp.maximum(mf, jnp.where(lane8 >= s, pltpu.roll(mf, s, 1), NEG_INF))
            mb = jnp.maximum(mb, jnp.where(lane8 < L - s, pltpu.roll(mb, L - s, 1), NEG_INF))
            s *= 2
        mb = jnp.where(lane8 < L - 1, pltpu.roll(mb, L - 1, 1), NEG_INF)
        reps = [None if state_only and t not in (2, 5) else
                lax.dot_general(_split3(val), sel3[:, 0:n_used * 128], TN_DIMS, preferred_element_type=F32)
                for t, val in enumerate((mf, b_f, g_f, mb, b_b, g_b))]

        def rows_of(c):
            out = jnp.zeros((8, L), F32)
            for r, val in enumerate((z_f, z_b, g_f, g_b, b_f, b_b)):
                out = jnp.where(sub8 == r, val[c:c + 1], out)
            return out

        return rows_of, reps

    lat_rows, lat_reps = chunk_tables(gt_ref[0, 0], n_chunks, False)
    for c in range(n_chunks):
        row_ref[c] = lat_rows(c)
        for t in range(N_TAB):
            tab_ref[t, c * L:(c + 1) * L, :] = lat_reps[t][:, c * 128:(c + 1) * 128]

    def lanes2(x):
        return jnp.concatenate([x, x], axis=1)

    def advance(k, v, g_rep, g_row, b_last, c_ref, m_ref):
        m = m_ref[...]
        m_new = jnp.maximum(b_last + m, jnp.max(g_row, axis=-1, keepdims=True))
        kw = (k.astype(F32) * jnp.exp(lanes2(g_rep) - m_new)).astype(BF16)
        v_aug = jnp.concatenate([v, ones_cols], axis=1)
        c_ref[...] = jnp.exp(b_last + m - m_new) * c_ref[...] + lax.dot_general(
            kw, v_aug, TN_DIMS, preferred_element_type=F32)
        m_ref[...] = m_new

    for r in (cf_ref, mf_ref, cb_ref, mb_ref):
        r[...] = jnp.zeros_like(r)
    ctx_rows, ctx_reps = chunk_tables(cgt_ref[0, 0], n_ctx_chunks, True)
    for c in range(n_ctx_chunks):
        rows = ctx_rows(c)
        advance(ck_ref[c * L:(c + 1) * L, :], cv_ref[0, c * L:(c + 1) * L, :],
                ctx_reps[2][:, c * 128:(c + 1) * 128], rows[2:3], rows[4:5, L - 1:L], cf_ref, mf_ref)
    for c in reversed(range(n_ctx_chunks)):
        rows = ctx_rows(c)
        advance(ck_ref[c * L:(c + 1) * L, :], cv_ref[0, c * L:(c + 1) * L, :],
                ctx_reps[5][:, c * 128:(c + 1) * 128], rows[3:4], rows[5:6, 0:1], cb_ref, mb_ref)

    def state_pass(i, carry):
        cb = n_chunks - 1 - i
        rf = pl.multiple_of(i * L, L)
        rb = pl.multiple_of(cb * L, L)
        cfs_ref[i] = cf_ref[...].astype(BF16)
        mfs_ref[i] = mf_ref[...]
        cbs_ref[cb] = cb_ref[...].astype(BF16)
        mbs_ref[cb] = mb_ref[...]
        rows_f = row_ref[i]
        rows_b = row_ref[cb]
        advance(k_ref[pl.ds(rf, L), :], v_ref[0, pl.ds(rf, L), :], tab_ref[2, pl.ds(rf, L), :],
                rows_f[2:3], rows_f[4:5, L - 1:L], cf_ref, mf_ref)
        advance(k_ref[pl.ds(rb, L), :], v_ref[0, pl.ds(rb, L), :], tab_ref[5, pl.ds(rb, L), :],
                rows_b[3:4], rows_b[5:6, 0:1], cb_ref, mb_ref)
        conv_q(i)
        return carry

    def finish_states():
        conv_q(n_chunks - 1)
        cfs_ref[n_chunks - 1] = cf_ref[...].astype(BF16)
        mfs_ref[n_chunks - 1] = mf_ref[...]
        cbs_ref[0] = cb_ref[...].astype(BF16)
        mbs_ref[0] = mb_ref[...]

    def direction(q, v_aug, s, z_row, zmax_rep, b_rep, mask, c_in, m_in):
        mx = jnp.maximum(zmax_rep, m_in)
        att = s * jnp.exp((z_row - lanes2(mx)) + mask)
        na = jnp.dot(att.astype(BF16), v_aug, preferred_element_type=F32)
        qa = jnp.dot(q, c_in, preferred_element_type=F32)
        a = jnp.exp(m_in - mx)
        num = na[:, 0:HEAD_W] + lanes2(a) * qa[:, 0:HEAD_W]
        den = na[:, HEAD_W:] + a * qa[:, HEAD_W:]
        scale = 1.0 / jnp.maximum(jnp.abs(den), jnp.exp(-(b_rep + mx)))
        return num * lanes2(scale)

    mask_ref[0] = jnp.where(ci <= ri, 0.0, NEG_INF)
    mask_ref[1] = jnp.where(ci > ri, 0.0, NEG_INF)

    def out_chunk(c):
        r0 = pl.multiple_of(c * L, L)
        q = q_ref[pl.ds(r0, L), :]
        k = k_ref[pl.ds(r0, L), :]
        v_aug = jnp.concatenate([v_ref[0, pl.ds(r0, L), :], ones_cols], axis=1)
        s = lax.dot_general(q, k, NT_DIMS, preferred_element_type=F32)
        rows = row_ref[c]
        tot = direction(q, v_aug, s, rows[0:1], tab_ref[0, pl.ds(r0, L), :], tab_ref[1, pl.ds(r0, L), :],
                        mask_ref[0], cfs_ref[c], mfs_ref[c])
        tot = tot + direction(q, v_aug, s, rows[1:2], tab_ref[3, pl.ds(r0, L), :],
                              tab_ref[4, pl.ds(r0, L), :], mask_ref[1], cbs_ref[c], mbs_ref[c])
        mo = mo_ref[0, pl.ds(r0, L), :].astype(F32)
        o_ref[0, pl.ds(r0, L), :] = (_layer_norm(tot) * jax.nn.sigmoid(mo)).astype(BF16)

    return state_pass, finish_states, out_chunk


def _mlstm_scratch(T, Tc, n_chunks):
    state = [pltpu.VMEM((HEAD_W, AUG_W), F32), pltpu.VMEM((1, 1), F32)]
    snaps = [pltpu.VMEM((n_chunks, HEAD_W, AUG_W), BF16), pltpu.VMEM((n_chunks, 1, 1), F32)]
    return [pltpu.VMEM((N_TAB, T, 128), F32), pltpu.VMEM((n_chunks, 8, SCAN_L), F32)] \
        + state + state + snaps + snaps + [pltpu.VMEM((2, SCAN_L, SCAN_L), F32)] \
        + [pltpu.VMEM((T + 16, HEAD_W), F32), pltpu.VMEM((T, HEAD_W), BF16),
           pltpu.VMEM((T, HEAD_W), BF16), pltpu.VMEM((Tc, HEAD_W), BF16)]


def _scan_kernel(n_ret_scratch, dl_ref, rq_ref, rk_ref, rv_ref, rg_ref, rck_ref, rcv_ref,
                 mq_ref, mk_ref, mv_ref, mo_ref, mck_ref, mcv_ref, gt_ref, cgt_ref,
                 wq_ref, bq_ref, wk_ref, bk_ref, r_ref, m_ref, *scratch):
    n_chunks = rq_ref.shape[1] // SCAN_L
    ret = _ret_build(dl_ref, rq_ref, rk_ref, rv_ref, rg_ref, rck_ref, rcv_ref, r_ref,
                     *scratch[:n_ret_scratch])
    mls = _mlstm_build(mq_ref, mk_ref, mv_ref, mo_ref, mck_ref, mcv_ref, gt_ref, cgt_ref,
                       wq_ref, bq_ref, wk_ref, bk_ref, m_ref, *scratch[n_ret_scratch:])

    def state_pass(i, carry):
        ret[0](i, carry)
        mls[0](i, carry)
        return carry

    lax.fori_loop(0, n_chunks - 1, state_pass, 0)
    ret[1]()
    mls[1]()

    def out_pass(i, carry):
        for c in (2 * i, 2 * i + 1):
            ret[2](c)
            mls[2](c)
        return carry

    lax.fori_loop(0, n_chunks // 2, out_pass, 0)


def _scans(decay_logit, p_lat, p_ctx, gt, cgt, conv_w, conv_b, ret_lat, ret_ctx, ml_lat, ml_ctx):
    B, T, _ = p_lat.shape
    Tc = p_ctx.shape[1]
    assert T % (2 * SCAN_L) == 0 and Tc % SCAN_L == 0 and T // SCAN_L <= 8
    n_chunks = T // SCAN_L

    def lat(sec):
        return pl.BlockSpec((1, T, HEAD_W), lambda b, h: (b, 0, sec * HEADS + h))

    def cx(sec):
        return pl.BlockSpec((1, Tc, HEAD_W), lambda b, h: (b, 0, sec * HEADS + h))

    gates = pl.BlockSpec((1, 1, N_GK, 8, SCAN_L), lambda b, h: (b, h, 0, 0, 0))
    out = pl.BlockSpec((1, T, HEAD_W), lambda b, h: (b, 0, h))
    conv_specs = [pl.BlockSpec((3, HEAD_W), lambda b, h: (0, h)),
                  pl.BlockSpec((1, HEAD_W), lambda b, h: (0, h)),
                  pl.BlockSpec((3, HEAD_W), lambda b, h: (0, HEADS + h)),
                  pl.BlockSpec((1, HEAD_W), lambda b, h: (0, HEADS + h))]
    ret_scratch = _ret_scratch(n_chunks)
    return pl.pallas_call(
        functools.partial(_scan_kernel, len(ret_scratch)),
        out_shape=(jax.ShapeDtypeStruct((B, T, BRANCH_W), BF16),
                   jax.ShapeDtypeStruct((B, T, BRANCH_W), BF16)),
        grid=(B, HEADS),
        in_specs=[pl.BlockSpec(memory_space=pltpu.SMEM)]
        + [lat(s) for s in ret_lat] + [cx(s) for s in ret_ctx]
        + [lat(s) for s in ml_lat] + [cx(s) for s in ml_ctx] + [gates, gates] + conv_specs,
        out_specs=(out, out),
        scratch_shapes=ret_scratch + _mlstm_scratch(T, Tc, n_chunks),
        compiler_params=_cparams("parallel", "parallel"),
        name="scans",
    )(decay_logit, *([p_lat] * 4), *([p_ctx] * 2), *([p_lat] * 4), *([p_ctx] * 2), gt, cgt,
      conv_w, conv_b, conv_w, conv_b)


def _merge_kernel(alpha, r_ref, m_ref, gr_ref, gm_ref, x_ref, g1_ref, sh2_ref, sc2_ref,
                  lng_ref, lnb_ref, wr_ref, wm_ref, wo_ref, wrt_ref, brt_ref,
                  x1_ref, ua_ref, ub_ref, ri_ref, rw_ref, cnt_ref, carry_ref, u_ref):
    @pl.when(jnp.logical_and(pl.program_id(0) == 0, pl.program_id(1) == 0))
    def _():
        carry_ref[...] = jnp.zeros_like(carry_ref)
        tm = x_ref.shape[1]
        r = lax.broadcasted_iota(I32, (tm, tm), 0)
        c = lax.broadcasted_iota(I32, (tm, tm), 1)
        u_ref[...] = (r < c).astype(BF16)

    yr = jnp.dot(r_ref[0], wr_ref[...], preferred_element_type=F32)
    ym = jnp.dot(m_ref[0], wm_ref[...], preferred_element_type=F32)
    y = jax.nn.sigmoid(gr_ref[0].astype(F32)) * yr + jax.nn.sigmoid(gm_ref[0].astype(F32)) * ym
    yo = jnp.dot(y.astype(BF16), wo_ref[...], preferred_element_type=F32)
    x1 = _layer_norm(alpha * x_ref[0] + g1_ref[0] * yo) * lng_ref[...] + lnb_ref[...]
    x1_ref[0] = x1
    u2 = _layer_norm(x1) * (1.0 + sc2_ref[0]) + sh2_ref[0]
    ua_ref[0] = _pack_pairs(u2[:, 0:PACK_W], u2[:, PACK_W:2 * PACK_W])
    ub_ref[0] = _pack_pairs(u2[:, 2 * PACK_W:3 * PACK_W], u2[:, 3 * PACK_W:4 * PACK_W])
    lt = lax.dot_general(wrt_ref[...], u2.astype(BF16), NT_DIMS, preferred_element_type=F32) + brt_ref[...]
    _route_tile(lt, ri_ref, rw_ref, cnt_ref, carry_ref, u_ref)


def _route_tile(lt, ri_ref, rw_ref, cnt_ref, carry_ref, u_ref):
    tm = lt.shape[1]
    lg = lt[0:N_GROUPS, :]
    eg = jnp.exp(lg - jnp.max(lg, axis=0, keepdims=True))
    pg = eg / jnp.sum(eg, axis=0, keepdims=True)
    pg_top = jnp.max(pg, axis=0, keepdims=True)
    rows_g = lax.broadcasted_iota(I32, pg.shape, 0)
    g_idx = jnp.min(jnp.where(pg == pg_top, rows_g, N_GROUPS), axis=0, keepdims=True)

    le = jnp.zeros((EXP_PER_GROUP, tm), F32)
    for g in range(N_GROUPS):
        lo = 8 + g * EXP_PER_GROUP
        le = jnp.where(g_idx == g, lt[lo:lo + EXP_PER_GROUP, :], le)
    ee = jnp.exp(le - jnp.max(le, axis=0, keepdims=True))
    pe = ee / jnp.sum(ee, axis=0, keepdims=True)
    rows_e = lax.broadcasted_iota(I32, pe.shape, 0)
    v1 = jnp.max(pe, axis=0, keepdims=True)
    i1 = jnp.min(jnp.where(pe == v1, rows_e, EXP_PER_GROUP), axis=0, keepdims=True)
    pe2 = jnp.where(rows_e == i1, -1.0, pe)
    v2 = jnp.max(pe2, axis=0, keepdims=True)
    i2 = jnp.min(jnp.where(pe2 == v2, rows_e, EXP_PER_GROUP), axis=0, keepdims=True)
    den = v1 + v2
    rw_ref[...] = jnp.zeros_like(rw_ref)
    rw_ref[0:1, :] = pg_top * v1 / den
    rw_ref[1:2, :] = pg_top * v2 / den
    e1 = g_idx * EXP_PER_GROUP + i1
    e2 = g_idx * EXP_PER_GROUP + i2

    rows_x = lax.broadcasted_iota(I32, (N_EXPERTS, tm), 0)
    oh1 = (rows_x == e1).astype(F32)
    oh2 = (rows_x == e2).astype(F32)
    both = oh1 + oh2
    before = carry_ref[:, 0:1] + jnp.dot(both.astype(BF16), u_ref[...], preferred_element_type=F32)
    ri_ref[0:1, :] = e1
    ri_ref[1:2, :] = e2
    ri_ref[2:3, :] = jnp.sum(oh1 * before, axis=0, keepdims=True).astype(I32)
    ri_ref[3:4, :] = jnp.sum(oh2 * before, axis=0, keepdims=True).astype(I32)
    carry_ref[...] = carry_ref[...] + jnp.sum(both, axis=1, keepdims=True)
    cnt_ref[...] = carry_ref[...].astype(I32)


def _merge(alpha, r, m, p_lat, sec_gates, x, g1, sh2, sc2, lng, lnb, wr, wm, wo, wrt, brt):
    B, T, D = x.shape
    tm = MERGE_TM
    per_b = T // tm
    n = B * T

    def tile(w):
        return pl.BlockSpec((1, tm, w), lambda b, i: (b, i, 0))

    def sec(s):
        return pl.BlockSpec((1, tm, BRANCH_W), lambda b, i: (b, i, s))

    def mod():
        return pl.BlockSpec((1, 1, D), lambda b, i: (b, 0, 0))

    def const(shape):
        return pl.BlockSpec(shape, lambda b, i: (0,) * len(shape))

    return pl.pallas_call(
        functools.partial(_merge_kernel, alpha),
        out_shape=(jax.ShapeDtypeStruct((B, T, D), F32),
                   jax.ShapeDtypeStruct((B, T, PACK_W), U32),
                   jax.ShapeDtypeStruct((B, T, PACK_W), U32),
                   jax.ShapeDtypeStruct((4, n), I32),
                   jax.ShapeDtypeStruct((8, n), F32),
                   jax.ShapeDtypeStruct((N_EXPERTS, 128), I32)),
        grid=(B, per_b),
        in_specs=[tile(BRANCH_W), tile(BRANCH_W), sec(sec_gates[0]), sec(sec_gates[1]), tile(D),
                  mod(), mod(), mod(), const((1, D)), const((1, D)),
                  const((BRANCH_W, D)), const((BRANCH_W, D)), const((D, D)),
                  const((ROUTE_ROWS, D)), const((ROUTE_ROWS, 1))],
        out_specs=(tile(D), tile(PACK_W), tile(PACK_W),
                   pl.BlockSpec((4, tm), lambda b, i: (0, b * per_b + i)),
                   pl.BlockSpec((8, tm), lambda b, i: (0, b * per_b + i)),
                   pl.BlockSpec((N_EXPERTS, 128), lambda b, i: (0, 0))),
        scratch_shapes=[pltpu.VMEM((N_EXPERTS, 128), F32), pltpu.VMEM((tm, tm), BF16)],
        compiler_params=_cparams("arbitrary", "arbitrary"),
        name="merge",
    )(r, m, p_lat, p_lat, x, g1, sh2, sc2, lng, lnb, wr, wm, wo, wrt, brt)


def _sc_mesh():
    return plsc.VectorSubcoreMesh(core_axis_name="c", subcore_axis_name="s")


def _sc_scatter2(rows_a, rows_b, idx0, idx1, n_out):
    m, w = rows_a.shape
    out = jax.ShapeDtypeStruct((n_out, w), rows_a.dtype)

    @functools.partial(pl.kernel, out_type=(out, out), mesh=_sc_mesh(), scratch_types=[])
    def k(xa_hbm, xb_hbm, i0_hbm, i1_hbm, oa_hbm, ob_hbm):
        for x_hbm, o_hbm in ((xa_hbm, oa_hbm), (xb_hbm, ob_hbm)):
            def body(x_vmem, i0_vmem, i1_vmem, o_hbm=o_hbm):
                pltpu.sync_copy(x_vmem, o_hbm.at[i0_vmem.at[0]])
                pltpu.sync_copy(x_vmem, o_hbm.at[i1_vmem.at[0]])

            pltpu.emit_pipeline(
                body,
                grid=(m // SC_WIN,),
                in_specs=[pl.BlockSpec((SC_WIN, w), lambda i: (i, 0)),
                          pl.BlockSpec((1, SC_WIN), lambda i: (0, i)),
                          pl.BlockSpec((1, SC_WIN), lambda i: (0, i))],
                out_specs=[],
                core_axis_name=("c", "s"),
                dimension_semantics=(pltpu.PARALLEL,),
            )(x_hbm, i0_hbm, i1_hbm)

    return k(rows_a, rows_b, idx0.reshape(1, m), idx1.reshape(1, m))


def _sc_gather(table_a, table_b, idx):
    m = idx.shape[0]
    w = table_a.shape[1]
    out = jax.ShapeDtypeStruct((m, w), table_a.dtype)

    @functools.partial(pl.kernel, out_type=(out, out), mesh=_sc_mesh(), scratch_types=[])
    def k(ta_hbm, tb_hbm, i_hbm, oa_hbm, ob_hbm):
        for t_hbm, o_hbm in ((ta_hbm, oa_hbm), (tb_hbm, ob_hbm)):
            def body(i_vmem, o_vmem, t_hbm=t_hbm):
                pltpu.sync_copy(t_hbm.at[i_vmem.at[0]], o_vmem)

            pltpu.emit_pipeline(
                body,
                grid=(m // SC_WIN,),
                in_specs=[pl.BlockSpec((1, SC_WIN), lambda i: (0, i))],
                out_specs=[pl.BlockSpec((SC_WIN, w), lambda i: (i, 0))],
                core_axis_name=("c", "s"),
                dimension_semantics=(pltpu.PARALLEL,),
            )(i_hbm, o_hbm)

    return k(table_a, table_b, idx.reshape(1, m))


def _expert_kernel(be_ref, nv_ref, xa_ref, xb_ref, w1f_ref, w3f_ref, w2f_ref, ya_ref, yb_ref,
                   w1_ref, w3_ref, w2_ref):
    j = pl.program_id(0)
    nv = nv_ref[j]

    @pl.when(jnp.logical_or(j == 0, be_ref[j] != be_ref[jnp.maximum(j - 1, 0)]))
    def _():
        w1_ref[0] = w1f_ref[0].astype(BF16)
        w3_ref[0] = w3f_ref[0].astype(BF16)
        w2_ref[0] = w2f_ref[0].astype(BF16)

    @pl.when(nv > 0)
    def _():
        valid = lax.broadcasted_iota(I32, xa_ref.shape, 0) < nv
        zero = jnp.zeros(xa_ref.shape, U32)
        parts = _unpack_pairs(jnp.where(valid, xa_ref[...], zero)) + \
            _unpack_pairs(jnp.where(valid, xb_ref[...], zero))
        x = jnp.concatenate([p.astype(BF16) for p in parts], axis=1)
        h1 = jnp.dot(x, w1_ref[0], preferred_element_type=F32)
        h3 = jnp.dot(x, w3_ref[0], preferred_element_type=F32)
        y = jnp.dot((_silu(h1) * h3).astype(BF16), w2_ref[0], preferred_element_type=F32)
        ya_ref[...] = _pack_pairs(y[:, 0:PACK_W], y[:, PACK_W:2 * PACK_W])
        yb_ref[...] = _pack_pairs(y[:, 2 * PACK_W:3 * PACK_W], y[:, 3 * PACK_W:4 * PACK_W])

    @pl.when(nv == 0)
    def _():
        ya_ref[...] = jnp.zeros_like(ya_ref)
        yb_ref[...] = jnp.zeros_like(yb_ref)


def _experts(block_exp, n_valid, xa, xb, w1, w3, w2):
    n_slots = xa.shape[0]
    n_blocks = n_slots // MOE_BLK
    d, de = w1.shape[1], w1.shape[2]
    slot = pl.BlockSpec((MOE_BLK, PACK_W), lambda j, be, nv: (j, 0))
    grid_spec = pltpu.PrefetchScalarGridSpec(
        num_scalar_prefetch=2,
        grid=(n_blocks,),
        in_specs=[slot, slot,
                  pl.BlockSpec((1, d, de), lambda j, be, nv: (be[j], 0, 0)),
                  pl.BlockSpec((1, d, de), lambda j, be, nv: (be[j], 0, 0)),
                  pl.BlockSpec((1, de, d), lambda j, be, nv: (be[j], 0, 0))],
        out_specs=(slot, slot),
        scratch_shapes=[pltpu.VMEM((1, d, de), BF16), pltpu.VMEM((1, d, de), BF16),
                        pltpu.VMEM((1, de, d), BF16)],
    )
    return pl.pallas_call(
        _expert_kernel,
        out_shape=(jax.ShapeDtypeStruct((n_slots, PACK_W), U32),
                   jax.ShapeDtypeStruct((n_slots, PACK_W), U32)),
        grid_spec=grid_spec,
        compiler_params=_cparams("arbitrary"),
        name="experts",
    )(block_exp, n_valid, xa, xb, w1, w3, w2)


def _final_kernel(alpha, x1_ref, a0_ref, b0_ref, a1_ref, b1_ref, w_ref, g2_ref, lng_ref, lnb_ref, o_ref):
    w = w_ref[...].T
    w0 = w[:, 0:1]
    w1 = w[:, 1:2]
    parts0 = _unpack_pairs(a0_ref[...]) + _unpack_pairs(b0_ref[...])
    parts1 = _unpack_pairs(a1_ref[...]) + _unpack_pairs(b1_ref[...])
    f = jnp.concatenate([w0 * p0 + w1 * p1 for p0, p1 in zip(parts0, parts1)], axis=1)
    o_ref[0] = _layer_norm(alpha * x1_ref[0] + g2_ref[0] * f) * lng_ref[...] + lnb_ref[...]


def _final(alpha, x1, ya, yb, w, g2, lng, lnb):
    B, T, D = x1.shape
    tm = MERGE_TM
    per_b = T // tm
    n_tiles = B * per_b

    def rows(k):
        return pl.BlockSpec((tm, PACK_W), lambda b, i: (k * n_tiles + b * per_b + i, 0))

    return pl.pallas_call(
        functools.partial(_final_kernel, alpha),
        out_shape=jax.ShapeDtypeStruct((B, T, D), F32),
        grid=(B, per_b),
        in_specs=[pl.BlockSpec((1, tm, D), lambda b, i: (b, i, 0)),
                  rows(0), rows(0), rows(1), rows(1),
                  pl.BlockSpec((8, tm), lambda b, i: (0, b * per_b + i)),
                  pl.BlockSpec((1, 1, D), lambda b, i: (b, 0, 0)),
                  pl.BlockSpec((1, D), lambda b, i: (0, 0)),
                  pl.BlockSpec((1, D), lambda b, i: (0, 0))],
        out_specs=pl.BlockSpec((1, tm, D), lambda b, i: (b, i, 0)),
        compiler_params=_cparams("parallel", "parallel"),
        name="final",
    )(x1, ya, yb, ya, yb, w, g2, lng, lnb)


def _rotary_tables(T):
    quarter = HEAD_W // 4
    freqs = ROPE_BASE ** (-jnp.arange(quarter, dtype=F32) / quarter)
    t = jnp.arange(T)
    ang_r = (t // GRID_W).astype(F32)[:, None] * freqs[None, :]
    ang_c = (t % GRID_W).astype(F32)[:, None] * freqs[None, :]
    cos = jnp.concatenate([jnp.cos(ang_r)] * 2 + [jnp.cos(ang_c)] * 2, axis=1)
    sin = jnp.concatenate([-jnp.sin(ang_r), jnp.sin(ang_r), -jnp.sin(ang_c), jnp.sin(ang_c)], axis=1)
    return cos, sin


def _per_head_gates(gt):
    B, _, T = gt.shape
    n_chunks = T // SCAN_L
    gth = gt.reshape(B, N_GK, HEADS, n_chunks, SCAN_L).transpose(0, 2, 1, 3, 4)
    return jnp.pad(gth, ((0, 0), (0, 0), (0, 0), (0, 8 - n_chunks), (0, 0)))


def _table_lookup(table, idx):
    sel = idx[..., None] == jnp.arange(table.shape[0], dtype=idx.dtype)
    return jnp.sum(jnp.where(sel, table, 0), axis=-1)


def kernel(x, c, ctx, c_ctx, w_ada, b_ada, w_in, b_mgate, ml_conv_w, ml_conv_b, ret_decay_logit, w_ret_branch, w_ml_branch, w_out, ln1_g, ln1_b, w_rg, b_rg, w_re, b_re, w_e1, w_e3, w_e2, ln2_g, ln2_b):
    B, T, D = x.shape
    depth = w_ada.shape[0]
    assert depth == 1 and D == BRANCH_W and T % GRID_W == 0
    alpha = (2 * depth) ** 0.25
    n_tok = B * T

    n_rows = -(-(B + 1) // 8) * 8
    cs = jnp.zeros((n_rows, D), F32).at[:B].set(c).at[B].set(c_ctx)
    mod = _ada(cs, w_ada[0], b_ada[0][None, :])
    sh1, sc1, g1, sh2, sc2, g2 = [mod[:B, None, i * D:(i + 1) * D] for i in range(6)]
    csh1 = mod[B, 0 * D:1 * D].reshape(1, 1, D)
    csc1 = mod[B, 1 * D:2 * D].reshape(1, 1, D)

    w = w_in[0]
    sec_w = [w[:, s * BRANCH_W:(s + 1) * BRANCH_W] for s in range(8)]
    g_lo = 8 * BRANCH_W
    w_gate_t = w[:, g_lo:g_lo + N_GATES].T.astype(BF16)
    b_gate = b_mgate[0][:, None]
    sec_w += [w[:, g_lo + N_GATES:g_lo + N_GATES + D], w[:, g_lo + N_GATES + D:]]
    w_lat = jnp.concatenate(sec_w, axis=1).astype(BF16)
    w_ctx = jnp.concatenate([sec_w[1], sec_w[2], sec_w[5], sec_w[6]], axis=1).astype(BF16)
    kinds_lat = ("rot", "rot_scale") + ("plain",) * 8
    kinds_ctx = ("scale", "plain", "plain", "plain")
    p_lat, gt_lat = _proj(x, sh1, sc1, w_lat, w_gate_t, b_gate, kinds_lat, _rotary_tables(T))
    Tc = ctx.shape[1]
    p_ctx, gt_ctx = _proj(ctx.reshape(1, B * Tc, D), csh1, csc1, w_ctx, w_gate_t, b_gate, kinds_ctx)
    p_ctx = p_ctx.reshape(B, Tc, -1)
    gt_ctx = gt_ctx.reshape(N_GATES, B, Tc).transpose(1, 0, 2)

    ret, mls = _scans(ret_decay_logit[0], p_lat, p_ctx, _per_head_gates(gt_lat), _per_head_gates(gt_ctx),
                      ml_conv_w[0], ml_conv_b[0][None, :], (0, 1, 2, 3), (0, 1), (4, 5, 6, 7), (2, 3))

    wrt = jnp.zeros((ROUTE_ROWS, D), F32).at[:N_GROUPS].set(w_rg[0].T).at[8:8 + N_EXPERTS].set(w_re[0].T)
    brt = jnp.zeros((ROUTE_ROWS, 1), F32).at[:N_GROUPS, 0].set(b_rg[0]).at[8:8 + N_EXPERTS, 0].set(b_re[0])
    x1, ua, ub, ri, rw, cnt = _merge(alpha, ret, mls, p_lat, (8, 9), x, g1, sh2, sc2,
                                     ln1_g[0][None, :], ln1_b[0][None, :],
                                     w_ret_branch[0].astype(BF16), w_ml_branch[0].astype(BF16),
                                     w_out[0].astype(BF16), wrt.astype(BF16), brt)

    counts = cnt[:, 0]
    padded = (counts + MOE_BLK - 1) // MOE_BLK * MOE_BLK
    pad_end = jnp.cumsum(padded)
    pad_off = pad_end - padded
    dest = _table_lookup(pad_off, ri[0:2]) + ri[2:4]
    n_blocks = (2 * n_tok) // MOE_BLK + N_EXPERTS
    n_slots = n_blocks * MOE_BLK
    block_start = jnp.arange(n_blocks, dtype=I32) * MOE_BLK
    block_exp = jnp.minimum((block_start[:, None] >= pad_end[None, :]).sum(1), N_EXPERTS - 1).astype(I32)
    n_valid = jnp.clip(_table_lookup(counts, block_exp) - (block_start - _table_lookup(pad_off, block_exp)),
                       0, MOE_BLK).astype(I32)

    xa, xb = _sc_scatter2(ua.reshape(n_tok, PACK_W), ub.reshape(n_tok, PACK_W), dest[0], dest[1], n_slots)
    ya, yb = _experts(block_exp, n_valid, xa, xb, w_e1[0], w_e3[0], w_e2[0])
    ga, gb = _sc_gather(ya, yb, dest.reshape(2 * n_tok))
    return _final(alpha, x1, ga, gb, rw, g2, ln2_g[0][None, :], ln2_b[0][None, :])
```

```python
import functools

import jax
import jax.numpy as jnp
from jax import lax
from jax.experimental import pallas as pl
from jax.experimental.pallas import tpu as pltpu
from jax.experimental.pallas import tpu_sc as plsc

F32 = jnp.float32
BF16 = jnp.bfloat16
U32 = jnp.uint32
I32 = jnp.int32
HIGHEST = lax.Precision.HIGHEST

HEADS = 4
HEAD_W = 256
BRANCH_W = HEADS * HEAD_W
GRID_W = 64
ROPE_BASE = 10000.0
N_GATES = 16
N_GK = N_GATES // HEADS
N_GROUPS = 4
EXP_PER_GROUP = 8
N_EXPERTS = N_GROUPS * EXP_PER_GROUP
LN_EPS = 1e-5
NEG_INF = -1e30
KEY_SCALE = HEAD_W ** -0.5

SCAN_L = 256
CONV_ROWS = 128
PROJ_TM = 2048
PROJ_SUB = 256
MERGE_TM = 512
MOE_BLK = 512
SC_WIN = 128
PACK_W = 256
ROUTE_ROWS = 64
N_TAB = 6
AUG_W = HEAD_W + 128
VMEM_LIMIT = 48 * 1024 * 1024

NT_DIMS = (((1,), (1,)), ((), ()))
TN_DIMS = (((0,), (0,)), ((), ()))


def _cparams(*sem):
    return pltpu.CompilerParams(dimension_semantics=sem, vmem_limit_bytes=VMEM_LIMIT)


def _layer_norm(x):
    mu = jnp.mean(x, axis=-1, keepdims=True)
    xc = x - mu
    var = jnp.mean(xc * xc, axis=-1, keepdims=True)
    return xc * lax.rsqrt(var + LN_EPS)


def _log_sigmoid(x):
    return jnp.minimum(x, 0.0) - jnp.log1p(jnp.exp(-jnp.abs(x)))


def _silu(x):
    return x * jax.nn.sigmoid(x)


def _pack_pairs(hi, lo):
    hb = lax.bitcast_convert_type(hi.astype(BF16).astype(F32), U32)
    lb = lax.bitcast_convert_type(lo.astype(BF16).astype(F32), U32)
    return (hb & jnp.uint32(0xFFFF0000)) | (lb >> 16)


def _unpack_pairs(p):
    hi = lax.bitcast_convert_type(p & jnp.uint32(0xFFFF0000), F32)
    lo = lax.bitcast_convert_type(p << 16, F32)
    return hi, lo


def _split3(x):
    hi = x.astype(BF16).astype(F32)
    r1 = x - hi
    mid = r1.astype(BF16).astype(F32)
    lo = (r1 - mid).astype(BF16).astype(F32)
    return jnp.concatenate([hi, mid, lo], axis=0).astype(BF16)


def _ada_kernel(c_ref, w_ref, b_ref, o_ref):
    s = _silu(c_ref[...])
    o_ref[...] = jnp.dot(s, w_ref[...], precision=HIGHEST, preferred_element_type=F32) + b_ref[...]


def _ada(cs, w, b):
    rows, d = cs.shape
    cols = w.shape[1]
    tn = 1024
    return pl.pallas_call(
        _ada_kernel,
        out_shape=jax.ShapeDtypeStruct((rows, cols), F32),
        grid=(cols // tn,),
        in_specs=[pl.BlockSpec((rows, d), lambda j: (0, 0)),
                  pl.BlockSpec((d, tn), lambda j: (0, j)),
                  pl.BlockSpec((1, tn), lambda j: (0, j))],
        out_specs=pl.BlockSpec((rows, tn), lambda j: (0, j)),
        compiler_params=_cparams("parallel"),
        name="ada",
    )(cs, w, b)


def _proj_kernel(kinds, x_ref, sh_ref, sc_ref, w_ref, wg_ref, bg_ref, *rest):
    if "rot" in kinds or "rot_scale" in kinds:
        cos_ref, sin_ref, o_ref, gt_ref, u_ref = rest
    else:
        o_ref, gt_ref, u_ref = rest
    j = pl.program_id(2)
    tm = x_ref.shape[1]
    sub = min(PROJ_SUB, tm)

    def rotary(acc, rows, scale):
        for s in range(acc.shape[1] // 128):
            a = acc[:, s * 128:(s + 1) * 128]
            half = s % 2
            cs = cos_ref[rows, half * 128:(half + 1) * 128]
            sn = sin_ref[rows, half * 128:(half + 1) * 128]
            r = a * cs + pltpu.roll(a, 64, 1) * sn
            if scale != 1.0:
                r = r * scale
            o_ref[0, rows, s * 128:(s + 1) * 128] = r.astype(BF16)

    def section(kind, first):
        for r in range(tm // sub):
            rows = slice(r * sub, (r + 1) * sub)
            if first:
                u = _layer_norm(x_ref[0, rows, :]) * (1.0 + sc_ref[0]) + sh_ref[0]
                ub = u.astype(BF16)
                u_ref[rows, :] = ub
                gt_ref[0, :, rows] = lax.dot_general(wg_ref[...], ub, NT_DIMS,
                                                     preferred_element_type=F32) + bg_ref[...]
            else:
                ub = u_ref[rows, :]
            acc = jnp.dot(ub, w_ref[...], preferred_element_type=F32)
            if kind == "rot":
                rotary(acc, rows, 1.0)
            elif kind == "rot_scale":
                rotary(acc, rows, KEY_SCALE)
            elif kind == "scale":
                o_ref[0, rows, :] = (acc * KEY_SCALE).astype(BF16)
            else:
                o_ref[0, rows, :] = acc.astype(BF16)

    variants = {}
    for s, kind in enumerate(kinds):
        variants.setdefault((kind, s == 0), []).append(s)
    for (kind, first), secs in variants.items():
        cond = functools.reduce(jnp.logical_or, [j == s for s in secs])

        @pl.when(cond)
        def _(kind=kind, first=first):
            section(kind, first)


def _proj(x, sh, sc, w_main, w_gate_t, b_gate, kinds, tables=None):
    B, T, D = x.shape
    n_sec = len(kinds)
    tm = min(PROJ_TM, T)
    tn = BRANCH_W
    assert T % tm == 0
    in_specs = [
        pl.BlockSpec((1, tm, D), lambda i, b, j: (b, i, 0)),
        pl.BlockSpec((1, 1, D), lambda i, b, j: (b, 0, 0)),
        pl.BlockSpec((1, 1, D), lambda i, b, j: (b, 0, 0)),
        pl.BlockSpec((D, tn), lambda i, b, j: (0, j)),
        pl.BlockSpec((N_GATES, D), lambda i, b, j: (0, 0)),
        pl.BlockSpec((N_GATES, 1), lambda i, b, j: (0, 0)),
    ]
    args = [x, sh, sc, w_main, w_gate_t, b_gate]
    if tables is not None:
        in_specs += [pl.BlockSpec((tm, HEAD_W), lambda i, b, j: (i, 0))] * 2
        args += list(tables)
    return pl.pallas_call(
        functools.partial(_proj_kernel, kinds),
        out_shape=(jax.ShapeDtypeStruct((B, T, n_sec * tn), BF16),
                   jax.ShapeDtypeStruct((B, N_GATES, T), F32)),
        grid=(T // tm, B, n_sec),
        in_specs=in_specs,
        out_specs=(pl.BlockSpec((1, tm, tn), lambda i, b, j: (b, i, j)),
                   pl.BlockSpec((1, N_GATES, tm), lambda i, b, j: (b, 0, i))),
        scratch_shapes=[pltpu.VMEM((tm, D), BF16)],
        compiler_params=_cparams("parallel", "parallel", "arbitrary"),
        name="proj_lat" if tables is not None else "proj_ctx",
    )(*args)


def _ret_build(dl_ref, q_ref, k_ref, v_ref, rg_ref, ck_ref, cv_ref, o_ref,
               sf_ref, sb_ref, fs_ref, bs_ref, dec_ref, d_ref):
    h = pl.program_id(0)
    L = SCAN_L
    n_chunks = q_ref.shape[1] // L
    n_ctx_chunks = ck_ref.shape[1] // L
    lgf = _log_sigmoid(jnp.full((1, 1), dl_ref[0, h], F32))
    lgb = _log_sigmoid(jnp.full((1, 1), dl_ref[1, h], F32))

    @pl.when(pl.program_id(1) == 0)
    def _():
        ri = lax.broadcasted_iota(I32, (L, L), 0)
        ci = lax.broadcasted_iota(I32, (L, L), 1)
        rel = (ri - ci).astype(F32)
        d_ref[...] = jnp.where(rel >= 0.0, jnp.exp(jnp.maximum(rel, 0.0) * lgf),
                               jnp.exp(jnp.maximum(-rel, 0.0) * lgb))
        row = lax.broadcasted_iota(I32, (L, HEAD_W), 0).astype(F32)
        dec_ref[0] = jnp.exp((row + 1.0) * lgf)
        dec_ref[1] = jnp.exp((L - 1.0 - row) * lgf)
        dec_ref[2] = jnp.exp((L - row) * lgb)
        dec_ref[3] = jnp.exp(row * lgb)

    cdf = jnp.exp(L * lgf)
    cdb = jnp.exp(L * lgb)

    def update(s_ref, kc, vc, kd, cd):
        kdec = (kc.astype(F32) * kd).astype(BF16)
        s_ref[...] = s_ref[...] * cd + lax.dot_general(kdec, vc, TN_DIMS, preferred_element_type=F32)

    sf_ref[...] = jnp.zeros_like(sf_ref)
    sb_ref[...] = jnp.zeros_like(sb_ref)
    for c in range(n_ctx_chunks):
        update(sf_ref, ck_ref[0, c * L:(c + 1) * L, :], cv_ref[0, c * L:(c + 1) * L, :], dec_ref[1], cdf)
    for c in reversed(range(n_ctx_chunks)):
        update(sb_ref, ck_ref[0, c * L:(c + 1) * L, :], cv_ref[0, c * L:(c + 1) * L, :], dec_ref[3], cdb)

    def state_pass(i, carry):
        cb = n_chunks - 1 - i
        rf = pl.multiple_of(i * L, L)
        rb = pl.multiple_of(cb * L, L)
        fs_ref[i] = sf_ref[...].astype(BF16)
        bs_ref[cb] = sb_ref[...].astype(BF16)
        update(sf_ref, k_ref[0, pl.ds(rf, L), :], v_ref[0, pl.ds(rf, L), :], dec_ref[1], cdf)
        update(sb_ref, k_ref[0, pl.ds(rb, L), :], v_ref[0, pl.ds(rb, L), :], dec_ref[3], cdb)
        return carry

    def finish_states():
        fs_ref[n_chunks - 1] = sf_ref[...].astype(BF16)
        bs_ref[0] = sb_ref[...].astype(BF16)

    def out_chunk(c):
        r0 = pl.multiple_of(c * L, L)
        q = q_ref[0, pl.ds(r0, L), :]
        k = k_ref[0, pl.ds(r0, L), :]
        v = v_ref[0, pl.ds(r0, L), :]
        s = lax.dot_general(q, k, NT_DIMS, preferred_element_type=F32)
        att = (s * d_ref[...]).astype(BF16)
        o = jnp.dot(att, v, preferred_element_type=F32)
        o = o + jnp.dot(q, fs_ref[c], preferred_element_type=F32) * dec_ref[0]
        o = o + jnp.dot(q, bs_ref[c], preferred_element_type=F32) * dec_ref[2]
        rg = rg_ref[0, pl.ds(r0, L), :].astype(F32)
        o_ref[0, pl.ds(r0, L), :] = (_layer_norm(o) * _silu(rg)).astype(BF16)

    return state_pass, finish_states, out_chunk


def _ret_scratch(n_chunks):
    return [pltpu.VMEM((HEAD_W, HEAD_W), F32),
            pltpu.VMEM((HEAD_W, HEAD_W), F32),
            pltpu.VMEM((n_chunks, HEAD_W, HEAD_W), BF16),
            pltpu.VMEM((n_chunks, HEAD_W, HEAD_W), BF16),
            pltpu.VMEM((4, SCAN_L, HEAD_W), F32),
            pltpu.VMEM((SCAN_L, SCAN_L), F32)]


def _mlstm_build(qp_ref, kp_ref, v_ref, mo_ref, ckp_ref, cv_ref, gt_ref, cgt_ref,
                 wq_ref, bq_ref, wk_ref, bk_ref, o_ref,
                 tab_ref, row_ref,
                 cf_ref, mf_ref, cb_ref, mb_ref, cfs_ref, mfs_ref, cbs_ref, mbs_ref, mask_ref,
                 xf_ref, q_ref, k_ref, ck_ref):
    L = SCAN_L
    T = qp_ref.shape[1]
    Tc = ckp_ref.shape[1]
    n_chunks = T // L
    n_ctx_chunks = Tc // L
    CV = CONV_ROWS

    def conv_stage(src_ref, t_len):
        xf_ref[pl.ds(0, 8), :] = jnp.zeros((8, HEAD_W), F32)
        xf_ref[pl.ds(8 + t_len, 8), :] = jnp.zeros((8, HEAD_W), F32)
        xf_ref[pl.ds(8, t_len), :] = src_ref[0].astype(F32)

    def conv_chunk(c, w, b, dst_ref, scale):
        r0 = pl.multiple_of(c * CV, CV)
        win = xf_ref[pl.ds(r0, CV + 16), :]
        prev = pltpu.roll(win, 1, 0)[8:8 + CV, :]
        cur = win[8:8 + CV, :]
        nxt = pltpu.roll(win, CV + 15, 0)[8:8 + CV, :]
        y = _silu(prev * w[0:1, :] + cur * w[1:2, :] + nxt * w[2:3, :] + b)
        if scale != 1.0:
            y = y * scale
        dst_ref[pl.ds(r0, CV), :] = y.astype(BF16)

    def conv_all(src_ref, w_ref, b_ref, dst_ref, t_len, scale):
        conv_stage(src_ref, t_len)
        w = w_ref[...]
        b = b_ref[...]

        def body(c, carry):
            conv_chunk(c, w, b, dst_ref, scale)
            return carry

        lax.fori_loop(0, t_len // CV, body, 0)

    conv_all(kp_ref, wk_ref, bk_ref, k_ref, T, KEY_SCALE)
    conv_all(ckp_ref, wk_ref, bk_ref, ck_ref, Tc, KEY_SCALE)
    conv_stage(qp_ref, T)
    q_per_chunk = L // CV

    def conv_q(c):
        for u in range(q_per_chunk):
            conv_chunk(c * q_per_chunk + u, wq_ref[...], bq_ref[...], q_ref, 1.0)

    ri = lax.broadcasted_iota(I32, (L, L), 0)
    ci = lax.broadcasted_iota(I32, (L, L), 1)
    tri_u = (ri <= ci).astype(BF16)
    lane8 = lax.broadcasted_iota(I32, (8, L), 1)
    sub8 = lax.broadcasted_iota(I32, (8, L), 0)
    sel_r = lax.broadcasted_iota(I32, (24, 8 * 128), 0) % 8
    sel_c = lax.broadcasted_iota(I32, (24, 8 * 128), 1) // 128
    sel3 = (sel_r == sel_c).astype(BF16)
    ones_cols = jnp.ones((L, AUG_W - HEAD_W), BF16)

    def chunk_tables(g8, n_used, state_only):
        i_f, i_b = g8[0], g8[2]
        lf_f, lf_b = _log_sigmoid(g8[1]), _log_sigmoid(g8[3])
        cs3 = jnp.dot(_split3(jnp.concatenate([lf_f, lf_b], axis=0)), tri_u,
                      preferred_element_type=F32)
        cs = cs3[0:16] + cs3[16:32] + cs3[32:48]
        b_f = cs[0:8]
        b_b = cs[8:16, L - 1:L] - cs[8:16] + lf_b
        z_f = i_f - b_f
        z_b = i_b - b_b
        g_f = b_f[:, L - 1:L] - b_f + i_f
        g_b = b_b[:, 0:1] - b_b + i_b
        mf, mb = z_f, z_b
        s = 1
        while s < L:
            mf = jnp.maximum(mf, jnp.where(lane8 >= s, pltpu.roll(mf, s, 1), NEG_INF))
            mb = jnp.maximum(mb, jnp.where(lane8 < L - s, pltpu.roll(mb, L - s, 1), NEG_INF))
            s *= 2
        mb = jnp.where(lane8 < L - 1, pltpu.roll(mb, L - 1, 1), NEG_INF)
        reps = [None if state_only and t not in (2, 5) else
                lax.dot_general(_split3(val), sel3[:, 0:n_used * 128], TN_DIMS, preferred_element_type=F32)
                for t, val in enumerate((mf, b_f, g_f, mb, b_b, g_b))]

        def rows_of(c):
            out = jnp.zeros((8, L), F32)
            for r, val in enumerate((z_f, z_b, g_f, g_b, b_f, b_b)):
                out = jnp.where(sub8 == r, val[c:c + 1], out)
            return out

        return rows_of, reps

    lat_rows, lat_reps = chunk_tables(gt_ref[0, 0], n_chunks, False)
    for c in range(n_chunks):
        row_ref[c] = lat_rows(c)
        for t in range(N_TAB):
            tab_ref[t, c * L:(c + 1) * L, :] = lat_reps[t][:, c * 128:(c + 1) * 128]

    def lanes2(x):
        return jnp.concatenate([x, x], axis=1)

    def advance(k, v, g_rep, g_row, b_last, c_ref, m_ref):
        m = m_ref[...]
        m_new = jnp.maximum(b_last + m, jnp.max(g_row, axis=-1, keepdims=True))
        kw = (k.astype(F32) * jnp.exp(lanes2(g_rep) - m_new)).astype(BF16)
        v_aug = jnp.concatenate([v, ones_cols], axis=1)
        c_ref[...] = jnp.exp(b_last + m - m_new) * c_ref[...] + lax.dot_general(
            kw, v_aug, TN_DIMS, preferred_element_type=F32)
        m_ref[...] = m_new

    for r in (cf_ref, mf_ref, cb_ref, mb_ref):
        r[...] = jnp.zeros_like(r)
    ctx_rows, ctx_reps = chunk_tables(cgt_ref[0, 0], n_ctx_chunks, True)
    for c in range(n_ctx_chunks):
        rows = ctx_rows(c)
        advance(ck_ref[c * L:(c + 1) * L, :], cv_ref[0, c * L:(c + 1) * L, :],
                ctx_reps[2][:, c * 128:(c + 1) * 128], rows[2:3], rows[4:5, L - 1:L], cf_ref, mf_ref)
    for c in reversed(range(n_ctx_chunks)):
        rows = ctx_rows(c)
        advance(ck_ref[c * L:(c + 1) * L, :], cv_ref[0, c * L:(c + 1) * L, :],
                ctx_reps[5][:, c * 128:(c + 1) * 128], rows[3:4], rows[5:6, 0:1], cb_ref, mb_ref)

    def state_pass(i, carry):
        cb = n_chunks - 1 - i
        rf = pl.multiple_of(i * L, L)
        rb = pl.multiple_of(cb * L, L)
        cfs_ref[i] = cf_ref[...].astype(BF16)
        mfs_ref[i] = mf_ref[...]
        cbs_ref[cb] = cb_ref[...].astype(BF16)
        mbs_ref[cb] = mb_ref[...]
        rows_f = row_ref[i]
        rows_b = row_ref[cb]
        advance(k_ref[pl.ds(rf, L), :], v_ref[0, pl.ds(rf, L), :], tab_ref[2, pl.ds(rf, L), :],
                rows_f[2:3], rows_f[4:5, L - 1:L], cf_ref, mf_ref)
        advance(k_ref[pl.ds(rb, L), :], v_ref[0, pl.ds(rb, L), :], tab_ref[5, pl.ds(rb, L), :],
                rows_b[3:4], rows_b[5:6, 0:1], cb_ref, mb_ref)
        conv_q(i)
        return carry

    def finish_states():
        conv_q(n_chunks - 1)
        cfs_ref[n_chunks - 1] = cf_ref[...].astype(BF16)
        mfs_ref[n_chunks - 1] = mf_ref[...]
        cbs_ref[0] = cb_ref[...].astype(BF16)
        mbs_ref[0] = mb_ref[...]

    def direction(q, v_aug, s, z_row, zmax_rep, b_rep, mask, c_in, m_in):
        mx = jnp.maximum(zmax_rep, m_in)
        att = s * jnp.exp((z_row - lanes2(mx)) + mask)
        na = jnp.dot(att.astype(BF16), v_aug, preferred_element_type=F32)
        qa = jnp.dot(q, c_in, preferred_element_type=F32)
        a = jnp.exp(m_in - mx)
        num = na[:, 0:HEAD_W] + lanes2(a) * qa[:, 0:HEAD_W]
        den = na[:, HEAD_W:] + a * qa[:, HEAD_W:]
        scale = 1.0 / jnp.maximum(jnp.abs(den), jnp.exp(-(b_rep + mx)))
        return num * lanes2(scale)

    @pl.when(pl.program_id(1) == 0)
    def _():
        mask_ref[0] = jnp.where(ci <= ri, 0.0, NEG_INF)
        mask_ref[1] = jnp.where(ci > ri, 0.0, NEG_INF)

    def out_chunk(c):
        r0 = pl.multiple_of(c * L, L)
        q = q_ref[pl.ds(r0, L), :]
        k = k_ref[pl.ds(r0, L), :]
        v_aug = jnp.concatenate([v_ref[0, pl.ds(r0, L), :], ones_cols], axis=1)
        s = lax.dot_general(q, k, NT_DIMS, preferred_element_type=F32)
        rows = row_ref[c]
        tot = direction(q, v_aug, s, rows[0:1], tab_ref[0, pl.ds(r0, L), :], tab_ref[1, pl.ds(r0, L), :],
                        mask_ref[0], cfs_ref[c], mfs_ref[c])
        tot = tot + direction(q, v_aug, s, rows[1:2], tab_ref[3, pl.ds(r0, L), :],
                              tab_ref[4, pl.ds(r0, L), :], mask_ref[1], cbs_ref[c], mbs_ref[c])
        mo = mo_ref[0, pl.ds(r0, L), :].astype(F32)
        o_ref[0, pl.ds(r0, L), :] = (_layer_norm(tot) * jax.nn.sigmoid(mo)).astype(BF16)

    return state_pass, finish_states, out_chunk


def _mlstm_scratch(T, Tc, n_chunks):
    state = [pltpu.VMEM((HEAD_W, AUG_W), F32), pltpu.VMEM((1, 1), F32)]
    snaps = [pltpu.VMEM((n_chunks, HEAD_W, AUG_W), BF16), pltpu.VMEM((n_chunks, 1, 1), F32)]
    return [pltpu.VMEM((N_TAB, T, 128), F32), pltpu.VMEM((n_chunks, 8, SCAN_L), F32)] \
        + state + state + snaps + snaps + [pltpu.VMEM((2, SCAN_L, SCAN_L), F32)] \
        + [pltpu.VMEM((T + 16, HEAD_W), F32), pltpu.VMEM((T, HEAD_W), BF16),
           pltpu.VMEM((T, HEAD_W), BF16), pltpu.VMEM((Tc, HEAD_W), BF16)]


def _scan_kernel(n_ret_scratch, dl_ref, rq_ref, rk_ref, rv_ref, rg_ref, rck_ref, rcv_ref,
                 mq_ref, mk_ref, mv_ref, mo_ref, mck_ref, mcv_ref, gt_ref, cgt_ref,
                 wq_ref, bq_ref, wk_ref, bk_ref, r_ref, m_ref, *scratch):
    n_chunks = rq_ref.shape[1] // SCAN_L
    ret = _ret_build(dl_ref, rq_ref, rk_ref, rv_ref, rg_ref, rck_ref, rcv_ref, r_ref,
                     *scratch[:n_ret_scratch])
    mls = _mlstm_build(mq_ref, mk_ref, mv_ref, mo_ref, mck_ref, mcv_ref, gt_ref, cgt_ref,
                       wq_ref, bq_ref, wk_ref, bk_ref, m_ref, *scratch[n_ret_scratch:])

    def state_pass(i, carry):
        ret[0](i, carry)
        mls[0](i, carry)
        return carry

    lax.fori_loop(0, n_chunks - 1, state_pass, 0)
    ret[1]()
    mls[1]()

    def out_pass(i, carry):
        for c in (2 * i, 2 * i + 1):
            ret[2](c)
            mls[2](c)
        return carry

    lax.fori_loop(0, n_chunks // 2, out_pass, 0)


def _scans(decay_logit, p_lat, p_ctx, gt, cgt, conv_w, conv_b, ret_lat, ret_ctx, ml_lat, ml_ctx):
    B, T, _ = p_lat.shape
    Tc = p_ctx.shape[1]
    assert T % (2 * SCAN_L) == 0 and Tc % SCAN_L == 0 and T // SCAN_L <= 8
    n_chunks = T // SCAN_L

    def lat(sec):
        return pl.BlockSpec((1, T, HEAD_W), lambda h, b: (b, 0, sec * HEADS + h))

    def cx(sec):
        return pl.BlockSpec((1, Tc, HEAD_W), lambda h, b: (b, 0, sec * HEADS + h))

    gates = pl.BlockSpec((1, 1, N_GK, 8, SCAN_L), lambda h, b: (b, h, 0, 0, 0))
    out = pl.BlockSpec((1, T, HEAD_W), lambda h, b: (b, 0, h))
    conv_specs = [pl.BlockSpec((3, HEAD_W), lambda h, b: (0, h)),
                  pl.BlockSpec((1, HEAD_W), lambda h, b: (0, h)),
                  pl.BlockSpec((3, HEAD_W), lambda h, b: (0, HEADS + h)),
                  pl.BlockSpec((1, HEAD_W), lambda h, b: (0, HEADS + h))]
    ret_scratch = _ret_scratch(n_chunks)
    return pl.pallas_call(
        functools.partial(_scan_kernel, len(ret_scratch)),
        out_shape=(jax.ShapeDtypeStruct((B, T, BRANCH_W), BF16),
                   jax.ShapeDtypeStruct((B, T, BRANCH_W), BF16)),
        grid=(HEADS, B),
        in_specs=[pl.BlockSpec(memory_space=pltpu.SMEM)]
        + [lat(s) for s in ret_lat] + [cx(s) for s in ret_ctx]
        + [lat(s) for s in ml_lat] + [cx(s) for s in ml_ctx] + [gates, gates] + conv_specs,
        out_specs=(out, out),
        scratch_shapes=ret_scratch + _mlstm_scratch(T, Tc, n_chunks),
        compiler_params=_cparams("arbitrary", "arbitrary"),
        name="scans",
    )(decay_logit, *([p_lat] * 4), *([p_ctx] * 2), *([p_lat] * 4), *([p_ctx] * 2), gt, cgt,
      conv_w, conv_b, conv_w, conv_b)


def _merge_kernel(alpha, r_ref, m_ref, gr_ref, gm_ref, x_ref, g1_ref, sh2_ref, sc2_ref,
                  lng_ref, lnb_ref, wr_ref, wm_ref, wo_ref, wrt_ref, brt_ref,
                  x1_ref, ua_ref, ub_ref, ri_ref, rw_ref, cnt_ref, carry_ref, u_ref):
    @pl.when(jnp.logical_and(pl.program_id(0) == 0, pl.program_id(1) == 0))
    def _():
        carry_ref[...] = jnp.zeros_like(carry_ref)
        tm = x_ref.shape[1]
        r = lax.broadcasted_iota(I32, (tm, tm), 0)
        c = lax.broadcasted_iota(I32, (tm, tm), 1)
        u_ref[...] = (r < c).astype(BF16)

    yr = jnp.dot(r_ref[0], wr_ref[...], preferred_element_type=F32)
    ym = jnp.dot(m_ref[0], wm_ref[...], preferred_element_type=F32)
    y = jax.nn.sigmoid(gr_ref[0].astype(F32)) * yr + jax.nn.sigmoid(gm_ref[0].astype(F32)) * ym
    yo = jnp.dot(y.astype(BF16), wo_ref[...], preferred_element_type=F32)
    x1 = _layer_norm(alpha * x_ref[0] + g1_ref[0] * yo) * lng_ref[...] + lnb_ref[...]
    x1_ref[0] = x1
    u2 = _layer_norm(x1) * (1.0 + sc2_ref[0]) + sh2_ref[0]
    ua_ref[0] = _pack_pairs(u2[:, 0:PACK_W], u2[:, PACK_W:2 * PACK_W])
    ub_ref[0] = _pack_pairs(u2[:, 2 * PACK_W:3 * PACK_W], u2[:, 3 * PACK_W:4 * PACK_W])
    lt = lax.dot_general(wrt_ref[...], u2.astype(BF16), NT_DIMS, preferred_element_type=F32) + brt_ref[...]
    _route_tile(lt, ri_ref, rw_ref, cnt_ref, carry_ref, u_ref)


def _route_tile(lt, ri_ref, rw_ref, cnt_ref, carry_ref, u_ref):
    tm = lt.shape[1]
    lg = lt[0:N_GROUPS, :]
    eg = jnp.exp(lg - jnp.max(lg, axis=0, keepdims=True))
    pg = eg / jnp.sum(eg, axis=0, keepdims=True)
    pg_top = jnp.max(pg, axis=0, keepdims=True)
    rows_g = lax.broadcasted_iota(I32, pg.shape, 0)
    g_idx = jnp.min(jnp.where(pg == pg_top, rows_g, N_GROUPS), axis=0, keepdims=True)

    le = jnp.zeros((EXP_PER_GROUP, tm), F32)
    for g in range(N_GROUPS):
        lo = 8 + g * EXP_PER_GROUP
        le = jnp.where(g_idx == g, lt[lo:lo + EXP_PER_GROUP, :], le)
    ee = jnp.exp(le - jnp.max(le, axis=0, keepdims=True))
    pe = ee / jnp.sum(ee, axis=0, keepdims=True)
    rows_e = lax.broadcasted_iota(I32, pe.shape, 0)
    v1 = jnp.max(pe, axis=0, keepdims=True)
    i1 = jnp.min(jnp.where(pe == v1, rows_e, EXP_PER_GROUP), axis=0, keepdims=True)
    pe2 = jnp.where(rows_e == i1, -1.0, pe)
    v2 = jnp.max(pe2, axis=0, keepdims=True)
    i2 = jnp.min(jnp.where(pe2 == v2, rows_e, EXP_PER_GROUP), axis=0, keepdims=True)
    den = v1 + v2
    rw_ref[...] = jnp.zeros_like(rw_ref)
    rw_ref[0:1, :] = pg_top * v1 / den
    rw_ref[1:2, :] = pg_top * v2 / den
    e1 = g_idx * EXP_PER_GROUP + i1
    e2 = g_idx * EXP_PER_GROUP + i2

    rows_x = lax.broadcasted_iota(I32, (N_EXPERTS, tm), 0)
    oh1 = (rows_x == e1).astype(F32)
    oh2 = (rows_x == e2).astype(F32)
    both = oh1 + oh2
    before = carry_ref[:, 0:1] + jnp.dot(both.astype(BF16), u_ref[...], preferred_element_type=F32)
    ri_ref[0:1, :] = e1
    ri_ref[1:2, :] = e2
    ri_ref[2:3, :] = jnp.sum(oh1 * before, axis=0, keepdims=True).astype(I32)
    ri_ref[3:4, :] = jnp.sum(oh2 * before, axis=0, keepdims=True).astype(I32)
    carry_ref[...] = carry_ref[...] + jnp.sum(both, axis=1, keepdims=True)
    cnt_ref[...] = carry_ref[...].astype(I32)


def _merge(alpha, r, m, p_lat, sec_gates, x, g1, sh2, sc2, lng, lnb, wr, wm, wo, wrt, brt):
    B, T, D = x.shape
    tm = MERGE_TM
    per_b = T // tm
    n = B * T

    def tile(w):
        return pl.BlockSpec((1, tm, w), lambda b, i: (b, i, 0))

    def sec(s):
        return pl.BlockSpec((1, tm, BRANCH_W), lambda b, i: (b, i, s))

    def mod():
        return pl.BlockSpec((1, 1, D), lambda b, i: (b, 0, 0))

    def const(shape):
        return pl.BlockSpec(shape, lambda b, i: (0,) * len(shape))

    return pl.pallas_call(
        functools.partial(_merge_kernel, alpha),
        out_shape=(jax.ShapeDtypeStruct((B, T, D), F32),
                   jax.ShapeDtypeStruct((B, T, PACK_W), U32),
                   jax.ShapeDtypeStruct((B, T, PACK_W), U32),
                   jax.ShapeDtypeStruct((4, n), I32),
                   jax.ShapeDtypeStruct((8, n), F32),
                   jax.ShapeDtypeStruct((N_EXPERTS, 128), I32)),
        grid=(B, per_b),
        in_specs=[tile(BRANCH_W), tile(BRANCH_W), sec(sec_gates[0]), sec(sec_gates[1]), tile(D),
                  mod(), mod(), mod(), const((1, D)), const((1, D)),
                  const((BRANCH_W, D)), const((BRANCH_W, D)), const((D, D)),
                  const((ROUTE_ROWS, D)), const((ROUTE_ROWS, 1))],
        out_specs=(tile(D), tile(PACK_W), tile(PACK_W),
                   pl.BlockSpec((4, tm), lambda b, i: (0, b * per_b + i)),
                   pl.BlockSpec((8, tm), lambda b, i: (0, b * per_b + i)),
                   pl.BlockSpec((N_EXPERTS, 128), lambda b, i: (0, 0))),
        scratch_shapes=[pltpu.VMEM((N_EXPERTS, 128), F32), pltpu.VMEM((tm, tm), BF16)],
        compiler_params=_cparams("arbitrary", "arbitrary"),
        name="merge",
    )(r, m, p_lat, p_lat, x, g1, sh2, sc2, lng, lnb, wr, wm, wo, wrt, brt)


def _sc_mesh():
    return plsc.VectorSubcoreMesh(core_axis_name="c", subcore_axis_name="s")


def _sc_scatter2(rows_a, rows_b, idx0, idx1, n_out):
    m, w = rows_a.shape
    out = jax.ShapeDtypeStruct((n_out, w), rows_a.dtype)

    @functools.partial(pl.kernel, out_type=(out, out), mesh=_sc_mesh(), scratch_types=[])
    def k(xa_hbm, xb_hbm, i0_hbm, i1_hbm, oa_hbm, ob_hbm):
        for x_hbm, o_hbm in ((xa_hbm, oa_hbm), (xb_hbm, ob_hbm)):
            def body(x_vmem, i0_vmem, i1_vmem, o_hbm=o_hbm):
                pltpu.sync_copy(x_vmem, o_hbm.at[i0_vmem.at[0]])
                pltpu.sync_copy(x_vmem, o_hbm.at[i1_vmem.at[0]])

            pltpu.emit_pipeline(
                body,
                grid=(m // SC_WIN,),
                in_specs=[pl.BlockSpec((SC_WIN, w), lambda i: (i, 0)),
                          pl.BlockSpec((1, SC_WIN), lambda i: (0, i)),
                          pl.BlockSpec((1, SC_WIN), lambda i: (0, i))],
                out_specs=[],
                core_axis_name=("c", "s"),
                dimension_semantics=(pltpu.PARALLEL,),
            )(x_hbm, i0_hbm, i1_hbm)

    return k(rows_a, rows_b, idx0.reshape(1, m), idx1.reshape(1, m))


def _sc_gather(table_a, table_b, idx):
    m = idx.shape[0]
    w = table_a.shape[1]
    out = jax.ShapeDtypeStruct((m, w), table_a.dtype)

    @functools.partial(pl.kernel, out_type=(out, out), mesh=_sc_mesh(), scratch_types=[])
    def k(ta_hbm, tb_hbm, i_hbm, oa_hbm, ob_hbm):
        for t_hbm, o_hbm in ((ta_hbm, oa_hbm), (tb_hbm, ob_hbm)):
            def body(i_vmem, o_vmem, t_hbm=t_hbm):
                pltpu.sync_copy(t_hbm.at[i_vmem.at[0]], o_vmem)

            pltpu.emit_pipeline(
                body,
                grid=(m // SC_WIN,),
                in_specs=[pl.BlockSpec((1, SC_WIN), lambda i: (0, i))],
                out_specs=[pl.BlockSpec((SC_WIN, w), lambda i: (i, 0))],
                core_axis_name=("c", "s"),
                dimension_semantics=(pltpu.PARALLEL,),
            )(i_hbm, o_hbm)

    return k(table_a, table_b, idx.reshape(1, m))


def _expert_kernel(be_ref, nv_ref, xa_ref, xb_ref, w1f_ref, w3f_ref, w2f_ref, ya_ref, yb_ref,
                   w1_ref, w3_ref, w2_ref):
    j = pl.program_id(0)
    nv = nv_ref[j]

    @pl.when(jnp.logical_or(j == 0, be_ref[j] != be_ref[jnp.maximum(j - 1, 0)]))
    def _():
        w1_ref[0] = w1f_ref[0].astype(BF16)
        w3_ref[0] = w3f_ref[0].astype(BF16)
        w2_ref[0] = w2f_ref[0].astype(BF16)

    @pl.when(nv > 0)
    def _():
        valid = lax.broadcasted_iota(I32, xa_ref.shape, 0) < nv
        zero = jnp.zeros(xa_ref.shape, U32)
        parts = _unpack_pairs(jnp.where(valid, xa_ref[...], zero)) + \
            _unpack_pairs(jnp.where(valid, xb_ref[...], zero))
        x = jnp.concatenate([p.astype(BF16) for p in parts], axis=1)
        h1 = jnp.dot(x, w1_ref[0], preferred_element_type=F32)
        h3 = jnp.dot(x, w3_ref[0], preferred_element_type=F32)
        y = jnp.dot((_silu(h1) * h3).astype(BF16), w2_ref[0], preferred_element_type=F32)
        ya_ref[...] = _pack_pairs(y[:, 0:PACK_W], y[:, PACK_W:2 * PACK_W])
        yb_ref[...] = _pack_pairs(y[:, 2 * PACK_W:3 * PACK_W], y[:, 3 * PACK_W:4 * PACK_W])

    @pl.when(nv == 0)
    def _():
        ya_ref[...] = jnp.zeros_like(ya_ref)
        yb_ref[...] = jnp.zeros_like(yb_ref)


def _experts(block_exp, n_valid, xa, xb, w1, w3, w2):
    n_slots = xa.shape[0]
    n_blocks = n_slots // MOE_BLK
    d, de = w1.shape[1], w1.shape[2]
    slot = pl.BlockSpec((MOE_BLK, PACK_W), lambda j, be, nv: (j, 0))
    grid_spec = pltpu.PrefetchScalarGridSpec(
        num_scalar_prefetch=2,
        grid=(n_blocks,),
        in_specs=[slot, slot,
                  pl.BlockSpec((1, d, de), lambda j, be, nv: (be[j], 0, 0)),
                  pl.BlockSpec((1, d, de), lambda j, be, nv: (be[j], 0, 0)),
                  pl.BlockSpec((1, de, d), lambda j, be, nv: (be[j], 0, 0))],
        out_specs=(slot, slot),
        scratch_shapes=[pltpu.VMEM((1, d, de), BF16), pltpu.VMEM((1, d, de), BF16),
                        pltpu.VMEM((1, de, d), BF16)],
    )
    return pl.pallas_call(
        _expert_kernel,
        out_shape=(jax.ShapeDtypeStruct((n_slots, PACK_W), U32),
                   jax.ShapeDtypeStruct((n_slots, PACK_W), U32)),
        grid_spec=grid_spec,
        compiler_params=_cparams("arbitrary"),
        name="experts",
    )(block_exp, n_valid, xa, xb, w1, w3, w2)


def _final_kernel(alpha, x1_ref, a0_ref, b0_ref, a1_ref, b1_ref, w_ref, g2_ref, lng_ref, lnb_ref, o_ref):
    w = w_ref[...].T
    w0 = w[:, 0:1]
    w1 = w[:, 1:2]
    parts0 = _unpack_pairs(a0_ref[...]) + _unpack_pairs(b0_ref[...])
    parts1 = _unpack_pairs(a1_ref[...]) + _unpack_pairs(b1_ref[...])
    f = jnp.concatenate([w0 * p0 + w1 * p1 for p0, p1 in zip(parts0, parts1)], axis=1)
    o_ref[0] = _layer_norm(alpha * x1_ref[0] + g2_ref[0] * f) * lng_ref[...] + lnb_ref[...]


def _final(alpha, x1, ya, yb, w, g2, lng, lnb):
    B, T, D = x1.shape
    tm = MERGE_TM
    per_b = T // tm
    n_tiles = B * per_b

    def rows(k):
        return pl.BlockSpec((tm, PACK_W), lambda b, i: (k * n_tiles + b * per_b + i, 0))

    return pl.pallas_call(
        functools.partial(_final_kernel, alpha),
        out_shape=jax.ShapeDtypeStruct((B, T, D), F32),
        grid=(B, per_b),
        in_specs=[pl.BlockSpec((1, tm, D), lambda b, i: (b, i, 0)),
                  rows(0), rows(0), rows(1), rows(1),
                  pl.BlockSpec((8, tm), lambda b, i: (0, b * per_b + i)),
                  pl.BlockSpec((1, 1, D), lambda b, i: (b, 0, 0)),
                  pl.BlockSpec((1, D), lambda b, i: (0, 0)),
                  pl.BlockSpec((1, D), lambda b, i: (0, 0))],
        out_specs=pl.BlockSpec((1, tm, D), lambda b, i: (b, i, 0)),
        compiler_params=_cparams("parallel", "parallel"),
        name="final",
    )(x1, ya, yb, ya, yb, w, g2, lng, lnb)


def _rotary_tables(T):
    quarter = HEAD_W // 4
    freqs = ROPE_BASE ** (-jnp.arange(quarter, dtype=F32) / quarter)
    t = jnp.arange(T)
    ang_r = (t // GRID_W).astype(F32)[:, None] * freqs[None, :]
    ang_c = (t % GRID_W).astype(F32)[:, None] * freqs[None, :]
    cos = jnp.concatenate([jnp.cos(ang_r)] * 2 + [jnp.cos(ang_c)] * 2, axis=1)
    sin = jnp.concatenate([-jnp.sin(ang_r), jnp.sin(ang_r), -jnp.sin(ang_c), jnp.sin(ang_c)], axis=1)
    return cos, sin


def _per_head_gates(gt):
    B, _, T = gt.shape
    n_chunks = T // SCAN_L
    gth = gt.reshape(B, N_GK, HEADS, n_chunks, SCAN_L).transpose(0, 2, 1, 3, 4)
    return jnp.pad(gth, ((0, 0), (0, 0), (0, 0), (0, 8 - n_chunks), (0, 0)))


def _table_lookup(table, idx):
    sel = idx[..., None] == jnp.arange(table.shape[0], dtype=idx.dtype)
    return jnp.sum(jnp.where(sel, table, 0), axis=-1)


def kernel(x, c, ctx, c_ctx, w_ada, b_ada, w_in, b_mgate, ml_conv_w, ml_conv_b, ret_decay_logit, w_ret_branch, w_ml_branch, w_out, ln1_g, ln1_b, w_rg, b_rg, w_re, b_re, w_e1, w_e3, w_e2, ln2_g, ln2_b):
    B, T, D = x.shape
    depth = w_ada.shape[0]
    assert depth == 1 and D == BRANCH_W and T % GRID_W == 0
    alpha = (2 * depth) ** 0.25
    n_tok = B * T

    n_rows = -(-(B + 1) // 8) * 8
    cs = jnp.zeros((n_rows, D), F32).at[:B].set(c).at[B].set(c_ctx)
    mod = _ada(cs, w_ada[0], b_ada[0][None, :])
    sh1, sc1, g1, sh2, sc2, g2 = [mod[:B, None, i * D:(i + 1) * D] for i in range(6)]
    csh1 = mod[B, 0 * D:1 * D].reshape(1, 1, D)
    csc1 = mod[B, 1 * D:2 * D].reshape(1, 1, D)

    w = w_in[0]
    sec_w = [w[:, s * BRANCH_W:(s + 1) * BRANCH_W] for s in range(8)]
    g_lo = 8 * BRANCH_W
    w_gate_t = w[:, g_lo:g_lo + N_GATES].T.astype(BF16)
    b_gate = b_mgate[0][:, None]
    sec_w += [w[:, g_lo + N_GATES:g_lo + N_GATES + D], w[:, g_lo + N_GATES + D:]]
    w_lat = jnp.concatenate(sec_w, axis=1).astype(BF16)
    w_ctx = jnp.concatenate([sec_w[1], sec_w[2], sec_w[5], sec_w[6]], axis=1).astype(BF16)
    kinds_lat = ("rot", "rot_scale") + ("plain",) * 8
    kinds_ctx = ("scale", "plain", "plain", "plain")
    p_lat, gt_lat = _proj(x, sh1, sc1, w_lat, w_gate_t, b_gate, kinds_lat, _rotary_tables(T))
    Tc = ctx.shape[1]
    p_ctx, gt_ctx = _proj(ctx.reshape(1, B * Tc, D), csh1, csc1, w_ctx, w_gate_t, b_gate, kinds_ctx)
    p_ctx = p_ctx.reshape(B, Tc, -1)
    gt_ctx = gt_ctx.reshape(N_GATES, B, Tc).transpose(1, 0, 2)

    ret, mls = _scans(ret_decay_logit[0], p_lat, p_ctx, _per_head_gates(gt_lat), _per_head_gates(gt_ctx),
                      ml_conv_w[0], ml_conv_b[0][None, :], (0, 1, 2, 3), (0, 1), (4, 5, 6, 7), (2, 3))

    wrt = jnp.zeros((ROUTE_ROWS, D), F32).at[:N_GROUPS].set(w_rg[0].T).at[8:8 + N_EXPERTS].set(w_re[0].T)
    brt = jnp.zeros((ROUTE_ROWS, 1), F32).at[:N_GROUPS, 0].set(b_rg[0]).at[8:8 + N_EXPERTS, 0].set(b_re[0])
    x1, ua, ub, ri, rw, cnt = _merge(alpha, ret, mls, p_lat, (8, 9), x, g1, sh2, sc2,
                                     ln1_g[0][None, :], ln1_b[0][None, :],
                                     w_ret_branch[0].astype(BF16), w_ml_branch[0].astype(BF16),
                                     w_out[0].astype(BF16), wrt.astype(BF16), brt)

    counts = cnt[:, 0]
    padded = (counts + MOE_BLK - 1) // MOE_BLK * MOE_BLK
    pad_end = jnp.cumsum(padded)
    pad_off = pad_end - padded
    dest = _table_lookup(pad_off, ri[0:2]) + ri[2:4]
    n_blocks = (2 * n_tok) // MOE_BLK + N_EXPERTS
    n_slots = n_blocks * MOE_BLK
    block_start = jnp.arange(n_blocks, dtype=I32) * MOE_BLK
    block_exp = jnp.minimum((block_start[:, None] >= pad_end[None, :]).sum(1), N_EXPERTS - 1).astype(I32)
    n_valid = jnp.clip(_table_lookup(counts, block_exp) - (block_start - _table_lookup(pad_off, block_exp)),
                       0, MOE_BLK).astype(I32)

    xa, xb = _sc_scatter2(ua.reshape(n_tok, PACK_W), ub.reshape(n_tok, PACK_W), dest[0], dest[1], n_slots)
    ya, yb = _experts(block_exp, n_valid, xa, xb, w_e1[0], w_e3[0], w_e2[0])
    ga, gb = _sc_gather(ya, yb, dest.reshape(2 * n_tok))
    return _final(alpha, x1, ga, gb, rw, g2, ln2_g[0][None, :], ln2_b[0][None, :])
```

```python
import functools

import jax
import jax.numpy as jnp
from jax import lax
from jax.experimental import pallas as pl
from jax.experimental.pallas import tpu as pltpu
from jax.experimental.pallas import tpu_sc as plsc

F32 = jnp.float32
BF16 = jnp.bfloat16
U32 = jnp.uint32
I32 = jnp.int32
HIGHEST = lax.Precision.HIGHEST

HEADS = 4
HEAD_W = 256
BRANCH_W = HEADS * HEAD_W
GRID_W = 64
ROPE_BASE = 10000.0
N_GATES = 16
N_GK = N_GATES // HEADS
N_GROUPS = 4
EXP_PER_GROUP = 8
N_EXPERTS = N_GROUPS * EXP_PER_GROUP
LN_EPS = 1e-5
NEG_INF = -1e30
KEY_SCALE = HEAD_W ** -0.5

SCAN_L = 256
CONV_ROWS = 128
PROJ_TM = 2048
PROJ_SUB = 256
MERGE_TM = 512
MOE_BLK = 512
SC_WIN = 128
PACK_W = 256
ROUTE_ROWS = 64
N_TAB = 6
AUG_W = HEAD_W + 128
VMEM_LIMIT = 48 * 1024 * 1024

NT_DIMS = (((1,), (1,)), ((), ()))
TN_DIMS = (((0,), (0,)), ((), ()))


def _cparams(*sem):
    return pltpu.CompilerParams(dimension_semantics=sem, vmem_limit_bytes=VMEM_LIMIT)


def _layer_norm(x):
    mu = jnp.mean(x, axis=-1, keepdims=True)
    xc = x - mu
    var = jnp.mean(xc * xc, axis=-1, keepdims=True)
    return xc * lax.rsqrt(var + LN_EPS)


def _log_sigmoid(x):
    return jnp.minimum(x, 0.0) - jnp.log1p(jnp.exp(-jnp.abs(x)))


def _silu(x):
    return x * jax.nn.sigmoid(x)


def _pack_pairs(hi, lo):
    hb = lax.bitcast_convert_type(hi.astype(BF16).astype(F32), U32)
    lb = lax.bitcast_convert_type(lo.astype(BF16).astype(F32), U32)
    return (hb & jnp.uint32(0xFFFF0000)) | (lb >> 16)


def _unpack_pairs(p):
    hi = lax.bitcast_convert_type(p & jnp.uint32(0xFFFF0000), F32)
    lo = lax.bitcast_convert_type(p << 16, F32)
    return hi, lo


def _split3(x):
    hi = x.astype(BF16).astype(F32)
    r1 = x - hi
    mid = r1.astype(BF16).astype(F32)
    lo = (r1 - mid).astype(BF16).astype(F32)
    return jnp.concatenate([hi, mid, lo], axis=0).astype(BF16)


def _ada_kernel(c_ref, w_ref, b_ref, o_ref):
    s = _silu(c_ref[...])
    o_ref[...] = jnp.dot(s, w_ref[...], precision=HIGHEST, preferred_element_type=F32) + b_ref[...]


def _ada(cs, w, b):
    rows, d = cs.shape
    cols = w.shape[1]
    tn = 1024
    return pl.pallas_call(
        _ada_kernel,
        out_shape=jax.ShapeDtypeStruct((rows, cols), F32),
        grid=(cols // tn,),
        in_specs=[pl.BlockSpec((rows, d), lambda j: (0, 0)),
                  pl.BlockSpec((d, tn), lambda j: (0, j)),
                  pl.BlockSpec((1, tn), lambda j: (0, j))],
        out_specs=pl.BlockSpec((rows, tn), lambda j: (0, j)),
        compiler_params=_cparams("parallel"),
        name="ada",
    )(cs, w, b)


def _proj_kernel(kinds, x_ref, sh_ref, sc_ref, w_ref, wg_ref, bg_ref, *rest):
    if "rot" in kinds or "rot_scale" in kinds:
        cos_ref, sin_ref, o_ref, gt_ref, u_ref = rest
    else:
        o_ref, gt_ref, u_ref = rest
    j = pl.program_id(2)
    tm = x_ref.shape[1]
    sub = min(PROJ_SUB, tm)

    def rotary(acc, rows, scale):
        for s in range(acc.shape[1] // 128):
            a = acc[:, s * 128:(s + 1) * 128]
            half = s % 2
            cs = cos_ref[rows, half * 128:(half + 1) * 128]
            sn = sin_ref[rows, half * 128:(half + 1) * 128]
            r = a * cs + pltpu.roll(a, 64, 1) * sn
            if scale != 1.0:
                r = r * scale
            o_ref[0, rows, s * 128:(s + 1) * 128] = r.astype(BF16)

    def section(kind, first):
        for r in range(tm // sub):
            rows = slice(r * sub, (r + 1) * sub)
            if first:
                u = _layer_norm(x_ref[0, rows, :]) * (1.0 + sc_ref[0]) + sh_ref[0]
                ub = u.astype(BF16)
                u_ref[rows, :] = ub
                gt_ref[0, :, rows] = lax.dot_general(wg_ref[...], ub, NT_DIMS,
                                                     preferred_element_type=F32) + bg_ref[...]
            else:
                ub = u_ref[rows, :]
            acc = jnp.dot(ub, w_ref[...], preferred_element_type=F32)
            if kind == "rot":
                rotary(acc, rows, 1.0)
            elif kind == "rot_scale":
                rotary(acc, rows, KEY_SCALE)
            elif kind == "scale":
                o_ref[0, rows, :] = (acc * KEY_SCALE).astype(BF16)
            else:
                o_ref[0, rows, :] = acc.astype(BF16)

    variants = {}
    for s, kind in enumerate(kinds):
        variants.setdefault((kind, s == 0), []).append(s)
    for (kind, first), secs in variants.items():
        cond = functools.reduce(jnp.logical_or, [j == s for s in secs])

        @pl.when(cond)
        def _(kind=kind, first=first):
            section(kind, first)


def _proj(x, sh, sc, w_main, w_gate_t, b_gate, kinds, w_block, tables=None):
    B, T, D = x.shape
    n_sec = len(kinds)
    tm = min(PROJ_TM, T)
    tn = BRANCH_W
    assert T % tm == 0
    in_specs = [
        pl.BlockSpec((1, tm, D), lambda i, b, j: (b, i, 0)),
        pl.BlockSpec((1, 1, D), lambda i, b, j: (b, 0, 0)),
        pl.BlockSpec((1, 1, D), lambda i, b, j: (b, 0, 0)),
        pl.BlockSpec((D, tn), lambda i, b, j: (0, w_block(j))),
        pl.BlockSpec((N_GATES, D), lambda i, b, j: (0, 0)),
        pl.BlockSpec((N_GATES, 1), lambda i, b, j: (0, 0)),
    ]
    args = [x, sh, sc, w_main, w_gate_t, b_gate]
    if tables is not None:
        in_specs += [pl.BlockSpec((tm, HEAD_W), lambda i, b, j: (i, 0))] * 2
        args += list(tables)
    return pl.pallas_call(
        functools.partial(_proj_kernel, kinds),
        out_shape=(jax.ShapeDtypeStruct((B, T, n_sec * tn), BF16),
                   jax.ShapeDtypeStruct((B, N_GATES, T), F32)),
        grid=(T // tm, B, n_sec),
        in_specs=in_specs,
        out_specs=(pl.BlockSpec((1, tm, tn), lambda i, b, j: (b, i, j)),
                   pl.BlockSpec((1, N_GATES, tm), lambda i, b, j: (b, 0, i))),
        scratch_shapes=[pltpu.VMEM((tm, D), BF16)],
        compiler_params=_cparams("parallel", "parallel", "arbitrary"),
        name="proj_lat" if tables is not None else "proj_ctx",
    )(*args)


def _ret_build(dl_ref, q_ref, k_ref, v_ref, rg_ref, ck_ref, cv_ref, o_ref,
               sf_ref, sb_ref, fs_ref, bs_ref, dec_ref, d_ref):
    h = pl.program_id(0)
    L = SCAN_L
    n_chunks = q_ref.shape[1] // L
    n_ctx_chunks = ck_ref.shape[1] // L
    lgf = _log_sigmoid(jnp.full((1, 1), dl_ref[0, h], F32))
    lgb = _log_sigmoid(jnp.full((1, 1), dl_ref[1, h], F32))

    @pl.when(pl.program_id(1) == 0)
    def _():
        ri = lax.broadcasted_iota(I32, (L, L), 0)
        ci = lax.broadcasted_iota(I32, (L, L), 1)
        rel = (ri - ci).astype(F32)
        d_ref[...] = jnp.where(rel >= 0.0, jnp.exp(jnp.maximum(rel, 0.0) * lgf),
                               jnp.exp(jnp.maximum(-rel, 0.0) * lgb))
        row = lax.broadcasted_iota(I32, (L, HEAD_W), 0).astype(F32)
        dec_ref[0] = jnp.exp((row + 1.0) * lgf)
        dec_ref[1] = jnp.exp((L - 1.0 - row) * lgf)
        dec_ref[2] = jnp.exp((L - row) * lgb)
        dec_ref[3] = jnp.exp(row * lgb)

    cdf = jnp.exp(L * lgf)
    cdb = jnp.exp(L * lgb)

    def update(s_ref, kc, vc, kd, cd):
        kdec = (kc.astype(F32) * kd).astype(BF16)
        s_ref[...] = s_ref[...] * cd + lax.dot_general(kdec, vc, TN_DIMS, preferred_element_type=F32)

    sf_ref[...] = jnp.zeros_like(sf_ref)
    sb_ref[...] = jnp.zeros_like(sb_ref)
    for c in range(n_ctx_chunks):
        update(sf_ref, ck_ref[0, c * L:(c + 1) * L, :], cv_ref[0, c * L:(c + 1) * L, :], dec_ref[1], cdf)
    for c in reversed(range(n_ctx_chunks)):
        update(sb_ref, ck_ref[0, c * L:(c + 1) * L, :], cv_ref[0, c * L:(c + 1) * L, :], dec_ref[3], cdb)

    def state_pass(i, carry):
        cb = n_chunks - 1 - i
        rf = pl.multiple_of(i * L, L)
        rb = pl.multiple_of(cb * L, L)
        fs_ref[i] = sf_ref[...].astype(BF16)
        bs_ref[cb] = sb_ref[...].astype(BF16)
        update(sf_ref, k_ref[0, pl.ds(rf, L), :], v_ref[0, pl.ds(rf, L), :], dec_ref[1], cdf)
        update(sb_ref, k_ref[0, pl.ds(rb, L), :], v_ref[0, pl.ds(rb, L), :], dec_ref[3], cdb)
        return carry

    def finish_states():
        fs_ref[n_chunks - 1] = sf_ref[...].astype(BF16)
        bs_ref[0] = sb_ref[...].astype(BF16)

    def out_chunk(c):
        r0 = pl.multiple_of(c * L, L)
        q = q_ref[0, pl.ds(r0, L), :]
        k = k_ref[0, pl.ds(r0, L), :]
        v = v_ref[0, pl.ds(r0, L), :]
        s = lax.dot_general(q, k, NT_DIMS, preferred_element_type=F32)
        att = (s * d_ref[...]).astype(BF16)
        o = jnp.dot(att, v, preferred_element_type=F32)
        o = o + jnp.dot(q, fs_ref[c], preferred_element_type=F32) * dec_ref[0]
        o = o + jnp.dot(q, bs_ref[c], preferred_element_type=F32) * dec_ref[2]
        rg = rg_ref[0, pl.ds(r0, L), :].astype(F32)
        o_ref[0, pl.ds(r0, L), :] = (_layer_norm(o) * _silu(rg)).astype(BF16)

    return state_pass, finish_states, out_chunk


def _ret_scratch(n_chunks):
    return [pltpu.VMEM((HEAD_W, HEAD_W), F32),
            pltpu.VMEM((HEAD_W, HEAD_W), F32),
            pltpu.VMEM((n_chunks, HEAD_W, HEAD_W), BF16),
            pltpu.VMEM((n_chunks, HEAD_W, HEAD_W), BF16),
            pltpu.VMEM((4, SCAN_L, HEAD_W), F32),
            pltpu.VMEM((SCAN_L, SCAN_L), F32)]


def _mlstm_build(qp_ref, kp_ref, v_ref, mo_ref, ckp_ref, cv_ref, gt_ref, cgt_ref,
                 wq_ref, bq_ref, wk_ref, bk_ref, o_ref,
                 tab_ref, row_ref,
                 cf_ref, mf_ref, cb_ref, mb_ref, cfs_ref, mfs_ref, cbs_ref, mbs_ref, mask_ref,
                 xf_ref, q_ref, k_ref, ck_ref):
    L = SCAN_L
    T = qp_ref.shape[1]
    Tc = ckp_ref.shape[1]
    n_chunks = T // L
    n_ctx_chunks = Tc // L
    CV = CONV_ROWS

    def conv_stage(src_ref, t_len):
        xf_ref[pl.ds(0, 8), :] = jnp.zeros((8, HEAD_W), F32)
        xf_ref[pl.ds(8 + t_len, 8), :] = jnp.zeros((8, HEAD_W), F32)
        xf_ref[pl.ds(8, t_len), :] = src_ref[0].astype(F32)

    def conv_chunk(c, w, b, dst_ref, scale):
        r0 = pl.multiple_of(c * CV, CV)
        win = xf_ref[pl.ds(r0, CV + 16), :]
        prev = pltpu.roll(win, 1, 0)[8:8 + CV, :]
        cur = win[8:8 + CV, :]
        nxt = pltpu.roll(win, CV + 15, 0)[8:8 + CV, :]
        y = _silu(prev * w[0:1, :] + cur * w[1:2, :] + nxt * w[2:3, :] + b)
        if scale != 1.0:
            y = y * scale
        dst_ref[pl.ds(r0, CV), :] = y.astype(BF16)

    def conv_all(src_ref, w_ref, b_ref, dst_ref, t_len, scale):
        conv_stage(src_ref, t_len)
        w = w_ref[...]
        b = b_ref[...]

        def body(c, carry):
            conv_chunk(c, w, b, dst_ref, scale)
            return carry

        lax.fori_loop(0, t_len // CV, body, 0)

    conv_all(kp_ref, wk_ref, bk_ref, k_ref, T, KEY_SCALE)
    conv_all(ckp_ref, wk_ref, bk_ref, ck_ref, Tc, KEY_SCALE)
    conv_stage(qp_ref, T)
    q_per_chunk = L // CV

    def conv_q(c):
        for u in range(q_per_chunk):
            conv_chunk(c * q_per_chunk + u, wq_ref[...], bq_ref[...], q_ref, 1.0)

    ri = lax.broadcasted_iota(I32, (L, L), 0)
    ci = lax.broadcasted_iota(I32, (L, L), 1)
    tri_u = (ri <= ci).astype(BF16)
    lane8 = lax.broadcasted_iota(I32, (8, L), 1)
    sub8 = lax.broadcasted_iota(I32, (8, L), 0)
    sel_r = lax.broadcasted_iota(I32, (24, 8 * 128), 0) % 8
    sel_c = lax.broadcasted_iota(I32, (24, 8 * 128), 1) // 128
    sel3 = (sel_r == sel_c).astype(BF16)
    ones_cols = jnp.ones((L, AUG_W - HEAD_W), BF16)

    def chunk_tables(g8, n_used, state_only):
        i_f, i_b = g8[0], g8[2]
        lf_f, lf_b = _log_sigmoid(g8[1]), _log_sigmoid(g8[3])
        cs3 = jnp.dot(_split3(jnp.concatenate([lf_f, lf_b], axis=0)), tri_u,
                      preferred_element_type=F32)
        cs = cs3[0:16] + cs3[16:32] + cs3[32:48]
        b_f = cs[0:8]
        b_b = cs[8:16, L - 1:L] - cs[8:16] + lf_b
        z_f = i_f - b_f
        z_b = i_b - b_b
        g_f = b_f[:, L - 1:L] - b_f + i_f
        g_b = b_b[:, 0:1] - b_b + i_b
        mf, mb = z_f, z_b
        s = 1
        while s < L:
            mf = jnp.maximum(mf, jnp.where(lane8 >= s, pltpu.roll(mf, s, 1), NEG_INF))
            mb = jnp.maximum(mb, jnp.where(lane8 < L - s, pltpu.roll(mb, L - s, 1), NEG_INF))
            s *= 2
        mb = jnp.where(lane8 < L - 1, pltpu.roll(mb, L - 1, 1), NEG_INF)
        reps = [None if state_only and t not in (2, 5) else
                lax.dot_general(_split3(val), sel3[:, 0:n_used * 128], TN_DIMS, preferred_element_type=F32)
                for t, val in enumerate((mf, b_f, g_f, mb, b_b, g_b))]

        def rows_of(c):
            out = jnp.zeros((8, L), F32)
            for r, val in enumerate((z_f, z_b, g_f, g_b, b_f, b_b)):
                out = jnp.where(sub8 == r, val[c:c + 1], out)
            return out

        return rows_of, reps

    lat_rows, lat_reps = chunk_tables(gt_ref[0, 0], n_chunks, False)
    for c in range(n_chunks):
        row_ref[c] = lat_rows(c)
        for t in range(N_TAB):
            tab_ref[t, c * L:(c + 1) * L, :] = lat_reps[t][:, c * 128:(c + 1) * 128]

    def lanes2(x):
        return jnp.concatenate([x, x], axis=1)

    def advance(k, v, g_rep, g_row, b_last, c_ref, m_ref):
        m = m_ref[...]
        m_new = jnp.maximum(b_last + m, jnp.max(g_row, axis=-1, keepdims=True))
        kw = (k.astype(F32) * jnp.exp(lanes2(g_rep) - m_new)).astype(BF16)
        v_aug = jnp.concatenate([v, ones_cols], axis=1)
        c_ref[...] = jnp.exp(b_last + m - m_new) * c_ref[...] + lax.dot_general(
            kw, v_aug, TN_DIMS, preferred_element_type=F32)
        m_ref[...] = m_new

    for r in (cf_ref, mf_ref, cb_ref, mb_ref):
        r[...] = jnp.zeros_like(r)
    ctx_rows, ctx_reps = chunk_tables(cgt_ref[0, 0], n_ctx_chunks, True)
    for c in range(n_ctx_chunks):
        rows = ctx_rows(c)
        advance(ck_ref[c * L:(c + 1) * L, :], cv_ref[0, c * L:(c + 1) * L, :],
                ctx_reps[2][:, c * 128:(c + 1) * 128], rows[2:3], rows[4:5, L - 1:L], cf_ref, mf_ref)
    for c in reversed(range(n_ctx_chunks)):
        rows = ctx_rows(c)
        advance(ck_ref[c * L:(c + 1) * L, :], cv_ref[0, c * L:(c + 1) * L, :],
                ctx_reps[5][:, c * 128:(c + 1) * 128], rows[3:4], rows[5:6, 0:1], cb_ref, mb_ref)

    def state_pass(i, carry):
        cb = n_chunks - 1 - i
        rf = pl.multiple_of(i * L, L)
        rb = pl.multiple_of(cb * L, L)
        cfs_ref[i] = cf_ref[...].astype(BF16)
        mfs_ref[i] = mf_ref[...]
        cbs_ref[cb] = cb_ref[...].astype(BF16)
        mbs_ref[cb] = mb_ref[...]
        rows_f = row_ref[i]
        rows_b = row_ref[cb]
        advance(k_ref[pl.ds(rf, L), :], v_ref[0, pl.ds(rf, L), :], tab_ref[2, pl.ds(rf, L), :],
                rows_f[2:3], rows_f[4:5, L - 1:L], cf_ref, mf_ref)
        advance(k_ref[pl.ds(rb, L), :], v_ref[0, pl.ds(rb, L), :], tab_ref[5, pl.ds(rb, L), :],
                rows_b[3:4], rows_b[5:6, 0:1], cb_ref, mb_ref)
        conv_q(i)
        return carry

    def finish_states():
        conv_q(n_chunks - 1)
        cfs_ref[n_chunks - 1] = cf_ref[...].astype(BF16)
        mfs_ref[n_chunks - 1] = mf_ref[...]
        cbs_ref[0] = cb_ref[...].astype(BF16)
        mbs_ref[0] = mb_ref[...]

    def direction(q, v_aug, s, z_row, zmax_rep, b_rep, mask, c_in, m_in):
        mx = jnp.maximum(zmax_rep, m_in)
        att = s * jnp.exp((z_row - lanes2(mx)) + mask)
        na = jnp.dot(att.astype(BF16), v_aug, preferred_element_type=F32)
        qa = jnp.dot(q, c_in, preferred_element_type=F32)
        a = jnp.exp(m_in - mx)
        num = na[:, 0:HEAD_W] + lanes2(a) * qa[:, 0:HEAD_W]
        den = na[:, HEAD_W:] + a * qa[:, HEAD_W:]
        scale = 1.0 / jnp.maximum(jnp.abs(den), jnp.exp(-(b_rep + mx)))
        return num * lanes2(scale)

    @pl.when(pl.program_id(1) == 0)
    def _():
        mask_ref[0] = jnp.where(ci <= ri, 0.0, NEG_INF)
        mask_ref[1] = jnp.where(ci > ri, 0.0, NEG_INF)

    def out_chunk(c):
        r0 = pl.multiple_of(c * L, L)
        q = q_ref[pl.ds(r0, L), :]
        k = k_ref[pl.ds(r0, L), :]
        v_aug = jnp.concatenate([v_ref[0, pl.ds(r0, L), :], ones_cols], axis=1)
        s = lax.dot_general(q, k, NT_DIMS, preferred_element_type=F32)
        rows = row_ref[c]
        tot = direction(q, v_aug, s, rows[0:1], tab_ref[0, pl.ds(r0, L), :], tab_ref[1, pl.ds(r0, L), :],
                        mask_ref[0], cfs_ref[c], mfs_ref[c])
        tot = tot + direction(q, v_aug, s, rows[1:2], tab_ref[3, pl.ds(r0, L), :],
                              tab_ref[4, pl.ds(r0, L), :], mask_ref[1], cbs_ref[c], mbs_ref[c])
        mo = mo_ref[0, pl.ds(r0, L), :].astype(F32)
        o_ref[0, pl.ds(r0, L), :] = (_layer_norm(tot) * jax.nn.sigmoid(mo)).astype(BF16)

    return state_pass, finish_states, out_chunk


def _mlstm_scratch(T, Tc, n_chunks):
    state = [pltpu.VMEM((HEAD_W, AUG_W), F32), pltpu.VMEM((1, 1), F32)]
    snaps = [pltpu.VMEM((n_chunks, HEAD_W, AUG_W), BF16), pltpu.VMEM((n_chunks, 1, 1), F32)]
    return [pltpu.VMEM((N_TAB, T, 128), F32), pltpu.VMEM((n_chunks, 8, SCAN_L), F32)] \
        + state + state + snaps + snaps + [pltpu.VMEM((2, SCAN_L, SCAN_L), F32)] \
        + [pltpu.VMEM((T + 16, HEAD_W), F32), pltpu.VMEM((T, HEAD_W), BF16),
           pltpu.VMEM((T, HEAD_W), BF16), pltpu.VMEM((Tc, HEAD_W), BF16)]


def _scan_kernel(n_ret_scratch, dl_ref, rq_ref, rk_ref, rv_ref, rg_ref, rck_ref, rcv_ref,
                 mq_ref, mk_ref, mv_ref, mo_ref, mck_ref, mcv_ref, gt_ref, cgt_ref,
                 wq_ref, bq_ref, wk_ref, bk_ref, r_ref, m_ref, *scratch):
    n_chunks = rq_ref.shape[1] // SCAN_L
    ret = _ret_build(dl_ref, rq_ref, rk_ref, rv_ref, rg_ref, rck_ref, rcv_ref, r_ref,
                     *scratch[:n_ret_scratch])
    mls = _mlstm_build(mq_ref, mk_ref, mv_ref, mo_ref, mck_ref, mcv_ref, gt_ref, cgt_ref,
                       wq_ref, bq_ref, wk_ref, bk_ref, m_ref, *scratch[n_ret_scratch:])

    def state_pass(i, carry):
        ret[0](i, carry)
        mls[0](i, carry)
        return carry

    lax.fori_loop(0, n_chunks - 1, state_pass, 0)
    ret[1]()
    mls[1]()

    def out_pass(i, carry):
        for c in (2 * i, 2 * i + 1):
            ret[2](c)
            mls[2](c)
        return carry

    lax.fori_loop(0, n_chunks // 2, out_pass, 0)


def _scans(decay_logit, p_lat, p_ctx, gt, cgt, conv_w, conv_b, ret_lat, ret_ctx, ml_lat, ml_ctx):
    B, T, _ = p_lat.shape
    Tc = p_ctx.shape[1]
    assert T % (2 * SCAN_L) == 0 and Tc % SCAN_L == 0 and T // SCAN_L <= 8
    n_chunks = T // SCAN_L

    def lat(sec):
        return pl.BlockSpec((1, T, HEAD_W), lambda h, b: (b, 0, sec * HEADS + h))

    def cx(sec):
        return pl.BlockSpec((1, Tc, HEAD_W), lambda h, b: (b, 0, sec * HEADS + h))

    gates = pl.BlockSpec((1, 1, N_GK, 8, SCAN_L), lambda h, b: (b, h, 0, 0, 0))
    out = pl.BlockSpec((1, T, HEAD_W), lambda h, b: (b, 0, h))
    conv_specs = [pl.BlockSpec((3, HEAD_W), lambda h, b: (0, h)),
                  pl.BlockSpec((1, HEAD_W), lambda h, b: (0, h)),
                  pl.BlockSpec((3, HEAD_W), lambda h, b: (0, HEADS + h)),
                  pl.BlockSpec((1, HEAD_W), lambda h, b: (0, HEADS + h))]
    ret_scratch = _ret_scratch(n_chunks)
    return pl.pallas_call(
        functools.partial(_scan_kernel, len(ret_scratch)),
        out_shape=(jax.ShapeDtypeStruct((B, T, BRANCH_W), BF16),
                   jax.ShapeDtypeStruct((B, T, BRANCH_W), BF16)),
        grid=(HEADS, B),
        in_specs=[pl.BlockSpec(memory_space=pltpu.SMEM)]
        + [lat(s) for s in ret_lat] + [cx(s) for s in ret_ctx]
        + [lat(s) for s in ml_lat] + [cx(s) for s in ml_ctx] + [gates, gates] + conv_specs,
        out_specs=(out, out),
        scratch_shapes=ret_scratch + _mlstm_scratch(T, Tc, n_chunks),
        compiler_params=_cparams("arbitrary", "arbitrary"),
        name="scans",
    )(decay_logit, *([p_lat] * 4), *([p_ctx] * 2), *([p_lat] * 4), *([p_ctx] * 2), gt, cgt,
      conv_w, conv_b, conv_w, conv_b)


def _merge_kernel(alpha, r_ref, m_ref, gr_ref, gm_ref, x_ref, g1_ref, sh2_ref, sc2_ref,
                  lng_ref, lnb_ref, wr_ref, wm_ref, wo_ref, wrt_ref, brt_ref,
                  x1_ref, ua_ref, ub_ref, ri_ref, rw_ref, cnt_ref, carry_ref, u_ref):
    @pl.when(jnp.logical_and(pl.program_id(0) == 0, pl.program_id(1) == 0))
    def _():
        carry_ref[...] = jnp.zeros_like(carry_ref)
        tm = x_ref.shape[1]
        r = lax.broadcasted_iota(I32, (tm, tm), 0)
        c = lax.broadcasted_iota(I32, (tm, tm), 1)
        u_ref[...] = (r < c).astype(BF16)

    yr = jnp.dot(r_ref[0], wr_ref[...], preferred_element_type=F32)
    ym = jnp.dot(m_ref[0], wm_ref[...], preferred_element_type=F32)
    y = jax.nn.sigmoid(gr_ref[0].astype(F32)) * yr + jax.nn.sigmoid(gm_ref[0].astype(F32)) * ym
    yo = jnp.dot(y.astype(BF16), wo_ref[...], preferred_element_type=F32)
    x1 = _layer_norm(alpha * x_ref[0] + g1_ref[0] * yo) * lng_ref[...] + lnb_ref[...]
    x1_ref[0] = x1
    u2 = _layer_norm(x1) * (1.0 + sc2_ref[0]) + sh2_ref[0]
    ua_ref[0] = _pack_pairs(u2[:, 0:PACK_W], u2[:, PACK_W:2 * PACK_W])
    ub_ref[0] = _pack_pairs(u2[:, 2 * PACK_W:3 * PACK_W], u2[:, 3 * PACK_W:4 * PACK_W])
    lt = lax.dot_general(wrt_ref[...], u2.astype(BF16), NT_DIMS, preferred_element_type=F32) + brt_ref[...]
    _route_tile(lt, ri_ref, rw_ref, cnt_ref, carry_ref, u_ref)


def _route_tile(lt, ri_ref, rw_ref, cnt_ref, carry_ref, u_ref):
    tm = lt.shape[1]
    lg = lt[0:N_GROUPS, :]
    eg = jnp.exp(lg - jnp.max(lg, axis=0, keepdims=True))
    pg = eg / jnp.sum(eg, axis=0, keepdims=True)
    pg_top = jnp.max(pg, axis=0, keepdims=True)
    rows_g = lax.broadcasted_iota(I32, pg.shape, 0)
    g_idx = jnp.min(jnp.where(pg == pg_top, rows_g, N_GROUPS), axis=0, keepdims=True)

    le = jnp.zeros((EXP_PER_GROUP, tm), F32)
    for g in range(N_GROUPS):
        lo = 8 + g * EXP_PER_GROUP
        le = jnp.where(g_idx == g, lt[lo:lo + EXP_PER_GROUP, :], le)
    ee = jnp.exp(le - jnp.max(le, axis=0, keepdims=True))
    pe = ee / jnp.sum(ee, axis=0, keepdims=True)
    rows_e = lax.broadcasted_iota(I32, pe.shape, 0)
    v1 = jnp.max(pe, axis=0, keepdims=True)
    i1 = jnp.min(jnp.where(pe == v1, rows_e, EXP_PER_GROUP), axis=0, keepdims=True)
    pe2 = jnp.where(rows_e == i1, -1.0, pe)
    v2 = jnp.max(pe2, axis=0, keepdims=True)
    i2 = jnp.min(jnp.where(pe2 == v2, rows_e, EXP_PER_GROUP), axis=0, keepdims=True)
    den = v1 + v2
    rw_ref[...] = jnp.zeros_like(rw_ref)
    rw_ref[0:1, :] = pg_top * v1 / den
    rw_ref[1:2, :] = pg_top * v2 / den
    e1 = g_idx * EXP_PER_GROUP + i1
    e2 = g_idx * EXP_PER_GROUP + i2

    rows_x = lax.broadcasted_iota(I32, (N_EXPERTS, tm), 0)
    oh1 = (rows_x == e1).astype(F32)
    oh2 = (rows_x == e2).astype(F32)
    both = oh1 + oh2
    before = carry_ref[:, 0:1] + jnp.dot(both.astype(BF16), u_ref[...], preferred_element_type=F32)
    ri_ref[0:1, :] = e1
    ri_ref[1:2, :] = e2
    ri_ref[2:3, :] = jnp.sum(oh1 * before, axis=0, keepdims=True).astype(I32)
    ri_ref[3:4, :] = jnp.sum(oh2 * before, axis=0, keepdims=True).astype(I32)
    carry_ref[...] = carry_ref[...] + jnp.sum(both, axis=1, keepdims=True)
    cnt_ref[...] = carry_ref[...].astype(I32)


def _merge(alpha, r, m, p_lat, sec_gates, x, g1, sh2, sc2, lng, lnb, wr, wm, wo, wrt, brt):
    B, T, D = x.shape
    tm = MERGE_TM
    per_b = T // tm
    n = B * T

    def tile(w):
        return pl.BlockSpec((1, tm, w), lambda b, i: (b, i, 0))

    def sec(s):
        return pl.BlockSpec((1, tm, BRANCH_W), lambda b, i: (b, i, s))

    def mod():
        return pl.BlockSpec((1, 1, D), lambda b, i: (b, 0, 0))

    def const(shape):
        return pl.BlockSpec(shape, lambda b, i: (0,) * len(shape))

    return pl.pallas_call(
        functools.partial(_merge_kernel, alpha),
        out_shape=(jax.ShapeDtypeStruct((B, T, D), F32),
                   jax.ShapeDtypeStruct((B, T, PACK_W), U32),
                   jax.ShapeDtypeStruct((B, T, PACK_W), U32),
                   jax.ShapeDtypeStruct((4, n), I32),
                   jax.ShapeDtypeStruct((8, n), F32),
                   jax.ShapeDtypeStruct((N_EXPERTS, 128), I32)),
        grid=(B, per_b),
        in_specs=[tile(BRANCH_W), tile(BRANCH_W), sec(sec_gates[0]), sec(sec_gates[1]), tile(D),
                  mod(), mod(), mod(), const((1, D)), const((1, D)),
                  const((BRANCH_W, D)), const((BRANCH_W, D)), const((D, D)),
                  const((ROUTE_ROWS, D)), const((ROUTE_ROWS, 1))],
        out_specs=(tile(D), tile(PACK_W), tile(PACK_W),
                   pl.BlockSpec((4, tm), lambda b, i: (0, b * per_b + i)),
                   pl.BlockSpec((8, tm), lambda b, i: (0, b * per_b + i)),
                   pl.BlockSpec((N_EXPERTS, 128), lambda b, i: (0, 0))),
        scratch_shapes=[pltpu.VMEM((N_EXPERTS, 128), F32), pltpu.VMEM((tm, tm), BF16)],
        compiler_params=_cparams("arbitrary", "arbitrary"),
        name="merge",
    )(r, m, p_lat, p_lat, x, g1, sh2, sc2, lng, lnb, wr, wm, wo, wrt, brt)


def _sc_mesh():
    return plsc.VectorSubcoreMesh(core_axis_name="c", subcore_axis_name="s")


def _sc_scatter2(rows_a, rows_b, idx0, idx1, n_out):
    m, w = rows_a.shape
    out = jax.ShapeDtypeStruct((n_out, w), rows_a.dtype)

    @functools.partial(pl.kernel, out_type=(out, out), mesh=_sc_mesh(), scratch_types=[])
    def k(xa_hbm, xb_hbm, i0_hbm, i1_hbm, oa_hbm, ob_hbm):
        for x_hbm, o_hbm in ((xa_hbm, oa_hbm), (xb_hbm, ob_hbm)):
            def body(x_vmem, i0_vmem, i1_vmem, o_hbm=o_hbm):
                pltpu.sync_copy(x_vmem, o_hbm.at[i0_vmem.at[0]])
                pltpu.sync_copy(x_vmem, o_hbm.at[i1_vmem.at[0]])

            pltpu.emit_pipeline(
                body,
                grid=(m // SC_WIN,),
                in_specs=[pl.BlockSpec((SC_WIN, w), lambda i: (i, 0)),
                          pl.BlockSpec((1, SC_WIN), lambda i: (0, i)),
                          pl.BlockSpec((1, SC_WIN), lambda i: (0, i))],
                out_specs=[],
                core_axis_name=("c", "s"),
                dimension_semantics=(pltpu.PARALLEL,),
            )(x_hbm, i0_hbm, i1_hbm)

    return k(rows_a, rows_b, idx0.reshape(1, m), idx1.reshape(1, m))


def _sc_gather(table_a, table_b, idx):
    m = idx.shape[0]
    w = table_a.shape[1]
    out = jax.ShapeDtypeStruct((m, w), table_a.dtype)

    @functools.partial(pl.kernel, out_type=(out, out), mesh=_sc_mesh(), scratch_types=[])
    def k(ta_hbm, tb_hbm, i_hbm, oa_hbm, ob_hbm):
        for t_hbm, o_hbm in ((ta_hbm, oa_hbm), (tb_hbm, ob_hbm)):
            def body(i_vmem, o_vmem, t_hbm=t_hbm):
                pltpu.sync_copy(t_hbm.at[i_vmem.at[0]], o_vmem)

            pltpu.emit_pipeline(
                body,
                grid=(m // SC_WIN,),
                in_specs=[pl.BlockSpec((1, SC_WIN), lambda i: (0, i))],
                out_specs=[pl.BlockSpec((SC_WIN, w), lambda i: (i, 0))],
                core_axis_name=("c", "s"),
                dimension_semantics=(pltpu.PARALLEL,),
            )(i_hbm, o_hbm)

    return k(table_a, table_b, idx.reshape(1, m))


def _expert_kernel(be_ref, nv_ref, xa_ref, xb_ref, w1_ref, w3_ref, w2_ref, ya_ref, yb_ref):
    j = pl.program_id(0)
    nv = nv_ref[j]

    @pl.when(nv > 0)
    def _():
        valid = lax.broadcasted_iota(I32, xa_ref.shape, 0) < nv
        zero = jnp.zeros(xa_ref.shape, U32)
        parts = _unpack_pairs(jnp.where(valid, xa_ref[...], zero)) + \
            _unpack_pairs(jnp.where(valid, xb_ref[...], zero))
        x = jnp.concatenate([p.astype(BF16) for p in parts], axis=1)
        h1 = jnp.dot(x, w1_ref[0], preferred_element_type=F32)
        h3 = jnp.dot(x, w3_ref[0], preferred_element_type=F32)
        y = jnp.dot((_silu(h1) * h3).astype(BF16), w2_ref[0], preferred_element_type=F32)
        ya_ref[...] = _pack_pairs(y[:, 0:PACK_W], y[:, PACK_W:2 * PACK_W])
        yb_ref[...] = _pack_pairs(y[:, 2 * PACK_W:3 * PACK_W], y[:, 3 * PACK_W:4 * PACK_W])

    @pl.when(nv == 0)
    def _():
        ya_ref[...] = jnp.zeros_like(ya_ref)
        yb_ref[...] = jnp.zeros_like(yb_ref)


def _experts(block_exp, n_valid, xa, xb, w1, w3, w2):
    n_slots = xa.shape[0]
    n_blocks = n_slots // MOE_BLK
    d, de = w1.shape[1], w1.shape[2]
    slot = pl.BlockSpec((MOE_BLK, PACK_W), lambda j, be, nv: (j, 0))
    grid_spec = pltpu.PrefetchScalarGridSpec(
        num_scalar_prefetch=2,
        grid=(n_blocks,),
        in_specs=[slot, slot,
                  pl.BlockSpec((1, d, de), lambda j, be, nv: (be[j], 0, 0)),
                  pl.BlockSpec((1, d, de), lambda j, be, nv: (be[j], 0, 0)),
                  pl.BlockSpec((1, de, d), lambda j, be, nv: (be[j], 0, 0))],
        out_specs=(slot, slot),
    )
    return pl.pallas_call(
        _expert_kernel,
        out_shape=(jax.ShapeDtypeStruct((n_slots, PACK_W), U32),
                   jax.ShapeDtypeStruct((n_slots, PACK_W), U32)),
        grid_spec=grid_spec,
        compiler_params=_cparams("arbitrary"),
        name="experts",
    )(block_exp, n_valid, xa, xb, w1, w3, w2)


def _final_kernel(alpha, x1_ref, a0_ref, b0_ref, a1_ref, b1_ref, w_ref, g2_ref, lng_ref, lnb_ref, o_ref):
    w = w_ref[...].T
    w0 = w[:, 0:1]
    w1 = w[:, 1:2]
    parts0 = _unpack_pairs(a0_ref[...]) + _unpack_pairs(b0_ref[...])
    parts1 = _unpack_pairs(a1_ref[...]) + _unpack_pairs(b1_ref[...])
    f = jnp.concatenate([w0 * p0 + w1 * p1 for p0, p1 in zip(parts0, parts1)], axis=1)
    o_ref[0] = _layer_norm(alpha * x1_ref[0] + g2_ref[0] * f) * lng_ref[...] + lnb_ref[...]


def _final(alpha, x1, ya, yb, w, g2, lng, lnb):
    B, T, D = x1.shape
    tm = MERGE_TM
    per_b = T // tm
    n_tiles = B * per_b

    def rows(k):
        return pl.BlockSpec((tm, PACK_W), lambda b, i: (k * n_tiles + b * per_b + i, 0))

    return pl.pallas_call(
        functools.partial(_final_kernel, alpha),
        out_shape=jax.ShapeDtypeStruct((B, T, D), F32),
        grid=(B, per_b),
        in_specs=[pl.BlockSpec((1, tm, D), lambda b, i: (b, i, 0)),
                  rows(0), rows(0), rows(1), rows(1),
                  pl.BlockSpec((8, tm), lambda b, i: (0, b * per_b + i)),
                  pl.BlockSpec((1, 1, D), lambda b, i: (b, 0, 0)),
                  pl.BlockSpec((1, D), lambda b, i: (0, 0)),
                  pl.BlockSpec((1, D), lambda b, i: (0, 0))],
        out_specs=pl.BlockSpec((1, tm, D), lambda b, i: (b, i, 0)),
        compiler_params=_cparams("parallel", "parallel"),
        name="final",
    )(x1, ya, yb, ya, yb, w, g2, lng, lnb)


def _rotary_tables(T):
    quarter = HEAD_W // 4
    freqs = ROPE_BASE ** (-jnp.arange(quarter, dtype=F32) / quarter)
    t = jnp.arange(T)
    ang_r = (t // GRID_W).astype(F32)[:, None] * freqs[None, :]
    ang_c = (t % GRID_W).astype(F32)[:, None] * freqs[None, :]
    cos = jnp.concatenate([jnp.cos(ang_r)] * 2 + [jnp.cos(ang_c)] * 2, axis=1)
    sin = jnp.concatenate([-jnp.sin(ang_r), jnp.sin(ang_r), -jnp.sin(ang_c), jnp.sin(ang_c)], axis=1)
    return cos, sin


def _per_head_gates(gt):
    B, _, T = gt.shape
    n_chunks = T // SCAN_L
    gth = gt.reshape(B, N_GK, HEADS, n_chunks, SCAN_L).transpose(0, 2, 1, 3, 4)
    return jnp.pad(gth, ((0, 0), (0, 0), (0, 0), (0, 8 - n_chunks), (0, 0)))


def _table_lookup(table, idx):
    sel = idx[..., None] == jnp.arange(table.shape[0], dtype=idx.dtype)
    return jnp.sum(jnp.where(sel, table, 0), axis=-1)


def kernel(x, c, ctx, c_ctx, w_ada, b_ada, w_in, b_mgate, ml_conv_w, ml_conv_b, ret_decay_logit, w_ret_branch, w_ml_branch, w_out, ln1_g, ln1_b, w_rg, b_rg, w_re, b_re, w_e1, w_e3, w_e2, ln2_g, ln2_b):
    B, T, D = x.shape
    depth = w_ada.shape[0]
    assert depth == 1 and D == BRANCH_W and T % GRID_W == 0
    alpha = (2 * depth) ** 0.25
    n_tok = B * T

    n_rows = -(-(B + 1) // 8) * 8
    cs = jnp.zeros((n_rows, D), F32).at[:B].set(c).at[B].set(c_ctx)
    mod = _ada(cs, w_ada[0], b_ada[0][None, :])
    sh1, sc1, g1, sh2, sc2, g2 = [mod[:B, None, i * D:(i + 1) * D] for i in range(6)]
    csh1 = mod[B, 0 * D:1 * D].reshape(1, 1, D)
    csc1 = mod[B, 1 * D:2 * D].reshape(1, 1, D)

    w = w_in[0]
    sec_w = [w[:, s * BRANCH_W:(s + 1) * BRANCH_W] for s in range(8)]
    g_lo = 8 * BRANCH_W
    w_gate_t = w[:, g_lo:g_lo + N_GATES].T.astype(BF16)
    b_gate = b_mgate[0][:, None]
    sec_w += [w[:, g_lo + N_GATES:g_lo + N_GATES + D], w[:, g_lo + N_GATES + D:]]
    w_sec = jnp.concatenate(sec_w, axis=1).astype(BF16)
    kinds_lat = ("rot", "rot_scale") + ("plain",) * 8
    kinds_ctx = ("scale", "plain", "plain", "plain")
    p_lat, gt_lat = _proj(x, sh1, sc1, w_sec, w_gate_t, b_gate, kinds_lat, lambda j: j, _rotary_tables(T))
    Tc = ctx.shape[1]
    p_ctx, gt_ctx = _proj(ctx.reshape(1, B * Tc, D), csh1, csc1, w_sec, w_gate_t, b_gate, kinds_ctx,
                          lambda j: j + 1 + 2 * (j // 2))
    p_ctx = p_ctx.reshape(B, Tc, -1)
    gt_ctx = gt_ctx.reshape(N_GATES, B, Tc).transpose(1, 0, 2)

    ret, mls = _scans(ret_decay_logit[0], p_lat, p_ctx, _per_head_gates(gt_lat), _per_head_gates(gt_ctx),
                      ml_conv_w[0], ml_conv_b[0][None, :], (0, 1, 2, 3), (0, 1), (4, 5, 6, 7), (2, 3))

    wrt = jnp.zeros((ROUTE_ROWS, D), F32).at[:N_GROUPS].set(w_rg[0].T).at[8:8 + N_EXPERTS].set(w_re[0].T)
    brt = jnp.zeros((ROUTE_ROWS, 1), F32).at[:N_GROUPS, 0].set(b_rg[0]).at[8:8 + N_EXPERTS, 0].set(b_re[0])
    x1, ua, ub, ri, rw, cnt = _merge(alpha, ret, mls, p_lat, (8, 9), x, g1, sh2, sc2,
                                     ln1_g[0][None, :], ln1_b[0][None, :],
                                     w_ret_branch[0].astype(BF16), w_ml_branch[0].astype(BF16),
                                     w_out[0].astype(BF16), wrt.astype(BF16), brt)

    counts = cnt[:, 0]
    padded = (counts + MOE_BLK - 1) // MOE_BLK * MOE_BLK
    pad_end = jnp.cumsum(padded)
    pad_off = pad_end - padded
    dest = _table_lookup(pad_off, ri[0:2]) + ri[2:4]
    n_blocks = (2 * n_tok) // MOE_BLK + N_EXPERTS
    n_slots = n_blocks * MOE_BLK
    block_start = jnp.arange(n_blocks, dtype=I32) * MOE_BLK
    block_exp = jnp.minimum((block_start[:, None] >= pad_end[None, :]).sum(1), N_EXPERTS - 1).astype(I32)
    n_valid = jnp.clip(_table_lookup(counts, block_exp) - (block_start - _table_lookup(pad_off, block_exp)),
                       0, MOE_BLK).astype(I32)

    xa, xb = _sc_scatter2(ua.reshape(n_tok, PACK_W), ub.reshape(n_tok, PACK_W), dest[0], dest[1], n_slots)
    ya, yb = _experts(block_exp, n_valid, xa, xb,
                      w_e1[0].astype(BF16), w_e3[0].astype(BF16), w_e2[0].astype(BF16))
    ga, gb = _sc_gather(ya, yb, dest.reshape(2 * n_tok))
    return _final(alpha, x1, ga, gb, rw, g2, ln2_g[0][None, :], ln2_b[0][None, :])
```

```python
import functools

import jax
import jax.numpy as jnp
from jax import lax
from jax.experimental import pallas as pl
from jax.experimental.pallas import tpu as pltpu
from jax.experimental.pallas import tpu_sc as plsc

F32 = jnp.float32
BF16 = jnp.bfloat16
U32 = jnp.uint32
I32 = jnp.int32
HIGHEST = lax.Precision.HIGHEST

HEADS = 4
HEAD_W = 256
BRANCH_W = HEADS * HEAD_W
GRID_W = 64
ROPE_BASE = 10000.0
N_GATES = 16
N_GK = N_GATES // HEADS
N_GROUPS = 4
EXP_PER_GROUP = 8
N_EXPERTS = N_GROUPS * EXP_PER_GROUP
LN_EPS = 1e-5
NEG_INF = -1e30
KEY_SCALE = HEAD_W ** -0.5

SCAN_L = 256
CONV_ROWS = 128
PROJ_TM = 2048
PROJ_SUB = 256
MERGE_TM = 512
MOE_BLK = 512
SC_WIN = 128
PACK_W = 256
ROUTE_ROWS = 64
N_TAB = 6
AUG_W = HEAD_W + 128
VMEM_LIMIT = 48 * 1024 * 1024

NT_DIMS = (((1,), (1,)), ((), ()))
TN_DIMS = (((0,), (0,)), ((), ()))


def _cparams(*sem):
    return pltpu.CompilerParams(dimension_semantics=sem, vmem_limit_bytes=VMEM_LIMIT)


def _layer_norm(x):
    mu = jnp.mean(x, axis=-1, keepdims=True)
    xc = x - mu
    var = jnp.mean(xc * xc, axis=-1, keepdims=True)
    return xc * lax.rsqrt(var + LN_EPS)


def _log_sigmoid(x):
    return jnp.minimum(x, 0.0) - jnp.log1p(jnp.exp(-jnp.abs(x)))


def _silu(x):
    return x * jax.nn.sigmoid(x)


def _pack_pairs(hi, lo):
    hb = lax.bitcast_convert_type(hi.astype(BF16).astype(F32), U32)
    lb = lax.bitcast_convert_type(lo.astype(BF16).astype(F32), U32)
    return (hb & jnp.uint32(0xFFFF0000)) | (lb >> 16)


def _unpack_pairs(p):
    hi = lax.bitcast_convert_type(p & jnp.uint32(0xFFFF0000), F32)
    lo = lax.bitcast_convert_type(p << 16, F32)
    return hi, lo


def _split3(x):
    hi = x.astype(BF16).astype(F32)
    r1 = x - hi
    mid = r1.astype(BF16).astype(F32)
    lo = (r1 - mid).astype(BF16).astype(F32)
    return jnp.concatenate([hi, mid, lo], axis=0).astype(BF16)


def _ada_kernel(c_ref, w_ref, b_ref, o_ref):
    s = _silu(c_ref[...])
    o_ref[...] = jnp.dot(s, w_ref[...], precision=HIGHEST, preferred_element_type=F32) + b_ref[...]


def _ada(cs, w, b):
    rows, d = cs.shape
    cols = w.shape[1]
    tn = 1024
    return pl.pallas_call(
        _ada_kernel,
        out_shape=jax.ShapeDtypeStruct((rows, cols), F32),
        grid=(cols // tn,),
        in_specs=[pl.BlockSpec((rows, d), lambda j: (0, 0)),
                  pl.BlockSpec((d, tn), lambda j: (0, j)),
                  pl.BlockSpec((1, tn), lambda j: (0, j))],
        out_specs=pl.BlockSpec((rows, tn), lambda j: (0, j)),
        compiler_params=_cparams("parallel"),
        name="ada",
    )(cs, w, b)


def _proj_kernel(kinds, x_ref, sh_ref, sc_ref, w_ref, wg_ref, bg_ref, *rest):
    if "rot" in kinds or "rot_scale" in kinds:
        cos_ref, sin_ref, o_ref, gt_ref, u_ref = rest
    else:
        o_ref, gt_ref, u_ref = rest
    j = pl.program_id(2)
    tm = x_ref.shape[1]
    sub = min(PROJ_SUB, tm)

    def rotary(acc, rows, scale):
        for s in range(acc.shape[1] // 128):
            a = acc[:, s * 128:(s + 1) * 128]
            half = s % 2
            cs = cos_ref[rows, half * 128:(half + 1) * 128]
            sn = sin_ref[rows, half * 128:(half + 1) * 128]
            r = a * cs + pltpu.roll(a, 64, 1) * sn
            if scale != 1.0:
                r = r * scale
            o_ref[0, rows, s * 128:(s + 1) * 128] = r.astype(BF16)

    def section(kind, first):
        for r in range(tm // sub):
            rows = slice(r * sub, (r + 1) * sub)
            if first:
                u = _layer_norm(x_ref[0, rows, :]) * (1.0 + sc_ref[0]) + sh_ref[0]
                ub = u.astype(BF16)
                u_ref[rows, :] = ub
                gt_ref[0, :, rows] = lax.dot_general(wg_ref[...], ub, NT_DIMS,
                                                     preferred_element_type=F32) + bg_ref[...]
            else:
                ub = u_ref[rows, :]
            acc = jnp.dot(ub, w_ref[...], preferred_element_type=F32)
            if kind == "rot":
                rotary(acc, rows, 1.0)
            elif kind == "rot_scale":
                rotary(acc, rows, KEY_SCALE)
            elif kind == "scale":
                o_ref[0, rows, :] = (acc * KEY_SCALE).astype(BF16)
            else:
                o_ref[0, rows, :] = acc.astype(BF16)

    variants = {}
    for s, kind in enumerate(kinds):
        variants.setdefault((kind, s == 0), []).append(s)
    for (kind, first), secs in variants.items():
        cond = functools.reduce(jnp.logical_or, [j == s for s in secs])

        @pl.when(cond)
        def _(kind=kind, first=first):
            section(kind, first)


def _proj(x, sh, sc, w_main, w_gate_t, b_gate, kinds, w_block, tables=None):
    B, T, D = x.shape
    n_sec = len(kinds)
    tm = min(PROJ_TM, T)
    tn = BRANCH_W
    assert T % tm == 0
    in_specs = [
        pl.BlockSpec((1, tm, D), lambda i, b, j: (b, i, 0)),
        pl.BlockSpec((1, 1, D), lambda i, b, j: (b, 0, 0)),
        pl.BlockSpec((1, 1, D), lambda i, b, j: (b, 0, 0)),
        pl.BlockSpec((D, tn), lambda i, b, j: (0, w_block(j))),
        pl.BlockSpec((N_GATES, D), lambda i, b, j: (0, 0)),
        pl.BlockSpec((N_GATES, 1), lambda i, b, j: (0, 0)),
    ]
    args = [x, sh, sc, w_main, w_gate_t, b_gate]
    if tables is not None:
        in_specs += [pl.BlockSpec((tm, HEAD_W), lambda i, b, j: (i, 0))] * 2
        args += list(tables)
    return pl.pallas_call(
        functools.partial(_proj_kernel, kinds),
        out_shape=(jax.ShapeDtypeStruct((B, T, n_sec * tn), BF16),
                   jax.ShapeDtypeStruct((B, N_GATES, T), F32)),
        grid=(T // tm, B, n_sec),
        in_specs=in_specs,
        out_specs=(pl.BlockSpec((1, tm, tn), lambda i, b, j: (b, i, j)),
                   pl.BlockSpec((1, N_GATES, tm), lambda i, b, j: (b, 0, i))),
        scratch_shapes=[pltpu.VMEM((tm, D), BF16)],
        compiler_params=_cparams("parallel", "parallel", "arbitrary"),
        name="proj_lat" if tables is not None else "proj_ctx",
    )(*args)


def _ret_build(dl_ref, q_ref, k_ref, v_ref, rg_ref, ck_ref, cv_ref, o_ref,
               sf_ref, sb_ref, fs_ref, bs_ref, dec_ref, d_ref):
    h = pl.program_id(0)
    L = SCAN_L
    n_chunks = q_ref.shape[1] // L
    n_ctx_chunks = ck_ref.shape[1] // L
    lgf = _log_sigmoid(jnp.full((1, 1), dl_ref[0, h], F32))
    lgb = _log_sigmoid(jnp.full((1, 1), dl_ref[1, h], F32))

    @pl.when(pl.program_id(1) == 0)
    def _():
        ri = lax.broadcasted_iota(I32, (L, L), 0)
        ci = lax.broadcasted_iota(I32, (L, L), 1)
        rel = (ri - ci).astype(F32)
        d_ref[...] = jnp.where(rel >= 0.0, jnp.exp(jnp.maximum(rel, 0.0) * lgf),
                               jnp.exp(jnp.maximum(-rel, 0.0) * lgb))
        row = lax.broadcasted_iota(I32, (L, HEAD_W), 0).astype(F32)
        dec_ref[0] = jnp.exp((row + 1.0) * lgf)
        dec_ref[1] = jnp.exp((L - 1.0 - row) * lgf)
        dec_ref[2] = jnp.exp((L - row) * lgb)
        dec_ref[3] = jnp.exp(row * lgb)

    cdf = jnp.exp(L * lgf)
    cdb = jnp.exp(L * lgb)

    def update(s_ref, kc, vc, kd, cd):
        kdec = (kc.astype(F32) * kd).astype(BF16)
        s_ref[...] = s_ref[...] * cd + lax.dot_general(kdec, vc, TN_DIMS, preferred_element_type=F32)

    sf_ref[...] = jnp.zeros_like(sf_ref)
    sb_ref[...] = jnp.zeros_like(sb_ref)
    for c in range(n_ctx_chunks):
        update(sf_ref, ck_ref[0, c * L:(c + 1) * L, :], cv_ref[0, c * L:(c + 1) * L, :], dec_ref[1], cdf)
    for c in reversed(range(n_ctx_chunks)):
        update(sb_ref, ck_ref[0, c * L:(c + 1) * L, :], cv_ref[0, c * L:(c + 1) * L, :], dec_ref[3], cdb)

    def state_pass(i, carry):
        cb = n_chunks - 1 - i
        rf = pl.multiple_of(i * L, L)
        rb = pl.multiple_of(cb * L, L)
        fs_ref[i] = sf_ref[...].astype(BF16)
        bs_ref[cb] = sb_ref[...].astype(BF16)
        update(sf_ref, k_ref[0, pl.ds(rf, L), :], v_ref[0, pl.ds(rf, L), :], dec_ref[1], cdf)
        update(sb_ref, k_ref[0, pl.ds(rb, L), :], v_ref[0, pl.ds(rb, L), :], dec_ref[3], cdb)
        return carry

    def finish_states():
        fs_ref[n_chunks - 1] = sf_ref[...].astype(BF16)
        bs_ref[0] = sb_ref[...].astype(BF16)

    def out_chunk(c):
        r0 = pl.multiple_of(c * L, L)
        q = q_ref[0, pl.ds(r0, L), :]
        k = k_ref[0, pl.ds(r0, L), :]
        v = v_ref[0, pl.ds(r0, L), :]
        s = lax.dot_general(q, k, NT_DIMS, preferred_element_type=F32)
        att = (s * d_ref[...]).astype(BF16)
        o = jnp.dot(att, v, preferred_element_type=F32)
        o = o + jnp.dot(q, fs_ref[c], preferred_element_type=F32) * dec_ref[0]
        o = o + jnp.dot(q, bs_ref[c], preferred_element_type=F32) * dec_ref[2]
        rg = rg_ref[0, pl.ds(r0, L), :].astype(F32)
        o_ref[0, pl.ds(r0, L), :] = (_layer_norm(o) * _silu(rg)).astype(BF16)

    return state_pass, finish_states, out_chunk


def _ret_scratch(n_chunks):
    return [pltpu.VMEM((HEAD_W, HEAD_W), F32),
            pltpu.VMEM((HEAD_W, HEAD_W), F32),
            pltpu.VMEM((n_chunks, HEAD_W, HEAD_W), BF16),
            pltpu.VMEM((n_chunks, HEAD_W, HEAD_W), BF16),
            pltpu.VMEM((4, SCAN_L, HEAD_W), F32),
            pltpu.VMEM((SCAN_L, SCAN_L), F32)]


def _mlstm_build(qp_ref, kp_ref, v_ref, mo_ref, ckp_ref, cv_ref, gt_ref, cgt_ref,
                 wq_ref, bq_ref, wk_ref, bk_ref, o_ref,
                 tab_ref, row_ref,
                 cf_ref, mf_ref, cb_ref, mb_ref, cfs_ref, mfs_ref, cbs_ref, mbs_ref, mask_ref,
                 xf_ref, q_ref, k_ref, ck_ref):
    L = SCAN_L
    T = qp_ref.shape[1]
    Tc = ckp_ref.shape[1]
    n_chunks = T // L
    n_ctx_chunks = Tc // L
    CV = CONV_ROWS

    def conv_stage(src_ref, t_len):
        xf_ref[pl.ds(0, 8), :] = jnp.zeros((8, HEAD_W), F32)
        xf_ref[pl.ds(8 + t_len, 8), :] = jnp.zeros((8, HEAD_W), F32)
        xf_ref[pl.ds(8, t_len), :] = src_ref[0].astype(F32)

    def conv_chunk(c, w, b, dst_ref, scale):
        r0 = pl.multiple_of(c * CV, CV)
        win = xf_ref[pl.ds(r0, CV + 16), :]
        prev = pltpu.roll(win, 1, 0)[8:8 + CV, :]
        cur = win[8:8 + CV, :]
        nxt = pltpu.roll(win, CV + 15, 0)[8:8 + CV, :]
        y = _silu(prev * w[0:1, :] + cur * w[1:2, :] + nxt * w[2:3, :] + b)
        if scale != 1.0:
            y = y * scale
        dst_ref[pl.ds(r0, CV), :] = y.astype(BF16)

    def conv_all(src_ref, w_ref, b_ref, dst_ref, t_len, scale):
        conv_stage(src_ref, t_len)
        w = w_ref[...]
        b = b_ref[...]

        def body(c, carry):
            conv_chunk(c, w, b, dst_ref, scale)
            return carry

        lax.fori_loop(0, t_len // CV, body, 0)

    conv_all(kp_ref, wk_ref, bk_ref, k_ref, T, KEY_SCALE)
    conv_all(ckp_ref, wk_ref, bk_ref, ck_ref, Tc, KEY_SCALE)
    conv_stage(qp_ref, T)
    q_per_chunk = L // CV

    def conv_q(c):
        for u in range(q_per_chunk):
            conv_chunk(c * q_per_chunk + u, wq_ref[...], bq_ref[...], q_ref, 1.0)

    ri = lax.broadcasted_iota(I32, (L, L), 0)
    ci = lax.broadcasted_iota(I32, (L, L), 1)
    tri_u = (ri <= ci).astype(BF16)
    lane8 = lax.broadcasted_iota(I32, (8, L), 1)
    sub8 = lax.broadcasted_iota(I32, (8, L), 0)
    sel_r = lax.broadcasted_iota(I32, (24, 8 * 128), 0) % 8
    sel_c = lax.broadcasted_iota(I32, (24, 8 * 128), 1) // 128
    sel3 = (sel_r == sel_c).astype(BF16)
    ones_cols = jnp.ones((L, AUG_W - HEAD_W), BF16)

    def chunk_tables(g8, n_used, state_only):
        i_f, i_b = g8[0], g8[2]
        lf_f, lf_b = _log_sigmoid(g8[1]), _log_sigmoid(g8[3])
        cs3 = jnp.dot(_split3(jnp.concatenate([lf_f, lf_b], axis=0)), tri_u,
                      preferred_element_type=F32)
        cs = cs3[0:16] + cs3[16:32] + cs3[32:48]
        b_f = cs[0:8]
        b_b = cs[8:16, L - 1:L] - cs[8:16] + lf_b
        z_f = i_f - b_f
        z_b = i_b - b_b
        g_f = b_f[:, L - 1:L] - b_f + i_f
        g_b = b_b[:, 0:1] - b_b + i_b
        mf, mb = z_f, z_b
        s = 1
        while s < L:
            mf = jnp.maximum(mf, jnp.where(lane8 >= s, pltpu.roll(mf, s, 1), NEG_INF))
            mb = jnp.maximum(mb, jnp.where(lane8 < L - s, pltpu.roll(mb, L - s, 1), NEG_INF))
            s *= 2
        mb = jnp.where(lane8 < L - 1, pltpu.roll(mb, L - 1, 1), NEG_INF)
        reps = [None if state_only and t not in (2, 5) else
                lax.dot_general(_split3(val), sel3[:, 0:n_used * 128], TN_DIMS, preferred_element_type=F32)
                for t, val in enumerate((mf, b_f, g_f, mb, b_b, g_b))]

        def rows_of(c):
            out = jnp.zeros((8, L), F32)
            for r, val in enumerate((z_f, z_b, g_f, g_b, b_f, b_b)):
                out = jnp.where(sub8 == r, val[c:c + 1], out)
            return out

        return rows_of, reps

    lat_rows, lat_reps = chunk_tables(gt_ref[0, 0], n_chunks, False)
    for c in range(n_chunks):
        row_ref[c] = lat_rows(c)
        for t in range(N_TAB):
            tab_ref[t, c * L:(c + 1) * L, :] = lat_reps[t][:, c * 128:(c + 1) * 128]

    def lanes2(x):
        return jnp.concatenate([x, x], axis=1)

    def advance(k, v, g_rep, g_row, b_last, c_ref, m_ref):
        m = m_ref[...]
        m_new = jnp.maximum(b_last + m, jnp.max(g_row, axis=-1, keepdims=True))
        kw = (k.astype(F32) * jnp.exp(lanes2(g_rep) - m_new)).astype(BF16)
        v_aug = jnp.concatenate([v, ones_cols], axis=1)
        c_ref[...] = jnp.exp(b_last + m - m_new) * c_ref[...] + lax.dot_general(
            kw, v_aug, TN_DIMS, preferred_element_type=F32)
        m_ref[...] = m_new

    for r in (cf_ref, mf_ref, cb_ref, mb_ref):
        r[...] = jnp.zeros_like(r)
    ctx_rows, ctx_reps = chunk_tables(cgt_ref[0, 0], n_ctx_chunks, True)
    for c in range(n_ctx_chunks):
        rows = ctx_rows(c)
        advance(ck_ref[c * L:(c + 1) * L, :], cv_ref[0, c * L:(c + 1) * L, :],
                ctx_reps[2][:, c * 128:(c + 1) * 128], rows[2:3], rows[4:5, L - 1:L], cf_ref, mf_ref)
    for c in reversed(range(n_ctx_chunks)):
        rows = ctx_rows(c)
        advance(ck_ref[c * L:(c + 1) * L, :], cv_ref[0, c * L:(c + 1) * L, :],
                ctx_reps[5][:, c * 128:(c + 1) * 128], rows[3:4], rows[5:6, 0:1], cb_ref, mb_ref)

    def state_pass(i, carry):
        cb = n_chunks - 1 - i
        rf = pl.multiple_of(i * L, L)
        rb = pl.multiple_of(cb * L, L)
        cfs_ref[i] = cf_ref[...].astype(BF16)
        mfs_ref[i] = mf_ref[...]
        cbs_ref[cb] = cb_ref[...].astype(BF16)
        mbs_ref[cb] = mb_ref[...]
        rows_f = row_ref[i]
        rows_b = row_ref[cb]
        advance(k_ref[pl.ds(rf, L), :], v_ref[0, pl.ds(rf, L), :], tab_ref[2, pl.ds(rf, L), :],
                rows_f[2:3], rows_f[4:5, L - 1:L], cf_ref, mf_ref)
        advance(k_ref[pl.ds(rb, L), :], v_ref[0, pl.ds(rb, L), :], tab_ref[5, pl.ds(rb, L), :],
                rows_b[3:4], rows_b[5:6, 0:1], cb_ref, mb_ref)
        conv_q(i)
        return carry

    def finish_states():
        conv_q(n_chunks - 1)
        cfs_ref[n_chunks - 1] = cf_ref[...].astype(BF16)
        mfs_ref[n_chunks - 1] = mf_ref[...]
        cbs_ref[0] = cb_ref[...].astype(BF16)
        mbs_ref[0] = mb_ref[...]

    def direction(q, v_aug, s, z_row, zmax_rep, b_rep, mask, c_in, m_in):
        mx = jnp.maximum(zmax_rep, m_in)
        att = s * jnp.exp((z_row - lanes2(mx)) + mask)
        na = jnp.dot(att.astype(BF16), v_aug, preferred_element_type=F32)
        qa = jnp.dot(q, c_in, preferred_element_type=F32)
        a = jnp.exp(m_in - mx)
        num = na[:, 0:HEAD_W] + lanes2(a) * qa[:, 0:HEAD_W]
        den = na[:, HEAD_W:] + a * qa[:, HEAD_W:]
        scale = 1.0 / jnp.maximum(jnp.abs(den), jnp.exp(-(b_rep + mx)))
        return num * lanes2(scale)

    @pl.when(pl.program_id(1) == 0)
    def _():
        mask_ref[0] = jnp.where(ci <= ri, 0.0, NEG_INF)
        mask_ref[1] = jnp.where(ci > ri, 0.0, NEG_INF)

    def out_chunk(c):
        r0 = pl.multiple_of(c * L, L)
        q = q_ref[pl.ds(r0, L), :]
        k = k_ref[pl.ds(r0, L), :]
        v_aug = jnp.concatenate([v_ref[0, pl.ds(r0, L), :], ones_cols], axis=1)
        s = lax.dot_general(q, k, NT_DIMS, preferred_element_type=F32)
        rows = row_ref[c]
        tot = direction(q, v_aug, s, rows[0:1], tab_ref[0, pl.ds(r0, L), :], tab_ref[1, pl.ds(r0, L), :],
                        mask_ref[0], cfs_ref[c], mfs_ref[c])
        tot = tot + direction(q, v_aug, s, rows[1:2], tab_ref[3, pl.ds(r0, L), :],
                              tab_ref[4, pl.ds(r0, L), :], mask_ref[1], cbs_ref[c], mbs_ref[c])
        mo = mo_ref[0, pl.ds(r0, L), :].astype(F32)
        o_ref[0, pl.ds(r0, L), :] = (_layer_norm(tot) * jax.nn.sigmoid(mo)).astype(BF16)

    return state_pass, finish_states, out_chunk


def _mlstm_scratch(T, Tc, n_chunks):
    state = [pltpu.VMEM((HEAD_W, AUG_W), F32), pltpu.VMEM((1, 1), F32)]
    snaps = [pltpu.VMEM((n_chunks, HEAD_W, AUG_W), BF16), pltpu.VMEM((n_chunks, 1, 1), F32)]
    return [pltpu.VMEM((N_TAB, T, 128), F32), pltpu.VMEM((n_chunks, 8, SCAN_L), F32)] \
        + state + state + snaps + snaps + [pltpu.VMEM((2, SCAN_L, SCAN_L), F32)] \
        + [pltpu.VMEM((T + 16, HEAD_W), F32), pltpu.VMEM((T, HEAD_W), BF16),
           pltpu.VMEM((T, HEAD_W), BF16), pltpu.VMEM((Tc, HEAD_W), BF16)]


def _scan_kernel(n_ret_scratch, dl_ref, rq_ref, rk_ref, rv_ref, rg_ref, rck_ref, rcv_ref,
                 mq_ref, mk_ref, mv_ref, mo_ref, mck_ref, mcv_ref, gt_ref, cgt_ref,
                 wq_ref, bq_ref, wk_ref, bk_ref, r_ref, m_ref, *scratch):
    n_chunks = rq_ref.shape[1] // SCAN_L
    ret = _ret_build(dl_ref, rq_ref, rk_ref, rv_ref, rg_ref, rck_ref, rcv_ref, r_ref,
                     *scratch[:n_ret_scratch])
    mls = _mlstm_build(mq_ref, mk_ref, mv_ref, mo_ref, mck_ref, mcv_ref, gt_ref, cgt_ref,
                       wq_ref, bq_ref, wk_ref, bk_ref, m_ref, *scratch[n_ret_scratch:])

    def state_pass(i, carry):
        ret[0](i, carry)
        mls[0](i, carry)
        return carry

    lax.fori_loop(0, n_chunks - 1, state_pass, 0)
    ret[1]()
    mls[1]()

    def out_pass(i, carry):
        for c in (2 * i, 2 * i + 1):
            ret[2](c)
            mls[2](c)
        return carry

    lax.fori_loop(0, n_chunks // 2, out_pass, 0)


def _scans(decay_logit, p_lat, p_ctx, gt, cgt, conv_w, conv_b, ret_lat, ret_ctx, ml_lat, ml_ctx):
    B, T, _ = p_lat.shape
    Tc = p_ctx.shape[1]
    assert T % (2 * SCAN_L) == 0 and Tc % SCAN_L == 0 and T // SCAN_L <= 8
    n_chunks = T // SCAN_L

    def lat(sec):
        return pl.BlockSpec((1, T, HEAD_W), lambda h, b: (b, 0, sec * HEADS + h))

    def cx(sec):
        return pl.BlockSpec((1, Tc, HEAD_W), lambda h, b: (b, 0, sec * HEADS + h))

    gates = pl.BlockSpec((1, 1, N_GK, 8, SCAN_L), lambda h, b: (b, h, 0, 0, 0))
    out = pl.BlockSpec((1, T, HEAD_W), lambda h, b: (b, 0, h))
    conv_specs = [pl.BlockSpec((3, HEAD_W), lambda h, b: (0, h)),
                  pl.BlockSpec((1, HEAD_W), lambda h, b: (0, h)),
                  pl.BlockSpec((3, HEAD_W), lambda h, b: (0, HEADS + h)),
                  pl.BlockSpec((1, HEAD_W), lambda h, b: (0, HEADS + h))]
    ret_scratch = _ret_scratch(n_chunks)
    return pl.pallas_call(
        functools.partial(_scan_kernel, len(ret_scratch)),
        out_shape=(jax.ShapeDtypeStruct((B, T, BRANCH_W), BF16),
                   jax.ShapeDtypeStruct((B, T, BRANCH_W), BF16)),
        grid=(HEADS, B),
        in_specs=[pl.BlockSpec(memory_space=pltpu.SMEM)]
        + [lat(s) for s in ret_lat] + [cx(s) for s in ret_ctx]
        + [lat(s) for s in ml_lat] + [cx(s) for s in ml_ctx] + [gates, gates] + conv_specs,
        out_specs=(out, out),
        scratch_shapes=ret_scratch + _mlstm_scratch(T, Tc, n_chunks),
        compiler_params=_cparams("arbitrary", "arbitrary"),
        name="scans",
    )(decay_logit, *([p_lat] * 4), *([p_ctx] * 2), *([p_lat] * 4), *([p_ctx] * 2), gt, cgt,
      conv_w, conv_b, conv_w, conv_b)


def _merge_kernel(alpha, r_ref, m_ref, gr_ref, gm_ref, x_ref, g1_ref, sh2_ref, sc2_ref,
                  lng_ref, lnb_ref, wr_ref, wm_ref, wo_ref, wrt_ref, brt_ref,
                  x1_ref, ua_ref, ub_ref, ri_ref, rw_ref, cnt_ref, carry_ref, u_ref):
    @pl.when(jnp.logical_and(pl.program_id(0) == 0, pl.program_id(1) == 0))
    def _():
        carry_ref[...] = jnp.zeros_like(carry_ref)
        tm = x_ref.shape[1]
        r = lax.broadcasted_iota(I32, (tm, tm), 0)
        c = lax.broadcasted_iota(I32, (tm, tm), 1)
        u_ref[...] = (r < c).astype(BF16)

    yr = jnp.dot(r_ref[0], wr_ref[...], preferred_element_type=F32)
    ym = jnp.dot(m_ref[0], wm_ref[...], preferred_element_type=F32)
    y = jax.nn.sigmoid(gr_ref[0].astype(F32)) * yr + jax.nn.sigmoid(gm_ref[0].astype(F32)) * ym
    yo = jnp.dot(y.astype(BF16), wo_ref[...], preferred_element_type=F32)
    x1 = _layer_norm(alpha * x_ref[0] + g1_ref[0] * yo) * lng_ref[...] + lnb_ref[...]
    x1_ref[0] = x1
    u2 = _layer_norm(x1) * (1.0 + sc2_ref[0]) + sh2_ref[0]
    ua_ref[0] = _pack_pairs(u2[:, 0:PACK_W], u2[:, PACK_W:2 * PACK_W])
    ub_ref[0] = _pack_pairs(u2[:, 2 * PACK_W:3 * PACK_W], u2[:, 3 * PACK_W:4 * PACK_W])
    lt = lax.dot_general(wrt_ref[...], u2.astype(BF16), NT_DIMS, preferred_element_type=F32) + brt_ref[...]
    _route_tile(lt, ri_ref, rw_ref, cnt_ref, carry_ref, u_ref)


def _route_tile(lt, ri_ref, rw_ref, cnt_ref, carry_ref, u_ref):
    tm = lt.shape[1]
    lg = lt[0:N_GROUPS, :]
    eg = jnp.exp(lg - jnp.max(lg, axis=0, keepdims=True))
    pg = eg / jnp.sum(eg, axis=0, keepdims=True)
    pg_top = jnp.max(pg, axis=0, keepdims=True)
    rows_g = lax.broadcasted_iota(I32, pg.shape, 0)
    g_idx = jnp.min(jnp.where(pg == pg_top, rows_g, N_GROUPS), axis=0, keepdims=True)

    le = jnp.zeros((EXP_PER_GROUP, tm), F32)
    for g in range(N_GROUPS):
        lo = 8 + g * EXP_PER_GROUP
        le = jnp.where(g_idx == g, lt[lo:lo + EXP_PER_GROUP, :], le)
    ee = jnp.exp(le - jnp.max(le, axis=0, keepdims=True))
    pe = ee / jnp.sum(ee, axis=0, keepdims=True)
    rows_e = lax.broadcasted_iota(I32, pe.shape, 0)
    v1 = jnp.max(pe, axis=0, keepdims=True)
    i1 = jnp.min(jnp.where(pe == v1, rows_e, EXP_PER_GROUP), axis=0, keepdims=True)
    pe2 = jnp.where(rows_e == i1, -1.0, pe)
    v2 = jnp.max(pe2, axis=0, keepdims=True)
    i2 = jnp.min(jnp.where(pe2 == v2, rows_e, EXP_PER_GROUP), axis=0, keepdims=True)
    den = v1 + v2
    rw_ref[...] = jnp.zeros_like(rw_ref)
    rw_ref[0:1, :] = pg_top * v1 / den
    rw_ref[1:2, :] = pg_top * v2 / den
    e1 = g_idx * EXP_PER_GROUP + i1
    e2 = g_idx * EXP_PER_GROUP + i2

    rows_x = lax.broadcasted_iota(I32, (N_EXPERTS, tm), 0)
    oh1 = (rows_x == e1).astype(F32)
    oh2 = (rows_x == e2).astype(F32)
    both = oh1 + oh2
    before = carry_ref[:, 0:1] + jnp.dot(both.astype(BF16), u_ref[...], preferred_element_type=F32)
    ri_ref[0:1, :] = e1
    ri_ref[1:2, :] = e2
    ri_ref[2:3, :] = jnp.sum(oh1 * before, axis=0, keepdims=True).astype(I32)
    ri_ref[3:4, :] = jnp.sum(oh2 * before, axis=0, keepdims=True).astype(I32)
    carry_ref[...] = carry_ref[...] + jnp.sum(both, axis=1, keepdims=True)
    cnt_ref[...] = carry_ref[...].astype(I32)


def _merge(alpha, r, m, p_lat, sec_gates, x, g1, sh2, sc2, lng, lnb, wr, wm, wo, wrt, brt):
    B, T, D = x.shape
    tm = MERGE_TM
    per_b = T // tm
    n = B * T

    def tile(w):
        return pl.BlockSpec((1, tm, w), lambda b, i: (b, i, 0))

    def sec(s):
        return pl.BlockSpec((1, tm, BRANCH_W), lambda b, i: (b, i, s))

    def mod():
        return pl.BlockSpec((1, 1, D), lambda b, i: (b, 0, 0))

    def const(shape):
        return pl.BlockSpec(shape, lambda b, i: (0,) * len(shape))

    return pl.pallas_call(
        functools.partial(_merge_kernel, alpha),
        out_shape=(jax.ShapeDtypeStruct((B, T, D), F32),
                   jax.ShapeDtypeStruct((B, T, PACK_W), U32),
                   jax.ShapeDtypeStruct((B, T, PACK_W), U32),
                   jax.ShapeDtypeStruct((4, n), I32),
                   jax.ShapeDtypeStruct((8, n), F32),
                   jax.ShapeDtypeStruct((N_EXPERTS, 128), I32)),
        grid=(B, per_b),
        in_specs=[tile(BRANCH_W), tile(BRANCH_W), sec(sec_gates[0]), sec(sec_gates[1]), tile(D),
                  mod(), mod(), mod(), const((1, D)), const((1, D)),
                  const((BRANCH_W, D)), const((BRANCH_W, D)), const((D, D)),
                  const((ROUTE_ROWS, D)), const((ROUTE_ROWS, 1))],
        out_specs=(tile(D), tile(PACK_W), tile(PACK_W),
                   pl.BlockSpec((4, tm), lambda b, i: (0, b * per_b + i)),
                   pl.BlockSpec((8, tm), lambda b, i: (0, b * per_b + i)),
                   pl.BlockSpec((N_EXPERTS, 128), lambda b, i: (0, 0))),
        scratch_shapes=[pltpu.VMEM((N_EXPERTS, 128), F32), pltpu.VMEM((tm, tm), BF16)],
        compiler_params=_cparams("arbitrary", "arbitrary"),
        name="merge",
    )(r, m, p_lat, p_lat, x, g1, sh2, sc2, lng, lnb, wr, wm, wo, wrt, brt)


def _sc_mesh():
    return plsc.VectorSubcoreMesh(core_axis_name="c", subcore_axis_name="s")


def _sc_scatter2(rows_a, rows_b, idx0, idx1, n_out):
    m, w = rows_a.shape
    out = jax.ShapeDtypeStruct((n_out, w), rows_a.dtype)

    @functools.partial(pl.kernel, out_type=(out, out), mesh=_sc_mesh(), scratch_types=[])
    def k(xa_hbm, xb_hbm, i0_hbm, i1_hbm, oa_hbm, ob_hbm):
        for x_hbm, o_hbm in ((xa_hbm, oa_hbm), (xb_hbm, ob_hbm)):
            def body(x_vmem, i0_vmem, i1_vmem, o_hbm=o_hbm):
                pltpu.sync_copy(x_vmem, o_hbm.at[i0_vmem.at[0]])
                pltpu.sync_copy(x_vmem, o_hbm.at[i1_vmem.at[0]])

            pltpu.emit_pipeline(
                body,
                grid=(m // SC_WIN,),
                in_specs=[pl.BlockSpec((SC_WIN, w), lambda i: (i, 0)),
                          pl.BlockSpec((1, SC_WIN), lambda i: (0, i)),
                          pl.BlockSpec((1, SC_WIN), lambda i: (0, i))],
                out_specs=[],
                core_axis_name=("c", "s"),
                dimension_semantics=(pltpu.PARALLEL,),
            )(x_hbm, i0_hbm, i1_hbm)

    return k(rows_a, rows_b, idx0.reshape(1, m), idx1.reshape(1, m))


def _sc_gather(table_a, table_b, idx):
    m = idx.shape[0]
    w = table_a.shape[1]
    out = jax.ShapeDtypeStruct((m, w), table_a.dtype)

    @functools.partial(pl.kernel, out_type=(out, out), mesh=_sc_mesh(), scratch_types=[])
    def k(ta_hbm, tb_hbm, i_hbm, oa_hbm, ob_hbm):
        for t_hbm, o_hbm in ((ta_hbm, oa_hbm), (tb_hbm, ob_hbm)):
            def body(i_vmem, o_vmem, t_hbm=t_hbm):
                pltpu.sync_copy(t_hbm.at[i_vmem.at[0]], o_vmem)

            pltpu.emit_pipeline(
                body,
                grid=(m // SC_WIN,),
                in_specs=[pl.BlockSpec((1, SC_WIN), lambda i: (0, i))],
                out_specs=[pl.BlockSpec((SC_WIN, w), lambda i: (i, 0))],
                core_axis_name=("c", "s"),
                dimension_semantics=(pltpu.PARALLEL,),
            )(i_hbm, o_hbm)

    return k(table_a, table_b, idx.reshape(1, m))


def _expert_kernel(be_ref, nv_ref, xa_ref, xb_ref, w1f_ref, w3f_ref, w2f_ref, ya_ref, yb_ref,
                   w1_ref, w3_ref, w2_ref):
    j = pl.program_id(0)
    nv = nv_ref[j]

    @pl.when(jnp.logical_or(j == 0, be_ref[j] != be_ref[jnp.maximum(j - 1, 0)]))
    def _():
        w1_ref[0] = w1f_ref[0].astype(BF16)
        w3_ref[0] = w3f_ref[0].astype(BF16)
        w2_ref[0] = w2f_ref[0].astype(BF16)

    @pl.when(nv > 0)
    def _():
        valid = lax.broadcasted_iota(I32, xa_ref.shape, 0) < nv
        zero = jnp.zeros(xa_ref.shape, U32)
        parts = _unpack_pairs(jnp.where(valid, xa_ref[...], zero)) + \
            _unpack_pairs(jnp.where(valid, xb_ref[...], zero))
        x = jnp.concatenate([p.astype(BF16) for p in parts], axis=1)
        h1 = jnp.dot(x, w1_ref[0], preferred_element_type=F32)
        h3 = jnp.dot(x, w3_ref[0], preferred_element_type=F32)
        y = jnp.dot((_silu(h1) * h3).astype(BF16), w2_ref[0], preferred_element_type=F32)
        ya_ref[...] = _pack_pairs(y[:, 0:PACK_W], y[:, PACK_W:2 * PACK_W])
        yb_ref[...] = _pack_pairs(y[:, 2 * PACK_W:3 * PACK_W], y[:, 3 * PACK_W:4 * PACK_W])

    @pl.when(nv == 0)
    def _():
        ya_ref[...] = jnp.zeros_like(ya_ref)
        yb_ref[...] = jnp.zeros_like(yb_ref)


def _experts(block_exp, n_valid, xa, xb, w1, w3, w2):
    n_slots = xa.shape[0]
    n_blocks = n_slots // MOE_BLK
    d, de = w1.shape[1], w1.shape[2]
    slot = pl.BlockSpec((MOE_BLK, PACK_W), lambda j, be, nv: (j, 0))
    grid_spec = pltpu.PrefetchScalarGridSpec(
        num_scalar_prefetch=2,
        grid=(n_blocks,),
        in_specs=[slot, slot,
                  pl.BlockSpec((1, d, de), lambda j, be, nv: (be[j], 0, 0)),
                  pl.BlockSpec((1, d, de), lambda j, be, nv: (be[j], 0, 0)),
                  pl.BlockSpec((1, de, d), lambda j, be, nv: (be[j], 0, 0))],
        out_specs=(slot, slot),
        scratch_shapes=[pltpu.VMEM((1, d, de), BF16), pltpu.VMEM((1, d, de), BF16),
                        pltpu.VMEM((1, de, d), BF16)],
    )
    return pl.pallas_call(
        _expert_kernel,
        out_shape=(jax.ShapeDtypeStruct((n_slots, PACK_W), U32),
                   jax.ShapeDtypeStruct((n_slots, PACK_W), U32)),
        grid_spec=grid_spec,
        compiler_params=_cparams("arbitrary"),
        name="experts",
    )(block_exp, n_valid, xa, xb, w1, w3, w2)


def _final_kernel(alpha, x1_ref, a0_ref, b0_ref, a1_ref, b1_ref, w_ref, g2_ref, lng_ref, lnb_ref, o_ref):
    w = w_ref[...].T
    w0 = w[:, 0:1]
    w1 = w[:, 1:2]
    parts0 = _unpack_pairs(a0_ref[...]) + _unpack_pairs(b0_ref[...])
    parts1 = _unpack_pairs(a1_ref[...]) + _unpack_pairs(b1_ref[...])
    f = jnp.concatenate([w0 * p0 + w1 * p1 for p0, p1 in zip(parts0, parts1)], axis=1)
    o_ref[0] = _layer_norm(alpha * x1_ref[0] + g2_ref[0] * f) * lng_ref[...] + lnb_ref[...]


def _final(alpha, x1, ya, yb, w, g2, lng, lnb):
    B, T, D = x1.shape
    tm = MERGE_TM
    per_b = T // tm
    n_tiles = B * per_b

    def rows(k):
        return pl.BlockSpec((tm, PACK_W), lambda b, i: (k * n_tiles + b * per_b + i, 0))

    return pl.pallas_call(
        functools.partial(_final_kernel, alpha),
        out_shape=jax.ShapeDtypeStruct((B, T, D), F32),
        grid=(B, per_b),
        in_specs=[pl.BlockSpec((1, tm, D), lambda b, i: (b, i, 0)),
                  rows(0), rows(0), rows(1), rows(1),
                  pl.BlockSpec((8, tm), lambda b, i: (0, b * per_b + i)),
                  pl.BlockSpec((1, 1, D), lambda b, i: (b, 0, 0)),
                  pl.BlockSpec((1, D), lambda b, i: (0, 0)),
                  pl.BlockSpec((1, D), lambda b, i: (0, 0))],
        out_specs=pl.BlockSpec((1, tm, D), lambda b, i: (b, i, 0)),
        compiler_params=_cparams("parallel", "parallel"),
        name="final",
    )(x1, ya, yb, ya, yb, w, g2, lng, lnb)


def _rotary_tables(T):
    quarter = HEAD_W // 4
    freqs = ROPE_BASE ** (-jnp.arange(quarter, dtype=F32) / quarter)
    t = jnp.arange(T)
    ang_r = (t // GRID_W).astype(F32)[:, None] * freqs[None, :]
    ang_c = (t % GRID_W).astype(F32)[:, None] * freqs[None, :]
    cos = jnp.concatenate([jnp.cos(ang_r)] * 2 + [jnp.cos(ang_c)] * 2, axis=1)
    sin = jnp.concatenate([-jnp.sin(ang_r), jnp.sin(ang_r), -jnp.sin(ang_c), jnp.sin(ang_c)], axis=1)
    return cos, sin


def _per_head_gates(gt):
    B, _, T = gt.shape
    n_chunks = T // SCAN_L
    gth = gt.reshape(B, N_GK, HEADS, n_chunks, SCAN_L).transpose(0, 2, 1, 3, 4)
    return jnp.pad(gth, ((0, 0), (0, 0), (0, 0), (0, 8 - n_chunks), (0, 0)))


def _table_lookup(table, idx):
    sel = idx[..., None] == jnp.arange(table.shape[0], dtype=idx.dtype)
    return jnp.sum(jnp.where(sel, table, 0), axis=-1)


def kernel(x, c, ctx, c_ctx, w_ada, b_ada, w_in, b_mgate, ml_conv_w, ml_conv_b, ret_decay_logit, w_ret_branch, w_ml_branch, w_out, ln1_g, ln1_b, w_rg, b_rg, w_re, b_re, w_e1, w_e3, w_e2, ln2_g, ln2_b):
    B, T, D = x.shape
    depth = w_ada.shape[0]
    assert depth == 1 and D == BRANCH_W and T % GRID_W == 0
    alpha = (2 * depth) ** 0.25
    n_tok = B * T

    n_rows = -(-(B + 1) // 8) * 8
    cs = jnp.zeros((n_rows, D), F32).at[:B].set(c).at[B].set(c_ctx)
    mod = _ada(cs, w_ada[0], b_ada[0][None, :])
    sh1, sc1, g1, sh2, sc2, g2 = [mod[:B, None, i * D:(i + 1) * D] for i in range(6)]
    csh1 = mod[B, 0 * D:1 * D].reshape(1, 1, D)
    csc1 = mod[B, 1 * D:2 * D].reshape(1, 1, D)

    w = w_in[0]
    sec_w = [w[:, s * BRANCH_W:(s + 1) * BRANCH_W] for s in range(8)]
    g_lo = 8 * BRANCH_W
    w_gate_t = w[:, g_lo:g_lo + N_GATES].T.astype(BF16)
    b_gate = b_mgate[0][:, None]
    sec_w += [w[:, g_lo + N_GATES:g_lo + N_GATES + D], w[:, g_lo + N_GATES + D:]]
    w_sec = jnp.concatenate(sec_w, axis=1).astype(BF16)
    kinds_lat = ("rot", "rot_scale") + ("plain",) * 8
    kinds_ctx = ("scale", "plain", "plain", "plain")
    p_lat, gt_lat = _proj(x, sh1, sc1, w_sec, w_gate_t, b_gate, kinds_lat, lambda j: j, _rotary_tables(T))
    Tc = ctx.shape[1]
    p_ctx, gt_ctx = _proj(ctx.reshape(1, B * Tc, D), csh1, csc1, w_sec, w_gate_t, b_gate, kinds_ctx,
                          lambda j: j + 1 + 2 * (j // 2))
    p_ctx = p_ctx.reshape(B, Tc, -1)
    gt_ctx = gt_ctx.reshape(N_GATES, B, Tc).transpose(1, 0, 2)

    ret, mls = _scans(ret_decay_logit[0], p_lat, p_ctx, _per_head_gates(gt_lat), _per_head_gates(gt_ctx),
                      ml_conv_w[0], ml_conv_b[0][None, :], (0, 1, 2, 3), (0, 1), (4, 5, 6, 7), (2, 3))

    wrt = jnp.zeros((ROUTE_ROWS, D), F32).at[:N_GROUPS].set(w_rg[0].T).at[8:8 + N_EXPERTS].set(w_re[0].T)
    brt = jnp.zeros((ROUTE_ROWS, 1), F32).at[:N_GROUPS, 0].set(b_rg[0]).at[8:8 + N_EXPERTS, 0].set(b_re[0])
    x1, ua, ub, ri, rw, cnt = _merge(alpha, ret, mls, p_lat, (8, 9), x, g1, sh2, sc2,
                                     ln1_g[0][None, :], ln1_b[0][None, :],
                                     w_ret_branch[0].astype(BF16), w_ml_branch[0].astype(BF16),
                                     w_out[0].astype(BF16), wrt.astype(BF16), brt)

    counts = cnt[:, 0]
    padded = (counts + MOE_BLK - 1) // MOE_BLK * MOE_BLK
    pad_end = jnp.cumsum(padded)
    pad_off = pad_end - padded
    dest = _table_lookup(pad_off, ri[0:2]) + ri[2:4]
    n_blocks = (2 * n_tok) // MOE_BLK + N_EXPERTS
    n_slots = n_blocks * MOE_BLK
    block_start = jnp.arange(n_blocks, dtype=I32) * MOE_BLK
    block_exp = jnp.minimum((block_start[:, None] >= pad_end[None, :]).sum(1), N_EXPERTS - 1).astype(I32)
    n_valid = jnp.clip(_table_lookup(counts, block_exp) - (block_start - _table_lookup(pad_off, block_exp)),
                       0, MOE_BLK).astype(I32)

    xa, xb = _sc_scatter2(ua.reshape(n_tok, PACK_W), ub.reshape(n_tok, PACK_W), dest[0], dest[1], n_slots)
    ya, yb = _experts(block_exp, n_valid, xa, xb, w_e1[0], w_e3[0], w_e2[0])
    ga, gb = _sc_gather(ya, yb, dest.reshape(2 * n_tok))
    return _final(alpha, x1, ga, gb, rw, g2, ln2_g[0][None, :], ln2_b[0][None, :])
```

```python
import functools

import jax
import jax.numpy as jnp
from jax import lax
from jax.experimental import pallas as pl
from jax.experimental.pallas import tpu as pltpu
from jax.experimental.pallas import tpu_sc as plsc

F32 = jnp.float32
BF16 = jnp.bfloat16
U32 = jnp.uint32
I32 = jnp.int32
HIGHEST = lax.Precision.HIGHEST

HEADS = 4
HEAD_W = 256
BRANCH_W = HEADS * HEAD_W
GRID_W = 64
ROPE_BASE = 10000.0
N_GATES = 16
N_GK = N_GATES // HEADS
N_GROUPS = 4
EXP_PER_GROUP = 8
N_EXPERTS = N_GROUPS * EXP_PER_GROUP
LN_EPS = 1e-5
NEG_INF = -1e30
KEY_SCALE = HEAD_W ** -0.5

SCAN_L = 256
CONV_ROWS = 128
PROJ_TM = 2048
PROJ_SUB = 256
MERGE_TM = 512
MOE_BLK = 512
SC_WIN = 128
PACK_W = 256
ROUTE_ROWS = 64
N_TAB = 6
AUG_W = HEAD_W + 128
VMEM_LIMIT = 48 * 1024 * 1024

NT_DIMS = (((1,), (1,)), ((), ()))
TN_DIMS = (((0,), (0,)), ((), ()))


def _cparams(*sem):
    return pltpu.CompilerParams(dimension_semantics=sem, vmem_limit_bytes=VMEM_LIMIT)


def _layer_norm(x):
    mu = jnp.mean(x, axis=-1, keepdims=True)
    xc = x - mu
    var = jnp.mean(xc * xc, axis=-1, keepdims=True)
    return xc * lax.rsqrt(var + LN_EPS)


def _log_sigmoid(x):
    return jnp.minimum(x, 0.0) - jnp.log1p(jnp.exp(-jnp.abs(x)))


def _silu(x):
    return x * jax.nn.sigmoid(x)


def _pack_pairs(hi, lo):
    hb = lax.bitcast_convert_type(hi.astype(BF16).astype(F32), U32)
    lb = lax.bitcast_convert_type(lo.astype(BF16).astype(F32), U32)
    return (hb & jnp.uint32(0xFFFF0000)) | (lb >> 16)


def _unpack_pairs(p):
    hi = lax.bitcast_convert_type(p & jnp.uint32(0xFFFF0000), F32)
    lo = lax.bitcast_convert_type(p << 16, F32)
    return hi, lo


def _split3(x):
    hi = x.astype(BF16).astype(F32)
    r1 = x - hi
    mid = r1.astype(BF16).astype(F32)
    lo = (r1 - mid).astype(BF16).astype(F32)
    return jnp.concatenate([hi, mid, lo], axis=0).astype(BF16)


def _ada_kernel(c_ref, w_ref, b_ref, o_ref):
    s = _silu(c_ref[...])
    o_ref[...] = jnp.dot(s, w_ref[...], precision=HIGHEST, preferred_element_type=F32) + b_ref[...]


def _ada(cs, w, b):
    rows, d = cs.shape
    cols = w.shape[1]
    tn = 1024
    return pl.pallas_call(
        _ada_kernel,
        out_shape=jax.ShapeDtypeStruct((rows, cols), F32),
        grid=(cols // tn,),
        in_specs=[pl.BlockSpec((rows, d), lambda j: (0, 0)),
                  pl.BlockSpec((d, tn), lambda j: (0, j)),
                  pl.BlockSpec((1, tn), lambda j: (0, j))],
        out_specs=pl.BlockSpec((rows, tn), lambda j: (0, j)),
        compiler_params=_cparams("parallel"),
        name="ada",
    )(cs, w, b)


def _proj_kernel(kinds, x_ref, sh_ref, sc_ref, w_ref, wg_ref, bg_ref, *rest):
    if "rot" in kinds or "rot_scale" in kinds:
        cos_ref, sin_ref, o_ref, gt_ref, u_ref = rest
    else:
        o_ref, gt_ref, u_ref = rest
    j = pl.program_id(2)
    tm = x_ref.shape[1]
    sub = min(PROJ_SUB, tm)

    def rotary(acc, rows, scale):
        for s in range(acc.shape[1] // 128):
            a = acc[:, s * 128:(s + 1) * 128]
            half = s % 2
            cs = cos_ref[rows, half * 128:(half + 1) * 128]
            sn = sin_ref[rows, half * 128:(half + 1) * 128]
            r = a * cs + pltpu.roll(a, 64, 1) * sn
            if scale != 1.0:
                r = r * scale
            o_ref[0, rows, s * 128:(s + 1) * 128] = r.astype(BF16)

    def section(kind, first):
        for r in range(tm // sub):
            rows = slice(r * sub, (r + 1) * sub)
            if first:
                u = _layer_norm(x_ref[0, rows, :]) * (1.0 + sc_ref[0]) + sh_ref[0]
                ub = u.astype(BF16)
                u_ref[rows, :] = ub
                gt_ref[0, :, rows] = lax.dot_general(wg_ref[...], ub, NT_DIMS,
                                                     preferred_element_type=F32) + bg_ref[...]
            else:
                ub = u_ref[rows, :]
            acc = jnp.dot(ub, w_ref[...], preferred_element_type=F32)
            if kind == "rot":
                rotary(acc, rows, 1.0)
            elif kind == "rot_scale":
                rotary(acc, rows, KEY_SCALE)
            elif kind == "scale":
                o_ref[0, rows, :] = (acc * KEY_SCALE).astype(BF16)
            else:
                o_ref[0, rows, :] = acc.astype(BF16)

    variants = {}
    for s, kind in enumerate(kinds):
        variants.setdefault((kind, s == 0), []).append(s)
    for (kind, first), secs in variants.items():
        cond = functools.reduce(jnp.logical_or, [j == s for s in secs])

        @pl.when(cond)
        def _(kind=kind, first=first):
            section(kind, first)


def _proj(x, sh, sc, w_main, w_gate_t, b_gate, kinds, tables=None):
    B, T, D = x.shape
    n_sec = len(kinds)
    tm = min(PROJ_TM, T)
    tn = BRANCH_W
    assert T % tm == 0
    in_specs = [
        pl.BlockSpec((1, tm, D), lambda i, b, j: (b, i, 0)),
        pl.BlockSpec((1, 1, D), lambda i, b, j: (b, 0, 0)),
        pl.BlockSpec((1, 1, D), lambda i, b, j: (b, 0, 0)),
        pl.BlockSpec((D, tn), lambda i, b, j: (0, j)),
        pl.BlockSpec((N_GATES, D), lambda i, b, j: (0, 0)),
        pl.BlockSpec((N_GATES, 1), lambda i, b, j: (0, 0)),
    ]
    args = [x, sh, sc, w_main, w_gate_t, b_gate]
    if tables is not None:
        in_specs += [pl.BlockSpec((tm, HEAD_W), lambda i, b, j: (i, 0))] * 2
        args += list(tables)
    return pl.pallas_call(
        functools.partial(_proj_kernel, kinds),
        out_shape=(jax.ShapeDtypeStruct((B, T, n_sec * tn), BF16),
                   jax.ShapeDtypeStruct((B, N_GATES, T), F32)),
        grid=(T // tm, B, n_sec),
        in_specs=in_specs,
        out_specs=(pl.BlockSpec((1, tm, tn), lambda i, b, j: (b, i, j)),
                   pl.BlockSpec((1, N_GATES, tm), lambda i, b, j: (b, 0, i))),
        scratch_shapes=[pltpu.VMEM((tm, D), BF16)],
        compiler_params=_cparams("parallel", "parallel", "arbitrary"),
        name="proj_lat" if tables is not None else "proj_ctx",
    )(*args)


def _ret_build(dl_ref, q_ref, k_ref, v_ref, rg_ref, ck_ref, cv_ref, o_ref,
               sf_ref, sb_ref, fs_ref, bs_ref, dec_ref, d_ref):
    h = pl.program_id(0)
    L = SCAN_L
    n_chunks = q_ref.shape[1] // L
    n_ctx_chunks = ck_ref.shape[1] // L
    lgf = _log_sigmoid(jnp.full((1, 1), dl_ref[0, h], F32))
    lgb = _log_sigmoid(jnp.full((1, 1), dl_ref[1, h], F32))

    @pl.when(pl.program_id(1) == 0)
    def _():
        ri = lax.broadcasted_iota(I32, (L, L), 0)
        ci = lax.broadcasted_iota(I32, (L, L), 1)
        rel = (ri - ci).astype(F32)
        d_ref[...] = jnp.where(rel >= 0.0, jnp.exp(jnp.maximum(rel, 0.0) * lgf),
                               jnp.exp(jnp.maximum(-rel, 0.0) * lgb))
        row = lax.broadcasted_iota(I32, (L, HEAD_W), 0).astype(F32)
        dec_ref[0] = jnp.exp((row + 1.0) * lgf)
        dec_ref[1] = jnp.exp((L - 1.0 - row) * lgf)
        dec_ref[2] = jnp.exp((L - row) * lgb)
        dec_ref[3] = jnp.exp(row * lgb)

    cdf = jnp.exp(L * lgf)
    cdb = jnp.exp(L * lgb)

    def update(s_ref, kc, vc, kd, cd):
        kdec = (kc.astype(F32) * kd).astype(BF16)
        s_ref[...] = s_ref[...] * cd + lax.dot_general(kdec, vc, TN_DIMS, preferred_element_type=F32)

    sf_ref[...] = jnp.zeros_like(sf_ref)
    sb_ref[...] = jnp.zeros_like(sb_ref)
    for c in range(n_ctx_chunks):
        update(sf_ref, ck_ref[0, c * L:(c + 1) * L, :], cv_ref[0, c * L:(c + 1) * L, :], dec_ref[1], cdf)
    for c in reversed(range(n_ctx_chunks)):
        update(sb_ref, ck_ref[0, c * L:(c + 1) * L, :], cv_ref[0, c * L:(c + 1) * L, :], dec_ref[3], cdb)

    def state_pass(i, carry):
        cb = n_chunks - 1 - i
        rf = pl.multiple_of(i * L, L)
        rb = pl.multiple_of(cb * L, L)
        fs_ref[i] = sf_ref[...].astype(BF16)
        bs_ref[cb] = sb_ref[...].astype(BF16)
        update(sf_ref, k_ref[0, pl.ds(rf, L), :], v_ref[0, pl.ds(rf, L), :], dec_ref[1], cdf)
        update(sb_ref, k_ref[0, pl.ds(rb, L), :], v_ref[0, pl.ds(rb, L), :], dec_ref[3], cdb)
        return carry

    def finish_states():
        fs_ref[n_chunks - 1] = sf_ref[...].astype(BF16)
        bs_ref[0] = sb_ref[...].astype(BF16)

    def out_chunk(c):
        r0 = pl.multiple_of(c * L, L)
        q = q_ref[0, pl.ds(r0, L), :]
        k = k_ref[0, pl.ds(r0, L), :]
        v = v_ref[0, pl.ds(r0, L), :]
        s = lax.dot_general(q, k, NT_DIMS, preferred_element_type=F32)
        att = (s * d_ref[...]).astype(BF16)
        o = jnp.dot(att, v, preferred_element_type=F32)
        o = o + jnp.dot(q, fs_ref[c], preferred_element_type=F32) * dec_ref[0]
        o = o + jnp.dot(q, bs_ref[c], preferred_element_type=F32) * dec_ref[2]
        rg = rg_ref[0, pl.ds(r0, L), :].astype(F32)
        o_ref[0, pl.ds(r0, L), :] = (_layer_norm(o) * _silu(rg)).astype(BF16)

    return state_pass, finish_states, out_chunk


def _ret_scratch(n_chunks):
    return [pltpu.VMEM((HEAD_W, HEAD_W), F32),
            pltpu.VMEM((HEAD_W, HEAD_W), F32),
            pltpu.VMEM((n_chunks, HEAD_W, HEAD_W), BF16),
            pltpu.VMEM((n_chunks, HEAD_W, HEAD_W), BF16),
            pltpu.VMEM((4, SCAN_L, HEAD_W), F32),
            pltpu.VMEM((SCAN_L, SCAN_L), F32)]


def _mlstm_build(qp_ref, kp_ref, v_ref, mo_ref, ckp_ref, cv_ref, gt_ref, cgt_ref,
                 wq_ref, bq_ref, wk_ref, bk_ref, o_ref,
                 tab_ref, row_ref,
                 cf_ref, mf_ref, cb_ref, mb_ref, cfs_ref, mfs_ref, cbs_ref, mbs_ref, mask_ref,
                 xk_ref, xq_ref, q_ref, k_ref, ck_ref):
    L = SCAN_L
    T = qp_ref.shape[1]
    Tc = ckp_ref.shape[1]
    n_chunks = T // L
    n_ctx_chunks = Tc // L
    CV = CONV_ROWS

    def conv_stage(src_ref, xs_ref, t_len):
        xs_ref[pl.ds(0, 8), :] = jnp.zeros((8, HEAD_W), F32)
        xs_ref[pl.ds(8 + t_len, 8), :] = jnp.zeros((8, HEAD_W), F32)
        xs_ref[pl.ds(8, t_len), :] = src_ref[0].astype(F32)

    def conv_rows(xs_ref, c, w_ref, b_ref, dst_ref, scale):
        w = w_ref[...]
        r0 = pl.multiple_of(c * CV, CV)
        win = xs_ref[pl.ds(r0, CV + 16), :]
        prev = pltpu.roll(win, 1, 0)[8:8 + CV, :]
        cur = win[8:8 + CV, :]
        nxt = pltpu.roll(win, CV + 15, 0)[8:8 + CV, :]
        y = _silu(prev * w[0:1, :] + cur * w[1:2, :] + nxt * w[2:3, :] + b_ref[...])
        if scale != 1.0:
            y = y * scale
        dst_ref[pl.ds(r0, CV), :] = y.astype(BF16)

    per_chunk = L // CV

    def conv_k(c):
        for u in range(per_chunk):
            conv_rows(xk_ref, c * per_chunk + u, wk_ref, bk_ref, k_ref, KEY_SCALE)

    def conv_q(c):
        for u in range(per_chunk):
            conv_rows(xq_ref, c * per_chunk + u, wq_ref, bq_ref, q_ref, 1.0)

    conv_stage(ckp_ref, xk_ref, Tc)
    for c in range(Tc // CV):
        conv_rows(xk_ref, c, wk_ref, bk_ref, ck_ref, KEY_SCALE)
    conv_stage(kp_ref, xk_ref, T)
    conv_stage(qp_ref, xq_ref, T)
    conv_k(0)
    conv_k(n_chunks - 1)
    n_k_steps = n_chunks // 2 - 1

    def conv_step(i, keys):
        if keys:
            conv_k(i + 1)
            conv_k(n_chunks - 2 - i)
        else:
            conv_q(2 * (i - n_k_steps))
            conv_q(2 * (i - n_k_steps) + 1)

    ri = lax.broadcasted_iota(I32, (L, L), 0)
    ci = lax.broadcasted_iota(I32, (L, L), 1)
    tri_u = (ri <= ci).astype(BF16)
    lane8 = lax.broadcasted_iota(I32, (8, L), 1)
    sub8 = lax.broadcasted_iota(I32, (8, L), 0)
    sel_r = lax.broadcasted_iota(I32, (24, 8 * 128), 0) % 8
    sel_c = lax.broadcasted_iota(I32, (24, 8 * 128), 1) // 128
    sel3 = (sel_r == sel_c).astype(BF16)
    ones_cols = jnp.ones((L, AUG_W - HEAD_W), BF16)

    def chunk_tables(g8, n_used, state_only):
        i_f, i_b = g8[0], g8[2]
        lf_f, lf_b = _log_sigmoid(g8[1]), _log_sigmoid(g8[3])
        cs3 = jnp.dot(_split3(jnp.concatenate([lf_f, lf_b], axis=0)), tri_u,
                      preferred_element_type=F32)
        cs = cs3[0:16] + cs3[16:32] + cs3[32:48]
        b_f = cs[0:8]
        b_b = cs[8:16, L - 1:L] - cs[8:16] + lf_b
        z_f = i_f - b_f
        z_b = i_b - b_b
        g_f = b_f[:, L - 1:L] - b_f + i_f
        g_b = b_b[:, 0:1] - b_b + i_b
        mf, mb = z_f, z_b
        s = 1
        while s < L:
            mf = jnp.maximum(mf, jnp.where(lane8 >= s, pltpu.roll(mf, s, 1), NEG_INF))
            mb = jnp.maximum(mb, jnp.where(lane8 < L - s, pltpu.roll(mb, L - s, 1), NEG_INF))
            s *= 2
        mb = jnp.where(lane8 < L - 1, pltpu.roll(mb, L - 1, 1), NEG_INF)
        reps = [None if state_only and t not in (2, 5) else
                lax.dot_general(_split3(val), sel3[:, 0:n_used * 128], TN_DIMS, preferred_element_type=F32)
                for t, val in enumerate((mf, b_f, g_f, mb, b_b, g_b))]

        def rows_of(c):
            out = jnp.zeros((8, L), F32)
            for r, val in enumerate((z_f, z_b, g_f, g_b, b_f, b_b)):
                out = jnp.where(sub8 == r, val[c:c + 1], out)
            return out

        return rows_of, reps

    lat_rows, lat_reps = chunk_tables(gt_ref[0, 0], n_chunks, False)
    for c in range(n_chunks):
        row_ref[c] = lat_rows(c)
        for t in range(N_TAB):
            tab_ref[t, c * L:(c + 1) * L, :] = lat_reps[t][:, c * 128:(c + 1) * 128]

    def lanes2(x):
        return jnp.concatenate([x, x], axis=1)

    def advance(k, v, g_rep, g_row, b_last, c_ref, m_ref):
        m = m_ref[...]
        m_new = jnp.maximum(b_last + m, jnp.max(g_row, axis=-1, keepdims=True))
        kw = (k.astype(F32) * jnp.exp(lanes2(g_rep) - m_new)).astype(BF16)
        v_aug = jnp.concatenate([v, ones_cols], axis=1)
        c_ref[...] = jnp.exp(b_last + m - m_new) * c_ref[...] + lax.dot_general(
            kw, v_aug, TN_DIMS, preferred_element_type=F32)
        m_ref[...] = m_new

    for r in (cf_ref, mf_ref, cb_ref, mb_ref):
        r[...] = jnp.zeros_like(r)
    ctx_rows, ctx_reps = chunk_tables(cgt_ref[0, 0], n_ctx_chunks, True)
    for c in range(n_ctx_chunks):
        rows = ctx_rows(c)
        advance(ck_ref[c * L:(c + 1) * L, :], cv_ref[0, c * L:(c + 1) * L, :],
                ctx_reps[2][:, c * 128:(c + 1) * 128], rows[2:3], rows[4:5, L - 1:L], cf_ref, mf_ref)
    for c in reversed(range(n_ctx_chunks)):
        rows = ctx_rows(c)
        advance(ck_ref[c * L:(c + 1) * L, :], cv_ref[0, c * L:(c + 1) * L, :],
                ctx_reps[5][:, c * 128:(c + 1) * 128], rows[3:4], rows[5:6, 0:1], cb_ref, mb_ref)

    def state_pass(i, carry, keys):
        cb = n_chunks - 1 - i
        rf = pl.multiple_of(i * L, L)
        rb = pl.multiple_of(cb * L, L)
        cfs_ref[i] = cf_ref[...].astype(BF16)
        mfs_ref[i] = mf_ref[...]
        cbs_ref[cb] = cb_ref[...].astype(BF16)
        mbs_ref[cb] = mb_ref[...]
        rows_f = row_ref[i]
        rows_b = row_ref[cb]
        advance(k_ref[pl.ds(rf, L), :], v_ref[0, pl.ds(rf, L), :], tab_ref[2, pl.ds(rf, L), :],
                rows_f[2:3], rows_f[4:5, L - 1:L], cf_ref, mf_ref)
        advance(k_ref[pl.ds(rb, L), :], v_ref[0, pl.ds(rb, L), :], tab_ref[5, pl.ds(rb, L), :],
                rows_b[3:4], rows_b[5:6, 0:1], cb_ref, mb_ref)
        conv_step(i, keys)
        return carry

    def finish_states():
        cfs_ref[n_chunks - 1] = cf_ref[...].astype(BF16)
        mfs_ref[n_chunks - 1] = mf_ref[...]
        cbs_ref[0] = cb_ref[...].astype(BF16)
        mbs_ref[0] = mb_ref[...]

    def direction(q, v_aug, s, z_row, zmax_rep, b_rep, mask, c_in, m_in):
        mx = jnp.maximum(zmax_rep, m_in)
        att = s * jnp.exp((z_row - lanes2(mx)) + mask)
        na = jnp.dot(att.astype(BF16), v_aug, preferred_element_type=F32)
        qa = jnp.dot(q, c_in, preferred_element_type=F32)
        a = jnp.exp(m_in - mx)
        num = na[:, 0:HEAD_W] + lanes2(a) * qa[:, 0:HEAD_W]
        den = na[:, HEAD_W:] + a * qa[:, HEAD_W:]
        scale = 1.0 / jnp.maximum(jnp.abs(den), jnp.exp(-(b_rep + mx)))
        return num * lanes2(scale)

    @pl.when(pl.program_id(1) == 0)
    def _():
        mask_ref[0] = jnp.where(ci <= ri, 0.0, NEG_INF)
        mask_ref[1] = jnp.where(ci > ri, 0.0, NEG_INF)

    def out_chunk(c):
        r0 = pl.multiple_of(c * L, L)
        q = q_ref[pl.ds(r0, L), :]
        k = k_ref[pl.ds(r0, L), :]
        v_aug = jnp.concatenate([v_ref[0, pl.ds(r0, L), :], ones_cols], axis=1)
        s = lax.dot_general(q, k, NT_DIMS, preferred_element_type=F32)
        rows = row_ref[c]
        tot = direction(q, v_aug, s, rows[0:1], tab_ref[0, pl.ds(r0, L), :], tab_ref[1, pl.ds(r0, L), :],
                        mask_ref[0], cfs_ref[c], mfs_ref[c])
        tot = tot + direction(q, v_aug, s, rows[1:2], tab_ref[3, pl.ds(r0, L), :],
                              tab_ref[4, pl.ds(r0, L), :], mask_ref[1], cbs_ref[c], mbs_ref[c])
        mo = mo_ref[0, pl.ds(r0, L), :].astype(F32)
        o_ref[0, pl.ds(r0, L), :] = (_layer_norm(tot) * jax.nn.sigmoid(mo)).astype(BF16)

    return state_pass, finish_states, out_chunk, n_k_steps


def _mlstm_scratch(T, Tc, n_chunks):
    state = [pltpu.VMEM((HEAD_W, AUG_W), F32), pltpu.VMEM((1, 1), F32)]
    snaps = [pltpu.VMEM((n_chunks, HEAD_W, AUG_W), BF16), pltpu.VMEM((n_chunks, 1, 1), F32)]
    return [pltpu.VMEM((N_TAB, T, 128), F32), pltpu.VMEM((n_chunks, 8, SCAN_L), F32)] \
        + state + state + snaps + snaps + [pltpu.VMEM((2, SCAN_L, SCAN_L), F32)] \
        + [pltpu.VMEM((T + 16, HEAD_W), F32), pltpu.VMEM((T + 16, HEAD_W), F32),
           pltpu.VMEM((T, HEAD_W), BF16), pltpu.VMEM((T, HEAD_W), BF16), pltpu.VMEM((Tc, HEAD_W), BF16)]


def _scan_kernel(n_ret_scratch, dl_ref, rq_ref, rk_ref, rv_ref, rg_ref, rck_ref, rcv_ref,
                 mq_ref, mk_ref, mv_ref, mo_ref, mck_ref, mcv_ref, gt_ref, cgt_ref,
                 wq_ref, bq_ref, wk_ref, bk_ref, r_ref, m_ref, *scratch):
    n_chunks = rq_ref.shape[1] // SCAN_L
    ret = _ret_build(dl_ref, rq_ref, rk_ref, rv_ref, rg_ref, rck_ref, rcv_ref, r_ref,
                     *scratch[:n_ret_scratch])
    mls = _mlstm_build(mq_ref, mk_ref, mv_ref, mo_ref, mck_ref, mcv_ref, gt_ref, cgt_ref,
                       wq_ref, bq_ref, wk_ref, bk_ref, m_ref, *scratch[n_ret_scratch:])

    def state_pass(i, carry, keys):
        ret[0](i, carry)
        mls[0](i, carry, keys)
        return carry

    n_k_steps = mls[3]
    lax.fori_loop(0, n_k_steps, functools.partial(state_pass, keys=True), 0)
    lax.fori_loop(n_k_steps, n_chunks - 1, functools.partial(state_pass, keys=False), 0)
    ret[1]()
    mls[1]()

    def out_pass(i, carry):
        for c in (2 * i, 2 * i + 1):
            ret[2](c)
            mls[2](c)
        return carry

    lax.fori_loop(0, n_chunks // 2, out_pass, 0)


def _scans(decay_logit, p_lat, p_ctx, gt, cgt, conv_w, conv_b, ret_lat, ret_ctx, ml_lat, ml_ctx):
    B, T, _ = p_lat.shape
    Tc = p_ctx.shape[1]
    assert T % (2 * SCAN_L) == 0 and Tc % SCAN_L == 0 and T // SCAN_L <= 8
    n_chunks = T // SCAN_L

    def lat(sec):
        return pl.BlockSpec((1, T, HEAD_W), lambda h, b: (b, 0, sec * HEADS + h))

    def cx(sec):
        return pl.BlockSpec((1, Tc, HEAD_W), lambda h, b: (b, 0, sec * HEADS + h))

    gates = pl.BlockSpec((1, 1, N_GK, 8, SCAN_L), lambda h, b: (b, h, 0, 0, 0))
    out = pl.BlockSpec((1, T, HEAD_W), lambda h, b: (b, 0, h))
    conv_specs = [pl.BlockSpec((3, HEAD_W), lambda h, b: (0, h)),
                  pl.BlockSpec((1, HEAD_W), lambda h, b: (0, h)),
                  pl.BlockSpec((3, HEAD_W), lambda h, b: (0, HEADS + h)),
                  pl.BlockSpec((1, HEAD_W), lambda h, b: (0, HEADS + h))]
    ret_scratch = _ret_scratch(n_chunks)
    return pl.pallas_call(
        functools.partial(_scan_kernel, len(ret_scratch)),
        out_shape=(jax.ShapeDtypeStruct((B, T, BRANCH_W), BF16),
                   jax.ShapeDtypeStruct((B, T, BRANCH_W), BF16)),
        grid=(HEADS, B),
        in_specs=[pl.BlockSpec(memory_space=pltpu.SMEM)]
        + [lat(s) for s in ret_lat] + [cx(s) for s in ret_ctx]
        + [lat(s) for s in ml_lat] + [cx(s) for s in ml_ctx] + [gates, gates] + conv_specs,
        out_specs=(out, out),
        scratch_shapes=ret_scratch + _mlstm_scratch(T, Tc, n_chunks),
        compiler_params=_cparams("arbitrary", "arbitrary"),
        name="scans",
    )(decay_logit, *([p_lat] * 4), *([p_ctx] * 2), *([p_lat] * 4), *([p_ctx] * 2), gt, cgt,
      conv_w, conv_b, conv_w, conv_b)


def _merge_kernel(alpha, r_ref, m_ref, gr_ref, gm_ref, x_ref, g1_ref, sh2_ref, sc2_ref,
                  lng_ref, lnb_ref, wr_ref, wm_ref, wo_ref, wrt_ref, brt_ref,
                  x1_ref, ua_ref, ub_ref, ri_ref, rw_ref, cnt_ref, carry_ref, u_ref):
    @pl.when(jnp.logical_and(pl.program_id(0) == 0, pl.program_id(1) == 0))
    def _():
        carry_ref[...] = jnp.zeros_like(carry_ref)
        tm = x_ref.shape[1]
        r = lax.broadcasted_iota(I32, (tm, tm), 0)
        c = lax.broadcasted_iota(I32, (tm, tm), 1)
        u_ref[...] = (r < c).astype(BF16)

    yr = jnp.dot(r_ref[0], wr_ref[...], preferred_element_type=F32)
    ym = jnp.dot(m_ref[0], wm_ref[...], preferred_element_type=F32)
    y = jax.nn.sigmoid(gr_ref[0].astype(F32)) * yr + jax.nn.sigmoid(gm_ref[0].astype(F32)) * ym
    yo = jnp.dot(y.astype(BF16), wo_ref[...], preferred_element_type=F32)
    x1 = _layer_norm(alpha * x_ref[0] + g1_ref[0] * yo) * lng_ref[...] + lnb_ref[...]
    x1_ref[0] = x1
    u2 = _layer_norm(x1) * (1.0 + sc2_ref[0]) + sh2_ref[0]
    ua_ref[0] = _pack_pairs(u2[:, 0:PACK_W], u2[:, PACK_W:2 * PACK_W])
    ub_ref[0] = _pack_pairs(u2[:, 2 * PACK_W:3 * PACK_W], u2[:, 3 * PACK_W:4 * PACK_W])
    lt = lax.dot_general(wrt_ref[...], u2.astype(BF16), NT_DIMS, preferred_element_type=F32) + brt_ref[...]
    _route_tile(lt, ri_ref, rw_ref, cnt_ref, carry_ref, u_ref)


def _route_tile(lt, ri_ref, rw_ref, cnt_ref, carry_ref, u_ref):
    tm = lt.shape[1]
    lg = lt[0:N_GROUPS, :]
    eg = jnp.exp(lg - jnp.max(lg, axis=0, keepdims=True))
    pg = eg / jnp.sum(eg, axis=0, keepdims=True)
    pg_top = jnp.max(pg, axis=0, keepdims=True)
    rows_g = lax.broadcasted_iota(I32, pg.shape, 0)
    g_idx = jnp.min(jnp.where(pg == pg_top, rows_g, N_GROUPS), axis=0, keepdims=True)

    le = jnp.zeros((EXP_PER_GROUP, tm), F32)
    for g in range(N_GROUPS):
        lo = 8 + g * EXP_PER_GROUP
        le = jnp.where(g_idx == g, lt[lo:lo + EXP_PER_GROUP, :], le)
    ee = jnp.exp(le - jnp.max(le, axis=0, keepdims=True))
    pe = ee / jnp.sum(ee, axis=0, keepdims=True)
    rows_e = lax.broadcasted_iota(I32, pe.shape, 0)
    v1 = jnp.max(pe, axis=0, keepdims=True)
    i1 = jnp.min(jnp.where(pe == v1, rows_e, EXP_PER_GROUP), axis=0, keepdims=True)
    pe2 = jnp.where(rows_e == i1, -1.0, pe)
    v2 = jnp.max(pe2, axis=0, keepdims=True)
    i2 = jnp.min(jnp.where(pe2 == v2, rows_e, EXP_PER_GROUP), axis=0, keepdims=True)
    den = v1 + v2
    rw_ref[...] = jnp.zeros_like(rw_ref)
    rw_ref[0:1, :] = pg_top * v1 / den
    rw_ref[1:2, :] = pg_top * v2 / den
    e1 = g_idx * EXP_PER_GROUP + i1
    e2 = g_idx * EXP_PER_GROUP + i2

    rows_x = lax.broadcasted_iota(I32, (N_EXPERTS, tm), 0)
    oh1 = (rows_x == e1).astype(F32)
    oh2 = (rows_x == e2).astype(F32)
    both = oh1 + oh2
    before = carry_ref[:, 0:1] + jnp.dot(both.astype(BF16), u_ref[...], preferred_element_type=F32)
    ri_ref[0:1, :] = e1
    ri_ref[1:2, :] = e2
    ri_ref[2:3, :] = jnp.sum(oh1 * before, axis=0, keepdims=True).astype(I32)
    ri_ref[3:4, :] = jnp.sum(oh2 * before, axis=0, keepdims=True).astype(I32)
    carry_ref[...] = carry_ref[...] + jnp.sum(both, axis=1, keepdims=True)
    cnt_ref[...] = carry_ref[...].astype(I32)


def _merge(alpha, r, m, p_lat, sec_gates, x, g1, sh2, sc2, lng, lnb, wr, wm, wo, wrt, brt):
    B, T, D = x.shape
    tm = MERGE_TM
    per_b = T // tm
    n = B * T

    def tile(w):
        return pl.BlockSpec((1, tm, w), lambda b, i: (b, i, 0))

    def sec(s):
        return pl.BlockSpec((1, tm, BRANCH_W), lambda b, i: (b, i, s))

    def mod():
        return pl.BlockSpec((1, 1, D), lambda b, i: (b, 0, 0))

    def const(shape):
        return pl.BlockSpec(shape, lambda b, i: (0,) * len(shape))

    return pl.pallas_call(
        functools.partial(_merge_kernel, alpha),
        out_shape=(jax.ShapeDtypeStruct((B, T, D), F32),
                   jax.ShapeDtypeStruct((B, T, PACK_W), U32),
                   jax.ShapeDtypeStruct((B, T, PACK_W), U32),
                   jax.ShapeDtypeStruct((4, n), I32),
                   jax.ShapeDtypeStruct((8, n), F32),
                   jax.ShapeDtypeStruct((N_EXPERTS, 128), I32)),
        grid=(B, per_b),
        in_specs=[tile(BRANCH_W), tile(BRANCH_W), sec(sec_gates[0]), sec(sec_gates[1]), tile(D),
                  mod(), mod(), mod(), const((1, D)), const((1, D)),
                  const((BRANCH_W, D)), const((BRANCH_W, D)), const((D, D)),
                  const((ROUTE_ROWS, D)), const((ROUTE_ROWS, 1))],
        out_specs=(tile(D), tile(PACK_W), tile(PACK_W),
                   pl.BlockSpec((4, tm), lambda b, i: (0, b * per_b + i)),
                   pl.BlockSpec((8, tm), lambda b, i: (0, b * per_b + i)),
                   pl.BlockSpec((N_EXPERTS, 128), lambda b, i: (0, 0))),
        scratch_shapes=[pltpu.VMEM((N_EXPERTS, 128), F32), pltpu.VMEM((tm, tm), BF16)],
        compiler_params=_cparams("arbitrary", "arbitrary"),
        name="merge",
    )(r, m, p_lat, p_lat, x, g1, sh2, sc2, lng, lnb, wr, wm, wo, wrt, brt)


def _sc_mesh():
    return plsc.VectorSubcoreMesh(core_axis_name="c", subcore_axis_name="s")


def _sc_scatter2(rows_a, rows_b, idx0, idx1, n_out):
    m, w = rows_a.shape
    out = jax.ShapeDtypeStruct((n_out, w), rows_a.dtype)

    @functools.partial(pl.kernel, out_type=(out, out), mesh=_sc_mesh(), scratch_types=[])
    def k(xa_hbm, xb_hbm, i0_hbm, i1_hbm, oa_hbm, ob_hbm):
        for x_hbm, o_hbm in ((xa_hbm, oa_hbm), (xb_hbm, ob_hbm)):
            def body(x_vmem, i0_vmem, i1_vmem, o_hbm=o_hbm):
                pltpu.sync_copy(x_vmem, o_hbm.at[i0_vmem.at[0]])
                pltpu.sync_copy(x_vmem, o_hbm.at[i1_vmem.at[0]])

            pltpu.emit_pipeline(
                body,
                grid=(m // SC_WIN,),
                in_specs=[pl.BlockSpec((SC_WIN, w), lambda i: (i, 0)),
                          pl.BlockSpec((1, SC_WIN), lambda i: (0, i)),
                          pl.BlockSpec((1, SC_WIN), lambda i: (0, i))],
                out_specs=[],
                core_axis_name=("c", "s"),
                dimension_semantics=(pltpu.PARALLEL,),
            )(x_hbm, i0_hbm, i1_hbm)

    return k(rows_a, rows_b, idx0.reshape(1, m), idx1.reshape(1, m))


def _sc_gather(table_a, table_b, idx):
    m = idx.shape[0]
    w = table_a.shape[1]
    out = jax.ShapeDtypeStruct((m, w), table_a.dtype)

    @functools.partial(pl.kernel, out_type=(out, out), mesh=_sc_mesh(), scratch_types=[])
    def k(ta_hbm, tb_hbm, i_hbm, oa_hbm, ob_hbm):
        for t_hbm, o_hbm in ((ta_hbm, oa_hbm), (tb_hbm, ob_hbm)):
            def body(i_vmem, o_vmem, t_hbm=t_hbm):
                pltpu.sync_copy(t_hbm.at[i_vmem.at[0]], o_vmem)

            pltpu.emit_pipeline(
                body,
                grid=(m // SC_WIN,),
                in_specs=[pl.BlockSpec((1, SC_WIN), lambda i: (0, i))],
                out_specs=[pl.BlockSpec((SC_WIN, w), lambda i: (i, 0))],
                core_axis_name=("c", "s"),
                dimension_semantics=(pltpu.PARALLEL,),
            )(i_hbm, o_hbm)

    return k(table_a, table_b, idx.reshape(1, m))


def _expert_kernel(be_ref, nv_ref, xa_ref, xb_ref, w1f_ref, w3f_ref, w2f_ref, ya_ref, yb_ref,
                   w1_ref, w3_ref, w2_ref):
    j = pl.program_id(0)
    nv = nv_ref[j]

    @pl.when(jnp.logical_or(j == 0, be_ref[j] != be_ref[jnp.maximum(j - 1, 0)]))
    def _():
        w1_ref[0] = w1f_ref[0].astype(BF16)
        w3_ref[0] = w3f_ref[0].astype(BF16)
        w2_ref[0] = w2f_ref[0].astype(BF16)

    @pl.when(nv > 0)
    def _():
        valid = lax.broadcasted_iota(I32, xa_ref.shape, 0) < nv
        zero = jnp.zeros(xa_ref.shape, U32)
        parts = _unpack_pairs(jnp.where(valid, xa_ref[...], zero)) + \
            _unpack_pairs(jnp.where(valid, xb_ref[...], zero))
        x = jnp.concatenate([p.astype(BF16) for p in parts], axis=1)
        h1 = jnp.dot(x, w1_ref[0], preferred_element_type=F32)
        h3 = jnp.dot(x, w3_ref[0], preferred_element_type=F32)
        y = jnp.dot((_silu(h1) * h3).astype(BF16), w2_ref[0], preferred_element_type=F32)
        ya_ref[...] = _pack_pairs(y[:, 0:PACK_W], y[:, PACK_W:2 * PACK_W])
        yb_ref[...] = _pack_pairs(y[:, 2 * PACK_W:3 * PACK_W], y[:, 3 * PACK_W:4 * PACK_W])

    @pl.when(nv == 0)
    def _():
        ya_ref[...] = jnp.zeros_like(ya_ref)
        yb_ref[...] = jnp.zeros_like(yb_ref)


def _experts(block_exp, n_valid, xa, xb, w1, w3, w2):
    n_slots = xa.shape[0]
    n_blocks = n_slots // MOE_BLK
    d, de = w1.shape[1], w1.shape[2]
    slot = pl.BlockSpec((MOE_BLK, PACK_W), lambda j, be, nv: (j, 0))
    grid_spec = pltpu.PrefetchScalarGridSpec(
        num_scalar_prefetch=2,
        grid=(n_blocks,),
        in_specs=[slot, slot,
                  pl.BlockSpec((1, d, de), lambda j, be, nv: (be[j], 0, 0)),
                  pl.BlockSpec((1, d, de), lambda j, be, nv: (be[j], 0, 0)),
                  pl.BlockSpec((1, de, d), lambda j, be, nv: (be[j], 0, 0))],
        out_specs=(slot, slot),
        scratch_shapes=[pltpu.VMEM((1, d, de), BF16), pltpu.VMEM((1, d, de), BF16),
                        pltpu.VMEM((1, de, d), BF16)],
    )
    return pl.pallas_call(
        _expert_kernel,
        out_shape=(jax.ShapeDtypeStruct((n_slots, PACK_W), U32),
                   jax.ShapeDtypeStruct((n_slots, PACK_W), U32)),
        grid_spec=grid_spec,
        compiler_params=_cparams("arbitrary"),
        name="experts",
    )(block_exp, n_valid, xa, xb, w1, w3, w2)


def _final_kernel(alpha, x1_ref, a0_ref, b0_ref, a1_ref, b1_ref, w_ref, g2_ref, lng_ref, lnb_ref, o_ref):
    w = w_ref[...].T
    w0 = w[:, 0:1]
    w1 = w[:, 1:2]
    parts0 = _unpack_pairs(a0_ref[...]) + _unpack_pairs(b0_ref[...])
    parts1 = _unpack_pairs(a1_ref[...]) + _unpack_pairs(b1_ref[...])
    f = jnp.concatenate([w0 * p0 + w1 * p1 for p0, p1 in zip(parts0, parts1)], axis=1)
    o_ref[0] = _layer_norm(alpha * x1_ref[0] + g2_ref[0] * f) * lng_ref[...] + lnb_ref[...]


def _final(alpha, x1, ya, yb, w, g2, lng, lnb):
    B, T, D = x1.shape
    tm = MERGE_TM
    per_b = T // tm
    n_tiles = B * per_b

    def rows(k):
        return pl.BlockSpec((tm, PACK_W), lambda b, i: (k * n_tiles + b * per_b + i, 0))

    return pl.pallas_call(
        functools.partial(_final_kernel, alpha),
        out_shape=jax.ShapeDtypeStruct((B, T, D), F32),
        grid=(B, per_b),
        in_specs=[pl.BlockSpec((1, tm, D), lambda b, i: (b, i, 0)),
                  rows(0), rows(0), rows(1), rows(1),
                  pl.BlockSpec((8, tm), lambda b, i: (0, b * per_b + i)),
                  pl.BlockSpec((1, 1, D), lambda b, i: (b, 0, 0)),
                  pl.BlockSpec((1, D), lambda b, i: (0, 0)),
                  pl.BlockSpec((1, D), lambda b, i: (0, 0))],
        out_specs=pl.BlockSpec((1, tm, D), lambda b, i: (b, i, 0)),
        compiler_params=_cparams("parallel", "parallel"),
        name="final",
    )(x1, ya, yb, ya, yb, w, g2, lng, lnb)


def _rotary_tables(T):
    quarter = HEAD_W // 4
    freqs = ROPE_BASE ** (-jnp.arange(quarter, dtype=F32) / quarter)
    t = jnp.arange(T)
    ang_r = (t // GRID_W).astype(F32)[:, None] * freqs[None, :]
    ang_c = (t % GRID_W).astype(F32)[:, None] * freqs[None, :]
    cos = jnp.concatenate([jnp.cos(ang_r)] * 2 + [jnp.cos(ang_c)] * 2, axis=1)
    sin = jnp.concatenate([-jnp.sin(ang_r), jnp.sin(ang_r), -jnp.sin(ang_c), jnp.sin(ang_c)], axis=1)
    return cos, sin


def _per_head_gates(gt):
    B, _, T = gt.shape
    n_chunks = T // SCAN_L
    gth = gt.reshape(B, N_GK, HEADS, n_chunks, SCAN_L).transpose(0, 2, 1, 3, 4)
    return jnp.pad(gth, ((0, 0), (0, 0), (0, 0), (0, 8 - n_chunks), (0, 0)))


def _table_lookup(table, idx):
    sel = idx[..., None] == jnp.arange(table.shape[0], dtype=idx.dtype)
    return jnp.sum(jnp.where(sel, table, 0), axis=-1)


def kernel(x, c, ctx, c_ctx, w_ada, b_ada, w_in, b_mgate, ml_conv_w, ml_conv_b, ret_decay_logit, w_ret_branch, w_ml_branch, w_out, ln1_g, ln1_b, w_rg, b_rg, w_re, b_re, w_e1, w_e3, w_e2, ln2_g, ln2_b):
    B, T, D = x.shape
    depth = w_ada.shape[0]
    assert depth == 1 and D == BRANCH_W and T % GRID_W == 0
    alpha = (2 * depth) ** 0.25
    n_tok = B * T

    n_rows = -(-(B + 1) // 8) * 8
    cs = jnp.zeros((n_rows, D), F32).at[:B].set(c).at[B].set(c_ctx)
    mod = _ada(cs, w_ada[0], b_ada[0][None, :])
    sh1, sc1, g1, sh2, sc2, g2 = [mod[:B, None, i * D:(i + 1) * D] for i in range(6)]
    csh1 = mod[B, 0 * D:1 * D].reshape(1, 1, D)
    csc1 = mod[B, 1 * D:2 * D].reshape(1, 1, D)

    w = w_in[0]
    sec_w = [w[:, s * BRANCH_W:(s + 1) * BRANCH_W] for s in range(8)]
    g_lo = 8 * BRANCH_W
    w_gate_t = w[:, g_lo:g_lo + N_GATES].T.astype(BF16)
    b_gate = b_mgate[0][:, None]
    sec_w += [w[:, g_lo + N_GATES:g_lo + N_GATES + D], w[:, g_lo + N_GATES + D:]]
    w_lat = jnp.concatenate(sec_w, axis=1).astype(BF16)
    w_ctx = jnp.concatenate([sec_w[1], sec_w[2], sec_w[5], sec_w[6]], axis=1).astype(BF16)
    kinds_lat = ("rot", "rot_scale") + ("plain",) * 8
    kinds_ctx = ("scale", "plain", "plain", "plain")
    p_lat, gt_lat = _proj(x, sh1, sc1, w_lat, w_gate_t, b_gate, kinds_lat, _rotary_tables(T))
    Tc = ctx.shape[1]
    p_ctx, gt_ctx = _proj(ctx.reshape(1, B * Tc, D), csh1, csc1, w_ctx, w_gate_t, b_gate, kinds_ctx)
    p_ctx = p_ctx.reshape(B, Tc, -1)
    gt_ctx = gt_ctx.reshape(N_GATES, B, Tc).transpose(1, 0, 2)

    ret, mls = _scans(ret_decay_logit[0], p_lat, p_ctx, _per_head_gates(gt_lat), _per_head_gates(gt_ctx),
                      ml_conv_w[0], ml_conv_b[0][None, :], (0, 1, 2, 3), (0, 1), (4, 5, 6, 7), (2, 3))

    wrt = jnp.zeros((ROUTE_ROWS, D), F32).at[:N_GROUPS].set(w_rg[0].T).at[8:8 + N_EXPERTS].set(w_re[0].T)
    brt = jnp.zeros((ROUTE_ROWS, 1), F32).at[:N_GROUPS, 0].set(b_rg[0]).at[8:8 + N_EXPERTS, 0].set(b_re[0])
    x1, ua, ub, ri, rw, cnt = _merge(alpha, ret, mls, p_lat, (8, 9), x, g1, sh2, sc2,
                                     ln1_g[0][None, :], ln1_b[0][None, :],
                                     w_ret_branch[0].astype(BF16), w_ml_branch[0].astype(BF16),
                                     w_out[0].astype(BF16), wrt.astype(BF16), brt)

    counts = cnt[:, 0]
    padded = (counts + MOE_BLK - 1) // MOE_BLK * MOE_BLK
    pad_end = jnp.cumsum(padded)
    pad_off = pad_end - padded
    dest = _table_lookup(pad_off, ri[0:2]) + ri[2:4]
    n_blocks = (2 * n_tok) // MOE_BLK + N_EXPERTS
    n_slots = n_blocks * MOE_BLK
    block_start = jnp.arange(n_blocks, dtype=I32) * MOE_BLK
    block_exp = jnp.minimum((block_start[:, None] >= pad_end[None, :]).sum(1), N_EXPERTS - 1).astype(I32)
    n_valid = jnp.clip(_table_lookup(counts, block_exp) - (block_start - _table_lookup(pad_off, block_exp)),
                       0, MOE_BLK).astype(I32)

    xa, xb = _sc_scatter2(ua.reshape(n_tok, PACK_W), ub.reshape(n_tok, PACK_W), dest[0], dest[1], n_slots)
    ya, yb = _experts(block_exp, n_valid, xa, xb, w_e1[0], w_e3[0], w_e2[0])
    ga, gb = _sc_gather(ya, yb, dest.reshape(2 * n_tok))
    return _final(alpha, x1, ga, gb, rw, g2, ln2_g[0][None, :], ln2_b[0][None, :])
```

```python
import functools

import jax
import jax.numpy as jnp
from jax import lax
from jax.experimental import pallas as pl
from jax.experimental.pallas import tpu as pltpu
from jax.experimental.pallas import tpu_sc as plsc

F32 = jnp.float32
BF16 = jnp.bfloat16
U32 = jnp.uint32
I32 = jnp.int32
HIGHEST = lax.Precision.HIGHEST

HEADS = 4
HEAD_W = 256
BRANCH_W = HEADS * HEAD_W
GRID_W = 64
ROPE_BASE = 10000.0
N_GATES = 16
N_GK = N_GATES // HEADS
N_GROUPS = 4
EXP_PER_GROUP = 8
N_EXPERTS = N_GROUPS * EXP_PER_GROUP
LN_EPS = 1e-5
NEG_INF = -1e30
KEY_SCALE = HEAD_W ** -0.5

SCAN_L = 256
CONV_ROWS = 128
PROJ_TM = 2048
PROJ_SUB = 256
MERGE_TM = 512
FINAL_TM = 1024
MOE_BLK = 512
SC_WIN = 128
PACK_W = 256
ROUTE_ROWS = 64
N_TAB = 6
AUG_W = HEAD_W + 128
VMEM_LIMIT = 48 * 1024 * 1024

NT_DIMS = (((1,), (1,)), ((), ()))
TN_DIMS = (((0,), (0,)), ((), ()))


def _cparams(*sem):
    return pltpu.CompilerParams(dimension_semantics=sem, vmem_limit_bytes=VMEM_LIMIT)


def _layer_norm(x):
    mu = jnp.mean(x, axis=-1, keepdims=True)
    xc = x - mu
    var = jnp.mean(xc * xc, axis=-1, keepdims=True)
    return xc * lax.rsqrt(var + LN_EPS)


def _log_sigmoid(x):
    return jnp.minimum(x, 0.0) - jnp.log1p(jnp.exp(-jnp.abs(x)))


def _silu(x):
    return x * jax.nn.sigmoid(x)


def _pack_pairs(hi, lo):
    hb = lax.bitcast_convert_type(hi.astype(BF16).astype(F32), U32)
    lb = lax.bitcast_convert_type(lo.astype(BF16).astype(F32), U32)
    return (hb & jnp.uint32(0xFFFF0000)) | (lb >> 16)


def _unpack_pairs(p):
    hi = lax.bitcast_convert_type(p & jnp.uint32(0xFFFF0000), F32)
    lo = lax.bitcast_convert_type(p << 16, F32)
    return hi, lo


def _split3(x):
    hi = x.astype(BF16).astype(F32)
    r1 = x - hi
    mid = r1.astype(BF16).astype(F32)
    lo = (r1 - mid).astype(BF16).astype(F32)
    return jnp.concatenate([hi, mid, lo], axis=0).astype(BF16)


def _ada_kernel(c_ref, w_ref, b_ref, o_ref):
    s = _silu(c_ref[...])
    o_ref[...] = jnp.dot(s, w_ref[...], precision=HIGHEST, preferred_element_type=F32) + b_ref[...]


def _ada(cs, w, b):
    rows, d = cs.shape
    cols = w.shape[1]
    tn = 1024
    return pl.pallas_call(
        _ada_kernel,
        out_shape=jax.ShapeDtypeStruct((rows, cols), F32),
        grid=(cols // tn,),
        in_specs=[pl.BlockSpec((rows, d), lambda j: (0, 0)),
                  pl.BlockSpec((d, tn), lambda j: (0, j)),
                  pl.BlockSpec((1, tn), lambda j: (0, j))],
        out_specs=pl.BlockSpec((rows, tn), lambda j: (0, j)),
        compiler_params=_cparams("parallel"),
        name="ada",
    )(cs, w, b)


def _proj_kernel(kinds, x_ref, sh_ref, sc_ref, w_ref, wg_ref, bg_ref, *rest):
    if "rot" in kinds or "rot_scale" in kinds:
        cos_ref, sin_ref, o_ref, gt_ref, u_ref = rest
    else:
        o_ref, gt_ref, u_ref = rest
    j = pl.program_id(2)
    tm = x_ref.shape[1]
    sub = min(PROJ_SUB, tm)

    def rotary(acc, rows, scale):
        for s in range(acc.shape[1] // 128):
            a = acc[:, s * 128:(s + 1) * 128]
            half = s % 2
            cs = cos_ref[rows, half * 128:(half + 1) * 128]
            sn = sin_ref[rows, half * 128:(half + 1) * 128]
            r = a * cs + pltpu.roll(a, 64, 1) * sn
            if scale != 1.0:
                r = r * scale
            o_ref[0, rows, s * 128:(s + 1) * 128] = r.astype(BF16)

    def section(kind, first):
        for r in range(tm // sub):
            rows = slice(r * sub, (r + 1) * sub)
            if first:
                u = _layer_norm(x_ref[0, rows, :]) * (1.0 + sc_ref[0]) + sh_ref[0]
                ub = u.astype(BF16)
                u_ref[rows, :] = ub
                gt_ref[0, :, rows] = lax.dot_general(wg_ref[...], ub, NT_DIMS,
                                                     preferred_element_type=F32) + bg_ref[...]
            else:
                ub = u_ref[rows, :]
            acc = jnp.dot(ub, w_ref[...], preferred_element_type=F32)
            if kind == "rot":
                rotary(acc, rows, 1.0)
            elif kind == "rot_scale":
                rotary(acc, rows, KEY_SCALE)
            elif kind == "scale":
                o_ref[0, rows, :] = (acc * KEY_SCALE).astype(BF16)
            else:
                o_ref[0, rows, :] = acc.astype(BF16)

    variants = {}
    for s, kind in enumerate(kinds):
        variants.setdefault((kind, s == 0), []).append(s)
    for (kind, first), secs in variants.items():
        cond = functools.reduce(jnp.logical_or, [j == s for s in secs])

        @pl.when(cond)
        def _(kind=kind, first=first):
            section(kind, first)


def _proj(x, sh, sc, w_main, w_gate_t, b_gate, kinds, tables=None):
    B, T, D = x.shape
    n_sec = len(kinds)
    tm = min(PROJ_TM, T)
    tn = BRANCH_W
    assert T % tm == 0
    in_specs = [
        pl.BlockSpec((1, tm, D), lambda i, b, j: (b, i, 0)),
        pl.BlockSpec((1, 1, D), lambda i, b, j: (b, 0, 0)),
        pl.BlockSpec((1, 1, D), lambda i, b, j: (b, 0, 0)),
        pl.BlockSpec((D, tn), lambda i, b, j: (0, j)),
        pl.BlockSpec((N_GATES, D), lambda i, b, j: (0, 0)),
        pl.BlockSpec((N_GATES, 1), lambda i, b, j: (0, 0)),
    ]
    args = [x, sh, sc, w_main, w_gate_t, b_gate]
    if tables is not None:
        in_specs += [pl.BlockSpec((tm, HEAD_W), lambda i, b, j: (i, 0))] * 2
        args += list(tables)
    return pl.pallas_call(
        functools.partial(_proj_kernel, kinds),
        out_shape=(jax.ShapeDtypeStruct((B, T, n_sec * tn), BF16),
                   jax.ShapeDtypeStruct((B, N_GATES, T), F32)),
        grid=(T // tm, B, n_sec),
        in_specs=in_specs,
        out_specs=(pl.BlockSpec((1, tm, tn), lambda i, b, j: (b, i, j)),
                   pl.BlockSpec((1, N_GATES, tm), lambda i, b, j: (b, 0, i))),
        scratch_shapes=[pltpu.VMEM((tm, D), BF16)],
        compiler_params=_cparams("parallel", "parallel", "arbitrary"),
        name="proj_lat" if tables is not None else "proj_ctx",
    )(*args)


def _ret_build(dl_ref, q_ref, k_ref, v_ref, rg_ref, ck_ref, cv_ref, o_ref,
               sf_ref, sb_ref, fs_ref, bs_ref, dec_ref, d_ref):
    h = pl.program_id(0)
    L = SCAN_L
    n_chunks = q_ref.shape[1] // L
    n_ctx_chunks = ck_ref.shape[1] // L
    lgf = _log_sigmoid(jnp.full((1, 1), dl_ref[0, h], F32))
    lgb = _log_sigmoid(jnp.full((1, 1), dl_ref[1, h], F32))

    @pl.when(pl.program_id(1) == 0)
    def _():
        ri = lax.broadcasted_iota(I32, (L, L), 0)
        ci = lax.broadcasted_iota(I32, (L, L), 1)
        rel = (ri - ci).astype(F32)
        d_ref[...] = jnp.where(rel >= 0.0, jnp.exp(jnp.maximum(rel, 0.0) * lgf),
                               jnp.exp(jnp.maximum(-rel, 0.0) * lgb))
        row = lax.broadcasted_iota(I32, (L, HEAD_W), 0).astype(F32)
        dec_ref[0] = jnp.exp((row + 1.0) * lgf)
        dec_ref[1] = jnp.exp((L - 1.0 - row) * lgf)
        dec_ref[2] = jnp.exp((L - row) * lgb)
        dec_ref[3] = jnp.exp(row * lgb)

    cdf = jnp.exp(L * lgf)
    cdb = jnp.exp(L * lgb)

    def update(s_ref, kc, vc, kd, cd):
        kdec = (kc.astype(F32) * kd).astype(BF16)
        s_ref[...] = s_ref[...] * cd + lax.dot_general(kdec, vc, TN_DIMS, preferred_element_type=F32)

    sf_ref[...] = jnp.zeros_like(sf_ref)
    sb_ref[...] = jnp.zeros_like(sb_ref)
    for c in range(n_ctx_chunks):
        update(sf_ref, ck_ref[0, c * L:(c + 1) * L, :], cv_ref[0, c * L:(c + 1) * L, :], dec_ref[1], cdf)
    for c in reversed(range(n_ctx_chunks)):
        update(sb_ref, ck_ref[0, c * L:(c + 1) * L, :], cv_ref[0, c * L:(c + 1) * L, :], dec_ref[3], cdb)

    def state_pass(i, carry):
        cb = n_chunks - 1 - i
        rf = pl.multiple_of(i * L, L)
        rb = pl.multiple_of(cb * L, L)
        fs_ref[i] = sf_ref[...].astype(BF16)
        bs_ref[cb] = sb_ref[...].astype(BF16)
        update(sf_ref, k_ref[0, pl.ds(rf, L), :], v_ref[0, pl.ds(rf, L), :], dec_ref[1], cdf)
        update(sb_ref, k_ref[0, pl.ds(rb, L), :], v_ref[0, pl.ds(rb, L), :], dec_ref[3], cdb)
        return carry

    def finish_states():
        fs_ref[n_chunks - 1] = sf_ref[...].astype(BF16)
        bs_ref[0] = sb_ref[...].astype(BF16)

    def out_chunk(c):
        r0 = pl.multiple_of(c * L, L)
        q = q_ref[0, pl.ds(r0, L), :]
        k = k_ref[0, pl.ds(r0, L), :]
        v = v_ref[0, pl.ds(r0, L), :]
        s = lax.dot_general(q, k, NT_DIMS, preferred_element_type=F32)
        att = (s * d_ref[...]).astype(BF16)
        o = jnp.dot(att, v, preferred_element_type=F32)
        o = o + jnp.dot(q, fs_ref[c], preferred_element_type=F32) * dec_ref[0]
        o = o + jnp.dot(q, bs_ref[c], preferred_element_type=F32) * dec_ref[2]
        rg = rg_ref[0, pl.ds(r0, L), :].astype(F32)
        o_ref[0, pl.ds(r0, L), :] = (_layer_norm(o) * _silu(rg)).astype(BF16)

    return state_pass, finish_states, out_chunk


def _ret_scratch(n_chunks):
    return [pltpu.VMEM((HEAD_W, HEAD_W), F32),
            pltpu.VMEM((HEAD_W, HEAD_W), F32),
            pltpu.VMEM((n_chunks, HEAD_W, HEAD_W), BF16),
            pltpu.VMEM((n_chunks, HEAD_W, HEAD_W), BF16),
            pltpu.VMEM((4, SCAN_L, HEAD_W), F32),
            pltpu.VMEM((SCAN_L, SCAN_L), F32)]


def _mlstm_build(qp_ref, kp_ref, v_ref, mo_ref, ckp_ref, cv_ref, gt_ref, cgt_ref,
                 wq_ref, bq_ref, wk_ref, bk_ref, o_ref,
                 tab_ref, row_ref,
                 cf_ref, mf_ref, cb_ref, mb_ref, cfs_ref, mfs_ref, cbs_ref, mbs_ref, mask_ref,
                 xk_ref, xq_ref, q_ref, k_ref, ck_ref):
    L = SCAN_L
    T = qp_ref.shape[1]
    Tc = ckp_ref.shape[1]
    n_chunks = T // L
    n_ctx_chunks = Tc // L
    CV = CONV_ROWS

    def conv_stage(src_ref, xs_ref, t_len):
        xs_ref[pl.ds(0, 8), :] = jnp.zeros((8, HEAD_W), F32)
        xs_ref[pl.ds(8 + t_len, 8), :] = jnp.zeros((8, HEAD_W), F32)
        xs_ref[pl.ds(8, t_len), :] = src_ref[0].astype(F32)

    def conv_rows(xs_ref, c, w_ref, b_ref, dst_ref, scale):
        w = w_ref[...]
        r0 = pl.multiple_of(c * CV, CV)
        win = xs_ref[pl.ds(r0, CV + 16), :]
        prev = pltpu.roll(win, 1, 0)[8:8 + CV, :]
        cur = win[8:8 + CV, :]
        nxt = pltpu.roll(win, CV + 15, 0)[8:8 + CV, :]
        y = _silu(prev * w[0:1, :] + cur * w[1:2, :] + nxt * w[2:3, :] + b_ref[...])
        if scale != 1.0:
            y = y * scale
        dst_ref[pl.ds(r0, CV), :] = y.astype(BF16)

    per_chunk = L // CV

    def conv_k(c):
        for u in range(per_chunk):
            conv_rows(xk_ref, c * per_chunk + u, wk_ref, bk_ref, k_ref, KEY_SCALE)

    def conv_q(c):
        for u in range(per_chunk):
            conv_rows(xq_ref, c * per_chunk + u, wq_ref, bq_ref, q_ref, 1.0)

    conv_stage(ckp_ref, xk_ref, Tc)
    for c in range(Tc // CV):
        conv_rows(xk_ref, c, wk_ref, bk_ref, ck_ref, KEY_SCALE)
    conv_stage(kp_ref, xk_ref, T)
    conv_stage(qp_ref, xq_ref, T)
    conv_k(0)
    conv_k(n_chunks - 1)
    n_k_steps = n_chunks // 2 - 1

    def conv_step(i, keys):
        if keys:
            conv_k(i + 1)
            conv_k(n_chunks - 2 - i)
        else:
            conv_q(2 * (i - n_k_steps))
            conv_q(2 * (i - n_k_steps) + 1)

    ri = lax.broadcasted_iota(I32, (L, L), 0)
    ci = lax.broadcasted_iota(I32, (L, L), 1)
    tri_u = (ri <= ci).astype(BF16)
    lane8 = lax.broadcasted_iota(I32, (8, L), 1)
    sub8 = lax.broadcasted_iota(I32, (8, L), 0)
    sel_r = lax.broadcasted_iota(I32, (24, 8 * 128), 0) % 8
    sel_c = lax.broadcasted_iota(I32, (24, 8 * 128), 1) // 128
    sel3 = (sel_r == sel_c).astype(BF16)
    ones_cols = jnp.ones((L, AUG_W - HEAD_W), BF16)

    def chunk_tables(g8, n_used, state_only):
        i_f, i_b = g8[0], g8[2]
        lf_f, lf_b = _log_sigmoid(g8[1]), _log_sigmoid(g8[3])
        cs3 = jnp.dot(_split3(jnp.concatenate([lf_f, lf_b], axis=0)), tri_u,
                      preferred_element_type=F32)
        cs = cs3[0:16] + cs3[16:32] + cs3[32:48]
        b_f = cs[0:8]
        b_b = cs[8:16, L - 1:L] - cs[8:16] + lf_b
        z_f = i_f - b_f
        z_b = i_b - b_b
        g_f = b_f[:, L - 1:L] - b_f + i_f
        g_b = b_b[:, 0:1] - b_b + i_b
        mf, mb = z_f, z_b
        s = 1
        while s < L:
            mf = jnp.maximum(mf, jnp.where(lane8 >= s, pltpu.roll(mf, s, 1), NEG_INF))
            mb = jnp.maximum(mb, jnp.where(lane8 < L - s, pltpu.roll(mb, L - s, 1), NEG_INF))
            s *= 2
        mb = jnp.where(lane8 < L - 1, pltpu.roll(mb, L - 1, 1), NEG_INF)
        reps = [None if state_only and t not in (2, 5) else
                lax.dot_general(_split3(val), sel3[:, 0:n_used * 128], TN_DIMS, preferred_element_type=F32)
                for t, val in enumerate((mf, b_f, g_f, mb, b_b, g_b))]

        def rows_of(c):
            out = jnp.zeros((8, L), F32)
            for r, val in enumerate((z_f, z_b, g_f, g_b, b_f, b_b)):
                out = jnp.where(sub8 == r, val[c:c + 1], out)
            return out

        return rows_of, reps

    lat_rows, lat_reps = chunk_tables(gt_ref[0, 0], n_chunks, False)
    for c in range(n_chunks):
        row_ref[c] = lat_rows(c)
        for t in range(N_TAB):
            tab_ref[t, c * L:(c + 1) * L, :] = lat_reps[t][:, c * 128:(c + 1) * 128]

    def lanes2(x):
        return jnp.concatenate([x, x], axis=1)

    def advance(k, v, g_rep, g_row, b_last, c_ref, m_ref):
        m = m_ref[...]
        m_new = jnp.maximum(b_last + m, jnp.max(g_row, axis=-1, keepdims=True))
        kw = (k.astype(F32) * jnp.exp(lanes2(g_rep) - m_new)).astype(BF16)
        v_aug = jnp.concatenate([v, ones_cols], axis=1)
        c_ref[...] = jnp.exp(b_last + m - m_new) * c_ref[...] + lax.dot_general(
            kw, v_aug, TN_DIMS, preferred_element_type=F32)
        m_ref[...] = m_new

    for r in (cf_ref, mf_ref, cb_ref, mb_ref):
        r[...] = jnp.zeros_like(r)
    ctx_rows, ctx_reps = chunk_tables(cgt_ref[0, 0], n_ctx_chunks, True)
    for c in range(n_ctx_chunks):
        rows = ctx_rows(c)
        advance(ck_ref[c * L:(c + 1) * L, :], cv_ref[0, c * L:(c + 1) * L, :],
                ctx_reps[2][:, c * 128:(c + 1) * 128], rows[2:3], rows[4:5, L - 1:L], cf_ref, mf_ref)
    for c in reversed(range(n_ctx_chunks)):
        rows = ctx_rows(c)
        advance(ck_ref[c * L:(c + 1) * L, :], cv_ref[0, c * L:(c + 1) * L, :],
                ctx_reps[5][:, c * 128:(c + 1) * 128], rows[3:4], rows[5:6, 0:1], cb_ref, mb_ref)

    def state_pass(i, carry, keys):
        cb = n_chunks - 1 - i
        rf = pl.multiple_of(i * L, L)
        rb = pl.multiple_of(cb * L, L)
        cfs_ref[i] = cf_ref[...].astype(BF16)
        mfs_ref[i] = mf_ref[...]
        cbs_ref[cb] = cb_ref[...].astype(BF16)
        mbs_ref[cb] = mb_ref[...]
        rows_f = row_ref[i]
        rows_b = row_ref[cb]
        advance(k_ref[pl.ds(rf, L), :], v_ref[0, pl.ds(rf, L), :], tab_ref[2, pl.ds(rf, L), :],
                rows_f[2:3], rows_f[4:5, L - 1:L], cf_ref, mf_ref)
        advance(k_ref[pl.ds(rb, L), :], v_ref[0, pl.ds(rb, L), :], tab_ref[5, pl.ds(rb, L), :],
                rows_b[3:4], rows_b[5:6, 0:1], cb_ref, mb_ref)
        conv_step(i, keys)
        return carry

    def finish_states():
        cfs_ref[n_chunks - 1] = cf_ref[...].astype(BF16)
        mfs_ref[n_chunks - 1] = mf_ref[...]
        cbs_ref[0] = cb_ref[...].astype(BF16)
        mbs_ref[0] = mb_ref[...]

    def direction(q, v_aug, s, z_row, zmax_rep, b_rep, mask, c_in, m_in):
        mx = jnp.maximum(zmax_rep, m_in)
        att = s * jnp.exp((z_row - lanes2(mx)) + mask)
        na = jnp.dot(att.astype(BF16), v_aug, preferred_element_type=F32)
        qa = jnp.dot(q, c_in, preferred_element_type=F32)
        a = jnp.exp(m_in - mx)
        num = na[:, 0:HEAD_W] + lanes2(a) * qa[:, 0:HEAD_W]
        den = na[:, HEAD_W:] + a * qa[:, HEAD_W:]
        scale = 1.0 / jnp.maximum(jnp.abs(den), jnp.exp(-(b_rep + mx)))
        return num * lanes2(scale)

    @pl.when(pl.program_id(1) == 0)
    def _():
        mask_ref[0] = jnp.where(ci <= ri, 0.0, NEG_INF)
        mask_ref[1] = jnp.where(ci > ri, 0.0, NEG_INF)

    def out_chunk(c):
        r0 = pl.multiple_of(c * L, L)
        q = q_ref[pl.ds(r0, L), :]
        k = k_ref[pl.ds(r0, L), :]
        v_aug = jnp.concatenate([v_ref[0, pl.ds(r0, L), :], ones_cols], axis=1)
        s = lax.dot_general(q, k, NT_DIMS, preferred_element_type=F32)
        rows = row_ref[c]
        tot = direction(q, v_aug, s, rows[0:1], tab_ref[0, pl.ds(r0, L), :], tab_ref[1, pl.ds(r0, L), :],
                        mask_ref[0], cfs_ref[c], mfs_ref[c])
        tot = tot + direction(q, v_aug, s, rows[1:2], tab_ref[3, pl.ds(r0, L), :],
                              tab_ref[4, pl.ds(r0, L), :], mask_ref[1], cbs_ref[c], mbs_ref[c])
        mo = mo_ref[0, pl.ds(r0, L), :].astype(F32)
        o_ref[0, pl.ds(r0, L), :] = (_layer_norm(tot) * jax.nn.sigmoid(mo)).astype(BF16)

    return state_pass, finish_states, out_chunk, n_k_steps


def _mlstm_scratch(T, Tc, n_chunks):
    state = [pltpu.VMEM((HEAD_W, AUG_W), F32), pltpu.VMEM((1, 1), F32)]
    snaps = [pltpu.VMEM((n_chunks, HEAD_W, AUG_W), BF16), pltpu.VMEM((n_chunks, 1, 1), F32)]
    return [pltpu.VMEM((N_TAB, T, 128), F32), pltpu.VMEM((n_chunks, 8, SCAN_L), F32)] \
        + state + state + snaps + snaps + [pltpu.VMEM((2, SCAN_L, SCAN_L), F32)] \
        + [pltpu.VMEM((T + 16, HEAD_W), F32), pltpu.VMEM((T + 16, HEAD_W), F32),
           pltpu.VMEM((T, HEAD_W), BF16), pltpu.VMEM((T, HEAD_W), BF16), pltpu.VMEM((Tc, HEAD_W), BF16)]


def _scan_kernel(n_ret_scratch, dl_ref, rq_ref, rk_ref, rv_ref, rg_ref, rck_ref, rcv_ref,
                 mq_ref, mk_ref, mv_ref, mo_ref, mck_ref, mcv_ref, gt_ref, cgt_ref,
                 wq_ref, bq_ref, wk_ref, bk_ref, r_ref, m_ref, *scratch):
    n_chunks = rq_ref.shape[1] // SCAN_L
    ret = _ret_build(dl_ref, rq_ref, rk_ref, rv_ref, rg_ref, rck_ref, rcv_ref, r_ref,
                     *scratch[:n_ret_scratch])
    mls = _mlstm_build(mq_ref, mk_ref, mv_ref, mo_ref, mck_ref, mcv_ref, gt_ref, cgt_ref,
                       wq_ref, bq_ref, wk_ref, bk_ref, m_ref, *scratch[n_ret_scratch:])

    def state_pass(i, carry, keys):
        ret[0](i, carry)
        mls[0](i, carry, keys)
        return carry

    n_k_steps = mls[3]
    lax.fori_loop(0, n_k_steps, functools.partial(state_pass, keys=True), 0)
    lax.fori_loop(n_k_steps, n_chunks - 1, functools.partial(state_pass, keys=False), 0)
    ret[1]()
    mls[1]()

    def out_pass(i, carry):
        for c in (2 * i, 2 * i + 1):
            ret[2](c)
            mls[2](c)
        return carry

    lax.fori_loop(0, n_chunks // 2, out_pass, 0)


def _scans(decay_logit, p_lat, p_ctx, gt, cgt, conv_w, conv_b, ret_lat, ret_ctx, ml_lat, ml_ctx):
    B, T, _ = p_lat.shape
    Tc = p_ctx.shape[1]
    assert T % (2 * SCAN_L) == 0 and Tc % SCAN_L == 0 and T // SCAN_L <= 8
    n_chunks = T // SCAN_L

    def lat(sec):
        return pl.BlockSpec((1, T, HEAD_W), lambda h, b: (b, 0, sec * HEADS + h))

    def cx(sec):
        return pl.BlockSpec((1, Tc, HEAD_W), lambda h, b: (b, 0, sec * HEADS + h))

    gates = pl.BlockSpec((1, 1, N_GK, 8, SCAN_L), lambda h, b: (b, h, 0, 0, 0))
    out = pl.BlockSpec((1, T, HEAD_W), lambda h, b: (b, 0, h))
    conv_specs = [pl.BlockSpec((3, HEAD_W), lambda h, b: (0, h)),
                  pl.BlockSpec((1, HEAD_W), lambda h, b: (0, h)),
                  pl.BlockSpec((3, HEAD_W), lambda h, b: (0, HEADS + h)),
                  pl.BlockSpec((1, HEAD_W), lambda h, b: (0, HEADS + h))]
    ret_scratch = _ret_scratch(n_chunks)
    return pl.pallas_call(
        functools.partial(_scan_kernel, len(ret_scratch)),
        out_shape=(jax.ShapeDtypeStruct((B, T, BRANCH_W), BF16),
                   jax.ShapeDtypeStruct((B, T, BRANCH_W), BF16)),
        grid=(HEADS, B),
        in_specs=[pl.BlockSpec(memory_space=pltpu.SMEM)]
        + [lat(s) for s in ret_lat] + [cx(s) for s in ret_ctx]
        + [lat(s) for s in ml_lat] + [cx(s) for s in ml_ctx] + [gates, gates] + conv_specs,
        out_specs=(out, out),
        scratch_shapes=ret_scratch + _mlstm_scratch(T, Tc, n_chunks),
        compiler_params=_cparams("arbitrary", "arbitrary"),
        name="scans",
    )(decay_logit, *([p_lat] * 4), *([p_ctx] * 2), *([p_lat] * 4), *([p_ctx] * 2), gt, cgt,
      conv_w, conv_b, conv_w, conv_b)


def _merge_kernel(alpha, r_ref, m_ref, gr_ref, gm_ref, x_ref, g1_ref, sh2_ref, sc2_ref,
                  lng_ref, lnb_ref, wr_ref, wm_ref, wo_ref, wrt_ref, brt_ref,
                  x1_ref, ua_ref, ub_ref, ri_ref, rw_ref, cnt_ref, carry_ref, u_ref):
    @pl.when(jnp.logical_and(pl.program_id(0) == 0, pl.program_id(1) == 0))
    def _():
        carry_ref[...] = jnp.zeros_like(carry_ref)
        tm = x_ref.shape[1]
        r = lax.broadcasted_iota(I32, (tm, tm), 0)
        c = lax.broadcasted_iota(I32, (tm, tm), 1)
        u_ref[...] = (r < c).astype(BF16)

    yr = jnp.dot(r_ref[0], wr_ref[...], preferred_element_type=F32)
    ym = jnp.dot(m_ref[0], wm_ref[...], preferred_element_type=F32)
    y = jax.nn.sigmoid(gr_ref[0].astype(F32)) * yr + jax.nn.sigmoid(gm_ref[0].astype(F32)) * ym
    yo = jnp.dot(y.astype(BF16), wo_ref[...], preferred_element_type=F32)
    x1 = _layer_norm(alpha * x_ref[0] + g1_ref[0] * yo) * lng_ref[...] + lnb_ref[...]
    x1_ref[0] = x1
    u2 = _layer_norm(x1) * (1.0 + sc2_ref[0]) + sh2_ref[0]
    ua_ref[0] = _pack_pairs(u2[:, 0:PACK_W], u2[:, PACK_W:2 * PACK_W])
    ub_ref[0] = _pack_pairs(u2[:, 2 * PACK_W:3 * PACK_W], u2[:, 3 * PACK_W:4 * PACK_W])
    lt = lax.dot_general(wrt_ref[...], u2.astype(BF16), NT_DIMS, preferred_element_type=F32) + brt_ref[...]
    _route_tile(lt, ri_ref, rw_ref, cnt_ref, carry_ref, u_ref)


def _route_tile(lt, ri_ref, rw_ref, cnt_ref, carry_ref, u_ref):
    tm = lt.shape[1]
    lg = lt[0:N_GROUPS, :]
    eg = jnp.exp(lg - jnp.max(lg, axis=0, keepdims=True))
    pg = eg / jnp.sum(eg, axis=0, keepdims=True)
    pg_top = jnp.max(pg, axis=0, keepdims=True)
    rows_g = lax.broadcasted_iota(I32, pg.shape, 0)
    g_idx = jnp.min(jnp.where(pg == pg_top, rows_g, N_GROUPS), axis=0, keepdims=True)

    le = jnp.zeros((EXP_PER_GROUP, tm), F32)
    for g in range(N_GROUPS):
        lo = 8 + g * EXP_PER_GROUP
        le = jnp.where(g_idx == g, lt[lo:lo + EXP_PER_GROUP, :], le)
    ee = jnp.exp(le - jnp.max(le, axis=0, keepdims=True))
    pe = ee / jnp.sum(ee, axis=0, keepdims=True)
    rows_e = lax.broadcasted_iota(I32, pe.shape, 0)
    v1 = jnp.max(pe, axis=0, keepdims=True)
    i1 = jnp.min(jnp.where(pe == v1, rows_e, EXP_PER_GROUP), axis=0, keepdims=True)
    pe2 = jnp.where(rows_e == i1, -1.0, pe)
    v2 = jnp.max(pe2, axis=0, keepdims=True)
    i2 = jnp.min(jnp.where(pe2 == v2, rows_e, EXP_PER_GROUP), axis=0, keepdims=True)
    den = v1 + v2
    rw_ref[...] = jnp.zeros_like(rw_ref)
    rw_ref[0:1, :] = pg_top * v1 / den
    rw_ref[1:2, :] = pg_top * v2 / den
    e1 = g_idx * EXP_PER_GROUP + i1
    e2 = g_idx * EXP_PER_GROUP + i2

    rows_x = lax.broadcasted_iota(I32, (N_EXPERTS, tm), 0)
    oh1 = (rows_x == e1).astype(F32)
    oh2 = (rows_x == e2).astype(F32)
    both = oh1 + oh2
    before = carry_ref[:, 0:1] + jnp.dot(both.astype(BF16), u_ref[...], preferred_element_type=F32)
    ri_ref[0:1, :] = e1
    ri_ref[1:2, :] = e2
    ri_ref[2:3, :] = jnp.sum(oh1 * before, axis=0, keepdims=True).astype(I32)
    ri_ref[3:4, :] = jnp.sum(oh2 * before, axis=0, keepdims=True).astype(I32)
    carry_ref[...] = carry_ref[...] + jnp.sum(both, axis=1, keepdims=True)
    cnt_ref[...] = carry_ref[...].astype(I32)


def _merge(alpha, r, m, p_lat, sec_gates, x, g1, sh2, sc2, lng, lnb, wr, wm, wo, wrt, brt):
    B, T, D = x.shape
    tm = MERGE_TM
    per_b = T // tm
    n = B * T

    def tile(w):
        return pl.BlockSpec((1, tm, w), lambda b, i: (b, i, 0))

    def sec(s):
        return pl.BlockSpec((1, tm, BRANCH_W), lambda b, i: (b, i, s))

    def mod():
        return pl.BlockSpec((1, 1, D), lambda b, i: (b, 0, 0))

    def const(shape):
        return pl.BlockSpec(shape, lambda b, i: (0,) * len(shape))

    return pl.pallas_call(
        functools.partial(_merge_kernel, alpha),
        out_shape=(jax.ShapeDtypeStruct((B, T, D), F32),
                   jax.ShapeDtypeStruct((B, T, PACK_W), U32),
                   jax.ShapeDtypeStruct((B, T, PACK_W), U32),
                   jax.ShapeDtypeStruct((4, n), I32),
                   jax.ShapeDtypeStruct((8, n), F32),
                   jax.ShapeDtypeStruct((N_EXPERTS, 128), I32)),
        grid=(B, per_b),
        in_specs=[tile(BRANCH_W), tile(BRANCH_W), sec(sec_gates[0]), sec(sec_gates[1]), tile(D),
                  mod(), mod(), mod(), const((1, D)), const((1, D)),
                  const((BRANCH_W, D)), const((BRANCH_W, D)), const((D, D)),
                  const((ROUTE_ROWS, D)), const((ROUTE_ROWS, 1))],
        out_specs=(tile(D), tile(PACK_W), tile(PACK_W),
                   pl.BlockSpec((4, tm), lambda b, i: (0, b * per_b + i)),
                   pl.BlockSpec((8, tm), lambda b, i: (0, b * per_b + i)),
                   pl.BlockSpec((N_EXPERTS, 128), lambda b, i: (0, 0))),
        scratch_shapes=[pltpu.VMEM((N_EXPERTS, 128), F32), pltpu.VMEM((tm, tm), BF16)],
        compiler_params=_cparams("arbitrary", "arbitrary"),
        name="merge",
    )(r, m, p_lat, p_lat, x, g1, sh2, sc2, lng, lnb, wr, wm, wo, wrt, brt)


def _sc_mesh():
    return plsc.VectorSubcoreMesh(core_axis_name="c", subcore_axis_name="s")


def _sc_scatter2(rows_a, rows_b, idx0, idx1, n_out):
    m, w = rows_a.shape
    out = jax.ShapeDtypeStruct((n_out, w), rows_a.dtype)

    @functools.partial(pl.kernel, out_type=(out, out), mesh=_sc_mesh(), scratch_types=[])
    def k(xa_hbm, xb_hbm, i0_hbm, i1_hbm, oa_hbm, ob_hbm):
        for x_hbm, o_hbm in ((xa_hbm, oa_hbm), (xb_hbm, ob_hbm)):
            def body(x_vmem, i0_vmem, i1_vmem, o_hbm=o_hbm):
                pltpu.sync_copy(x_vmem, o_hbm.at[i0_vmem.at[0]])
                pltpu.sync_copy(x_vmem, o_hbm.at[i1_vmem.at[0]])

            pltpu.emit_pipeline(
                body,
                grid=(m // SC_WIN,),
                in_specs=[pl.BlockSpec((SC_WIN, w), lambda i: (i, 0)),
                          pl.BlockSpec((1, SC_WIN), lambda i: (0, i)),
                          pl.BlockSpec((1, SC_WIN), lambda i: (0, i))],
                out_specs=[],
                core_axis_name=("c", "s"),
                dimension_semantics=(pltpu.PARALLEL,),
            )(x_hbm, i0_hbm, i1_hbm)

    return k(rows_a, rows_b, idx0.reshape(1, m), idx1.reshape(1, m))


def _sc_gather(table_a, table_b, idx):
    m = idx.shape[0]
    w = table_a.shape[1]
    out = jax.ShapeDtypeStruct((m, w), table_a.dtype)

    @functools.partial(pl.kernel, out_type=(out, out), mesh=_sc_mesh(), scratch_types=[])
    def k(ta_hbm, tb_hbm, i_hbm, oa_hbm, ob_hbm):
        for t_hbm, o_hbm in ((ta_hbm, oa_hbm), (tb_hbm, ob_hbm)):
            def body(i_vmem, o_vmem, t_hbm=t_hbm):
                pltpu.sync_copy(t_hbm.at[i_vmem.at[0]], o_vmem)

            pltpu.emit_pipeline(
                body,
                grid=(m // SC_WIN,),
                in_specs=[pl.BlockSpec((1, SC_WIN), lambda i: (0, i))],
                out_specs=[pl.BlockSpec((SC_WIN, w), lambda i: (i, 0))],
                core_axis_name=("c", "s"),
                dimension_semantics=(pltpu.PARALLEL,),
            )(i_hbm, o_hbm)

    return k(table_a, table_b, idx.reshape(1, m))


def _expert_kernel(be_ref, nv_ref, xa_ref, xb_ref, w1f_ref, w3f_ref, w2f_ref, ya_ref, yb_ref,
                   w1_ref, w3_ref, w2_ref):
    j = pl.program_id(0)
    nv = nv_ref[j]

    @pl.when(jnp.logical_or(j == 0, be_ref[j] != be_ref[jnp.maximum(j - 1, 0)]))
    def _():
        w1_ref[0] = w1f_ref[0].astype(BF16)
        w3_ref[0] = w3f_ref[0].astype(BF16)
        w2_ref[0] = w2f_ref[0].astype(BF16)

    @pl.when(nv > 0)
    def _():
        valid = lax.broadcasted_iota(I32, xa_ref.shape, 0) < nv
        zero = jnp.zeros(xa_ref.shape, U32)
        parts = _unpack_pairs(jnp.where(valid, xa_ref[...], zero)) + \
            _unpack_pairs(jnp.where(valid, xb_ref[...], zero))
        x = jnp.concatenate([p.astype(BF16) for p in parts], axis=1)
        h1 = jnp.dot(x, w1_ref[0], preferred_element_type=F32)
        h3 = jnp.dot(x, w3_ref[0], preferred_element_type=F32)
        y = jnp.dot((_silu(h1) * h3).astype(BF16), w2_ref[0], preferred_element_type=F32)
        ya_ref[...] = _pack_pairs(y[:, 0:PACK_W], y[:, PACK_W:2 * PACK_W])
        yb_ref[...] = _pack_pairs(y[:, 2 * PACK_W:3 * PACK_W], y[:, 3 * PACK_W:4 * PACK_W])

    @pl.when(nv == 0)
    def _():
        ya_ref[...] = jnp.zeros_like(ya_ref)
        yb_ref[...] = jnp.zeros_like(yb_ref)


def _experts(block_exp, n_valid, xa, xb, w1, w3, w2):
    n_slots = xa.shape[0]
    n_blocks = n_slots // MOE_BLK
    d, de = w1.shape[1], w1.shape[2]
    slot = pl.BlockSpec((MOE_BLK, PACK_W), lambda j, be, nv: (j, 0))
    grid_spec = pltpu.PrefetchScalarGridSpec(
        num_scalar_prefetch=2,
        grid=(n_blocks,),
        in_specs=[slot, slot,
                  pl.BlockSpec((1, d, de), lambda j, be, nv: (be[j], 0, 0)),
                  pl.BlockSpec((1, d, de), lambda j, be, nv: (be[j], 0, 0)),
                  pl.BlockSpec((1, de, d), lambda j, be, nv: (be[j], 0, 0))],
        out_specs=(slot, slot),
        scratch_shapes=[pltpu.VMEM((1, d, de), BF16), pltpu.VMEM((1, d, de), BF16),
                        pltpu.VMEM((1, de, d), BF16)],
    )
    return pl.pallas_call(
        _expert_kernel,
        out_shape=(jax.ShapeDtypeStruct((n_slots, PACK_W), U32),
                   jax.ShapeDtypeStruct((n_slots, PACK_W), U32)),
        grid_spec=grid_spec,
        compiler_params=_cparams("arbitrary"),
        name="experts",
    )(block_exp, n_valid, xa, xb, w1, w3, w2)


def _final_kernel(alpha, x1_ref, a0_ref, b0_ref, a1_ref, b1_ref, w_ref, g2_ref, lng_ref, lnb_ref, o_ref):
    w = w_ref[...].T
    w0 = w[:, 0:1]
    w1 = w[:, 1:2]
    parts0 = _unpack_pairs(a0_ref[...]) + _unpack_pairs(b0_ref[...])
    parts1 = _unpack_pairs(a1_ref[...]) + _unpack_pairs(b1_ref[...])
    f = jnp.concatenate([w0 * p0 + w1 * p1 for p0, p1 in zip(parts0, parts1)], axis=1)
    o_ref[0] = _layer_norm(alpha * x1_ref[0] + g2_ref[0] * f) * lng_ref[...] + lnb_ref[...]


def _final(alpha, x1, ya, yb, w, g2, lng, lnb):
    B, T, D = x1.shape
    tm = min(FINAL_TM, T)
    per_b = T // tm
    n_tiles = B * per_b

    def rows(k):
        return pl.BlockSpec((tm, PACK_W), lambda b, i: (k * n_tiles + b * per_b + i, 0))

    return pl.pallas_call(
        functools.partial(_final_kernel, alpha),
        out_shape=jax.ShapeDtypeStruct((B, T, D), F32),
        grid=(B, per_b),
        in_specs=[pl.BlockSpec((1, tm, D), lambda b, i: (b, i, 0)),
                  rows(0), rows(0), rows(1), rows(1),
                  pl.BlockSpec((8, tm), lambda b, i: (0, b * per_b + i)),
                  pl.BlockSpec((1, 1, D), lambda b, i: (b, 0, 0)),
                  pl.BlockSpec((1, D), lambda b, i: (0, 0)),
                  pl.BlockSpec((1, D), lambda b, i: (0, 0))],
        out_specs=pl.BlockSpec((1, tm, D), lambda b, i: (b, i, 0)),
        compiler_params=_cparams("parallel", "parallel"),
        name="final",
    )(x1, ya, yb, ya, yb, w, g2, lng, lnb)


def _rotary_tables(T):
    quarter = HEAD_W // 4
    freqs = ROPE_BASE ** (-jnp.arange(quarter, dtype=F32) / quarter)
    t = jnp.arange(T)
    ang_r = (t // GRID_W).astype(F32)[:, None] * freqs[None, :]
    ang_c = (t % GRID_W).astype(F32)[:, None] * freqs[None, :]
    cos = jnp.concatenate([jnp.cos(ang_r)] * 2 + [jnp.cos(ang_c)] * 2, axis=1)
    sin = jnp.concatenate([-jnp.sin(ang_r), jnp.sin(ang_r), -jnp.sin(ang_c), jnp.sin(ang_c)], axis=1)
    return cos, sin


def _per_head_gates(gt):
    B, _, T = gt.shape
    n_chunks = T // SCAN_L
    gth = gt.reshape(B, N_GK, HEADS, n_chunks, SCAN_L).transpose(0, 2, 1, 3, 4)
    return jnp.pad(gth, ((0, 0), (0, 0), (0, 0), (0, 8 - n_chunks), (0, 0)))


def _table_lookup(table, idx):
    sel = idx[..., None] == jnp.arange(table.shape[0], dtype=idx.dtype)
    return jnp.sum(jnp.where(sel, table, 0), axis=-1)


def kernel(x, c, ctx, c_ctx, w_ada, b_ada, w_in, b_mgate, ml_conv_w, ml_conv_b, ret_decay_logit, w_ret_branch, w_ml_branch, w_out, ln1_g, ln1_b, w_rg, b_rg, w_re, b_re, w_e1, w_e3, w_e2, ln2_g, ln2_b):
    B, T, D = x.shape
    depth = w_ada.shape[0]
    assert depth == 1 and D == BRANCH_W and T % GRID_W == 0
    alpha = (2 * depth) ** 0.25
    n_tok = B * T

    n_rows = -(-(B + 1) // 8) * 8
    cs = jnp.zeros((n_rows, D), F32).at[:B].set(c).at[B].set(c_ctx)
    mod = _ada(cs, w_ada[0], b_ada[0][None, :])
    sh1, sc1, g1, sh2, sc2, g2 = [mod[:B, None, i * D:(i + 1) * D] for i in range(6)]
    csh1 = mod[B, 0 * D:1 * D].reshape(1, 1, D)
    csc1 = mod[B, 1 * D:2 * D].reshape(1, 1, D)

    w = w_in[0]
    sec_w = [w[:, s * BRANCH_W:(s + 1) * BRANCH_W] for s in range(8)]
    g_lo = 8 * BRANCH_W
    w_gate_t = w[:, g_lo:g_lo + N_GATES].T.astype(BF16)
    b_gate = b_mgate[0][:, None]
    sec_w += [w[:, g_lo + N_GATES:g_lo + N_GATES + D], w[:, g_lo + N_GATES + D:]]
    w_lat = jnp.concatenate(sec_w, axis=1).astype(BF16)
    w_ctx = jnp.concatenate([sec_w[1], sec_w[2], sec_w[5], sec_w[6]], axis=1).astype(BF16)
    kinds_lat = ("rot", "rot_scale") + ("plain",) * 8
    kinds_ctx = ("scale", "plain", "plain", "plain")
    p_lat, gt_lat = _proj(x, sh1, sc1, w_lat, w_gate_t, b_gate, kinds_lat, _rotary_tables(T))
    Tc = ctx.shape[1]
    p_ctx, gt_ctx = _proj(ctx.reshape(1, B * Tc, D), csh1, csc1, w_ctx, w_gate_t, b_gate, kinds_ctx)
    p_ctx = p_ctx.reshape(B, Tc, -1)
    gt_ctx = gt_ctx.reshape(N_GATES, B, Tc).transpose(1, 0, 2)

    ret, mls = _scans(ret_decay_logit[0], p_lat, p_ctx, _per_head_gates(gt_lat), _per_head_gates(gt_ctx),
                      ml_conv_w[0], ml_conv_b[0][None, :], (0, 1, 2, 3), (0, 1), (4, 5, 6, 7), (2, 3))

    wrt = jnp.zeros((ROUTE_ROWS, D), F32).at[:N_GROUPS].set(w_rg[0].T).at[8:8 + N_EXPERTS].set(w_re[0].T)
    brt = jnp.zeros((ROUTE_ROWS, 1), F32).at[:N_GROUPS, 0].set(b_rg[0]).at[8:8 + N_EXPERTS, 0].set(b_re[0])
    x1, ua, ub, ri, rw, cnt = _merge(alpha, ret, mls, p_lat, (8, 9), x, g1, sh2, sc2,
                                     ln1_g[0][None, :], ln1_b[0][None, :],
                                     w_ret_branch[0].astype(BF16), w_ml_branch[0].astype(BF16),
                                     w_out[0].astype(BF16), wrt.astype(BF16), brt)

    counts = cnt[:, 0]
    padded = (counts + MOE_BLK - 1) // MOE_BLK * MOE_BLK
    pad_end = jnp.cumsum(padded)
    pad_off = pad_end - padded
    dest = _table_lookup(pad_off, ri[0:2]) + ri[2:4]
    n_blocks = (2 * n_tok) // MOE_BLK + N_EXPERTS
    n_slots = n_blocks * MOE_BLK
    block_start = jnp.arange(n_blocks, dtype=I32) * MOE_BLK
    block_exp = jnp.minimum((block_start[:, None] >= pad_end[None, :]).sum(1), N_EXPERTS - 1).astype(I32)
    n_valid = jnp.clip(_table_lookup(counts, block_exp) - (block_start - _table_lookup(pad_off, block_exp)),
                       0, MOE_BLK).astype(I32)

    xa, xb = _sc_scatter2(ua.reshape(n_tok, PACK_W), ub.reshape(n_tok, PACK_W), dest[0], dest[1], n_slots)
    ya, yb = _experts(block_exp, n_valid, xa, xb, w_e1[0], w_e3[0], w_e2[0])
    ga, gb = _sc_gather(ya, yb, dest.reshape(2 * n_tok))
    return _final(alpha, x1, ga, gb, rw, g2, ln2_g[0][None, :], ln2_b[0][None, :])
```

```python
import functools

import jax
import jax.numpy as jnp
from jax import lax
from jax.experimental import pallas as pl
from jax.experimental.pallas import tpu as pltpu
from jax.experimental.pallas import tpu_sc as plsc

F32 = jnp.float32
BF16 = jnp.bfloat16
U32 = jnp.uint32
I32 = jnp.int32
HIGHEST = lax.Precision.HIGHEST

HEADS = 4
HEAD_W = 256
BRANCH_W = HEADS * HEAD_W
GRID_W = 64
ROPE_BASE = 10000.0
N_GATES = 16
N_GK = N_GATES // HEADS
N_GROUPS = 4
EXP_PER_GROUP = 8
N_EXPERTS = N_GROUPS * EXP_PER_GROUP
LN_EPS = 1e-5
NEG_INF = -1e30
KEY_SCALE = HEAD_W ** -0.5

SCAN_L = 256
CONV_ROWS = 128
PROJ_TM = 2048
PROJ_SUB = 256
MERGE_TM = 512
FINAL_TM = 1024
MOE_BLK = 512
SC_WIN = 128
PACK_W = 256
ROUTE_ROWS = 64
N_TAB = 6
AUG_W = HEAD_W + 128
VMEM_LIMIT = 48 * 1024 * 1024

NT_DIMS = (((1,), (1,)), ((), ()))
TN_DIMS = (((0,), (0,)), ((), ()))


def _cparams(*sem):
    return pltpu.CompilerParams(dimension_semantics=sem, vmem_limit_bytes=VMEM_LIMIT)


def _layer_norm(x):
    mu = jnp.mean(x, axis=-1, keepdims=True)
    xc = x - mu
    var = jnp.mean(xc * xc, axis=-1, keepdims=True)
    return xc * lax.rsqrt(var + LN_EPS)


def _log_sigmoid(x):
    return jnp.minimum(x, 0.0) - jnp.log1p(jnp.exp(-jnp.abs(x)))


def _silu(x):
    return x * jax.nn.sigmoid(x)


def _pack_pairs(hi, lo):
    hb = lax.bitcast_convert_type(hi.astype(BF16).astype(F32), U32)
    lb = lax.bitcast_convert_type(lo.astype(BF16).astype(F32), U32)
    return (hb & jnp.uint32(0xFFFF0000)) | (lb >> 16)


def _unpack_pairs(p):
    hi = lax.bitcast_convert_type(p & jnp.uint32(0xFFFF0000), F32)
    lo = lax.bitcast_convert_type(p << 16, F32)
    return hi, lo


def _split3(x):
    hi = x.astype(BF16).astype(F32)
    r1 = x - hi
    mid = r1.astype(BF16).astype(F32)
    lo = (r1 - mid).astype(BF16).astype(F32)
    return jnp.concatenate([hi, mid, lo], axis=0).astype(BF16)


def _ada_kernel(c_ref, w_ref, b_ref, o_ref):
    s = _silu(c_ref[...])
    o_ref[...] = jnp.dot(s, w_ref[...], precision=HIGHEST, preferred_element_type=F32) + b_ref[...]


def _ada(cs, w, b):
    rows, d = cs.shape
    cols = w.shape[1]
    tn = 1024
    return pl.pallas_call(
        _ada_kernel,
        out_shape=jax.ShapeDtypeStruct((rows, cols), F32),
        grid=(cols // tn,),
        in_specs=[pl.BlockSpec((rows, d), lambda j: (0, 0)),
                  pl.BlockSpec((d, tn), lambda j: (0, j)),
                  pl.BlockSpec((1, tn), lambda j: (0, j))],
        out_specs=pl.BlockSpec((rows, tn), lambda j: (0, j)),
        compiler_params=_cparams("parallel"),
        name="ada",
    )(cs, w, b)


def _proj_kernel(kinds, x_ref, sh_ref, sc_ref, w_ref, wg_ref, bg_ref, *rest):
    if "rot" in kinds or "rot_scale" in kinds:
        cos_ref, sin_ref, o_ref, gt_ref, u_ref = rest
    else:
        o_ref, gt_ref, u_ref = rest
    j = pl.program_id(2)
    tm = x_ref.shape[1]
    sub = min(PROJ_SUB, tm)

    def rotary(acc, rows, scale):
        for s in range(acc.shape[1] // 128):
            a = acc[:, s * 128:(s + 1) * 128]
            half = s % 2
            cs = cos_ref[rows, half * 128:(half + 1) * 128]
            sn = sin_ref[rows, half * 128:(half + 1) * 128]
            r = a * cs + pltpu.roll(a, 64, 1) * sn
            if scale != 1.0:
                r = r * scale
            o_ref[0, rows, s * 128:(s + 1) * 128] = r.astype(BF16)

    def section(kind, first):
        for r in range(tm // sub):
            rows = slice(r * sub, (r + 1) * sub)
            if first:
                u = _layer_norm(x_ref[0, rows, :]) * (1.0 + sc_ref[0]) + sh_ref[0]
                ub = u.astype(BF16)
                u_ref[rows, :] = ub
                gt_ref[0, :, rows] = lax.dot_general(wg_ref[...], ub, NT_DIMS,
                                                     preferred_element_type=F32) + bg_ref[...]
            else:
                ub = u_ref[rows, :]
            acc = jnp.dot(ub, w_ref[...], preferred_element_type=F32)
            if kind == "rot":
                rotary(acc, rows, 1.0)
            elif kind == "rot_scale":
                rotary(acc, rows, KEY_SCALE)
            elif kind == "scale":
                o_ref[0, rows, :] = (acc * KEY_SCALE).astype(BF16)
            else:
                o_ref[0, rows, :] = acc.astype(BF16)

    variants = {}
    for s, kind in enumerate(kinds):
        variants.setdefault((kind, s == 0), []).append(s)
    for (kind, first), secs in variants.items():
        cond = functools.reduce(jnp.logical_or, [j == s for s in secs])

        @pl.when(cond)
        def _(kind=kind, first=first):
            section(kind, first)


def _proj(x, sh, sc, w_main, w_gate_t, b_gate, kinds, tables=None):
    B, T, D = x.shape
    n_sec = len(kinds)
    tm = min(PROJ_TM, T)
    tn = BRANCH_W
    assert T % tm == 0
    in_specs = [
        pl.BlockSpec((1, tm, D), lambda i, b, j: (b, i, 0)),
        pl.BlockSpec((1, 1, D), lambda i, b, j: (b, 0, 0)),
        pl.BlockSpec((1, 1, D), lambda i, b, j: (b, 0, 0)),
        pl.BlockSpec((D, tn), lambda i, b, j: (0, j)),
        pl.BlockSpec((N_GATES, D), lambda i, b, j: (0, 0)),
        pl.BlockSpec((N_GATES, 1), lambda i, b, j: (0, 0)),
    ]
    args = [x, sh, sc, w_main, w_gate_t, b_gate]
    if tables is not None:
        in_specs += [pl.BlockSpec((tm, HEAD_W), lambda i, b, j: (i, 0))] * 2
        args += list(tables)
    return pl.pallas_call(
        functools.partial(_proj_kernel, kinds),
        out_shape=(jax.ShapeDtypeStruct((B, T, n_sec * tn), BF16),
                   jax.ShapeDtypeStruct((B, N_GATES, T), F32)),
        grid=(T // tm, B, n_sec),
        in_specs=in_specs,
        out_specs=(pl.BlockSpec((1, tm, tn), lambda i, b, j: (b, i, j)),
                   pl.BlockSpec((1, N_GATES, tm), lambda i, b, j: (b, 0, i))),
        scratch_shapes=[pltpu.VMEM((tm, D), BF16)],
        compiler_params=_cparams("parallel", "parallel", "arbitrary"),
        name="proj_lat" if tables is not None else "proj_ctx",
    )(*args)


def _ret_build(dl_ref, q_ref, k_ref, v_ref, rg_ref, ck_ref, cv_ref, o_ref,
               sf_ref, sb_ref, fs_ref, bs_ref, dec_ref, d_ref):
    h = pl.program_id(0)
    L = SCAN_L
    n_chunks = q_ref.shape[1] // L
    n_ctx_chunks = ck_ref.shape[1] // L
    lgf = _log_sigmoid(jnp.full((1, 1), dl_ref[0, h], F32))
    lgb = _log_sigmoid(jnp.full((1, 1), dl_ref[1, h], F32))

    @pl.when(pl.program_id(1) == 0)
    def _():
        ri = lax.broadcasted_iota(I32, (L, L), 0)
        ci = lax.broadcasted_iota(I32, (L, L), 1)
        rel = (ri - ci).astype(F32)
        d_ref[...] = jnp.where(rel >= 0.0, jnp.exp(jnp.maximum(rel, 0.0) * lgf),
                               jnp.exp(jnp.maximum(-rel, 0.0) * lgb))
        row = lax.broadcasted_iota(I32, (L, HEAD_W), 0).astype(F32)
        dec_ref[0] = jnp.exp((row + 1.0) * lgf)
        dec_ref[1] = jnp.exp((L - 1.0 - row) * lgf)
        dec_ref[2] = jnp.exp((L - row) * lgb)
        dec_ref[3] = jnp.exp(row * lgb)

    cdf = jnp.exp(L * lgf)
    cdb = jnp.exp(L * lgb)

    def update(s_ref, kc, vc, kd, cd):
        kdec = (kc.astype(F32) * kd).astype(BF16)
        s_ref[...] = s_ref[...] * cd + lax.dot_general(kdec, vc, TN_DIMS, preferred_element_type=F32)

    sf_ref[...] = jnp.zeros_like(sf_ref)
    sb_ref[...] = jnp.zeros_like(sb_ref)
    for c in range(n_ctx_chunks):
        update(sf_ref, ck_ref[0, c * L:(c + 1) * L, :], cv_ref[0, c * L:(c + 1) * L, :], dec_ref[1], cdf)
    for c in reversed(range(n_ctx_chunks)):
        update(sb_ref, ck_ref[0, c * L:(c + 1) * L, :], cv_ref[0, c * L:(c + 1) * L, :], dec_ref[3], cdb)

    def state_pass(i, carry):
        cb = n_chunks - 1 - i
        rf = pl.multiple_of(i * L, L)
        rb = pl.multiple_of(cb * L, L)
        fs_ref[i] = sf_ref[...].astype(BF16)
        bs_ref[cb] = sb_ref[...].astype(BF16)
        update(sf_ref, k_ref[0, pl.ds(rf, L), :], v_ref[0, pl.ds(rf, L), :], dec_ref[1], cdf)
        update(sb_ref, k_ref[0, pl.ds(rb, L), :], v_ref[0, pl.ds(rb, L), :], dec_ref[3], cdb)
        return carry

    def finish_states():
        fs_ref[n_chunks - 1] = sf_ref[...].astype(BF16)
        bs_ref[0] = sb_ref[...].astype(BF16)

    def out_chunk(c):
        r0 = pl.multiple_of(c * L, L)
        q = q_ref[0, pl.ds(r0, L), :]
        k = k_ref[0, pl.ds(r0, L), :]
        v = v_ref[0, pl.ds(r0, L), :]
        s = lax.dot_general(q, k, NT_DIMS, preferred_element_type=F32)
        att = (s * d_ref[...]).astype(BF16)
        o = jnp.dot(att, v, preferred_element_type=F32)
        o = o + jnp.dot(q, fs_ref[c], preferred_element_type=F32) * dec_ref[0]
        o = o + jnp.dot(q, bs_ref[c], preferred_element_type=F32) * dec_ref[2]
        rg = rg_ref[0, pl.ds(r0, L), :].astype(F32)
        o_ref[0, pl.ds(r0, L), :] = (_layer_norm(o) * _silu(rg)).astype(BF16)

    return state_pass, finish_states, out_chunk


def _ret_scratch(n_chunks):
    return [pltpu.VMEM((HEAD_W, HEAD_W), F32),
            pltpu.VMEM((HEAD_W, HEAD_W), F32),
            pltpu.VMEM((n_chunks, HEAD_W, HEAD_W), BF16),
            pltpu.VMEM((n_chunks, HEAD_W, HEAD_W), BF16),
            pltpu.VMEM((4, SCAN_L, HEAD_W), F32),
            pltpu.VMEM((SCAN_L, SCAN_L), F32)]


def _mlstm_build(qp_ref, kp_ref, v_ref, mo_ref, ckp_ref, cv_ref, gt_ref, cgt_ref,
                 wq_ref, bq_ref, wk_ref, bk_ref, o_ref,
                 tab_ref, row_ref,
                 cf_ref, mf_ref, cb_ref, mb_ref, cfs_ref, mfs_ref, cbs_ref, mbs_ref, mask_ref,
                 xk_ref, xq_ref, q_ref, k_ref, ck_ref):
    L = SCAN_L
    T = qp_ref.shape[1]
    Tc = ckp_ref.shape[1]
    n_chunks = T // L
    n_ctx_chunks = Tc // L
    CV = CONV_ROWS

    def conv_stage(src_ref, xs_ref, t_len):
        xs_ref[pl.ds(0, 8), :] = jnp.zeros((8, HEAD_W), F32)
        xs_ref[pl.ds(8 + t_len, 8), :] = jnp.zeros((8, HEAD_W), F32)
        xs_ref[pl.ds(8, t_len), :] = src_ref[0].astype(F32)

    def conv_rows(xs_ref, c, w_ref, b_ref, dst_ref, scale):
        w = w_ref[...]
        r0 = pl.multiple_of(c * CV, CV)
        win = xs_ref[pl.ds(r0, CV + 16), :]
        prev = pltpu.roll(win, 1, 0)[8:8 + CV, :]
        cur = win[8:8 + CV, :]
        nxt = pltpu.roll(win, CV + 15, 0)[8:8 + CV, :]
        y = _silu(prev * w[0:1, :] + cur * w[1:2, :] + nxt * w[2:3, :] + b_ref[...])
        if scale != 1.0:
            y = y * scale
        dst_ref[pl.ds(r0, CV), :] = y.astype(BF16)

    per_chunk = L // CV

    def conv_k(c):
        for u in range(per_chunk):
            conv_rows(xk_ref, c * per_chunk + u, wk_ref, bk_ref, k_ref, KEY_SCALE)

    def conv_q(c):
        for u in range(per_chunk):
            conv_rows(xq_ref, c * per_chunk + u, wq_ref, bq_ref, q_ref, 1.0)

    conv_stage(ckp_ref, xk_ref, Tc)
    for c in range(Tc // CV):
        conv_rows(xk_ref, c, wk_ref, bk_ref, ck_ref, KEY_SCALE)
    conv_stage(kp_ref, xk_ref, T)
    conv_stage(qp_ref, xq_ref, T)
    conv_k(0)
    conv_k(n_chunks - 1)
    n_k_steps = n_chunks // 2 - 1

    def conv_step(i, keys):
        if keys:
            conv_k(i + 1)
            conv_k(n_chunks - 2 - i)
        else:
            conv_q(2 * (i - n_k_steps))
            conv_q(2 * (i - n_k_steps) + 1)

    ri = lax.broadcasted_iota(I32, (L, L), 0)
    ci = lax.broadcasted_iota(I32, (L, L), 1)
    tri_u = (ri <= ci).astype(BF16)
    lane8 = lax.broadcasted_iota(I32, (8, L), 1)
    sub8 = lax.broadcasted_iota(I32, (8, L), 0)
    sel_r = lax.broadcasted_iota(I32, (24, 8 * 128), 0) % 8
    sel_c = lax.broadcasted_iota(I32, (24, 8 * 128), 1) // 128
    sel3 = (sel_r == sel_c).astype(BF16)
    ones_cols = jnp.ones((L, AUG_W - HEAD_W), BF16)

    def chunk_tables(g8, n_used, state_only):
        i_f, i_b = g8[0], g8[2]
        lf_f, lf_b = _log_sigmoid(g8[1]), _log_sigmoid(g8[3])
        cs3 = jnp.dot(_split3(jnp.concatenate([lf_f, lf_b], axis=0)), tri_u,
                      preferred_element_type=F32)
        cs = cs3[0:16] + cs3[16:32] + cs3[32:48]
        b_f = cs[0:8]
        b_b = cs[8:16, L - 1:L] - cs[8:16] + lf_b
        z_f = i_f - b_f
        z_b = i_b - b_b
        g_f = b_f[:, L - 1:L] - b_f + i_f
        g_b = b_b[:, 0:1] - b_b + i_b
        mf, mb = z_f, z_b
        s = 1
        while s < L:
            mf = jnp.maximum(mf, jnp.where(lane8 >= s, pltpu.roll(mf, s, 1), NEG_INF))
            mb = jnp.maximum(mb, jnp.where(lane8 < L - s, pltpu.roll(mb, L - s, 1), NEG_INF))
            s *= 2
        mb = jnp.where(lane8 < L - 1, pltpu.roll(mb, L - 1, 1), NEG_INF)
        reps = [None if state_only and t not in (2, 5) else
                lax.dot_general(_split3(val), sel3[:, 0:n_used * 128], TN_DIMS, preferred_element_type=F32)
                for t, val in enumerate((mf, b_f, g_f, mb, b_b, g_b))]

        def rows_of(c):
            out = jnp.zeros((8, L), F32)
            for r, val in enumerate((z_f, z_b, g_f, g_b, b_f, b_b)):
                out = jnp.where(sub8 == r, val[c:c + 1], out)
            return out

        return rows_of, reps

    lat_rows, lat_reps = chunk_tables(gt_ref[0, 0], n_chunks, False)
    for c in range(n_chunks):
        row_ref[c] = lat_rows(c)
        for t in range(N_TAB):
            tab_ref[t, c * L:(c + 1) * L, :] = lat_reps[t][:, c * 128:(c + 1) * 128]

    def lanes2(x):
        return jnp.concatenate([x, x], axis=1)

    def advance(k, v, g_rep, g_row, b_last, c_ref, m_ref):
        m = m_ref[...]
        m_new = jnp.maximum(b_last + m, jnp.max(g_row, axis=-1, keepdims=True))
        kw = (k.astype(F32) * jnp.exp(lanes2(g_rep) - m_new)).astype(BF16)
        v_aug = jnp.concatenate([v, ones_cols], axis=1)
        c_ref[...] = jnp.exp(b_last + m - m_new) * c_ref[...] + lax.dot_general(
            kw, v_aug, TN_DIMS, preferred_element_type=F32)
        m_ref[...] = m_new

    for r in (cf_ref, mf_ref, cb_ref, mb_ref):
        r[...] = jnp.zeros_like(r)
    ctx_rows, ctx_reps = chunk_tables(cgt_ref[0, 0], n_ctx_chunks, True)
    for c in range(n_ctx_chunks):
        rows = ctx_rows(c)
        advance(ck_ref[c * L:(c + 1) * L, :], cv_ref[0, c * L:(c + 1) * L, :],
                ctx_reps[2][:, c * 128:(c + 1) * 128], rows[2:3], rows[4:5, L - 1:L], cf_ref, mf_ref)
    for c in reversed(range(n_ctx_chunks)):
        rows = ctx_rows(c)
        advance(ck_ref[c * L:(c + 1) * L, :], cv_ref[0, c * L:(c + 1) * L, :],
                ctx_reps[5][:, c * 128:(c + 1) * 128], rows[3:4], rows[5:6, 0:1], cb_ref, mb_ref)

    def state_pass(i, carry, keys):
        cb = n_chunks - 1 - i
        rf = pl.multiple_of(i * L, L)
        rb = pl.multiple_of(cb * L, L)
        cfs_ref[i] = cf_ref[...].astype(BF16)
        mfs_ref[i] = mf_ref[...]
        cbs_ref[cb] = cb_ref[...].astype(BF16)
        mbs_ref[cb] = mb_ref[...]
        rows_f = row_ref[i]
        rows_b = row_ref[cb]
        advance(k_ref[pl.ds(rf, L), :], v_ref[0, pl.ds(rf, L), :], tab_ref[2, pl.ds(rf, L), :],
                rows_f[2:3], rows_f[4:5, L - 1:L], cf_ref, mf_ref)
        advance(k_ref[pl.ds(rb, L), :], v_ref[0, pl.ds(rb, L), :], tab_ref[5, pl.ds(rb, L), :],
                rows_b[3:4], rows_b[5:6, 0:1], cb_ref, mb_ref)
        conv_step(i, keys)
        return carry

    def finish_states():
        cfs_ref[n_chunks - 1] = cf_ref[...].astype(BF16)
        mfs_ref[n_chunks - 1] = mf_ref[...]
        cbs_ref[0] = cb_ref[...].astype(BF16)
        mbs_ref[0] = mb_ref[...]

    def direction(q, v_aug, s, z_row, zmax_rep, b_rep, mask, c_in, m_in):
        mx = jnp.maximum(zmax_rep, m_in)
        att = s * jnp.exp((z_row - lanes2(mx)) + mask)
        na = jnp.dot(att.astype(BF16), v_aug, preferred_element_type=F32)
        qa = jnp.dot(q, c_in, preferred_element_type=F32)
        a = jnp.exp(m_in - mx)
        num = na[:, 0:HEAD_W] + lanes2(a) * qa[:, 0:HEAD_W]
        den = na[:, HEAD_W:] + a * qa[:, HEAD_W:]
        scale = 1.0 / jnp.maximum(jnp.abs(den), jnp.exp(-(b_rep + mx)))
        return num * lanes2(scale)

    @pl.when(pl.program_id(1) == 0)
    def _():
        mask_ref[0] = jnp.where(ci <= ri, 0.0, NEG_INF)
        mask_ref[1] = jnp.where(ci > ri, 0.0, NEG_INF)

    def out_chunk(c):
        r0 = pl.multiple_of(c * L, L)
        q = q_ref[pl.ds(r0, L), :]
        k = k_ref[pl.ds(r0, L), :]
        v_aug = jnp.concatenate([v_ref[0, pl.ds(r0, L), :], ones_cols], axis=1)
        s = lax.dot_general(q, k, NT_DIMS, preferred_element_type=F32)
        rows = row_ref[c]
        tot = direction(q, v_aug, s, rows[0:1], tab_ref[0, pl.ds(r0, L), :], tab_ref[1, pl.ds(r0, L), :],
                        mask_ref[0], cfs_ref[c], mfs_ref[c])
        tot = tot + direction(q, v_aug, s, rows[1:2], tab_ref[3, pl.ds(r0, L), :],
                              tab_ref[4, pl.ds(r0, L), :], mask_ref[1], cbs_ref[c], mbs_ref[c])
        mo = mo_ref[0, pl.ds(r0, L), :].astype(F32)
        o_ref[0, pl.ds(r0, L), :] = (_layer_norm(tot) * jax.nn.sigmoid(mo)).astype(BF16)

    return state_pass, finish_states, out_chunk, n_k_steps


def _mlstm_scratch(T, Tc, n_chunks):
    state = [pltpu.VMEM((HEAD_W, AUG_W), F32), pltpu.VMEM((1, 1), F32)]
    snaps = [pltpu.VMEM((n_chunks, HEAD_W, AUG_W), BF16), pltpu.VMEM((n_chunks, 1, 1), F32)]
    return [pltpu.VMEM((N_TAB, T, 128), F32), pltpu.VMEM((n_chunks, 8, SCAN_L), F32)] \
        + state + state + snaps + snaps + [pltpu.VMEM((2, SCAN_L, SCAN_L), F32)] \
        + [pltpu.VMEM((T + 16, HEAD_W), F32), pltpu.VMEM((T + 16, HEAD_W), F32),
           pltpu.VMEM((T, HEAD_W), BF16), pltpu.VMEM((T, HEAD_W), BF16), pltpu.VMEM((Tc, HEAD_W), BF16)]


def _scan_kernel(n_ret_scratch, dl_ref, rq_ref, rk_ref, rv_ref, rg_ref, rck_ref, rcv_ref,
                 mq_ref, mk_ref, mv_ref, mo_ref, mck_ref, mcv_ref, gt_ref, cgt_ref,
                 wq_ref, bq_ref, wk_ref, bk_ref, r_ref, m_ref, *scratch):
    n_chunks = rq_ref.shape[1] // SCAN_L
    ret = _ret_build(dl_ref, rq_ref, rk_ref, rv_ref, rg_ref, rck_ref, rcv_ref, r_ref,
                     *scratch[:n_ret_scratch])
    mls = _mlstm_build(mq_ref, mk_ref, mv_ref, mo_ref, mck_ref, mcv_ref, gt_ref, cgt_ref,
                       wq_ref, bq_ref, wk_ref, bk_ref, m_ref, *scratch[n_ret_scratch:])

    def state_pass(i, carry, keys):
        ret[0](i, carry)
        mls[0](i, carry, keys)
        return carry

    n_k_steps = mls[3]
    lax.fori_loop(0, n_k_steps, functools.partial(state_pass, keys=True), 0)
    lax.fori_loop(n_k_steps, n_chunks - 1, functools.partial(state_pass, keys=False), 0)
    ret[1]()
    mls[1]()

    def out_pass(i, carry):
        for c in (2 * i, 2 * i + 1):
            ret[2](c)
            mls[2](c)
        return carry

    lax.fori_loop(0, n_chunks // 2, out_pass, 0)


def _scans(decay_logit, p_lat, p_ctx, gt, cgt, conv_w, conv_b, ret_lat, ret_ctx, ml_lat, ml_ctx):
    B, T, _ = p_lat.shape
    Tc = p_ctx.shape[1]
    assert T % (2 * SCAN_L) == 0 and Tc % SCAN_L == 0 and T // SCAN_L <= 8
    n_chunks = T // SCAN_L

    def lat(sec):
        return pl.BlockSpec((1, T, HEAD_W), lambda h, b: (b, 0, sec * HEADS + h))

    def cx(sec):
        return pl.BlockSpec((1, Tc, HEAD_W), lambda h, b: (b, 0, sec * HEADS + h))

    gates = pl.BlockSpec((1, 1, N_GK, 8, SCAN_L), lambda h, b: (b, h, 0, 0, 0))
    out = pl.BlockSpec((1, T, HEAD_W), lambda h, b: (b, 0, h))
    conv_specs = [pl.BlockSpec((3, HEAD_W), lambda h, b: (0, h)),
                  pl.BlockSpec((1, HEAD_W), lambda h, b: (0, h)),
                  pl.BlockSpec((3, HEAD_W), lambda h, b: (0, HEADS + h)),
                  pl.BlockSpec((1, HEAD_W), lambda h, b: (0, HEADS + h))]
    ret_scratch = _ret_scratch(n_chunks)
    return pl.pallas_call(
        functools.partial(_scan_kernel, len(ret_scratch)),
        out_shape=(jax.ShapeDtypeStruct((B, T, BRANCH_W), BF16),
                   jax.ShapeDtypeStruct((B, T, BRANCH_W), BF16)),
        grid=(HEADS, B),
        in_specs=[pl.BlockSpec(memory_space=pltpu.SMEM)]
        + [lat(s) for s in ret_lat] + [cx(s) for s in ret_ctx]
        + [lat(s) for s in ml_lat] + [cx(s) for s in ml_ctx] + [gates, gates] + conv_specs,
        out_specs=(out, out),
        scratch_shapes=ret_scratch + _mlstm_scratch(T, Tc, n_chunks),
        compiler_params=_cparams("arbitrary", "arbitrary"),
        name="scans",
    )(decay_logit, *([p_lat] * 4), *([p_ctx] * 2), *([p_lat] * 4), *([p_ctx] * 2), gt, cgt,
      conv_w, conv_b, conv_w, conv_b)


def _merge_kernel(alpha, r_ref, m_ref, gr_ref, gm_ref, x_ref, g1_ref, sh2_ref, sc2_ref,
                  lng_ref, lnb_ref, wr_ref, wm_ref, wo_ref, wrt_ref, brt_ref, e1_ref, e3_ref, e2_ref,
                  x1_ref, ua_ref, ub_ref, ri_ref, rw_ref, cnt_ref, e1b_ref, e3b_ref, e2b_ref,
                  carry_ref, u_ref):
    for src, dst in ((e1_ref, e1b_ref), (e3_ref, e3b_ref), (e2_ref, e2b_ref)):
        dst[...] = src[...].astype(BF16)

    @pl.when(jnp.logical_and(pl.program_id(0) == 0, pl.program_id(1) == 0))
    def _():
        carry_ref[...] = jnp.zeros_like(carry_ref)
        tm = x_ref.shape[1]
        r = lax.broadcasted_iota(I32, (tm, tm), 0)
        c = lax.broadcasted_iota(I32, (tm, tm), 1)
        u_ref[...] = (r < c).astype(BF16)

    yr = jnp.dot(r_ref[0], wr_ref[...], preferred_element_type=F32)
    ym = jnp.dot(m_ref[0], wm_ref[...], preferred_element_type=F32)
    y = jax.nn.sigmoid(gr_ref[0].astype(F32)) * yr + jax.nn.sigmoid(gm_ref[0].astype(F32)) * ym
    yo = jnp.dot(y.astype(BF16), wo_ref[...], preferred_element_type=F32)
    x1 = _layer_norm(alpha * x_ref[0] + g1_ref[0] * yo) * lng_ref[...] + lnb_ref[...]
    x1_ref[0] = x1
    u2 = _layer_norm(x1) * (1.0 + sc2_ref[0]) + sh2_ref[0]
    ua_ref[0] = _pack_pairs(u2[:, 0:PACK_W], u2[:, PACK_W:2 * PACK_W])
    ub_ref[0] = _pack_pairs(u2[:, 2 * PACK_W:3 * PACK_W], u2[:, 3 * PACK_W:4 * PACK_W])
    lt = lax.dot_general(wrt_ref[...], u2.astype(BF16), NT_DIMS, preferred_element_type=F32) + brt_ref[...]
    _route_tile(lt, ri_ref, rw_ref, cnt_ref, carry_ref, u_ref)


def _route_tile(lt, ri_ref, rw_ref, cnt_ref, carry_ref, u_ref):
    tm = lt.shape[1]
    lg = lt[0:N_GROUPS, :]
    eg = jnp.exp(lg - jnp.max(lg, axis=0, keepdims=True))
    pg = eg / jnp.sum(eg, axis=0, keepdims=True)
    pg_top = jnp.max(pg, axis=0, keepdims=True)
    rows_g = lax.broadcasted_iota(I32, pg.shape, 0)
    g_idx = jnp.min(jnp.where(pg == pg_top, rows_g, N_GROUPS), axis=0, keepdims=True)

    le = jnp.zeros((EXP_PER_GROUP, tm), F32)
    for g in range(N_GROUPS):
        lo = 8 + g * EXP_PER_GROUP
        le = jnp.where(g_idx == g, lt[lo:lo + EXP_PER_GROUP, :], le)
    ee = jnp.exp(le - jnp.max(le, axis=0, keepdims=True))
    pe = ee / jnp.sum(ee, axis=0, keepdims=True)
    rows_e = lax.broadcasted_iota(I32, pe.shape, 0)
    v1 = jnp.max(pe, axis=0, keepdims=True)
    i1 = jnp.min(jnp.where(pe == v1, rows_e, EXP_PER_GROUP), axis=0, keepdims=True)
    pe2 = jnp.where(rows_e == i1, -1.0, pe)
    v2 = jnp.max(pe2, axis=0, keepdims=True)
    i2 = jnp.min(jnp.where(pe2 == v2, rows_e, EXP_PER_GROUP), axis=0, keepdims=True)
    den = v1 + v2
    rw_ref[...] = jnp.zeros_like(rw_ref)
    rw_ref[0:1, :] = pg_top * v1 / den
    rw_ref[1:2, :] = pg_top * v2 / den
    e1 = g_idx * EXP_PER_GROUP + i1
    e2 = g_idx * EXP_PER_GROUP + i2

    rows_x = lax.broadcasted_iota(I32, (N_EXPERTS, tm), 0)
    oh1 = (rows_x == e1).astype(F32)
    oh2 = (rows_x == e2).astype(F32)
    both = oh1 + oh2
    before = carry_ref[:, 0:1] + jnp.dot(both.astype(BF16), u_ref[...], preferred_element_type=F32)
    ri_ref[0:1, :] = e1
    ri_ref[1:2, :] = e2
    ri_ref[2:3, :] = jnp.sum(oh1 * before, axis=0, keepdims=True).astype(I32)
    ri_ref[3:4, :] = jnp.sum(oh2 * before, axis=0, keepdims=True).astype(I32)
    carry_ref[...] = carry_ref[...] + jnp.sum(both, axis=1, keepdims=True)
    cnt_ref[...] = carry_ref[...].astype(I32)


def _merge(alpha, r, m, p_lat, sec_gates, x, g1, sh2, sc2, lng, lnb, wr, wm, wo, wrt, brt, expert_w):
    B, T, D = x.shape
    tm = MERGE_TM
    per_b = T // tm
    n = B * T
    n_steps = B * per_b
    sliced = [w.reshape(n_steps, w.shape[0] * w.shape[1] // n_steps, w.shape[2]) for w in expert_w]
    assert all(s.shape[1] % 16 == 0 and s.size == w.size for s, w in zip(sliced, expert_w))

    def slab(s):
        return pl.BlockSpec((1,) + s.shape[1:], lambda b, i: (b * per_b + i, 0, 0))

    def tile(w):
        return pl.BlockSpec((1, tm, w), lambda b, i: (b, i, 0))

    def sec(s):
        return pl.BlockSpec((1, tm, BRANCH_W), lambda b, i: (b, i, s))

    def mod():
        return pl.BlockSpec((1, 1, D), lambda b, i: (b, 0, 0))

    def const(shape):
        return pl.BlockSpec(shape, lambda b, i: (0,) * len(shape))

    outs = pl.pallas_call(
        functools.partial(_merge_kernel, alpha),
        out_shape=(jax.ShapeDtypeStruct((B, T, D), F32),
                   jax.ShapeDtypeStruct((B, T, PACK_W), U32),
                   jax.ShapeDtypeStruct((B, T, PACK_W), U32),
                   jax.ShapeDtypeStruct((4, n), I32),
                   jax.ShapeDtypeStruct((8, n), F32),
                   jax.ShapeDtypeStruct((N_EXPERTS, 128), I32))
        + tuple(jax.ShapeDtypeStruct(s.shape, BF16) for s in sliced),
        grid=(B, per_b),
        in_specs=[tile(BRANCH_W), tile(BRANCH_W), sec(sec_gates[0]), sec(sec_gates[1]), tile(D),
                  mod(), mod(), mod(), const((1, D)), const((1, D)),
                  const((BRANCH_W, D)), const((BRANCH_W, D)), const((D, D)),
                  const((ROUTE_ROWS, D)), const((ROUTE_ROWS, 1))] + [slab(s) for s in sliced],
        out_specs=(tile(D), tile(PACK_W), tile(PACK_W),
                   pl.BlockSpec((4, tm), lambda b, i: (0, b * per_b + i)),
                   pl.BlockSpec((8, tm), lambda b, i: (0, b * per_b + i)),
                   pl.BlockSpec((N_EXPERTS, 128), lambda b, i: (0, 0)))
        + tuple(slab(s) for s in sliced),
        scratch_shapes=[pltpu.VMEM((N_EXPERTS, 128), F32), pltpu.VMEM((tm, tm), BF16)],
        compiler_params=_cparams("arbitrary", "arbitrary"),
        name="merge",
    )(r, m, p_lat, p_lat, x, g1, sh2, sc2, lng, lnb, wr, wm, wo, wrt, brt, *sliced)
    return outs[:6] + tuple(o.reshape(w.shape) for o, w in zip(outs[6:], expert_w))


def _sc_mesh():
    return plsc.VectorSubcoreMesh(core_axis_name="c", subcore_axis_name="s")


def _sc_scatter2(rows_a, rows_b, idx0, idx1, n_out):
    m, w = rows_a.shape
    out = jax.ShapeDtypeStruct((n_out, w), rows_a.dtype)

    @functools.partial(pl.kernel, out_type=(out, out), mesh=_sc_mesh(), scratch_types=[])
    def k(xa_hbm, xb_hbm, i0_hbm, i1_hbm, oa_hbm, ob_hbm):
        for x_hbm, o_hbm in ((xa_hbm, oa_hbm), (xb_hbm, ob_hbm)):
            def body(x_vmem, i0_vmem, i1_vmem, o_hbm=o_hbm):
                pltpu.sync_copy(x_vmem, o_hbm.at[i0_vmem.at[0]])
                pltpu.sync_copy(x_vmem, o_hbm.at[i1_vmem.at[0]])

            pltpu.emit_pipeline(
                body,
                grid=(m // SC_WIN,),
                in_specs=[pl.BlockSpec((SC_WIN, w), lambda i: (i, 0)),
                          pl.BlockSpec((1, SC_WIN), lambda i: (0, i)),
                          pl.BlockSpec((1, SC_WIN), lambda i: (0, i))],
                out_specs=[],
                core_axis_name=("c", "s"),
                dimension_semantics=(pltpu.PARALLEL,),
            )(x_hbm, i0_hbm, i1_hbm)

    return k(rows_a, rows_b, idx0.reshape(1, m), idx1.reshape(1, m))


def _sc_gather(table_a, table_b, idx):
    m = idx.shape[0]
    w = table_a.shape[1]
    out = jax.ShapeDtypeStruct((m, w), table_a.dtype)

    @functools.partial(pl.kernel, out_type=(out, out), mesh=_sc_mesh(), scratch_types=[])
    def k(ta_hbm, tb_hbm, i_hbm, oa_hbm, ob_hbm):
        for t_hbm, o_hbm in ((ta_hbm, oa_hbm), (tb_hbm, ob_hbm)):
            def body(i_vmem, o_vmem, t_hbm=t_hbm):
                pltpu.sync_copy(t_hbm.at[i_vmem.at[0]], o_vmem)

            pltpu.emit_pipeline(
                body,
                grid=(m // SC_WIN,),
                in_specs=[pl.BlockSpec((1, SC_WIN), lambda i: (0, i))],
                out_specs=[pl.BlockSpec((SC_WIN, w), lambda i: (i, 0))],
                core_axis_name=("c", "s"),
                dimension_semantics=(pltpu.PARALLEL,),
            )(i_hbm, o_hbm)

    return k(table_a, table_b, idx.reshape(1, m))


def _expert_kernel(be_ref, nv_ref, xa_ref, xb_ref, w1_ref, w3_ref, w2_ref, ya_ref, yb_ref):
    j = pl.program_id(0)
    nv = nv_ref[j]

    @pl.when(nv > 0)
    def _():
        valid = lax.broadcasted_iota(I32, xa_ref.shape, 0) < nv
        zero = jnp.zeros(xa_ref.shape, U32)
        parts = _unpack_pairs(jnp.where(valid, xa_ref[...], zero)) + \
            _unpack_pairs(jnp.where(valid, xb_ref[...], zero))
        x = jnp.concatenate([p.astype(BF16) for p in parts], axis=1)
        h1 = jnp.dot(x, w1_ref[0], preferred_element_type=F32)
        h3 = jnp.dot(x, w3_ref[0], preferred_element_type=F32)
        y = jnp.dot((_silu(h1) * h3).astype(BF16), w2_ref[0], preferred_element_type=F32)
        ya_ref[...] = _pack_pairs(y[:, 0:PACK_W], y[:, PACK_W:2 * PACK_W])
        yb_ref[...] = _pack_pairs(y[:, 2 * PACK_W:3 * PACK_W], y[:, 3 * PACK_W:4 * PACK_W])

    @pl.when(nv == 0)
    def _():
        ya_ref[...] = jnp.zeros_like(ya_ref)
        yb_ref[...] = jnp.zeros_like(yb_ref)


def _experts(block_exp, n_valid, xa, xb, w1, w3, w2):
    n_slots = xa.shape[0]
    n_blocks = n_slots // MOE_BLK
    d, de = w1.shape[1], w1.shape[2]
    slot = pl.BlockSpec((MOE_BLK, PACK_W), lambda j, be, nv: (j, 0))
    grid_spec = pltpu.PrefetchScalarGridSpec(
        num_scalar_prefetch=2,
        grid=(n_blocks,),
        in_specs=[slot, slot,
                  pl.BlockSpec((1, d, de), lambda j, be, nv: (be[j], 0, 0)),
                  pl.BlockSpec((1, d, de), lambda j, be, nv: (be[j], 0, 0)),
                  pl.BlockSpec((1, de, d), lambda j, be, nv: (be[j], 0, 0))],
        out_specs=(slot, slot),
    )
    return pl.pallas_call(
        _expert_kernel,
        out_shape=(jax.ShapeDtypeStruct((n_slots, PACK_W), U32),
                   jax.ShapeDtypeStruct((n_slots, PACK_W), U32)),
        grid_spec=grid_spec,
        compiler_params=_cparams("parallel"),
        name="experts",
    )(block_exp, n_valid, xa, xb, w1, w3, w2)


def _final_kernel(alpha, x1_ref, a0_ref, b0_ref, a1_ref, b1_ref, w_ref, g2_ref, lng_ref, lnb_ref, o_ref):
    w = w_ref[...].T
    w0 = w[:, 0:1]
    w1 = w[:, 1:2]
    parts0 = _unpack_pairs(a0_ref[...]) + _unpack_pairs(b0_ref[...])
    parts1 = _unpack_pairs(a1_ref[...]) + _unpack_pairs(b1_ref[...])
    f = jnp.concatenate([w0 * p0 + w1 * p1 for p0, p1 in zip(parts0, parts1)], axis=1)
    o_ref[0] = _layer_norm(alpha * x1_ref[0] + g2_ref[0] * f) * lng_ref[...] + lnb_ref[...]


def _final(alpha, x1, ya, yb, w, g2, lng, lnb):
    B, T, D = x1.shape
    tm = min(FINAL_TM, T)
    per_b = T // tm
    n_tiles = B * per_b

    def rows(k):
        return pl.BlockSpec((tm, PACK_W), lambda b, i: (k * n_tiles + b * per_b + i, 0))

    return pl.pallas_call(
        functools.partial(_final_kernel, alpha),
        out_shape=jax.ShapeDtypeStruct((B, T, D), F32),
        grid=(B, per_b),
        in_specs=[pl.BlockSpec((1, tm, D), lambda b, i: (b, i, 0)),
                  rows(0), rows(0), rows(1), rows(1),
                  pl.BlockSpec((8, tm), lambda b, i: (0, b * per_b + i)),
                  pl.BlockSpec((1, 1, D), lambda b, i: (b, 0, 0)),
                  pl.BlockSpec((1, D), lambda b, i: (0, 0)),
                  pl.BlockSpec((1, D), lambda b, i: (0, 0))],
        out_specs=pl.BlockSpec((1, tm, D), lambda b, i: (b, i, 0)),
        compiler_params=_cparams("parallel", "parallel"),
        name="final",
    )(x1, ya, yb, ya, yb, w, g2, lng, lnb)


def _rotary_tables(T):
    quarter = HEAD_W // 4
    freqs = ROPE_BASE ** (-jnp.arange(quarter, dtype=F32) / quarter)
    t = jnp.arange(T)
    ang_r = (t // GRID_W).astype(F32)[:, None] * freqs[None, :]
    ang_c = (t % GRID_W).astype(F32)[:, None] * freqs[None, :]
    cos = jnp.concatenate([jnp.cos(ang_r)] * 2 + [jnp.cos(ang_c)] * 2, axis=1)
    sin = jnp.concatenate([-jnp.sin(ang_r), jnp.sin(ang_r), -jnp.sin(ang_c), jnp.sin(ang_c)], axis=1)
    return cos, sin


def _per_head_gates(gt):
    B, _, T = gt.shape
    n_chunks = T // SCAN_L
    gth = gt.reshape(B, N_GK, HEADS, n_chunks, SCAN_L).transpose(0, 2, 1, 3, 4)
    return jnp.pad(gth, ((0, 0), (0, 0), (0, 0), (0, 8 - n_chunks), (0, 0)))


def _table_lookup(table, idx):
    sel = idx[..., None] == jnp.arange(table.shape[0], dtype=idx.dtype)
    return jnp.sum(jnp.where(sel, table, 0), axis=-1)


def kernel(x, c, ctx, c_ctx, w_ada, b_ada, w_in, b_mgate, ml_conv_w, ml_conv_b, ret_decay_logit, w_ret_branch, w_ml_branch, w_out, ln1_g, ln1_b, w_rg, b_rg, w_re, b_re, w_e1, w_e3, w_e2, ln2_g, ln2_b):
    B, T, D = x.shape
    depth = w_ada.shape[0]
    assert depth == 1 and D == BRANCH_W and T % GRID_W == 0
    alpha = (2 * depth) ** 0.25
    n_tok = B * T

    n_rows = -(-(B + 1) // 8) * 8
    cs = jnp.zeros((n_rows, D), F32).at[:B].set(c).at[B].set(c_ctx)
    mod = _ada(cs, w_ada[0], b_ada[0][None, :])
    sh1, sc1, g1, sh2, sc2, g2 = [mod[:B, None, i * D:(i + 1) * D] for i in range(6)]
    csh1 = mod[B, 0 * D:1 * D].reshape(1, 1, D)
    csc1 = mod[B, 1 * D:2 * D].reshape(1, 1, D)

    w = w_in[0]
    sec_w = [w[:, s * BRANCH_W:(s + 1) * BRANCH_W] for s in range(8)]
    g_lo = 8 * BRANCH_W
    w_gate_t = w[:, g_lo:g_lo + N_GATES].T.astype(BF16)
    b_gate = b_mgate[0][:, None]
    sec_w += [w[:, g_lo + N_GATES:g_lo + N_GATES + D], w[:, g_lo + N_GATES + D:]]
    w_lat = jnp.concatenate(sec_w, axis=1).astype(BF16)
    w_ctx = jnp.concatenate([sec_w[1], sec_w[2], sec_w[5], sec_w[6]], axis=1).astype(BF16)
    kinds_lat = ("rot", "rot_scale") + ("plain",) * 8
    kinds_ctx = ("scale", "plain", "plain", "plain")
    p_lat, gt_lat = _proj(x, sh1, sc1, w_lat, w_gate_t, b_gate, kinds_lat, _rotary_tables(T))
    Tc = ctx.shape[1]
    p_ctx, gt_ctx = _proj(ctx.reshape(1, B * Tc, D), csh1, csc1, w_ctx, w_gate_t, b_gate, kinds_ctx)
    p_ctx = p_ctx.reshape(B, Tc, -1)
    gt_ctx = gt_ctx.reshape(N_GATES, B, Tc).transpose(1, 0, 2)

    ret, mls = _scans(ret_decay_logit[0], p_lat, p_ctx, _per_head_gates(gt_lat), _per_head_gates(gt_ctx),
                      ml_conv_w[0], ml_conv_b[0][None, :], (0, 1, 2, 3), (0, 1), (4, 5, 6, 7), (2, 3))

    wrt = jnp.zeros((ROUTE_ROWS, D), F32).at[:N_GROUPS].set(w_rg[0].T).at[8:8 + N_EXPERTS].set(w_re[0].T)
    brt = jnp.zeros((ROUTE_ROWS, 1), F32).at[:N_GROUPS, 0].set(b_rg[0]).at[8:8 + N_EXPERTS, 0].set(b_re[0])
    x1, ua, ub, ri, rw, cnt, we1, we3, we2 = _merge(
        alpha, ret, mls, p_lat, (8, 9), x, g1, sh2, sc2, ln1_g[0][None, :], ln1_b[0][None, :],
        w_ret_branch[0].astype(BF16), w_ml_branch[0].astype(BF16), w_out[0].astype(BF16),
        wrt.astype(BF16), brt, (w_e1[0], w_e3[0], w_e2[0]))

    counts = cnt[:, 0]
    padded = (counts + MOE_BLK - 1) // MOE_BLK * MOE_BLK
    pad_end = jnp.cumsum(padded)
    pad_off = pad_end - padded
    dest = _table_lookup(pad_off, ri[0:2]) + ri[2:4]
    n_blocks = (2 * n_tok) // MOE_BLK + N_EXPERTS
    n_slots = n_blocks * MOE_BLK
    block_start = jnp.arange(n_blocks, dtype=I32) * MOE_BLK
    block_exp = jnp.minimum((block_start[:, None] >= pad_end[None, :]).sum(1), N_EXPERTS - 1).astype(I32)
    n_valid = jnp.clip(_table_lookup(counts, block_exp) - (block_start - _table_lookup(pad_off, block_exp)),
                       0, MOE_BLK).astype(I32)

    xa, xb = _sc_scatter2(ua.reshape(n_tok, PACK_W), ub.reshape(n_tok, PACK_W), dest[0], dest[1], n_slots)
    ya, yb = _experts(block_exp, n_valid, xa, xb, we1, we3, we2)
    ga, gb = _sc_gather(ya, yb, dest.reshape(2 * n_tok))
    return _final(alpha, x1, ga, gb, rw, g2, ln2_g[0][None, :], ln2_b[0][None, :])
```

```python
import functools

import jax
import jax.numpy as jnp
from jax import lax
from jax.experimental import pallas as pl
from jax.experimental.pallas import tpu as pltpu
from jax.experimental.pallas import tpu_sc as plsc

F32 = jnp.float32
BF16 = jnp.bfloat16
U32 = jnp.uint32
I32 = jnp.int32
HIGHEST = lax.Precision.HIGHEST

HEADS = 4
HEAD_W = 256
BRANCH_W = HEADS * HEAD_W
GRID_W = 64
ROPE_BASE = 10000.0
N_GATES = 16
N_GK = N_GATES // HEADS
N_GROUPS = 4
EXP_PER_GROUP = 8
N_EXPERTS = N_GROUPS * EXP_PER_GROUP
LN_EPS = 1e-5
NEG_INF = -1e30
KEY_SCALE = HEAD_W ** -0.5

SCAN_L = 256
CONV_ROWS = 128
PROJ_TM = 2048
PROJ_SUB = 256
MERGE_TM = 512
FINAL_TM = 1024
MOE_BLK = 512
SC_WIN = 128
PACK_W = 256
ROUTE_ROWS = 64
N_TAB = 6
AUG_W = HEAD_W + 128
VMEM_LIMIT = 48 * 1024 * 1024

NT_DIMS = (((1,), (1,)), ((), ()))
TN_DIMS = (((0,), (0,)), ((), ()))


def _cparams(*sem):
    return pltpu.CompilerParams(dimension_semantics=sem, vmem_limit_bytes=VMEM_LIMIT)


def _layer_norm(x):
    mu = jnp.mean(x, axis=-1, keepdims=True)
    xc = x - mu
    var = jnp.mean(xc * xc, axis=-1, keepdims=True)
    return xc * lax.rsqrt(var + LN_EPS)


def _log_sigmoid(x):
    return jnp.minimum(x, 0.0) - jnp.log1p(jnp.exp(-jnp.abs(x)))


def _silu(x):
    return x * jax.nn.sigmoid(x)


def _pack_pairs(hi, lo):
    hb = lax.bitcast_convert_type(hi.astype(BF16).astype(F32), U32)
    lb = lax.bitcast_convert_type(lo.astype(BF16).astype(F32), U32)
    return (hb & jnp.uint32(0xFFFF0000)) | (lb >> 16)


def _unpack_pairs(p):
    hi = lax.bitcast_convert_type(p & jnp.uint32(0xFFFF0000), F32)
    lo = lax.bitcast_convert_type(p << 16, F32)
    return hi, lo


def _split3(x):
    hi = x.astype(BF16).astype(F32)
    r1 = x - hi
    mid = r1.astype(BF16).astype(F32)
    lo = (r1 - mid).astype(BF16).astype(F32)
    return jnp.concatenate([hi, mid, lo], axis=0).astype(BF16)


def _ada_kernel(c_ref, w_ref, b_ref, o_ref):
    s = _silu(c_ref[...])
    o_ref[...] = jnp.dot(s, w_ref[...], precision=HIGHEST, preferred_element_type=F32) + b_ref[...]


def _ada(cs, w, b):
    rows, d = cs.shape
    cols = w.shape[1]
    tn = 1024
    return pl.pallas_call(
        _ada_kernel,
        out_shape=jax.ShapeDtypeStruct((rows, cols), F32),
        grid=(cols // tn,),
        in_specs=[pl.BlockSpec((rows, d), lambda j: (0, 0)),
                  pl.BlockSpec((d, tn), lambda j: (0, j)),
                  pl.BlockSpec((1, tn), lambda j: (0, j))],
        out_specs=pl.BlockSpec((rows, tn), lambda j: (0, j)),
        compiler_params=_cparams("parallel"),
        name="ada",
    )(cs, w, b)


def _repack_kernel(n_head, wh_ref, wt_ref, o_ref):
    j = pl.program_id(0)

    @pl.when(j < n_head)
    def _():
        o_ref[...] = wh_ref[0].astype(BF16)

    @pl.when(j >= n_head)
    def _():
        o_ref[...] = wt_ref[...].astype(BF16)


def _repack(w_in, w_tail, n_head):
    _, d, _ = w_in.shape
    tn = BRANCH_W
    n_tail = w_tail.shape[1] // tn
    return pl.pallas_call(
        functools.partial(_repack_kernel, n_head),
        out_shape=jax.ShapeDtypeStruct((d, (n_head + n_tail) * tn), BF16),
        grid=(n_head + n_tail,),
        in_specs=[pl.BlockSpec((1, d, tn), lambda j: (0, 0, jnp.minimum(j, n_head - 1))),
                  pl.BlockSpec((d, tn), lambda j: (0, jnp.maximum(j - n_head, 0)))],
        out_specs=pl.BlockSpec((d, tn), lambda j: (0, j)),
        compiler_params=_cparams("parallel"),
        name="repack",
    )(w_in, w_tail)


def _proj_kernel(kinds, x_ref, sh_ref, sc_ref, w_ref, wg_ref, bg_ref, *rest):
    if "rot" in kinds or "rot_scale" in kinds:
        cos_ref, sin_ref, o_ref, gt_ref, u_ref = rest
    else:
        o_ref, gt_ref, u_ref = rest
    j = pl.program_id(2)
    tm = x_ref.shape[1]
    sub = min(PROJ_SUB, tm)

    def rotary(acc, rows, scale):
        for s in range(acc.shape[1] // 128):
            a = acc[:, s * 128:(s + 1) * 128]
            half = s % 2
            cs = cos_ref[rows, half * 128:(half + 1) * 128]
            sn = sin_ref[rows, half * 128:(half + 1) * 128]
            r = a * cs + pltpu.roll(a, 64, 1) * sn
            if scale != 1.0:
                r = r * scale
            o_ref[0, rows, s * 128:(s + 1) * 128] = r.astype(BF16)

    def section(kind, first):
        for r in range(tm // sub):
            rows = slice(r * sub, (r + 1) * sub)
            if first:
                u = _layer_norm(x_ref[0, rows, :]) * (1.0 + sc_ref[0]) + sh_ref[0]
                ub = u.astype(BF16)
                u_ref[rows, :] = ub
                gt_ref[0, :, rows] = lax.dot_general(wg_ref[...], ub, NT_DIMS,
                                                     preferred_element_type=F32) + bg_ref[...]
            else:
                ub = u_ref[rows, :]
            acc = jnp.dot(ub, w_ref[...], preferred_element_type=F32)
            if kind == "rot":
                rotary(acc, rows, 1.0)
            elif kind == "rot_scale":
                rotary(acc, rows, KEY_SCALE)
            elif kind == "scale":
                o_ref[0, rows, :] = (acc * KEY_SCALE).astype(BF16)
            else:
                o_ref[0, rows, :] = acc.astype(BF16)

    variants = {}
    for s, kind in enumerate(kinds):
        variants.setdefault((kind, s == 0), []).append(s)
    for (kind, first), secs in variants.items():
        cond = functools.reduce(jnp.logical_or, [j == s for s in secs])

        @pl.when(cond)
        def _(kind=kind, first=first):
            section(kind, first)


def _proj(x, sh, sc, w_main, w_gate_t, b_gate, kinds, tables=None):
    B, T, D = x.shape
    n_sec = len(kinds)
    tm = min(PROJ_TM, T)
    tn = BRANCH_W
    assert T % tm == 0
    in_specs = [
        pl.BlockSpec((1, tm, D), lambda i, b, j: (b, i, 0)),
        pl.BlockSpec((1, 1, D), lambda i, b, j: (b, 0, 0)),
        pl.BlockSpec((1, 1, D), lambda i, b, j: (b, 0, 0)),
        pl.BlockSpec((D, tn), lambda i, b, j: (0, j)),
        pl.BlockSpec((N_GATES, D), lambda i, b, j: (0, 0)),
        pl.BlockSpec((N_GATES, 1), lambda i, b, j: (0, 0)),
    ]
    args = [x, sh, sc, w_main, w_gate_t, b_gate]
    if tables is not None:
        in_specs += [pl.BlockSpec((tm, HEAD_W), lambda i, b, j: (i, 0))] * 2
        args += list(tables)
    return pl.pallas_call(
        functools.partial(_proj_kernel, kinds),
        out_shape=(jax.ShapeDtypeStruct((B, T, n_sec * tn), BF16),
                   jax.ShapeDtypeStruct((B, N_GATES, T), F32)),
        grid=(T // tm, B, n_sec),
        in_specs=in_specs,
        out_specs=(pl.BlockSpec((1, tm, tn), lambda i, b, j: (b, i, j)),
                   pl.BlockSpec((1, N_GATES, tm), lambda i, b, j: (b, 0, i))),
        scratch_shapes=[pltpu.VMEM((tm, D), BF16)],
        compiler_params=_cparams("parallel", "parallel", "arbitrary"),
        name="proj_lat" if tables is not None else "proj_ctx",
    )(*args)


def _ret_build(dl_ref, q_ref, k_ref, v_ref, rg_ref, ck_ref, cv_ref, o_ref,
               sf_ref, sb_ref, fs_ref, bs_ref, dec_ref, d_ref):
    h = pl.program_id(0)
    L = SCAN_L
    n_chunks = q_ref.shape[1] // L
    n_ctx_chunks = ck_ref.shape[1] // L
    lgf = _log_sigmoid(jnp.full((1, 1), dl_ref[0, h], F32))
    lgb = _log_sigmoid(jnp.full((1, 1), dl_ref[1, h], F32))

    @pl.when(pl.program_id(1) == 0)
    def _():
        ri = lax.broadcasted_iota(I32, (L, L), 0)
        ci = lax.broadcasted_iota(I32, (L, L), 1)
        rel = (ri - ci).astype(F32)
        d_ref[...] = jnp.where(rel >= 0.0, jnp.exp(jnp.maximum(rel, 0.0) * lgf),
                               jnp.exp(jnp.maximum(-rel, 0.0) * lgb))
        row = lax.broadcasted_iota(I32, (L, HEAD_W), 0).astype(F32)
        dec_ref[0] = jnp.exp((row + 1.0) * lgf)
        dec_ref[1] = jnp.exp((L - 1.0 - row) * lgf)
        dec_ref[2] = jnp.exp((L - row) * lgb)
        dec_ref[3] = jnp.exp(row * lgb)

    cdf = jnp.exp(L * lgf)
    cdb = jnp.exp(L * lgb)

    def update(s_ref, kc, vc, kd, cd):
        kdec = (kc.astype(F32) * kd).astype(BF16)
        s_ref[...] = s_ref[...] * cd + lax.dot_general(kdec, vc, TN_DIMS, preferred_element_type=F32)

    sf_ref[...] = jnp.zeros_like(sf_ref)
    sb_ref[...] = jnp.zeros_like(sb_ref)
    for c in range(n_ctx_chunks):
        update(sf_ref, ck_ref[0, c * L:(c + 1) * L, :], cv_ref[0, c * L:(c + 1) * L, :], dec_ref[1], cdf)
    for c in reversed(range(n_ctx_chunks)):
        update(sb_ref, ck_ref[0, c * L:(c + 1) * L, :], cv_ref[0, c * L:(c + 1) * L, :], dec_ref[3], cdb)

    def state_pass(i, carry):
        cb = n_chunks - 1 - i
        rf = pl.multiple_of(i * L, L)
        rb = pl.multiple_of(cb * L, L)
        fs_ref[i] = sf_ref[...].astype(BF16)
        bs_ref[cb] = sb_ref[...].astype(BF16)
        update(sf_ref, k_ref[0, pl.ds(rf, L), :], v_ref[0, pl.ds(rf, L), :], dec_ref[1], cdf)
        update(sb_ref, k_ref[0, pl.ds(rb, L), :], v_ref[0, pl.ds(rb, L), :], dec_ref[3], cdb)
        return carry

    def finish_states():
        fs_ref[n_chunks - 1] = sf_ref[...].astype(BF16)
        bs_ref[0] = sb_ref[...].astype(BF16)

    def out_chunk(c):
        r0 = pl.multiple_of(c * L, L)
        q = q_ref[0, pl.ds(r0, L), :]
        k = k_ref[0, pl.ds(r0, L), :]
        v = v_ref[0, pl.ds(r0, L), :]
        s = lax.dot_general(q, k, NT_DIMS, preferred_element_type=F32)
        att = (s * d_ref[...]).astype(BF16)
        o = jnp.dot(att, v, preferred_element_type=F32)
        o = o + jnp.dot(q, fs_ref[c], preferred_element_type=F32) * dec_ref[0]
        o = o + jnp.dot(q, bs_ref[c], preferred_element_type=F32) * dec_ref[2]
        rg = rg_ref[0, pl.ds(r0, L), :].astype(F32)
        o_ref[0, pl.ds(r0, L), :] = (_layer_norm(o) * _silu(rg)).astype(BF16)

    return state_pass, finish_states, out_chunk


def _ret_scratch(n_chunks):
    return [pltpu.VMEM((HEAD_W, HEAD_W), F32),
            pltpu.VMEM((HEAD_W, HEAD_W), F32),
            pltpu.VMEM((n_chunks, HEAD_W, HEAD_W), BF16),
            pltpu.VMEM((n_chunks, HEAD_W, HEAD_W), BF16),
            pltpu.VMEM((4, SCAN_L, HEAD_W), F32),
            pltpu.VMEM((SCAN_L, SCAN_L), F32)]


def _mlstm_build(qp_ref, kp_ref, v_ref, mo_ref, ckp_ref, cv_ref, gt_ref, cgt_ref,
                 wq_ref, bq_ref, wk_ref, bk_ref, o_ref,
                 tab_ref, row_ref,
                 cf_ref, mf_ref, cb_ref, mb_ref, cfs_ref, mfs_ref, cbs_ref, mbs_ref, mask_ref,
                 xk_ref, xq_ref, q_ref, k_ref, ck_ref):
    L = SCAN_L
    T = qp_ref.shape[1]
    Tc = ckp_ref.shape[1]
    n_chunks = T // L
    n_ctx_chunks = Tc // L
    CV = CONV_ROWS

    def conv_stage(src_ref, xs_ref, t_len):
        xs_ref[pl.ds(0, 8), :] = jnp.zeros((8, HEAD_W), F32)
        xs_ref[pl.ds(8 + t_len, 8), :] = jnp.zeros((8, HEAD_W), F32)
        xs_ref[pl.ds(8, t_len), :] = src_ref[0].astype(F32)

    def conv_rows(xs_ref, c, w_ref, b_ref, dst_ref, scale):
        w = w_ref[...]
        r0 = pl.multiple_of(c * CV, CV)
        win = xs_ref[pl.ds(r0, CV + 16), :]
        prev = pltpu.roll(win, 1, 0)[8:8 + CV, :]
        cur = win[8:8 + CV, :]
        nxt = pltpu.roll(win, CV + 15, 0)[8:8 + CV, :]
        y = _silu(prev * w[0:1, :] + cur * w[1:2, :] + nxt * w[2:3, :] + b_ref[...])
        if scale != 1.0:
            y = y * scale
        dst_ref[pl.ds(r0, CV), :] = y.astype(BF16)

    per_chunk = L // CV

    def conv_k(c):
        for u in range(per_chunk):
            conv_rows(xk_ref, c * per_chunk + u, wk_ref, bk_ref, k_ref, KEY_SCALE)

    def conv_q(c):
        for u in range(per_chunk):
            conv_rows(xq_ref, c * per_chunk + u, wq_ref, bq_ref, q_ref, 1.0)

    conv_stage(ckp_ref, xk_ref, Tc)
    for c in range(Tc // CV):
        conv_rows(xk_ref, c, wk_ref, bk_ref, ck_ref, KEY_SCALE)
    conv_stage(kp_ref, xk_ref, T)
    conv_stage(qp_ref, xq_ref, T)
    conv_k(0)
    conv_k(n_chunks - 1)
    n_k_steps = n_chunks // 2 - 1

    def conv_step(i, keys):
        if keys:
            conv_k(i + 1)
            conv_k(n_chunks - 2 - i)
        else:
            conv_q(2 * (i - n_k_steps))
            conv_q(2 * (i - n_k_steps) + 1)

    ri = lax.broadcasted_iota(I32, (L, L), 0)
    ci = lax.broadcasted_iota(I32, (L, L), 1)
    tri_u = (ri <= ci).astype(BF16)
    lane8 = lax.broadcasted_iota(I32, (8, L), 1)
    sub8 = lax.broadcasted_iota(I32, (8, L), 0)
    sel_r = lax.broadcasted_iota(I32, (24, 8 * 128), 0) % 8
    sel_c = lax.broadcasted_iota(I32, (24, 8 * 128), 1) // 128
    sel3 = (sel_r == sel_c).astype(BF16)
    ones_cols = jnp.ones((L, AUG_W - HEAD_W), BF16)

    def chunk_tables(g8, n_used, state_only):
        i_f, i_b = g8[0], g8[2]
        lf_f, lf_b = _log_sigmoid(g8[1]), _log_sigmoid(g8[3])
        cs3 = jnp.dot(_split3(jnp.concatenate([lf_f, lf_b], axis=0)), tri_u,
                      preferred_element_type=F32)
        cs = cs3[0:16] + cs3[16:32] + cs3[32:48]
        b_f = cs[0:8]
        b_b = cs[8:16, L - 1:L] - cs[8:16] + lf_b
        z_f = i_f - b_f
        z_b = i_b - b_b
        g_f = b_f[:, L - 1:L] - b_f + i_f
        g_b = b_b[:, 0:1] - b_b + i_b
        mf, mb = z_f, z_b
        s = 1
        while s < L:
            mf = jnp.maximum(mf, jnp.where(lane8 >= s, pltpu.roll(mf, s, 1), NEG_INF))
            mb = jnp.maximum(mb, jnp.where(lane8 < L - s, pltpu.roll(mb, L - s, 1), NEG_INF))
            s *= 2
        mb = jnp.where(lane8 < L - 1, pltpu.roll(mb, L - 1, 1), NEG_INF)
        reps = [None if state_only and t not in (2, 5) else
                lax.dot_general(_split3(val), sel3[:, 0:n_used * 128], TN_DIMS, preferred_element_type=F32)
                for t, val in enumerate((mf, b_f, g_f, mb, b_b, g_b))]

        def rows_of(c):
            out = jnp.zeros((8, L), F32)
            for r, val in enumerate((z_f, z_b, g_f, g_b, b_f, b_b)):
                out = jnp.where(sub8 == r, val[c:c + 1], out)
            return out

        return rows_of, reps

    lat_rows, lat_reps = chunk_tables(gt_ref[0, 0], n_chunks, False)
    for c in range(n_chunks):
        row_ref[c] = lat_rows(c)
        for t in range(N_TAB):
            tab_ref[t, c * L:(c + 1) * L, :] = lat_reps[t][:, c * 128:(c + 1) * 128]

    def lanes2(x):
        return jnp.concatenate([x, x], axis=1)

    def advance(k, v, g_rep, g_row, b_last, c_ref, m_ref):
        m = m_ref[...]
        m_new = jnp.maximum(b_last + m, jnp.max(g_row, axis=-1, keepdims=True))
        kw = (k.astype(F32) * jnp.exp(lanes2(g_rep) - m_new)).astype(BF16)
        v_aug = jnp.concatenate([v, ones_cols], axis=1)
        c_ref[...] = jnp.exp(b_last + m - m_new) * c_ref[...] + lax.dot_general(
            kw, v_aug, TN_DIMS, preferred_element_type=F32)
        m_ref[...] = m_new

    for r in (cf_ref, mf_ref, cb_ref, mb_ref):
        r[...] = jnp.zeros_like(r)
    ctx_rows, ctx_reps = chunk_tables(cgt_ref[0, 0], n_ctx_chunks, True)
    for c in range(n_ctx_chunks):
        rows = ctx_rows(c)
        advance(ck_ref[c * L:(c + 1) * L, :], cv_ref[0, c * L:(c + 1) * L, :],
                ctx_reps[2][:, c * 128:(c + 1) * 128], rows[2:3], rows[4:5, L - 1:L], cf_ref, mf_ref)
    for c in reversed(range(n_ctx_chunks)):
        rows = ctx_rows(c)
        advance(ck_ref[c * L:(c + 1) * L, :], cv_ref[0, c * L:(c + 1) * L, :],
                ctx_reps[5][:, c * 128:(c + 1) * 128], rows[3:4], rows[5:6, 0:1], cb_ref, mb_ref)

    def state_pass(i, carry, keys):
        cb = n_chunks - 1 - i
        rf = pl.multiple_of(i * L, L)
        rb = pl.multiple_of(cb * L, L)
        cfs_ref[i] = cf_ref[...].astype(BF16)
        mfs_ref[i] = mf_ref[...]
        cbs_ref[cb] = cb_ref[...].astype(BF16)
        mbs_ref[cb] = mb_ref[...]
        rows_f = row_ref[i]
        rows_b = row_ref[cb]
        advance(k_ref[pl.ds(rf, L), :], v_ref[0, pl.ds(rf, L), :], tab_ref[2, pl.ds(rf, L), :],
                rows_f[2:3], rows_f[4:5, L - 1:L], cf_ref, mf_ref)
        advance(k_ref[pl.ds(rb, L), :], v_ref[0, pl.ds(rb, L), :], tab_ref[5, pl.ds(rb, L), :],
                rows_b[3:4], rows_b[5:6, 0:1], cb_ref, mb_ref)
        conv_step(i, keys)
        return carry

    def finish_states():
        cfs_ref[n_chunks - 1] = cf_ref[...].astype(BF16)
        mfs_ref[n_chunks - 1] = mf_ref[...]
        cbs_ref[0] = cb_ref[...].astype(BF16)
        mbs_ref[0] = mb_ref[...]

    def direction(q, v_aug, s, z_row, zmax_rep, b_rep, mask, c_in, m_in):
        mx = jnp.maximum(zmax_rep, m_in)
        att = s * jnp.exp((z_row - lanes2(mx)) + mask)
        na = jnp.dot(att.astype(BF16), v_aug, preferred_element_type=F32)
        qa = jnp.dot(q, c_in, preferred_element_type=F32)
        a = jnp.exp(m_in - mx)
        num = na[:, 0:HEAD_W] + lanes2(a) * qa[:, 0:HEAD_W]
        den = na[:, HEAD_W:] + a * qa[:, HEAD_W:]
        scale = 1.0 / jnp.maximum(jnp.abs(den), jnp.exp(-(b_rep + mx)))
        return num * lanes2(scale)

    @pl.when(pl.program_id(1) == 0)
    def _():
        mask_ref[0] = jnp.where(ci <= ri, 0.0, NEG_INF)
        mask_ref[1] = jnp.where(ci > ri, 0.0, NEG_INF)

    def out_chunk(c):
        r0 = pl.multiple_of(c * L, L)
        q = q_ref[pl.ds(r0, L), :]
        k = k_ref[pl.ds(r0, L), :]
        v_aug = jnp.concatenate([v_ref[0, pl.ds(r0, L), :], ones_cols], axis=1)
        s = lax.dot_general(q, k, NT_DIMS, preferred_element_type=F32)
        rows = row_ref[c]
        tot = direction(q, v_aug, s, rows[0:1], tab_ref[0, pl.ds(r0, L), :], tab_ref[1, pl.ds(r0, L), :],
                        mask_ref[0], cfs_ref[c], mfs_ref[c])
        tot = tot + direction(q, v_aug, s, rows[1:2], tab_ref[3, pl.ds(r0, L), :],
                              tab_ref[4, pl.ds(r0, L), :], mask_ref[1], cbs_ref[c], mbs_ref[c])
        mo = mo_ref[0, pl.ds(r0, L), :].astype(F32)
        o_ref[0, pl.ds(r0, L), :] = (_layer_norm(tot) * jax.nn.sigmoid(mo)).astype(BF16)

    return state_pass, finish_states, out_chunk, n_k_steps


def _mlstm_scratch(T, Tc, n_chunks):
    state = [pltpu.VMEM((HEAD_W, AUG_W), F32), pltpu.VMEM((1, 1), F32)]
    snaps = [pltpu.VMEM((n_chunks, HEAD_W, AUG_W), BF16), pltpu.VMEM((n_chunks, 1, 1), F32)]
    return [pltpu.VMEM((N_TAB, T, 128), F32), pltpu.VMEM((n_chunks, 8, SCAN_L), F32)] \
        + state + state + snaps + snaps + [pltpu.VMEM((2, SCAN_L, SCAN_L), F32)] \
        + [pltpu.VMEM((T + 16, HEAD_W), F32), pltpu.VMEM((T + 16, HEAD_W), F32),
           pltpu.VMEM((T, HEAD_W), BF16), pltpu.VMEM((T, HEAD_W), BF16), pltpu.VMEM((Tc, HEAD_W), BF16)]


def _scan_kernel(n_ret_scratch, dl_ref, rq_ref, rk_ref, rv_ref, rg_ref, rck_ref, rcv_ref,
                 mq_ref, mk_ref, mv_ref, mo_ref, mck_ref, mcv_ref, gt_ref, cgt_ref,
                 wq_ref, bq_ref, wk_ref, bk_ref, r_ref, m_ref, *scratch):
    n_chunks = rq_ref.shape[1] // SCAN_L
    ret = _ret_build(dl_ref, rq_ref, rk_ref, rv_ref, rg_ref, rck_ref, rcv_ref, r_ref,
                     *scratch[:n_ret_scratch])
    mls = _mlstm_build(mq_ref, mk_ref, mv_ref, mo_ref, mck_ref, mcv_ref, gt_ref, cgt_ref,
                       wq_ref, bq_ref, wk_ref, bk_ref, m_ref, *scratch[n_ret_scratch:])

    def state_pass(i, carry, keys):
        ret[0](i, carry)
        mls[0](i, carry, keys)
        return carry

    n_k_steps = mls[3]
    lax.fori_loop(0, n_k_steps, functools.partial(state_pass, keys=True), 0)
    lax.fori_loop(n_k_steps, n_chunks - 1, functools.partial(state_pass, keys=False), 0)
    ret[1]()
    mls[1]()

    def out_pass(i, carry):
        for c in (2 * i, 2 * i + 1):
            ret[2](c)
            mls[2](c)
        return carry

    lax.fori_loop(0, n_chunks // 2, out_pass, 0)


def _scans(decay_logit, p_lat, p_ctx, gt, cgt, conv_w, conv_b, ret_lat, ret_ctx, ml_lat, ml_ctx):
    B, T, _ = p_lat.shape
    Tc = p_ctx.shape[1]
    assert T % (2 * SCAN_L) == 0 and Tc % SCAN_L == 0 and T // SCAN_L <= 8
    n_chunks = T // SCAN_L

    def lat(sec):
        return pl.BlockSpec((1, T, HEAD_W), lambda h, b: (b, 0, sec * HEADS + h))

    def cx(sec):
        return pl.BlockSpec((1, Tc, HEAD_W), lambda h, b: (b, 0, sec * HEADS + h))

    gates = pl.BlockSpec((1, 1, N_GK, 8, SCAN_L), lambda h, b: (b, h, 0, 0, 0))
    out = pl.BlockSpec((1, T, HEAD_W), lambda h, b: (b, 0, h))
    conv_specs = [pl.BlockSpec((3, HEAD_W), lambda h, b: (0, h)),
                  pl.BlockSpec((1, HEAD_W), lambda h, b: (0, h)),
                  pl.BlockSpec((3, HEAD_W), lambda h, b: (0, HEADS + h)),
                  pl.BlockSpec((1, HEAD_W), lambda h, b: (0, HEADS + h))]
    ret_scratch = _ret_scratch(n_chunks)
    return pl.pallas_call(
        functools.partial(_scan_kernel, len(ret_scratch)),
        out_shape=(jax.ShapeDtypeStruct((B, T, BRANCH_W), BF16),
                   jax.ShapeDtypeStruct((B, T, BRANCH_W), BF16)),
        grid=(HEADS, B),
        in_specs=[pl.BlockSpec(memory_space=pltpu.SMEM)]
        + [lat(s) for s in ret_lat] + [cx(s) for s in ret_ctx]
        + [lat(s) for s in ml_lat] + [cx(s) for s in ml_ctx] + [gates, gates] + conv_specs,
        out_specs=(out, out),
        scratch_shapes=ret_scratch + _mlstm_scratch(T, Tc, n_chunks),
        compiler_params=_cparams("arbitrary", "arbitrary"),
        name="scans",
    )(decay_logit, *([p_lat] * 4), *([p_ctx] * 2), *([p_lat] * 4), *([p_ctx] * 2), gt, cgt,
      conv_w, conv_b, conv_w, conv_b)


def _merge_kernel(alpha, r_ref, m_ref, gr_ref, gm_ref, x_ref, g1_ref, sh2_ref, sc2_ref,
                  lng_ref, lnb_ref, wr_ref, wm_ref, wo_ref, wrt_ref, brt_ref, e1_ref, e3_ref, e2_ref,
                  x1_ref, ua_ref, ub_ref, ri_ref, rw_ref, cnt_ref, e1b_ref, e3b_ref, e2b_ref,
                  carry_ref, u_ref):
    for src, dst in ((e1_ref, e1b_ref), (e3_ref, e3b_ref), (e2_ref, e2b_ref)):
        dst[...] = src[...].astype(BF16)

    @pl.when(jnp.logical_and(pl.program_id(0) == 0, pl.program_id(1) == 0))
    def _():
        carry_ref[...] = jnp.zeros_like(carry_ref)
        tm = x_ref.shape[1]
        r = lax.broadcasted_iota(I32, (tm, tm), 0)
        c = lax.broadcasted_iota(I32, (tm, tm), 1)
        u_ref[...] = (r < c).astype(BF16)

    yr = jnp.dot(r_ref[0], wr_ref[...], preferred_element_type=F32)
    ym = jnp.dot(m_ref[0], wm_ref[...], preferred_element_type=F32)
    y = jax.nn.sigmoid(gr_ref[0].astype(F32)) * yr + jax.nn.sigmoid(gm_ref[0].astype(F32)) * ym
    yo = jnp.dot(y.astype(BF16), wo_ref[...], preferred_element_type=F32)
    x1 = _layer_norm(alpha * x_ref[0] + g1_ref[0] * yo) * lng_ref[...] + lnb_ref[...]
    x1_ref[0] = x1
    u2 = _layer_norm(x1) * (1.0 + sc2_ref[0]) + sh2_ref[0]
    ua_ref[0] = _pack_pairs(u2[:, 0:PACK_W], u2[:, PACK_W:2 * PACK_W])
    ub_ref[0] = _pack_pairs(u2[:, 2 * PACK_W:3 * PACK_W], u2[:, 3 * PACK_W:4 * PACK_W])
    lt = lax.dot_general(wrt_ref[...], u2.astype(BF16), NT_DIMS, preferred_element_type=F32) + brt_ref[...]
    _route_tile(lt, ri_ref, rw_ref, cnt_ref, carry_ref, u_ref)


def _route_tile(lt, ri_ref, rw_ref, cnt_ref, carry_ref, u_ref):
    tm = lt.shape[1]
    lg = lt[0:N_GROUPS, :]
    eg = jnp.exp(lg - jnp.max(lg, axis=0, keepdims=True))
    pg = eg / jnp.sum(eg, axis=0, keepdims=True)
    pg_top = jnp.max(pg, axis=0, keepdims=True)
    rows_g = lax.broadcasted_iota(I32, pg.shape, 0)
    g_idx = jnp.min(jnp.where(pg == pg_top, rows_g, N_GROUPS), axis=0, keepdims=True)

    le = jnp.zeros((EXP_PER_GROUP, tm), F32)
    for g in range(N_GROUPS):
        lo = 8 + g * EXP_PER_GROUP
        le = jnp.where(g_idx == g, lt[lo:lo + EXP_PER_GROUP, :], le)
    ee = jnp.exp(le - jnp.max(le, axis=0, keepdims=True))
    pe = ee / jnp.sum(ee, axis=0, keepdims=True)
    rows_e = lax.broadcasted_iota(I32, pe.shape, 0)
    v1 = jnp.max(pe, axis=0, keepdims=True)
    i1 = jnp.min(jnp.where(pe == v1, rows_e, EXP_PER_GROUP), axis=0, keepdims=True)
    pe2 = jnp.where(rows_e == i1, -1.0, pe)
    v2 = jnp.max(pe2, axis=0, keepdims=True)
    i2 = jnp.min(jnp.where(pe2 == v2, rows_e, EXP_PER_GROUP), axis=0, keepdims=True)
    den = v1 + v2
    rw_ref[...] = jnp.zeros_like(rw_ref)
    rw_ref[0:1, :] = pg_top * v1 / den
    rw_ref[1:2, :] = pg_top * v2 / den
    e1 = g_idx * EXP_PER_GROUP + i1
    e2 = g_idx * EXP_PER_GROUP + i2

    rows_x = lax.broadcasted_iota(I32, (N_EXPERTS, tm), 0)
    oh1 = (rows_x == e1).astype(F32)
    oh2 = (rows_x == e2).astype(F32)
    both = oh1 + oh2
    before = carry_ref[:, 0:1] + jnp.dot(both.astype(BF16), u_ref[...], preferred_element_type=F32)
    ri_ref[0:1, :] = e1
    ri_ref[1:2, :] = e2
    ri_ref[2:3, :] = jnp.sum(oh1 * before, axis=0, keepdims=True).astype(I32)
    ri_ref[3:4, :] = jnp.sum(oh2 * before, axis=0, keepdims=True).astype(I32)
    carry_ref[...] = carry_ref[...] + jnp.sum(both, axis=1, keepdims=True)
    cnt_ref[...] = carry_ref[...].astype(I32)


def _merge(alpha, r, m, p_lat, sec_gates, x, g1, sh2, sc2, lng, lnb, wr, wm, wo, wrt, brt, expert_w):
    B, T, D = x.shape
    tm = MERGE_TM
    per_b = T // tm
    n = B * T
    n_steps = B * per_b
    sliced = [w.reshape(n_steps, w.shape[0] * w.shape[1] // n_steps, w.shape[2]) for w in expert_w]
    assert all(s.shape[1] % 16 == 0 and s.size == w.size for s, w in zip(sliced, expert_w))

    def slab(s):
        return pl.BlockSpec((1,) + s.shape[1:], lambda b, i: (b * per_b + i, 0, 0))

    def tile(w):
        return pl.BlockSpec((1, tm, w), lambda b, i: (b, i, 0))

    def sec(s):
        return pl.BlockSpec((1, tm, BRANCH_W), lambda b, i: (b, i, s))

    def mod():
        return pl.BlockSpec((1, 1, D), lambda b, i: (b, 0, 0))

    def const(shape):
        return pl.BlockSpec(shape, lambda b, i: (0,) * len(shape))

    outs = pl.pallas_call(
        functools.partial(_merge_kernel, alpha),
        out_shape=(jax.ShapeDtypeStruct((B, T, D), F32),
                   jax.ShapeDtypeStruct((B, T, PACK_W), U32),
                   jax.ShapeDtypeStruct((B, T, PACK_W), U32),
                   jax.ShapeDtypeStruct((4, n), I32),
                   jax.ShapeDtypeStruct((8, n), F32),
                   jax.ShapeDtypeStruct((N_EXPERTS, 128), I32))
        + tuple(jax.ShapeDtypeStruct(s.shape, BF16) for s in sliced),
        grid=(B, per_b),
        in_specs=[tile(BRANCH_W), tile(BRANCH_W), sec(sec_gates[0]), sec(sec_gates[1]), tile(D),
                  mod(), mod(), mod(), const((1, D)), const((1, D)),
                  const((BRANCH_W, D)), const((BRANCH_W, D)), const((D, D)),
                  const((ROUTE_ROWS, D)), const((ROUTE_ROWS, 1))] + [slab(s) for s in sliced],
        out_specs=(tile(D), tile(PACK_W), tile(PACK_W),
                   pl.BlockSpec((4, tm), lambda b, i: (0, b * per_b + i)),
                   pl.BlockSpec((8, tm), lambda b, i: (0, b * per_b + i)),
                   pl.BlockSpec((N_EXPERTS, 128), lambda b, i: (0, 0)))
        + tuple(slab(s) for s in sliced),
        scratch_shapes=[pltpu.VMEM((N_EXPERTS, 128), F32), pltpu.VMEM((tm, tm), BF16)],
        compiler_params=_cparams("arbitrary", "arbitrary"),
        name="merge",
    )(r, m, p_lat, p_lat, x, g1, sh2, sc2, lng, lnb, wr, wm, wo, wrt, brt, *sliced)
    return outs[:6] + tuple(o.reshape(w.shape) for o, w in zip(outs[6:], expert_w))


def _sc_mesh():
    return plsc.VectorSubcoreMesh(core_axis_name="c", subcore_axis_name="s")


def _sc_scatter2(rows_a, rows_b, idx0, idx1, n_out):
    m, w = rows_a.shape
    out = jax.ShapeDtypeStruct((n_out, w), rows_a.dtype)

    @functools.partial(pl.kernel, out_type=(out, out), mesh=_sc_mesh(), scratch_types=[])
    def k(xa_hbm, xb_hbm, i0_hbm, i1_hbm, oa_hbm, ob_hbm):
        for x_hbm, o_hbm in ((xa_hbm, oa_hbm), (xb_hbm, ob_hbm)):
            def body(x_vmem, i0_vmem, i1_vmem, o_hbm=o_hbm):
                pltpu.sync_copy(x_vmem, o_hbm.at[i0_vmem.at[0]])
                pltpu.sync_copy(x_vmem, o_hbm.at[i1_vmem.at[0]])

            pltpu.emit_pipeline(
                body,
                grid=(m // SC_WIN,),
                in_specs=[pl.BlockSpec((SC_WIN, w), lambda i: (i, 0)),
                          pl.BlockSpec((1, SC_WIN), lambda i: (0, i)),
                          pl.BlockSpec((1, SC_WIN), lambda i: (0, i))],
                out_specs=[],
                core_axis_name=("c", "s"),
                dimension_semantics=(pltpu.PARALLEL,),
            )(x_hbm, i0_hbm, i1_hbm)

    return k(rows_a, rows_b, idx0.reshape(1, m), idx1.reshape(1, m))


def _sc_gather(table_a, table_b, idx):
    m = idx.shape[0]
    w = table_a.shape[1]
    out = jax.ShapeDtypeStruct((m, w), table_a.dtype)

    @functools.partial(pl.kernel, out_type=(out, out), mesh=_sc_mesh(), scratch_types=[])
    def k(ta_hbm, tb_hbm, i_hbm, oa_hbm, ob_hbm):
        for t_hbm, o_hbm in ((ta_hbm, oa_hbm), (tb_hbm, ob_hbm)):
            def body(i_vmem, o_vmem, t_hbm=t_hbm):
                pltpu.sync_copy(t_hbm.at[i_vmem.at[0]], o_vmem)

            pltpu.emit_pipeline(
                body,
                grid=(m // SC_WIN,),
                in_specs=[pl.BlockSpec((1, SC_WIN), lambda i: (0, i))],
                out_specs=[pl.BlockSpec((SC_WIN, w), lambda i: (i, 0))],
                core_axis_name=("c", "s"),
                dimension_semantics=(pltpu.PARALLEL,),
            )(i_hbm, o_hbm)

    return k(table_a, table_b, idx.reshape(1, m))


def _expert_kernel(be_ref, nv_ref, xa_ref, xb_ref, w1_ref, w3_ref, w2_ref, ya_ref, yb_ref):
    j = pl.program_id(0)
    nv = nv_ref[j]

    @pl.when(nv > 0)
    def _():
        valid = lax.broadcasted_iota(I32, xa_ref.shape, 0) < nv
        zero = jnp.zeros(xa_ref.shape, U32)
        parts = _unpack_pairs(jnp.where(valid, xa_ref[...], zero)) + \
            _unpack_pairs(jnp.where(valid, xb_ref[...], zero))
        x = jnp.concatenate([p.astype(BF16) for p in parts], axis=1)
        h1 = jnp.dot(x, w1_ref[0], preferred_element_type=F32)
        h3 = jnp.dot(x, w3_ref[0], preferred_element_type=F32)
        y = jnp.dot((_silu(h1) * h3).astype(BF16), w2_ref[0], preferred_element_type=F32)
        ya_ref[...] = _pack_pairs(y[:, 0:PACK_W], y[:, PACK_W:2 * PACK_W])
        yb_ref[...] = _pack_pairs(y[:, 2 * PACK_W:3 * PACK_W], y[:, 3 * PACK_W:4 * PACK_W])

    @pl.when(nv == 0)
    def _():
        ya_ref[...] = jnp.zeros_like(ya_ref)
        yb_ref[...] = jnp.zeros_like(yb_ref)


def _experts(block_exp, n_valid, xa, xb, w1, w3, w2):
    n_slots = xa.shape[0]
    n_blocks = n_slots // MOE_BLK
    d, de = w1.shape[1], w1.shape[2]
    slot = pl.BlockSpec((MOE_BLK, PACK_W), lambda j, be, nv: (j, 0))
    grid_spec = pltpu.PrefetchScalarGridSpec(
        num_scalar_prefetch=2,
        grid=(n_blocks,),
        in_specs=[slot, slot,
                  pl.BlockSpec((1, d, de), lambda j, be, nv: (be[j], 0, 0)),
                  pl.BlockSpec((1, d, de), lambda j, be, nv: (be[j], 0, 0)),
                  pl.BlockSpec((1, de, d), lambda j, be, nv: (be[j], 0, 0))],
        out_specs=(slot, slot),
    )
    return pl.pallas_call(
        _expert_kernel,
        out_shape=(jax.ShapeDtypeStruct((n_slots, PACK_W), U32),
                   jax.ShapeDtypeStruct((n_slots, PACK_W), U32)),
        grid_spec=grid_spec,
        compiler_params=_cparams("parallel"),
        name="experts",
    )(block_exp, n_valid, xa, xb, w1, w3, w2)


def _final_kernel(alpha, x1_ref, a0_ref, b0_ref, a1_ref, b1_ref, w_ref, g2_ref, lng_ref, lnb_ref, o_ref):
    w = w_ref[...].T
    w0 = w[:, 0:1]
    w1 = w[:, 1:2]
    parts0 = _unpack_pairs(a0_ref[...]) + _unpack_pairs(b0_ref[...])
    parts1 = _unpack_pairs(a1_ref[...]) + _unpack_pairs(b1_ref[...])
    f = jnp.concatenate([w0 * p0 + w1 * p1 for p0, p1 in zip(parts0, parts1)], axis=1)
    o_ref[0] = _layer_norm(alpha * x1_ref[0] + g2_ref[0] * f) * lng_ref[...] + lnb_ref[...]


def _final(alpha, x1, ya, yb, w, g2, lng, lnb):
    B, T, D = x1.shape
    tm = min(FINAL_TM, T)
    per_b = T // tm
    n_tiles = B * per_b

    def rows(k):
        return pl.BlockSpec((tm, PACK_W), lambda b, i: (k * n_tiles + b * per_b + i, 0))

    return pl.pallas_call(
        functools.partial(_final_kernel, alpha),
        out_shape=jax.ShapeDtypeStruct((B, T, D), F32),
        grid=(B, per_b),
        in_specs=[pl.BlockSpec((1, tm, D), lambda b, i: (b, i, 0)),
                  rows(0), rows(0), rows(1), rows(1),
                  pl.BlockSpec((8, tm), lambda b, i: (0, b * per_b + i)),
                  pl.BlockSpec((1, 1, D), lambda b, i: (b, 0, 0)),
                  pl.BlockSpec((1, D), lambda b, i: (0, 0)),
                  pl.BlockSpec((1, D), lambda b, i: (0, 0))],
        out_specs=pl.BlockSpec((1, tm, D), lambda b, i: (b, i, 0)),
        compiler_params=_cparams("parallel", "parallel"),
        name="final",
    )(x1, ya, yb, ya, yb, w, g2, lng, lnb)


def _rotary_tables(T):
    quarter = HEAD_W // 4
    freqs = ROPE_BASE ** (-jnp.arange(quarter, dtype=F32) / quarter)
    t = jnp.arange(T)
    ang_r = (t // GRID_W).astype(F32)[:, None] * freqs[None, :]
    ang_c = (t % GRID_W).astype(F32)[:, None] * freqs[None, :]
    cos = jnp.concatenate([jnp.cos(ang_r)] * 2 + [jnp.cos(ang_c)] * 2, axis=1)
    sin = jnp.concatenate([-jnp.sin(ang_r), jnp.sin(ang_r), -jnp.sin(ang_c), jnp.sin(ang_c)], axis=1)
    return cos, sin


def _per_head_gates(gt):
    B, _, T = gt.shape
    n_chunks = T // SCAN_L
    gth = gt.reshape(B, N_GK, HEADS, n_chunks, SCAN_L).transpose(0, 2, 1, 3, 4)
    return jnp.pad(gth, ((0, 0), (0, 0), (0, 0), (0, 8 - n_chunks), (0, 0)))


def _table_lookup(table, idx):
    sel = idx[..., None] == jnp.arange(table.shape[0], dtype=idx.dtype)
    return jnp.sum(jnp.where(sel, table, 0), axis=-1)


def kernel(x, c, ctx, c_ctx, w_ada, b_ada, w_in, b_mgate, ml_conv_w, ml_conv_b, ret_decay_logit, w_ret_branch, w_ml_branch, w_out, ln1_g, ln1_b, w_rg, b_rg, w_re, b_re, w_e1, w_e3, w_e2, ln2_g, ln2_b):
    B, T, D = x.shape
    depth = w_ada.shape[0]
    assert depth == 1 and D == BRANCH_W and T % GRID_W == 0
    alpha = (2 * depth) ** 0.25
    n_tok = B * T

    n_rows = -(-(B + 1) // 8) * 8
    cs = jnp.zeros((n_rows, D), F32).at[:B].set(c).at[B].set(c_ctx)
    mod = _ada(cs, w_ada[0], b_ada[0][None, :])
    sh1, sc1, g1, sh2, sc2, g2 = [mod[:B, None, i * D:(i + 1) * D] for i in range(6)]
    csh1 = mod[B, 0 * D:1 * D].reshape(1, 1, D)
    csc1 = mod[B, 1 * D:2 * D].reshape(1, 1, D)

    w = w_in[0]
    sec_w = [w[:, s * BRANCH_W:(s + 1) * BRANCH_W] for s in range(8)]
    g_lo = 8 * BRANCH_W
    w_gate_t = w[:, g_lo:g_lo + N_GATES].T.astype(BF16)
    b_gate = b_mgate[0][:, None]
    sec_w += [w[:, g_lo + N_GATES:g_lo + N_GATES + D], w[:, g_lo + N_GATES + D:]]
    w_lat = _repack(w_in, w[:, g_lo + N_GATES:], 8)
    w_ctx = jnp.concatenate([sec_w[1], sec_w[2], sec_w[5], sec_w[6]], axis=1).astype(BF16)
    kinds_lat = ("rot", "rot_scale") + ("plain",) * 8
    kinds_ctx = ("scale", "plain", "plain", "plain")
    p_lat, gt_lat = _proj(x, sh1, sc1, w_lat, w_gate_t, b_gate, kinds_lat, _rotary_tables(T))
    Tc = ctx.shape[1]
    p_ctx, gt_ctx = _proj(ctx.reshape(1, B * Tc, D), csh1, csc1, w_ctx, w_gate_t, b_gate, kinds_ctx)
    p_ctx = p_ctx.reshape(B, Tc, -1)
    gt_ctx = gt_ctx.reshape(N_GATES, B, Tc).transpose(1, 0, 2)

    ret, mls = _scans(ret_decay_logit[0], p_lat, p_ctx, _per_head_gates(gt_lat), _per_head_gates(gt_ctx),
                      ml_conv_w[0], ml_conv_b[0][None, :], (0, 1, 2, 3), (0, 1), (4, 5, 6, 7), (2, 3))

    wrt = jnp.zeros((ROUTE_ROWS, D), F32).at[:N_GROUPS].set(w_rg[0].T).at[8:8 + N_EXPERTS].set(w_re[0].T)
    brt = jnp.zeros((ROUTE_ROWS, 1), F32).at[:N_GROUPS, 0].set(b_rg[0]).at[8:8 + N_EXPERTS, 0].set(b_re[0])
    x1, ua, ub, ri, rw, cnt, we1, we3, we2 = _merge(
        alpha, ret, mls, p_lat, (8, 9), x, g1, sh2, sc2, ln1_g[0][None, :], ln1_b[0][None, :],
        w_ret_branch[0].astype(BF16), w_ml_branch[0].astype(BF16), w_out[0].astype(BF16),
        wrt.astype(BF16), brt, (w_e1[0], w_e3[0], w_e2[0]))

    counts = cnt[:, 0]
    padded = (counts + MOE_BLK - 1) // MOE_BLK * MOE_BLK
    pad_end = jnp.cumsum(padded)
    pad_off = pad_end - padded
    dest = _table_lookup(pad_off, ri[0:2]) + ri[2:4]
    n_blocks = (2 * n_tok) // MOE_BLK + N_EXPERTS
    n_slots = n_blocks * MOE_BLK
    block_start = jnp.arange(n_blocks, dtype=I32) * MOE_BLK
    block_exp = jnp.minimum((block_start[:, None] >= pad_end[None, :]).sum(1), N_EXPERTS - 1).astype(I32)
    n_valid = jnp.clip(_table_lookup(counts, block_exp) - (block_start - _table_lookup(pad_off, block_exp)),
                       0, MOE_BLK).astype(I32)

    xa, xb = _sc_scatter2(ua.reshape(n_tok, PACK_W), ub.reshape(n_tok, PACK_W), dest[0], dest[1], n_slots)
    ya, yb = _experts(block_exp, n_valid, xa, xb, we1, we3, we2)
    ga, gb = _sc_gather(ya, yb, dest.reshape(2 * n_tok))
    return _final(alpha, x1, ga, gb, rw, g2, ln2_g[0][None, :], ln2_b[0][None, :])
```

```python
import functools

import jax
import jax.numpy as jnp
from jax import lax
from jax.experimental import pallas as pl
from jax.experimental.pallas import tpu as pltpu
from jax.experimental.pallas import tpu_sc as plsc

F32 = jnp.float32
BF16 = jnp.bfloat16
U32 = jnp.uint32
I32 = jnp.int32
HIGHEST = lax.Precision.HIGHEST

HEADS = 4
HEAD_W = 256
BRANCH_W = HEADS * HEAD_W
GRID_W = 64
ROPE_BASE = 10000.0
N_GATES = 16
N_GK = N_GATES // HEADS
N_GROUPS = 4
EXP_PER_GROUP = 8
N_EXPERTS = N_GROUPS * EXP_PER_GROUP
LN_EPS = 1e-5
NEG_INF = -1e30
KEY_SCALE = HEAD_W ** -0.5

SCAN_L = 256
CONV_ROWS = 128
PROJ_TM = 2048
PROJ_SUB = 256
MERGE_TM = 512
FINAL_TM = 1024
MOE_BLK = 512
SC_WIN = 128
PACK_W = 256
ROUTE_ROWS = 64
N_TAB = 6
AUG_W = HEAD_W + 128
VMEM_LIMIT = 48 * 1024 * 1024

NT_DIMS = (((1,), (1,)), ((), ()))
TN_DIMS = (((0,), (0,)), ((), ()))


def _cparams(*sem):
    return pltpu.CompilerParams(dimension_semantics=sem, vmem_limit_bytes=VMEM_LIMIT)


def _layer_norm(x):
    mu = jnp.mean(x, axis=-1, keepdims=True)
    xc = x - mu
    var = jnp.mean(xc * xc, axis=-1, keepdims=True)
    return xc * lax.rsqrt(var + LN_EPS)


def _log_sigmoid(x):
    return jnp.minimum(x, 0.0) - jnp.log1p(jnp.exp(-jnp.abs(x)))


def _silu(x):
    return x * jax.nn.sigmoid(x)


def _pack_pairs(hi, lo):
    hb = lax.bitcast_convert_type(hi.astype(BF16).astype(F32), U32)
    lb = lax.bitcast_convert_type(lo.astype(BF16).astype(F32), U32)
    return (hb & jnp.uint32(0xFFFF0000)) | (lb >> 16)


def _unpack_pairs(p):
    hi = lax.bitcast_convert_type(p & jnp.uint32(0xFFFF0000), F32)
    lo = lax.bitcast_convert_type(p << 16, F32)
    return hi, lo


def _split3(x):
    hi = x.astype(BF16).astype(F32)
    r1 = x - hi
    mid = r1.astype(BF16).astype(F32)
    lo = (r1 - mid).astype(BF16).astype(F32)
    return jnp.concatenate([hi, mid, lo], axis=0).astype(BF16)


def _ada_kernel(c_ref, w_ref, b_ref, o_ref):
    s = _silu(c_ref[...])
    o_ref[...] = jnp.dot(s, w_ref[...], precision=HIGHEST, preferred_element_type=F32) + b_ref[...]


def _ada(cs, w, b):
    rows, d = cs.shape
    cols = w.shape[1]
    tn = 1024
    return pl.pallas_call(
        _ada_kernel,
        out_shape=jax.ShapeDtypeStruct((rows, cols), F32),
        grid=(cols // tn,),
        in_specs=[pl.BlockSpec((rows, d), lambda j: (0, 0)),
                  pl.BlockSpec((d, tn), lambda j: (0, j)),
                  pl.BlockSpec((1, tn), lambda j: (0, j))],
        out_specs=pl.BlockSpec((rows, tn), lambda j: (0, j)),
        compiler_params=_cparams("parallel"),
        name="ada",
    )(cs, w, b)


def _proj_kernel(kinds, srcs, hb_ref, tb_ref, x_ref, sh_ref, sc_ref, wh_ref, wt_ref, wg_ref, bg_ref, *rest):
    if "rot" in kinds or "rot_scale" in kinds:
        cos_ref, sin_ref, o_ref, gt_ref, u_ref = rest
    else:
        o_ref, gt_ref, u_ref = rest
    j = pl.program_id(2)
    tm = x_ref.shape[1]
    sub = min(PROJ_SUB, tm)

    def rotary(acc, rows, scale):
        for s in range(acc.shape[1] // 128):
            a = acc[:, s * 128:(s + 1) * 128]
            half = s % 2
            cs = cos_ref[rows, half * 128:(half + 1) * 128]
            sn = sin_ref[rows, half * 128:(half + 1) * 128]
            r = a * cs + pltpu.roll(a, 64, 1) * sn
            if scale != 1.0:
                r = r * scale
            o_ref[0, rows, s * 128:(s + 1) * 128] = r.astype(BF16)

    def section(kind, first, src):
        w_ref = wh_ref if src == "h" else wt_ref
        for r in range(tm // sub):
            rows = slice(r * sub, (r + 1) * sub)
            if first:
                u = _layer_norm(x_ref[0, rows, :]) * (1.0 + sc_ref[0]) + sh_ref[0]
                ub = u.astype(BF16)
                u_ref[rows, :] = ub
                gt_ref[0, :, rows] = lax.dot_general(wg_ref[...], ub, NT_DIMS,
                                                     preferred_element_type=F32) + bg_ref[...]
            else:
                ub = u_ref[rows, :]
            acc = jnp.dot(ub, w_ref[...], preferred_element_type=F32)
            if kind == "rot":
                rotary(acc, rows, 1.0)
            elif kind == "rot_scale":
                rotary(acc, rows, KEY_SCALE)
            elif kind == "scale":
                o_ref[0, rows, :] = (acc * KEY_SCALE).astype(BF16)
            else:
                o_ref[0, rows, :] = acc.astype(BF16)

    variants = {}
    for s, key in enumerate(zip(kinds, srcs)):
        variants.setdefault(key + (s == 0,), []).append(s)
    for (kind, src, first), secs in variants.items():
        cond = functools.reduce(jnp.logical_or, [j == s for s in secs])

        @pl.when(cond)
        def _(kind=kind, first=first, src=src):
            section(kind, first, src)


def _proj(x, sh, sc, w_head, w_tail, sections, w_gate_t, b_gate, kinds, tables=None):
    B, T, D = x.shape
    n_sec = len(kinds)
    tm = min(PROJ_TM, T)
    tn = BRANCH_W
    assert T % tm == 0
    srcs = tuple(src for src, _ in sections)
    hb, tb, h_last, t_last = [], [], 0, 0
    for src, blk in sections:
        h_last, t_last = (blk, t_last) if src == "h" else (h_last, blk)
        hb.append(h_last)
        tb.append(t_last)
    in_specs = [
        pl.BlockSpec((1, tm, D), lambda i, b, j, hb, tb: (b, i, 0)),
        pl.BlockSpec((1, 1, D), lambda i, b, j, hb, tb: (b, 0, 0)),
        pl.BlockSpec((1, 1, D), lambda i, b, j, hb, tb: (b, 0, 0)),
        pl.BlockSpec((D, tn), lambda i, b, j, hb, tb: (0, hb[j])),
        pl.BlockSpec((D, tn), lambda i, b, j, hb, tb: (0, tb[j])),
        pl.BlockSpec((N_GATES, D), lambda i, b, j, hb, tb: (0, 0)),
        pl.BlockSpec((N_GATES, 1), lambda i, b, j, hb, tb: (0, 0)),
    ]
    args = [x, sh, sc, w_head, w_tail, w_gate_t, b_gate]
    if tables is not None:
        in_specs += [pl.BlockSpec((tm, HEAD_W), lambda i, b, j, hb, tb: (i, 0))] * 2
        args += list(tables)
    grid_spec = pltpu.PrefetchScalarGridSpec(
        num_scalar_prefetch=2,
        grid=(T // tm, B, n_sec),
        in_specs=in_specs,
        out_specs=(pl.BlockSpec((1, tm, tn), lambda i, b, j, hb, tb: (b, i, j)),
                   pl.BlockSpec((1, N_GATES, tm), lambda i, b, j, hb, tb: (b, 0, i))),
        scratch_shapes=[pltpu.VMEM((tm, D), BF16)],
    )
    return pl.pallas_call(
        functools.partial(_proj_kernel, kinds, srcs),
        out_shape=(jax.ShapeDtypeStruct((B, T, n_sec * tn), BF16),
                   jax.ShapeDtypeStruct((B, N_GATES, T), F32)),
        grid_spec=grid_spec,
        compiler_params=_cparams("parallel", "parallel", "arbitrary"),
        name="proj_lat" if tables is not None else "proj_ctx",
    )(jnp.asarray(hb, I32), jnp.asarray(tb, I32), *args)


def _ret_build(dl_ref, q_ref, k_ref, v_ref, rg_ref, ck_ref, cv_ref, o_ref,
               sf_ref, sb_ref, fs_ref, bs_ref, dec_ref, d_ref):
    h = pl.program_id(0)
    L = SCAN_L
    n_chunks = q_ref.shape[1] // L
    n_ctx_chunks = ck_ref.shape[1] // L
    lgf = _log_sigmoid(jnp.full((1, 1), dl_ref[0, h], F32))
    lgb = _log_sigmoid(jnp.full((1, 1), dl_ref[1, h], F32))

    @pl.when(pl.program_id(1) == 0)
    def _():
        ri = lax.broadcasted_iota(I32, (L, L), 0)
        ci = lax.broadcasted_iota(I32, (L, L), 1)
        rel = (ri - ci).astype(F32)
        d_ref[...] = jnp.where(rel >= 0.0, jnp.exp(jnp.maximum(rel, 0.0) * lgf),
                               jnp.exp(jnp.maximum(-rel, 0.0) * lgb))
        row = lax.broadcasted_iota(I32, (L, HEAD_W), 0).astype(F32)
        dec_ref[0] = jnp.exp((row + 1.0) * lgf)
        dec_ref[1] = jnp.exp((L - 1.0 - row) * lgf)
        dec_ref[2] = jnp.exp((L - row) * lgb)
        dec_ref[3] = jnp.exp(row * lgb)

    cdf = jnp.exp(L * lgf)
    cdb = jnp.exp(L * lgb)

    def update(s_ref, kc, vc, kd, cd):
        kdec = (kc.astype(F32) * kd).astype(BF16)
        s_ref[...] = s_ref[...] * cd + lax.dot_general(kdec, vc, TN_DIMS, preferred_element_type=F32)

    sf_ref[...] = jnp.zeros_like(sf_ref)
    sb_ref[...] = jnp.zeros_like(sb_ref)
    for c in range(n_ctx_chunks):
        update(sf_ref, ck_ref[0, c * L:(c + 1) * L, :], cv_ref[0, c * L:(c + 1) * L, :], dec_ref[1], cdf)
    for c in reversed(range(n_ctx_chunks)):
        update(sb_ref, ck_ref[0, c * L:(c + 1) * L, :], cv_ref[0, c * L:(c + 1) * L, :], dec_ref[3], cdb)

    def state_pass(i, carry):
        cb = n_chunks - 1 - i
        rf = pl.multiple_of(i * L, L)
        rb = pl.multiple_of(cb * L, L)
        fs_ref[i] = sf_ref[...].astype(BF16)
        bs_ref[cb] = sb_ref[...].astype(BF16)
        update(sf_ref, k_ref[0, pl.ds(rf, L), :], v_ref[0, pl.ds(rf, L), :], dec_ref[1], cdf)
        update(sb_ref, k_ref[0, pl.ds(rb, L), :], v_ref[0, pl.ds(rb, L), :], dec_ref[3], cdb)
        return carry

    def finish_states():
        fs_ref[n_chunks - 1] = sf_ref[...].astype(BF16)
        bs_ref[0] = sb_ref[...].astype(BF16)

    def out_chunk(c):
        r0 = pl.multiple_of(c * L, L)
        q = q_ref[0, pl.ds(r0, L), :]
        k = k_ref[0, pl.ds(r0, L), :]
        v = v_ref[0, pl.ds(r0, L), :]
        s = lax.dot_general(q, k, NT_DIMS, preferred_element_type=F32)
        att = (s * d_ref[...]).astype(BF16)
        o = jnp.dot(att, v, preferred_element_type=F32)
        o = o + jnp.dot(q, fs_ref[c], preferred_element_type=F32) * dec_ref[0]
        o = o + jnp.dot(q, bs_ref[c], preferred_element_type=F32) * dec_ref[2]
        rg = rg_ref[0, pl.ds(r0, L), :].astype(F32)
        o_ref[0, pl.ds(r0, L), :] = (_layer_norm(o) * _silu(rg)).astype(BF16)

    return state_pass, finish_states, out_chunk


def _ret_scratch(n_chunks):
    return [pltpu.VMEM((HEAD_W, HEAD_W), F32),
            pltpu.VMEM((HEAD_W, HEAD_W), F32),
            pltpu.VMEM((n_chunks, HEAD_W, HEAD_W), BF16),
            pltpu.VMEM((n_chunks, HEAD_W, HEAD_W), BF16),
            pltpu.VMEM((4, SCAN_L, HEAD_W), F32),
            pltpu.VMEM((SCAN_L, SCAN_L), F32)]


def _mlstm_build(qp_ref, kp_ref, v_ref, mo_ref, ckp_ref, cv_ref, gt_ref, cgt_ref,
                 wq_ref, bq_ref, wk_ref, bk_ref, o_ref,
                 tab_ref, row_ref,
                 cf_ref, mf_ref, cb_ref, mb_ref, cfs_ref, mfs_ref, cbs_ref, mbs_ref, mask_ref,
                 xk_ref, xq_ref, q_ref, k_ref, ck_ref):
    L = SCAN_L
    T = qp_ref.shape[1]
    Tc = ckp_ref.shape[1]
    n_chunks = T // L
    n_ctx_chunks = Tc // L
    CV = CONV_ROWS

    def conv_stage(src_ref, xs_ref, t_len):
        xs_ref[pl.ds(0, 8), :] = jnp.zeros((8, HEAD_W), F32)
        xs_ref[pl.ds(8 + t_len, 8), :] = jnp.zeros((8, HEAD_W), F32)
        xs_ref[pl.ds(8, t_len), :] = src_ref[0].astype(F32)

    def conv_rows(xs_ref, c, w_ref, b_ref, dst_ref, scale):
        w = w_ref[...]
        r0 = pl.multiple_of(c * CV, CV)
        win = xs_ref[pl.ds(r0, CV + 16), :]
        prev = pltpu.roll(win, 1, 0)[8:8 + CV, :]
        cur = win[8:8 + CV, :]
        nxt = pltpu.roll(win, CV + 15, 0)[8:8 + CV, :]
        y = _silu(prev * w[0:1, :] + cur * w[1:2, :] + nxt * w[2:3, :] + b_ref[...])
        if scale != 1.0:
            y = y * scale
        dst_ref[pl.ds(r0, CV), :] = y.astype(BF16)

    per_chunk = L // CV

    def conv_k(c):
        for u in range(per_chunk):
            conv_rows(xk_ref, c * per_chunk + u, wk_ref, bk_ref, k_ref, KEY_SCALE)

    def conv_q(c):
        for u in range(per_chunk):
            conv_rows(xq_ref, c * per_chunk + u, wq_ref, bq_ref, q_ref, 1.0)

    conv_stage(ckp_ref, xk_ref, Tc)
    for c in range(Tc // CV):
        conv_rows(xk_ref, c, wk_ref, bk_ref, ck_ref, KEY_SCALE)
    conv_stage(kp_ref, xk_ref, T)
    conv_stage(qp_ref, xq_ref, T)
    conv_k(0)
    conv_k(n_chunks - 1)
    n_k_steps = n_chunks // 2 - 1

    def conv_step(i, keys):
        if keys:
            conv_k(i + 1)
            conv_k(n_chunks - 2 - i)
        else:
            conv_q(2 * (i - n_k_steps))
            conv_q(2 * (i - n_k_steps) + 1)

    ri = lax.broadcasted_iota(I32, (L, L), 0)
    ci = lax.broadcasted_iota(I32, (L, L), 1)
    tri_u = (ri <= ci).astype(BF16)
    lane8 = lax.broadcasted_iota(I32, (8, L), 1)
    sub8 = lax.broadcasted_iota(I32, (8, L), 0)
    sel_r = lax.broadcasted_iota(I32, (24, 8 * 128), 0) % 8
    sel_c = lax.broadcasted_iota(I32, (24, 8 * 128), 1) // 128
    sel3 = (sel_r == sel_c).astype(BF16)
    ones_cols = jnp.ones((L, AUG_W - HEAD_W), BF16)

    def chunk_tables(g8, n_used, state_only):
        i_f, i_b = g8[0], g8[2]
        lf_f, lf_b = _log_sigmoid(g8[1]), _log_sigmoid(g8[3])
        cs3 = jnp.dot(_split3(jnp.concatenate([lf_f, lf_b], axis=0)), tri_u,
                      preferred_element_type=F32)
        cs = cs3[0:16] + cs3[16:32] + cs3[32:48]
        b_f = cs[0:8]
        b_b = cs[8:16, L - 1:L] - cs[8:16] + lf_b
        z_f = i_f - b_f
        z_b = i_b - b_b
        g_f = b_f[:, L - 1:L] - b_f + i_f
        g_b = b_b[:, 0:1] - b_b + i_b
        mf, mb = z_f, z_b
        s = 1
        while s < L:
            mf = jnp.maximum(mf, jnp.where(lane8 >= s, pltpu.roll(mf, s, 1), NEG_INF))
            mb = jnp.maximum(mb, jnp.where(lane8 < L - s, pltpu.roll(mb, L - s, 1), NEG_INF))
            s *= 2
        mb = jnp.where(lane8 < L - 1, pltpu.roll(mb, L - 1, 1), NEG_INF)
        reps = [None if state_only and t not in (2, 5) else
                lax.dot_general(_split3(val), sel3[:, 0:n_used * 128], TN_DIMS, preferred_element_type=F32)
                for t, val in enumerate((mf, b_f, g_f, mb, b_b, g_b))]

        def rows_of(c):
            out = jnp.zeros((8, L), F32)
            for r, val in enumerate((z_f, z_b, g_f, g_b, b_f, b_b)):
                out = jnp.where(sub8 == r, val[c:c + 1], out)
            return out

        return rows_of, reps

    lat_rows, lat_reps = chunk_tables(gt_ref[0, 0], n_chunks, False)
    for c in range(n_chunks):
        row_ref[c] = lat_rows(c)
        for t in range(N_TAB):
            tab_ref[t, c * L:(c + 1) * L, :] = lat_reps[t][:, c * 128:(c + 1) * 128]

    def lanes2(x):
        return jnp.concatenate([x, x], axis=1)

    def advance(k, v, g_rep, g_row, b_last, c_ref, m_ref):
        m = m_ref[...]
        m_new = jnp.maximum(b_last + m, jnp.max(g_row, axis=-1, keepdims=True))
        kw = (k.astype(F32) * jnp.exp(lanes2(g_rep) - m_new)).astype(BF16)
        v_aug = jnp.concatenate([v, ones_cols], axis=1)
        c_ref[...] = jnp.exp(b_last + m - m_new) * c_ref[...] + lax.dot_general(
            kw, v_aug, TN_DIMS, preferred_element_type=F32)
        m_ref[...] = m_new

    for r in (cf_ref, mf_ref, cb_ref, mb_ref):
        r[...] = jnp.zeros_like(r)
    ctx_rows, ctx_reps = chunk_tables(cgt_ref[0, 0], n_ctx_chunks, True)
    for c in range(n_ctx_chunks):
        rows = ctx_rows(c)
        advance(ck_ref[c * L:(c + 1) * L, :], cv_ref[0, c * L:(c + 1) * L, :],
                ctx_reps[2][:, c * 128:(c + 1) * 128], rows[2:3], rows[4:5, L - 1:L], cf_ref, mf_ref)
    for c in reversed(range(n_ctx_chunks)):
        rows = ctx_rows(c)
        advance(ck_ref[c * L:(c + 1) * L, :], cv_ref[0, c * L:(c + 1) * L, :],
                ctx_reps[5][:, c * 128:(c + 1) * 128], rows[3:4], rows[5:6, 0:1], cb_ref, mb_ref)

    def state_pass(i, carry, keys):
        cb = n_chunks - 1 - i
        rf = pl.multiple_of(i * L, L)
        rb = pl.multiple_of(cb * L, L)
        cfs_ref[i] = cf_ref[...].astype(BF16)
        mfs_ref[i] = mf_ref[...]
        cbs_ref[cb] = cb_ref[...].astype(BF16)
        mbs_ref[cb] = mb_ref[...]
        rows_f = row_ref[i]
        rows_b = row_ref[cb]
        advance(k_ref[pl.ds(rf, L), :], v_ref[0, pl.ds(rf, L), :], tab_ref[2, pl.ds(rf, L), :],
                rows_f[2:3], rows_f[4:5, L - 1:L], cf_ref, mf_ref)
        advance(k_ref[pl.ds(rb, L), :], v_ref[0, pl.ds(rb, L), :], tab_ref[5, pl.ds(rb, L), :],
                rows_b[3:4], rows_b[5:6, 0:1], cb_ref, mb_ref)
        conv_step(i, keys)
        return carry

    def finish_states():
        cfs_ref[n_chunks - 1] = cf_ref[...].astype(BF16)
        mfs_ref[n_chunks - 1] = mf_ref[...]
        cbs_ref[0] = cb_ref[...].astype(BF16)
        mbs_ref[0] = mb_ref[...]

    def direction(q, v_aug, s, z_row, zmax_rep, b_rep, mask, c_in, m_in):
        mx = jnp.maximum(zmax_rep, m_in)
        att = s * jnp.exp((z_row - lanes2(mx)) + mask)
        na = jnp.dot(att.astype(BF16), v_aug, preferred_element_type=F32)
        qa = jnp.dot(q, c_in, preferred_element_type=F32)
        a = jnp.exp(m_in - mx)
        num = na[:, 0:HEAD_W] + lanes2(a) * qa[:, 0:HEAD_W]
        den = na[:, HEAD_W:] + a * qa[:, HEAD_W:]
        scale = 1.0 / jnp.maximum(jnp.abs(den), jnp.exp(-(b_rep + mx)))
        return num * lanes2(scale)

    @pl.when(pl.program_id(1) == 0)
    def _():
        mask_ref[0] = jnp.where(ci <= ri, 0.0, NEG_INF)
        mask_ref[1] = jnp.where(ci > ri, 0.0, NEG_INF)

    def out_chunk(c):
        r0 = pl.multiple_of(c * L, L)
        q = q_ref[pl.ds(r0, L), :]
        k = k_ref[pl.ds(r0, L), :]
        v_aug = jnp.concatenate([v_ref[0, pl.ds(r0, L), :], ones_cols], axis=1)
        s = lax.dot_general(q, k, NT_DIMS, preferred_element_type=F32)
        rows = row_ref[c]
        tot = direction(q, v_aug, s, rows[0:1], tab_ref[0, pl.ds(r0, L), :], tab_ref[1, pl.ds(r0, L), :],
                        mask_ref[0], cfs_ref[c], mfs_ref[c])
        tot = tot + direction(q, v_aug, s, rows[1:2], tab_ref[3, pl.ds(r0, L), :],
                              tab_ref[4, pl.ds(r0, L), :], mask_ref[1], cbs_ref[c], mbs_ref[c])
        mo = mo_ref[0, pl.ds(r0, L), :].astype(F32)
        o_ref[0, pl.ds(r0, L), :] = (_layer_norm(tot) * jax.nn.sigmoid(mo)).astype(BF16)

    return state_pass, finish_states, out_chunk, n_k_steps


def _mlstm_scratch(T, Tc, n_chunks):
    state = [pltpu.VMEM((HEAD_W, AUG_W), F32), pltpu.VMEM((1, 1), F32)]
    snaps = [pltpu.VMEM((n_chunks, HEAD_W, AUG_W), BF16), pltpu.VMEM((n_chunks, 1, 1), F32)]
    return [pltpu.VMEM((N_TAB, T, 128), F32), pltpu.VMEM((n_chunks, 8, SCAN_L), F32)] \
        + state + state + snaps + snaps + [pltpu.VMEM((2, SCAN_L, SCAN_L), F32)] \
        + [pltpu.VMEM((T + 16, HEAD_W), F32), pltpu.VMEM((T + 16, HEAD_W), F32),
           pltpu.VMEM((T, HEAD_W), BF16), pltpu.VMEM((T, HEAD_W), BF16), pltpu.VMEM((Tc, HEAD_W), BF16)]


def _scan_kernel(n_ret_scratch, dl_ref, rq_ref, rk_ref, rv_ref, rg_ref, rck_ref, rcv_ref,
                 mq_ref, mk_ref, mv_ref, mo_ref, mck_ref, mcv_ref, gt_ref, cgt_ref,
                 wq_ref, bq_ref, wk_ref, bk_ref, r_ref, m_ref, *scratch):
    n_chunks = rq_ref.shape[1] // SCAN_L
    ret = _ret_build(dl_ref, rq_ref, rk_ref, rv_ref, rg_ref, rck_ref, rcv_ref, r_ref,
                     *scratch[:n_ret_scratch])
    mls = _mlstm_build(mq_ref, mk_ref, mv_ref, mo_ref, mck_ref, mcv_ref, gt_ref, cgt_ref,
                       wq_ref, bq_ref, wk_ref, bk_ref, m_ref, *scratch[n_ret_scratch:])

    def state_pass(i, carry, keys):
        ret[0](i, carry)
        mls[0](i, carry, keys)
        return carry

    n_k_steps = mls[3]
    lax.fori_loop(0, n_k_steps, functools.partial(state_pass, keys=True), 0)
    lax.fori_loop(n_k_steps, n_chunks - 1, functools.partial(state_pass, keys=False), 0)
    ret[1]()
    mls[1]()

    def out_pass(i, carry):
        for c in (2 * i, 2 * i + 1):
            ret[2](c)
            mls[2](c)
        return carry

    lax.fori_loop(0, n_chunks // 2, out_pass, 0)


def _scans(decay_logit, p_lat, p_ctx, gt, cgt, conv_w, conv_b, ret_lat, ret_ctx, ml_lat, ml_ctx):
    B, T, _ = p_lat.shape
    Tc = p_ctx.shape[1]
    assert T % (2 * SCAN_L) == 0 and Tc % SCAN_L == 0 and T // SCAN_L <= 8
    n_chunks = T // SCAN_L

    def lat(sec):
        return pl.BlockSpec((1, T, HEAD_W), lambda h, b: (b, 0, sec * HEADS + h))

    def cx(sec):
        return pl.BlockSpec((1, Tc, HEAD_W), lambda h, b: (b, 0, sec * HEADS + h))

    gates = pl.BlockSpec((1, 1, N_GK, 8, SCAN_L), lambda h, b: (b, h, 0, 0, 0))
    out = pl.BlockSpec((1, T, HEAD_W), lambda h, b: (b, 0, h))
    conv_specs = [pl.BlockSpec((3, HEAD_W), lambda h, b: (0, h)),
                  pl.BlockSpec((1, HEAD_W), lambda h, b: (0, h)),
                  pl.BlockSpec((3, HEAD_W), lambda h, b: (0, HEADS + h)),
                  pl.BlockSpec((1, HEAD_W), lambda h, b: (0, HEADS + h))]
    ret_scratch = _ret_scratch(n_chunks)
    return pl.pallas_call(
        functools.partial(_scan_kernel, len(ret_scratch)),
        out_shape=(jax.ShapeDtypeStruct((B, T, BRANCH_W), BF16),
                   jax.ShapeDtypeStruct((B, T, BRANCH_W), BF16)),
        grid=(HEADS, B),
        in_specs=[pl.BlockSpec(memory_space=pltpu.SMEM)]
        + [lat(s) for s in ret_lat] + [cx(s) for s in ret_ctx]
        + [lat(s) for s in ml_lat] + [cx(s) for s in ml_ctx] + [gates, gates] + conv_specs,
        out_specs=(out, out),
        scratch_shapes=ret_scratch + _mlstm_scratch(T, Tc, n_chunks),
        compiler_params=_cparams("arbitrary", "arbitrary"),
        name="scans",
    )(decay_logit, *([p_lat] * 4), *([p_ctx] * 2), *([p_lat] * 4), *([p_ctx] * 2), gt, cgt,
      conv_w, conv_b, conv_w, conv_b)


def _merge_kernel(alpha, r_ref, m_ref, gr_ref, gm_ref, x_ref, g1_ref, sh2_ref, sc2_ref,
                  lng_ref, lnb_ref, wr_ref, wm_ref, wo_ref, wrt_ref, brt_ref, e1_ref, e3_ref, e2_ref,
                  x1_ref, ua_ref, ub_ref, ri_ref, rw_ref, cnt_ref, e1b_ref, e3b_ref, e2b_ref,
                  carry_ref, u_ref):
    for src, dst in ((e1_ref, e1b_ref), (e3_ref, e3b_ref), (e2_ref, e2b_ref)):
        dst[...] = src[...].astype(BF16)

    @pl.when(jnp.logical_and(pl.program_id(0) == 0, pl.program_id(1) == 0))
    def _():
        carry_ref[...] = jnp.zeros_like(carry_ref)
        tm = x_ref.shape[1]
        r = lax.broadcasted_iota(I32, (tm, tm), 0)
        c = lax.broadcasted_iota(I32, (tm, tm), 1)
        u_ref[...] = (r < c).astype(BF16)

    yr = jnp.dot(r_ref[0], wr_ref[...], preferred_element_type=F32)
    ym = jnp.dot(m_ref[0], wm_ref[...], preferred_element_type=F32)
    y = jax.nn.sigmoid(gr_ref[0].astype(F32)) * yr + jax.nn.sigmoid(gm_ref[0].astype(F32)) * ym
    yo = jnp.dot(y.astype(BF16), wo_ref[...], preferred_element_type=F32)
    x1 = _layer_norm(alpha * x_ref[0] + g1_ref[0] * yo) * lng_ref[...] + lnb_ref[...]
    x1_ref[0] = x1
    u2 = _layer_norm(x1) * (1.0 + sc2_ref[0]) + sh2_ref[0]
    ua_ref[0] = _pack_pairs(u2[:, 0:PACK_W], u2[:, PACK_W:2 * PACK_W])
    ub_ref[0] = _pack_pairs(u2[:, 2 * PACK_W:3 * PACK_W], u2[:, 3 * PACK_W:4 * PACK_W])
    lt = lax.dot_general(wrt_ref[...], u2.astype(BF16), NT_DIMS, preferred_element_type=F32) + brt_ref[...]
    _route_tile(lt, ri_ref, rw_ref, cnt_ref, carry_ref, u_ref)


def _route_tile(lt, ri_ref, rw_ref, cnt_ref, carry_ref, u_ref):
    tm = lt.shape[1]
    lg = lt[0:N_GROUPS, :]
    eg = jnp.exp(lg - jnp.max(lg, axis=0, keepdims=True))
    pg = eg / jnp.sum(eg, axis=0, keepdims=True)
    pg_top = jnp.max(pg, axis=0, keepdims=True)
    rows_g = lax.broadcasted_iota(I32, pg.shape, 0)
    g_idx = jnp.min(jnp.where(pg == pg_top, rows_g, N_GROUPS), axis=0, keepdims=True)

    le = jnp.zeros((EXP_PER_GROUP, tm), F32)
    for g in range(N_GROUPS):
        lo = 8 + g * EXP_PER_GROUP
        le = jnp.where(g_idx == g, lt[lo:lo + EXP_PER_GROUP, :], le)
    ee = jnp.exp(le - jnp.max(le, axis=0, keepdims=True))
    pe = ee / jnp.sum(ee, axis=0, keepdims=True)
    rows_e = lax.broadcasted_iota(I32, pe.shape, 0)
    v1 = jnp.max(pe, axis=0, keepdims=True)
    i1 = jnp.min(jnp.where(pe == v1, rows_e, EXP_PER_GROUP), axis=0, keepdims=True)
    pe2 = jnp.where(rows_e == i1, -1.0, pe)
    v2 = jnp.max(pe2, axis=0, keepdims=True)
    i2 = jnp.min(jnp.where(pe2 == v2, rows_e, EXP_PER_GROUP), axis=0, keepdims=True)
    den = v1 + v2
    rw_ref[...] = jnp.zeros_like(rw_ref)
    rw_ref[0:1, :] = pg_top * v1 / den
    rw_ref[1:2, :] = pg_top * v2 / den
    e1 = g_idx * EXP_PER_GROUP + i1
    e2 = g_idx * EXP_PER_GROUP + i2

    rows_x = lax.broadcasted_iota(I32, (N_EXPERTS, tm), 0)
    oh1 = (rows_x == e1).astype(F32)
    oh2 = (rows_x == e2).astype(F32)
    both = oh1 + oh2
    before = carry_ref[:, 0:1] + jnp.dot(both.astype(BF16), u_ref[...], preferred_element_type=F32)
    ri_ref[0:1, :] = e1
    ri_ref[1:2, :] = e2
    ri_ref[2:3, :] = jnp.sum(oh1 * before, axis=0, keepdims=True).astype(I32)
    ri_ref[3:4, :] = jnp.sum(oh2 * before, axis=0, keepdims=True).astype(I32)
    carry_ref[...] = carry_ref[...] + jnp.sum(both, axis=1, keepdims=True)
    cnt_ref[...] = carry_ref[...].astype(I32)


def _merge(alpha, r, m, p_lat, sec_gates, x, g1, sh2, sc2, lng, lnb, wr, wm, wo, wrt, brt, expert_w):
    B, T, D = x.shape
    tm = MERGE_TM
    per_b = T // tm
    n = B * T
    n_steps = B * per_b
    sliced = [w.reshape(n_steps, w.shape[0] * w.shape[1] // n_steps, w.shape[2]) for w in expert_w]
    assert all(s.shape[1] % 16 == 0 and s.size == w.size for s, w in zip(sliced, expert_w))

    def slab(s):
        return pl.BlockSpec((1,) + s.shape[1:], lambda b, i: (b * per_b + i, 0, 0))

    def tile(w):
        return pl.BlockSpec((1, tm, w), lambda b, i: (b, i, 0))

    def sec(s):
        return pl.BlockSpec((1, tm, BRANCH_W), lambda b, i: (b, i, s))

    def mod():
        return pl.BlockSpec((1, 1, D), lambda b, i: (b, 0, 0))

    def const(shape):
        return pl.BlockSpec(shape, lambda b, i: (0,) * len(shape))

    outs = pl.pallas_call(
        functools.partial(_merge_kernel, alpha),
        out_shape=(jax.ShapeDtypeStruct((B, T, D), F32),
                   jax.ShapeDtypeStruct((B, T, PACK_W), U32),
                   jax.ShapeDtypeStruct((B, T, PACK_W), U32),
                   jax.ShapeDtypeStruct((4, n), I32),
                   jax.ShapeDtypeStruct((8, n), F32),
                   jax.ShapeDtypeStruct((N_EXPERTS, 128), I32))
        + tuple(jax.ShapeDtypeStruct(s.shape, BF16) for s in sliced),
        grid=(B, per_b),
        in_specs=[tile(BRANCH_W), tile(BRANCH_W), sec(sec_gates[0]), sec(sec_gates[1]), tile(D),
                  mod(), mod(), mod(), const((1, D)), const((1, D)),
                  const((BRANCH_W, D)), const((BRANCH_W, D)), const((D, D)),
                  const((ROUTE_ROWS, D)), const((ROUTE_ROWS, 1))] + [slab(s) for s in sliced],
        out_specs=(tile(D), tile(PACK_W), tile(PACK_W),
                   pl.BlockSpec((4, tm), lambda b, i: (0, b * per_b + i)),
                   pl.BlockSpec((8, tm), lambda b, i: (0, b * per_b + i)),
                   pl.BlockSpec((N_EXPERTS, 128), lambda b, i: (0, 0)))
        + tuple(slab(s) for s in sliced),
        scratch_shapes=[pltpu.VMEM((N_EXPERTS, 128), F32), pltpu.VMEM((tm, tm), BF16)],
        compiler_params=_cparams("arbitrary", "arbitrary"),
        name="merge",
    )(r, m, p_lat, p_lat, x, g1, sh2, sc2, lng, lnb, wr, wm, wo, wrt, brt, *sliced)
    return outs[:6] + tuple(o.reshape(w.shape) for o, w in zip(outs[6:], expert_w))


def _sc_mesh():
    return plsc.VectorSubcoreMesh(core_axis_name="c", subcore_axis_name="s")


def _sc_scatter2(rows_a, rows_b, idx0, idx1, n_out):
    m, w = rows_a.shape
    out = jax.ShapeDtypeStruct((n_out, w), rows_a.dtype)

    @functools.partial(pl.kernel, out_type=(out, out), mesh=_sc_mesh(), scratch_types=[])
    def k(xa_hbm, xb_hbm, i0_hbm, i1_hbm, oa_hbm, ob_hbm):
        for x_hbm, o_hbm in ((xa_hbm, oa_hbm), (xb_hbm, ob_hbm)):
            def body(x_vmem, i0_vmem, i1_vmem, o_hbm=o_hbm):
                pltpu.sync_copy(x_vmem, o_hbm.at[i0_vmem.at[0]])
                pltpu.sync_copy(x_vmem, o_hbm.at[i1_vmem.at[0]])

            pltpu.emit_pipeline(
                body,
                grid=(m // SC_WIN,),
                in_specs=[pl.BlockSpec((SC_WIN, w), lambda i: (i, 0)),
                          pl.BlockSpec((1, SC_WIN), lambda i: (0, i)),
                          pl.BlockSpec((1, SC_WIN), lambda i: (0, i))],
                out_specs=[],
                core_axis_name=("c", "s"),
                dimension_semantics=(pltpu.PARALLEL,),
            )(x_hbm, i0_hbm, i1_hbm)

    return k(rows_a, rows_b, idx0.reshape(1, m), idx1.reshape(1, m))


def _sc_gather(table_a, table_b, idx):
    m = idx.shape[0]
    w = table_a.shape[1]
    out = jax.ShapeDtypeStruct((m, w), table_a.dtype)

    @functools.partial(pl.kernel, out_type=(out, out), mesh=_sc_mesh(), scratch_types=[])
    def k(ta_hbm, tb_hbm, i_hbm, oa_hbm, ob_hbm):
        for t_hbm, o_hbm in ((ta_hbm, oa_hbm), (tb_hbm, ob_hbm)):
            def body(i_vmem, o_vmem, t_hbm=t_hbm):
                pltpu.sync_copy(t_hbm.at[i_vmem.at[0]], o_vmem)

            pltpu.emit_pipeline(
                body,
                grid=(m // SC_WIN,),
                in_specs=[pl.BlockSpec((1, SC_WIN), lambda i: (0, i))],
                out_specs=[pl.BlockSpec((SC_WIN, w), lambda i: (i, 0))],
                core_axis_name=("c", "s"),
                dimension_semantics=(pltpu.PARALLEL,),
            )(i_hbm, o_hbm)

    return k(table_a, table_b, idx.reshape(1, m))


def _expert_kernel(be_ref, nv_ref, xa_ref, xb_ref, w1_ref, w3_ref, w2_ref, ya_ref, yb_ref):
    j = pl.program_id(0)
    nv = nv_ref[j]

    @pl.when(nv > 0)
    def _():
        valid = lax.broadcasted_iota(I32, xa_ref.shape, 0) < nv
        zero = jnp.zeros(xa_ref.shape, U32)
        parts = _unpack_pairs(jnp.where(valid, xa_ref[...], zero)) + \
            _unpack_pairs(jnp.where(valid, xb_ref[...], zero))
        x = jnp.concatenate([p.astype(BF16) for p in parts], axis=1)
        h1 = jnp.dot(x, w1_ref[0], preferred_element_type=F32)
        h3 = jnp.dot(x, w3_ref[0], preferred_element_type=F32)
        y = jnp.dot((_silu(h1) * h3).astype(BF16), w2_ref[0], preferred_element_type=F32)
        ya_ref[...] = _pack_pairs(y[:, 0:PACK_W], y[:, PACK_W:2 * PACK_W])
        yb_ref[...] = _pack_pairs(y[:, 2 * PACK_W:3 * PACK_W], y[:, 3 * PACK_W:4 * PACK_W])

    @pl.when(nv == 0)
    def _():
        ya_ref[...] = jnp.zeros_like(ya_ref)
        yb_ref[...] = jnp.zeros_like(yb_ref)


def _experts(block_exp, n_valid, xa, xb, w1, w3, w2):
    n_slots = xa.shape[0]
    n_blocks = n_slots // MOE_BLK
    d, de = w1.shape[1], w1.shape[2]
    slot = pl.BlockSpec((MOE_BLK, PACK_W), lambda j, be, nv: (j, 0))
    grid_spec = pltpu.PrefetchScalarGridSpec(
        num_scalar_prefetch=2,
        grid=(n_blocks,),
        in_specs=[slot, slot,
                  pl.BlockSpec((1, d, de), lambda j, be, nv: (be[j], 0, 0)),
                  pl.BlockSpec((1, d, de), lambda j, be, nv: (be[j], 0, 0)),
                  pl.BlockSpec((1, de, d), lambda j, be, nv: (be[j], 0, 0))],
        out_specs=(slot, slot),
    )
    return pl.pallas_call(
        _expert_kernel,
        out_shape=(jax.ShapeDtypeStruct((n_slots, PACK_W), U32),
                   jax.ShapeDtypeStruct((n_slots, PACK_W), U32)),
        grid_spec=grid_spec,
        compiler_params=_cparams("parallel"),
        name="experts",
    )(block_exp, n_valid, xa, xb, w1, w3, w2)


def _final_kernel(alpha, x1_ref, a0_ref, b0_ref, a1_ref, b1_ref, w_ref, g2_ref, lng_ref, lnb_ref, o_ref):
    w = w_ref[...].T
    w0 = w[:, 0:1]
    w1 = w[:, 1:2]
    parts0 = _unpack_pairs(a0_ref[...]) + _unpack_pairs(b0_ref[...])
    parts1 = _unpack_pairs(a1_ref[...]) + _unpack_pairs(b1_ref[...])
    f = jnp.concatenate([w0 * p0 + w1 * p1 for p0, p1 in zip(parts0, parts1)], axis=1)
    o_ref[0] = _layer_norm(alpha * x1_ref[0] + g2_ref[0] * f) * lng_ref[...] + lnb_ref[...]


def _final(alpha, x1, ya, yb, w, g2, lng, lnb):
    B, T, D = x1.shape
    tm = min(FINAL_TM, T)
    per_b = T // tm
    n_tiles = B * per_b

    def rows(k):
        return pl.BlockSpec((tm, PACK_W), lambda b, i: (k * n_tiles + b * per_b + i, 0))

    return pl.pallas_call(
        functools.partial(_final_kernel, alpha),
        out_shape=jax.ShapeDtypeStruct((B, T, D), F32),
        grid=(B, per_b),
        in_specs=[pl.BlockSpec((1, tm, D), lambda b, i: (b, i, 0)),
                  rows(0), rows(0), rows(1), rows(1),
                  pl.BlockSpec((8, tm), lambda b, i: (0, b * per_b + i)),
                  pl.BlockSpec((1, 1, D), lambda b, i: (b, 0, 0)),
                  pl.BlockSpec((1, D), lambda b, i: (0, 0)),
                  pl.BlockSpec((1, D), lambda b, i: (0, 0))],
        out_specs=pl.BlockSpec((1, tm, D), lambda b, i: (b, i, 0)),
        compiler_params=_cparams("parallel", "parallel"),
        name="final",
    )(x1, ya, yb, ya, yb, w, g2, lng, lnb)


def _rotary_tables(T):
    quarter = HEAD_W // 4
    freqs = ROPE_BASE ** (-jnp.arange(quarter, dtype=F32) / quarter)
    t = jnp.arange(T)
    ang_r = (t // GRID_W).astype(F32)[:, None] * freqs[None, :]
    ang_c = (t % GRID_W).astype(F32)[:, None] * freqs[None, :]
    cos = jnp.concatenate([jnp.cos(ang_r)] * 2 + [jnp.cos(ang_c)] * 2, axis=1)
    sin = jnp.concatenate([-jnp.sin(ang_r), jnp.sin(ang_r), -jnp.sin(ang_c), jnp.sin(ang_c)], axis=1)
    return cos, sin


def _per_head_gates(gt):
    B, _, T = gt.shape
    n_chunks = T // SCAN_L
    gth = gt.reshape(B, N_GK, HEADS, n_chunks, SCAN_L).transpose(0, 2, 1, 3, 4)
    return jnp.pad(gth, ((0, 0), (0, 0), (0, 0), (0, 8 - n_chunks), (0, 0)))


def _table_lookup(table, idx):
    sel = idx[..., None] == jnp.arange(table.shape[0], dtype=idx.dtype)
    return jnp.sum(jnp.where(sel, table, 0), axis=-1)


def kernel(x, c, ctx, c_ctx, w_ada, b_ada, w_in, b_mgate, ml_conv_w, ml_conv_b, ret_decay_logit, w_ret_branch, w_ml_branch, w_out, ln1_g, ln1_b, w_rg, b_rg, w_re, b_re, w_e1, w_e3, w_e2, ln2_g, ln2_b):
    B, T, D = x.shape
    depth = w_ada.shape[0]
    assert depth == 1 and D == BRANCH_W and T % GRID_W == 0
    alpha = (2 * depth) ** 0.25
    n_tok = B * T

    n_rows = -(-(B + 1) // 8) * 8
    cs = jnp.zeros((n_rows, D), F32).at[:B].set(c).at[B].set(c_ctx)
    mod = _ada(cs, w_ada[0], b_ada[0][None, :])
    sh1, sc1, g1, sh2, sc2, g2 = [mod[:B, None, i * D:(i + 1) * D] for i in range(6)]
    csh1 = mod[B, 0 * D:1 * D].reshape(1, 1, D)
    csc1 = mod[B, 1 * D:2 * D].reshape(1, 1, D)

    w = w_in[0]
    g_lo = 8 * BRANCH_W
    w_gate_t = w[:, g_lo:g_lo + N_GATES].T.astype(BF16)
    b_gate = b_mgate[0][:, None]
    w_head = w[:, :g_lo].astype(BF16)
    w_tail = w[:, g_lo + N_GATES:].astype(BF16)
    sec_lat = tuple(("h", s) for s in range(8)) + (("t", 0), ("t", 1))
    sec_ctx = (("h", 1), ("h", 2), ("h", 5), ("h", 6))
    kinds_lat = ("rot", "rot_scale") + ("plain",) * 8
    kinds_ctx = ("scale", "plain", "plain", "plain")
    p_lat, gt_lat = _proj(x, sh1, sc1, w_head, w_tail, sec_lat, w_gate_t, b_gate, kinds_lat,
                          _rotary_tables(T))
    Tc = ctx.shape[1]
    p_ctx, gt_ctx = _proj(ctx.reshape(1, B * Tc, D), csh1, csc1, w_head, w_tail, sec_ctx, w_gate_t,
                          b_gate, kinds_ctx)
    p_ctx = p_ctx.reshape(B, Tc, -1)
    gt_ctx = gt_ctx.reshape(N_GATES, B, Tc).transpose(1, 0, 2)

    ret, mls = _scans(ret_decay_logit[0], p_lat, p_ctx, _per_head_gates(gt_lat), _per_head_gates(gt_ctx),
                      ml_conv_w[0], ml_conv_b[0][None, :], (0, 1, 2, 3), (0, 1), (4, 5, 6, 7), (2, 3))

    wrt = jnp.zeros((ROUTE_ROWS, D), F32).at[:N_GROUPS].set(w_rg[0].T).at[8:8 + N_EXPERTS].set(w_re[0].T)
    brt = jnp.zeros((ROUTE_ROWS, 1), F32).at[:N_GROUPS, 0].set(b_rg[0]).at[8:8 + N_EXPERTS, 0].set(b_re[0])
    x1, ua, ub, ri, rw, cnt, we1, we3, we2 = _merge(
        alpha, ret, mls, p_lat, (8, 9), x, g1, sh2, sc2, ln1_g[0][None, :], ln1_b[0][None, :],
        w_ret_branch[0].astype(BF16), w_ml_branch[0].astype(BF16), w_out[0].astype(BF16),
        wrt.astype(BF16), brt, (w_e1[0], w_e3[0], w_e2[0]))

    counts = cnt[:, 0]
    padded = (counts + MOE_BLK - 1) // MOE_BLK * MOE_BLK
    pad_end = jnp.cumsum(padded)
    pad_off = pad_end - padded
    dest = _table_lookup(pad_off, ri[0:2]) + ri[2:4]
    n_blocks = (2 * n_tok) // MOE_BLK + N_EXPERTS
    n_slots = n_blocks * MOE_BLK
    block_start = jnp.arange(n_blocks, dtype=I32) * MOE_BLK
    block_exp = jnp.minimum((block_start[:, None] >= pad_end[None, :]).sum(1), N_EXPERTS - 1).astype(I32)
    n_valid = jnp.clip(_table_lookup(counts, block_exp) - (block_start - _table_lookup(pad_off, block_exp)),
                       0, MOE_BLK).astype(I32)

    xa, xb = _sc_scatter2(ua.reshape(n_tok, PACK_W), ub.reshape(n_tok, PACK_W), dest[0], dest[1], n_slots)
    ya, yb = _experts(block_exp, n_valid, xa, xb, we1, we3, we2)
    ga, gb = _sc_gather(ya, yb, dest.reshape(2 * n_tok))
    return _final(alpha, x1, ga, gb, rw, g2, ln2_g[0][None, :], ln2_b[0][None, :])
```

```python
import functools

import jax
import jax.numpy as jnp
from jax import lax
from jax.experimental import pallas as pl
from jax.experimental.pallas import tpu as pltpu
from jax.experimental.pallas import tpu_sc as plsc

F32 = jnp.float32
BF16 = jnp.bfloat16
U32 = jnp.uint32
I32 = jnp.int32
HIGHEST = lax.Precision.HIGHEST

HEADS = 4
HEAD_W = 256
BRANCH_W = HEADS * HEAD_W
GRID_W = 64
ROPE_BASE = 10000.0
N_GATES = 16
N_GK = N_GATES // HEADS
N_GROUPS = 4
EXP_PER_GROUP = 8
N_EXPERTS = N_GROUPS * EXP_PER_GROUP
LN_EPS = 1e-5
NEG_INF = -1e30
KEY_SCALE = HEAD_W ** -0.5

SCAN_L = 256
CONV_ROWS = 128
PROJ_TM = 2048
PROJ_SUB = 256
MERGE_TM = 512
FINAL_TM = 1024
MOE_BLK = 512
SC_WIN = 128
PACK_W = 256
ROUTE_ROWS = 64
N_TAB = 6
AUG_W = HEAD_W + 128
VMEM_LIMIT = 48 * 1024 * 1024

NT_DIMS = (((1,), (1,)), ((), ()))
TN_DIMS = (((0,), (0,)), ((), ()))


def _cparams(*sem):
    return pltpu.CompilerParams(dimension_semantics=sem, vmem_limit_bytes=VMEM_LIMIT)


def _layer_norm(x):
    mu = jnp.mean(x, axis=-1, keepdims=True)
    xc = x - mu
    var = jnp.mean(xc * xc, axis=-1, keepdims=True)
    return xc * lax.rsqrt(var + LN_EPS)


def _log_sigmoid(x):
    return jnp.minimum(x, 0.0) - jnp.log1p(jnp.exp(-jnp.abs(x)))


def _silu(x):
    return x * jax.nn.sigmoid(x)


def _pack_pairs(hi, lo):
    hb = lax.bitcast_convert_type(hi.astype(BF16).astype(F32), U32)
    lb = lax.bitcast_convert_type(lo.astype(BF16).astype(F32), U32)
    return (hb & jnp.uint32(0xFFFF0000)) | (lb >> 16)


def _unpack_pairs(p):
    hi = lax.bitcast_convert_type(p & jnp.uint32(0xFFFF0000), F32)
    lo = lax.bitcast_convert_type(p << 16, F32)
    return hi, lo


def _split3(x):
    hi = x.astype(BF16).astype(F32)
    r1 = x - hi
    mid = r1.astype(BF16).astype(F32)
    lo = (r1 - mid).astype(BF16).astype(F32)
    return jnp.concatenate([hi, mid, lo], axis=0).astype(BF16)


def _ada_kernel(c_ref, w_ref, b_ref, o_ref):
    s = _silu(c_ref[...])
    o_ref[...] = jnp.dot(s, w_ref[...], precision=HIGHEST, preferred_element_type=F32) + b_ref[...]


def _ada(cs, w, b):
    rows, d = cs.shape
    cols = w.shape[1]
    tn = 1024
    return pl.pallas_call(
        _ada_kernel,
        out_shape=jax.ShapeDtypeStruct((rows, cols), F32),
        grid=(cols // tn,),
        in_specs=[pl.BlockSpec((rows, d), lambda j: (0, 0)),
                  pl.BlockSpec((d, tn), lambda j: (0, j)),
                  pl.BlockSpec((1, tn), lambda j: (0, j))],
        out_specs=pl.BlockSpec((rows, tn), lambda j: (0, j)),
        compiler_params=_cparams("parallel"),
        name="ada",
    )(cs, w, b)


def _proj_kernel(kinds, srcs, n_cast, hb_ref, tb_ref, x_ref, sh_ref, sc_ref, wh_ref, wt_ref, wg_ref,
                 bg_ref, *rest):
    cast_in, rest = rest[:n_cast], rest[n_cast:]
    if "rot" in kinds or "rot_scale" in kinds:
        cos_ref, sin_ref, rest = rest[0], rest[1], rest[2:]
    o_ref, gt_ref = rest[0], rest[1]
    cast_out, u_ref = rest[2:2 + n_cast], rest[2 + n_cast]
    for src, dst in zip(cast_in, cast_out):
        dst[...] = src[...].astype(BF16)
    j = pl.program_id(2)
    tm = x_ref.shape[1]
    sub = min(PROJ_SUB, tm)

    def rotary(acc, rows, scale):
        for s in range(acc.shape[1] // 128):
            a = acc[:, s * 128:(s + 1) * 128]
            half = s % 2
            cs = cos_ref[rows, half * 128:(half + 1) * 128]
            sn = sin_ref[rows, half * 128:(half + 1) * 128]
            r = a * cs + pltpu.roll(a, 64, 1) * sn
            if scale != 1.0:
                r = r * scale
            o_ref[0, rows, s * 128:(s + 1) * 128] = r.astype(BF16)

    def section(kind, first, src):
        w_ref = wh_ref if src == "h" else wt_ref
        for r in range(tm // sub):
            rows = slice(r * sub, (r + 1) * sub)
            if first:
                u = _layer_norm(x_ref[0, rows, :]) * (1.0 + sc_ref[0]) + sh_ref[0]
                ub = u.astype(BF16)
                u_ref[rows, :] = ub
                gt_ref[0, :, rows] = lax.dot_general(wg_ref[...], ub, NT_DIMS,
                                                     preferred_element_type=F32) + bg_ref[...]
            else:
                ub = u_ref[rows, :]
            acc = jnp.dot(ub, w_ref[...], preferred_element_type=F32)
            if kind == "rot":
                rotary(acc, rows, 1.0)
            elif kind == "rot_scale":
                rotary(acc, rows, KEY_SCALE)
            elif kind == "scale":
                o_ref[0, rows, :] = (acc * KEY_SCALE).astype(BF16)
            else:
                o_ref[0, rows, :] = acc.astype(BF16)

    variants = {}
    for s, key in enumerate(zip(kinds, srcs)):
        variants.setdefault(key + (s == 0,), []).append(s)
    for (kind, src, first), secs in variants.items():
        cond = functools.reduce(jnp.logical_or, [j == s for s in secs])

        @pl.when(cond)
        def _(kind=kind, first=first, src=src):
            section(kind, first, src)


def _proj(x, sh, sc, w_head, w_tail, sections, w_gate_t, b_gate, kinds, tables=None, side_cast=()):
    B, T, D = x.shape
    n_sec = len(kinds)
    tm = min(PROJ_TM, T)
    tn = BRANCH_W
    assert T % tm == 0
    n_steps = (T // tm) * B * n_sec
    slabs = [a.reshape(n_steps, a.shape[0] // n_steps, a.shape[1]) for a in side_cast]
    assert all(s.shape[1] % 16 == 0 and s.size == a.size for s, a in zip(slabs, side_cast))

    def slab(s):
        return pl.BlockSpec((1,) + s.shape[1:],
                            lambda i, b, j, hb, tb: ((i * B + b) * n_sec + j, 0, 0))

    srcs = tuple(src for src, _ in sections)
    hb, tb, h_last, t_last = [], [], 0, 0
    for src, blk in sections:
        h_last, t_last = (blk, t_last) if src == "h" else (h_last, blk)
        hb.append(h_last)
        tb.append(t_last)
    in_specs = [
        pl.BlockSpec((1, tm, D), lambda i, b, j, hb, tb: (b, i, 0)),
        pl.BlockSpec((1, 1, D), lambda i, b, j, hb, tb: (b, 0, 0)),
        pl.BlockSpec((1, 1, D), lambda i, b, j, hb, tb: (b, 0, 0)),
        pl.BlockSpec((D, tn), lambda i, b, j, hb, tb: (0, hb[j])),
        pl.BlockSpec((D, tn), lambda i, b, j, hb, tb: (0, tb[j])),
        pl.BlockSpec((N_GATES, D), lambda i, b, j, hb, tb: (0, 0)),
        pl.BlockSpec((N_GATES, 1), lambda i, b, j, hb, tb: (0, 0)),
    ]
    args = [x, sh, sc, w_head, w_tail, w_gate_t, b_gate] + slabs
    in_specs += [slab(s) for s in slabs]
    if tables is not None:
        in_specs += [pl.BlockSpec((tm, HEAD_W), lambda i, b, j, hb, tb: (i, 0))] * 2
        args += list(tables)
    grid_spec = pltpu.PrefetchScalarGridSpec(
        num_scalar_prefetch=2,
        grid=(T // tm, B, n_sec),
        in_specs=in_specs,
        out_specs=(pl.BlockSpec((1, tm, tn), lambda i, b, j, hb, tb: (b, i, j)),
                   pl.BlockSpec((1, N_GATES, tm), lambda i, b, j, hb, tb: (b, 0, i)))
        + tuple(slab(s) for s in slabs),
        scratch_shapes=[pltpu.VMEM((tm, D), BF16)],
    )
    outs = pl.pallas_call(
        functools.partial(_proj_kernel, kinds, srcs, len(slabs)),
        out_shape=(jax.ShapeDtypeStruct((B, T, n_sec * tn), BF16),
                   jax.ShapeDtypeStruct((B, N_GATES, T), F32))
        + tuple(jax.ShapeDtypeStruct(s.shape, BF16) for s in slabs),
        grid_spec=grid_spec,
        compiler_params=_cparams("parallel", "parallel", "arbitrary"),
        name="proj_lat" if tables is not None else "proj_ctx",
    )(jnp.asarray(hb, I32), jnp.asarray(tb, I32), *args)
    return outs[0], outs[1], [o.reshape(a.shape) for o, a in zip(outs[2:], side_cast)]


def _ret_build(dl_ref, q_ref, k_ref, v_ref, rg_ref, ck_ref, cv_ref, o_ref,
               sf_ref, sb_ref, fs_ref, bs_ref, dec_ref, d_ref):
    h = pl.program_id(0)
    L = SCAN_L
    n_chunks = q_ref.shape[1] // L
    n_ctx_chunks = ck_ref.shape[1] // L
    lgf = _log_sigmoid(jnp.full((1, 1), dl_ref[0, h], F32))
    lgb = _log_sigmoid(jnp.full((1, 1), dl_ref[1, h], F32))

    @pl.when(pl.program_id(1) == 0)
    def _():
        ri = lax.broadcasted_iota(I32, (L, L), 0)
        ci = lax.broadcasted_iota(I32, (L, L), 1)
        rel = (ri - ci).astype(F32)
        d_ref[...] = jnp.where(rel >= 0.0, jnp.exp(jnp.maximum(rel, 0.0) * lgf),
                               jnp.exp(jnp.maximum(-rel, 0.0) * lgb))
        row = lax.broadcasted_iota(I32, (L, HEAD_W), 0).astype(F32)
        dec_ref[0] = jnp.exp((row + 1.0) * lgf)
        dec_ref[1] = jnp.exp((L - 1.0 - row) * lgf)
        dec_ref[2] = jnp.exp((L - row) * lgb)
        dec_ref[3] = jnp.exp(row * lgb)

    cdf = jnp.exp(L * lgf)
    cdb = jnp.exp(L * lgb)

    def update(s_ref, kc, vc, kd, cd):
        kdec = (kc.astype(F32) * kd).astype(BF16)
        s_ref[...] = s_ref[...] * cd + lax.dot_general(kdec, vc, TN_DIMS, preferred_element_type=F32)

    sf_ref[...] = jnp.zeros_like(sf_ref)
    sb_ref[...] = jnp.zeros_like(sb_ref)
    for c in range(n_ctx_chunks):
        update(sf_ref, ck_ref[0, c * L:(c + 1) * L, :], cv_ref[0, c * L:(c + 1) * L, :], dec_ref[1], cdf)
    for c in reversed(range(n_ctx_chunks)):
        update(sb_ref, ck_ref[0, c * L:(c + 1) * L, :], cv_ref[0, c * L:(c + 1) * L, :], dec_ref[3], cdb)

    def state_pass(i, carry):
        cb = n_chunks - 1 - i
        rf = pl.multiple_of(i * L, L)
        rb = pl.multiple_of(cb * L, L)
        fs_ref[i] = sf_ref[...].astype(BF16)
        bs_ref[cb] = sb_ref[...].astype(BF16)
        update(sf_ref, k_ref[0, pl.ds(rf, L), :], v_ref[0, pl.ds(rf, L), :], dec_ref[1], cdf)
        update(sb_ref, k_ref[0, pl.ds(rb, L), :], v_ref[0, pl.ds(rb, L), :], dec_ref[3], cdb)
        return carry

    def finish_states():
        fs_ref[n_chunks - 1] = sf_ref[...].astype(BF16)
        bs_ref[0] = sb_ref[...].astype(BF16)

    def out_chunk(c):
        r0 = pl.multiple_of(c * L, L)
        q = q_ref[0, pl.ds(r0, L), :]
        k = k_ref[0, pl.ds(r0, L), :]
        v = v_ref[0, pl.ds(r0, L), :]
        s = lax.dot_general(q, k, NT_DIMS, preferred_element_type=F32)
        att = (s * d_ref[...]).astype(BF16)
        o = jnp.dot(att, v, preferred_element_type=F32)
        o = o + jnp.dot(q, fs_ref[c], preferred_element_type=F32) * dec_ref[0]
        o = o + jnp.dot(q, bs_ref[c], preferred_element_type=F32) * dec_ref[2]
        rg = rg_ref[0, pl.ds(r0, L), :].astype(F32)
        o_ref[0, pl.ds(r0, L), :] = (_layer_norm(o) * _silu(rg)).astype(BF16)

    return state_pass, finish_states, out_chunk


def _ret_scratch(n_chunks):
    return [pltpu.VMEM((HEAD_W, HEAD_W), F32),
            pltpu.VMEM((HEAD_W, HEAD_W), F32),
            pltpu.VMEM((n_chunks, HEAD_W, HEAD_W), BF16),
            pltpu.VMEM((n_chunks, HEAD_W, HEAD_W), BF16),
            pltpu.VMEM((4, SCAN_L, HEAD_W), F32),
            pltpu.VMEM((SCAN_L, SCAN_L), F32)]


def _mlstm_build(qp_ref, kp_ref, v_ref, mo_ref, ckp_ref, cv_ref, gt_ref, cgt_ref,
                 wq_ref, bq_ref, wk_ref, bk_ref, o_ref,
                 tab_ref, row_ref,
                 cf_ref, mf_ref, cb_ref, mb_ref, cfs_ref, mfs_ref, cbs_ref, mbs_ref, mask_ref,
                 xk_ref, xq_ref, q_ref, k_ref, ck_ref):
    L = SCAN_L
    T = qp_ref.shape[1]
    Tc = ckp_ref.shape[1]
    n_chunks = T // L
    n_ctx_chunks = Tc // L
    CV = CONV_ROWS

    def conv_stage(src_ref, xs_ref, t_len):
        xs_ref[pl.ds(0, 8), :] = jnp.zeros((8, HEAD_W), F32)
        xs_ref[pl.ds(8 + t_len, 8), :] = jnp.zeros((8, HEAD_W), F32)
        xs_ref[pl.ds(8, t_len), :] = src_ref[0].astype(F32)

    def conv_rows(xs_ref, c, w_ref, b_ref, dst_ref, scale):
        w = w_ref[...]
        r0 = pl.multiple_of(c * CV, CV)
        win = xs_ref[pl.ds(r0, CV + 16), :]
        prev = pltpu.roll(win, 1, 0)[8:8 + CV, :]
        cur = win[8:8 + CV, :]
        nxt = pltpu.roll(win, CV + 15, 0)[8:8 + CV, :]
        y = _silu(prev * w[0:1, :] + cur * w[1:2, :] + nxt * w[2:3, :] + b_ref[...])
        if scale != 1.0:
            y = y * scale
        dst_ref[pl.ds(r0, CV), :] = y.astype(BF16)

    per_chunk = L // CV

    def conv_k(c):
        for u in range(per_chunk):
            conv_rows(xk_ref, c * per_chunk + u, wk_ref, bk_ref, k_ref, KEY_SCALE)

    def conv_q(c):
        for u in range(per_chunk):
            conv_rows(xq_ref, c * per_chunk + u, wq_ref, bq_ref, q_ref, 1.0)

    conv_stage(ckp_ref, xk_ref, Tc)
    for c in range(Tc // CV):
        conv_rows(xk_ref, c, wk_ref, bk_ref, ck_ref, KEY_SCALE)
    conv_stage(kp_ref, xk_ref, T)
    conv_stage(qp_ref, xq_ref, T)
    conv_k(0)
    conv_k(n_chunks - 1)
    n_k_steps = n_chunks // 2 - 1

    def conv_step(i, keys):
        if keys:
            conv_k(i + 1)
            conv_k(n_chunks - 2 - i)
        else:
            conv_q(2 * (i - n_k_steps))
            conv_q(2 * (i - n_k_steps) + 1)

    ri = lax.broadcasted_iota(I32, (L, L), 0)
    ci = lax.broadcasted_iota(I32, (L, L), 1)
    tri_u = (ri <= ci).astype(BF16)
    lane8 = lax.broadcasted_iota(I32, (8, L), 1)
    sub8 = lax.broadcasted_iota(I32, (8, L), 0)
    sel_r = lax.broadcasted_iota(I32, (24, 8 * 128), 0) % 8
    sel_c = lax.broadcasted_iota(I32, (24, 8 * 128), 1) // 128
    sel3 = (sel_r == sel_c).astype(BF16)
    ones_cols = jnp.ones((L, AUG_W - HEAD_W), BF16)

    def chunk_tables(g8, n_used, state_only):
        i_f, i_b = g8[0], g8[2]
        lf_f, lf_b = _log_sigmoid(g8[1]), _log_sigmoid(g8[3])
        cs3 = jnp.dot(_split3(jnp.concatenate([lf_f, lf_b], axis=0)), tri_u,
                      preferred_element_type=F32)
        cs = cs3[0:16] + cs3[16:32] + cs3[32:48]
        b_f = cs[0:8]
        b_b = cs[8:16, L - 1:L] - cs[8:16] + lf_b
        z_f = i_f - b_f
        z_b = i_b - b_b
        g_f = b_f[:, L - 1:L] - b_f + i_f
        g_b = b_b[:, 0:1] - b_b + i_b
        mf, mb = z_f, z_b
        s = 1
        while s < L:
            mf = jnp.maximum(mf, jnp.where(lane8 >= s, pltpu.roll(mf, s, 1), NEG_INF))
            mb = jnp.maximum(mb, jnp.where(lane8 < L - s, pltpu.roll(mb, L - s, 1), NEG_INF))
            s *= 2
        mb = jnp.where(lane8 < L - 1, pltpu.roll(mb, L - 1, 1), NEG_INF)
        reps = [None if state_only and t not in (2, 5) else
                lax.dot_general(_split3(val), sel3[:, 0:n_used * 128], TN_DIMS, preferred_element_type=F32)
                for t, val in enumerate((mf, b_f, g_f, mb, b_b, g_b))]

        def rows_of(c):
            out = jnp.zeros((8, L), F32)
            for r, val in enumerate((z_f, z_b, g_f, g_b, b_f, b_b)):
                out = jnp.where(sub8 == r, val[c:c + 1], out)
            return out

        return rows_of, reps

    lat_rows, lat_reps = chunk_tables(gt_ref[0, 0], n_chunks, False)
    for c in range(n_chunks):
        row_ref[c] = lat_rows(c)
        for t in range(N_TAB):
            tab_ref[t, c * L:(c + 1) * L, :] = lat_reps[t][:, c * 128:(c + 1) * 128]

    def lanes2(x):
        return jnp.concatenate([x, x], axis=1)

    def advance(k, v, g_rep, g_row, b_last, c_ref, m_ref):
        m = m_ref[...]
        m_new = jnp.maximum(b_last + m, jnp.max(g_row, axis=-1, keepdims=True))
        kw = (k.astype(F32) * jnp.exp(lanes2(g_rep) - m_new)).astype(BF16)
        v_aug = jnp.concatenate([v, ones_cols], axis=1)
        c_ref[...] = jnp.exp(b_last + m - m_new) * c_ref[...] + lax.dot_general(
            kw, v_aug, TN_DIMS, preferred_element_type=F32)
        m_ref[...] = m_new

    for r in (cf_ref, mf_ref, cb_ref, mb_ref):
        r[...] = jnp.zeros_like(r)
    ctx_rows, ctx_reps = chunk_tables(cgt_ref[0, 0], n_ctx_chunks, True)
    for c in range(n_ctx_chunks):
        rows = ctx_rows(c)
        advance(ck_ref[c * L:(c + 1) * L, :], cv_ref[0, c * L:(c + 1) * L, :],
                ctx_reps[2][:, c * 128:(c + 1) * 128], rows[2:3], rows[4:5, L - 1:L], cf_ref, mf_ref)
    for c in reversed(range(n_ctx_chunks)):
        rows = ctx_rows(c)
        advance(ck_ref[c * L:(c + 1) * L, :], cv_ref[0, c * L:(c + 1) * L, :],
                ctx_reps[5][:, c * 128:(c + 1) * 128], rows[3:4], rows[5:6, 0:1], cb_ref, mb_ref)

    def state_pass(i, carry, keys):
        cb = n_chunks - 1 - i
        rf = pl.multiple_of(i * L, L)
        rb = pl.multiple_of(cb * L, L)
        cfs_ref[i] = cf_ref[...].astype(BF16)
        mfs_ref[i] = mf_ref[...]
        cbs_ref[cb] = cb_ref[...].astype(BF16)
        mbs_ref[cb] = mb_ref[...]
        rows_f = row_ref[i]
        rows_b = row_ref[cb]
        advance(k_ref[pl.ds(rf, L), :], v_ref[0, pl.ds(rf, L), :], tab_ref[2, pl.ds(rf, L), :],
                rows_f[2:3], rows_f[4:5, L - 1:L], cf_ref, mf_ref)
        advance(k_ref[pl.ds(rb, L), :], v_ref[0, pl.ds(rb, L), :], tab_ref[5, pl.ds(rb, L), :],
                rows_b[3:4], rows_b[5:6, 0:1], cb_ref, mb_ref)
        conv_step(i, keys)
        return carry

    def finish_states():
        cfs_ref[n_chunks - 1] = cf_ref[...].astype(BF16)
        mfs_ref[n_chunks - 1] = mf_ref[...]
        cbs_ref[0] = cb_ref[...].astype(BF16)
        mbs_ref[0] = mb_ref[...]

    def direction(q, v_aug, s, z_row, zmax_rep, b_rep, mask, c_in, m_in):
        mx = jnp.maximum(zmax_rep, m_in)
        att = s * jnp.exp((z_row - lanes2(mx)) + mask)
        na = jnp.dot(att.astype(BF16), v_aug, preferred_element_type=F32)
        qa = jnp.dot(q, c_in, preferred_element_type=F32)
        a = jnp.exp(m_in - mx)
        num = na[:, 0:HEAD_W] + lanes2(a) * qa[:, 0:HEAD_W]
        den = na[:, HEAD_W:] + a * qa[:, HEAD_W:]
        scale = 1.0 / jnp.maximum(jnp.abs(den), jnp.exp(-(b_rep + mx)))
        return num * lanes2(scale)

    @pl.when(pl.program_id(1) == 0)
    def _():
        mask_ref[0] = jnp.where(ci <= ri, 0.0, NEG_INF)
        mask_ref[1] = jnp.where(ci > ri, 0.0, NEG_INF)

    def out_chunk(c):
        r0 = pl.multiple_of(c * L, L)
        q = q_ref[pl.ds(r0, L), :]
        k = k_ref[pl.ds(r0, L), :]
        v_aug = jnp.concatenate([v_ref[0, pl.ds(r0, L), :], ones_cols], axis=1)
        s = lax.dot_general(q, k, NT_DIMS, preferred_element_type=F32)
        rows = row_ref[c]
        tot = direction(q, v_aug, s, rows[0:1], tab_ref[0, pl.ds(r0, L), :], tab_ref[1, pl.ds(r0, L), :],
                        mask_ref[0], cfs_ref[c], mfs_ref[c])
        tot = tot + direction(q, v_aug, s, rows[1:2], tab_ref[3, pl.ds(r0, L), :],
                              tab_ref[4, pl.ds(r0, L), :], mask_ref[1], cbs_ref[c], mbs_ref[c])
        mo = mo_ref[0, pl.ds(r0, L), :].astype(F32)
        o_ref[0, pl.ds(r0, L), :] = (_layer_norm(tot) * jax.nn.sigmoid(mo)).astype(BF16)

    return state_pass, finish_states, out_chunk, n_k_steps


def _mlstm_scratch(T, Tc, n_chunks):
    state = [pltpu.VMEM((HEAD_W, AUG_W), F32), pltpu.VMEM((1, 1), F32)]
    snaps = [pltpu.VMEM((n_chunks, HEAD_W, AUG_W), BF16), pltpu.VMEM((n_chunks, 1, 1), F32)]
    return [pltpu.VMEM((N_TAB, T, 128), F32), pltpu.VMEM((n_chunks, 8, SCAN_L), F32)] \
        + state + state + snaps + snaps + [pltpu.VMEM((2, SCAN_L, SCAN_L), F32)] \
        + [pltpu.VMEM((T + 16, HEAD_W), F32), pltpu.VMEM((T + 16, HEAD_W), F32),
           pltpu.VMEM((T, HEAD_W), BF16), pltpu.VMEM((T, HEAD_W), BF16), pltpu.VMEM((Tc, HEAD_W), BF16)]


def _scan_kernel(n_ret_scratch, dl_ref, rq_ref, rk_ref, rv_ref, rg_ref, rck_ref, rcv_ref,
                 mq_ref, mk_ref, mv_ref, mo_ref, mck_ref, mcv_ref, gt_ref, cgt_ref,
                 wq_ref, bq_ref, wk_ref, bk_ref, r_ref, m_ref, *scratch):
    n_chunks = rq_ref.shape[1] // SCAN_L
    ret = _ret_build(dl_ref, rq_ref, rk_ref, rv_ref, rg_ref, rck_ref, rcv_ref, r_ref,
                     *scratch[:n_ret_scratch])
    mls = _mlstm_build(mq_ref, mk_ref, mv_ref, mo_ref, mck_ref, mcv_ref, gt_ref, cgt_ref,
                       wq_ref, bq_ref, wk_ref, bk_ref, m_ref, *scratch[n_ret_scratch:])

    def state_pass(i, carry, keys):
        ret[0](i, carry)
        mls[0](i, carry, keys)
        return carry

    n_k_steps = mls[3]
    lax.fori_loop(0, n_k_steps, functools.partial(state_pass, keys=True), 0)
    lax.fori_loop(n_k_steps, n_chunks - 1, functools.partial(state_pass, keys=False), 0)
    ret[1]()
    mls[1]()

    def out_pass(i, carry):
        for c in (2 * i, 2 * i + 1):
            ret[2](c)
            mls[2](c)
        return carry

    lax.fori_loop(0, n_chunks // 2, out_pass, 0)


def _scans(decay_logit, p_lat, p_ctx, gt, cgt, conv_w, conv_b, ret_lat, ret_ctx, ml_lat, ml_ctx):
    B, T, _ = p_lat.shape
    Tc = p_ctx.shape[1]
    assert T % (2 * SCAN_L) == 0 and Tc % SCAN_L == 0 and T // SCAN_L <= 8
    n_chunks = T // SCAN_L

    def lat(sec):
        return pl.BlockSpec((1, T, HEAD_W), lambda h, b: (b, 0, sec * HEADS + h))

    def cx(sec):
        return pl.BlockSpec((1, Tc, HEAD_W), lambda h, b: (b, 0, sec * HEADS + h))

    gates = pl.BlockSpec((1, 1, N_GK, 8, SCAN_L), lambda h, b: (b, h, 0, 0, 0))
    out = pl.BlockSpec((1, T, HEAD_W), lambda h, b: (b, 0, h))
    conv_specs = [pl.BlockSpec((3, HEAD_W), lambda h, b: (0, h)),
                  pl.BlockSpec((1, HEAD_W), lambda h, b: (0, h)),
                  pl.BlockSpec((3, HEAD_W), lambda h, b: (0, HEADS + h)),
                  pl.BlockSpec((1, HEAD_W), lambda h, b: (0, HEADS + h))]
    ret_scratch = _ret_scratch(n_chunks)
    return pl.pallas_call(
        functools.partial(_scan_kernel, len(ret_scratch)),
        out_shape=(jax.ShapeDtypeStruct((B, T, BRANCH_W), BF16),
                   jax.ShapeDtypeStruct((B, T, BRANCH_W), BF16)),
        grid=(HEADS, B),
        in_specs=[pl.BlockSpec(memory_space=pltpu.SMEM)]
        + [lat(s) for s in ret_lat] + [cx(s) for s in ret_ctx]
        + [lat(s) for s in ml_lat] + [cx(s) for s in ml_ctx] + [gates, gates] + conv_specs,
        out_specs=(out, out),
        scratch_shapes=ret_scratch + _mlstm_scratch(T, Tc, n_chunks),
        compiler_params=_cparams("arbitrary", "arbitrary"),
        name="scans",
    )(decay_logit, *([p_lat] * 4), *([p_ctx] * 2), *([p_lat] * 4), *([p_ctx] * 2), gt, cgt,
      conv_w, conv_b, conv_w, conv_b)


def _merge_kernel(alpha, r_ref, m_ref, gr_ref, gm_ref, x_ref, g1_ref, sh2_ref, sc2_ref,
                  lng_ref, lnb_ref, wr_ref, wm_ref, wo_ref, wrt_ref, brt_ref, e1_ref, e3_ref, e2_ref,
                  x1_ref, ua_ref, ub_ref, ri_ref, rw_ref, cnt_ref, e1b_ref, e3b_ref, e2b_ref,
                  carry_ref, u_ref):
    for src, dst in ((e1_ref, e1b_ref), (e3_ref, e3b_ref), (e2_ref, e2b_ref)):
        dst[...] = src[...].astype(BF16)

    @pl.when(jnp.logical_and(pl.program_id(0) == 0, pl.program_id(1) == 0))
    def _():
        carry_ref[...] = jnp.zeros_like(carry_ref)
        tm = x_ref.shape[1]
        r = lax.broadcasted_iota(I32, (tm, tm), 0)
        c = lax.broadcasted_iota(I32, (tm, tm), 1)
        u_ref[...] = (r < c).astype(BF16)

    yr = jnp.dot(r_ref[0], wr_ref[...], preferred_element_type=F32)
    ym = jnp.dot(m_ref[0], wm_ref[...], preferred_element_type=F32)
    y = jax.nn.sigmoid(gr_ref[0].astype(F32)) * yr + jax.nn.sigmoid(gm_ref[0].astype(F32)) * ym
    yo = jnp.dot(y.astype(BF16), wo_ref[...], preferred_element_type=F32)
    x1 = _layer_norm(alpha * x_ref[0] + g1_ref[0] * yo) * lng_ref[...] + lnb_ref[...]
    x1_ref[0] = x1
    u2 = _layer_norm(x1) * (1.0 + sc2_ref[0]) + sh2_ref[0]
    ua_ref[0] = _pack_pairs(u2[:, 0:PACK_W], u2[:, PACK_W:2 * PACK_W])
    ub_ref[0] = _pack_pairs(u2[:, 2 * PACK_W:3 * PACK_W], u2[:, 3 * PACK_W:4 * PACK_W])
    lt = lax.dot_general(wrt_ref[...], u2.astype(BF16), NT_DIMS, preferred_element_type=F32) + brt_ref[...]
    _route_tile(lt, ri_ref, rw_ref, cnt_ref, carry_ref, u_ref)


def _route_tile(lt, ri_ref, rw_ref, cnt_ref, carry_ref, u_ref):
    tm = lt.shape[1]
    lg = lt[0:N_GROUPS, :]
    eg = jnp.exp(lg - jnp.max(lg, axis=0, keepdims=True))
    pg = eg / jnp.sum(eg, axis=0, keepdims=True)
    pg_top = jnp.max(pg, axis=0, keepdims=True)
    rows_g = lax.broadcasted_iota(I32, pg.shape, 0)
    g_idx = jnp.min(jnp.where(pg == pg_top, rows_g, N_GROUPS), axis=0, keepdims=True)

    le = jnp.zeros((EXP_PER_GROUP, tm), F32)
    for g in range(N_GROUPS):
        lo = 8 + g * EXP_PER_GROUP
        le = jnp.where(g_idx == g, lt[lo:lo + EXP_PER_GROUP, :], le)
    ee = jnp.exp(le - jnp.max(le, axis=0, keepdims=True))
    pe = ee / jnp.sum(ee, axis=0, keepdims=True)
    rows_e = lax.broadcasted_iota(I32, pe.shape, 0)
    v1 = jnp.max(pe, axis=0, keepdims=True)
    i1 = jnp.min(jnp.where(pe == v1, rows_e, EXP_PER_GROUP), axis=0, keepdims=True)
    pe2 = jnp.where(rows_e == i1, -1.0, pe)
    v2 = jnp.max(pe2, axis=0, keepdims=True)
    i2 = jnp.min(jnp.where(pe2 == v2, rows_e, EXP_PER_GROUP), axis=0, keepdims=True)
    den = v1 + v2
    rw_ref[...] = jnp.zeros_like(rw_ref)
    rw_ref[0:1, :] = pg_top * v1 / den
    rw_ref[1:2, :] = pg_top * v2 / den
    e1 = g_idx * EXP_PER_GROUP + i1
    e2 = g_idx * EXP_PER_GROUP + i2

    rows_x = lax.broadcasted_iota(I32, (N_EXPERTS, tm), 0)
    oh1 = (rows_x == e1).astype(F32)
    oh2 = (rows_x == e2).astype(F32)
    both = oh1 + oh2
    before = carry_ref[:, 0:1] + jnp.dot(both.astype(BF16), u_ref[...], preferred_element_type=F32)
    ri_ref[0:1, :] = e1
    ri_ref[1:2, :] = e2
    ri_ref[2:3, :] = jnp.sum(oh1 * before, axis=0, keepdims=True).astype(I32)
    ri_ref[3:4, :] = jnp.sum(oh2 * before, axis=0, keepdims=True).astype(I32)
    carry_ref[...] = carry_ref[...] + jnp.sum(both, axis=1, keepdims=True)
    cnt_ref[...] = carry_ref[...].astype(I32)


def _merge(alpha, r, m, p_lat, sec_gates, x, g1, sh2, sc2, lng, lnb, wr, wm, wo, wrt, brt, expert_w):
    B, T, D = x.shape
    tm = MERGE_TM
    per_b = T // tm
    n = B * T
    n_steps = B * per_b
    sliced = [w.reshape(n_steps, w.shape[0] * w.shape[1] // n_steps, w.shape[2]) for w in expert_w]
    assert all(s.shape[1] % 16 == 0 and s.size == w.size for s, w in zip(sliced, expert_w))

    def slab(s):
        return pl.BlockSpec((1,) + s.shape[1:], lambda b, i: (b * per_b + i, 0, 0))

    def tile(w):
        return pl.BlockSpec((1, tm, w), lambda b, i: (b, i, 0))

    def sec(s):
        return pl.BlockSpec((1, tm, BRANCH_W), lambda b, i: (b, i, s))

    def mod():
        return pl.BlockSpec((1, 1, D), lambda b, i: (b, 0, 0))

    def const(shape):
        return pl.BlockSpec(shape, lambda b, i: (0,) * len(shape))

    outs = pl.pallas_call(
        functools.partial(_merge_kernel, alpha),
        out_shape=(jax.ShapeDtypeStruct((B, T, D), F32),
                   jax.ShapeDtypeStruct((B, T, PACK_W), U32),
                   jax.ShapeDtypeStruct((B, T, PACK_W), U32),
                   jax.ShapeDtypeStruct((4, n), I32),
                   jax.ShapeDtypeStruct((8, n), F32),
                   jax.ShapeDtypeStruct((N_EXPERTS, 128), I32))
        + tuple(jax.ShapeDtypeStruct(s.shape, BF16) for s in sliced),
        grid=(B, per_b),
        in_specs=[tile(BRANCH_W), tile(BRANCH_W), sec(sec_gates[0]), sec(sec_gates[1]), tile(D),
                  mod(), mod(), mod(), const((1, D)), const((1, D)),
                  const((BRANCH_W, D)), const((BRANCH_W, D)), const((D, D)),
                  const((ROUTE_ROWS, D)), const((ROUTE_ROWS, 1))] + [slab(s) for s in sliced],
        out_specs=(tile(D), tile(PACK_W), tile(PACK_W),
                   pl.BlockSpec((4, tm), lambda b, i: (0, b * per_b + i)),
                   pl.BlockSpec((8, tm), lambda b, i: (0, b * per_b + i)),
                   pl.BlockSpec((N_EXPERTS, 128), lambda b, i: (0, 0)))
        + tuple(slab(s) for s in sliced),
        scratch_shapes=[pltpu.VMEM((N_EXPERTS, 128), F32), pltpu.VMEM((tm, tm), BF16)],
        compiler_params=_cparams("arbitrary", "arbitrary"),
        name="merge",
    )(r, m, p_lat, p_lat, x, g1, sh2, sc2, lng, lnb, wr, wm, wo, wrt, brt, *sliced)
    return outs[:6] + tuple(o.reshape(w.shape) for o, w in zip(outs[6:], expert_w))


def _sc_mesh():
    return plsc.VectorSubcoreMesh(core_axis_name="c", subcore_axis_name="s")


def _sc_scatter2(rows_a, rows_b, idx0, idx1, n_out):
    m, w = rows_a.shape
    out = jax.ShapeDtypeStruct((n_out, w), rows_a.dtype)

    @functools.partial(pl.kernel, out_type=(out, out), mesh=_sc_mesh(), scratch_types=[])
    def k(xa_hbm, xb_hbm, i0_hbm, i1_hbm, oa_hbm, ob_hbm):
        for x_hbm, o_hbm in ((xa_hbm, oa_hbm), (xb_hbm, ob_hbm)):
            def body(x_vmem, i0_vmem, i1_vmem, o_hbm=o_hbm):
                pltpu.sync_copy(x_vmem, o_hbm.at[i0_vmem.at[0]])
                pltpu.sync_copy(x_vmem, o_hbm.at[i1_vmem.at[0]])

            pltpu.emit_pipeline(
                body,
                grid=(m // SC_WIN,),
                in_specs=[pl.BlockSpec((SC_WIN, w), lambda i: (i, 0)),
                          pl.BlockSpec((1, SC_WIN), lambda i: (0, i)),
                          pl.BlockSpec((1, SC_WIN), lambda i: (0, i))],
                out_specs=[],
                core_axis_name=("c", "s"),
                dimension_semantics=(pltpu.PARALLEL,),
            )(x_hbm, i0_hbm, i1_hbm)

    return k(rows_a, rows_b, idx0.reshape(1, m), idx1.reshape(1, m))


def _sc_gather(table_a, table_b, idx):
    m = idx.shape[0]
    w = table_a.shape[1]
    out = jax.ShapeDtypeStruct((m, w), table_a.dtype)

    @functools.partial(pl.kernel, out_type=(out, out), mesh=_sc_mesh(), scratch_types=[])
    def k(ta_hbm, tb_hbm, i_hbm, oa_hbm, ob_hbm):
        for t_hbm, o_hbm in ((ta_hbm, oa_hbm), (tb_hbm, ob_hbm)):
            def body(i_vmem, o_vmem, t_hbm=t_hbm):
                pltpu.sync_copy(t_hbm.at[i_vmem.at[0]], o_vmem)

            pltpu.emit_pipeline(
                body,
                grid=(m // SC_WIN,),
                in_specs=[pl.BlockSpec((1, SC_WIN), lambda i: (0, i))],
                out_specs=[pl.BlockSpec((SC_WIN, w), lambda i: (i, 0))],
                core_axis_name=("c", "s"),
                dimension_semantics=(pltpu.PARALLEL,),
            )(i_hbm, o_hbm)

    return k(table_a, table_b, idx.reshape(1, m))


def _expert_kernel(be_ref, nv_ref, xa_ref, xb_ref, w1_ref, w3_ref, w2_ref, ya_ref, yb_ref):
    j = pl.program_id(0)
    nv = nv_ref[j]

    @pl.when(nv > 0)
    def _():
        valid = lax.broadcasted_iota(I32, xa_ref.shape, 0) < nv
        zero = jnp.zeros(xa_ref.shape, U32)
        parts = _unpack_pairs(jnp.where(valid, xa_ref[...], zero)) + \
            _unpack_pairs(jnp.where(valid, xb_ref[...], zero))
        x = jnp.concatenate([p.astype(BF16) for p in parts], axis=1)
        h1 = jnp.dot(x, w1_ref[0], preferred_element_type=F32)
        h3 = jnp.dot(x, w3_ref[0], preferred_element_type=F32)
        y = jnp.dot((_silu(h1) * h3).astype(BF16), w2_ref[0], preferred_element_type=F32)
        ya_ref[...] = _pack_pairs(y[:, 0:PACK_W], y[:, PACK_W:2 * PACK_W])
        yb_ref[...] = _pack_pairs(y[:, 2 * PACK_W:3 * PACK_W], y[:, 3 * PACK_W:4 * PACK_W])

    @pl.when(nv == 0)
    def _():
        ya_ref[...] = jnp.zeros_like(ya_ref)
        yb_ref[...] = jnp.zeros_like(yb_ref)


def _experts(block_exp, n_valid, xa, xb, w1, w3, w2):
    n_slots = xa.shape[0]
    n_blocks = n_slots // MOE_BLK
    d, de = w1.shape[1], w1.shape[2]
    slot = pl.BlockSpec((MOE_BLK, PACK_W), lambda j, be, nv: (j, 0))
    grid_spec = pltpu.PrefetchScalarGridSpec(
        num_scalar_prefetch=2,
        grid=(n_blocks,),
        in_specs=[slot, slot,
                  pl.BlockSpec((1, d, de), lambda j, be, nv: (be[j], 0, 0)),
                  pl.BlockSpec((1, d, de), lambda j, be, nv: (be[j], 0, 0)),
                  pl.BlockSpec((1, de, d), lambda j, be, nv: (be[j], 0, 0))],
        out_specs=(slot, slot),
    )
    return pl.pallas_call(
        _expert_kernel,
        out_shape=(jax.ShapeDtypeStruct((n_slots, PACK_W), U32),
                   jax.ShapeDtypeStruct((n_slots, PACK_W), U32)),
        grid_spec=grid_spec,
        compiler_params=_cparams("parallel"),
        name="experts",
    )(block_exp, n_valid, xa, xb, w1, w3, w2)


def _final_kernel(alpha, x1_ref, a0_ref, b0_ref, a1_ref, b1_ref, w_ref, g2_ref, lng_ref, lnb_ref, o_ref):
    w = w_ref[...].T
    w0 = w[:, 0:1]
    w1 = w[:, 1:2]
    parts0 = _unpack_pairs(a0_ref[...]) + _unpack_pairs(b0_ref[...])
    parts1 = _unpack_pairs(a1_ref[...]) + _unpack_pairs(b1_ref[...])
    f = jnp.concatenate([w0 * p0 + w1 * p1 for p0, p1 in zip(parts0, parts1)], axis=1)
    o_ref[0] = _layer_norm(alpha * x1_ref[0] + g2_ref[0] * f) * lng_ref[...] + lnb_ref[...]


def _final(alpha, x1, ya, yb, w, g2, lng, lnb):
    B, T, D = x1.shape
    tm = min(FINAL_TM, T)
    per_b = T // tm
    n_tiles = B * per_b

    def rows(k):
        return pl.BlockSpec((tm, PACK_W), lambda b, i: (k * n_tiles + b * per_b + i, 0))

    return pl.pallas_call(
        functools.partial(_final_kernel, alpha),
        out_shape=jax.ShapeDtypeStruct((B, T, D), F32),
        grid=(B, per_b),
        in_specs=[pl.BlockSpec((1, tm, D), lambda b, i: (b, i, 0)),
                  rows(0), rows(0), rows(1), rows(1),
                  pl.BlockSpec((8, tm), lambda b, i: (0, b * per_b + i)),
                  pl.BlockSpec((1, 1, D), lambda b, i: (b, 0, 0)),
                  pl.BlockSpec((1, D), lambda b, i: (0, 0)),
                  pl.BlockSpec((1, D), lambda b, i: (0, 0))],
        out_specs=pl.BlockSpec((1, tm, D), lambda b, i: (b, i, 0)),
        compiler_params=_cparams("parallel", "parallel"),
        name="final",
    )(x1, ya, yb, ya, yb, w, g2, lng, lnb)


def _rotary_tables(T):
    quarter = HEAD_W // 4
    freqs = ROPE_BASE ** (-jnp.arange(quarter, dtype=F32) / quarter)
    t = jnp.arange(T)
    ang_r = (t // GRID_W).astype(F32)[:, None] * freqs[None, :]
    ang_c = (t % GRID_W).astype(F32)[:, None] * freqs[None, :]
    cos = jnp.concatenate([jnp.cos(ang_r)] * 2 + [jnp.cos(ang_c)] * 2, axis=1)
    sin = jnp.concatenate([-jnp.sin(ang_r), jnp.sin(ang_r), -jnp.sin(ang_c), jnp.sin(ang_c)], axis=1)
    return cos, sin


def _per_head_gates(gt):
    B, _, T = gt.shape
    n_chunks = T // SCAN_L
    gth = gt.reshape(B, N_GK, HEADS, n_chunks, SCAN_L).transpose(0, 2, 1, 3, 4)
    return jnp.pad(gth, ((0, 0), (0, 0), (0, 0), (0, 8 - n_chunks), (0, 0)))


def _table_lookup(table, idx):
    sel = idx[..., None] == jnp.arange(table.shape[0], dtype=idx.dtype)
    return jnp.sum(jnp.where(sel, table, 0), axis=-1)


def kernel(x, c, ctx, c_ctx, w_ada, b_ada, w_in, b_mgate, ml_conv_w, ml_conv_b, ret_decay_logit, w_ret_branch, w_ml_branch, w_out, ln1_g, ln1_b, w_rg, b_rg, w_re, b_re, w_e1, w_e3, w_e2, ln2_g, ln2_b):
    B, T, D = x.shape
    depth = w_ada.shape[0]
    assert depth == 1 and D == BRANCH_W and T % GRID_W == 0
    alpha = (2 * depth) ** 0.25
    n_tok = B * T

    n_rows = -(-(B + 1) // 8) * 8
    cs = jnp.zeros((n_rows, D), F32).at[:B].set(c).at[B].set(c_ctx)
    mod = _ada(cs, w_ada[0], b_ada[0][None, :])
    sh1, sc1, g1, sh2, sc2, g2 = [mod[:B, None, i * D:(i + 1) * D] for i in range(6)]
    csh1 = mod[B, 0 * D:1 * D].reshape(1, 1, D)
    csc1 = mod[B, 1 * D:2 * D].reshape(1, 1, D)

    w = w_in[0]
    g_lo = 8 * BRANCH_W
    w_gate_t = w[:, g_lo:g_lo + N_GATES].T.astype(BF16)
    b_gate = b_mgate[0][:, None]
    w_head = w[:, :g_lo].astype(BF16)
    w_tail = w[:, g_lo + N_GATES:].astype(BF16)
    sec_lat = tuple(("h", s) for s in range(8)) + (("t", 0), ("t", 1))
    sec_ctx = (("h", 1), ("h", 2), ("h", 5), ("h", 6))
    kinds_lat = ("rot", "rot_scale") + ("plain",) * 8
    kinds_ctx = ("scale", "plain", "plain", "plain")
    p_lat, gt_lat, _ = _proj(x, sh1, sc1, w_head, w_tail, sec_lat, w_gate_t, b_gate, kinds_lat,
                             _rotary_tables(T))
    Tc = ctx.shape[1]
    p_ctx, gt_ctx, (w_rb, w_mb, w_ob) = _proj(
        ctx.reshape(1, B * Tc, D), csh1, csc1, w_head, w_tail, sec_ctx, w_gate_t, b_gate, kinds_ctx,
        side_cast=(w_ret_branch[0], w_ml_branch[0], w_out[0]))
    p_ctx = p_ctx.reshape(B, Tc, -1)
    gt_ctx = gt_ctx.reshape(N_GATES, B, Tc).transpose(1, 0, 2)

    ret, mls = _scans(ret_decay_logit[0], p_lat, p_ctx, _per_head_gates(gt_lat), _per_head_gates(gt_ctx),
                      ml_conv_w[0], ml_conv_b[0][None, :], (0, 1, 2, 3), (0, 1), (4, 5, 6, 7), (2, 3))

    wrt = jnp.zeros((ROUTE_ROWS, D), F32).at[:N_GROUPS].set(w_rg[0].T).at[8:8 + N_EXPERTS].set(w_re[0].T)
    brt = jnp.zeros((ROUTE_ROWS, 1), F32).at[:N_GROUPS, 0].set(b_rg[0]).at[8:8 + N_EXPERTS, 0].set(b_re[0])
    x1, ua, ub, ri, rw, cnt, we1, we3, we2 = _merge(
        alpha, ret, mls, p_lat, (8, 9), x, g1, sh2, sc2, ln1_g[0][None, :], ln1_b[0][None, :],
        w_rb, w_mb, w_ob, wrt.astype(BF16), brt, (w_e1[0], w_e3[0], w_e2[0]))

    counts = cnt[:, 0]
    padded = (counts + MOE_BLK - 1) // MOE_BLK * MOE_BLK
    pad_end = jnp.cumsum(padded)
    pad_off = pad_end - padded
    dest = _table_lookup(pad_off, ri[0:2]) + ri[2:4]
    n_blocks = (2 * n_tok) // MOE_BLK + N_EXPERTS
    n_slots = n_blocks * MOE_BLK
    block_start = jnp.arange(n_blocks, dtype=I32) * MOE_BLK
    block_exp = jnp.minimum((block_start[:, None] >= pad_end[None, :]).sum(1), N_EXPERTS - 1).astype(I32)
    n_valid = jnp.clip(_table_lookup(counts, block_exp) - (block_start - _table_lookup(pad_off, block_exp)),
                       0, MOE_BLK).astype(I32)

    xa, xb = _sc_scatter2(ua.reshape(n_tok, PACK_W), ub.reshape(n_tok, PACK_W), dest[0], dest[1], n_slots)
    ya, yb = _experts(block_exp, n_valid, xa, xb, we1, we3, we2)
    ga, gb = _sc_gather(ya, yb, dest.reshape(2 * n_tok))
    return _final(alpha, x1, ga, gb, rw, g2, ln2_g[0][None, :], ln2_b[0][None, :])
```

```python
import functools

import jax
import jax.numpy as jnp
from jax import lax
from jax.experimental import pallas as pl
from jax.experimental.pallas import tpu as pltpu
from jax.experimental.pallas import tpu_sc as plsc

F32 = jnp.float32
BF16 = jnp.bfloat16
U32 = jnp.uint32
I32 = jnp.int32
HIGHEST = lax.Precision.HIGHEST

HEADS = 4
HEAD_W = 256
BRANCH_W = HEADS * HEAD_W
GRID_W = 64
ROPE_BASE = 10000.0
N_GATES = 16
N_GK = N_GATES // HEADS
N_GROUPS = 4
EXP_PER_GROUP = 8
N_EXPERTS = N_GROUPS * EXP_PER_GROUP
LN_EPS = 1e-5
NEG_INF = -1e30
KEY_SCALE = HEAD_W ** -0.5

SCAN_L = 256
CONV_ROWS = 128
PROJ_TM = 2048
PROJ_SUB = 256
MERGE_TM = 512
FINAL_TM = 1024
MOE_BLK = 512
SC_WIN = 128
PACK_W = 256
ROUTE_ROWS = 64
N_TAB = 6
AUG_W = HEAD_W + 128
VMEM_LIMIT = 48 * 1024 * 1024

NT_DIMS = (((1,), (1,)), ((), ()))
TN_DIMS = (((0,), (0,)), ((), ()))


def _cparams(*sem):
    return pltpu.CompilerParams(dimension_semantics=sem, vmem_limit_bytes=VMEM_LIMIT)


def _layer_norm(x):
    mu = jnp.mean(x, axis=-1, keepdims=True)
    xc = x - mu
    var = jnp.mean(xc * xc, axis=-1, keepdims=True)
    return xc * lax.rsqrt(var + LN_EPS)


def _log_sigmoid(x):
    return jnp.minimum(x, 0.0) - jnp.log1p(jnp.exp(-jnp.abs(x)))


def _silu(x):
    return x * jax.nn.sigmoid(x)


def _pack_pairs(hi, lo):
    hb = lax.bitcast_convert_type(hi.astype(BF16).astype(F32), U32)
    lb = lax.bitcast_convert_type(lo.astype(BF16).astype(F32), U32)
    return (hb & jnp.uint32(0xFFFF0000)) | (lb >> 16)


def _unpack_pairs(p):
    hi = lax.bitcast_convert_type(p & jnp.uint32(0xFFFF0000), F32)
    lo = lax.bitcast_convert_type(p << 16, F32)
    return hi, lo


def _split3(x):
    hi = x.astype(BF16).astype(F32)
    r1 = x - hi
    mid = r1.astype(BF16).astype(F32)
    lo = (r1 - mid).astype(BF16).astype(F32)
    return jnp.concatenate([hi, mid, lo], axis=0).astype(BF16)


def _ada_kernel(c_ref, w_ref, b_ref, o_ref):
    s = _silu(c_ref[...])
    o_ref[...] = jnp.dot(s, w_ref[...], precision=HIGHEST, preferred_element_type=F32) + b_ref[...]


def _ada(cs, w, b):
    rows, d = cs.shape
    cols = w.shape[1]
    tn = 1024
    return pl.pallas_call(
        _ada_kernel,
        out_shape=jax.ShapeDtypeStruct((rows, cols), F32),
        grid=(cols // tn,),
        in_specs=[pl.BlockSpec((rows, d), lambda j: (0, 0)),
                  pl.BlockSpec((d, tn), lambda j: (0, j)),
                  pl.BlockSpec((1, tn), lambda j: (0, j))],
        out_specs=pl.BlockSpec((rows, tn), lambda j: (0, j)),
        compiler_params=_cparams("parallel"),
        name="ada",
    )(cs, w, b)


def _proj_kernel(kinds, srcs, n_cast, hb_ref, tb_ref, x_ref, sh_ref, sc_ref, wh_ref, wt_ref, wg_ref,
                 bg_ref, *rest):
    cast_in, rest = rest[:n_cast], rest[n_cast:]
    if "rot" in kinds or "rot_scale" in kinds:
        cos_ref, sin_ref, rest = rest[0], rest[1], rest[2:]
    o_ref, gt_ref = rest[0], rest[1]
    cast_out, u_ref = rest[2:2 + n_cast], rest[2 + n_cast]
    for src, dst in zip(cast_in, cast_out):
        dst[...] = src[...].astype(BF16)
    j = pl.program_id(2)
    tm = x_ref.shape[1]
    sub = min(PROJ_SUB, tm)

    def rotary(acc, rows, scale):
        for s in range(acc.shape[1] // 128):
            a = acc[:, s * 128:(s + 1) * 128]
            half = s % 2
            cs = cos_ref[rows, half * 128:(half + 1) * 128]
            sn = sin_ref[rows, half * 128:(half + 1) * 128]
            r = a * cs + pltpu.roll(a, 64, 1) * sn
            if scale != 1.0:
                r = r * scale
            o_ref[0, rows, s * 128:(s + 1) * 128] = r.astype(BF16)

    def section(kind, first, src):
        w_ref = wh_ref if src == "h" else wt_ref
        for r in range(tm // sub):
            rows = slice(r * sub, (r + 1) * sub)
            if first:
                u = _layer_norm(x_ref[0, rows, :]) * (1.0 + sc_ref[0]) + sh_ref[0]
                ub = u.astype(BF16)
                u_ref[rows, :] = ub
                gt_ref[0, :, rows] = lax.dot_general(wg_ref[...], ub, NT_DIMS,
                                                     preferred_element_type=F32) + bg_ref[...]
            else:
                ub = u_ref[rows, :]
            acc = jnp.dot(ub, w_ref[...], preferred_element_type=F32)
            if kind == "rot":
                rotary(acc, rows, 1.0)
            elif kind == "rot_scale":
                rotary(acc, rows, KEY_SCALE)
            elif kind == "scale":
                o_ref[0, rows, :] = (acc * KEY_SCALE).astype(BF16)
            else:
                o_ref[0, rows, :] = acc.astype(BF16)

    variants = {}
    for s, key in enumerate(zip(kinds, srcs)):
        variants.setdefault(key + (s == 0,), []).append(s)
    for (kind, src, first), secs in variants.items():
        cond = functools.reduce(jnp.logical_or, [j == s for s in secs])

        @pl.when(cond)
        def _(kind=kind, first=first, src=src):
            section(kind, first, src)


def _proj(x, sh, sc, w_head, w_tail, sections, w_gate_t, b_gate, kinds, tables=None, side_cast=()):
    B, T, D = x.shape
    n_sec = len(kinds)
    tm = min(PROJ_TM, T)
    tn = BRANCH_W
    assert T % tm == 0
    n_steps = (T // tm) * B * n_sec
    slabs = [a.reshape(n_steps, a.shape[0] // n_steps, a.shape[1]) for a in side_cast]
    assert all(s.shape[1] % 16 == 0 and s.size == a.size for s, a in zip(slabs, side_cast))

    def slab(s):
        return pl.BlockSpec((1,) + s.shape[1:],
                            lambda i, b, j, hb, tb: ((i * B + b) * n_sec + j, 0, 0))

    srcs = tuple(src for src, _ in sections)
    hb, tb, h_last, t_last = [], [], 0, 0
    for src, blk in sections:
        h_last, t_last = (blk, t_last) if src == "h" else (h_last, blk)
        hb.append(h_last)
        tb.append(t_last)
    in_specs = [
        pl.BlockSpec((1, tm, D), lambda i, b, j, hb, tb: (b, i, 0)),
        pl.BlockSpec((1, 1, D), lambda i, b, j, hb, tb: (b, 0, 0)),
        pl.BlockSpec((1, 1, D), lambda i, b, j, hb, tb: (b, 0, 0)),
        pl.BlockSpec((D, tn), lambda i, b, j, hb, tb: (0, hb[j])),
        pl.BlockSpec((D, tn), lambda i, b, j, hb, tb: (0, tb[j])),
        pl.BlockSpec((N_GATES, D), lambda i, b, j, hb, tb: (0, 0)),
        pl.BlockSpec((N_GATES, 1), lambda i, b, j, hb, tb: (0, 0)),
    ]
    args = [x, sh, sc, w_head, w_tail, w_gate_t, b_gate] + slabs
    in_specs += [slab(s) for s in slabs]
    if tables is not None:
        in_specs += [pl.BlockSpec((tm, HEAD_W), lambda i, b, j, hb, tb: (i, 0))] * 2
        args += list(tables)
    grid_spec = pltpu.PrefetchScalarGridSpec(
        num_scalar_prefetch=2,
        grid=(T // tm, B, n_sec),
        in_specs=in_specs,
        out_specs=(pl.BlockSpec((1, tm, tn), lambda i, b, j, hb, tb: (b, i, j)),
                   pl.BlockSpec((1, N_GATES, tm), lambda i, b, j, hb, tb: (b, 0, i)))
        + tuple(slab(s) for s in slabs),
        scratch_shapes=[pltpu.VMEM((tm, D), BF16)],
    )
    outs = pl.pallas_call(
        functools.partial(_proj_kernel, kinds, srcs, len(slabs)),
        out_shape=(jax.ShapeDtypeStruct((B, T, n_sec * tn), BF16),
                   jax.ShapeDtypeStruct((B, N_GATES, T), F32))
        + tuple(jax.ShapeDtypeStruct(s.shape, BF16) for s in slabs),
        grid_spec=grid_spec,
        compiler_params=_cparams("parallel", "parallel", "arbitrary"),
        name="proj_lat" if tables is not None else "proj_ctx",
    )(jnp.asarray(hb, I32), jnp.asarray(tb, I32), *args)
    return outs[0], outs[1], [o.reshape(a.shape) for o, a in zip(outs[2:], side_cast)]


def _ret_build(dl_ref, q_ref, k_ref, v_ref, rg_ref, ck_ref, cv_ref, o_ref,
               sf_ref, sb_ref, fs_ref, bs_ref, dec_ref, d_ref):
    h = pl.program_id(0)
    L = SCAN_L
    n_chunks = q_ref.shape[1] // L
    n_ctx_chunks = ck_ref.shape[1] // L
    lgf = _log_sigmoid(jnp.full((1, 1), dl_ref[0, h], F32))
    lgb = _log_sigmoid(jnp.full((1, 1), dl_ref[1, h], F32))

    @pl.when(pl.program_id(1) == 0)
    def _():
        ri = lax.broadcasted_iota(I32, (L, L), 0)
        ci = lax.broadcasted_iota(I32, (L, L), 1)
        rel = (ri - ci).astype(F32)
        d_ref[...] = jnp.where(rel >= 0.0, jnp.exp(jnp.maximum(rel, 0.0) * lgf),
                               jnp.exp(jnp.maximum(-rel, 0.0) * lgb))
        row = lax.broadcasted_iota(I32, (L, HEAD_W), 0).astype(F32)
        dec_ref[0] = jnp.exp((row + 1.0) * lgf)
        dec_ref[1] = jnp.exp((L - 1.0 - row) * lgf)
        dec_ref[2] = jnp.exp((L - row) * lgb)
        dec_ref[3] = jnp.exp(row * lgb)

    cdf = jnp.exp(L * lgf)
    cdb = jnp.exp(L * lgb)

    def update(s_ref, kc, vc, kd, cd):
        kdec = (kc.astype(F32) * kd).astype(BF16)
        s_ref[...] = s_ref[...] * cd + lax.dot_general(kdec, vc, TN_DIMS, preferred_element_type=F32)

    sf_ref[...] = jnp.zeros_like(sf_ref)
    sb_ref[...] = jnp.zeros_like(sb_ref)
    for c in range(n_ctx_chunks):
        update(sf_ref, ck_ref[0, c * L:(c + 1) * L, :], cv_ref[0, c * L:(c + 1) * L, :], dec_ref[1], cdf)
    for c in reversed(range(n_ctx_chunks)):
        update(sb_ref, ck_ref[0, c * L:(c + 1) * L, :], cv_ref[0, c * L:(c + 1) * L, :], dec_ref[3], cdb)

    def state_pass(i, carry):
        cb = n_chunks - 1 - i
        rf = pl.multiple_of(i * L, L)
        rb = pl.multiple_of(cb * L, L)
        fs_ref[i] = sf_ref[...].astype(BF16)
        bs_ref[cb] = sb_ref[...].astype(BF16)
        update(sf_ref, k_ref[0, pl.ds(rf, L), :], v_ref[0, pl.ds(rf, L), :], dec_ref[1], cdf)
        update(sb_ref, k_ref[0, pl.ds(rb, L), :], v_ref[0, pl.ds(rb, L), :], dec_ref[3], cdb)
        return carry

    def finish_states():
        fs_ref[n_chunks - 1] = sf_ref[...].astype(BF16)
        bs_ref[0] = sb_ref[...].astype(BF16)

    def out_chunk(c):
        r0 = pl.multiple_of(c * L, L)
        q = q_ref[0, pl.ds(r0, L), :]
        k = k_ref[0, pl.ds(r0, L), :]
        v = v_ref[0, pl.ds(r0, L), :]
        s = lax.dot_general(q, k, NT_DIMS, preferred_element_type=F32)
        att = (s * d_ref[...]).astype(BF16)
        o = jnp.dot(att, v, preferred_element_type=F32)
        o = o + jnp.dot(q, fs_ref[c], preferred_element_type=F32) * dec_ref[0]
        o = o + jnp.dot(q, bs_ref[c], preferred_element_type=F32) * dec_ref[2]
        rg = rg_ref[0, pl.ds(r0, L), :].astype(F32)
        o_ref[0, pl.ds(r0, L), :] = (_layer_norm(o) * _silu(rg)).astype(BF16)

    return state_pass, finish_states, out_chunk


def _ret_scratch(n_chunks):
    return [pltpu.VMEM((HEAD_W, HEAD_W), F32),
            pltpu.VMEM((HEAD_W, HEAD_W), F32),
            pltpu.VMEM((n_chunks, HEAD_W, HEAD_W), BF16),
            pltpu.VMEM((n_chunks, HEAD_W, HEAD_W), BF16),
            pltpu.VMEM((4, SCAN_L, HEAD_W), F32),
            pltpu.VMEM((SCAN_L, SCAN_L), F32)]


def _mlstm_build(qp_ref, kp_ref, v_ref, mo_ref, ckp_ref, cv_ref, gt_ref, cgt_ref,
                 wq_ref, bq_ref, wk_ref, bk_ref, o_ref,
                 tab_ref, row_ref,
                 cf_ref, mf_ref, cb_ref, mb_ref, cfs_ref, mfs_ref, cbs_ref, mbs_ref, mask_ref,
                 xk_ref, xq_ref, q_ref, k_ref, ck_ref):
    L = SCAN_L
    T = qp_ref.shape[1]
    Tc = ckp_ref.shape[1]
    n_chunks = T // L
    n_ctx_chunks = Tc // L
    CV = CONV_ROWS

    def conv_stage(src_ref, xs_ref, t_len):
        xs_ref[pl.ds(0, 8), :] = jnp.zeros((8, HEAD_W), F32)
        xs_ref[pl.ds(8 + t_len, 8), :] = jnp.zeros((8, HEAD_W), F32)
        xs_ref[pl.ds(8, t_len), :] = src_ref[0].astype(F32)

    def conv_rows(xs_ref, c, w_ref, b_ref, dst_ref, scale):
        w = w_ref[...]
        r0 = pl.multiple_of(c * CV, CV)
        win = xs_ref[pl.ds(r0, CV + 16), :]
        prev = pltpu.roll(win, 1, 0)[8:8 + CV, :]
        cur = win[8:8 + CV, :]
        nxt = pltpu.roll(win, CV + 15, 0)[8:8 + CV, :]
        y = _silu(prev * w[0:1, :] + cur * w[1:2, :] + nxt * w[2:3, :] + b_ref[...])
        if scale != 1.0:
            y = y * scale
        dst_ref[pl.ds(r0, CV), :] = y.astype(BF16)

    per_chunk = L // CV

    def conv_k(c):
        for u in range(per_chunk):
            conv_rows(xk_ref, c * per_chunk + u, wk_ref, bk_ref, k_ref, KEY_SCALE)

    def conv_q(c):
        for u in range(per_chunk):
            conv_rows(xq_ref, c * per_chunk + u, wq_ref, bq_ref, q_ref, 1.0)

    conv_stage(ckp_ref, xk_ref, Tc)
    for c in range(Tc // CV):
        conv_rows(xk_ref, c, wk_ref, bk_ref, ck_ref, KEY_SCALE)
    conv_stage(kp_ref, xk_ref, T)
    conv_stage(qp_ref, xq_ref, T)
    conv_k(0)
    conv_k(n_chunks - 1)
    n_k_steps = n_chunks // 2 - 1

    def conv_step(i, keys):
        if keys:
            conv_k(i + 1)
            conv_k(n_chunks - 2 - i)
        else:
            conv_q(2 * (i - n_k_steps))
            conv_q(2 * (i - n_k_steps) + 1)

    ri = lax.broadcasted_iota(I32, (L, L), 0)
    ci = lax.broadcasted_iota(I32, (L, L), 1)
    tri_u = (ri <= ci).astype(BF16)
    lane8 = lax.broadcasted_iota(I32, (8, L), 1)
    sub8 = lax.broadcasted_iota(I32, (8, L), 0)
    sel_r = lax.broadcasted_iota(I32, (24, 8 * 128), 0) % 8
    sel_c = lax.broadcasted_iota(I32, (24, 8 * 128), 1) // 128
    sel3 = (sel_r == sel_c).astype(BF16)
    ones_cols = jnp.ones((L, AUG_W - HEAD_W), BF16)

    def chunk_tables(g8, n_used, state_only):
        i_f, i_b = g8[0], g8[2]
        lf_f, lf_b = _log_sigmoid(g8[1]), _log_sigmoid(g8[3])
        cs3 = jnp.dot(_split3(jnp.concatenate([lf_f, lf_b], axis=0)), tri_u,
                      preferred_element_type=F32)
        cs = cs3[0:16] + cs3[16:32] + cs3[32:48]
        b_f = cs[0:8]
        b_b = cs[8:16, L - 1:L] - cs[8:16] + lf_b
        z_f = i_f - b_f
        z_b = i_b - b_b
        g_f = b_f[:, L - 1:L] - b_f + i_f
        g_b = b_b[:, 0:1] - b_b + i_b
        mf, mb = z_f, z_b
        s = 1
        while s < L:
            mf = jnp.maximum(mf, jnp.where(lane8 >= s, pltpu.roll(mf, s, 1), NEG_INF))
            mb = jnp.maximum(mb, jnp.where(lane8 < L - s, pltpu.roll(mb, L - s, 1), NEG_INF))
            s *= 2
        mb = jnp.where(lane8 < L - 1, pltpu.roll(mb, L - 1, 1), NEG_INF)
        reps = [None if state_only and t not in (2, 5) else
                lax.dot_general(_split3(val), sel3[:, 0:n_used * 128], TN_DIMS, preferred_element_type=F32)
                for t, val in enumerate((mf, b_f, g_f, mb, b_b, g_b))]

        def rows_of(c):
            out = jnp.zeros((8, L), F32)
            for r, val in enumerate((z_f, z_b, g_f, g_b, b_f, b_b)):
                out = jnp.where(sub8 == r, val[c:c + 1], out)
            return out

        return rows_of, reps

    lat_rows, lat_reps = chunk_tables(gt_ref[0, 0], n_chunks, False)
    for c in range(n_chunks):
        row_ref[c] = lat_rows(c)
        for t in range(N_TAB):
            tab_ref[t, c * L:(c + 1) * L, :] = lat_reps[t][:, c * 128:(c + 1) * 128]

    def lanes2(x):
        return jnp.concatenate([x, x], axis=1)

    def advance(k, v, g_rep, g_row, b_last, c_ref, m_ref):
        m = m_ref[...]
        m_new = jnp.maximum(b_last + m, jnp.max(g_row, axis=-1, keepdims=True))
        kw = (k.astype(F32) * jnp.exp(lanes2(g_rep) - m_new)).astype(BF16)
        v_aug = jnp.concatenate([v, ones_cols], axis=1)
        c_ref[...] = jnp.exp(b_last + m - m_new) * c_ref[...] + lax.dot_general(
            kw, v_aug, TN_DIMS, preferred_element_type=F32)
        m_ref[...] = m_new

    for r in (cf_ref, mf_ref, cb_ref, mb_ref):
        r[...] = jnp.zeros_like(r)
    ctx_rows, ctx_reps = chunk_tables(cgt_ref[0, 0], n_ctx_chunks, True)
    for c in range(n_ctx_chunks):
        rows = ctx_rows(c)
        advance(ck_ref[c * L:(c + 1) * L, :], cv_ref[0, c * L:(c + 1) * L, :],
                ctx_reps[2][:, c * 128:(c + 1) * 128], rows[2:3], rows[4:5, L - 1:L], cf_ref, mf_ref)
    for c in reversed(range(n_ctx_chunks)):
        rows = ctx_rows(c)
        advance(ck_ref[c * L:(c + 1) * L, :], cv_ref[0, c * L:(c + 1) * L, :],
                ctx_reps[5][:, c * 128:(c + 1) * 128], rows[3:4], rows[5:6, 0:1], cb_ref, mb_ref)

    def state_pass(i, carry, keys):
        cb = n_chunks - 1 - i
        rf = pl.multiple_of(i * L, L)
        rb = pl.multiple_of(cb * L, L)
        cfs_ref[i] = cf_ref[...].astype(BF16)
        mfs_ref[i] = mf_ref[...]
        cbs_ref[cb] = cb_ref[...].astype(BF16)
        mbs_ref[cb] = mb_ref[...]
        rows_f = row_ref[i]
        rows_b = row_ref[cb]
        advance(k_ref[pl.ds(rf, L), :], v_ref[0, pl.ds(rf, L), :], tab_ref[2, pl.ds(rf, L), :],
                rows_f[2:3], rows_f[4:5, L - 1:L], cf_ref, mf_ref)
        advance(k_ref[pl.ds(rb, L), :], v_ref[0, pl.ds(rb, L), :], tab_ref[5, pl.ds(rb, L), :],
                rows_b[3:4], rows_b[5:6, 0:1], cb_ref, mb_ref)
        conv_step(i, keys)
        return carry

    def finish_states():
        cfs_ref[n_chunks - 1] = cf_ref[...].astype(BF16)
        mfs_ref[n_chunks - 1] = mf_ref[...]
        cbs_ref[0] = cb_ref[...].astype(BF16)
        mbs_ref[0] = mb_ref[...]

    def direction(q, v_aug, s, z_row, zmax_rep, b_rep, mask, c_in, m_in):
        mx = jnp.maximum(zmax_rep, m_in)
        att = s * jnp.exp((z_row - lanes2(mx)) + mask)
        na = jnp.dot(att.astype(BF16), v_aug, preferred_element_type=F32)
        qa = jnp.dot(q, c_in, preferred_element_type=F32)
        a = jnp.exp(m_in - mx)
        num = na[:, 0:HEAD_W] + lanes2(a) * qa[:, 0:HEAD_W]
        den = na[:, HEAD_W:] + a * qa[:, HEAD_W:]
        scale = 1.0 / jnp.maximum(jnp.abs(den), jnp.exp(-(b_rep + mx)))
        return num * lanes2(scale)

    @pl.when(pl.program_id(1) == 0)
    def _():
        mask_ref[0] = jnp.where(ci <= ri, 0.0, NEG_INF)
        mask_ref[1] = jnp.where(ci > ri, 0.0, NEG_INF)

    def out_chunk(c):
        r0 = pl.multiple_of(c * L, L)
        q = q_ref[pl.ds(r0, L), :]
        k = k_ref[pl.ds(r0, L), :]
        v_aug = jnp.concatenate([v_ref[0, pl.ds(r0, L), :], ones_cols], axis=1)
        s = lax.dot_general(q, k, NT_DIMS, preferred_element_type=F32)
        rows = row_ref[c]
        tot = direction(q, v_aug, s, rows[0:1], tab_ref[0, pl.ds(r0, L), :], tab_ref[1, pl.ds(r0, L), :],
                        mask_ref[0], cfs_ref[c], mfs_ref[c])
        tot = tot + direction(q, v_aug, s, rows[1:2], tab_ref[3, pl.ds(r0, L), :],
                              tab_ref[4, pl.ds(r0, L), :], mask_ref[1], cbs_ref[c], mbs_ref[c])
        mo = mo_ref[0, pl.ds(r0, L), :].astype(F32)
        o_ref[0, pl.ds(r0, L), :] = (_layer_norm(tot) * jax.nn.sigmoid(mo)).astype(BF16)

    return state_pass, finish_states, out_chunk, n_k_steps


def _mlstm_scratch(T, Tc, n_chunks):
    state = [pltpu.VMEM((HEAD_W, AUG_W), F32), pltpu.VMEM((1, 1), F32)]
    snaps = [pltpu.VMEM((n_chunks, HEAD_W, AUG_W), BF16), pltpu.VMEM((n_chunks, 1, 1), F32)]
    return [pltpu.VMEM((N_TAB, T, 128), F32), pltpu.VMEM((n_chunks, 8, SCAN_L), F32)] \
        + state + state + snaps + snaps + [pltpu.VMEM((2, SCAN_L, SCAN_L), F32)] \
        + [pltpu.VMEM((T + 16, HEAD_W), F32), pltpu.VMEM((T + 16, HEAD_W), F32),
           pltpu.VMEM((T, HEAD_W), BF16), pltpu.VMEM((T, HEAD_W), BF16), pltpu.VMEM((Tc, HEAD_W), BF16)]


def _scan_kernel(n_ret_scratch, dl_ref, rq_ref, rk_ref, rv_ref, rg_ref, rck_ref, rcv_ref,
                 mq_ref, mk_ref, mv_ref, mo_ref, mck_ref, mcv_ref, gt_ref, cgt_ref,
                 wq_ref, bq_ref, wk_ref, bk_ref, r_ref, m_ref, *scratch):
    n_chunks = rq_ref.shape[1] // SCAN_L
    ret = _ret_build(dl_ref, rq_ref, rk_ref, rv_ref, rg_ref, rck_ref, rcv_ref, r_ref,
                     *scratch[:n_ret_scratch])
    mls = _mlstm_build(mq_ref, mk_ref, mv_ref, mo_ref, mck_ref, mcv_ref, gt_ref, cgt_ref,
                       wq_ref, bq_ref, wk_ref, bk_ref, m_ref, *scratch[n_ret_scratch:])

    def state_pass(i, carry, keys):
        ret[0](i, carry)
        mls[0](i, carry, keys)
        return carry

    n_k_steps = mls[3]
    lax.fori_loop(0, n_k_steps, functools.partial(state_pass, keys=True), 0)
    lax.fori_loop(n_k_steps, n_chunks - 1, functools.partial(state_pass, keys=False), 0)
    ret[1]()
    mls[1]()

    def out_pass(i, carry):
        for c in (2 * i, 2 * i + 1):
            ret[2](c)
            mls[2](c)
        return carry

    lax.fori_loop(0, n_chunks // 2, out_pass, 0)


def _scans(decay_logit, p_lat, p_ctx, gt, cgt, conv_w, conv_b, ret_lat, ret_ctx, ml_lat, ml_ctx):
    B, T, _ = p_lat.shape
    Tc = p_ctx.shape[1]
    assert T % (2 * SCAN_L) == 0 and Tc % SCAN_L == 0 and T // SCAN_L <= 8
    n_chunks = T // SCAN_L

    def lat(sec):
        return pl.BlockSpec((1, T, HEAD_W), lambda h, b: (b, 0, sec * HEADS + h))

    def cx(sec):
        return pl.BlockSpec((1, Tc, HEAD_W), lambda h, b: (b, 0, sec * HEADS + h))

    gates = pl.BlockSpec((1, 1, N_GK, 8, SCAN_L), lambda h, b: (b, h, 0, 0, 0))
    out = pl.BlockSpec((1, T, HEAD_W), lambda h, b: (b, 0, h))
    conv_specs = [pl.BlockSpec((3, HEAD_W), lambda h, b: (0, h)),
                  pl.BlockSpec((1, HEAD_W), lambda h, b: (0, h)),
                  pl.BlockSpec((3, HEAD_W), lambda h, b: (0, HEADS + h)),
                  pl.BlockSpec((1, HEAD_W), lambda h, b: (0, HEADS + h))]
    ret_scratch = _ret_scratch(n_chunks)
    return pl.pallas_call(
        functools.partial(_scan_kernel, len(ret_scratch)),
        out_shape=(jax.ShapeDtypeStruct((B, T, BRANCH_W), BF16),
                   jax.ShapeDtypeStruct((B, T, BRANCH_W), BF16)),
        grid=(HEADS, B),
        in_specs=[pl.BlockSpec(memory_space=pltpu.SMEM)]
        + [lat(s) for s in ret_lat] + [cx(s) for s in ret_ctx]
        + [lat(s) for s in ml_lat] + [cx(s) for s in ml_ctx] + [gates, gates] + conv_specs,
        out_specs=(out, out),
        scratch_shapes=ret_scratch + _mlstm_scratch(T, Tc, n_chunks),
        compiler_params=_cparams("arbitrary", "arbitrary"),
        name="scans",
    )(decay_logit, *([p_lat] * 4), *([p_ctx] * 2), *([p_lat] * 4), *([p_ctx] * 2), gt, cgt,
      conv_w, conv_b, conv_w, conv_b)


def _merge_kernel(alpha, r_ref, m_ref, gr_ref, gm_ref, x_ref, g1_ref, sh2_ref, sc2_ref,
                  lng_ref, lnb_ref, wr_ref, wm_ref, wo_ref, wrt_ref, brt_ref, e1_ref, e3_ref, e2_ref,
                  x1_ref, ua_ref, ub_ref, ri_ref, rw_ref, cnt_ref, e1b_ref, e3b_ref, e2b_ref,
                  carry_ref, u_ref):
    @pl.when(jnp.logical_and(pl.program_id(0) == 0, pl.program_id(1) == 0))
    def _():
        carry_ref[...] = jnp.zeros_like(carry_ref)
        tm = x_ref.shape[1]
        r = lax.broadcasted_iota(I32, (tm, tm), 0)
        c = lax.broadcasted_iota(I32, (tm, tm), 1)
        u_ref[...] = (r < c).astype(BF16)

    yr = jnp.dot(r_ref[0], wr_ref[...], preferred_element_type=F32)
    ym = jnp.dot(m_ref[0], wm_ref[...], preferred_element_type=F32)
    for src, dst in ((e1_ref, e1b_ref), (e3_ref, e3b_ref), (e2_ref, e2b_ref)):
        dst[...] = src[...].astype(BF16)
    y = jax.nn.sigmoid(gr_ref[0].astype(F32)) * yr + jax.nn.sigmoid(gm_ref[0].astype(F32)) * ym
    yo = jnp.dot(y.astype(BF16), wo_ref[...], preferred_element_type=F32)
    x1 = _layer_norm(alpha * x_ref[0] + g1_ref[0] * yo) * lng_ref[...] + lnb_ref[...]
    x1_ref[0] = x1
    u2 = _layer_norm(x1) * (1.0 + sc2_ref[0]) + sh2_ref[0]
    ua_ref[0] = _pack_pairs(u2[:, 0:PACK_W], u2[:, PACK_W:2 * PACK_W])
    ub_ref[0] = _pack_pairs(u2[:, 2 * PACK_W:3 * PACK_W], u2[:, 3 * PACK_W:4 * PACK_W])
    lt = lax.dot_general(wrt_ref[...], u2.astype(BF16), NT_DIMS, preferred_element_type=F32) + brt_ref[...]
    _route_tile(lt, ri_ref, rw_ref, cnt_ref, carry_ref, u_ref)


def _route_tile(lt, ri_ref, rw_ref, cnt_ref, carry_ref, u_ref):
    tm = lt.shape[1]
    lg = lt[0:N_GROUPS, :]
    eg = jnp.exp(lg - jnp.max(lg, axis=0, keepdims=True))
    pg = eg / jnp.sum(eg, axis=0, keepdims=True)
    pg_top = jnp.max(pg, axis=0, keepdims=True)
    rows_g = lax.broadcasted_iota(I32, pg.shape, 0)
    g_idx = jnp.min(jnp.where(pg == pg_top, rows_g, N_GROUPS), axis=0, keepdims=True)

    le = jnp.zeros((EXP_PER_GROUP, tm), F32)
    for g in range(N_GROUPS):
        lo = 8 + g * EXP_PER_GROUP
        le = jnp.where(g_idx == g, lt[lo:lo + EXP_PER_GROUP, :], le)
    ee = jnp.exp(le - jnp.max(le, axis=0, keepdims=True))
    pe = ee / jnp.sum(ee, axis=0, keepdims=True)
    rows_e = lax.broadcasted_iota(I32, pe.shape, 0)
    v1 = jnp.max(pe, axis=0, keepdims=True)
    i1 = jnp.min(jnp.where(pe == v1, rows_e, EXP_PER_GROUP), axis=0, keepdims=True)
    pe2 = jnp.where(rows_e == i1, -1.0, pe)
    v2 = jnp.max(pe2, axis=0, keepdims=True)
    i2 = jnp.min(jnp.where(pe2 == v2, rows_e, EXP_PER_GROUP), axis=0, keepdims=True)
    den = v1 + v2
    rw_ref[...] = jnp.zeros_like(rw_ref)
    rw_ref[0:1, :] = pg_top * v1 / den
    rw_ref[1:2, :] = pg_top * v2 / den
    e1 = g_idx * EXP_PER_GROUP + i1
    e2 = g_idx * EXP_PER_GROUP + i2

    rows_x = lax.broadcasted_iota(I32, (N_EXPERTS, tm), 0)
    oh1 = (rows_x == e1).astype(F32)
    oh2 = (rows_x == e2).astype(F32)
    both = oh1 + oh2
    before = carry_ref[:, 0:1] + jnp.dot(both.astype(BF16), u_ref[...], preferred_element_type=F32)
    ri_ref[0:1, :] = e1
    ri_ref[1:2, :] = e2
    ri_ref[2:3, :] = jnp.sum(oh1 * before, axis=0, keepdims=True).astype(I32)
    ri_ref[3:4, :] = jnp.sum(oh2 * before, axis=0, keepdims=True).astype(I32)
    carry_ref[...] = carry_ref[...] + jnp.sum(both, axis=1, keepdims=True)
    cnt_ref[...] = carry_ref[...].astype(I32)


def _merge(alpha, r, m, p_lat, sec_gates, x, g1, sh2, sc2, lng, lnb, wr, wm, wo, wrt, brt, expert_w):
    B, T, D = x.shape
    tm = MERGE_TM
    per_b = T // tm
    n = B * T
    n_steps = B * per_b
    sliced = [w.reshape(n_steps, w.shape[0] * w.shape[1] // n_steps, w.shape[2]) for w in expert_w]
    assert all(s.shape[1] % 16 == 0 and s.size == w.size for s, w in zip(sliced, expert_w))

    def slab(s):
        return pl.BlockSpec((1,) + s.shape[1:], lambda b, i: (b * per_b + i, 0, 0))

    def tile(w):
        return pl.BlockSpec((1, tm, w), lambda b, i: (b, i, 0))

    def sec(s):
        return pl.BlockSpec((1, tm, BRANCH_W), lambda b, i: (b, i, s))

    def mod():
        return pl.BlockSpec((1, 1, D), lambda b, i: (b, 0, 0))

    def const(shape):
        return pl.BlockSpec(shape, lambda b, i: (0,) * len(shape))

    outs = pl.pallas_call(
        functools.partial(_merge_kernel, alpha),
        out_shape=(jax.ShapeDtypeStruct((B, T, D), F32),
                   jax.ShapeDtypeStruct((B, T, PACK_W), U32),
                   jax.ShapeDtypeStruct((B, T, PACK_W), U32),
                   jax.ShapeDtypeStruct((4, n), I32),
                   jax.ShapeDtypeStruct((8, n), F32),
                   jax.ShapeDtypeStruct((N_EXPERTS, 128), I32))
        + tuple(jax.ShapeDtypeStruct(s.shape, BF16) for s in sliced),
        grid=(B, per_b),
        in_specs=[tile(BRANCH_W), tile(BRANCH_W), sec(sec_gates[0]), sec(sec_gates[1]), tile(D),
                  mod(), mod(), mod(), const((1, D)), const((1, D)),
                  const((BRANCH_W, D)), const((BRANCH_W, D)), const((D, D)),
                  const((ROUTE_ROWS, D)), const((ROUTE_ROWS, 1))] + [slab(s) for s in sliced],
        out_specs=(tile(D), tile(PACK_W), tile(PACK_W),
                   pl.BlockSpec((4, tm), lambda b, i: (0, b * per_b + i)),
                   pl.BlockSpec((8, tm), lambda b, i: (0, b * per_b + i)),
                   pl.BlockSpec((N_EXPERTS, 128), lambda b, i: (0, 0)))
        + tuple(slab(s) for s in sliced),
        scratch_shapes=[pltpu.VMEM((N_EXPERTS, 128), F32), pltpu.VMEM((tm, tm), BF16)],
        compiler_params=_cparams("arbitrary", "arbitrary"),
        name="merge",
    )(r, m, p_lat, p_lat, x, g1, sh2, sc2, lng, lnb, wr, wm, wo, wrt, brt, *sliced)
    return outs[:6] + tuple(o.reshape(w.shape) for o, w in zip(outs[6:], expert_w))


def _sc_mesh():
    return plsc.VectorSubcoreMesh(core_axis_name="c", subcore_axis_name="s")


def _sc_scatter2(rows_a, rows_b, idx0, idx1, n_out):
    m, w = rows_a.shape
    out = jax.ShapeDtypeStruct((n_out, w), rows_a.dtype)

    @functools.partial(pl.kernel, out_type=(out, out), mesh=_sc_mesh(), scratch_types=[])
    def k(xa_hbm, xb_hbm, i0_hbm, i1_hbm, oa_hbm, ob_hbm):
        for x_hbm, o_hbm in ((xa_hbm, oa_hbm), (xb_hbm, ob_hbm)):
            def body(x_vmem, i0_vmem, i1_vmem, o_hbm=o_hbm):
                pltpu.sync_copy(x_vmem, o_hbm.at[i0_vmem.at[0]])
                pltpu.sync_copy(x_vmem, o_hbm.at[i1_vmem.at[0]])

            pltpu.emit_pipeline(
                body,
                grid=(m // SC_WIN,),
                in_specs=[pl.BlockSpec((SC_WIN, w), lambda i: (i, 0)),
                          pl.BlockSpec((1, SC_WIN), lambda i: (0, i)),
                          pl.BlockSpec((1, SC_WIN), lambda i: (0, i))],
                out_specs=[],
                core_axis_name=("c", "s"),
                dimension_semantics=(pltpu.PARALLEL,),
            )(x_hbm, i0_hbm, i1_hbm)

    return k(rows_a, rows_b, idx0.reshape(1, m), idx1.reshape(1, m))


def _sc_gather(table_a, table_b, idx):
    m = idx.shape[0]
    w = table_a.shape[1]
    out = jax.ShapeDtypeStruct((m, w), table_a.dtype)

    @functools.partial(pl.kernel, out_type=(out, out), mesh=_sc_mesh(), scratch_types=[])
    def k(ta_hbm, tb_hbm, i_hbm, oa_hbm, ob_hbm):
        for t_hbm, o_hbm in ((ta_hbm, oa_hbm), (tb_hbm, ob_hbm)):
            def body(i_vmem, o_vmem, t_hbm=t_hbm):
                pltpu.sync_copy(t_hbm.at[i_vmem.at[0]], o_vmem)

            pltpu.emit_pipeline(
                body,
                grid=(m // SC_WIN,),
                in_specs=[pl.BlockSpec((1, SC_WIN), lambda i: (0, i))],
                out_specs=[pl.BlockSpec((SC_WIN, w), lambda i: (i, 0))],
                core_axis_name=("c", "s"),
                dimension_semantics=(pltpu.PARALLEL,),
            )(i_hbm, o_hbm)

    return k(table_a, table_b, idx.reshape(1, m))


def _expert_kernel(be_ref, nv_ref, xa_ref, xb_ref, w1_ref, w3_ref, w2_ref, ya_ref, yb_ref):
    j = pl.program_id(0)
    nv = nv_ref[j]

    @pl.when(nv > 0)
    def _():
        valid = lax.broadcasted_iota(I32, xa_ref.shape, 0) < nv
        zero = jnp.zeros(xa_ref.shape, U32)
        parts = _unpack_pairs(jnp.where(valid, xa_ref[...], zero)) + \
            _unpack_pairs(jnp.where(valid, xb_ref[...], zero))
        x = jnp.concatenate([p.astype(BF16) for p in parts], axis=1)
        h1 = jnp.dot(x, w1_ref[0], preferred_element_type=F32)
        h3 = jnp.dot(x, w3_ref[0], preferred_element_type=F32)
        y = jnp.dot((_silu(h1) * h3).astype(BF16), w2_ref[0], preferred_element_type=F32)
        ya_ref[...] = _pack_pairs(y[:, 0:PACK_W], y[:, PACK_W:2 * PACK_W])
        yb_ref[...] = _pack_pairs(y[:, 2 * PACK_W:3 * PACK_W], y[:, 3 * PACK_W:4 * PACK_W])

    @pl.when(nv == 0)
    def _():
        ya_ref[...] = jnp.zeros_like(ya_ref)
        yb_ref[...] = jnp.zeros_like(yb_ref)


def _experts(block_exp, n_valid, xa, xb, w1, w3, w2):
    n_slots = xa.shape[0]
    n_blocks = n_slots // MOE_BLK
    d, de = w1.shape[1], w1.shape[2]
    slot = pl.BlockSpec((MOE_BLK, PACK_W), lambda j, be, nv: (j, 0))
    grid_spec = pltpu.PrefetchScalarGridSpec(
        num_scalar_prefetch=2,
        grid=(n_blocks,),
        in_specs=[slot, slot,
                  pl.BlockSpec((1, d, de), lambda j, be, nv: (be[j], 0, 0)),
                  pl.BlockSpec((1, d, de), lambda j, be, nv: (be[j], 0, 0)),
                  pl.BlockSpec((1, de, d), lambda j, be, nv: (be[j], 0, 0))],
        out_specs=(slot, slot),
    )
    return pl.pallas_call(
        _expert_kernel,
        out_shape=(jax.ShapeDtypeStruct((n_slots, PACK_W), U32),
                   jax.ShapeDtypeStruct((n_slots, PACK_W), U32)),
        grid_spec=grid_spec,
        compiler_params=_cparams("parallel"),
        name="experts",
    )(block_exp, n_valid, xa, xb, w1, w3, w2)


def _final_kernel(alpha, x1_ref, a0_ref, b0_ref, a1_ref, b1_ref, w_ref, g2_ref, lng_ref, lnb_ref, o_ref):
    w = w_ref[...].T
    w0 = w[:, 0:1]
    w1 = w[:, 1:2]
    parts0 = _unpack_pairs(a0_ref[...]) + _unpack_pairs(b0_ref[...])
    parts1 = _unpack_pairs(a1_ref[...]) + _unpack_pairs(b1_ref[...])
    f = jnp.concatenate([w0 * p0 + w1 * p1 for p0, p1 in zip(parts0, parts1)], axis=1)
    o_ref[0] = _layer_norm(alpha * x1_ref[0] + g2_ref[0] * f) * lng_ref[...] + lnb_ref[...]


def _final(alpha, x1, ya, yb, w, g2, lng, lnb):
    B, T, D = x1.shape
    tm = min(FINAL_TM, T)
    per_b = T // tm
    n_tiles = B * per_b

    def rows(k):
        return pl.BlockSpec((tm, PACK_W), lambda b, i: (k * n_tiles + b * per_b + i, 0))

    return pl.pallas_call(
        functools.partial(_final_kernel, alpha),
        out_shape=jax.ShapeDtypeStruct((B, T, D), F32),
        grid=(B, per_b),
        in_specs=[pl.BlockSpec((1, tm, D), lambda b, i: (b, i, 0)),
                  rows(0), rows(0), rows(1), rows(1),
                  pl.BlockSpec((8, tm), lambda b, i: (0, b * per_b + i)),
                  pl.BlockSpec((1, 1, D), lambda b, i: (b, 0, 0)),
                  pl.BlockSpec((1, D), lambda b, i: (0, 0)),
                  pl.BlockSpec((1, D), lambda b, i: (0, 0))],
        out_specs=pl.BlockSpec((1, tm, D), lambda b, i: (b, i, 0)),
        compiler_params=_cparams("parallel", "parallel"),
        name="final",
    )(x1, ya, yb, ya, yb, w, g2, lng, lnb)


def _rotary_tables(T):
    quarter = HEAD_W // 4
    freqs = ROPE_BASE ** (-jnp.arange(quarter, dtype=F32) / quarter)
    t = jnp.arange(T)
    ang_r = (t // GRID_W).astype(F32)[:, None] * freqs[None, :]
    ang_c = (t % GRID_W).astype(F32)[:, None] * freqs[None, :]
    cos = jnp.concatenate([jnp.cos(ang_r)] * 2 + [jnp.cos(ang_c)] * 2, axis=1)
    sin = jnp.concatenate([-jnp.sin(ang_r), jnp.sin(ang_r), -jnp.sin(ang_c), jnp.sin(ang_c)], axis=1)
    return cos, sin


def _per_head_gates(gt):
    B, _, T = gt.shape
    n_chunks = T // SCAN_L
    gth = gt.reshape(B, N_GK, HEADS, n_chunks, SCAN_L).transpose(0, 2, 1, 3, 4)
    return jnp.pad(gth, ((0, 0), (0, 0), (0, 0), (0, 8 - n_chunks), (0, 0)))


def _table_lookup(table, idx):
    sel = idx[..., None] == jnp.arange(table.shape[0], dtype=idx.dtype)
    return jnp.sum(jnp.where(sel, table, 0), axis=-1)


def kernel(x, c, ctx, c_ctx, w_ada, b_ada, w_in, b_mgate, ml_conv_w, ml_conv_b, ret_decay_logit, w_ret_branch, w_ml_branch, w_out, ln1_g, ln1_b, w_rg, b_rg, w_re, b_re, w_e1, w_e3, w_e2, ln2_g, ln2_b):
    B, T, D = x.shape
    depth = w_ada.shape[0]
    assert depth == 1 and D == BRANCH_W and T % GRID_W == 0
    alpha = (2 * depth) ** 0.25
    n_tok = B * T

    n_rows = -(-(B + 1) // 8) * 8
    cs = jnp.zeros((n_rows, D), F32).at[:B].set(c).at[B].set(c_ctx)
    mod = _ada(cs, w_ada[0], b_ada[0][None, :])
    sh1, sc1, g1, sh2, sc2, g2 = [mod[:B, None, i * D:(i + 1) * D] for i in range(6)]
    csh1 = mod[B, 0 * D:1 * D].reshape(1, 1, D)
    csc1 = mod[B, 1 * D:2 * D].reshape(1, 1, D)

    w = w_in[0]
    g_lo = 8 * BRANCH_W
    w_gate_t = w[:, g_lo:g_lo + N_GATES].T.astype(BF16)
    b_gate = b_mgate[0][:, None]
    w_head = w[:, :g_lo].astype(BF16)
    w_tail = w[:, g_lo + N_GATES:].astype(BF16)
    sec_lat = tuple(("h", s) for s in range(8)) + (("t", 0), ("t", 1))
    sec_ctx = (("h", 1), ("h", 2), ("h", 5), ("h", 6))
    kinds_lat = ("rot", "rot_scale") + ("plain",) * 8
    kinds_ctx = ("scale", "plain", "plain", "plain")
    p_lat, gt_lat, _ = _proj(x, sh1, sc1, w_head, w_tail, sec_lat, w_gate_t, b_gate, kinds_lat,
                             _rotary_tables(T))
    Tc = ctx.shape[1]
    p_ctx, gt_ctx, (w_rb, w_mb, w_ob) = _proj(
        ctx.reshape(1, B * Tc, D), csh1, csc1, w_head, w_tail, sec_ctx, w_gate_t, b_gate, kinds_ctx,
        side_cast=(w_ret_branch[0], w_ml_branch[0], w_out[0]))
    p_ctx = p_ctx.reshape(B, Tc, -1)
    gt_ctx = gt_ctx.reshape(N_GATES, B, Tc).transpose(1, 0, 2)

    ret, mls = _scans(ret_decay_logit[0], p_lat, p_ctx, _per_head_gates(gt_lat), _per_head_gates(gt_ctx),
                      ml_conv_w[0], ml_conv_b[0][None, :], (0, 1, 2, 3), (0, 1), (4, 5, 6, 7), (2, 3))

    wrt = jnp.zeros((ROUTE_ROWS, D), F32).at[:N_GROUPS].set(w_rg[0].T).at[8:8 + N_EXPERTS].set(w_re[0].T)
    brt = jnp.zeros((ROUTE_ROWS, 1), F32).at[:N_GROUPS, 0].set(b_rg[0]).at[8:8 + N_EXPERTS, 0].set(b_re[0])
    x1, ua, ub, ri, rw, cnt, we1, we3, we2 = _merge(
        alpha, ret, mls, p_lat, (8, 9), x, g1, sh2, sc2, ln1_g[0][None, :], ln1_b[0][None, :],
        w_rb, w_mb, w_ob, wrt.astype(BF16), brt, (w_e1[0], w_e3[0], w_e2[0]))

    counts = cnt[:, 0]
    padded = (counts + MOE_BLK - 1) // MOE_BLK * MOE_BLK
    pad_end = jnp.cumsum(padded)
    pad_off = pad_end - padded
    dest = _table_lookup(pad_off, ri[0:2]) + ri[2:4]
    n_blocks = (2 * n_tok) // MOE_BLK + N_EXPERTS
    n_slots = n_blocks * MOE_BLK
    block_start = jnp.arange(n_blocks, dtype=I32) * MOE_BLK
    block_exp = jnp.minimum((block_start[:, None] >= pad_end[None, :]).sum(1), N_EXPERTS - 1).astype(I32)
    n_valid = jnp.clip(_table_lookup(counts, block_exp) - (block_start - _table_lookup(pad_off, block_exp)),
                       0, MOE_BLK).astype(I32)

    xa, xb = _sc_scatter2(ua.reshape(n_tok, PACK_W), ub.reshape(n_tok, PACK_W), dest[0], dest[1], n_slots)
    ya, yb = _experts(block_exp, n_valid, xa, xb, we1, we3, we2)
    ga, gb = _sc_gather(ya, yb, dest.reshape(2 * n_tok))
    return _final(alpha, x1, ga, gb, rw, g2, ln2_g[0][None, :], ln2_b[0][None, :])
```

```python
import functools

import jax
import jax.numpy as jnp
from jax import lax
from jax.experimental import pallas as pl
from jax.experimental.pallas import tpu as pltpu
from jax.experimental.pallas import tpu_sc as plsc

F32 = jnp.float32
BF16 = jnp.bfloat16
U32 = jnp.uint32
I32 = jnp.int32
HIGHEST = lax.Precision.HIGHEST

HEADS = 4
HEAD_W = 256
BRANCH_W = HEADS * HEAD_W
GRID_W = 64
ROPE_BASE = 10000.0
N_GATES = 16
N_GK = N_GATES // HEADS
N_GROUPS = 4
EXP_PER_GROUP = 8
N_EXPERTS = N_GROUPS * EXP_PER_GROUP
LN_EPS = 1e-5
NEG_INF = -1e30
KEY_SCALE = HEAD_W ** -0.5

SCAN_L = 256
CONV_ROWS = 128
PROJ_TM = 2048
PROJ_SUB = 256
MERGE_TM = 512
FINAL_TM = 1024
MOE_BLK = 512
SC_WIN = 128
PACK_W = 256
ROUTE_ROWS = 64
N_TAB = 6
AUG_W = HEAD_W + 128
VMEM_LIMIT = 48 * 1024 * 1024

NT_DIMS = (((1,), (1,)), ((), ()))
TN_DIMS = (((0,), (0,)), ((), ()))


def _cparams(*sem):
    return pltpu.CompilerParams(dimension_semantics=sem, vmem_limit_bytes=VMEM_LIMIT)


def _layer_norm(x):
    mu = jnp.mean(x, axis=-1, keepdims=True)
    xc = x - mu
    var = jnp.mean(xc * xc, axis=-1, keepdims=True)
    return xc * lax.rsqrt(var + LN_EPS)


def _log_sigmoid(x):
    return jnp.minimum(x, 0.0) - jnp.log1p(jnp.exp(-jnp.abs(x)))


def _silu(x):
    return x * jax.nn.sigmoid(x)


def _pack_pairs(hi, lo):
    hb = lax.bitcast_convert_type(hi.astype(BF16).astype(F32), U32)
    lb = lax.bitcast_convert_type(lo.astype(BF16).astype(F32), U32)
    return (hb & jnp.uint32(0xFFFF0000)) | (lb >> 16)


def _unpack_pairs(p):
    hi = lax.bitcast_convert_type(p & jnp.uint32(0xFFFF0000), F32)
    lo = lax.bitcast_convert_type(p << 16, F32)
    return hi, lo


def _split3(x):
    hi = x.astype(BF16).astype(F32)
    r1 = x - hi
    mid = r1.astype(BF16).astype(F32)
    lo = (r1 - mid).astype(BF16).astype(F32)
    return jnp.concatenate([hi, mid, lo], axis=0).astype(BF16)


def _ada_kernel(c_ref, w_ref, b_ref, o_ref):
    s = _silu(c_ref[...])
    o_ref[...] = jnp.dot(s, w_ref[...], precision=HIGHEST, preferred_element_type=F32) + b_ref[...]


def _ada(cs, w, b):
    rows, d = cs.shape
    cols = w.shape[1]
    tn = 1024
    return pl.pallas_call(
        _ada_kernel,
        out_shape=jax.ShapeDtypeStruct((rows, cols), F32),
        grid=(cols // tn,),
        in_specs=[pl.BlockSpec((rows, d), lambda j: (0, 0)),
                  pl.BlockSpec((d, tn), lambda j: (0, j)),
                  pl.BlockSpec((1, tn), lambda j: (0, j))],
        out_specs=pl.BlockSpec((rows, tn), lambda j: (0, j)),
        compiler_params=_cparams("parallel"),
        name="ada",
    )(cs, w, b)


def _proj_kernel(kinds, srcs, n_cast, hb_ref, tb_ref, x_ref, sh_ref, sc_ref, wh_ref, wt_ref, wg_ref,
                 bg_ref, *rest):
    cast_in, rest = rest[:n_cast], rest[n_cast:]
    if "rot" in kinds or "rot_scale" in kinds:
        cos_ref, sin_ref, rest = rest[0], rest[1], rest[2:]
    o_ref, gt_ref = rest[0], rest[1]
    cast_out, u_ref = rest[2:2 + n_cast], rest[2 + n_cast]
    for src, dst in zip(cast_in, cast_out):
        dst[...] = src[...].astype(BF16)
    j = pl.program_id(2)
    tm = x_ref.shape[1]
    sub = min(PROJ_SUB, tm)

    def rotary(acc, rows, scale):
        for s in range(acc.shape[1] // 128):
            a = acc[:, s * 128:(s + 1) * 128]
            half = s % 2
            cs = cos_ref[rows, half * 128:(half + 1) * 128]
            sn = sin_ref[rows, half * 128:(half + 1) * 128]
            r = a * cs + pltpu.roll(a, 64, 1) * sn
            if scale != 1.0:
                r = r * scale
            o_ref[0, rows, s * 128:(s + 1) * 128] = r.astype(BF16)

    def section(kind, first, src):
        w_ref = wh_ref if src == "h" else wt_ref
        for r in range(tm // sub):
            rows = slice(r * sub, (r + 1) * sub)
            if first:
                u = _layer_norm(x_ref[0, rows, :]) * (1.0 + sc_ref[0]) + sh_ref[0]
                ub = u.astype(BF16)
                u_ref[rows, :] = ub
                gt_ref[0, :, rows] = lax.dot_general(wg_ref[...], ub, NT_DIMS,
                                                     preferred_element_type=F32) + bg_ref[...]
            else:
                ub = u_ref[rows, :]
            acc = jnp.dot(ub, w_ref[...], preferred_element_type=F32)
            if kind == "rot":
                rotary(acc, rows, 1.0)
            elif kind == "rot_scale":
                rotary(acc, rows, KEY_SCALE)
            elif kind == "scale":
                o_ref[0, rows, :] = (acc * KEY_SCALE).astype(BF16)
            else:
                o_ref[0, rows, :] = acc.astype(BF16)

    variants = {}
    for s, key in enumerate(zip(kinds, srcs)):
        variants.setdefault(key + (s == 0,), []).append(s)
    for (kind, src, first), secs in variants.items():
        cond = functools.reduce(jnp.logical_or, [j == s for s in secs])

        @pl.when(cond)
        def _(kind=kind, first=first, src=src):
            section(kind, first, src)


def _proj(x, sh, sc, w_head, w_tail, sections, w_gate_t, b_gate, kinds, tables=None, side_cast=()):
    B, T, D = x.shape
    n_sec = len(kinds)
    tm = min(PROJ_TM, T)
    tn = BRANCH_W
    assert T % tm == 0
    n_steps = (T // tm) * B * n_sec
    slabs = [a.reshape(n_steps, a.shape[0] // n_steps, a.shape[1]) for a in side_cast]
    assert all(s.shape[1] % 16 == 0 and s.size == a.size for s, a in zip(slabs, side_cast))

    def slab(s):
        return pl.BlockSpec((1,) + s.shape[1:],
                            lambda i, b, j, hb, tb: ((i * B + b) * n_sec + j, 0, 0))

    srcs = tuple(src for src, _ in sections)
    hb, tb, h_last, t_last = [], [], 0, 0
    for src, blk in sections:
        h_last, t_last = (blk, t_last) if src == "h" else (h_last, blk)
        hb.append(h_last)
        tb.append(t_last)
    in_specs = [
        pl.BlockSpec((1, tm, D), lambda i, b, j, hb, tb: (b, i, 0)),
        pl.BlockSpec((1, 1, D), lambda i, b, j, hb, tb: (b, 0, 0)),
        pl.BlockSpec((1, 1, D), lambda i, b, j, hb, tb: (b, 0, 0)),
        pl.BlockSpec((D, tn), lambda i, b, j, hb, tb: (0, hb[j])),
        pl.BlockSpec((D, tn), lambda i, b, j, hb, tb: (0, tb[j])),
        pl.BlockSpec((N_GATES, D), lambda i, b, j, hb, tb: (0, 0)),
        pl.BlockSpec((N_GATES, 1), lambda i, b, j, hb, tb: (0, 0)),
    ]
    args = [x, sh, sc, w_head, w_tail, w_gate_t, b_gate] + slabs
    in_specs += [slab(s) for s in slabs]
    if tables is not None:
        in_specs += [pl.BlockSpec((tm, HEAD_W), lambda i, b, j, hb, tb: (i, 0))] * 2
        args += list(tables)
    grid_spec = pltpu.PrefetchScalarGridSpec(
        num_scalar_prefetch=2,
        grid=(T // tm, B, n_sec),
        in_specs=in_specs,
        out_specs=(pl.BlockSpec((1, tm, tn), lambda i, b, j, hb, tb: (b, i, j)),
                   pl.BlockSpec((1, N_GATES, tm), lambda i, b, j, hb, tb: (b, 0, i)))
        + tuple(slab(s) for s in slabs),
        scratch_shapes=[pltpu.VMEM((tm, D), BF16)],
    )
    outs = pl.pallas_call(
        functools.partial(_proj_kernel, kinds, srcs, len(slabs)),
        out_shape=(jax.ShapeDtypeStruct((B, T, n_sec * tn), BF16),
                   jax.ShapeDtypeStruct((B, N_GATES, T), F32))
        + tuple(jax.ShapeDtypeStruct(s.shape, BF16) for s in slabs),
        grid_spec=grid_spec,
        compiler_params=_cparams("parallel", "parallel", "arbitrary"),
        name="proj_lat" if tables is not None else "proj_ctx",
    )(jnp.asarray(hb, I32), jnp.asarray(tb, I32), *args)
    return outs[0], outs[1], [o.reshape(a.shape) for o, a in zip(outs[2:], side_cast)]


def _ret_build(dl_ref, q_ref, k_ref, v_ref, rg_ref, ck_ref, cv_ref, o_ref,
               sf_ref, sb_ref, fs_ref, bs_ref, dec_ref, d_ref):
    h = pl.program_id(0)
    L = SCAN_L
    n_chunks = q_ref.shape[1] // L
    n_ctx_chunks = ck_ref.shape[1] // L
    lgf = _log_sigmoid(jnp.full((1, 1), dl_ref[0, h], F32))
    lgb = _log_sigmoid(jnp.full((1, 1), dl_ref[1, h], F32))

    @pl.when(pl.program_id(1) == 0)
    def _():
        ri = lax.broadcasted_iota(I32, (L, L), 0)
        ci = lax.broadcasted_iota(I32, (L, L), 1)
        rel = (ri - ci).astype(F32)
        d_ref[...] = jnp.where(rel >= 0.0, jnp.exp(jnp.maximum(rel, 0.0) * lgf),
                               jnp.exp(jnp.maximum(-rel, 0.0) * lgb))
        row = lax.broadcasted_iota(I32, (L, HEAD_W), 0).astype(F32)
        dec_ref[0] = jnp.exp((row + 1.0) * lgf)
        dec_ref[1] = jnp.exp((L - 1.0 - row) * lgf)
        dec_ref[2] = jnp.exp((L - row) * lgb)
        dec_ref[3] = jnp.exp(row * lgb)

    cdf = jnp.exp(L * lgf)
    cdb = jnp.exp(L * lgb)

    def update(s_ref, kc, vc, kd, cd):
        kdec = (kc.astype(F32) * kd).astype(BF16)
        s_ref[...] = s_ref[...] * cd + lax.dot_general(kdec, vc, TN_DIMS, preferred_element_type=F32)

    sf_ref[...] = jnp.zeros_like(sf_ref)
    sb_ref[...] = jnp.zeros_like(sb_ref)
    for c in range(n_ctx_chunks):
        update(sf_ref, ck_ref[0, c * L:(c + 1) * L, :], cv_ref[0, c * L:(c + 1) * L, :], dec_ref[1], cdf)
    for c in reversed(range(n_ctx_chunks)):
        update(sb_ref, ck_ref[0, c * L:(c + 1) * L, :], cv_ref[0, c * L:(c + 1) * L, :], dec_ref[3], cdb)

    def state_pass(i, carry):
        cb = n_chunks - 1 - i
        rf = pl.multiple_of(i * L, L)
        rb = pl.multiple_of(cb * L, L)
        fs_ref[i] = sf_ref[...].astype(BF16)
        bs_ref[cb] = sb_ref[...].astype(BF16)
        update(sf_ref, k_ref[0, pl.ds(rf, L), :], v_ref[0, pl.ds(rf, L), :], dec_ref[1], cdf)
        update(sb_ref, k_ref[0, pl.ds(rb, L), :], v_ref[0, pl.ds(rb, L), :], dec_ref[3], cdb)
        return carry

    def finish_states():
        fs_ref[n_chunks - 1] = sf_ref[...].astype(BF16)
        bs_ref[0] = sb_ref[...].astype(BF16)

    def out_chunk(c):
        r0 = pl.multiple_of(c * L, L)
        q = q_ref[0, pl.ds(r0, L), :]
        k = k_ref[0, pl.ds(r0, L), :]
        v = v_ref[0, pl.ds(r0, L), :]
        s = lax.dot_general(q, k, NT_DIMS, preferred_element_type=F32)
        att = (s * d_ref[...]).astype(BF16)
        o = jnp.dot(att, v, preferred_element_type=F32)
        o = o + jnp.dot(q, fs_ref[c], preferred_element_type=F32) * dec_ref[0]
        o = o + jnp.dot(q, bs_ref[c], preferred_element_type=F32) * dec_ref[2]
        rg = rg_ref[0, pl.ds(r0, L), :].astype(F32)
        o_ref[0, pl.ds(r0, L), :] = (_layer_norm(o) * _silu(rg)).astype(BF16)

    return state_pass, finish_states, out_chunk


def _ret_scratch(n_chunks):
    return [pltpu.VMEM((HEAD_W, HEAD_W), F32),
            pltpu.VMEM((HEAD_W, HEAD_W), F32),
            pltpu.VMEM((n_chunks, HEAD_W, HEAD_W), BF16),
            pltpu.VMEM((n_chunks, HEAD_W, HEAD_W), BF16),
            pltpu.VMEM((4, SCAN_L, HEAD_W), F32),
            pltpu.VMEM((SCAN_L, SCAN_L), F32)]


def _mlstm_build(qp_ref, kp_ref, v_ref, mo_ref, ckp_ref, cv_ref, gt_ref, cgt_ref,
                 wq_ref, bq_ref, wk_ref, bk_ref, o_ref,
                 tab_ref, row_ref,
                 cf_ref, mf_ref, cb_ref, mb_ref, cfs_ref, mfs_ref, cbs_ref, mbs_ref, mask_ref,
                 xk_ref, xq_ref, q_ref, k_ref, ck_ref):
    L = SCAN_L
    T = qp_ref.shape[1]
    Tc = ckp_ref.shape[1]
    n_chunks = T // L
    n_ctx_chunks = Tc // L
    CV = CONV_ROWS

    def conv_stage(src_ref, xs_ref, t_len):
        xs_ref[pl.ds(0, 8), :] = jnp.zeros((8, HEAD_W), F32)
        xs_ref[pl.ds(8 + t_len, 8), :] = jnp.zeros((8, HEAD_W), F32)
        xs_ref[pl.ds(8, t_len), :] = src_ref[0].astype(F32)

    def conv_rows(xs_ref, c, w_ref, b_ref, dst_ref, scale):
        w = w_ref[...]
        r0 = pl.multiple_of(c * CV, CV)
        win = xs_ref[pl.ds(r0, CV + 16), :]
        prev = pltpu.roll(win, 1, 0)[8:8 + CV, :]
        cur = win[8:8 + CV, :]
        nxt = pltpu.roll(win, CV + 15, 0)[8:8 + CV, :]
        y = _silu(prev * w[0:1, :] + cur * w[1:2, :] + nxt * w[2:3, :] + b_ref[...])
        if scale != 1.0:
            y = y * scale
        dst_ref[pl.ds(r0, CV), :] = y.astype(BF16)

    per_chunk = L // CV

    def conv_k(c):
        for u in range(per_chunk):
            conv_rows(xk_ref, c * per_chunk + u, wk_ref, bk_ref, k_ref, KEY_SCALE)

    def conv_q(c):
        for u in range(per_chunk):
            conv_rows(xq_ref, c * per_chunk + u, wq_ref, bq_ref, q_ref, 1.0)

    conv_stage(ckp_ref, xk_ref, Tc)
    for c in range(Tc // CV):
        conv_rows(xk_ref, c, wk_ref, bk_ref, ck_ref, KEY_SCALE)
    conv_stage(kp_ref, xk_ref, T)
    conv_stage(qp_ref, xq_ref, T)
    conv_k(0)
    conv_k(n_chunks - 1)
    n_k_steps = n_chunks // 2 - 1

    def conv_step(i, keys):
        if keys:
            conv_k(i + 1)
            conv_k(n_chunks - 2 - i)
        else:
            conv_q(2 * (i - n_k_steps))
            conv_q(2 * (i - n_k_steps) + 1)

    ri = lax.broadcasted_iota(I32, (L, L), 0)
    ci = lax.broadcasted_iota(I32, (L, L), 1)
    tri_u = (ri <= ci).astype(BF16)
    lane8 = lax.broadcasted_iota(I32, (8, L), 1)
    sub8 = lax.broadcasted_iota(I32, (8, L), 0)
    sel_r = lax.broadcasted_iota(I32, (24, 8 * 128), 0) % 8
    sel_c = lax.broadcasted_iota(I32, (24, 8 * 128), 1) // 128
    sel3 = (sel_r == sel_c).astype(BF16)
    ones_cols = jnp.ones((L, AUG_W - HEAD_W), BF16)

    def chunk_tables(g8, n_used, state_only):
        i_f, i_b = g8[0], g8[2]
        lf_f, lf_b = _log_sigmoid(g8[1]), _log_sigmoid(g8[3])
        cs3 = jnp.dot(_split3(jnp.concatenate([lf_f, lf_b], axis=0)), tri_u,
                      preferred_element_type=F32)
        cs = cs3[0:16] + cs3[16:32] + cs3[32:48]
        b_f = cs[0:8]
        b_b = cs[8:16, L - 1:L] - cs[8:16] + lf_b
        z_f = i_f - b_f
        z_b = i_b - b_b
        g_f = b_f[:, L - 1:L] - b_f + i_f
        g_b = b_b[:, 0:1] - b_b + i_b
        mf, mb = z_f, z_b
        s = 1
        while s < L:
            mf = jnp.maximum(mf, jnp.where(lane8 >= s, pltpu.roll(mf, s, 1), NEG_INF))
            mb = jnp.maximum(mb, jnp.where(lane8 < L - s, pltpu.roll(mb, L - s, 1), NEG_INF))
            s *= 2
        mb = jnp.where(lane8 < L - 1, pltpu.roll(mb, L - 1, 1), NEG_INF)
        reps = [None if state_only and t not in (2, 5) else
                lax.dot_general(_split3(val), sel3[:, 0:n_used * 128], TN_DIMS, preferred_element_type=F32)
                for t, val in enumerate((mf, b_f, g_f, mb, b_b, g_b))]

        def rows_of(c):
            out = jnp.zeros((8, L), F32)
            for r, val in enumerate((z_f, z_b, g_f, g_b, b_f, b_b)):
                out = jnp.where(sub8 == r, val[c:c + 1], out)
            return out

        return rows_of, reps

    lat_rows, lat_reps = chunk_tables(gt_ref[0, 0], n_chunks, False)
    for c in range(n_chunks):
        row_ref[c] = lat_rows(c)
        for t in range(N_TAB):
            tab_ref[t, c * L:(c + 1) * L, :] = lat_reps[t][:, c * 128:(c + 1) * 128]

    def lanes2(x):
        return jnp.concatenate([x, x], axis=1)

    def advance(k, v, g_rep, g_row, b_last, c_ref, m_ref):
        m = m_ref[...]
        m_new = jnp.maximum(b_last + m, jnp.max(g_row, axis=-1, keepdims=True))
        kw = (k.astype(F32) * jnp.exp(lanes2(g_rep) - m_new)).astype(BF16)
        v_aug = jnp.concatenate([v, ones_cols], axis=1)
        c_ref[...] = jnp.exp(b_last + m - m_new) * c_ref[...] + lax.dot_general(
            kw, v_aug, TN_DIMS, preferred_element_type=F32)
        m_ref[...] = m_new

    for r in (cf_ref, mf_ref, cb_ref, mb_ref):
        r[...] = jnp.zeros_like(r)
    ctx_rows, ctx_reps = chunk_tables(cgt_ref[0, 0], n_ctx_chunks, True)
    for c in range(n_ctx_chunks):
        rows = ctx_rows(c)
        advance(ck_ref[c * L:(c + 1) * L, :], cv_ref[0, c * L:(c + 1) * L, :],
                ctx_reps[2][:, c * 128:(c + 1) * 128], rows[2:3], rows[4:5, L - 1:L], cf_ref, mf_ref)
    for c in reversed(range(n_ctx_chunks)):
        rows = ctx_rows(c)
        advance(ck_ref[c * L:(c + 1) * L, :], cv_ref[0, c * L:(c + 1) * L, :],
                ctx_reps[5][:, c * 128:(c + 1) * 128], rows[3:4], rows[5:6, 0:1], cb_ref, mb_ref)

    def state_pass(i, carry, keys):
        cb = n_chunks - 1 - i
        rf = pl.multiple_of(i * L, L)
        rb = pl.multiple_of(cb * L, L)
        cfs_ref[i] = cf_ref[...].astype(BF16)
        mfs_ref[i] = mf_ref[...]
        cbs_ref[cb] = cb_ref[...].astype(BF16)
        mbs_ref[cb] = mb_ref[...]
        rows_f = row_ref[i]
        rows_b = row_ref[cb]
        advance(k_ref[pl.ds(rf, L), :], v_ref[0, pl.ds(rf, L), :], tab_ref[2, pl.ds(rf, L), :],
                rows_f[2:3], rows_f[4:5, L - 1:L], cf_ref, mf_ref)
        advance(k_ref[pl.ds(rb, L), :], v_ref[0, pl.ds(rb, L), :], tab_ref[5, pl.ds(rb, L), :],
                rows_b[3:4], rows_b[5:6, 0:1], cb_ref, mb_ref)
        conv_step(i, keys)
        return carry

    def finish_states():
        cfs_ref[n_chunks - 1] = cf_ref[...].astype(BF16)
        mfs_ref[n_chunks - 1] = mf_ref[...]
        cbs_ref[0] = cb_ref[...].astype(BF16)
        mbs_ref[0] = mb_ref[...]

    def direction(q, v_aug, s, z_row, zmax_rep, b_rep, mask, c_in, m_in):
        mx = jnp.maximum(zmax_rep, m_in)
        att = s * jnp.exp((z_row - lanes2(mx)) + mask)
        na = jnp.dot(att.astype(BF16), v_aug, preferred_element_type=F32)
        qa = jnp.dot(q, c_in, preferred_element_type=F32)
        a = jnp.exp(m_in - mx)
        num = na[:, 0:HEAD_W] + lanes2(a) * qa[:, 0:HEAD_W]
        den = na[:, HEAD_W:] + a * qa[:, HEAD_W:]
        scale = 1.0 / jnp.maximum(jnp.abs(den), jnp.exp(-(b_rep + mx)))
        return num * lanes2(scale)

    @pl.when(pl.program_id(1) == 0)
    def _():
        mask_ref[0] = jnp.where(ci <= ri, 0.0, NEG_INF)
        mask_ref[1] = jnp.where(ci > ri, 0.0, NEG_INF)

    def out_chunk(c):
        r0 = pl.multiple_of(c * L, L)
        q = q_ref[pl.ds(r0, L), :]
        k = k_ref[pl.ds(r0, L), :]
        v_aug = jnp.concatenate([v_ref[0, pl.ds(r0, L), :], ones_cols], axis=1)
        s = lax.dot_general(q, k, NT_DIMS, preferred_element_type=F32)
        rows = row_ref[c]
        tot = direction(q, v_aug, s, rows[0:1], tab_ref[0, pl.ds(r0, L), :], tab_ref[1, pl.ds(r0, L), :],
                        mask_ref[0], cfs_ref[c], mfs_ref[c])
        tot = tot + direction(q, v_aug, s, rows[1:2], tab_ref[3, pl.ds(r0, L), :],
                              tab_ref[4, pl.ds(r0, L), :], mask_ref[1], cbs_ref[c], mbs_ref[c])
        mo = mo_ref[0, pl.ds(r0, L), :].astype(F32)
        o_ref[0, pl.ds(r0, L), :] = (_layer_norm(tot) * jax.nn.sigmoid(mo)).astype(BF16)

    return state_pass, finish_states, out_chunk, n_k_steps


def _mlstm_scratch(T, Tc, n_chunks):
    state = [pltpu.VMEM((HEAD_W, AUG_W), F32), pltpu.VMEM((1, 1), F32)]
    snaps = [pltpu.VMEM((n_chunks, HEAD_W, AUG_W), BF16), pltpu.VMEM((n_chunks, 1, 1), F32)]
    return [pltpu.VMEM((N_TAB, T, 128), F32), pltpu.VMEM((n_chunks, 8, SCAN_L), F32)] \
        + state + state + snaps + snaps + [pltpu.VMEM((2, SCAN_L, SCAN_L), F32)] \
        + [pltpu.VMEM((T + 16, HEAD_W), F32), pltpu.VMEM((T + 16, HEAD_W), F32),
           pltpu.VMEM((T, HEAD_W), BF16), pltpu.VMEM((T, HEAD_W), BF16), pltpu.VMEM((Tc, HEAD_W), BF16)]


def _scan_kernel(n_ret_scratch, dl_ref, rq_ref, rk_ref, rv_ref, rg_ref, rck_ref, rcv_ref,
                 mq_ref, mk_ref, mv_ref, mo_ref, mck_ref, mcv_ref, gt_ref, cgt_ref,
                 wq_ref, bq_ref, wk_ref, bk_ref, r_ref, m_ref, *scratch):
    n_chunks = rq_ref.shape[1] // SCAN_L
    ret = _ret_build(dl_ref, rq_ref, rk_ref, rv_ref, rg_ref, rck_ref, rcv_ref, r_ref,
                     *scratch[:n_ret_scratch])
    mls = _mlstm_build(mq_ref, mk_ref, mv_ref, mo_ref, mck_ref, mcv_ref, gt_ref, cgt_ref,
                       wq_ref, bq_ref, wk_ref, bk_ref, m_ref, *scratch[n_ret_scratch:])

    def state_pass(i, carry, keys):
        ret[0](i, carry)
        mls[0](i, carry, keys)
        return carry

    n_k_steps = mls[3]
    lax.fori_loop(0, n_k_steps, functools.partial(state_pass, keys=True), 0)
    lax.fori_loop(n_k_steps, n_chunks - 1, functools.partial(state_pass, keys=False), 0)
    ret[1]()
    mls[1]()

    def out_pass(i, carry):
        for c in (2 * i, 2 * i + 1):
            ret[2](c)
            mls[2](c)
        return carry

    lax.fori_loop(0, n_chunks // 2, out_pass, 0)


def _scans(decay_logit, p_lat, p_ctx, gt, cgt, conv_w, conv_b, ret_lat, ret_ctx, ml_lat, ml_ctx):
    B, T, _ = p_lat.shape
    Tc = p_ctx.shape[1]
    assert T % (2 * SCAN_L) == 0 and Tc % SCAN_L == 0 and T // SCAN_L <= 8
    n_chunks = T // SCAN_L

    def lat(sec):
        return pl.BlockSpec((1, T, HEAD_W), lambda h, b: (b, 0, sec * HEADS + h))

    def cx(sec):
        return pl.BlockSpec((1, Tc, HEAD_W), lambda h, b: (b, 0, sec * HEADS + h))

    gates = pl.BlockSpec((1, 1, N_GK, 8, SCAN_L), lambda h, b: (b, h, 0, 0, 0))
    out = pl.BlockSpec((1, T, HEAD_W), lambda h, b: (b, 0, h))
    conv_specs = [pl.BlockSpec((3, HEAD_W), lambda h, b: (0, h)),
                  pl.BlockSpec((1, HEAD_W), lambda h, b: (0, h)),
                  pl.BlockSpec((3, HEAD_W), lambda h, b: (0, HEADS + h)),
                  pl.BlockSpec((1, HEAD_W), lambda h, b: (0, HEADS + h))]
    ret_scratch = _ret_scratch(n_chunks)
    return pl.pallas_call(
        functools.partial(_scan_kernel, len(ret_scratch)),
        out_shape=(jax.ShapeDtypeStruct((B, T, BRANCH_W), BF16),
                   jax.ShapeDtypeStruct((B, T, BRANCH_W), BF16)),
        grid=(HEADS, B),
        in_specs=[pl.BlockSpec(memory_space=pltpu.SMEM)]
        + [lat(s) for s in ret_lat] + [cx(s) for s in ret_ctx]
        + [lat(s) for s in ml_lat] + [cx(s) for s in ml_ctx] + [gates, gates] + conv_specs,
        out_specs=(out, out),
        scratch_shapes=ret_scratch + _mlstm_scratch(T, Tc, n_chunks),
        compiler_params=_cparams("arbitrary", "arbitrary"),
        name="scans",
    )(decay_logit, *([p_lat] * 4), *([p_ctx] * 2), *([p_lat] * 4), *([p_ctx] * 2), gt, cgt,
      conv_w, conv_b, conv_w, conv_b)


def _merge_kernel(alpha, r_ref, m_ref, gr_ref, gm_ref, x_ref, g1_ref, sh2_ref, sc2_ref,
                  lng_ref, lnb_ref, wr_ref, wm_ref, wo_ref, wrt_ref, brt_ref, e1_ref, e3_ref, e2_ref,
                  x1_ref, ua_ref, ub_ref, ri_ref, rw_ref, cnt_ref, e1b_ref, e3b_ref, e2b_ref,
                  carry_ref, u_ref):
    for src, dst in ((e1_ref, e1b_ref), (e3_ref, e3b_ref), (e2_ref, e2b_ref)):
        dst[...] = src[...].astype(BF16)

    @pl.when(jnp.logical_and(pl.program_id(0) == 0, pl.program_id(1) == 0))
    def _():
        carry_ref[...] = jnp.zeros_like(carry_ref)
        tm = x_ref.shape[1]
        r = lax.broadcasted_iota(I32, (tm, tm), 0)
        c = lax.broadcasted_iota(I32, (tm, tm), 1)
        u_ref[...] = (r < c).astype(BF16)

    yr = jnp.dot(r_ref[0], wr_ref[...], preferred_element_type=F32)
    ym = jnp.dot(m_ref[0], wm_ref[...], preferred_element_type=F32)
    y = jax.nn.sigmoid(gr_ref[0].astype(F32)) * yr + jax.nn.sigmoid(gm_ref[0].astype(F32)) * ym
    yo = jnp.dot(y.astype(BF16), wo_ref[...], preferred_element_type=F32)
    x1 = _layer_norm(alpha * x_ref[0] + g1_ref[0] * yo) * lng_ref[...] + lnb_ref[...]
    x1_ref[0] = x1
    u2 = _layer_norm(x1) * (1.0 + sc2_ref[0]) + sh2_ref[0]
    ua_ref[0] = _pack_pairs(u2[:, 0:PACK_W], u2[:, PACK_W:2 * PACK_W])
    ub_ref[0] = _pack_pairs(u2[:, 2 * PACK_W:3 * PACK_W], u2[:, 3 * PACK_W:4 * PACK_W])
    lt = lax.dot_general(wrt_ref[...], u2.astype(BF16), NT_DIMS, preferred_element_type=F32) + brt_ref[...]
    _route_tile(lt, ri_ref, rw_ref, cnt_ref, carry_ref, u_ref)


def _route_tile(lt, ri_ref, rw_ref, cnt_ref, carry_ref, u_ref):
    tm = lt.shape[1]
    lg = lt[0:N_GROUPS, :]
    eg = jnp.exp(lg - jnp.max(lg, axis=0, keepdims=True))
    pg = eg / jnp.sum(eg, axis=0, keepdims=True)
    pg_top = jnp.max(pg, axis=0, keepdims=True)
    rows_g = lax.broadcasted_iota(I32, pg.shape, 0)
    g_idx = jnp.min(jnp.where(pg == pg_top, rows_g, N_GROUPS), axis=0, keepdims=True)

    le = jnp.zeros((EXP_PER_GROUP, tm), F32)
    for g in range(N_GROUPS):
        lo = 8 + g * EXP_PER_GROUP
        le = jnp.where(g_idx == g, lt[lo:lo + EXP_PER_GROUP, :], le)
    ee = jnp.exp(le - jnp.max(le, axis=0, keepdims=True))
    pe = ee / jnp.sum(ee, axis=0, keepdims=True)
    rows_e = lax.broadcasted_iota(I32, pe.shape, 0)
    v1 = jnp.max(pe, axis=0, keepdims=True)
    i1 = jnp.min(jnp.where(pe == v1, rows_e, EXP_PER_GROUP), axis=0, keepdims=True)
    pe2 = jnp.where(rows_e == i1, -1.0, pe)
    v2 = jnp.max(pe2, axis=0, keepdims=True)
    i2 = jnp.min(jnp.where(pe2 == v2, rows_e, EXP_PER_GROUP), axis=0, keepdims=True)
    den = v1 + v2
    rw_ref[...] = jnp.zeros_like(rw_ref)
    rw_ref[0:1, :] = pg_top * v1 / den
    rw_ref[1:2, :] = pg_top * v2 / den
    e1 = g_idx * EXP_PER_GROUP + i1
    e2 = g_idx * EXP_PER_GROUP + i2

    rows_x = lax.broadcasted_iota(I32, (N_EXPERTS, tm), 0)
    oh1 = (rows_x == e1).astype(F32)
    oh2 = (rows_x == e2).astype(F32)
    both = oh1 + oh2
    before = carry_ref[:, 0:1] + jnp.dot(both.astype(BF16), u_ref[...], preferred_element_type=F32)
    ri_ref[0:1, :] = e1
    ri_ref[1:2, :] = e2
    ri_ref[2:3, :] = jnp.sum(oh1 * before, axis=0, keepdims=True).astype(I32)
    ri_ref[3:4, :] = jnp.sum(oh2 * before, axis=0, keepdims=True).astype(I32)
    carry_ref[...] = carry_ref[...] + jnp.sum(both, axis=1, keepdims=True)
    cnt_ref[...] = carry_ref[...].astype(I32)


def _merge(alpha, r, m, p_lat, sec_gates, x, g1, sh2, sc2, lng, lnb, wr, wm, wo, wrt, brt, expert_w):
    B, T, D = x.shape
    tm = MERGE_TM
    per_b = T // tm
    n = B * T
    n_steps = B * per_b
    sliced = [w.reshape(n_steps, w.shape[0] * w.shape[1] // n_steps, w.shape[2]) for w in expert_w]
    assert all(s.shape[1] % 16 == 0 and s.size == w.size for s, w in zip(sliced, expert_w))

    def slab(s):
        return pl.BlockSpec((1,) + s.shape[1:], lambda b, i: (b * per_b + i, 0, 0))

    def tile(w):
        return pl.BlockSpec((1, tm, w), lambda b, i: (b, i, 0))

    def sec(s):
        return pl.BlockSpec((1, tm, BRANCH_W), lambda b, i: (b, i, s))

    def mod():
        return pl.BlockSpec((1, 1, D), lambda b, i: (b, 0, 0))

    def const(shape):
        return pl.BlockSpec(shape, lambda b, i: (0,) * len(shape))

    outs = pl.pallas_call(
        functools.partial(_merge_kernel, alpha),
        out_shape=(jax.ShapeDtypeStruct((B, T, D), F32),
                   jax.ShapeDtypeStruct((B, T, PACK_W), U32),
                   jax.ShapeDtypeStruct((B, T, PACK_W), U32),
                   jax.ShapeDtypeStruct((4, n), I32),
                   jax.ShapeDtypeStruct((8, n), F32),
                   jax.ShapeDtypeStruct((N_EXPERTS, 128), I32))
        + tuple(jax.ShapeDtypeStruct(s.shape, BF16) for s in sliced),
        grid=(B, per_b),
        in_specs=[tile(BRANCH_W), tile(BRANCH_W), sec(sec_gates[0]), sec(sec_gates[1]), tile(D),
                  mod(), mod(), mod(), const((1, D)), const((1, D)),
                  const((BRANCH_W, D)), const((BRANCH_W, D)), const((D, D)),
                  const((ROUTE_ROWS, D)), const((ROUTE_ROWS, 1))] + [slab(s) for s in sliced],
        out_specs=(tile(D), tile(PACK_W), tile(PACK_W),
                   pl.BlockSpec((4, tm), lambda b, i: (0, b * per_b + i)),
                   pl.BlockSpec((8, tm), lambda b, i: (0, b * per_b + i)),
                   pl.BlockSpec((N_EXPERTS, 128), lambda b, i: (0, 0)))
        + tuple(slab(s) for s in sliced),
        scratch_shapes=[pltpu.VMEM((N_EXPERTS, 128), F32), pltpu.VMEM((tm, tm), BF16)],
        compiler_params=_cparams("arbitrary", "arbitrary"),
        name="merge",
    )(r, m, p_lat, p_lat, x, g1, sh2, sc2, lng, lnb, wr, wm, wo, wrt, brt, *sliced)
    return outs[:6] + tuple(o.reshape(w.shape) for o, w in zip(outs[6:], expert_w))


def _sc_mesh():
    return plsc.VectorSubcoreMesh(core_axis_name="c", subcore_axis_name="s")


def _sc_scatter2(rows_a, rows_b, idx0, idx1, n_out):
    m, w = rows_a.shape
    out = jax.ShapeDtypeStruct((n_out, w), rows_a.dtype)

    @functools.partial(pl.kernel, out_type=(out, out), mesh=_sc_mesh(), scratch_types=[])
    def k(xa_hbm, xb_hbm, i0_hbm, i1_hbm, oa_hbm, ob_hbm):
        for x_hbm, o_hbm in ((xa_hbm, oa_hbm), (xb_hbm, ob_hbm)):
            def body(x_vmem, i0_vmem, i1_vmem, o_hbm=o_hbm):
                pltpu.sync_copy(x_vmem, o_hbm.at[i0_vmem.at[0]])
                pltpu.sync_copy(x_vmem, o_hbm.at[i1_vmem.at[0]])

            pltpu.emit_pipeline(
                body,
                grid=(m // SC_WIN,),
                in_specs=[pl.BlockSpec((SC_WIN, w), lambda i: (i, 0)),
                          pl.BlockSpec((1, SC_WIN), lambda i: (0, i)),
                          pl.BlockSpec((1, SC_WIN), lambda i: (0, i))],
                out_specs=[],
                core_axis_name=("c", "s"),
                dimension_semantics=(pltpu.PARALLEL,),
            )(x_hbm, i0_hbm, i1_hbm)

    return k(rows_a, rows_b, idx0.reshape(1, m), idx1.reshape(1, m))


def _sc_gather(table_a, table_b, idx):
    m = idx.shape[0]
    w = table_a.shape[1]
    out = jax.ShapeDtypeStruct((m, w), table_a.dtype)

    @functools.partial(pl.kernel, out_type=(out, out), mesh=_sc_mesh(), scratch_types=[])
    def k(ta_hbm, tb_hbm, i_hbm, oa_hbm, ob_hbm):
        for t_hbm, o_hbm in ((ta_hbm, oa_hbm), (tb_hbm, ob_hbm)):
            def body(i_vmem, o_vmem, t_hbm=t_hbm):
                pltpu.sync_copy(t_hbm.at[i_vmem.at[0]], o_vmem)

            pltpu.emit_pipeline(
                body,
                grid=(m // SC_WIN,),
                in_specs=[pl.BlockSpec((1, SC_WIN), lambda i: (0, i))],
                out_specs=[pl.BlockSpec((SC_WIN, w), lambda i: (i, 0))],
                core_axis_name=("c", "s"),
                dimension_semantics=(pltpu.PARALLEL,),
            )(i_hbm, o_hbm)

    return k(table_a, table_b, idx.reshape(1, m))


def _expert_kernel(be_ref, nv_ref, xa_ref, xb_ref, w1_ref, w3_ref, w2_ref, ya_ref, yb_ref):
    j = pl.program_id(0)
    nv = nv_ref[j]

    @pl.when(nv > 0)
    def _():
        valid = lax.broadcasted_iota(I32, xa_ref.shape, 0) < nv
        zero = jnp.zeros(xa_ref.shape, U32)
        parts = _unpack_pairs(jnp.where(valid, xa_ref[...], zero)) + \
            _unpack_pairs(jnp.where(valid, xb_ref[...], zero))
        x = jnp.concatenate([p.astype(BF16) for p in parts], axis=1)
        h1 = jnp.dot(x, w1_ref[0], preferred_element_type=F32)
        h3 = jnp.dot(x, w3_ref[0], preferred_element_type=F32)
        y = jnp.dot((_silu(h1) * h3).astype(BF16), w2_ref[0], preferred_element_type=F32)
        ya_ref[...] = _pack_pairs(y[:, 0:PACK_W], y[:, PACK_W:2 * PACK_W])
        yb_ref[...] = _pack_pairs(y[:, 2 * PACK_W:3 * PACK_W], y[:, 3 * PACK_W:4 * PACK_W])

    @pl.when(nv == 0)
    def _():
        ya_ref[...] = jnp.zeros_like(ya_ref)
        yb_ref[...] = jnp.zeros_like(yb_ref)


def _experts(block_exp, n_valid, xa, xb, w1, w3, w2):
    n_slots = xa.shape[0]
    n_blocks = n_slots // MOE_BLK
    d, de = w1.shape[1], w1.shape[2]
    slot = pl.BlockSpec((MOE_BLK, PACK_W), lambda j, be, nv: (j, 0))
    grid_spec = pltpu.PrefetchScalarGridSpec(
        num_scalar_prefetch=2,
        grid=(n_blocks,),
        in_specs=[slot, slot,
                  pl.BlockSpec((1, d, de), lambda j, be, nv: (be[j], 0, 0)),
                  pl.BlockSpec((1, d, de), lambda j, be, nv: (be[j], 0, 0)),
                  pl.BlockSpec((1, de, d), lambda j, be, nv: (be[j], 0, 0))],
        out_specs=(slot, slot),
    )
    return pl.pallas_call(
        _expert_kernel,
        out_shape=(jax.ShapeDtypeStruct((n_slots, PACK_W), U32),
                   jax.ShapeDtypeStruct((n_slots, PACK_W), U32)),
        grid_spec=grid_spec,
        compiler_params=_cparams("parallel"),
        name="experts",
    )(block_exp, n_valid, xa, xb, w1, w3, w2)


def _final_kernel(alpha, x1_ref, a0_ref, b0_ref, a1_ref, b1_ref, w_ref, g2_ref, lng_ref, lnb_ref, o_ref):
    w = w_ref[...].T
    w0 = w[:, 0:1]
    w1 = w[:, 1:2]
    parts0 = _unpack_pairs(a0_ref[...]) + _unpack_pairs(b0_ref[...])
    parts1 = _unpack_pairs(a1_ref[...]) + _unpack_pairs(b1_ref[...])
    f = jnp.concatenate([w0 * p0 + w1 * p1 for p0, p1 in zip(parts0, parts1)], axis=1)
    o_ref[0] = _layer_norm(alpha * x1_ref[0] + g2_ref[0] * f) * lng_ref[...] + lnb_ref[...]


def _final(alpha, x1, ya, yb, w, g2, lng, lnb):
    B, T, D = x1.shape
    tm = min(FINAL_TM, T)
    per_b = T // tm
    n_tiles = B * per_b

    def rows(k):
        return pl.BlockSpec((tm, PACK_W), lambda b, i: (k * n_tiles + b * per_b + i, 0))

    return pl.pallas_call(
        functools.partial(_final_kernel, alpha),
        out_shape=jax.ShapeDtypeStruct((B, T, D), F32),
        grid=(B, per_b),
        in_specs=[pl.BlockSpec((1, tm, D), lambda b, i: (b, i, 0)),
                  rows(0), rows(0), rows(1), rows(1),
                  pl.BlockSpec((8, tm), lambda b, i: (0, b * per_b + i)),
                  pl.BlockSpec((1, 1, D), lambda b, i: (b, 0, 0)),
                  pl.BlockSpec((1, D), lambda b, i: (0, 0)),
                  pl.BlockSpec((1, D), lambda b, i: (0, 0))],
        out_specs=pl.BlockSpec((1, tm, D), lambda b, i: (b, i, 0)),
        compiler_params=_cparams("parallel", "parallel"),
        name="final",
    )(x1, ya, yb, ya, yb, w, g2, lng, lnb)


def _rotary_tables(T):
    quarter = HEAD_W // 4
    freqs = ROPE_BASE ** (-jnp.arange(quarter, dtype=F32) / quarter)
    t = jnp.arange(T)
    ang_r = (t // GRID_W).astype(F32)[:, None] * freqs[None, :]
    ang_c = (t % GRID_W).astype(F32)[:, None] * freqs[None, :]
    cos = jnp.concatenate([jnp.cos(ang_r)] * 2 + [jnp.cos(ang_c)] * 2, axis=1)
    sin = jnp.concatenate([-jnp.sin(ang_r), jnp.sin(ang_r), -jnp.sin(ang_c), jnp.sin(ang_c)], axis=1)
    return cos, sin


def _per_head_gates(gt):
    B, _, T = gt.shape
    n_chunks = T // SCAN_L
    gth = gt.reshape(B, N_GK, HEADS, n_chunks, SCAN_L).transpose(0, 2, 1, 3, 4)
    return jnp.pad(gth, ((0, 0), (0, 0), (0, 0), (0, 8 - n_chunks), (0, 0)))


def _table_lookup(table, idx):
    sel = idx[..., None] == jnp.arange(table.shape[0], dtype=idx.dtype)
    return jnp.sum(jnp.where(sel, table, 0), axis=-1)


def kernel(x, c, ctx, c_ctx, w_ada, b_ada, w_in, b_mgate, ml_conv_w, ml_conv_b, ret_decay_logit, w_ret_branch, w_ml_branch, w_out, ln1_g, ln1_b, w_rg, b_rg, w_re, b_re, w_e1, w_e3, w_e2, ln2_g, ln2_b):
    B, T, D = x.shape
    depth = w_ada.shape[0]
    assert depth == 1 and D == BRANCH_W and T % GRID_W == 0
    alpha = (2 * depth) ** 0.25
    n_tok = B * T

    n_rows = -(-(B + 1) // 8) * 8
    cs = jnp.zeros((n_rows, D), F32).at[:B].set(c).at[B].set(c_ctx)
    mod = _ada(cs, w_ada[0], b_ada[0][None, :])
    sh1, sc1, g1, sh2, sc2, g2 = [mod[:B, None, i * D:(i + 1) * D] for i in range(6)]
    csh1 = mod[B, 0 * D:1 * D].reshape(1, 1, D)
    csc1 = mod[B, 1 * D:2 * D].reshape(1, 1, D)

    w = w_in[0]
    g_lo = 8 * BRANCH_W
    w_gate_t = w[:, g_lo:g_lo + N_GATES].T.astype(BF16)
    b_gate = b_mgate[0][:, None]
    w_head = w[:, :g_lo].astype(BF16)
    w_tail = w[:, g_lo + N_GATES:].astype(BF16)
    sec_lat = tuple(("h", s) for s in range(8)) + (("t", 0), ("t", 1))
    sec_ctx = (("h", 1), ("h", 2), ("h", 5), ("h", 6))
    kinds_lat = ("rot", "rot_scale") + ("plain",) * 8
    kinds_ctx = ("scale", "plain", "plain", "plain")
    p_lat, gt_lat, _ = _proj(x, sh1, sc1, w_head, w_tail, sec_lat, w_gate_t, b_gate, kinds_lat,
                             _rotary_tables(T))
    Tc = ctx.shape[1]
    p_ctx, gt_ctx, (w_rb, w_mb, w_ob) = _proj(
        ctx.reshape(1, B * Tc, D), csh1, csc1, w_head, w_tail, sec_ctx, w_gate_t, b_gate, kinds_ctx,
        side_cast=(w_ret_branch[0], w_ml_branch[0], w_out[0]))
    p_ctx = p_ctx.reshape(B, Tc, -1)
    gt_ctx = gt_ctx.reshape(N_GATES, B, Tc).transpose(1, 0, 2)

    ret, mls = _scans(ret_decay_logit[0], p_lat, p_ctx, _per_head_gates(gt_lat), _per_head_gates(gt_ctx),
                      ml_conv_w[0], ml_conv_b[0][None, :], (0, 1, 2, 3), (0, 1), (4, 5, 6, 7), (2, 3))

    wrt = jnp.zeros((ROUTE_ROWS, D), F32).at[:N_GROUPS].set(w_rg[0].T).at[8:8 + N_EXPERTS].set(w_re[0].T)
    brt = jnp.zeros((ROUTE_ROWS, 1), F32).at[:N_GROUPS, 0].set(b_rg[0]).at[8:8 + N_EXPERTS, 0].set(b_re[0])
    x1, ua, ub, ri, rw, cnt, we1, we3, we2 = _merge(
        alpha, ret, mls, p_lat, (8, 9), x, g1, sh2, sc2, ln1_g[0][None, :], ln1_b[0][None, :],
        w_rb, w_mb, w_ob, wrt.astype(BF16), brt, (w_e1[0], w_e3[0], w_e2[0]))

    counts = cnt[:, 0]
    padded = (counts + MOE_BLK - 1) // MOE_BLK * MOE_BLK
    pad_end = jnp.cumsum(padded)
    pad_off = pad_end - padded
    dest = _table_lookup(pad_off, ri[0:2]) + ri[2:4]
    n_blocks = (2 * n_tok) // MOE_BLK + N_EXPERTS
    n_slots = n_blocks * MOE_BLK
    block_start = jnp.arange(n_blocks, dtype=I32) * MOE_BLK
    block_exp = jnp.minimum((block_start[:, None] >= pad_end[None, :]).sum(1), N_EXPERTS - 1).astype(I32)
    n_valid = jnp.clip(_table_lookup(counts, block_exp) - (block_start - _table_lookup(pad_off, block_exp)),
                       0, MOE_BLK).astype(I32)

    xa, xb = _sc_scatter2(ua.reshape(n_tok, PACK_W), ub.reshape(n_tok, PACK_W), dest[0], dest[1], n_slots)
    ya, yb = _experts(block_exp, n_valid, xa, xb, we1, we3, we2)
    ga, gb = _sc_gather(ya, yb, dest.reshape(2 * n_tok))
    return _final(alpha, x1, ga, gb, rw, g2, ln2_g[0][None, :], ln2_b[0][None, :])
```

```python
import functools

import jax
import jax.numpy as jnp
from jax import lax
from jax.experimental import pallas as pl
from jax.experimental.pallas import tpu as pltpu
from jax.experimental.pallas import tpu_sc as plsc

F32 = jnp.float32
BF16 = jnp.bfloat16
U32 = jnp.uint32
I32 = jnp.int32
HIGHEST = lax.Precision.HIGHEST

HEADS = 4
HEAD_W = 256
BRANCH_W = HEADS * HEAD_W
GRID_W = 64
ROPE_BASE = 10000.0
N_GATES = 16
N_GK = N_GATES // HEADS
N_GROUPS = 4
EXP_PER_GROUP = 8
N_EXPERTS = N_GROUPS * EXP_PER_GROUP
LN_EPS = 1e-5
NEG_INF = -1e30
KEY_SCALE = HEAD_W ** -0.5

SCAN_L = 256
CONV_ROWS = 128
PROJ_TM = 2048
PROJ_SUB = 256
MERGE_TM = 512
FINAL_TM = 1024
MOE_BLK = 512
SC_WIN = 128
PACK_W = 256
ROUTE_ROWS = 64
N_TAB = 6
AUG_W = HEAD_W + 128
VMEM_LIMIT = 48 * 1024 * 1024

NT_DIMS = (((1,), (1,)), ((), ()))
TN_DIMS = (((0,), (0,)), ((), ()))


def _cparams(*sem):
    return pltpu.CompilerParams(dimension_semantics=sem, vmem_limit_bytes=VMEM_LIMIT)


def _layer_norm(x):
    mu = jnp.mean(x, axis=-1, keepdims=True)
    xc = x - mu
    var = jnp.mean(xc * xc, axis=-1, keepdims=True)
    return xc * lax.rsqrt(var + LN_EPS)


def _log_sigmoid(x):
    return jnp.minimum(x, 0.0) - jnp.log1p(jnp.exp(-jnp.abs(x)))


def _silu(x):
    return x * jax.nn.sigmoid(x)


def _pack_pairs(hi, lo):
    hb = lax.bitcast_convert_type(hi.astype(BF16).astype(F32), U32)
    lb = lax.bitcast_convert_type(lo.astype(BF16).astype(F32), U32)
    return (hb & jnp.uint32(0xFFFF0000)) | (lb >> 16)


def _unpack_pairs(p):
    hi = lax.bitcast_convert_type(p & jnp.uint32(0xFFFF0000), F32)
    lo = lax.bitcast_convert_type(p << 16, F32)
    return hi, lo


def _split3(x):
    hi = x.astype(BF16).astype(F32)
    r1 = x - hi
    mid = r1.astype(BF16).astype(F32)
    lo = (r1 - mid).astype(BF16).astype(F32)
    return jnp.concatenate([hi, mid, lo], axis=0).astype(BF16)


def _ada_kernel(c_ref, w_ref, b_ref, o_ref):
    s = _silu(c_ref[...])
    o_ref[...] = jnp.dot(s, w_ref[...], precision=HIGHEST, preferred_element_type=F32) + b_ref[...]


def _ada(cs, w, b):
    rows, d = cs.shape
    cols = w.shape[1]
    tn = 1024
    return pl.pallas_call(
        _ada_kernel,
        out_shape=jax.ShapeDtypeStruct((rows, cols), F32),
        grid=(cols // tn,),
        in_specs=[pl.BlockSpec((rows, d), lambda j: (0, 0)),
                  pl.BlockSpec((d, tn), lambda j: (0, j)),
                  pl.BlockSpec((1, tn), lambda j: (0, j))],
        out_specs=pl.BlockSpec((rows, tn), lambda j: (0, j)),
        compiler_params=_cparams("parallel"),
        name="ada",
    )(cs, w, b)


def _proj_kernel(kinds, srcs, n_cast, hb_ref, tb_ref, x_ref, sh_ref, sc_ref, wh_ref, wt_ref, wg_ref,
                 bg_ref, *rest):
    cast_in, rest = rest[:n_cast], rest[n_cast:]
    if "rot" in kinds or "rot_scale" in kinds:
        cos_ref, sin_ref, rest = rest[0], rest[1], rest[2:]
    o_ref, gt_ref = rest[0], rest[1]
    cast_out, u_ref = rest[2:2 + n_cast], rest[2 + n_cast]
    for src, dst in zip(cast_in, cast_out):
        dst[...] = src[...].astype(BF16)
    j = pl.program_id(2)
    tm = x_ref.shape[1]
    sub = min(PROJ_SUB, tm)

    def rotary(acc, rows, scale):
        for s in range(acc.shape[1] // 128):
            a = acc[:, s * 128:(s + 1) * 128]
            half = s % 2
            cs = cos_ref[rows, half * 128:(half + 1) * 128]
            sn = sin_ref[rows, half * 128:(half + 1) * 128]
            r = a * cs + pltpu.roll(a, 64, 1) * sn
            if scale != 1.0:
                r = r * scale
            o_ref[0, rows, s * 128:(s + 1) * 128] = r.astype(BF16)

    def section(kind, first, src):
        w_ref = wh_ref if src == "h" else wt_ref
        for r in range(tm // sub):
            rows = slice(r * sub, (r + 1) * sub)
            if first:
                u = _layer_norm(x_ref[0, rows, :]) * (1.0 + sc_ref[0]) + sh_ref[0]
                ub = u.astype(BF16)
                u_ref[rows, :] = ub
                gt_ref[0, :, rows] = lax.dot_general(wg_ref[...], ub, NT_DIMS,
                                                     preferred_element_type=F32) + bg_ref[...]
            else:
                ub = u_ref[rows, :]
            acc = jnp.dot(ub, w_ref[...], preferred_element_type=F32)
            if kind == "rot":
                rotary(acc, rows, 1.0)
            elif kind == "rot_scale":
                rotary(acc, rows, KEY_SCALE)
            elif kind == "scale":
                o_ref[0, rows, :] = (acc * KEY_SCALE).astype(BF16)
            else:
                o_ref[0, rows, :] = acc.astype(BF16)

    variants = {}
    for s, key in enumerate(zip(kinds, srcs)):
        variants.setdefault(key + (s == 0,), []).append(s)
    for (kind, src, first), secs in variants.items():
        cond = functools.reduce(jnp.logical_or, [j == s for s in secs])

        @pl.when(cond)
        def _(kind=kind, first=first, src=src):
            section(kind, first, src)


def _proj(x, sh, sc, w_head, w_tail, sections, w_gate_t, b_gate, kinds, tables=None, side_cast=()):
    B, T, D = x.shape
    n_sec = len(kinds)
    tm = min(PROJ_TM, T)
    tn = BRANCH_W
    assert T % tm == 0
    n_steps = (T // tm) * B * n_sec
    slabs = [a.reshape(n_steps, a.shape[0] // n_steps, a.shape[1]) for a in side_cast]
    assert all(s.shape[1] % 16 == 0 and s.size == a.size for s, a in zip(slabs, side_cast))

    def slab(s):
        return pl.BlockSpec((1,) + s.shape[1:],
                            lambda i, b, j, hb, tb: ((i * B + b) * n_sec + j, 0, 0))

    srcs = tuple(src for src, _ in sections)
    hb, tb, h_last, t_last = [], [], 0, 0
    for src, blk in sections:
        h_last, t_last = (blk, t_last) if src == "h" else (h_last, blk)
        hb.append(h_last)
        tb.append(t_last)
    in_specs = [
        pl.BlockSpec((1, tm, D), lambda i, b, j, hb, tb: (b, i, 0)),
        pl.BlockSpec((1, 1, D), lambda i, b, j, hb, tb: (b, 0, 0)),
        pl.BlockSpec((1, 1, D), lambda i, b, j, hb, tb: (b, 0, 0)),
        pl.BlockSpec((D, tn), lambda i, b, j, hb, tb: (0, hb[j])),
        pl.BlockSpec((D, tn), lambda i, b, j, hb, tb: (0, tb[j])),
        pl.BlockSpec((N_GATES, D), lambda i, b, j, hb, tb: (0, 0)),
        pl.BlockSpec((N_GATES, 1), lambda i, b, j, hb, tb: (0, 0)),
    ]
    args = [x, sh, sc, w_head, w_tail, w_gate_t, b_gate] + slabs
    in_specs += [slab(s) for s in slabs]
    if tables is not None:
        in_specs += [pl.BlockSpec((tm, HEAD_W), lambda i, b, j, hb, tb: (i, 0))] * 2
        args += list(tables)
    grid_spec = pltpu.PrefetchScalarGridSpec(
        num_scalar_prefetch=2,
        grid=(T // tm, B, n_sec),
        in_specs=in_specs,
        out_specs=(pl.BlockSpec((1, tm, tn), lambda i, b, j, hb, tb: (b, i, j)),
                   pl.BlockSpec((1, N_GATES, tm), lambda i, b, j, hb, tb: (b, 0, i)))
        + tuple(slab(s) for s in slabs),
        scratch_shapes=[pltpu.VMEM((tm, D), BF16)],
    )
    outs = pl.pallas_call(
        functools.partial(_proj_kernel, kinds, srcs, len(slabs)),
        out_shape=(jax.ShapeDtypeStruct((B, T, n_sec * tn), BF16),
                   jax.ShapeDtypeStruct((B, N_GATES, T), F32))
        + tuple(jax.ShapeDtypeStruct(s.shape, BF16) for s in slabs),
        grid_spec=grid_spec,
        compiler_params=_cparams("parallel", "parallel", "arbitrary"),
        name="proj_lat" if tables is not None else "proj_ctx",
    )(jnp.asarray(hb, I32), jnp.asarray(tb, I32), *args)
    return outs[0], outs[1], [o.reshape(a.shape) for o, a in zip(outs[2:], side_cast)]


def _ret_build(dl_ref, q_ref, k_ref, v_ref, rg_ref, ck_ref, cv_ref, o_ref,
               sf_ref, sb_ref, fs_ref, bs_ref, dec_ref, d_ref):
    h = pl.program_id(0)
    L = SCAN_L
    n_chunks = q_ref.shape[1] // L
    n_ctx_chunks = ck_ref.shape[1] // L
    lgf = _log_sigmoid(jnp.full((1, 1), dl_ref[0, h], F32))
    lgb = _log_sigmoid(jnp.full((1, 1), dl_ref[1, h], F32))

    @pl.when(pl.program_id(1) == 0)
    def _():
        ri = lax.broadcasted_iota(I32, (L, L), 0)
        ci = lax.broadcasted_iota(I32, (L, L), 1)
        rel = (ri - ci).astype(F32)
        d_ref[...] = jnp.where(rel >= 0.0, jnp.exp(jnp.maximum(rel, 0.0) * lgf),
                               jnp.exp(jnp.maximum(-rel, 0.0) * lgb))
        row = lax.broadcasted_iota(I32, (L, HEAD_W), 0).astype(F32)
        dec_ref[0] = jnp.exp((row + 1.0) * lgf)
        dec_ref[1] = jnp.exp((L - 1.0 - row) * lgf)
        dec_ref[2] = jnp.exp((L - row) * lgb)
        dec_ref[3] = jnp.exp(row * lgb)

    cdf = jnp.exp(L * lgf)
    cdb = jnp.exp(L * lgb)

    def update(s_ref, kc, vc, kd, cd):
        kdec = (kc.astype(F32) * kd).astype(BF16)
        s_ref[...] = s_ref[...] * cd + lax.dot_general(kdec, vc, TN_DIMS, preferred_element_type=F32)

    sf_ref[...] = jnp.zeros_like(sf_ref)
    sb_ref[...] = jnp.zeros_like(sb_ref)
    for c in range(n_ctx_chunks):
        update(sf_ref, ck_ref[0, c * L:(c + 1) * L, :], cv_ref[0, c * L:(c + 1) * L, :], dec_ref[1], cdf)
    for c in reversed(range(n_ctx_chunks)):
        update(sb_ref, ck_ref[0, c * L:(c + 1) * L, :], cv_ref[0, c * L:(c + 1) * L, :], dec_ref[3], cdb)

    def state_pass(i, carry):
        cb = n_chunks - 1 - i
        rf = pl.multiple_of(i * L, L)
        rb = pl.multiple_of(cb * L, L)
        fs_ref[i] = sf_ref[...].astype(BF16)
        bs_ref[cb] = sb_ref[...].astype(BF16)
        update(sf_ref, k_ref[0, pl.ds(rf, L), :], v_ref[0, pl.ds(rf, L), :], dec_ref[1], cdf)
        update(sb_ref, k_ref[0, pl.ds(rb, L), :], v_ref[0, pl.ds(rb, L), :], dec_ref[3], cdb)
        return carry

    def finish_states():
        fs_ref[n_chunks - 1] = sf_ref[...].astype(BF16)
        bs_ref[0] = sb_ref[...].astype(BF16)

    def out_chunk(c):
        r0 = pl.multiple_of(c * L, L)
        q = q_ref[0, pl.ds(r0, L), :]
        k = k_ref[0, pl.ds(r0, L), :]
        v = v_ref[0, pl.ds(r0, L), :]
        s = lax.dot_general(q, k, NT_DIMS, preferred_element_type=F32)
        att = (s * d_ref[...]).astype(BF16)
        o = jnp.dot(att, v, preferred_element_type=F32)
        o = o + jnp.dot(q, fs_ref[c], preferred_element_type=F32) * dec_ref[0]
        o = o + jnp.dot(q, bs_ref[c], preferred_element_type=F32) * dec_ref[2]
        rg = rg_ref[0, pl.ds(r0, L), :].astype(F32)
        o_ref[0, pl.ds(r0, L), :] = (_layer_norm(o) * _silu(rg)).astype(BF16)

    return state_pass, finish_states, out_chunk


def _ret_scratch(n_chunks):
    return [pltpu.VMEM((HEAD_W, HEAD_W), F32),
            pltpu.VMEM((HEAD_W, HEAD_W), F32),
            pltpu.VMEM((n_chunks, HEAD_W, HEAD_W), BF16),
            pltpu.VMEM((n_chunks, HEAD_W, HEAD_W), BF16),
            pltpu.VMEM((4, SCAN_L, HEAD_W), F32),
            pltpu.VMEM((SCAN_L, SCAN_L), F32)]


def _mlstm_build(qp_ref, kp_ref, v_ref, mo_ref, ckp_ref, cv_ref, gt_ref, cgt_ref,
                 wq_ref, bq_ref, wk_ref, bk_ref, o_ref,
                 tab_ref, row_ref,
                 cf_ref, mf_ref, cb_ref, mb_ref, cfs_ref, mfs_ref, cbs_ref, mbs_ref, mask_ref,
                 xk_ref, xq_ref, q_ref, k_ref, ck_ref):
    L = SCAN_L
    T = qp_ref.shape[1]
    Tc = ckp_ref.shape[1]
    n_chunks = T // L
    n_ctx_chunks = Tc // L
    CV = CONV_ROWS

    def conv_stage(src_ref, xs_ref, t_len):
        xs_ref[pl.ds(0, 8), :] = jnp.zeros((8, HEAD_W), F32)
        xs_ref[pl.ds(8 + t_len, 8), :] = jnp.zeros((8, HEAD_W), F32)
        xs_ref[pl.ds(8, t_len), :] = src_ref[0].astype(F32)

    def conv_rows(xs_ref, c, w_ref, b_ref, dst_ref, scale):
        w = w_ref[...]
        r0 = pl.multiple_of(c * CV, CV)
        win = xs_ref[pl.ds(r0, CV + 16), :]
        prev = pltpu.roll(win, 1, 0)[8:8 + CV, :]
        cur = win[8:8 + CV, :]
        nxt = pltpu.roll(win, CV + 15, 0)[8:8 + CV, :]
        y = _silu(prev * w[0:1, :] + cur * w[1:2, :] + nxt * w[2:3, :] + b_ref[...])
        if scale != 1.0:
            y = y * scale
        dst_ref[pl.ds(r0, CV), :] = y.astype(BF16)

    per_chunk = L // CV

    def conv_k(c):
        for u in range(per_chunk):
            conv_rows(xk_ref, c * per_chunk + u, wk_ref, bk_ref, k_ref, KEY_SCALE)

    def conv_q(c):
        for u in range(per_chunk):
            conv_rows(xq_ref, c * per_chunk + u, wq_ref, bq_ref, q_ref, 1.0)

    conv_stage(ckp_ref, xk_ref, Tc)
    for c in range(Tc // CV):
        conv_rows(xk_ref, c, wk_ref, bk_ref, ck_ref, KEY_SCALE)
    conv_stage(kp_ref, xk_ref, T)
    conv_stage(qp_ref, xq_ref, T)
    conv_k(0)
    conv_k(n_chunks - 1)
    n_k_steps = n_chunks // 2 - 1

    def conv_step(i, keys):
        if keys:
            conv_k(i + 1)
            conv_k(n_chunks - 2 - i)
        else:
            conv_q(2 * (i - n_k_steps))
            conv_q(2 * (i - n_k_steps) + 1)

    ri = lax.broadcasted_iota(I32, (L, L), 0)
    ci = lax.broadcasted_iota(I32, (L, L), 1)
    tri_u = (ri <= ci).astype(BF16)
    lane8 = lax.broadcasted_iota(I32, (8, L), 1)
    sub8 = lax.broadcasted_iota(I32, (8, L), 0)
    sel_r = lax.broadcasted_iota(I32, (24, 8 * 128), 0) % 8
    sel_c = lax.broadcasted_iota(I32, (24, 8 * 128), 1) // 128
    sel3 = (sel_r == sel_c).astype(BF16)
    ones_cols = jnp.ones((L, AUG_W - HEAD_W), BF16)

    def chunk_tables(g8, n_used, state_only):
        i_f, i_b = g8[0], g8[2]
        lf_f, lf_b = _log_sigmoid(g8[1]), _log_sigmoid(g8[3])
        cs3 = jnp.dot(_split3(jnp.concatenate([lf_f, lf_b], axis=0)), tri_u,
                      preferred_element_type=F32)
        cs = cs3[0:16] + cs3[16:32] + cs3[32:48]
        b_f = cs[0:8]
        b_b = cs[8:16, L - 1:L] - cs[8:16] + lf_b
        z_f = i_f - b_f
        z_b = i_b - b_b
        g_f = b_f[:, L - 1:L] - b_f + i_f
        g_b = b_b[:, 0:1] - b_b + i_b
        mf, mb = z_f, z_b
        s = 1
        while s < L:
            mf = jnp.maximum(mf, jnp.where(lane8 >= s, pltpu.roll(mf, s, 1), NEG_INF))
            mb = jnp.maximum(mb, jnp.where(lane8 < L - s, pltpu.roll(mb, L - s, 1), NEG_INF))
            s *= 2
        mb = jnp.where(lane8 < L - 1, pltpu.roll(mb, L - 1, 1), NEG_INF)
        split = [_split3(val) for val in (mf, b_f, g_f, mb, b_b, g_b)]
        reps = [None if state_only and t not in (2, 5) else
                lax.dot_general(split[t], sel3[:, 0:n_used * 128], TN_DIMS, preferred_element_type=F32)
                for t in range(N_TAB)]

        def rows_of(c):
            out = jnp.zeros((8, L), F32)
            for r, val in enumerate((z_f, z_b, g_f, g_b, b_f, b_b)):
                out = jnp.where(sub8 == r, val[c:c + 1], out)
            return out

        return rows_of, reps, split

    lat_rows, lat_reps, lat_split = chunk_tables(gt_ref[0, 0], n_chunks, True)
    for c in range(n_chunks):
        row_ref[c] = lat_rows(c)
        for t in (2, 5):
            tab_ref[t, c * L:(c + 1) * L, :] = lat_reps[t][:, c * 128:(c + 1) * 128]
    sel_row = lax.broadcasted_iota(I32, (24, 128), 0) % 8

    def late_tables(c):
        sel_c = (sel_row == c).astype(BF16)
        for t in (0, 1, 3, 4):
            tab_ref[t, pl.ds(pl.multiple_of(c * L, L), L), :] = lax.dot_general(
                lat_split[t], sel_c, TN_DIMS, preferred_element_type=F32)

    def lanes2(x):
        return jnp.concatenate([x, x], axis=1)

    def advance(k, v, g_rep, g_row, b_last, c_ref, m_ref):
        m = m_ref[...]
        m_new = jnp.maximum(b_last + m, jnp.max(g_row, axis=-1, keepdims=True))
        kw = (k.astype(F32) * jnp.exp(lanes2(g_rep) - m_new)).astype(BF16)
        v_aug = jnp.concatenate([v, ones_cols], axis=1)
        c_ref[...] = jnp.exp(b_last + m - m_new) * c_ref[...] + lax.dot_general(
            kw, v_aug, TN_DIMS, preferred_element_type=F32)
        m_ref[...] = m_new

    for r in (cf_ref, mf_ref, cb_ref, mb_ref):
        r[...] = jnp.zeros_like(r)
    ctx_rows, ctx_reps, _ = chunk_tables(cgt_ref[0, 0], n_ctx_chunks, True)
    for c in range(n_ctx_chunks):
        rows = ctx_rows(c)
        advance(ck_ref[c * L:(c + 1) * L, :], cv_ref[0, c * L:(c + 1) * L, :],
                ctx_reps[2][:, c * 128:(c + 1) * 128], rows[2:3], rows[4:5, L - 1:L], cf_ref, mf_ref)
    for c in reversed(range(n_ctx_chunks)):
        rows = ctx_rows(c)
        advance(ck_ref[c * L:(c + 1) * L, :], cv_ref[0, c * L:(c + 1) * L, :],
                ctx_reps[5][:, c * 128:(c + 1) * 128], rows[3:4], rows[5:6, 0:1], cb_ref, mb_ref)

    def state_pass(i, carry, keys):
        cb = n_chunks - 1 - i
        rf = pl.multiple_of(i * L, L)
        rb = pl.multiple_of(cb * L, L)
        cfs_ref[i] = cf_ref[...].astype(BF16)
        mfs_ref[i] = mf_ref[...]
        cbs_ref[cb] = cb_ref[...].astype(BF16)
        mbs_ref[cb] = mb_ref[...]
        rows_f = row_ref[i]
        rows_b = row_ref[cb]
        advance(k_ref[pl.ds(rf, L), :], v_ref[0, pl.ds(rf, L), :], tab_ref[2, pl.ds(rf, L), :],
                rows_f[2:3], rows_f[4:5, L - 1:L], cf_ref, mf_ref)
        advance(k_ref[pl.ds(rb, L), :], v_ref[0, pl.ds(rb, L), :], tab_ref[5, pl.ds(rb, L), :],
                rows_b[3:4], rows_b[5:6, 0:1], cb_ref, mb_ref)
        conv_step(i, keys)
        late_tables(i)
        return carry

    def finish_states():
        late_tables(n_chunks - 1)
        cfs_ref[n_chunks - 1] = cf_ref[...].astype(BF16)
        mfs_ref[n_chunks - 1] = mf_ref[...]
        cbs_ref[0] = cb_ref[...].astype(BF16)
        mbs_ref[0] = mb_ref[...]

    def direction(q, v_aug, s, z_row, zmax_rep, b_rep, mask, c_in, m_in):
        mx = jnp.maximum(zmax_rep, m_in)
        att = s * jnp.exp((z_row - lanes2(mx)) + mask)
        na = jnp.dot(att.astype(BF16), v_aug, preferred_element_type=F32)
        qa = jnp.dot(q, c_in, preferred_element_type=F32)
        a = jnp.exp(m_in - mx)
        num = na[:, 0:HEAD_W] + lanes2(a) * qa[:, 0:HEAD_W]
        den = na[:, HEAD_W:] + a * qa[:, HEAD_W:]
        scale = 1.0 / jnp.maximum(jnp.abs(den), jnp.exp(-(b_rep + mx)))
        return num * lanes2(scale)

    @pl.when(pl.program_id(1) == 0)
    def _():
        mask_ref[0] = jnp.where(ci <= ri, 0.0, NEG_INF)
        mask_ref[1] = jnp.where(ci > ri, 0.0, NEG_INF)

    def out_chunk(c):
        r0 = pl.multiple_of(c * L, L)
        q = q_ref[pl.ds(r0, L), :]
        k = k_ref[pl.ds(r0, L), :]
        v_aug = jnp.concatenate([v_ref[0, pl.ds(r0, L), :], ones_cols], axis=1)
        s = lax.dot_general(q, k, NT_DIMS, preferred_element_type=F32)
        rows = row_ref[c]
        tot = direction(q, v_aug, s, rows[0:1], tab_ref[0, pl.ds(r0, L), :], tab_ref[1, pl.ds(r0, L), :],
                        mask_ref[0], cfs_ref[c], mfs_ref[c])
        tot = tot + direction(q, v_aug, s, rows[1:2], tab_ref[3, pl.ds(r0, L), :],
                              tab_ref[4, pl.ds(r0, L), :], mask_ref[1], cbs_ref[c], mbs_ref[c])
        mo = mo_ref[0, pl.ds(r0, L), :].astype(F32)
        o_ref[0, pl.ds(r0, L), :] = (_layer_norm(tot) * jax.nn.sigmoid(mo)).astype(BF16)

    return state_pass, finish_states, out_chunk, n_k_steps


def _mlstm_scratch(T, Tc, n_chunks):
    state = [pltpu.VMEM((HEAD_W, AUG_W), F32), pltpu.VMEM((1, 1), F32)]
    snaps = [pltpu.VMEM((n_chunks, HEAD_W, AUG_W), BF16), pltpu.VMEM((n_chunks, 1, 1), F32)]
    return [pltpu.VMEM((N_TAB, T, 128), F32), pltpu.VMEM((n_chunks, 8, SCAN_L), F32)] \
        + state + state + snaps + snaps + [pltpu.VMEM((2, SCAN_L, SCAN_L), F32)] \
        + [pltpu.VMEM((T + 16, HEAD_W), F32), pltpu.VMEM((T + 16, HEAD_W), F32),
           pltpu.VMEM((T, HEAD_W), BF16), pltpu.VMEM((T, HEAD_W), BF16), pltpu.VMEM((Tc, HEAD_W), BF16)]


def _scan_kernel(n_ret_scratch, dl_ref, rq_ref, rk_ref, rv_ref, rg_ref, rck_ref, rcv_ref,
                 mq_ref, mk_ref, mv_ref, mo_ref, mck_ref, mcv_ref, gt_ref, cgt_ref,
                 wq_ref, bq_ref, wk_ref, bk_ref, r_ref, m_ref, *scratch):
    n_chunks = rq_ref.shape[1] // SCAN_L
    ret = _ret_build(dl_ref, rq_ref, rk_ref, rv_ref, rg_ref, rck_ref, rcv_ref, r_ref,
                     *scratch[:n_ret_scratch])
    mls = _mlstm_build(mq_ref, mk_ref, mv_ref, mo_ref, mck_ref, mcv_ref, gt_ref, cgt_ref,
                       wq_ref, bq_ref, wk_ref, bk_ref, m_ref, *scratch[n_ret_scratch:])

    def state_pass(i, carry, keys):
        ret[0](i, carry)
        mls[0](i, carry, keys)
        return carry

    n_k_steps = mls[3]
    lax.fori_loop(0, n_k_steps, functools.partial(state_pass, keys=True), 0)
    lax.fori_loop(n_k_steps, n_chunks - 1, functools.partial(state_pass, keys=False), 0)
    ret[1]()
    mls[1]()

    def out_pass(i, carry):
        for c in (2 * i, 2 * i + 1):
            ret[2](c)
            mls[2](c)
        return carry

    lax.fori_loop(0, n_chunks // 2, out_pass, 0)


def _scans(decay_logit, p_lat, p_ctx, gt, cgt, conv_w, conv_b, ret_lat, ret_ctx, ml_lat, ml_ctx):
    B, T, _ = p_lat.shape
    Tc = p_ctx.shape[1]
    assert T % (2 * SCAN_L) == 0 and Tc % SCAN_L == 0 and T // SCAN_L <= 8
    n_chunks = T // SCAN_L

    def lat(sec):
        return pl.BlockSpec((1, T, HEAD_W), lambda h, b: (b, 0, sec * HEADS + h))

    def cx(sec):
        return pl.BlockSpec((1, Tc, HEAD_W), lambda h, b: (b, 0, sec * HEADS + h))

    gates = pl.BlockSpec((1, 1, N_GK, 8, SCAN_L), lambda h, b: (b, h, 0, 0, 0))
    out = pl.BlockSpec((1, T, HEAD_W), lambda h, b: (b, 0, h))
    conv_specs = [pl.BlockSpec((3, HEAD_W), lambda h, b: (0, h)),
                  pl.BlockSpec((1, HEAD_W), lambda h, b: (0, h)),
                  pl.BlockSpec((3, HEAD_W), lambda h, b: (0, HEADS + h)),
                  pl.BlockSpec((1, HEAD_W), lambda h, b: (0, HEADS + h))]
    ret_scratch = _ret_scratch(n_chunks)
    return pl.pallas_call(
        functools.partial(_scan_kernel, len(ret_scratch)),
        out_shape=(jax.ShapeDtypeStruct((B, T, BRANCH_W), BF16),
                   jax.ShapeDtypeStruct((B, T, BRANCH_W), BF16)),
        grid=(HEADS, B),
        in_specs=[pl.BlockSpec(memory_space=pltpu.SMEM)]
        + [lat(s) for s in ret_lat] + [cx(s) for s in ret_ctx]
        + [lat(s) for s in ml_lat] + [cx(s) for s in ml_ctx] + [gates, gates] + conv_specs,
        out_specs=(out, out),
        scratch_shapes=ret_scratch + _mlstm_scratch(T, Tc, n_chunks),
        compiler_params=_cparams("arbitrary", "arbitrary"),
        name="scans",
    )(decay_logit, *([p_lat] * 4), *([p_ctx] * 2), *([p_lat] * 4), *([p_ctx] * 2), gt, cgt,
      conv_w, conv_b, conv_w, conv_b)


def _merge_kernel(alpha, r_ref, m_ref, gr_ref, gm_ref, x_ref, g1_ref, sh2_ref, sc2_ref,
                  lng_ref, lnb_ref, wr_ref, wm_ref, wo_ref, wrt_ref, brt_ref, e1_ref, e3_ref, e2_ref,
                  x1_ref, ua_ref, ub_ref, ri_ref, rw_ref, cnt_ref, e1b_ref, e3b_ref, e2b_ref,
                  carry_ref, u_ref):
    for src, dst in ((e1_ref, e1b_ref), (e3_ref, e3b_ref), (e2_ref, e2b_ref)):
        dst[...] = src[...].astype(BF16)

    @pl.when(jnp.logical_and(pl.program_id(0) == 0, pl.program_id(1) == 0))
    def _():
        carry_ref[...] = jnp.zeros_like(carry_ref)
        tm = x_ref.shape[1]
        r = lax.broadcasted_iota(I32, (tm, tm), 0)
        c = lax.broadcasted_iota(I32, (tm, tm), 1)
        u_ref[...] = (r < c).astype(BF16)

    yr = jnp.dot(r_ref[0], wr_ref[...], preferred_element_type=F32)
    ym = jnp.dot(m_ref[0], wm_ref[...], preferred_element_type=F32)
    y = jax.nn.sigmoid(gr_ref[0].astype(F32)) * yr + jax.nn.sigmoid(gm_ref[0].astype(F32)) * ym
    yo = jnp.dot(y.astype(BF16), wo_ref[...], preferred_element_type=F32)
    x1 = _layer_norm(alpha * x_ref[0] + g1_ref[0] * yo) * lng_ref[...] + lnb_ref[...]
    x1_ref[0] = x1
    u2 = _layer_norm(x1) * (1.0 + sc2_ref[0]) + sh2_ref[0]
    ua_ref[0] = _pack_pairs(u2[:, 0:PACK_W], u2[:, PACK_W:2 * PACK_W])
    ub_ref[0] = _pack_pairs(u2[:, 2 * PACK_W:3 * PACK_W], u2[:, 3 * PACK_W:4 * PACK_W])
    lt = lax.dot_general(wrt_ref[...], u2.astype(BF16), NT_DIMS, preferred_element_type=F32) + brt_ref[...]
    _route_tile(lt, ri_ref, rw_ref, cnt_ref, carry_ref, u_ref)


def _route_tile(lt, ri_ref, rw_ref, cnt_ref, carry_ref, u_ref):
    tm = lt.shape[1]
    lg = lt[0:N_GROUPS, :]
    eg = jnp.exp(lg - jnp.max(lg, axis=0, keepdims=True))
    pg = eg / jnp.sum(eg, axis=0, keepdims=True)
    pg_top = jnp.max(pg, axis=0, keepdims=True)
    rows_g = lax.broadcasted_iota(I32, pg.shape, 0)
    g_idx = jnp.min(jnp.where(pg == pg_top, rows_g, N_GROUPS), axis=0, keepdims=True)

    le = jnp.zeros((EXP_PER_GROUP, tm), F32)
    for g in range(N_GROUPS):
        lo = 8 + g * EXP_PER_GROUP
        le = jnp.where(g_idx == g, lt[lo:lo + EXP_PER_GROUP, :], le)
    ee = jnp.exp(le - jnp.max(le, axis=0, keepdims=True))
    pe = ee / jnp.sum(ee, axis=0, keepdims=True)
    rows_e = lax.broadcasted_iota(I32, pe.shape, 0)
    v1 = jnp.max(pe, axis=0, keepdims=True)
    i1 = jnp.min(jnp.where(pe == v1, rows_e, EXP_PER_GROUP), axis=0, keepdims=True)
    pe2 = jnp.where(rows_e == i1, -1.0, pe)
    v2 = jnp.max(pe2, axis=0, keepdims=True)
    i2 = jnp.min(jnp.where(pe2 == v2, rows_e, EXP_PER_GROUP), axis=0, keepdims=True)
    den = v1 + v2
    rw_ref[...] = jnp.zeros_like(rw_ref)
    rw_ref[0:1, :] = pg_top * v1 / den
    rw_ref[1:2, :] = pg_top * v2 / den
    e1 = g_idx * EXP_PER_GROUP + i1
    e2 = g_idx * EXP_PER_GROUP + i2

    rows_x = lax.broadcasted_iota(I32, (N_EXPERTS, tm), 0)
    oh1 = (rows_x == e1).astype(F32)
    oh2 = (rows_x == e2).astype(F32)
    both = oh1 + oh2
    before = carry_ref[:, 0:1] + jnp.dot(both.astype(BF16), u_ref[...], preferred_element_type=F32)
    ri_ref[0:1, :] = e1
    ri_ref[1:2, :] = e2
    ri_ref[2:3, :] = jnp.sum(oh1 * before, axis=0, keepdims=True).astype(I32)
    ri_ref[3:4, :] = jnp.sum(oh2 * before, axis=0, keepdims=True).astype(I32)
    carry_ref[...] = carry_ref[...] + jnp.sum(both, axis=1, keepdims=True)
    cnt_ref[...] = carry_ref[...].astype(I32)


def _merge(alpha, r, m, p_lat, sec_gates, x, g1, sh2, sc2, lng, lnb, wr, wm, wo, wrt, brt, expert_w):
    B, T, D = x.shape
    tm = MERGE_TM
    per_b = T // tm
    n = B * T
    n_steps = B * per_b
    sliced = [w.reshape(n_steps, w.shape[0] * w.shape[1] // n_steps, w.shape[2]) for w in expert_w]
    assert all(s.shape[1] % 16 == 0 and s.size == w.size for s, w in zip(sliced, expert_w))

    def slab(s):
        return pl.BlockSpec((1,) + s.shape[1:], lambda b, i: (b * per_b + i, 0, 0))

    def tile(w):
        return pl.BlockSpec((1, tm, w), lambda b, i: (b, i, 0))

    def sec(s):
        return pl.BlockSpec((1, tm, BRANCH_W), lambda b, i: (b, i, s))

    def mod():
        return pl.BlockSpec((1, 1, D), lambda b, i: (b, 0, 0))

    def const(shape):
        return pl.BlockSpec(shape, lambda b, i: (0,) * len(shape))

    outs = pl.pallas_call(
        functools.partial(_merge_kernel, alpha),
        out_shape=(jax.ShapeDtypeStruct((B, T, D), F32),
                   jax.ShapeDtypeStruct((B, T, PACK_W), U32),
                   jax.ShapeDtypeStruct((B, T, PACK_W), U32),
                   jax.ShapeDtypeStruct((4, n), I32),
                   jax.ShapeDtypeStruct((8, n), F32),
                   jax.ShapeDtypeStruct((N_EXPERTS, 128), I32))
        + tuple(jax.ShapeDtypeStruct(s.shape, BF16) for s in sliced),
        grid=(B, per_b),
        in_specs=[tile(BRANCH_W), tile(BRANCH_W), sec(sec_gates[0]), sec(sec_gates[1]), tile(D),
                  mod(), mod(), mod(), const((1, D)), const((1, D)),
                  const((BRANCH_W, D)), const((BRANCH_W, D)), const((D, D)),
                  const((ROUTE_ROWS, D)), const((ROUTE_ROWS, 1))] + [slab(s) for s in sliced],
        out_specs=(tile(D), tile(PACK_W), tile(PACK_W),
                   pl.BlockSpec((4, tm), lambda b, i: (0, b * per_b + i)),
                   pl.BlockSpec((8, tm), lambda b, i: (0, b * per_b + i)),
                   pl.BlockSpec((N_EXPERTS, 128), lambda b, i: (0, 0)))
        + tuple(slab(s) for s in sliced),
        scratch_shapes=[pltpu.VMEM((N_EXPERTS, 128), F32), pltpu.VMEM((tm, tm), BF16)],
        compiler_params=_cparams("arbitrary", "arbitrary"),
        name="merge",
    )(r, m, p_lat, p_lat, x, g1, sh2, sc2, lng, lnb, wr, wm, wo, wrt, brt, *sliced)
    return outs[:6] + tuple(o.reshape(w.shape) for o, w in zip(outs[6:], expert_w))


def _sc_mesh():
    return plsc.VectorSubcoreMesh(core_axis_name="c", subcore_axis_name="s")


def _sc_scatter2(rows_a, rows_b, idx0, idx1, n_out):
    m, w = rows_a.shape
    out = jax.ShapeDtypeStruct((n_out, w), rows_a.dtype)

    @functools.partial(pl.kernel, out_type=(out, out), mesh=_sc_mesh(), scratch_types=[])
    def k(xa_hbm, xb_hbm, i0_hbm, i1_hbm, oa_hbm, ob_hbm):
        for x_hbm, o_hbm in ((xa_hbm, oa_hbm), (xb_hbm, ob_hbm)):
            def body(x_vmem, i0_vmem, i1_vmem, o_hbm=o_hbm):
                pltpu.sync_copy(x_vmem, o_hbm.at[i0_vmem.at[0]])
                pltpu.sync_copy(x_vmem, o_hbm.at[i1_vmem.at[0]])

            pltpu.emit_pipeline(
                body,
                grid=(m // SC_WIN,),
                in_specs=[pl.BlockSpec((SC_WIN, w), lambda i: (i, 0)),
                          pl.BlockSpec((1, SC_WIN), lambda i: (0, i)),
                          pl.BlockSpec((1, SC_WIN), lambda i: (0, i))],
                out_specs=[],
                core_axis_name=("c", "s"),
                dimension_semantics=(pltpu.PARALLEL,),
            )(x_hbm, i0_hbm, i1_hbm)

    return k(rows_a, rows_b, idx0.reshape(1, m), idx1.reshape(1, m))


def _sc_gather(table_a, table_b, idx):
    m = idx.shape[0]
    w = table_a.shape[1]
    out = jax.ShapeDtypeStruct((m, w), table_a.dtype)

    @functools.partial(pl.kernel, out_type=(out, out), mesh=_sc_mesh(), scratch_types=[])
    def k(ta_hbm, tb_hbm, i_hbm, oa_hbm, ob_hbm):
        for t_hbm, o_hbm in ((ta_hbm, oa_hbm), (tb_hbm, ob_hbm)):
            def body(i_vmem, o_vmem, t_hbm=t_hbm):
                pltpu.sync_copy(t_hbm.at[i_vmem.at[0]], o_vmem)

            pltpu.emit_pipeline(
                body,
                grid=(m // SC_WIN,),
                in_specs=[pl.BlockSpec((1, SC_WIN), lambda i: (0, i))],
                out_specs=[pl.BlockSpec((SC_WIN, w), lambda i: (i, 0))],
                core_axis_name=("c", "s"),
                dimension_semantics=(pltpu.PARALLEL,),
            )(i_hbm, o_hbm)

    return k(table_a, table_b, idx.reshape(1, m))


def _expert_kernel(be_ref, nv_ref, xa_ref, xb_ref, w1_ref, w3_ref, w2_ref, ya_ref, yb_ref):
    j = pl.program_id(0)
    nv = nv_ref[j]

    @pl.when(nv > 0)
    def _():
        valid = lax.broadcasted_iota(I32, xa_ref.shape, 0) < nv
        zero = jnp.zeros(xa_ref.shape, U32)
        parts = _unpack_pairs(jnp.where(valid, xa_ref[...], zero)) + \
            _unpack_pairs(jnp.where(valid, xb_ref[...], zero))
        x = jnp.concatenate([p.astype(BF16) for p in parts], axis=1)
        h1 = jnp.dot(x, w1_ref[0], preferred_element_type=F32)
        h3 = jnp.dot(x, w3_ref[0], preferred_element_type=F32)
        y = jnp.dot((_silu(h1) * h3).astype(BF16), w2_ref[0], preferred_element_type=F32)
        ya_ref[...] = _pack_pairs(y[:, 0:PACK_W], y[:, PACK_W:2 * PACK_W])
        yb_ref[...] = _pack_pairs(y[:, 2 * PACK_W:3 * PACK_W], y[:, 3 * PACK_W:4 * PACK_W])

    @pl.when(nv == 0)
    def _():
        ya_ref[...] = jnp.zeros_like(ya_ref)
        yb_ref[...] = jnp.zeros_like(yb_ref)


def _experts(block_exp, n_valid, xa, xb, w1, w3, w2):
    n_slots = xa.shape[0]
    n_blocks = n_slots // MOE_BLK
    d, de = w1.shape[1], w1.shape[2]
    slot = pl.BlockSpec((MOE_BLK, PACK_W), lambda j, be, nv: (j, 0))
    grid_spec = pltpu.PrefetchScalarGridSpec(
        num_scalar_prefetch=2,
        grid=(n_blocks,),
        in_specs=[slot, slot,
                  pl.BlockSpec((1, d, de), lambda j, be, nv: (be[j], 0, 0)),
                  pl.BlockSpec((1, d, de), lambda j, be, nv: (be[j], 0, 0)),
                  pl.BlockSpec((1, de, d), lambda j, be, nv: (be[j], 0, 0))],
        out_specs=(slot, slot),
    )
    return pl.pallas_call(
        _expert_kernel,
        out_shape=(jax.ShapeDtypeStruct((n_slots, PACK_W), U32),
                   jax.ShapeDtypeStruct((n_slots, PACK_W), U32)),
        grid_spec=grid_spec,
        compiler_params=_cparams("parallel"),
        name="experts",
    )(block_exp, n_valid, xa, xb, w1, w3, w2)


def _final_kernel(alpha, x1_ref, a0_ref, b0_ref, a1_ref, b1_ref, w_ref, g2_ref, lng_ref, lnb_ref, o_ref):
    w = w_ref[...].T
    w0 = w[:, 0:1]
    w1 = w[:, 1:2]
    parts0 = _unpack_pairs(a0_ref[...]) + _unpack_pairs(b0_ref[...])
    parts1 = _unpack_pairs(a1_ref[...]) + _unpack_pairs(b1_ref[...])
    f = jnp.concatenate([w0 * p0 + w1 * p1 for p0, p1 in zip(parts0, parts1)], axis=1)
    o_ref[0] = _layer_norm(alpha * x1_ref[0] + g2_ref[0] * f) * lng_ref[...] + lnb_ref[...]


def _final(alpha, x1, ya, yb, w, g2, lng, lnb):
    B, T, D = x1.shape
    tm = min(FINAL_TM, T)
    per_b = T // tm
    n_tiles = B * per_b

    def rows(k):
        return pl.BlockSpec((tm, PACK_W), lambda b, i: (k * n_tiles + b * per_b + i, 0))

    return pl.pallas_call(
        functools.partial(_final_kernel, alpha),
        out_shape=jax.ShapeDtypeStruct((B, T, D), F32),
        grid=(B, per_b),
        in_specs=[pl.BlockSpec((1, tm, D), lambda b, i: (b, i, 0)),
                  rows(0), rows(0), rows(1), rows(1),
                  pl.BlockSpec((8, tm), lambda b, i: (0, b * per_b + i)),
                  pl.BlockSpec((1, 1, D), lambda b, i: (b, 0, 0)),
                  pl.BlockSpec((1, D), lambda b, i: (0, 0)),
                  pl.BlockSpec((1, D), lambda b, i: (0, 0))],
        out_specs=pl.BlockSpec((1, tm, D), lambda b, i: (b, i, 0)),
        compiler_params=_cparams("parallel", "parallel"),
        name="final",
    )(x1, ya, yb, ya, yb, w, g2, lng, lnb)


def _rotary_tables(T):
    quarter = HEAD_W // 4
    freqs = ROPE_BASE ** (-jnp.arange(quarter, dtype=F32) / quarter)
    t = jnp.arange(T)
    ang_r = (t // GRID_W).astype(F32)[:, None] * freqs[None, :]
    ang_c = (t % GRID_W).astype(F32)[:, None] * freqs[None, :]
    cos = jnp.concatenate([jnp.cos(ang_r)] * 2 + [jnp.cos(ang_c)] * 2, axis=1)
    sin = jnp.concatenate([-jnp.sin(ang_r), jnp.sin(ang_r), -jnp.sin(ang_c), jnp.sin(ang_c)], axis=1)
    return cos, sin


def _per_head_gates(gt):
    B, _, T = gt.shape
    n_chunks = T // SCAN_L
    gth = gt.reshape(B, N_GK, HEADS, n_chunks, SCAN_L).transpose(0, 2, 1, 3, 4)
    return jnp.pad(gth, ((0, 0), (0, 0), (0, 0), (0, 8 - n_chunks), (0, 0)))


def _table_lookup(table, idx):
    sel = idx[..., None] == jnp.arange(table.shape[0], dtype=idx.dtype)
    return jnp.sum(jnp.where(sel, table, 0), axis=-1)


def kernel(x, c, ctx, c_ctx, w_ada, b_ada, w_in, b_mgate, ml_conv_w, ml_conv_b, ret_decay_logit, w_ret_branch, w_ml_branch, w_out, ln1_g, ln1_b, w_rg, b_rg, w_re, b_re, w_e1, w_e3, w_e2, ln2_g, ln2_b):
    B, T, D = x.shape
    depth = w_ada.shape[0]
    assert depth == 1 and D == BRANCH_W and T % GRID_W == 0
    alpha = (2 * depth) ** 0.25
    n_tok = B * T

    n_rows = -(-(B + 1) // 8) * 8
    cs = jnp.zeros((n_rows, D), F32).at[:B].set(c).at[B].set(c_ctx)
    mod = _ada(cs, w_ada[0], b_ada[0][None, :])
    sh1, sc1, g1, sh2, sc2, g2 = [mod[:B, None, i * D:(i + 1) * D] for i in range(6)]
    csh1 = mod[B, 0 * D:1 * D].reshape(1, 1, D)
    csc1 = mod[B, 1 * D:2 * D].reshape(1, 1, D)

    w = w_in[0]
    g_lo = 8 * BRANCH_W
    w_gate_t = w[:, g_lo:g_lo + N_GATES].T.astype(BF16)
    b_gate = b_mgate[0][:, None]
    w_head = w[:, :g_lo].astype(BF16)
    w_tail = w[:, g_lo + N_GATES:].astype(BF16)
    sec_lat = tuple(("h", s) for s in range(8)) + (("t", 0), ("t", 1))
    sec_ctx = (("h", 1), ("h", 2), ("h", 5), ("h", 6))
    kinds_lat = ("rot", "rot_scale") + ("plain",) * 8
    kinds_ctx = ("scale", "plain", "plain", "plain")
    p_lat, gt_lat, _ = _proj(x, sh1, sc1, w_head, w_tail, sec_lat, w_gate_t, b_gate, kinds_lat,
                             _rotary_tables(T))
    Tc = ctx.shape[1]
    p_ctx, gt_ctx, (w_rb, w_mb, w_ob) = _proj(
        ctx.reshape(1, B * Tc, D), csh1, csc1, w_head, w_tail, sec_ctx, w_gate_t, b_gate, kinds_ctx,
        side_cast=(w_ret_branch[0], w_ml_branch[0], w_out[0]))
    p_ctx = p_ctx.reshape(B, Tc, -1)
    gt_ctx = gt_ctx.reshape(N_GATES, B, Tc).transpose(1, 0, 2)

    ret, mls = _scans(ret_decay_logit[0], p_lat, p_ctx, _per_head_gates(gt_lat), _per_head_gates(gt_ctx),
                      ml_conv_w[0], ml_conv_b[0][None, :], (0, 1, 2, 3), (0, 1), (4, 5, 6, 7), (2, 3))

    wrt = jnp.zeros((ROUTE_ROWS, D), F32).at[:N_GROUPS].set(w_rg[0].T).at[8:8 + N_EXPERTS].set(w_re[0].T)
    brt = jnp.zeros((ROUTE_ROWS, 1), F32).at[:N_GROUPS, 0].set(b_rg[0]).at[8:8 + N_EXPERTS, 0].set(b_re[0])
    x1, ua, ub, ri, rw, cnt, we1, we3, we2 = _merge(
        alpha, ret, mls, p_lat, (8, 9), x, g1, sh2, sc2, ln1_g[0][None, :], ln1_b[0][None, :],
        w_rb, w_mb, w_ob, wrt.astype(BF16), brt, (w_e1[0], w_e3[0], w_e2[0]))

    counts = cnt[:, 0]
    padded = (counts + MOE_BLK - 1) // MOE_BLK * MOE_BLK
    pad_end = jnp.cumsum(padded)
    pad_off = pad_end - padded
    dest = _table_lookup(pad_off, ri[0:2]) + ri[2:4]
    n_blocks = (2 * n_tok) // MOE_BLK + N_EXPERTS
    n_slots = n_blocks * MOE_BLK
    block_start = jnp.arange(n_blocks, dtype=I32) * MOE_BLK
    block_exp = jnp.minimum((block_start[:, None] >= pad_end[None, :]).sum(1), N_EXPERTS - 1).astype(I32)
    n_valid = jnp.clip(_table_lookup(counts, block_exp) - (block_start - _table_lookup(pad_off, block_exp)),
                       0, MOE_BLK).astype(I32)

    xa, xb = _sc_scatter2(ua.reshape(n_tok, PACK_W), ub.reshape(n_tok, PACK_W), dest[0], dest[1], n_slots)
    ya, yb = _experts(block_exp, n_valid, xa, xb, we1, we3, we2)
    ga, gb = _sc_gather(ya, yb, dest.reshape(2 * n_tok))
    return _final(alpha, x1, ga, gb, rw, g2, ln2_g[0][None, :], ln2_b[0][None, :])
```

```python
import functools

import jax
import jax.numpy as jnp
from jax import lax
from jax.experimental import pallas as pl
from jax.experimental.pallas import tpu as pltpu
from jax.experimental.pallas import tpu_sc as plsc

F32 = jnp.float32
BF16 = jnp.bfloat16
U32 = jnp.uint32
I32 = jnp.int32
HIGHEST = lax.Precision.HIGHEST

HEADS = 4
HEAD_W = 256
BRANCH_W = HEADS * HEAD_W
GRID_W = 64
ROPE_BASE = 10000.0
N_GATES = 16
N_GK = N_GATES // HEADS
N_GROUPS = 4
EXP_PER_GROUP = 8
N_EXPERTS = N_GROUPS * EXP_PER_GROUP
LN_EPS = 1e-5
NEG_INF = -1e30
KEY_SCALE = HEAD_W ** -0.5

SCAN_L = 256
CONV_ROWS = 128
PROJ_TM = 2048
PROJ_SUB = 256
MERGE_TM = 512
FINAL_TM = 1024
MOE_BLK = 512
SC_WIN = 128
PACK_W = 256
ROUTE_ROWS = 64
N_TAB = 6
AUG_W = HEAD_W + 128
VMEM_LIMIT = 48 * 1024 * 1024

NT_DIMS = (((1,), (1,)), ((), ()))
TN_DIMS = (((0,), (0,)), ((), ()))


def _cparams(*sem):
    return pltpu.CompilerParams(dimension_semantics=sem, vmem_limit_bytes=VMEM_LIMIT)


def _layer_norm(x):
    mu = jnp.mean(x, axis=-1, keepdims=True)
    xc = x - mu
    var = jnp.mean(xc * xc, axis=-1, keepdims=True)
    return xc * lax.rsqrt(var + LN_EPS)


def _log_sigmoid(x):
    return jnp.minimum(x, 0.0) - jnp.log1p(jnp.exp(-jnp.abs(x)))


def _silu(x):
    return x * jax.nn.sigmoid(x)


def _pack_pairs(hi, lo):
    hb = lax.bitcast_convert_type(hi.astype(BF16).astype(F32), U32)
    lb = lax.bitcast_convert_type(lo.astype(BF16).astype(F32), U32)
    return (hb & jnp.uint32(0xFFFF0000)) | (lb >> 16)


def _unpack_pairs(p):
    hi = lax.bitcast_convert_type(p & jnp.uint32(0xFFFF0000), F32)
    lo = lax.bitcast_convert_type(p << 16, F32)
    return hi, lo


def _split3(x):
    hi = x.astype(BF16).astype(F32)
    r1 = x - hi
    mid = r1.astype(BF16).astype(F32)
    lo = (r1 - mid).astype(BF16).astype(F32)
    return jnp.concatenate([hi, mid, lo], axis=0).astype(BF16)


def _ada_kernel(c_ref, w_ref, b_ref, o_ref):
    s = _silu(c_ref[...])
    o_ref[...] = jnp.dot(s, w_ref[...], precision=HIGHEST, preferred_element_type=F32) + b_ref[...]


def _ada(cs, w, b):
    rows, d = cs.shape
    cols = w.shape[1]
    tn = 1024
    return pl.pallas_call(
        _ada_kernel,
        out_shape=jax.ShapeDtypeStruct((rows, cols), F32),
        grid=(cols // tn,),
        in_specs=[pl.BlockSpec((rows, d), lambda j: (0, 0)),
                  pl.BlockSpec((d, tn), lambda j: (0, j)),
                  pl.BlockSpec((1, tn), lambda j: (0, j))],
        out_specs=pl.BlockSpec((rows, tn), lambda j: (0, j)),
        compiler_params=_cparams("parallel"),
        name="ada",
    )(cs, w, b)


def _proj_kernel(kinds, srcs, n_cast, hb_ref, tb_ref, x_ref, sh_ref, sc_ref, wh_ref, wt_ref, wg_ref,
                 bg_ref, *rest):
    cast_in, rest = rest[:n_cast], rest[n_cast:]
    if "rot" in kinds or "rot_scale" in kinds:
        cos_ref, sin_ref, rest = rest[0], rest[1], rest[2:]
    o_ref, gt_ref = rest[0], rest[1]
    cast_out, u_ref = rest[2:2 + n_cast], rest[2 + n_cast]
    for src, dst in zip(cast_in, cast_out):
        dst[...] = src[...].astype(BF16)
    j = pl.program_id(2)
    tm = x_ref.shape[1]
    sub = min(PROJ_SUB, tm)

    def rotary(acc, rows, scale):
        for s in range(acc.shape[1] // 128):
            a = acc[:, s * 128:(s + 1) * 128]
            half = s % 2
            cs = cos_ref[rows, half * 128:(half + 1) * 128]
            sn = sin_ref[rows, half * 128:(half + 1) * 128]
            r = a * cs + pltpu.roll(a, 64, 1) * sn
            if scale != 1.0:
                r = r * scale
            o_ref[0, rows, s * 128:(s + 1) * 128] = r.astype(BF16)

    def section(kind, first, src):
        w_ref = wh_ref if src == "h" else wt_ref
        for r in range(tm // sub):
            rows = slice(r * sub, (r + 1) * sub)
            if first:
                u = _layer_norm(x_ref[0, rows, :]) * (1.0 + sc_ref[0]) + sh_ref[0]
                ub = u.astype(BF16)
                u_ref[rows, :] = ub
                gt_ref[0, :, rows] = lax.dot_general(wg_ref[...], ub, NT_DIMS,
                                                     preferred_element_type=F32) + bg_ref[...]
            else:
                ub = u_ref[rows, :]
            acc = jnp.dot(ub, w_ref[...], preferred_element_type=F32)
            if kind == "rot":
                rotary(acc, rows, 1.0)
            elif kind == "rot_scale":
                rotary(acc, rows, KEY_SCALE)
            elif kind == "scale":
                o_ref[0, rows, :] = (acc * KEY_SCALE).astype(BF16)
            else:
                o_ref[0, rows, :] = acc.astype(BF16)

    variants = {}
    for s, key in enumerate(zip(kinds, srcs)):
        variants.setdefault(key + (s == 0,), []).append(s)
    for (kind, src, first), secs in variants.items():
        cond = functools.reduce(jnp.logical_or, [j == s for s in secs])

        @pl.when(cond)
        def _(kind=kind, first=first, src=src):
            section(kind, first, src)


def _proj(x, sh, sc, w_head, w_tail, sections, w_gate_t, b_gate, kinds, tables=None, side_cast=()):
    B, T, D = x.shape
    n_sec = len(kinds)
    tm = min(PROJ_TM, T)
    tn = BRANCH_W
    assert T % tm == 0
    n_steps = (T // tm) * B * n_sec
    slabs = [a.reshape(n_steps, a.shape[0] // n_steps, a.shape[1]) for a in side_cast]
    assert all(s.shape[1] % 16 == 0 and s.size == a.size for s, a in zip(slabs, side_cast))

    def slab(s):
        return pl.BlockSpec((1,) + s.shape[1:],
                            lambda i, b, j, hb, tb: ((i * B + b) * n_sec + j, 0, 0))

    srcs = tuple(src for src, _ in sections)
    hb, tb, h_last, t_last = [], [], 0, 0
    for src, blk in sections:
        h_last, t_last = (blk, t_last) if src == "h" else (h_last, blk)
        hb.append(h_last)
        tb.append(t_last)
    in_specs = [
        pl.BlockSpec((1, tm, D), lambda i, b, j, hb, tb: (b, i, 0)),
        pl.BlockSpec((1, 1, D), lambda i, b, j, hb, tb: (b, 0, 0)),
        pl.BlockSpec((1, 1, D), lambda i, b, j, hb, tb: (b, 0, 0)),
        pl.BlockSpec((D, tn), lambda i, b, j, hb, tb: (0, hb[j])),
        pl.BlockSpec((D, tn), lambda i, b, j, hb, tb: (0, tb[j])),
        pl.BlockSpec((N_GATES, D), lambda i, b, j, hb, tb: (0, 0)),
        pl.BlockSpec((N_GATES, 1), lambda i, b, j, hb, tb: (0, 0)),
    ]
    args = [x, sh, sc, w_head, w_tail, w_gate_t, b_gate] + slabs
    in_specs += [slab(s) for s in slabs]
    if tables is not None:
        in_specs += [pl.BlockSpec((tm, HEAD_W), lambda i, b, j, hb, tb: (i, 0))] * 2
        args += list(tables)
    grid_spec = pltpu.PrefetchScalarGridSpec(
        num_scalar_prefetch=2,
        grid=(T // tm, B, n_sec),
        in_specs=in_specs,
        out_specs=(pl.BlockSpec((1, tm, tn), lambda i, b, j, hb, tb: (b, i, j)),
                   pl.BlockSpec((1, N_GATES, tm), lambda i, b, j, hb, tb: (b, 0, i)))
        + tuple(slab(s) for s in slabs),
        scratch_shapes=[pltpu.VMEM((tm, D), BF16)],
    )
    outs = pl.pallas_call(
        functools.partial(_proj_kernel, kinds, srcs, len(slabs)),
        out_shape=(jax.ShapeDtypeStruct((B, T, n_sec * tn), BF16),
                   jax.ShapeDtypeStruct((B, N_GATES, T), F32))
        + tuple(jax.ShapeDtypeStruct(s.shape, BF16) for s in slabs),
        grid_spec=grid_spec,
        compiler_params=_cparams("parallel", "parallel", "arbitrary"),
        name="proj_lat" if tables is not None else "proj_ctx",
    )(jnp.asarray(hb, I32), jnp.asarray(tb, I32), *args)
    return outs[0], outs[1], [o.reshape(a.shape) for o, a in zip(outs[2:], side_cast)]


def _ret_build(dl_ref, q_ref, k_ref, v_ref, rg_ref, ck_ref, cv_ref, o_ref,
               sf_ref, sb_ref, fs_ref, bs_ref, dec_ref, d_ref):
    h = pl.program_id(0)
    L = SCAN_L
    n_chunks = q_ref.shape[1] // L
    n_ctx_chunks = ck_ref.shape[1] // L
    lgf = _log_sigmoid(jnp.full((1, 1), dl_ref[0, h], F32))
    lgb = _log_sigmoid(jnp.full((1, 1), dl_ref[1, h], F32))

    @pl.when(pl.program_id(1) == 0)
    def _():
        ri = lax.broadcasted_iota(I32, (L, L), 0)
        ci = lax.broadcasted_iota(I32, (L, L), 1)
        rel = (ri - ci).astype(F32)
        d_ref[...] = jnp.where(rel >= 0.0, jnp.exp(jnp.maximum(rel, 0.0) * lgf),
                               jnp.exp(jnp.maximum(-rel, 0.0) * lgb))
        row = lax.broadcasted_iota(I32, (L, HEAD_W), 0).astype(F32)
        dec_ref[0] = jnp.exp((row + 1.0) * lgf)
        dec_ref[1] = jnp.exp((L - 1.0 - row) * lgf)
        dec_ref[2] = jnp.exp((L - row) * lgb)
        dec_ref[3] = jnp.exp(row * lgb)

    cdf = jnp.exp(L * lgf)
    cdb = jnp.exp(L * lgb)

    def update(s_ref, kc, vc, kd, cd):
        kdec = (kc.astype(F32) * kd).astype(BF16)
        s_ref[...] = s_ref[...] * cd + lax.dot_general(kdec, vc, TN_DIMS, preferred_element_type=F32)

    sf_ref[...] = jnp.zeros_like(sf_ref)
    sb_ref[...] = jnp.zeros_like(sb_ref)
    for c in range(n_ctx_chunks):
        update(sf_ref, ck_ref[0, c * L:(c + 1) * L, :], cv_ref[0, c * L:(c + 1) * L, :], dec_ref[1], cdf)
    for c in reversed(range(n_ctx_chunks)):
        update(sb_ref, ck_ref[0, c * L:(c + 1) * L, :], cv_ref[0, c * L:(c + 1) * L, :], dec_ref[3], cdb)

    def state_pass(i, carry):
        cb = n_chunks - 1 - i
        rf = pl.multiple_of(i * L, L)
        rb = pl.multiple_of(cb * L, L)
        fs_ref[i] = sf_ref[...].astype(BF16)
        bs_ref[cb] = sb_ref[...].astype(BF16)
        update(sf_ref, k_ref[0, pl.ds(rf, L), :], v_ref[0, pl.ds(rf, L), :], dec_ref[1], cdf)
        update(sb_ref, k_ref[0, pl.ds(rb, L), :], v_ref[0, pl.ds(rb, L), :], dec_ref[3], cdb)
        return carry

    def finish_states():
        fs_ref[n_chunks - 1] = sf_ref[...].astype(BF16)
        bs_ref[0] = sb_ref[...].astype(BF16)

    def out_chunk(c):
        r0 = pl.multiple_of(c * L, L)
        q = q_ref[0, pl.ds(r0, L), :]
        k = k_ref[0, pl.ds(r0, L), :]
        v = v_ref[0, pl.ds(r0, L), :]
        s = lax.dot_general(q, k, NT_DIMS, preferred_element_type=F32)
        att = (s * d_ref[...]).astype(BF16)
        o = jnp.dot(att, v, preferred_element_type=F32)
        o = o + jnp.dot(q, fs_ref[c], preferred_element_type=F32) * dec_ref[0]
        o = o + jnp.dot(q, bs_ref[c], preferred_element_type=F32) * dec_ref[2]
        rg = rg_ref[0, pl.ds(r0, L), :].astype(F32)
        o_ref[0, pl.ds(r0, L), :] = (_layer_norm(o) * _silu(rg)).astype(BF16)

    return state_pass, finish_states, out_chunk


def _ret_scratch(n_chunks):
    return [pltpu.VMEM((HEAD_W, HEAD_W), F32),
            pltpu.VMEM((HEAD_W, HEAD_W), F32),
            pltpu.VMEM((n_chunks, HEAD_W, HEAD_W), BF16),
            pltpu.VMEM((n_chunks, HEAD_W, HEAD_W), BF16),
            pltpu.VMEM((4, SCAN_L, HEAD_W), F32),
            pltpu.VMEM((SCAN_L, SCAN_L), F32)]


def _mlstm_build(qp_ref, kp_ref, v_ref, mo_ref, ckp_ref, cv_ref, gt_ref, cgt_ref,
                 wq_ref, bq_ref, wk_ref, bk_ref, o_ref,
                 tab_ref, row_ref,
                 cf_ref, mf_ref, cb_ref, mb_ref, cfs_ref, mfs_ref, cbs_ref, mbs_ref, mask_ref,
                 xk_ref, xq_ref, q_ref, k_ref, ck_ref):
    L = SCAN_L
    T = qp_ref.shape[1]
    Tc = ckp_ref.shape[1]
    n_chunks = T // L
    n_ctx_chunks = Tc // L
    CV = CONV_ROWS

    def conv_stage(src_ref, xs_ref, t_len):
        xs_ref[pl.ds(0, 8), :] = jnp.zeros((8, HEAD_W), F32)
        xs_ref[pl.ds(8 + t_len, 8), :] = jnp.zeros((8, HEAD_W), F32)
        xs_ref[pl.ds(8, t_len), :] = src_ref[0].astype(F32)

    def conv_rows(xs_ref, c, w_ref, b_ref, dst_ref, scale):
        w = w_ref[...]
        r0 = pl.multiple_of(c * CV, CV)
        win = xs_ref[pl.ds(r0, CV + 16), :]
        prev = pltpu.roll(win, 1, 0)[8:8 + CV, :]
        cur = win[8:8 + CV, :]
        nxt = pltpu.roll(win, CV + 15, 0)[8:8 + CV, :]
        y = _silu(prev * w[0:1, :] + cur * w[1:2, :] + nxt * w[2:3, :] + b_ref[...])
        if scale != 1.0:
            y = y * scale
        dst_ref[pl.ds(r0, CV), :] = y.astype(BF16)

    per_chunk = L // CV

    def conv_k(c):
        for u in range(per_chunk):
            conv_rows(xk_ref, c * per_chunk + u, wk_ref, bk_ref, k_ref, KEY_SCALE)

    def conv_q(c):
        for u in range(per_chunk):
            conv_rows(xq_ref, c * per_chunk + u, wq_ref, bq_ref, q_ref, 1.0)

    conv_stage(ckp_ref, xk_ref, Tc)
    for c in range(Tc // CV):
        conv_rows(xk_ref, c, wk_ref, bk_ref, ck_ref, KEY_SCALE)
    conv_stage(kp_ref, xk_ref, T)
    conv_stage(qp_ref, xq_ref, T)
    conv_k(0)
    conv_k(n_chunks - 1)
    n_k_steps = n_chunks // 2 - 1

    def conv_step(i, keys):
        if keys:
            conv_k(i + 1)
            conv_k(n_chunks - 2 - i)
        else:
            conv_q(2 * (i - n_k_steps))
            conv_q(2 * (i - n_k_steps) + 1)

    ri = lax.broadcasted_iota(I32, (L, L), 0)
    ci = lax.broadcasted_iota(I32, (L, L), 1)
    tri_u = (ri <= ci).astype(BF16)
    lane8 = lax.broadcasted_iota(I32, (8, L), 1)
    sub8 = lax.broadcasted_iota(I32, (8, L), 0)
    sel_r = lax.broadcasted_iota(I32, (24, 8 * 128), 0) % 8
    sel_c = lax.broadcasted_iota(I32, (24, 8 * 128), 1) // 128
    sel3 = (sel_r == sel_c).astype(BF16)
    ones_cols = jnp.ones((L, AUG_W - HEAD_W), BF16)

    def chunk_tables(g8, n_used, state_only):
        i_f, i_b = g8[0], g8[2]
        lf_f, lf_b = _log_sigmoid(g8[1]), _log_sigmoid(g8[3])
        cs3 = jnp.dot(_split3(jnp.concatenate([lf_f, lf_b], axis=0)), tri_u,
                      preferred_element_type=F32)
        cs = cs3[0:16] + cs3[16:32] + cs3[32:48]
        b_f = cs[0:8]
        b_b = cs[8:16, L - 1:L] - cs[8:16] + lf_b
        z_f = i_f - b_f
        z_b = i_b - b_b
        g_f = b_f[:, L - 1:L] - b_f + i_f
        g_b = b_b[:, 0:1] - b_b + i_b
        mf, mb = z_f, z_b
        s = 1
        while s < L:
            mf = jnp.maximum(mf, jnp.where(lane8 >= s, pltpu.roll(mf, s, 1), NEG_INF))
            mb = jnp.maximum(mb, jnp.where(lane8 < L - s, pltpu.roll(mb, L - s, 1), NEG_INF))
            s *= 2
        mb = jnp.where(lane8 < L - 1, pltpu.roll(mb, L - 1, 1), NEG_INF)
        reps = [None if state_only and t not in (2, 5) else
                lax.dot_general(_split3(val), sel3[:, 0:n_used * 128], TN_DIMS, preferred_element_type=F32)
                for t, val in enumerate((mf, b_f, g_f, mb, b_b, g_b))]

        def rows_of(c):
            out = jnp.zeros((8, L), F32)
            for r, val in enumerate((z_f, z_b, g_f, g_b, b_f, b_b)):
                out = jnp.where(sub8 == r, val[c:c + 1], out)
            return out

        return rows_of, reps

    lat_rows, lat_reps = chunk_tables(gt_ref[0, 0], n_chunks, False)
    for c in range(n_chunks):
        row_ref[c] = lat_rows(c)
        for t in range(N_TAB):
            tab_ref[t, c * L:(c + 1) * L, :] = lat_reps[t][:, c * 128:(c + 1) * 128]

    def lanes2(x):
        return jnp.concatenate([x, x], axis=1)

    def advance(k, v, g_rep, g_row, b_last, c_ref, m_ref):
        m = m_ref[...]
        m_new = jnp.maximum(b_last + m, jnp.max(g_row, axis=-1, keepdims=True))
        kw = (k.astype(F32) * jnp.exp(lanes2(g_rep) - m_new)).astype(BF16)
        v_aug = jnp.concatenate([v, ones_cols], axis=1)
        c_ref[...] = jnp.exp(b_last + m - m_new) * c_ref[...] + lax.dot_general(
            kw, v_aug, TN_DIMS, preferred_element_type=F32)
        m_ref[...] = m_new

    for r in (cf_ref, mf_ref, cb_ref, mb_ref):
        r[...] = jnp.zeros_like(r)
    ctx_rows, ctx_reps = chunk_tables(cgt_ref[0, 0], n_ctx_chunks, True)
    for c in range(n_ctx_chunks):
        rows = ctx_rows(c)
        advance(ck_ref[c * L:(c + 1) * L, :], cv_ref[0, c * L:(c + 1) * L, :],
                ctx_reps[2][:, c * 128:(c + 1) * 128], rows[2:3], rows[4:5, L - 1:L], cf_ref, mf_ref)
    for c in reversed(range(n_ctx_chunks)):
        rows = ctx_rows(c)
        advance(ck_ref[c * L:(c + 1) * L, :], cv_ref[0, c * L:(c + 1) * L, :],
                ctx_reps[5][:, c * 128:(c + 1) * 128], rows[3:4], rows[5:6, 0:1], cb_ref, mb_ref)

    def state_pass(i, carry, keys):
        cb = n_chunks - 1 - i
        rf = pl.multiple_of(i * L, L)
        rb = pl.multiple_of(cb * L, L)
        cfs_ref[i] = cf_ref[...].astype(BF16)
        mfs_ref[i] = mf_ref[...]
        cbs_ref[cb] = cb_ref[...].astype(BF16)
        mbs_ref[cb] = mb_ref[...]
        rows_f = row_ref[i]
        rows_b = row_ref[cb]
        advance(k_ref[pl.ds(rf, L), :], v_ref[0, pl.ds(rf, L), :], tab_ref[2, pl.ds(rf, L), :],
                rows_f[2:3], rows_f[4:5, L - 1:L], cf_ref, mf_ref)
        advance(k_ref[pl.ds(rb, L), :], v_ref[0, pl.ds(rb, L), :], tab_ref[5, pl.ds(rb, L), :],
                rows_b[3:4], rows_b[5:6, 0:1], cb_ref, mb_ref)
        conv_step(i, keys)
        return carry

    def finish_states():
        cfs_ref[n_chunks - 1] = cf_ref[...].astype(BF16)
        mfs_ref[n_chunks - 1] = mf_ref[...]
        cbs_ref[0] = cb_ref[...].astype(BF16)
        mbs_ref[0] = mb_ref[...]

    def direction(q, v_aug, s, z_row, zmax_rep, b_rep, mask, c_in, m_in):
        mx = jnp.maximum(zmax_rep, m_in)
        att = s * jnp.exp((z_row - lanes2(mx)) + mask)
        na = jnp.dot(att.astype(BF16), v_aug, preferred_element_type=F32)
        qa = jnp.dot(q, c_in, preferred_element_type=F32)
        a = jnp.exp(m_in - mx)
        num = na[:, 0:HEAD_W] + lanes2(a) * qa[:, 0:HEAD_W]
        den = na[:, HEAD_W:] + a * qa[:, HEAD_W:]
        scale = 1.0 / jnp.maximum(jnp.abs(den), jnp.exp(-(b_rep + mx)))
        return num * lanes2(scale)

    @pl.when(pl.program_id(1) == 0)
    def _():
        mask_ref[0] = jnp.where(ci <= ri, 0.0, NEG_INF)
        mask_ref[1] = jnp.where(ci > ri, 0.0, NEG_INF)

    def out_chunk(c):
        r0 = pl.multiple_of(c * L, L)
        q = q_ref[pl.ds(r0, L), :]
        k = k_ref[pl.ds(r0, L), :]
        v_aug = jnp.concatenate([v_ref[0, pl.ds(r0, L), :], ones_cols], axis=1)
        s = lax.dot_general(q, k, NT_DIMS, preferred_element_type=F32)
        rows = row_ref[c]
        tot = direction(q, v_aug, s, rows[0:1], tab_ref[0, pl.ds(r0, L), :], tab_ref[1, pl.ds(r0, L), :],
                        mask_ref[0], cfs_ref[c], mfs_ref[c])
        tot = tot + direction(q, v_aug, s, rows[1:2], tab_ref[3, pl.ds(r0, L), :],
                              tab_ref[4, pl.ds(r0, L), :], mask_ref[1], cbs_ref[c], mbs_ref[c])
        mo = mo_ref[0, pl.ds(r0, L), :].astype(F32)
        o_ref[0, pl.ds(r0, L), :] = (_layer_norm(tot) * jax.nn.sigmoid(mo)).astype(BF16)

    return state_pass, finish_states, out_chunk, n_k_steps


def _mlstm_scratch(T, Tc, n_chunks):
    state = [pltpu.VMEM((HEAD_W, AUG_W), F32), pltpu.VMEM((1, 1), F32)]
    snaps = [pltpu.VMEM((n_chunks, HEAD_W, AUG_W), BF16), pltpu.VMEM((n_chunks, 1, 1), F32)]
    return [pltpu.VMEM((N_TAB, T, 128), F32), pltpu.VMEM((n_chunks, 8, SCAN_L), F32)] \
        + state + state + snaps + snaps + [pltpu.VMEM((2, SCAN_L, SCAN_L), F32)] \
        + [pltpu.VMEM((T + 16, HEAD_W), F32), pltpu.VMEM((T + 16, HEAD_W), F32),
           pltpu.VMEM((T, HEAD_W), BF16), pltpu.VMEM((T, HEAD_W), BF16), pltpu.VMEM((Tc, HEAD_W), BF16)]


def _scan_kernel(n_ret_scratch, dl_ref, rq_ref, rk_ref, rv_ref, rg_ref, rck_ref, rcv_ref,
                 mq_ref, mk_ref, mv_ref, mo_ref, mck_ref, mcv_ref, gt_ref, cgt_ref,
                 wq_ref, bq_ref, wk_ref, bk_ref, r_ref, m_ref, *scratch):
    n_chunks = rq_ref.shape[1] // SCAN_L
    ret = _ret_build(dl_ref, rq_ref, rk_ref, rv_ref, rg_ref, rck_ref, rcv_ref, r_ref,
                     *scratch[:n_ret_scratch])
    mls = _mlstm_build(mq_ref, mk_ref, mv_ref, mo_ref, mck_ref, mcv_ref, gt_ref, cgt_ref,
                       wq_ref, bq_ref, wk_ref, bk_ref, m_ref, *scratch[n_ret_scratch:])

    def state_pass(i, carry, keys):
        ret[0](i, carry)
        mls[0](i, carry, keys)
        return carry

    n_k_steps = mls[3]
    lax.fori_loop(0, n_k_steps, functools.partial(state_pass, keys=True), 0)
    lax.fori_loop(n_k_steps, n_chunks - 1, functools.partial(state_pass, keys=False), 0)
    ret[1]()
    mls[1]()

    def out_pass(i, carry):
        mls[2](2 * i)
        ret[2](2 * i)
        mls[2](2 * i + 1)
        ret[2](2 * i + 1)
        return carry

    lax.fori_loop(0, n_chunks // 2, out_pass, 0)


def _scans(decay_logit, p_lat, p_ctx, gt, cgt, conv_w, conv_b, ret_lat, ret_ctx, ml_lat, ml_ctx):
    B, T, _ = p_lat.shape
    Tc = p_ctx.shape[1]
    assert T % (2 * SCAN_L) == 0 and Tc % SCAN_L == 0 and T // SCAN_L <= 8
    n_chunks = T // SCAN_L

    def lat(sec):
        return pl.BlockSpec((1, T, HEAD_W), lambda h, b: (b, 0, sec * HEADS + h))

    def cx(sec):
        return pl.BlockSpec((1, Tc, HEAD_W), lambda h, b: (b, 0, sec * HEADS + h))

    gates = pl.BlockSpec((1, 1, N_GK, 8, SCAN_L), lambda h, b: (b, h, 0, 0, 0))
    out = pl.BlockSpec((1, T, HEAD_W), lambda h, b: (b, 0, h))
    conv_specs = [pl.BlockSpec((3, HEAD_W), lambda h, b: (0, h)),
                  pl.BlockSpec((1, HEAD_W), lambda h, b: (0, h)),
                  pl.BlockSpec((3, HEAD_W), lambda h, b: (0, HEADS + h)),
                  pl.BlockSpec((1, HEAD_W), lambda h, b: (0, HEADS + h))]
    ret_scratch = _ret_scratch(n_chunks)
    return pl.pallas_call(
        functools.partial(_scan_kernel, len(ret_scratch)),
        out_shape=(jax.ShapeDtypeStruct((B, T, BRANCH_W), BF16),
                   jax.ShapeDtypeStruct((B, T, BRANCH_W), BF16)),
        grid=(HEADS, B),
        in_specs=[pl.BlockSpec(memory_space=pltpu.SMEM)]
        + [lat(s) for s in ret_lat] + [cx(s) for s in ret_ctx]
        + [lat(s) for s in ml_lat] + [cx(s) for s in ml_ctx] + [gates, gates] + conv_specs,
        out_specs=(out, out),
        scratch_shapes=ret_scratch + _mlstm_scratch(T, Tc, n_chunks),
        compiler_params=_cparams("arbitrary", "arbitrary"),
        name="scans",
    )(decay_logit, *([p_lat] * 4), *([p_ctx] * 2), *([p_lat] * 4), *([p_ctx] * 2), gt, cgt,
      conv_w, conv_b, conv_w, conv_b)


def _merge_kernel(alpha, r_ref, m_ref, gr_ref, gm_ref, x_ref, g1_ref, sh2_ref, sc2_ref,
                  lng_ref, lnb_ref, wr_ref, wm_ref, wo_ref, wrt_ref, brt_ref, e1_ref, e3_ref, e2_ref,
                  x1_ref, ua_ref, ub_ref, ri_ref, rw_ref, cnt_ref, e1b_ref, e3b_ref, e2b_ref,
                  carry_ref, u_ref):
    for src, dst in ((e1_ref, e1b_ref), (e3_ref, e3b_ref), (e2_ref, e2b_ref)):
        dst[...] = src[...].astype(BF16)

    @pl.when(jnp.logical_and(pl.program_id(0) == 0, pl.program_id(1) == 0))
    def _():
        carry_ref[...] = jnp.zeros_like(carry_ref)
        tm = x_ref.shape[1]
        r = lax.broadcasted_iota(I32, (tm, tm), 0)
        c = lax.broadcasted_iota(I32, (tm, tm), 1)
        u_ref[...] = (r < c).astype(BF16)

    yr = jnp.dot(r_ref[0], wr_ref[...], preferred_element_type=F32)
    ym = jnp.dot(m_ref[0], wm_ref[...], preferred_element_type=F32)
    y = jax.nn.sigmoid(gr_ref[0].astype(F32)) * yr + jax.nn.sigmoid(gm_ref[0].astype(F32)) * ym
    yo = jnp.dot(y.astype(BF16), wo_ref[...], preferred_element_type=F32)
    x1 = _layer_norm(alpha * x_ref[0] + g1_ref[0] * yo) * lng_ref[...] + lnb_ref[...]
    x1_ref[0] = x1
    u2 = _layer_norm(x1) * (1.0 + sc2_ref[0]) + sh2_ref[0]
    ua_ref[0] = _pack_pairs(u2[:, 0:PACK_W], u2[:, PACK_W:2 * PACK_W])
    ub_ref[0] = _pack_pairs(u2[:, 2 * PACK_W:3 * PACK_W], u2[:, 3 * PACK_W:4 * PACK_W])
    lt = lax.dot_general(wrt_ref[...], u2.astype(BF16), NT_DIMS, preferred_element_type=F32) + brt_ref[...]
    _route_tile(lt, ri_ref, rw_ref, cnt_ref, carry_ref, u_ref)


def _route_tile(lt, ri_ref, rw_ref, cnt_ref, carry_ref, u_ref):
    tm = lt.shape[1]
    lg = lt[0:N_GROUPS, :]
    eg = jnp.exp(lg - jnp.max(lg, axis=0, keepdims=True))
    pg = eg / jnp.sum(eg, axis=0, keepdims=True)
    pg_top = jnp.max(pg, axis=0, keepdims=True)
    rows_g = lax.broadcasted_iota(I32, pg.shape, 0)
    g_idx = jnp.min(jnp.where(pg == pg_top, rows_g, N_GROUPS), axis=0, keepdims=True)

    le = jnp.zeros((EXP_PER_GROUP, tm), F32)
    for g in range(N_GROUPS):
        lo = 8 + g * EXP_PER_GROUP
        le = jnp.where(g_idx == g, lt[lo:lo + EXP_PER_GROUP, :], le)
    ee = jnp.exp(le - jnp.max(le, axis=0, keepdims=True))
    pe = ee / jnp.sum(ee, axis=0, keepdims=True)
    rows_e = lax.broadcasted_iota(I32, pe.shape, 0)
    v1 = jnp.max(pe, axis=0, keepdims=True)
    i1 = jnp.min(jnp.where(pe == v1, rows_e, EXP_PER_GROUP), axis=0, keepdims=True)
    pe2 = jnp.where(rows_e == i1, -1.0, pe)
    v2 = jnp.max(pe2, axis=0, keepdims=True)
    i2 = jnp.min(jnp.where(pe2 == v2, rows_e, EXP_PER_GROUP), axis=0, keepdims=True)
    den = v1 + v2
    rw_ref[...] = jnp.zeros_like(rw_ref)
    rw_ref[0:1, :] = pg_top * v1 / den
    rw_ref[1:2, :] = pg_top * v2 / den
    e1 = g_idx * EXP_PER_GROUP + i1
    e2 = g_idx * EXP_PER_GROUP + i2

    rows_x = lax.broadcasted_iota(I32, (N_EXPERTS, tm), 0)
    oh1 = (rows_x == e1).astype(F32)
    oh2 = (rows_x == e2).astype(F32)
    both = oh1 + oh2
    before = carry_ref[:, 0:1] + jnp.dot(both.astype(BF16), u_ref[...], preferred_element_type=F32)
    ri_ref[0:1, :] = e1
    ri_ref[1:2, :] = e2
    ri_ref[2:3, :] = jnp.sum(oh1 * before, axis=0, keepdims=True).astype(I32)
    ri_ref[3:4, :] = jnp.sum(oh2 * before, axis=0, keepdims=True).astype(I32)
    carry_ref[...] = carry_ref[...] + jnp.sum(both, axis=1, keepdims=True)
    cnt_ref[...] = carry_ref[...].astype(I32)


def _merge(alpha, r, m, p_lat, sec_gates, x, g1, sh2, sc2, lng, lnb, wr, wm, wo, wrt, brt, expert_w):
    B, T, D = x.shape
    tm = MERGE_TM
    per_b = T // tm
    n = B * T
    n_steps = B * per_b
    sliced = [w.reshape(n_steps, w.shape[0] * w.shape[1] // n_steps, w.shape[2]) for w in expert_w]
    assert all(s.shape[1] % 16 == 0 and s.size == w.size for s, w in zip(sliced, expert_w))

    def slab(s):
        return pl.BlockSpec((1,) + s.shape[1:], lambda b, i: (b * per_b + i, 0, 0))

    def tile(w):
        return pl.BlockSpec((1, tm, w), lambda b, i: (b, i, 0))

    def sec(s):
        return pl.BlockSpec((1, tm, BRANCH_W), lambda b, i: (b, i, s))

    def mod():
        return pl.BlockSpec((1, 1, D), lambda b, i: (b, 0, 0))

    def const(shape):
        return pl.BlockSpec(shape, lambda b, i: (0,) * len(shape))

    outs = pl.pallas_call(
        functools.partial(_merge_kernel, alpha),
        out_shape=(jax.ShapeDtypeStruct((B, T, D), F32),
                   jax.ShapeDtypeStruct((B, T, PACK_W), U32),
                   jax.ShapeDtypeStruct((B, T, PACK_W), U32),
                   jax.ShapeDtypeStruct((4, n), I32),
                   jax.ShapeDtypeStruct((8, n), F32),
                   jax.ShapeDtypeStruct((N_EXPERTS, 128), I32))
        + tuple(jax.ShapeDtypeStruct(s.shape, BF16) for s in sliced),
        grid=(B, per_b),
        in_specs=[tile(BRANCH_W), tile(BRANCH_W), sec(sec_gates[0]), sec(sec_gates[1]), tile(D),
                  mod(), mod(), mod(), const((1, D)), const((1, D)),
                  const((BRANCH_W, D)), const((BRANCH_W, D)), const((D, D)),
                  const((ROUTE_ROWS, D)), const((ROUTE_ROWS, 1))] + [slab(s) for s in sliced],
        out_specs=(tile(D), tile(PACK_W), tile(PACK_W),
                   pl.BlockSpec((4, tm), lambda b, i: (0, b * per_b + i)),
                   pl.BlockSpec((8, tm), lambda b, i: (0, b * per_b + i)),
                   pl.BlockSpec((N_EXPERTS, 128), lambda b, i: (0, 0)))
        + tuple(slab(s) for s in sliced),
        scratch_shapes=[pltpu.VMEM((N_EXPERTS, 128), F32), pltpu.VMEM((tm, tm), BF16)],
        compiler_params=_cparams("arbitrary", "arbitrary"),
        name="merge",
    )(r, m, p_lat, p_lat, x, g1, sh2, sc2, lng, lnb, wr, wm, wo, wrt, brt, *sliced)
    return outs[:6] + tuple(o.reshape(w.shape) for o, w in zip(outs[6:], expert_w))


def _sc_mesh():
    return plsc.VectorSubcoreMesh(core_axis_name="c", subcore_axis_name="s")


def _sc_scatter2(rows_a, rows_b, idx0, idx1, n_out):
    m, w = rows_a.shape
    out = jax.ShapeDtypeStruct((n_out, w), rows_a.dtype)

    @functools.partial(pl.kernel, out_type=(out, out), mesh=_sc_mesh(), scratch_types=[])
    def k(xa_hbm, xb_hbm, i0_hbm, i1_hbm, oa_hbm, ob_hbm):
        for x_hbm, o_hbm in ((xa_hbm, oa_hbm), (xb_hbm, ob_hbm)):
            def body(x_vmem, i0_vmem, i1_vmem, o_hbm=o_hbm):
                pltpu.sync_copy(x_vmem, o_hbm.at[i0_vmem.at[0]])
                pltpu.sync_copy(x_vmem, o_hbm.at[i1_vmem.at[0]])

            pltpu.emit_pipeline(
                body,
                grid=(m // SC_WIN,),
                in_specs=[pl.BlockSpec((SC_WIN, w), lambda i: (i, 0)),
                          pl.BlockSpec((1, SC_WIN), lambda i: (0, i)),
                          pl.BlockSpec((1, SC_WIN), lambda i: (0, i))],
                out_specs=[],
                core_axis_name=("c", "s"),
                dimension_semantics=(pltpu.PARALLEL,),
            )(x_hbm, i0_hbm, i1_hbm)

    return k(rows_a, rows_b, idx0.reshape(1, m), idx1.reshape(1, m))


def _sc_gather(table_a, table_b, idx):
    m = idx.shape[0]
    w = table_a.shape[1]
    out = jax.ShapeDtypeStruct((m, w), table_a.dtype)

    @functools.partial(pl.kernel, out_type=(out, out), mesh=_sc_mesh(), scratch_types=[])
    def k(ta_hbm, tb_hbm, i_hbm, oa_hbm, ob_hbm):
        for t_hbm, o_hbm in ((ta_hbm, oa_hbm), (tb_hbm, ob_hbm)):
            def body(i_vmem, o_vmem, t_hbm=t_hbm):
                pltpu.sync_copy(t_hbm.at[i_vmem.at[0]], o_vmem)

            pltpu.emit_pipeline(
                body,
                grid=(m // SC_WIN,),
                in_specs=[pl.BlockSpec((1, SC_WIN), lambda i: (0, i))],
                out_specs=[pl.BlockSpec((SC_WIN, w), lambda i: (i, 0))],
                core_axis_name=("c", "s"),
                dimension_semantics=(pltpu.PARALLEL,),
            )(i_hbm, o_hbm)

    return k(table_a, table_b, idx.reshape(1, m))


def _expert_kernel(be_ref, nv_ref, xa_ref, xb_ref, w1_ref, w3_ref, w2_ref, ya_ref, yb_ref):
    j = pl.program_id(0)
    nv = nv_ref[j]

    @pl.when(nv > 0)
    def _():
        valid = lax.broadcasted_iota(I32, xa_ref.shape, 0) < nv
        zero = jnp.zeros(xa_ref.shape, U32)
        parts = _unpack_pairs(jnp.where(valid, xa_ref[...], zero)) + \
            _unpack_pairs(jnp.where(valid, xb_ref[...], zero))
        x = jnp.concatenate([p.astype(BF16) for p in parts], axis=1)
        h1 = jnp.dot(x, w1_ref[0], preferred_element_type=F32)
        h3 = jnp.dot(x, w3_ref[0], preferred_element_type=F32)
        y = jnp.dot((_silu(h1) * h3).astype(BF16), w2_ref[0], preferred_element_type=F32)
        ya_ref[...] = _pack_pairs(y[:, 0:PACK_W], y[:, PACK_W:2 * PACK_W])
        yb_ref[...] = _pack_pairs(y[:, 2 * PACK_W:3 * PACK_W], y[:, 3 * PACK_W:4 * PACK_W])

    @pl.when(nv == 0)
    def _():
        ya_ref[...] = jnp.zeros_like(ya_ref)
        yb_ref[...] = jnp.zeros_like(yb_ref)


def _experts(block_exp, n_valid, xa, xb, w1, w3, w2):
    n_slots = xa.shape[0]
    n_blocks = n_slots // MOE_BLK
    d, de = w1.shape[1], w1.shape[2]
    slot = pl.BlockSpec((MOE_BLK, PACK_W), lambda j, be, nv: (j, 0))
    grid_spec = pltpu.PrefetchScalarGridSpec(
        num_scalar_prefetch=2,
        grid=(n_blocks,),
        in_specs=[slot, slot,
                  pl.BlockSpec((1, d, de), lambda j, be, nv: (be[j], 0, 0)),
                  pl.BlockSpec((1, d, de), lambda j, be, nv: (be[j], 0, 0)),
                  pl.BlockSpec((1, de, d), lambda j, be, nv: (be[j], 0, 0))],
        out_specs=(slot, slot),
    )
    return pl.pallas_call(
        _expert_kernel,
        out_shape=(jax.ShapeDtypeStruct((n_slots, PACK_W), U32),
                   jax.ShapeDtypeStruct((n_slots, PACK_W), U32)),
        grid_spec=grid_spec,
        compiler_params=_cparams("parallel"),
        name="experts",
    )(block_exp, n_valid, xa, xb, w1, w3, w2)


def _final_kernel(alpha, x1_ref, a0_ref, b0_ref, a1_ref, b1_ref, w_ref, g2_ref, lng_ref, lnb_ref, o_ref):
    w = w_ref[...].T
    w0 = w[:, 0:1]
    w1 = w[:, 1:2]
    parts0 = _unpack_pairs(a0_ref[...]) + _unpack_pairs(b0_ref[...])
    parts1 = _unpack_pairs(a1_ref[...]) + _unpack_pairs(b1_ref[...])
    f = jnp.concatenate([w0 * p0 + w1 * p1 for p0, p1 in zip(parts0, parts1)], axis=1)
    o_ref[0] = _layer_norm(alpha * x1_ref[0] + g2_ref[0] * f) * lng_ref[...] + lnb_ref[...]


def _final(alpha, x1, ya, yb, w, g2, lng, lnb):
    B, T, D = x1.shape
    tm = min(FINAL_TM, T)
    per_b = T // tm
    n_tiles = B * per_b

    def rows(k):
        return pl.BlockSpec((tm, PACK_W), lambda b, i: (k * n_tiles + b * per_b + i, 0))

    return pl.pallas_call(
        functools.partial(_final_kernel, alpha),
        out_shape=jax.ShapeDtypeStruct((B, T, D), F32),
        grid=(B, per_b),
        in_specs=[pl.BlockSpec((1, tm, D), lambda b, i: (b, i, 0)),
                  rows(0), rows(0), rows(1), rows(1),
                  pl.BlockSpec((8, tm), lambda b, i: (0, b * per_b + i)),
                  pl.BlockSpec((1, 1, D), lambda b, i: (b, 0, 0)),
                  pl.BlockSpec((1, D), lambda b, i: (0, 0)),
                  pl.BlockSpec((1, D), lambda b, i: (0, 0))],
        out_specs=pl.BlockSpec((1, tm, D), lambda b, i: (b, i, 0)),
        compiler_params=_cparams("parallel", "parallel"),
        name="final",
    )(x1, ya, yb, ya, yb, w, g2, lng, lnb)


def _rotary_tables(T):
    quarter = HEAD_W // 4
    freqs = ROPE_BASE ** (-jnp.arange(quarter, dtype=F32) / quarter)
    t = jnp.arange(T)
    ang_r = (t // GRID_W).astype(F32)[:, None] * freqs[None, :]
    ang_c = (t % GRID_W).astype(F32)[:, None] * freqs[None, :]
    cos = jnp.concatenate([jnp.cos(ang_r)] * 2 + [jnp.cos(ang_c)] * 2, axis=1)
    sin = jnp.concatenate([-jnp.sin(ang_r), jnp.sin(ang_r), -jnp.sin(ang_c), jnp.sin(ang_c)], axis=1)
    return cos, sin


def _per_head_gates(gt):
    B, _, T = gt.shape
    n_chunks = T // SCAN_L
    gth = gt.reshape(B, N_GK, HEADS, n_chunks, SCAN_L).transpose(0, 2, 1, 3, 4)
    return jnp.pad(gth, ((0, 0), (0, 0), (0, 0), (0, 8 - n_chunks), (0, 0)))


def _table_lookup(table, idx):
    sel = idx[..., None] == jnp.arange(table.shape[0], dtype=idx.dtype)
    return jnp.sum(jnp.where(sel, table, 0), axis=-1)


def kernel(x, c, ctx, c_ctx, w_ada, b_ada, w_in, b_mgate, ml_conv_w, ml_conv_b, ret_decay_logit, w_ret_branch, w_ml_branch, w_out, ln1_g, ln1_b, w_rg, b_rg, w_re, b_re, w_e1, w_e3, w_e2, ln2_g, ln2_b):
    B, T, D = x.shape
    depth = w_ada.shape[0]
    assert depth == 1 and D == BRANCH_W and T % GRID_W == 0
    alpha = (2 * depth) ** 0.25
    n_tok = B * T

    n_rows = -(-(B + 1) // 8) * 8
    cs = jnp.zeros((n_rows, D), F32).at[:B].set(c).at[B].set(c_ctx)
    mod = _ada(cs, w_ada[0], b_ada[0][None, :])
    sh1, sc1, g1, sh2, sc2, g2 = [mod[:B, None, i * D:(i + 1) * D] for i in range(6)]
    csh1 = mod[B, 0 * D:1 * D].reshape(1, 1, D)
    csc1 = mod[B, 1 * D:2 * D].reshape(1, 1, D)

    w = w_in[0]
    g_lo = 8 * BRANCH_W
    w_gate_t = w[:, g_lo:g_lo + N_GATES].T.astype(BF16)
    b_gate = b_mgate[0][:, None]
    w_head = w[:, :g_lo].astype(BF16)
    w_tail = w[:, g_lo + N_GATES:].astype(BF16)
    sec_lat = tuple(("h", s) for s in range(8)) + (("t", 0), ("t", 1))
    sec_ctx = (("h", 1), ("h", 2), ("h", 5), ("h", 6))
    kinds_lat = ("rot", "rot_scale") + ("plain",) * 8
    kinds_ctx = ("scale", "plain", "plain", "plain")
    p_lat, gt_lat, _ = _proj(x, sh1, sc1, w_head, w_tail, sec_lat, w_gate_t, b_gate, kinds_lat,
                             _rotary_tables(T))
    Tc = ctx.shape[1]
    p_ctx, gt_ctx, (w_rb, w_mb, w_ob) = _proj(
        ctx.reshape(1, B * Tc, D), csh1, csc1, w_head, w_tail, sec_ctx, w_gate_t, b_gate, kinds_ctx,
        side_cast=(w_ret_branch[0], w_ml_branch[0], w_out[0]))
    p_ctx = p_ctx.reshape(B, Tc, -1)
    gt_ctx = gt_ctx.reshape(N_GATES, B, Tc).transpose(1, 0, 2)

    ret, mls = _scans(ret_decay_logit[0], p_lat, p_ctx, _per_head_gates(gt_lat), _per_head_gates(gt_ctx),
                      ml_conv_w[0], ml_conv_b[0][None, :], (0, 1, 2, 3), (0, 1), (4, 5, 6, 7), (2, 3))

    wrt = jnp.zeros((ROUTE_ROWS, D), F32).at[:N_GROUPS].set(w_rg[0].T).at[8:8 + N_EXPERTS].set(w_re[0].T)
    brt = jnp.zeros((ROUTE_ROWS, 1), F32).at[:N_GROUPS, 0].set(b_rg[0]).at[8:8 + N_EXPERTS, 0].set(b_re[0])
    x1, ua, ub, ri, rw, cnt, we1, we3, we2 = _merge(
        alpha, ret, mls, p_lat, (8, 9), x, g1, sh2, sc2, ln1_g[0][None, :], ln1_b[0][None, :],
        w_rb, w_mb, w_ob, wrt.astype(BF16), brt, (w_e1[0], w_e3[0], w_e2[0]))

    counts = cnt[:, 0]
    padded = (counts + MOE_BLK - 1) // MOE_BLK * MOE_BLK
    pad_end = jnp.cumsum(padded)
    pad_off = pad_end - padded
    dest = _table_lookup(pad_off, ri[0:2]) + ri[2:4]
    n_blocks = (2 * n_tok) // MOE_BLK + N_EXPERTS
    n_slots = n_blocks * MOE_BLK
    block_start = jnp.arange(n_blocks, dtype=I32) * MOE_BLK
    block_exp = jnp.minimum((block_start[:, None] >= pad_end[None, :]).sum(1), N_EXPERTS - 1).astype(I32)
    n_valid = jnp.clip(_table_lookup(counts, block_exp) - (block_start - _table_lookup(pad_off, block_exp)),
                       0, MOE_BLK).astype(I32)

    xa, xb = _sc_scatter2(ua.reshape(n_tok, PACK_W), ub.reshape(n_tok, PACK_W), dest[0], dest[1], n_slots)
    ya, yb = _experts(block_exp, n_valid, xa, xb, we1, we3, we2)
    ga, gb = _sc_gather(ya, yb, dest.reshape(2 * n_tok))
    return _final(alpha, x1, ga, gb, rw, g2, ln2_g[0][None, :], ln2_b[0][None, :])
```

```python
import functools

import jax
import jax.numpy as jnp
from jax import lax
from jax.experimental import pallas as pl
from jax.experimental.pallas import tpu as pltpu
from jax.experimental.pallas import tpu_sc as plsc

F32 = jnp.float32
BF16 = jnp.bfloat16
U32 = jnp.uint32
I32 = jnp.int32
HIGHEST = lax.Precision.HIGHEST

HEADS = 4
HEAD_W = 256
BRANCH_W = HEADS * HEAD_W
GRID_W = 64
ROPE_BASE = 10000.0
N_GATES = 16
N_GK = N_GATES // HEADS
N_GROUPS = 4
EXP_PER_GROUP = 8
N_EXPERTS = N_GROUPS * EXP_PER_GROUP
LN_EPS = 1e-5
NEG_INF = -1e30
KEY_SCALE = HEAD_W ** -0.5

SCAN_L = 256
CONV_ROWS = 128
PROJ_TM = 2048
PROJ_SUB = 256
MERGE_TM = 512
FINAL_TM = 1024
MOE_BLK = 512
SC_WIN = 128
PACK_W = 256
ROUTE_ROWS = 64
N_TAB = 6
AUG_W = HEAD_W + 128
VMEM_LIMIT = 48 * 1024 * 1024

NT_DIMS = (((1,), (1,)), ((), ()))
TN_DIMS = (((0,), (0,)), ((), ()))


def _cparams(*sem):
    return pltpu.CompilerParams(dimension_semantics=sem, vmem_limit_bytes=VMEM_LIMIT)


def _layer_norm(x):
    mu = jnp.mean(x, axis=-1, keepdims=True)
    xc = x - mu
    var = jnp.mean(xc * xc, axis=-1, keepdims=True)
    return xc * lax.rsqrt(var + LN_EPS)


def _log_sigmoid(x):
    return jnp.minimum(x, 0.0) - jnp.log1p(jnp.exp(-jnp.abs(x)))


def _silu(x):
    return x * jax.nn.sigmoid(x)


def _pack_pairs(hi, lo):
    hb = lax.bitcast_convert_type(hi.astype(BF16).astype(F32), U32)
    lb = lax.bitcast_convert_type(lo.astype(BF16).astype(F32), U32)
    return (hb & jnp.uint32(0xFFFF0000)) | (lb >> 16)


def _unpack_pairs(p):
    hi = lax.bitcast_convert_type(p & jnp.uint32(0xFFFF0000), F32)
    lo = lax.bitcast_convert_type(p << 16, F32)
    return hi, lo


def _split3(x):
    hi = x.astype(BF16).astype(F32)
    r1 = x - hi
    mid = r1.astype(BF16).astype(F32)
    lo = (r1 - mid).astype(BF16).astype(F32)
    return jnp.concatenate([hi, mid, lo], axis=0).astype(BF16)


def _ada_kernel(c_ref, w_ref, b_ref, o_ref):
    s = _silu(c_ref[...])
    o_ref[...] = jnp.dot(s, w_ref[...], precision=HIGHEST, preferred_element_type=F32) + b_ref[...]


def _ada(cs, w, b):
    rows, d = cs.shape
    cols = w.shape[1]
    tn = 1024
    return pl.pallas_call(
        _ada_kernel,
        out_shape=jax.ShapeDtypeStruct((rows, cols), F32),
        grid=(cols // tn,),
        in_specs=[pl.BlockSpec((rows, d), lambda j: (0, 0)),
                  pl.BlockSpec((d, tn), lambda j: (0, j)),
                  pl.BlockSpec((1, tn), lambda j: (0, j))],
        out_specs=pl.BlockSpec((rows, tn), lambda j: (0, j)),
        compiler_params=_cparams("parallel"),
        name="ada",
    )(cs, w, b)


def _proj_kernel(kinds, srcs, n_cast, hb_ref, tb_ref, x_ref, sh_ref, sc_ref, wh_ref, wt_ref, wg_ref,
                 bg_ref, *rest):
    cast_in, rest = rest[:n_cast], rest[n_cast:]
    if "rot" in kinds or "rot_scale" in kinds:
        cos_ref, sin_ref, rest = rest[0], rest[1], rest[2:]
    o_ref, gt_ref = rest[0], rest[1]
    cast_out, u_ref = rest[2:2 + n_cast], rest[2 + n_cast]
    for src, dst in zip(cast_in, cast_out):
        dst[...] = src[...].astype(BF16)
    j = pl.program_id(2)
    tm = x_ref.shape[1]
    sub = min(PROJ_SUB, tm)

    def rotary(acc, rows, scale):
        for s in range(acc.shape[1] // 128):
            a = acc[:, s * 128:(s + 1) * 128]
            half = s % 2
            cs = cos_ref[rows, half * 128:(half + 1) * 128]
            sn = sin_ref[rows, half * 128:(half + 1) * 128]
            r = a * cs + pltpu.roll(a, 64, 1) * sn
            if scale != 1.0:
                r = r * scale
            o_ref[0, rows, s * 128:(s + 1) * 128] = r.astype(BF16)

    def section(kind, first, src):
        w_ref = wh_ref if src == "h" else wt_ref
        for r in range(tm // sub):
            rows = slice(r * sub, (r + 1) * sub)
            if first:
                u = _layer_norm(x_ref[0, rows, :]) * (1.0 + sc_ref[0]) + sh_ref[0]
                ub = u.astype(BF16)
                u_ref[rows, :] = ub
                gt_ref[0, :, rows] = lax.dot_general(wg_ref[...], ub, NT_DIMS,
                                                     preferred_element_type=F32) + bg_ref[...]
            else:
                ub = u_ref[rows, :]
            acc = jnp.dot(ub, w_ref[...], preferred_element_type=F32)
            if kind == "rot":
                rotary(acc, rows, 1.0)
            elif kind == "rot_scale":
                rotary(acc, rows, KEY_SCALE)
            elif kind == "scale":
                o_ref[0, rows, :] = (acc * KEY_SCALE).astype(BF16)
            else:
                o_ref[0, rows, :] = acc.astype(BF16)

    variants = {}
    for s, key in enumerate(zip(kinds, srcs)):
        variants.setdefault(key + (s == 0,), []).append(s)
    for (kind, src, first), secs in variants.items():
        cond = functools.reduce(jnp.logical_or, [j == s for s in secs])

        @pl.when(cond)
        def _(kind=kind, first=first, src=src):
            section(kind, first, src)


def _proj(x, sh, sc, w_head, w_tail, sections, w_gate_t, b_gate, kinds, tables=None, side_cast=()):
    B, T, D = x.shape
    n_sec = len(kinds)
    tm = min(PROJ_TM, T)
    tn = BRANCH_W
    assert T % tm == 0
    n_steps = (T // tm) * B * n_sec
    slabs = [a.reshape(n_steps, a.shape[0] // n_steps, a.shape[1]) for a in side_cast]
    assert all(s.shape[1] % 16 == 0 and s.size == a.size for s, a in zip(slabs, side_cast))

    def slab(s):
        return pl.BlockSpec((1,) + s.shape[1:],
                            lambda i, b, j, hb, tb: ((i * B + b) * n_sec + j, 0, 0))

    srcs = tuple(src for src, _ in sections)
    hb, tb, h_last, t_last = [], [], 0, 0
    for src, blk in sections:
        h_last, t_last = (blk, t_last) if src == "h" else (h_last, blk)
        hb.append(h_last)
        tb.append(t_last)
    in_specs = [
        pl.BlockSpec((1, tm, D), lambda i, b, j, hb, tb: (b, i, 0)),
        pl.BlockSpec((1, 1, D), lambda i, b, j, hb, tb: (b, 0, 0)),
        pl.BlockSpec((1, 1, D), lambda i, b, j, hb, tb: (b, 0, 0)),
        pl.BlockSpec((D, tn), lambda i, b, j, hb, tb: (0, hb[j])),
        pl.BlockSpec((D, tn), lambda i, b, j, hb, tb: (0, tb[j])),
        pl.BlockSpec((N_GATES, D), lambda i, b, j, hb, tb: (0, 0)),
        pl.BlockSpec((N_GATES, 1), lambda i, b, j, hb, tb: (0, 0)),
    ]
    args = [x, sh, sc, w_head, w_tail, w_gate_t, b_gate] + slabs
    in_specs += [slab(s) for s in slabs]
    if tables is not None:
        in_specs += [pl.BlockSpec((tm, HEAD_W), lambda i, b, j, hb, tb: (i, 0))] * 2
        args += list(tables)
    grid_spec = pltpu.PrefetchScalarGridSpec(
        num_scalar_prefetch=2,
        grid=(T // tm, B, n_sec),
        in_specs=in_specs,
        out_specs=(pl.BlockSpec((1, tm, tn), lambda i, b, j, hb, tb: (b, i, j)),
                   pl.BlockSpec((1, N_GATES, tm), lambda i, b, j, hb, tb: (b, 0, i)))
        + tuple(slab(s) for s in slabs),
        scratch_shapes=[pltpu.VMEM((tm, D), BF16)],
    )
    outs = pl.pallas_call(
        functools.partial(_proj_kernel, kinds, srcs, len(slabs)),
        out_shape=(jax.ShapeDtypeStruct((B, T, n_sec * tn), BF16),
                   jax.ShapeDtypeStruct((B, N_GATES, T), F32))
        + tuple(jax.ShapeDtypeStruct(s.shape, BF16) for s in slabs),
        grid_spec=grid_spec,
        compiler_params=_cparams("parallel", "parallel", "arbitrary"),
        name="proj_lat" if tables is not None else "proj_ctx",
    )(jnp.asarray(hb, I32), jnp.asarray(tb, I32), *args)
    return outs[0], outs[1], [o.reshape(a.shape) for o, a in zip(outs[2:], side_cast)]


def _ret_build(dl_ref, q_ref, k_ref, v_ref, rg_ref, ck_ref, cv_ref, o_ref,
               sf_ref, sb_ref, fs_ref, bs_ref, dec_ref, d_ref):
    h = pl.program_id(0)
    L = SCAN_L
    n_chunks = q_ref.shape[1] // L
    n_ctx_chunks = ck_ref.shape[1] // L
    lgf = _log_sigmoid(jnp.full((1, 1), dl_ref[0, h], F32))
    lgb = _log_sigmoid(jnp.full((1, 1), dl_ref[1, h], F32))

    @pl.when(pl.program_id(1) == 0)
    def _():
        ri = lax.broadcasted_iota(I32, (L, L), 0)
        ci = lax.broadcasted_iota(I32, (L, L), 1)
        rel = (ri - ci).astype(F32)
        d_ref[...] = jnp.where(rel >= 0.0, jnp.exp(jnp.maximum(rel, 0.0) * lgf),
                               jnp.exp(jnp.maximum(-rel, 0.0) * lgb))
        row = lax.broadcasted_iota(I32, (L, HEAD_W), 0).astype(F32)
        dec_ref[0] = jnp.exp((row + 1.0) * lgf)
        dec_ref[1] = jnp.exp((L - 1.0 - row) * lgf)
        dec_ref[2] = jnp.exp((L - row) * lgb)
        dec_ref[3] = jnp.exp(row * lgb)

    cdf = jnp.exp(L * lgf)
    cdb = jnp.exp(L * lgb)

    def update(s_ref, kc, vc, kd, cd):
        kdec = (kc.astype(F32) * kd).astype(BF16)
        s_ref[...] = s_ref[...] * cd + lax.dot_general(kdec, vc, TN_DIMS, preferred_element_type=F32)

    sf_ref[...] = jnp.zeros_like(sf_ref)
    sb_ref[...] = jnp.zeros_like(sb_ref)
    for c in range(n_ctx_chunks):
        update(sf_ref, ck_ref[0, c * L:(c + 1) * L, :], cv_ref[0, c * L:(c + 1) * L, :], dec_ref[1], cdf)
    for c in reversed(range(n_ctx_chunks)):
        update(sb_ref, ck_ref[0, c * L:(c + 1) * L, :], cv_ref[0, c * L:(c + 1) * L, :], dec_ref[3], cdb)

    def state_pass(i, carry):
        cb = n_chunks - 1 - i
        rf = pl.multiple_of(i * L, L)
        rb = pl.multiple_of(cb * L, L)
        fs_ref[i] = sf_ref[...].astype(BF16)
        bs_ref[cb] = sb_ref[...].astype(BF16)
        update(sf_ref, k_ref[0, pl.ds(rf, L), :], v_ref[0, pl.ds(rf, L), :], dec_ref[1], cdf)
        update(sb_ref, k_ref[0, pl.ds(rb, L), :], v_ref[0, pl.ds(rb, L), :], dec_ref[3], cdb)
        return carry

    def finish_states():
        fs_ref[n_chunks - 1] = sf_ref[...].astype(BF16)
        bs_ref[0] = sb_ref[...].astype(BF16)

    def out_chunk(c):
        r0 = pl.multiple_of(c * L, L)
        q = q_ref[0, pl.ds(r0, L), :]
        k = k_ref[0, pl.ds(r0, L), :]
        v = v_ref[0, pl.ds(r0, L), :]
        s = lax.dot_general(q, k, NT_DIMS, preferred_element_type=F32)
        att = (s * d_ref[...]).astype(BF16)
        o = jnp.dot(att, v, preferred_element_type=F32)
        o = o + jnp.dot(q, fs_ref[c], preferred_element_type=F32) * dec_ref[0]
        o = o + jnp.dot(q, bs_ref[c], preferred_element_type=F32) * dec_ref[2]
        rg = rg_ref[0, pl.ds(r0, L), :].astype(F32)
        o_ref[0, pl.ds(r0, L), :] = (_layer_norm(o) * _silu(rg)).astype(BF16)

    return state_pass, finish_states, out_chunk


def _ret_scratch(n_chunks):
    return [pltpu.VMEM((HEAD_W, HEAD_W), F32),
            pltpu.VMEM((HEAD_W, HEAD_W), F32),
            pltpu.VMEM((n_chunks, HEAD_W, HEAD_W), BF16),
            pltpu.VMEM((n_chunks, HEAD_W, HEAD_W), BF16),
            pltpu.VMEM((4, SCAN_L, HEAD_W), F32),
            pltpu.VMEM((SCAN_L, SCAN_L), F32)]


def _mlstm_build(qp_ref, kp_ref, v_ref, mo_ref, ckp_ref, cv_ref, gt_ref, cgt_ref,
                 wq_ref, bq_ref, wk_ref, bk_ref, o_ref,
                 tab_ref, row_ref,
                 cf_ref, mf_ref, cb_ref, mb_ref, cfs_ref, mfs_ref, cbs_ref, mbs_ref, mask_ref,
                 xk_ref, xq_ref, q_ref, k_ref, ck_ref):
    L = SCAN_L
    T = qp_ref.shape[1]
    Tc = ckp_ref.shape[1]
    n_chunks = T // L
    n_ctx_chunks = Tc // L
    CV = CONV_ROWS

    def conv_stage(src_ref, xs_ref, t_len):
        xs_ref[pl.ds(0, 8), :] = jnp.zeros((8, HEAD_W), F32)
        xs_ref[pl.ds(8 + t_len, 8), :] = jnp.zeros((8, HEAD_W), F32)
        xs_ref[pl.ds(8, t_len), :] = src_ref[0].astype(F32)

    def conv_rows(xs_ref, c, w_ref, b_ref, dst_ref, scale):
        w = w_ref[...]
        r0 = pl.multiple_of(c * CV, CV)
        win = xs_ref[pl.ds(r0, CV + 16), :]
        prev = pltpu.roll(win, 1, 0)[8:8 + CV, :]
        cur = win[8:8 + CV, :]
        nxt = pltpu.roll(win, CV + 15, 0)[8:8 + CV, :]
        y = _silu(prev * w[0:1, :] + cur * w[1:2, :] + nxt * w[2:3, :] + b_ref[...])
        if scale != 1.0:
            y = y * scale
        dst_ref[pl.ds(r0, CV), :] = y.astype(BF16)

    per_chunk = L // CV

    def conv_k(c):
        for u in range(per_chunk):
            conv_rows(xk_ref, c * per_chunk + u, wk_ref, bk_ref, k_ref, KEY_SCALE)

    def conv_q(c):
        for u in range(per_chunk):
            conv_rows(xq_ref, c * per_chunk + u, wq_ref, bq_ref, q_ref, 1.0)

    conv_stage(ckp_ref, xk_ref, Tc)
    for c in range(Tc // CV):
        conv_rows(xk_ref, c, wk_ref, bk_ref, ck_ref, KEY_SCALE)
    conv_stage(kp_ref, xk_ref, T)
    conv_stage(qp_ref, xq_ref, T)
    conv_k(0)
    conv_k(n_chunks - 1)
    n_k_steps = n_chunks // 2 - 1

    def conv_step(i, keys):
        if keys:
            conv_k(i + 1)
            conv_k(n_chunks - 2 - i)
        else:
            conv_q(2 * (i - n_k_steps))
            conv_q(2 * (i - n_k_steps) + 1)

    ri = lax.broadcasted_iota(I32, (L, L), 0)
    ci = lax.broadcasted_iota(I32, (L, L), 1)
    tri_u = (ri <= ci).astype(BF16)
    lane8 = lax.broadcasted_iota(I32, (8, L), 1)
    sub8 = lax.broadcasted_iota(I32, (8, L), 0)
    sel_r = lax.broadcasted_iota(I32, (24, 8 * 128), 0) % 8
    sel_c = lax.broadcasted_iota(I32, (24, 8 * 128), 1) // 128
    sel3 = (sel_r == sel_c).astype(BF16)
    ones_cols = jnp.ones((L, AUG_W - HEAD_W), BF16)

    def chunk_tables(g8, n_used, state_only):
        i_f, i_b = g8[0], g8[2]
        lf_f, lf_b = _log_sigmoid(g8[1]), _log_sigmoid(g8[3])
        cs3 = jnp.dot(_split3(jnp.concatenate([lf_f, lf_b], axis=0)), tri_u,
                      preferred_element_type=F32)
        cs = cs3[0:16] + cs3[16:32] + cs3[32:48]
        b_f = cs[0:8]
        b_b = cs[8:16, L - 1:L] - cs[8:16] + lf_b
        z_f = i_f - b_f
        z_b = i_b - b_b
        g_f = b_f[:, L - 1:L] - b_f + i_f
        g_b = b_b[:, 0:1] - b_b + i_b
        mf, mb = z_f, z_b
        s = 1
        while s < L:
            mf = jnp.maximum(mf, jnp.where(lane8 >= s, pltpu.roll(mf, s, 1), NEG_INF))
            mb = jnp.maximum(mb, jnp.where(lane8 < L - s, pltpu.roll(mb, L - s, 1), NEG_INF))
            s *= 2
        mb = jnp.where(lane8 < L - 1, pltpu.roll(mb, L - 1, 1), NEG_INF)
        reps = [None if state_only and t not in (2, 5) else
                lax.dot_general(_split3(val), sel3[:, 0:n_used * 128], TN_DIMS, preferred_element_type=F32)
                for t, val in enumerate((mf, b_f, g_f, mb, b_b, g_b))]

        def rows_of(c):
            out = jnp.zeros((8, L), F32)
            for r, val in enumerate((z_f, z_b, g_f, g_b, b_f, b_b)):
                out = jnp.where(sub8 == r, val[c:c + 1], out)
            return out

        return rows_of, reps

    lat_rows, lat_reps = chunk_tables(gt_ref[0, 0], n_chunks, False)
    for c in range(n_chunks):
        row_ref[c] = lat_rows(c)
        for t in range(N_TAB):
            tab_ref[t, c * L:(c + 1) * L, :] = lat_reps[t][:, c * 128:(c + 1) * 128]

    def lanes2(x):
        return jnp.concatenate([x, x], axis=1)

    def advance(k, v, g_rep, g_row, b_last, c_ref, m_ref):
        m = m_ref[...]
        m_new = jnp.maximum(b_last + m, jnp.max(g_row, axis=-1, keepdims=True))
        kw = (k.astype(F32) * jnp.exp(lanes2(g_rep) - m_new)).astype(BF16)
        v_aug = jnp.concatenate([v, ones_cols], axis=1)
        c_ref[...] = jnp.exp(b_last + m - m_new) * c_ref[...] + lax.dot_general(
            kw, v_aug, TN_DIMS, preferred_element_type=F32)
        m_ref[...] = m_new

    for r in (cf_ref, mf_ref, cb_ref, mb_ref):
        r[...] = jnp.zeros_like(r)
    ctx_rows, ctx_reps = chunk_tables(cgt_ref[0, 0], n_ctx_chunks, True)
    for c in range(n_ctx_chunks):
        rows = ctx_rows(c)
        advance(ck_ref[c * L:(c + 1) * L, :], cv_ref[0, c * L:(c + 1) * L, :],
                ctx_reps[2][:, c * 128:(c + 1) * 128], rows[2:3], rows[4:5, L - 1:L], cf_ref, mf_ref)
    for c in reversed(range(n_ctx_chunks)):
        rows = ctx_rows(c)
        advance(ck_ref[c * L:(c + 1) * L, :], cv_ref[0, c * L:(c + 1) * L, :],
                ctx_reps[5][:, c * 128:(c + 1) * 128], rows[3:4], rows[5:6, 0:1], cb_ref, mb_ref)

    def state_pass(i, carry, keys):
        cb = n_chunks - 1 - i
        rf = pl.multiple_of(i * L, L)
        rb = pl.multiple_of(cb * L, L)
        cfs_ref[i] = cf_ref[...].astype(BF16)
        mfs_ref[i] = mf_ref[...]
        cbs_ref[cb] = cb_ref[...].astype(BF16)
        mbs_ref[cb] = mb_ref[...]
        rows_f = row_ref[i]
        rows_b = row_ref[cb]
        advance(k_ref[pl.ds(rf, L), :], v_ref[0, pl.ds(rf, L), :], tab_ref[2, pl.ds(rf, L), :],
                rows_f[2:3], rows_f[4:5, L - 1:L], cf_ref, mf_ref)
        advance(k_ref[pl.ds(rb, L), :], v_ref[0, pl.ds(rb, L), :], tab_ref[5, pl.ds(rb, L), :],
                rows_b[3:4], rows_b[5:6, 0:1], cb_ref, mb_ref)
        conv_step(i, keys)
        return carry

    def finish_states():
        cfs_ref[n_chunks - 1] = cf_ref[...].astype(BF16)
        mfs_ref[n_chunks - 1] = mf_ref[...]
        cbs_ref[0] = cb_ref[...].astype(BF16)
        mbs_ref[0] = mb_ref[...]

    def direction(q, v_aug, s, z_row, zmax_rep, b_rep, mask, c_in, m_in):
        mx = jnp.maximum(zmax_rep, m_in)
        att = s * jnp.exp((z_row - lanes2(mx)) + mask)
        na = jnp.dot(att.astype(BF16), v_aug, preferred_element_type=F32)
        qa = jnp.dot(q, c_in, preferred_element_type=F32)
        a = jnp.exp(m_in - mx)
        num = na[:, 0:HEAD_W] + lanes2(a) * qa[:, 0:HEAD_W]
        den = na[:, HEAD_W:] + a * qa[:, HEAD_W:]
        scale = 1.0 / jnp.maximum(jnp.abs(den), jnp.exp(-(b_rep + mx)))
        return num * lanes2(scale)

    @pl.when(pl.program_id(1) == 0)
    def _():
        mask_ref[0] = jnp.where(ci <= ri, 0.0, NEG_INF)
        mask_ref[1] = jnp.where(ci > ri, 0.0, NEG_INF)

    def out_chunk(c):
        r0 = pl.multiple_of(c * L, L)
        q = q_ref[pl.ds(r0, L), :]
        k = k_ref[pl.ds(r0, L), :]
        v_aug = jnp.concatenate([v_ref[0, pl.ds(r0, L), :], ones_cols], axis=1)
        s = lax.dot_general(q, k, NT_DIMS, preferred_element_type=F32)
        rows = row_ref[c]
        tot = direction(q, v_aug, s, rows[0:1], tab_ref[0, pl.ds(r0, L), :], tab_ref[1, pl.ds(r0, L), :],
                        mask_ref[0], cfs_ref[c], mfs_ref[c])
        tot = tot + direction(q, v_aug, s, rows[1:2], tab_ref[3, pl.ds(r0, L), :],
                              tab_ref[4, pl.ds(r0, L), :], mask_ref[1], cbs_ref[c], mbs_ref[c])
        mo = mo_ref[0, pl.ds(r0, L), :].astype(F32)
        o_ref[0, pl.ds(r0, L), :] = (_layer_norm(tot) * jax.nn.sigmoid(mo)).astype(BF16)

    return state_pass, finish_states, out_chunk, n_k_steps


def _mlstm_scratch(T, Tc, n_chunks):
    state = [pltpu.VMEM((HEAD_W, AUG_W), F32), pltpu.VMEM((1, 1), F32)]
    snaps = [pltpu.VMEM((n_chunks, HEAD_W, AUG_W), BF16), pltpu.VMEM((n_chunks, 1, 1), F32)]
    return [pltpu.VMEM((N_TAB, T, 128), F32), pltpu.VMEM((n_chunks, 8, SCAN_L), F32)] \
        + state + state + snaps + snaps + [pltpu.VMEM((2, SCAN_L, SCAN_L), F32)] \
        + [pltpu.VMEM((T + 16, HEAD_W), F32), pltpu.VMEM((T + 16, HEAD_W), F32),
           pltpu.VMEM((T, HEAD_W), BF16), pltpu.VMEM((T, HEAD_W), BF16), pltpu.VMEM((Tc, HEAD_W), BF16)]


def _scan_kernel(n_ret_scratch, dl_ref, rq_ref, rk_ref, rv_ref, rg_ref, rck_ref, rcv_ref,
                 mq_ref, mk_ref, mv_ref, mo_ref, mck_ref, mcv_ref, gt_ref, cgt_ref,
                 wq_ref, bq_ref, wk_ref, bk_ref, r_ref, m_ref, *scratch):
    n_chunks = rq_ref.shape[1] // SCAN_L
    ret = _ret_build(dl_ref, rq_ref, rk_ref, rv_ref, rg_ref, rck_ref, rcv_ref, r_ref,
                     *scratch[:n_ret_scratch])
    mls = _mlstm_build(mq_ref, mk_ref, mv_ref, mo_ref, mck_ref, mcv_ref, gt_ref, cgt_ref,
                       wq_ref, bq_ref, wk_ref, bk_ref, m_ref, *scratch[n_ret_scratch:])

    def state_pass(i, carry, keys):
        ret[0](i, carry)
        mls[0](i, carry, keys)
        return carry

    n_k_steps = mls[3]
    lax.fori_loop(0, n_k_steps, functools.partial(state_pass, keys=True), 0)
    lax.fori_loop(n_k_steps, n_chunks - 1, functools.partial(state_pass, keys=False), 0)
    ret[1]()
    mls[1]()

    def out_pass(i, carry):
        for c in (2 * i, 2 * i + 1):
            ret[2](c)
        for c in (2 * i, 2 * i + 1):
            mls[2](c)
        return carry

    lax.fori_loop(0, n_chunks // 2, out_pass, 0)


def _scans(decay_logit, p_lat, p_ctx, gt, cgt, conv_w, conv_b, ret_lat, ret_ctx, ml_lat, ml_ctx):
    B, T, _ = p_lat.shape
    Tc = p_ctx.shape[1]
    assert T % (2 * SCAN_L) == 0 and Tc % SCAN_L == 0 and T // SCAN_L <= 8
    n_chunks = T // SCAN_L

    def lat(sec):
        return pl.BlockSpec((1, T, HEAD_W), lambda h, b: (b, 0, sec * HEADS + h))

    def cx(sec):
        return pl.BlockSpec((1, Tc, HEAD_W), lambda h, b: (b, 0, sec * HEADS + h))

    gates = pl.BlockSpec((1, 1, N_GK, 8, SCAN_L), lambda h, b: (b, h, 0, 0, 0))
    out = pl.BlockSpec((1, T, HEAD_W), lambda h, b: (b, 0, h))
    conv_specs = [pl.BlockSpec((3, HEAD_W), lambda h, b: (0, h)),
                  pl.BlockSpec((1, HEAD_W), lambda h, b: (0, h)),
                  pl.BlockSpec((3, HEAD_W), lambda h, b: (0, HEADS + h)),
                  pl.BlockSpec((1, HEAD_W), lambda h, b: (0, HEADS + h))]
    ret_scratch = _ret_scratch(n_chunks)
    return pl.pallas_call(
        functools.partial(_scan_kernel, len(ret_scratch)),
        out_shape=(jax.ShapeDtypeStruct((B, T, BRANCH_W), BF16),
                   jax.ShapeDtypeStruct((B, T, BRANCH_W), BF16)),
        grid=(HEADS, B),
        in_specs=[pl.BlockSpec(memory_space=pltpu.SMEM)]
        + [lat(s) for s in ret_lat] + [cx(s) for s in ret_ctx]
        + [lat(s) for s in ml_lat] + [cx(s) for s in ml_ctx] + [gates, gates] + conv_specs,
        out_specs=(out, out),
        scratch_shapes=ret_scratch + _mlstm_scratch(T, Tc, n_chunks),
        compiler_params=_cparams("arbitrary", "arbitrary"),
        name="scans",
    )(decay_logit, *([p_lat] * 4), *([p_ctx] * 2), *([p_lat] * 4), *([p_ctx] * 2), gt, cgt,
      conv_w, conv_b, conv_w, conv_b)


def _merge_kernel(alpha, r_ref, m_ref, gr_ref, gm_ref, x_ref, g1_ref, sh2_ref, sc2_ref,
                  lng_ref, lnb_ref, wr_ref, wm_ref, wo_ref, wrt_ref, brt_ref, e1_ref, e3_ref, e2_ref,
                  x1_ref, ua_ref, ub_ref, ri_ref, rw_ref, cnt_ref, e1b_ref, e3b_ref, e2b_ref,
                  carry_ref, u_ref):
    for src, dst in ((e1_ref, e1b_ref), (e3_ref, e3b_ref), (e2_ref, e2b_ref)):
        dst[...] = src[...].astype(BF16)

    @pl.when(jnp.logical_and(pl.program_id(0) == 0, pl.program_id(1) == 0))
    def _():
        carry_ref[...] = jnp.zeros_like(carry_ref)
        tm = x_ref.shape[1]
        r = lax.broadcasted_iota(I32, (tm, tm), 0)
        c = lax.broadcasted_iota(I32, (tm, tm), 1)
        u_ref[...] = (r < c).astype(BF16)

    yr = jnp.dot(r_ref[0], wr_ref[...], preferred_element_type=F32)
    ym = jnp.dot(m_ref[0], wm_ref[...], preferred_element_type=F32)
    y = jax.nn.sigmoid(gr_ref[0].astype(F32)) * yr + jax.nn.sigmoid(gm_ref[0].astype(F32)) * ym
    yo = jnp.dot(y.astype(BF16), wo_ref[...], preferred_element_type=F32)
    x1 = _layer_norm(alpha * x_ref[0] + g1_ref[0] * yo) * lng_ref[...] + lnb_ref[...]
    x1_ref[0] = x1
    u2 = _layer_norm(x1) * (1.0 + sc2_ref[0]) + sh2_ref[0]
    ua_ref[0] = _pack_pairs(u2[:, 0:PACK_W], u2[:, PACK_W:2 * PACK_W])
    ub_ref[0] = _pack_pairs(u2[:, 2 * PACK_W:3 * PACK_W], u2[:, 3 * PACK_W:4 * PACK_W])
    lt = lax.dot_general(wrt_ref[...], u2.astype(BF16), NT_DIMS, preferred_element_type=F32) + brt_ref[...]
    _route_tile(lt, ri_ref, rw_ref, cnt_ref, carry_ref, u_ref)


def _route_tile(lt, ri_ref, rw_ref, cnt_ref, carry_ref, u_ref):
    tm = lt.shape[1]
    lg = lt[0:N_GROUPS, :]
    eg = jnp.exp(lg - jnp.max(lg, axis=0, keepdims=True))
    pg = eg / jnp.sum(eg, axis=0, keepdims=True)
    pg_top = jnp.max(pg, axis=0, keepdims=True)
    rows_g = lax.broadcasted_iota(I32, pg.shape, 0)
    g_idx = jnp.min(jnp.where(pg == pg_top, rows_g, N_GROUPS), axis=0, keepdims=True)

    le = jnp.zeros((EXP_PER_GROUP, tm), F32)
    for g in range(N_GROUPS):
        lo = 8 + g * EXP_PER_GROUP
        le = jnp.where(g_idx == g, lt[lo:lo + EXP_PER_GROUP, :], le)
    ee = jnp.exp(le - jnp.max(le, axis=0, keepdims=True))
    pe = ee / jnp.sum(ee, axis=0, keepdims=True)
    rows_e = lax.broadcasted_iota(I32, pe.shape, 0)
    v1 = jnp.max(pe, axis=0, keepdims=True)
    i1 = jnp.min(jnp.where(pe == v1, rows_e, EXP_PER_GROUP), axis=0, keepdims=True)
    pe2 = jnp.where(rows_e == i1, -1.0, pe)
    v2 = jnp.max(pe2, axis=0, keepdims=True)
    i2 = jnp.min(jnp.where(pe2 == v2, rows_e, EXP_PER_GROUP), axis=0, keepdims=True)
    den = v1 + v2
    rw_ref[...] = jnp.zeros_like(rw_ref)
    rw_ref[0:1, :] = pg_top * v1 / den
    rw_ref[1:2, :] = pg_top * v2 / den
    e1 = g_idx * EXP_PER_GROUP + i1
    e2 = g_idx * EXP_PER_GROUP + i2

    rows_x = lax.broadcasted_iota(I32, (N_EXPERTS, tm), 0)
    oh1 = (rows_x == e1).astype(F32)
    oh2 = (rows_x == e2).astype(F32)
    both = oh1 + oh2
    before = carry_ref[:, 0:1] + jnp.dot(both.astype(BF16), u_ref[...], preferred_element_type=F32)
    ri_ref[0:1, :] = e1
    ri_ref[1:2, :] = e2
    ri_ref[2:3, :] = jnp.sum(oh1 * before, axis=0, keepdims=True).astype(I32)
    ri_ref[3:4, :] = jnp.sum(oh2 * before, axis=0, keepdims=True).astype(I32)
    carry_ref[...] = carry_ref[...] + jnp.sum(both, axis=1, keepdims=True)
    cnt_ref[...] = carry_ref[...].astype(I32)


def _merge(alpha, r, m, p_lat, sec_gates, x, g1, sh2, sc2, lng, lnb, wr, wm, wo, wrt, brt, expert_w):
    B, T, D = x.shape
    tm = MERGE_TM
    per_b = T // tm
    n = B * T
    n_steps = B * per_b
    sliced = [w.reshape(n_steps, w.shape[0] * w.shape[1] // n_steps, w.shape[2]) for w in expert_w]
    assert all(s.shape[1] % 16 == 0 and s.size == w.size for s, w in zip(sliced, expert_w))

    def slab(s):
        return pl.BlockSpec((1,) + s.shape[1:], lambda b, i: (b * per_b + i, 0, 0))

    def tile(w):
        return pl.BlockSpec((1, tm, w), lambda b, i: (b, i, 0))

    def sec(s):
        return pl.BlockSpec((1, tm, BRANCH_W), lambda b, i: (b, i, s))

    def mod():
        return pl.BlockSpec((1, 1, D), lambda b, i: (b, 0, 0))

    def const(shape):
        return pl.BlockSpec(shape, lambda b, i: (0,) * len(shape))

    outs = pl.pallas_call(
        functools.partial(_merge_kernel, alpha),
        out_shape=(jax.ShapeDtypeStruct((B, T, D), F32),
                   jax.ShapeDtypeStruct((B, T, PACK_W), U32),
                   jax.ShapeDtypeStruct((B, T, PACK_W), U32),
                   jax.ShapeDtypeStruct((4, n), I32),
                   jax.ShapeDtypeStruct((8, n), F32),
                   jax.ShapeDtypeStruct((N_EXPERTS, 128), I32))
        + tuple(jax.ShapeDtypeStruct(s.shape, BF16) for s in sliced),
        grid=(B, per_b),
        in_specs=[tile(BRANCH_W), tile(BRANCH_W), sec(sec_gates[0]), sec(sec_gates[1]), tile(D),
                  mod(), mod(), mod(), const((1, D)), const((1, D)),
                  const((BRANCH_W, D)), const((BRANCH_W, D)), const((D, D)),
                  const((ROUTE_ROWS, D)), const((ROUTE_ROWS, 1))] + [slab(s) for s in sliced],
        out_specs=(tile(D), tile(PACK_W), tile(PACK_W),
                   pl.BlockSpec((4, tm), lambda b, i: (0, b * per_b + i)),
                   pl.BlockSpec((8, tm), lambda b, i: (0, b * per_b + i)),
                   pl.BlockSpec((N_EXPERTS, 128), lambda b, i: (0, 0)))
        + tuple(slab(s) for s in sliced),
        scratch_shapes=[pltpu.VMEM((N_EXPERTS, 128), F32), pltpu.VMEM((tm, tm), BF16)],
        compiler_params=_cparams("arbitrary", "arbitrary"),
        name="merge",
    )(r, m, p_lat, p_lat, x, g1, sh2, sc2, lng, lnb, wr, wm, wo, wrt, brt, *sliced)
    return outs[:6] + tuple(o.reshape(w.shape) for o, w in zip(outs[6:], expert_w))


def _sc_mesh():
    return plsc.VectorSubcoreMesh(core_axis_name="c", subcore_axis_name="s")


def _sc_scatter2(rows_a, rows_b, idx0, idx1, n_out):
    m, w = rows_a.shape
    out = jax.ShapeDtypeStruct((n_out, w), rows_a.dtype)

    @functools.partial(pl.kernel, out_type=(out, out), mesh=_sc_mesh(), scratch_types=[])
    def k(xa_hbm, xb_hbm, i0_hbm, i1_hbm, oa_hbm, ob_hbm):
        for x_hbm, o_hbm in ((xa_hbm, oa_hbm), (xb_hbm, ob_hbm)):
            def body(x_vmem, i0_vmem, i1_vmem, o_hbm=o_hbm):
                pltpu.sync_copy(x_vmem, o_hbm.at[i0_vmem.at[0]])
                pltpu.sync_copy(x_vmem, o_hbm.at[i1_vmem.at[0]])

            pltpu.emit_pipeline(
                body,
                grid=(m // SC_WIN,),
                in_specs=[pl.BlockSpec((SC_WIN, w), lambda i: (i, 0)),
                          pl.BlockSpec((1, SC_WIN), lambda i: (0, i)),
                          pl.BlockSpec((1, SC_WIN), lambda i: (0, i))],
                out_specs=[],
                core_axis_name=("c", "s"),
                dimension_semantics=(pltpu.PARALLEL,),
            )(x_hbm, i0_hbm, i1_hbm)

    return k(rows_a, rows_b, idx0.reshape(1, m), idx1.reshape(1, m))


def _sc_gather(table_a, table_b, idx):
    m = idx.shape[0]
    w = table_a.shape[1]
    out = jax.ShapeDtypeStruct((m, w), table_a.dtype)

    @functools.partial(pl.kernel, out_type=(out, out), mesh=_sc_mesh(), scratch_types=[])
    def k(ta_hbm, tb_hbm, i_hbm, oa_hbm, ob_hbm):
        for t_hbm, o_hbm in ((ta_hbm, oa_hbm), (tb_hbm, ob_hbm)):
            def body(i_vmem, o_vmem, t_hbm=t_hbm):
                pltpu.sync_copy(t_hbm.at[i_vmem.at[0]], o_vmem)

            pltpu.emit_pipeline(
                body,
                grid=(m // SC_WIN,),
                in_specs=[pl.BlockSpec((1, SC_WIN), lambda i: (0, i))],
                out_specs=[pl.BlockSpec((SC_WIN, w), lambda i: (i, 0))],
                core_axis_name=("c", "s"),
                dimension_semantics=(pltpu.PARALLEL,),
            )(i_hbm, o_hbm)

    return k(table_a, table_b, idx.reshape(1, m))


def _expert_kernel(be_ref, nv_ref, xa_ref, xb_ref, w1_ref, w3_ref, w2_ref, ya_ref, yb_ref):
    j = pl.program_id(0)
    nv = nv_ref[j]

    @pl.when(nv > 0)
    def _():
        valid = lax.broadcasted_iota(I32, xa_ref.shape, 0) < nv
        zero = jnp.zeros(xa_ref.shape, U32)
        parts = _unpack_pairs(jnp.where(valid, xa_ref[...], zero)) + \
            _unpack_pairs(jnp.where(valid, xb_ref[...], zero))
        x = jnp.concatenate([p.astype(BF16) for p in parts], axis=1)
        h1 = jnp.dot(x, w1_ref[0], preferred_element_type=F32)
        h3 = jnp.dot(x, w3_ref[0], preferred_element_type=F32)
        y = jnp.dot((_silu(h1) * h3).astype(BF16), w2_ref[0], preferred_element_type=F32)
        ya_ref[...] = _pack_pairs(y[:, 0:PACK_W], y[:, PACK_W:2 * PACK_W])
        yb_ref[...] = _pack_pairs(y[:, 2 * PACK_W:3 * PACK_W], y[:, 3 * PACK_W:4 * PACK_W])

    @pl.when(nv == 0)
    def _():
        ya_ref[...] = jnp.zeros_like(ya_ref)
        yb_ref[...] = jnp.zeros_like(yb_ref)


def _experts(block_exp, n_valid, xa, xb, w1, w3, w2):
    n_slots = xa.shape[0]
    n_blocks = n_slots // MOE_BLK
    d, de = w1.shape[1], w1.shape[2]
    slot = pl.BlockSpec((MOE_BLK, PACK_W), lambda j, be, nv: (j, 0))
    grid_spec = pltpu.PrefetchScalarGridSpec(
        num_scalar_prefetch=2,
        grid=(n_blocks,),
        in_specs=[slot, slot,
                  pl.BlockSpec((1, d, de), lambda j, be, nv: (be[j], 0, 0)),
                  pl.BlockSpec((1, d, de), lambda j, be, nv: (be[j], 0, 0)),
                  pl.BlockSpec((1, de, d), lambda j, be, nv: (be[j], 0, 0))],
        out_specs=(slot, slot),
    )
    return pl.pallas_call(
        _expert_kernel,
        out_shape=(jax.ShapeDtypeStruct((n_slots, PACK_W), U32),
                   jax.ShapeDtypeStruct((n_slots, PACK_W), U32)),
        grid_spec=grid_spec,
        compiler_params=_cparams("parallel"),
        name="experts",
    )(block_exp, n_valid, xa, xb, w1, w3, w2)


def _final_kernel(alpha, x1_ref, a0_ref, b0_ref, a1_ref, b1_ref, w_ref, g2_ref, lng_ref, lnb_ref, o_ref):
    w = w_ref[...].T
    w0 = w[:, 0:1]
    w1 = w[:, 1:2]
    parts0 = _unpack_pairs(a0_ref[...]) + _unpack_pairs(b0_ref[...])
    parts1 = _unpack_pairs(a1_ref[...]) + _unpack_pairs(b1_ref[...])
    f = jnp.concatenate([w0 * p0 + w1 * p1 for p0, p1 in zip(parts0, parts1)], axis=1)
    o_ref[0] = _layer_norm(alpha * x1_ref[0] + g2_ref[0] * f) * lng_ref[...] + lnb_ref[...]


def _final(alpha, x1, ya, yb, w, g2, lng, lnb):
    B, T, D = x1.shape
    tm = min(FINAL_TM, T)
    per_b = T // tm
    n_tiles = B * per_b

    def rows(k):
        return pl.BlockSpec((tm, PACK_W), lambda b, i: (k * n_tiles + b * per_b + i, 0))

    return pl.pallas_call(
        functools.partial(_final_kernel, alpha),
        out_shape=jax.ShapeDtypeStruct((B, T, D), F32),
        grid=(B, per_b),
        in_specs=[pl.BlockSpec((1, tm, D), lambda b, i: (b, i, 0)),
                  rows(0), rows(0), rows(1), rows(1),
                  pl.BlockSpec((8, tm), lambda b, i: (0, b * per_b + i)),
                  pl.BlockSpec((1, 1, D), lambda b, i: (b, 0, 0)),
                  pl.BlockSpec((1, D), lambda b, i: (0, 0)),
                  pl.BlockSpec((1, D), lambda b, i: (0, 0))],
        out_specs=pl.BlockSpec((1, tm, D), lambda b, i: (b, i, 0)),
        compiler_params=_cparams("parallel", "parallel"),
        name="final",
    )(x1, ya, yb, ya, yb, w, g2, lng, lnb)


def _rotary_tables(T):
    quarter = HEAD_W // 4
    freqs = ROPE_BASE ** (-jnp.arange(quarter, dtype=F32) / quarter)
    t = jnp.arange(T)
    ang_r = (t // GRID_W).astype(F32)[:, None] * freqs[None, :]
    ang_c = (t % GRID_W).astype(F32)[:, None] * freqs[None, :]
    cos = jnp.concatenate([jnp.cos(ang_r)] * 2 + [jnp.cos(ang_c)] * 2, axis=1)
    sin = jnp.concatenate([-jnp.sin(ang_r), jnp.sin(ang_r), -jnp.sin(ang_c), jnp.sin(ang_c)], axis=1)
    return cos, sin


def _per_head_gates(gt):
    B, _, T = gt.shape
    n_chunks = T // SCAN_L
    gth = gt.reshape(B, N_GK, HEADS, n_chunks, SCAN_L).transpose(0, 2, 1, 3, 4)
    return jnp.pad(gth, ((0, 0), (0, 0), (0, 0), (0, 8 - n_chunks), (0, 0)))


def _table_lookup(table, idx):
    sel = idx[..., None] == jnp.arange(table.shape[0], dtype=idx.dtype)
    return jnp.sum(jnp.where(sel, table, 0), axis=-1)


def kernel(x, c, ctx, c_ctx, w_ada, b_ada, w_in, b_mgate, ml_conv_w, ml_conv_b, ret_decay_logit, w_ret_branch, w_ml_branch, w_out, ln1_g, ln1_b, w_rg, b_rg, w_re, b_re, w_e1, w_e3, w_e2, ln2_g, ln2_b):
    B, T, D = x.shape
    depth = w_ada.shape[0]
    assert depth == 1 and D == BRANCH_W and T % GRID_W == 0
    alpha = (2 * depth) ** 0.25
    n_tok = B * T

    n_rows = -(-(B + 1) // 8) * 8
    cs = jnp.zeros((n_rows, D), F32).at[:B].set(c).at[B].set(c_ctx)
    mod = _ada(cs, w_ada[0], b_ada[0][None, :])
    sh1, sc1, g1, sh2, sc2, g2 = [mod[:B, None, i * D:(i + 1) * D] for i in range(6)]
    csh1 = mod[B, 0 * D:1 * D].reshape(1, 1, D)
    csc1 = mod[B, 1 * D:2 * D].reshape(1, 1, D)

    w = w_in[0]
    g_lo = 8 * BRANCH_W
    w_gate_t = w[:, g_lo:g_lo + N_GATES].T.astype(BF16)
    b_gate = b_mgate[0][:, None]
    w_head = w[:, :g_lo].astype(BF16)
    w_tail = w[:, g_lo + N_GATES:].astype(BF16)
    sec_lat = tuple(("h", s) for s in range(8)) + (("t", 0), ("t", 1))
    sec_ctx = (("h", 1), ("h", 2), ("h", 5), ("h", 6))
    kinds_lat = ("rot", "rot_scale") + ("plain",) * 8
    kinds_ctx = ("scale", "plain", "plain", "plain")
    p_lat, gt_lat, _ = _proj(x, sh1, sc1, w_head, w_tail, sec_lat, w_gate_t, b_gate, kinds_lat,
                             _rotary_tables(T))
    Tc = ctx.shape[1]
    p_ctx, gt_ctx, (w_rb, w_mb, w_ob) = _proj(
        ctx.reshape(1, B * Tc, D), csh1, csc1, w_head, w_tail, sec_ctx, w_gate_t, b_gate, kinds_ctx,
        side_cast=(w_ret_branch[0], w_ml_branch[0], w_out[0]))
    p_ctx = p_ctx.reshape(B, Tc, -1)
    gt_ctx = gt_ctx.reshape(N_GATES, B, Tc).transpose(1, 0, 2)

    ret, mls = _scans(ret_decay_logit[0], p_lat, p_ctx, _per_head_gates(gt_lat), _per_head_gates(gt_ctx),
                      ml_conv_w[0], ml_conv_b[0][None, :], (0, 1, 2, 3), (0, 1), (4, 5, 6, 7), (2, 3))

    wrt = jnp.zeros((ROUTE_ROWS, D), F32).at[:N_GROUPS].set(w_rg[0].T).at[8:8 + N_EXPERTS].set(w_re[0].T)
    brt = jnp.zeros((ROUTE_ROWS, 1), F32).at[:N_GROUPS, 0].set(b_rg[0]).at[8:8 + N_EXPERTS, 0].set(b_re[0])
    x1, ua, ub, ri, rw, cnt, we1, we3, we2 = _merge(
        alpha, ret, mls, p_lat, (8, 9), x, g1, sh2, sc2, ln1_g[0][None, :], ln1_b[0][None, :],
        w_rb, w_mb, w_ob, wrt.astype(BF16), brt, (w_e1[0], w_e3[0], w_e2[0]))

    counts = cnt[:, 0]
    padded = (counts + MOE_BLK - 1) // MOE_BLK * MOE_BLK
    pad_end = jnp.cumsum(padded)
    pad_off = pad_end - padded
    dest = _table_lookup(pad_off, ri[0:2]) + ri[2:4]
    n_blocks = (2 * n_tok) // MOE_BLK + N_EXPERTS
    n_slots = n_blocks * MOE_BLK
    block_start = jnp.arange(n_blocks, dtype=I32) * MOE_BLK
    block_exp = jnp.minimum((block_start[:, None] >= pad_end[None, :]).sum(1), N_EXPERTS - 1).astype(I32)
    n_valid = jnp.clip(_table_lookup(counts, block_exp) - (block_start - _table_lookup(pad_off, block_exp)),
                       0, MOE_BLK).astype(I32)

    xa, xb = _sc_scatter2(ua.reshape(n_tok, PACK_W), ub.reshape(n_tok, PACK_W), dest[0], dest[1], n_slots)
    ya, yb = _experts(block_exp, n_valid, xa, xb, we1, we3, we2)
    ga, gb = _sc_gather(ya, yb, dest.reshape(2 * n_tok))
    return _final(alpha, x1, ga, gb, rw, g2, ln2_g[0][None, :], ln2_b[0][None, :])
```

```python
import functools

import jax
import jax.numpy as jnp
from jax import lax
from jax.experimental import pallas as pl
from jax.experimental.pallas import tpu as pltpu
from jax.experimental.pallas import tpu_sc as plsc

F32 = jnp.float32
BF16 = jnp.bfloat16
U32 = jnp.uint32
I32 = jnp.int32
HIGHEST = lax.Precision.HIGHEST

HEADS = 4
HEAD_W = 256
BRANCH_W = HEADS * HEAD_W
GRID_W = 64
ROPE_BASE = 10000.0
N_GATES = 16
N_GK = N_GATES // HEADS
N_GROUPS = 4
EXP_PER_GROUP = 8
N_EXPERTS = N_GROUPS * EXP_PER_GROUP
LN_EPS = 1e-5
NEG_INF = -1e30
KEY_SCALE = HEAD_W ** -0.5

SCAN_L = 256
CONV_ROWS = 128
PROJ_TM = 2048
PROJ_SUB = 256
MERGE_TM = 512
FINAL_TM = 1024
MOE_BLK = 512
SC_WIN = 128
PACK_W = 256
ROUTE_ROWS = 64
N_TAB = 6
AUG_W = HEAD_W + 128
VMEM_LIMIT = 48 * 1024 * 1024

NT_DIMS = (((1,), (1,)), ((), ()))
TN_DIMS = (((0,), (0,)), ((), ()))


def _cparams(*sem):
    return pltpu.CompilerParams(dimension_semantics=sem, vmem_limit_bytes=VMEM_LIMIT)


def _layer_norm(x):
    mu = jnp.mean(x, axis=-1, keepdims=True)
    xc = x - mu
    var = jnp.mean(xc * xc, axis=-1, keepdims=True)
    return xc * lax.rsqrt(var + LN_EPS)


def _log_sigmoid(x):
    return jnp.minimum(x, 0.0) - jnp.log1p(jnp.exp(-jnp.abs(x)))


def _silu(x):
    return x * jax.nn.sigmoid(x)


def _pack_pairs(hi, lo):
    hb = lax.bitcast_convert_type(hi.astype(BF16).astype(F32), U32)
    lb = lax.bitcast_convert_type(lo.astype(BF16).astype(F32), U32)
    return (hb & jnp.uint32(0xFFFF0000)) | (lb >> 16)


def _unpack_pairs(p):
    hi = lax.bitcast_convert_type(p & jnp.uint32(0xFFFF0000), F32)
    lo = lax.bitcast_convert_type(p << 16, F32)
    return hi, lo


def _split3(x):
    hi = x.astype(BF16).astype(F32)
    r1 = x - hi
    mid = r1.astype(BF16).astype(F32)
    lo = (r1 - mid).astype(BF16).astype(F32)
    return jnp.concatenate([hi, mid, lo], axis=0).astype(BF16)


def _ada_kernel(c_ref, w_ref, b_ref, o_ref):
    s = _silu(c_ref[...])
    o_ref[...] = jnp.dot(s, w_ref[...], precision=HIGHEST, preferred_element_type=F32) + b_ref[...]


def _ada(cs, w, b):
    rows, d = cs.shape
    cols = w.shape[1]
    tn = 1024
    return pl.pallas_call(
        _ada_kernel,
        out_shape=jax.ShapeDtypeStruct((rows, cols), F32),
        grid=(cols // tn,),
        in_specs=[pl.BlockSpec((rows, d), lambda j: (0, 0)),
                  pl.BlockSpec((d, tn), lambda j: (0, j)),
                  pl.BlockSpec((1, tn), lambda j: (0, j))],
        out_specs=pl.BlockSpec((rows, tn), lambda j: (0, j)),
        compiler_params=_cparams("parallel"),
        name="ada",
    )(cs, w, b)


def _proj_kernel(kinds, srcs, n_cast, hb_ref, tb_ref, x_ref, sh_ref, sc_ref, wh_ref, wt_ref, wg_ref,
                 bg_ref, *rest):
    cast_in, rest = rest[:n_cast], rest[n_cast:]
    if "rot" in kinds or "rot_scale" in kinds:
        cos_ref, sin_ref, rest = rest[0], rest[1], rest[2:]
    o_ref, gt_ref = rest[0], rest[1]
    cast_out, u_ref = rest[2:2 + n_cast], rest[2 + n_cast]
    for src, dst in zip(cast_in, cast_out):
        dst[...] = src[...].astype(BF16)
    j = pl.program_id(2)
    tm = x_ref.shape[1]
    sub = min(PROJ_SUB, tm)

    def rotary(acc, rows, scale):
        for s in range(acc.shape[1] // 128):
            a = acc[:, s * 128:(s + 1) * 128]
            half = s % 2
            cs = cos_ref[rows, half * 128:(half + 1) * 128]
            sn = sin_ref[rows, half * 128:(half + 1) * 128]
            r = a * cs + pltpu.roll(a, 64, 1) * sn
            if scale != 1.0:
                r = r * scale
            o_ref[0, rows, s * 128:(s + 1) * 128] = r.astype(BF16)

    def section(kind, first, src):
        w_ref = wh_ref if src == "h" else wt_ref
        for r in range(tm // sub):
            rows = slice(r * sub, (r + 1) * sub)
            if first:
                u = _layer_norm(x_ref[0, rows, :]) * (1.0 + sc_ref[0]) + sh_ref[0]
                ub = u.astype(BF16)
                u_ref[rows, :] = ub
                gt_ref[0, :, rows] = lax.dot_general(wg_ref[...], ub, NT_DIMS,
                                                     preferred_element_type=F32) + bg_ref[...]
            else:
                ub = u_ref[rows, :]
            acc = jnp.dot(ub, w_ref[...], preferred_element_type=F32)
            if kind == "rot":
                rotary(acc, rows, 1.0)
            elif kind == "rot_scale":
                rotary(acc, rows, KEY_SCALE)
            elif kind == "scale":
                o_ref[0, rows, :] = (acc * KEY_SCALE).astype(BF16)
            else:
                o_ref[0, rows, :] = acc.astype(BF16)

    variants = {}
    for s, key in enumerate(zip(kinds, srcs)):
        variants.setdefault(key + (s == 0,), []).append(s)
    for (kind, src, first), secs in variants.items():
        cond = functools.reduce(jnp.logical_or, [j == s for s in secs])

        @pl.when(cond)
        def _(kind=kind, first=first, src=src):
            section(kind, first, src)


def _proj(x, sh, sc, w_head, w_tail, sections, w_gate_t, b_gate, kinds, tables=None, side_cast=()):
    B, T, D = x.shape
    n_sec = len(kinds)
    tm = min(PROJ_TM, T)
    tn = BRANCH_W
    assert T % tm == 0
    n_steps = (T // tm) * B * n_sec
    slabs = [a.reshape(n_steps, a.shape[0] // n_steps, a.shape[1]) for a in side_cast]
    assert all(s.shape[1] % 16 == 0 and s.size == a.size for s, a in zip(slabs, side_cast))

    def slab(s):
        return pl.BlockSpec((1,) + s.shape[1:],
                            lambda i, b, j, hb, tb: ((i * B + b) * n_sec + j, 0, 0))

    srcs = tuple(src for src, _ in sections)
    hb, tb, h_last, t_last = [], [], 0, 0
    for src, blk in sections:
        h_last, t_last = (blk, t_last) if src == "h" else (h_last, blk)
        hb.append(h_last)
        tb.append(t_last)
    in_specs = [
        pl.BlockSpec((1, tm, D), lambda i, b, j, hb, tb: (b, i, 0)),
        pl.BlockSpec((1, 1, D), lambda i, b, j, hb, tb: (b, 0, 0)),
        pl.BlockSpec((1, 1, D), lambda i, b, j, hb, tb: (b, 0, 0)),
        pl.BlockSpec((D, tn), lambda i, b, j, hb, tb: (0, hb[j])),
        pl.BlockSpec((D, tn), lambda i, b, j, hb, tb: (0, tb[j])),
        pl.BlockSpec((N_GATES, D), lambda i, b, j, hb, tb: (0, 0)),
        pl.BlockSpec((N_GATES, 1), lambda i, b, j, hb, tb: (0, 0)),
    ]
    args = [x, sh, sc, w_head, w_tail, w_gate_t, b_gate] + slabs
    in_specs += [slab(s) for s in slabs]
    if tables is not None:
        in_specs += [pl.BlockSpec((tm, HEAD_W), lambda i, b, j, hb, tb: (i, 0))] * 2
        args += list(tables)
    grid_spec = pltpu.PrefetchScalarGridSpec(
        num_scalar_prefetch=2,
        grid=(T // tm, B, n_sec),
        in_specs=in_specs,
        out_specs=(pl.BlockSpec((1, tm, tn), lambda i, b, j, hb, tb: (b, i, j)),
                   pl.BlockSpec((1, N_GATES, tm), lambda i, b, j, hb, tb: (b, 0, i)))
        + tuple(slab(s) for s in slabs),
        scratch_shapes=[pltpu.VMEM((tm, D), BF16)],
    )
    outs = pl.pallas_call(
        functools.partial(_proj_kernel, kinds, srcs, len(slabs)),
        out_shape=(jax.ShapeDtypeStruct((B, T, n_sec * tn), BF16),
                   jax.ShapeDtypeStruct((B, N_GATES, T), F32))
        + tuple(jax.ShapeDtypeStruct(s.shape, BF16) for s in slabs),
        grid_spec=grid_spec,
        compiler_params=pltpu.CompilerParams(
            dimension_semantics=("parallel", "parallel", "arbitrary"), vmem_limit_bytes=VMEM_LIMIT,
            allow_input_fusion=[False] * 5 + [True, True] + [False] * (len(args) - 5)),
        name="proj_lat" if tables is not None else "proj_ctx",
    )(jnp.asarray(hb, I32), jnp.asarray(tb, I32), *args)
    return outs[0], outs[1], [o.reshape(a.shape) for o, a in zip(outs[2:], side_cast)]


def _ret_build(dl_ref, q_ref, k_ref, v_ref, rg_ref, ck_ref, cv_ref, o_ref,
               sf_ref, sb_ref, fs_ref, bs_ref, dec_ref, d_ref):
    h = pl.program_id(0)
    L = SCAN_L
    n_chunks = q_ref.shape[1] // L
    n_ctx_chunks = ck_ref.shape[1] // L
    lgf = _log_sigmoid(jnp.full((1, 1), dl_ref[0, h], F32))
    lgb = _log_sigmoid(jnp.full((1, 1), dl_ref[1, h], F32))

    @pl.when(pl.program_id(1) == 0)
    def _():
        ri = lax.broadcasted_iota(I32, (L, L), 0)
        ci = lax.broadcasted_iota(I32, (L, L), 1)
        rel = (ri - ci).astype(F32)
        d_ref[...] = jnp.where(rel >= 0.0, jnp.exp(jnp.maximum(rel, 0.0) * lgf),
                               jnp.exp(jnp.maximum(-rel, 0.0) * lgb))
        row = lax.broadcasted_iota(I32, (L, HEAD_W), 0).astype(F32)
        dec_ref[0] = jnp.exp((row + 1.0) * lgf)
        dec_ref[1] = jnp.exp((L - 1.0 - row) * lgf)
        dec_ref[2] = jnp.exp((L - row) * lgb)
        dec_ref[3] = jnp.exp(row * lgb)

    cdf = jnp.exp(L * lgf)
    cdb = jnp.exp(L * lgb)

    def update(s_ref, kc, vc, kd, cd):
        kdec = (kc.astype(F32) * kd).astype(BF16)
        s_ref[...] = s_ref[...] * cd + lax.dot_general(kdec, vc, TN_DIMS, preferred_element_type=F32)

    sf_ref[...] = jnp.zeros_like(sf_ref)
    sb_ref[...] = jnp.zeros_like(sb_ref)
    for c in range(n_ctx_chunks):
        update(sf_ref, ck_ref[0, c * L:(c + 1) * L, :], cv_ref[0, c * L:(c + 1) * L, :], dec_ref[1], cdf)
    for c in reversed(range(n_ctx_chunks)):
        update(sb_ref, ck_ref[0, c * L:(c + 1) * L, :], cv_ref[0, c * L:(c + 1) * L, :], dec_ref[3], cdb)

    def state_pass(i, carry):
        cb = n_chunks - 1 - i
        rf = pl.multiple_of(i * L, L)
        rb = pl.multiple_of(cb * L, L)
        fs_ref[i] = sf_ref[...].astype(BF16)
        bs_ref[cb] = sb_ref[...].astype(BF16)
        update(sf_ref, k_ref[0, pl.ds(rf, L), :], v_ref[0, pl.ds(rf, L), :], dec_ref[1], cdf)
        update(sb_ref, k_ref[0, pl.ds(rb, L), :], v_ref[0, pl.ds(rb, L), :], dec_ref[3], cdb)
        return carry

    def finish_states():
        fs_ref[n_chunks - 1] = sf_ref[...].astype(BF16)
        bs_ref[0] = sb_ref[...].astype(BF16)

    def out_chunk(c):
        r0 = pl.multiple_of(c * L, L)
        q = q_ref[0, pl.ds(r0, L), :]
        k = k_ref[0, pl.ds(r0, L), :]
        v = v_ref[0, pl.ds(r0, L), :]
        s = lax.dot_general(q, k, NT_DIMS, preferred_element_type=F32)
        att = (s * d_ref[...]).astype(BF16)
        o = jnp.dot(att, v, preferred_element_type=F32)
        o = o + jnp.dot(q, fs_ref[c], preferred_element_type=F32) * dec_ref[0]
        o = o + jnp.dot(q, bs_ref[c], preferred_element_type=F32) * dec_ref[2]
        rg = rg_ref[0, pl.ds(r0, L), :].astype(F32)
        o_ref[0, pl.ds(r0, L), :] = (_layer_norm(o) * _silu(rg)).astype(BF16)

    return state_pass, finish_states, out_chunk


def _ret_scratch(n_chunks):
    return [pltpu.VMEM((HEAD_W, HEAD_W), F32),
            pltpu.VMEM((HEAD_W, HEAD_W), F32),
            pltpu.VMEM((n_chunks, HEAD_W, HEAD_W), BF16),
            pltpu.VMEM((n_chunks, HEAD_W, HEAD_W), BF16),
            pltpu.VMEM((4, SCAN_L, HEAD_W), F32),
            pltpu.VMEM((SCAN_L, SCAN_L), F32)]


def _mlstm_build(qp_ref, kp_ref, v_ref, mo_ref, ckp_ref, cv_ref, gt_ref, cgt_ref,
                 wq_ref, bq_ref, wk_ref, bk_ref, o_ref,
                 tab_ref, row_ref,
                 cf_ref, mf_ref, cb_ref, mb_ref, cfs_ref, mfs_ref, cbs_ref, mbs_ref, mask_ref,
                 xk_ref, xq_ref, q_ref, k_ref, ck_ref):
    L = SCAN_L
    T = qp_ref.shape[1]
    Tc = ckp_ref.shape[1]
    n_chunks = T // L
    n_ctx_chunks = Tc // L
    CV = CONV_ROWS

    def conv_stage(src_ref, xs_ref, t_len):
        xs_ref[pl.ds(0, 8), :] = jnp.zeros((8, HEAD_W), F32)
        xs_ref[pl.ds(8 + t_len, 8), :] = jnp.zeros((8, HEAD_W), F32)
        xs_ref[pl.ds(8, t_len), :] = src_ref[0].astype(F32)

    def conv_rows(xs_ref, c, w_ref, b_ref, dst_ref, scale):
        w = w_ref[...]
        r0 = pl.multiple_of(c * CV, CV)
        win = xs_ref[pl.ds(r0, CV + 16), :]
        prev = pltpu.roll(win, 1, 0)[8:8 + CV, :]
        cur = win[8:8 + CV, :]
        nxt = pltpu.roll(win, CV + 15, 0)[8:8 + CV, :]
        y = _silu(prev * w[0:1, :] + cur * w[1:2, :] + nxt * w[2:3, :] + b_ref[...])
        if scale != 1.0:
            y = y * scale
        dst_ref[pl.ds(r0, CV), :] = y.astype(BF16)

    per_chunk = L // CV

    def conv_k(c):
        for u in range(per_chunk):
            conv_rows(xk_ref, c * per_chunk + u, wk_ref, bk_ref, k_ref, KEY_SCALE)

    def conv_q(c):
        for u in range(per_chunk):
            conv_rows(xq_ref, c * per_chunk + u, wq_ref, bq_ref, q_ref, 1.0)

    conv_stage(ckp_ref, xk_ref, Tc)
    for c in range(Tc // CV):
        conv_rows(xk_ref, c, wk_ref, bk_ref, ck_ref, KEY_SCALE)
    conv_stage(kp_ref, xk_ref, T)
    conv_stage(qp_ref, xq_ref, T)
    conv_k(0)
    conv_k(n_chunks - 1)
    n_k_steps = n_chunks // 2 - 1

    def conv_step(i, keys):
        if keys:
            conv_k(i + 1)
            conv_k(n_chunks - 2 - i)
        else:
            conv_q(2 * (i - n_k_steps))
            conv_q(2 * (i - n_k_steps) + 1)

    ri = lax.broadcasted_iota(I32, (L, L), 0)
    ci = lax.broadcasted_iota(I32, (L, L), 1)
    tri_u = (ri <= ci).astype(BF16)
    lane8 = lax.broadcasted_iota(I32, (8, L), 1)
    sub8 = lax.broadcasted_iota(I32, (8, L), 0)
    sel_r = lax.broadcasted_iota(I32, (24, 8 * 128), 0) % 8
    sel_c = lax.broadcasted_iota(I32, (24, 8 * 128), 1) // 128
    sel3 = (sel_r == sel_c).astype(BF16)
    ones_cols = jnp.ones((L, AUG_W - HEAD_W), BF16)

    def chunk_tables(g8, n_used, state_only):
        i_f, i_b = g8[0], g8[2]
        lf_f, lf_b = _log_sigmoid(g8[1]), _log_sigmoid(g8[3])
        cs3 = jnp.dot(_split3(jnp.concatenate([lf_f, lf_b], axis=0)), tri_u,
                      preferred_element_type=F32)
        cs = cs3[0:16] + cs3[16:32] + cs3[32:48]
        b_f = cs[0:8]
        b_b = cs[8:16, L - 1:L] - cs[8:16] + lf_b
        z_f = i_f - b_f
        z_b = i_b - b_b
        g_f = b_f[:, L - 1:L] - b_f + i_f
        g_b = b_b[:, 0:1] - b_b + i_b
        mf, mb = z_f, z_b
        s = 1
        while s < L:
            mf = jnp.maximum(mf, jnp.where(lane8 >= s, pltpu.roll(mf, s, 1), NEG_INF))
            mb = jnp.maximum(mb, jnp.where(lane8 < L - s, pltpu.roll(mb, L - s, 1), NEG_INF))
            s *= 2
        mb = jnp.where(lane8 < L - 1, pltpu.roll(mb, L - 1, 1), NEG_INF)
        reps = [None if state_only and t not in (2, 5) else
                lax.dot_general(_split3(val), sel3[:, 0:n_used * 128], TN_DIMS, preferred_element_type=F32)
                for t, val in enumerate((mf, b_f, g_f, mb, b_b, g_b))]

        def rows_of(c):
            out = jnp.zeros((8, L), F32)
            for r, val in enumerate((z_f, z_b, g_f, g_b, b_f, b_b)):
                out = jnp.where(sub8 == r, val[c:c + 1], out)
            return out

        return rows_of, reps

    lat_rows, lat_reps = chunk_tables(gt_ref[0, 0], n_chunks, False)
    for c in range(n_chunks):
        row_ref[c] = lat_rows(c)
        for t in range(N_TAB):
            tab_ref[t, c * L:(c + 1) * L, :] = lat_reps[t][:, c * 128:(c + 1) * 128]

    def lanes2(x):
        return jnp.concatenate([x, x], axis=1)

    def advance(k, v, g_rep, g_row, b_last, c_ref, m_ref):
        m = m_ref[...]
        m_new = jnp.maximum(b_last + m, jnp.max(g_row, axis=-1, keepdims=True))
        kw = (k.astype(F32) * jnp.exp(lanes2(g_rep) - m_new)).astype(BF16)
        v_aug = jnp.concatenate([v, ones_cols], axis=1)
        c_ref[...] = jnp.exp(b_last + m - m_new) * c_ref[...] + lax.dot_general(
            kw, v_aug, TN_DIMS, preferred_element_type=F32)
        m_ref[...] = m_new

    for r in (cf_ref, mf_ref, cb_ref, mb_ref):
        r[...] = jnp.zeros_like(r)
    ctx_rows, ctx_reps = chunk_tables(cgt_ref[0, 0], n_ctx_chunks, True)
    for c in range(n_ctx_chunks):
        rows = ctx_rows(c)
        advance(ck_ref[c * L:(c + 1) * L, :], cv_ref[0, c * L:(c + 1) * L, :],
                ctx_reps[2][:, c * 128:(c + 1) * 128], rows[2:3], rows[4:5, L - 1:L], cf_ref, mf_ref)
    for c in reversed(range(n_ctx_chunks)):
        rows = ctx_rows(c)
        advance(ck_ref[c * L:(c + 1) * L, :], cv_ref[0, c * L:(c + 1) * L, :],
                ctx_reps[5][:, c * 128:(c + 1) * 128], rows[3:4], rows[5:6, 0:1], cb_ref, mb_ref)

    def state_pass(i, carry, keys):
        cb = n_chunks - 1 - i
        rf = pl.multiple_of(i * L, L)
        rb = pl.multiple_of(cb * L, L)
        cfs_ref[i] = cf_ref[...].astype(BF16)
        mfs_ref[i] = mf_ref[...]
        cbs_ref[cb] = cb_ref[...].astype(BF16)
        mbs_ref[cb] = mb_ref[...]
        rows_f = row_ref[i]
        rows_b = row_ref[cb]
        advance(k_ref[pl.ds(rf, L), :], v_ref[0, pl.ds(rf, L), :], tab_ref[2, pl.ds(rf, L), :],
                rows_f[2:3], rows_f[4:5, L - 1:L], cf_ref, mf_ref)
        advance(k_ref[pl.ds(rb, L), :], v_ref[0, pl.ds(rb, L), :], tab_ref[5, pl.ds(rb, L), :],
                rows_b[3:4], rows_b[5:6, 0:1], cb_ref, mb_ref)
        conv_step(i, keys)
        return carry

    def finish_states():
        cfs_ref[n_chunks - 1] = cf_ref[...].astype(BF16)
        mfs_ref[n_chunks - 1] = mf_ref[...]
        cbs_ref[0] = cb_ref[...].astype(BF16)
        mbs_ref[0] = mb_ref[...]

    def direction(q, v_aug, s, z_row, zmax_rep, b_rep, mask, c_in, m_in):
        mx = jnp.maximum(zmax_rep, m_in)
        att = s * jnp.exp((z_row - lanes2(mx)) + mask)
        na = jnp.dot(att.astype(BF16), v_aug, preferred_element_type=F32)
        qa = jnp.dot(q, c_in, preferred_element_type=F32)
        a = jnp.exp(m_in - mx)
        num = na[:, 0:HEAD_W] + lanes2(a) * qa[:, 0:HEAD_W]
        den = na[:, HEAD_W:] + a * qa[:, HEAD_W:]
        scale = 1.0 / jnp.maximum(jnp.abs(den), jnp.exp(-(b_rep + mx)))
        return num * lanes2(scale)

    @pl.when(pl.program_id(1) == 0)
    def _():
        mask_ref[0] = jnp.where(ci <= ri, 0.0, NEG_INF)
        mask_ref[1] = jnp.where(ci > ri, 0.0, NEG_INF)

    def out_chunk(c):
        r0 = pl.multiple_of(c * L, L)
        q = q_ref[pl.ds(r0, L), :]
        k = k_ref[pl.ds(r0, L), :]
        v_aug = jnp.concatenate([v_ref[0, pl.ds(r0, L), :], ones_cols], axis=1)
        s = lax.dot_general(q, k, NT_DIMS, preferred_element_type=F32)
        rows = row_ref[c]
        tot = direction(q, v_aug, s, rows[0:1], tab_ref[0, pl.ds(r0, L), :], tab_ref[1, pl.ds(r0, L), :],
                        mask_ref[0], cfs_ref[c], mfs_ref[c])
        tot = tot + direction(q, v_aug, s, rows[1:2], tab_ref[3, pl.ds(r0, L), :],
                              tab_ref[4, pl.ds(r0, L), :], mask_ref[1], cbs_ref[c], mbs_ref[c])
        mo = mo_ref[0, pl.ds(r0, L), :].astype(F32)
        o_ref[0, pl.ds(r0, L), :] = (_layer_norm(tot) * jax.nn.sigmoid(mo)).astype(BF16)

    return state_pass, finish_states, out_chunk, n_k_steps


def _mlstm_scratch(T, Tc, n_chunks):
    state = [pltpu.VMEM((HEAD_W, AUG_W), F32), pltpu.VMEM((1, 1), F32)]
    snaps = [pltpu.VMEM((n_chunks, HEAD_W, AUG_W), BF16), pltpu.VMEM((n_chunks, 1, 1), F32)]
    return [pltpu.VMEM((N_TAB, T, 128), F32), pltpu.VMEM((n_chunks, 8, SCAN_L), F32)] \
        + state + state + snaps + snaps + [pltpu.VMEM((2, SCAN_L, SCAN_L), F32)] \
        + [pltpu.VMEM((T + 16, HEAD_W), F32), pltpu.VMEM((T + 16, HEAD_W), F32),
           pltpu.VMEM((T, HEAD_W), BF16), pltpu.VMEM((T, HEAD_W), BF16), pltpu.VMEM((Tc, HEAD_W), BF16)]


def _scan_kernel(n_ret_scratch, dl_ref, rq_ref, rk_ref, rv_ref, rg_ref, rck_ref, rcv_ref,
                 mq_ref, mk_ref, mv_ref, mo_ref, mck_ref, mcv_ref, gt_ref, cgt_ref,
                 wq_ref, bq_ref, wk_ref, bk_ref, r_ref, m_ref, *scratch):
    n_chunks = rq_ref.shape[1] // SCAN_L
    ret = _ret_build(dl_ref, rq_ref, rk_ref, rv_ref, rg_ref, rck_ref, rcv_ref, r_ref,
                     *scratch[:n_ret_scratch])
    mls = _mlstm_build(mq_ref, mk_ref, mv_ref, mo_ref, mck_ref, mcv_ref, gt_ref, cgt_ref,
                       wq_ref, bq_ref, wk_ref, bk_ref, m_ref, *scratch[n_ret_scratch:])

    def state_pass(i, carry, keys):
        ret[0](i, carry)
        mls[0](i, carry, keys)
        return carry

    n_k_steps = mls[3]
    lax.fori_loop(0, n_k_steps, functools.partial(state_pass, keys=True), 0)
    lax.fori_loop(n_k_steps, n_chunks - 1, functools.partial(state_pass, keys=False), 0)
    ret[1]()
    mls[1]()

    def out_pass(i, carry):
        for c in (2 * i, 2 * i + 1):
            ret[2](c)
            mls[2](c)
        return carry

    lax.fori_loop(0, n_chunks // 2, out_pass, 0)


def _scans(decay_logit, p_lat, p_ctx, gt, cgt, conv_w, conv_b, ret_lat, ret_ctx, ml_lat, ml_ctx):
    B, T, _ = p_lat.shape
    Tc = p_ctx.shape[1]
    assert T % (2 * SCAN_L) == 0 and Tc % SCAN_L == 0 and T // SCAN_L <= 8
    n_chunks = T // SCAN_L

    def lat(sec):
        return pl.BlockSpec((1, T, HEAD_W), lambda h, b: (b, 0, sec * HEADS + h))

    def cx(sec):
        return pl.BlockSpec((1, Tc, HEAD_W), lambda h, b: (b, 0, sec * HEADS + h))

    gates = pl.BlockSpec((1, 1, N_GK, 8, SCAN_L), lambda h, b: (b, h, 0, 0, 0))
    out = pl.BlockSpec((1, T, HEAD_W), lambda h, b: (b, 0, h))
    conv_specs = [pl.BlockSpec((3, HEAD_W), lambda h, b: (0, h)),
                  pl.BlockSpec((1, HEAD_W), lambda h, b: (0, h)),
                  pl.BlockSpec((3, HEAD_W), lambda h, b: (0, HEADS + h)),
                  pl.BlockSpec((1, HEAD_W), lambda h, b: (0, HEADS + h))]
    ret_scratch = _ret_scratch(n_chunks)
    return pl.pallas_call(
        functools.partial(_scan_kernel, len(ret_scratch)),
        out_shape=(jax.ShapeDtypeStruct((B, T, BRANCH_W), BF16),
                   jax.ShapeDtypeStruct((B, T, BRANCH_W), BF16)),
        grid=(HEADS, B),
        in_specs=[pl.BlockSpec(memory_space=pltpu.SMEM)]
        + [lat(s) for s in ret_lat] + [cx(s) for s in ret_ctx]
        + [lat(s) for s in ml_lat] + [cx(s) for s in ml_ctx] + [gates, gates] + conv_specs,
        out_specs=(out, out),
        scratch_shapes=ret_scratch + _mlstm_scratch(T, Tc, n_chunks),
        compiler_params=_cparams("arbitrary", "arbitrary"),
        name="scans",
    )(decay_logit, *([p_lat] * 4), *([p_ctx] * 2), *([p_lat] * 4), *([p_ctx] * 2), gt, cgt,
      conv_w, conv_b, conv_w, conv_b)


def _merge_kernel(alpha, r_ref, m_ref, gr_ref, gm_ref, x_ref, g1_ref, sh2_ref, sc2_ref,
                  lng_ref, lnb_ref, wr_ref, wm_ref, wo_ref, wrt_ref, brt_ref, e1_ref, e3_ref, e2_ref,
                  x1_ref, ua_ref, ub_ref, ri_ref, rw_ref, cnt_ref, e1b_ref, e3b_ref, e2b_ref,
                  carry_ref, u_ref):
    for src, dst in ((e1_ref, e1b_ref), (e3_ref, e3b_ref), (e2_ref, e2b_ref)):
        dst[...] = src[...].astype(BF16)

    @pl.when(jnp.logical_and(pl.program_id(0) == 0, pl.program_id(1) == 0))
    def _():
        carry_ref[...] = jnp.zeros_like(carry_ref)
        tm = x_ref.shape[1]
        r = lax.broadcasted_iota(I32, (tm, tm), 0)
        c = lax.broadcasted_iota(I32, (tm, tm), 1)
        u_ref[...] = (r < c).astype(BF16)

    yr = jnp.dot(r_ref[0], wr_ref[...], preferred_element_type=F32)
    ym = jnp.dot(m_ref[0], wm_ref[...], preferred_element_type=F32)
    y = jax.nn.sigmoid(gr_ref[0].astype(F32)) * yr + jax.nn.sigmoid(gm_ref[0].astype(F32)) * ym
    yo = jnp.dot(y.astype(BF16), wo_ref[...], preferred_element_type=F32)
    x1 = _layer_norm(alpha * x_ref[0] + g1_ref[0] * yo) * lng_ref[...] + lnb_ref[...]
    x1_ref[0] = x1
    u2 = _layer_norm(x1) * (1.0 + sc2_ref[0]) + sh2_ref[0]
    ua_ref[0] = _pack_pairs(u2[:, 0:PACK_W], u2[:, PACK_W:2 * PACK_W])
    ub_ref[0] = _pack_pairs(u2[:, 2 * PACK_W:3 * PACK_W], u2[:, 3 * PACK_W:4 * PACK_W])
    lt = lax.dot_general(wrt_ref[...], u2.astype(BF16), NT_DIMS, preferred_element_type=F32) + brt_ref[...]
    _route_tile(lt, ri_ref, rw_ref, cnt_ref, carry_ref, u_ref)


def _route_tile(lt, ri_ref, rw_ref, cnt_ref, carry_ref, u_ref):
    tm = lt.shape[1]
    lg = lt[0:N_GROUPS, :]
    eg = jnp.exp(lg - jnp.max(lg, axis=0, keepdims=True))
    pg = eg / jnp.sum(eg, axis=0, keepdims=True)
    pg_top = jnp.max(pg, axis=0, keepdims=True)
    rows_g = lax.broadcasted_iota(I32, pg.shape, 0)
    g_idx = jnp.min(jnp.where(pg == pg_top, rows_g, N_GROUPS), axis=0, keepdims=True)

    le = jnp.zeros((EXP_PER_GROUP, tm), F32)
    for g in range(N_GROUPS):
        lo = 8 + g * EXP_PER_GROUP
        le = jnp.where(g_idx == g, lt[lo:lo + EXP_PER_GROUP, :], le)
    ee = jnp.exp(le - jnp.max(le, axis=0, keepdims=True))
    pe = ee / jnp.sum(ee, axis=0, keepdims=True)
    rows_e = lax.broadcasted_iota(I32, pe.shape, 0)
    v1 = jnp.max(pe, axis=0, keepdims=True)
    i1 = jnp.min(jnp.where(pe == v1, rows_e, EXP_PER_GROUP), axis=0, keepdims=True)
    pe2 = jnp.where(rows_e == i1, -1.0, pe)
    v2 = jnp.max(pe2, axis=0, keepdims=True)
    i2 = jnp.min(jnp.where(pe2 == v2, rows_e, EXP_PER_GROUP), axis=0, keepdims=True)
    den = v1 + v2
    rw_ref[...] = jnp.zeros_like(rw_ref)
    rw_ref[0:1, :] = pg_top * v1 / den
    rw_ref[1:2, :] = pg_top * v2 / den
    e1 = g_idx * EXP_PER_GROUP + i1
    e2 = g_idx * EXP_PER_GROUP + i2

    rows_x = lax.broadcasted_iota(I32, (N_EXPERTS, tm), 0)
    oh1 = (rows_x == e1).astype(F32)
    oh2 = (rows_x == e2).astype(F32)
    both = oh1 + oh2
    before = carry_ref[:, 0:1] + jnp.dot(both.astype(BF16), u_ref[...], preferred_element_type=F32)
    ri_ref[0:1, :] = e1
    ri_ref[1:2, :] = e2
    ri_ref[2:3, :] = jnp.sum(oh1 * before, axis=0, keepdims=True).astype(I32)
    ri_ref[3:4, :] = jnp.sum(oh2 * before, axis=0, keepdims=True).astype(I32)
    carry_ref[...] = carry_ref[...] + jnp.sum(both, axis=1, keepdims=True)
    cnt_ref[...] = carry_ref[...].astype(I32)


def _merge(alpha, r, m, p_lat, sec_gates, x, g1, sh2, sc2, lng, lnb, wr, wm, wo, wrt, brt, expert_w):
    B, T, D = x.shape
    tm = MERGE_TM
    per_b = T // tm
    n = B * T
    n_steps = B * per_b
    sliced = [w.reshape(n_steps, w.shape[0] * w.shape[1] // n_steps, w.shape[2]) for w in expert_w]
    assert all(s.shape[1] % 16 == 0 and s.size == w.size for s, w in zip(sliced, expert_w))

    def slab(s):
        return pl.BlockSpec((1,) + s.shape[1:], lambda b, i: (b * per_b + i, 0, 0))

    def tile(w):
        return pl.BlockSpec((1, tm, w), lambda b, i: (b, i, 0))

    def sec(s):
        return pl.BlockSpec((1, tm, BRANCH_W), lambda b, i: (b, i, s))

    def mod():
        return pl.BlockSpec((1, 1, D), lambda b, i: (b, 0, 0))

    def const(shape):
        return pl.BlockSpec(shape, lambda b, i: (0,) * len(shape))

    outs = pl.pallas_call(
        functools.partial(_merge_kernel, alpha),
        out_shape=(jax.ShapeDtypeStruct((B, T, D), F32),
                   jax.ShapeDtypeStruct((B, T, PACK_W), U32),
                   jax.ShapeDtypeStruct((B, T, PACK_W), U32),
                   jax.ShapeDtypeStruct((4, n), I32),
                   jax.ShapeDtypeStruct((8, n), F32),
                   jax.ShapeDtypeStruct((N_EXPERTS, 128), I32))
        + tuple(jax.ShapeDtypeStruct(s.shape, BF16) for s in sliced),
        grid=(B, per_b),
        in_specs=[tile(BRANCH_W), tile(BRANCH_W), sec(sec_gates[0]), sec(sec_gates[1]), tile(D),
                  mod(), mod(), mod(), const((1, D)), const((1, D)),
                  const((BRANCH_W, D)), const((BRANCH_W, D)), const((D, D)),
                  const((ROUTE_ROWS, D)), const((ROUTE_ROWS, 1))] + [slab(s) for s in sliced],
        out_specs=(tile(D), tile(PACK_W), tile(PACK_W),
                   pl.BlockSpec((4, tm), lambda b, i: (0, b * per_b + i)),
                   pl.BlockSpec((8, tm), lambda b, i: (0, b * per_b + i)),
                   pl.BlockSpec((N_EXPERTS, 128), lambda b, i: (0, 0)))
        + tuple(slab(s) for s in sliced),
        scratch_shapes=[pltpu.VMEM((N_EXPERTS, 128), F32), pltpu.VMEM((tm, tm), BF16)],
        compiler_params=_cparams("arbitrary", "arbitrary"),
        name="merge",
    )(r, m, p_lat, p_lat, x, g1, sh2, sc2, lng, lnb, wr, wm, wo, wrt, brt, *sliced)
    return outs[:6] + tuple(o.reshape(w.shape) for o, w in zip(outs[6:], expert_w))


def _sc_mesh():
    return plsc.VectorSubcoreMesh(core_axis_name="c", subcore_axis_name="s")


def _sc_scatter2(rows_a, rows_b, idx0, idx1, n_out):
    m, w = rows_a.shape
    out = jax.ShapeDtypeStruct((n_out, w), rows_a.dtype)

    @functools.partial(pl.kernel, out_type=(out, out), mesh=_sc_mesh(), scratch_types=[])
    def k(xa_hbm, xb_hbm, i0_hbm, i1_hbm, oa_hbm, ob_hbm):
        for x_hbm, o_hbm in ((xa_hbm, oa_hbm), (xb_hbm, ob_hbm)):
            def body(x_vmem, i0_vmem, i1_vmem, o_hbm=o_hbm):
                pltpu.sync_copy(x_vmem, o_hbm.at[i0_vmem.at[0]])
                pltpu.sync_copy(x_vmem, o_hbm.at[i1_vmem.at[0]])

            pltpu.emit_pipeline(
                body,
                grid=(m // SC_WIN,),
                in_specs=[pl.BlockSpec((SC_WIN, w), lambda i: (i, 0)),
                          pl.BlockSpec((1, SC_WIN), lambda i: (0, i)),
                          pl.BlockSpec((1, SC_WIN), lambda i: (0, i))],
                out_specs=[],
                core_axis_name=("c", "s"),
                dimension_semantics=(pltpu.PARALLEL,),
            )(x_hbm, i0_hbm, i1_hbm)

    return k(rows_a, rows_b, idx0.reshape(1, m), idx1.reshape(1, m))


def _sc_gather(table_a, table_b, idx):
    m = idx.shape[0]
    w = table_a.shape[1]
    out = jax.ShapeDtypeStruct((m, w), table_a.dtype)

    @functools.partial(pl.kernel, out_type=(out, out), mesh=_sc_mesh(), scratch_types=[])
    def k(ta_hbm, tb_hbm, i_hbm, oa_hbm, ob_hbm):
        for t_hbm, o_hbm in ((ta_hbm, oa_hbm), (tb_hbm, ob_hbm)):
            def body(i_vmem, o_vmem, t_hbm=t_hbm):
                pltpu.sync_copy(t_hbm.at[i_vmem.at[0]], o_vmem)

            pltpu.emit_pipeline(
                body,
                grid=(m // SC_WIN,),
                in_specs=[pl.BlockSpec((1, SC_WIN), lambda i: (0, i))],
                out_specs=[pl.BlockSpec((SC_WIN, w), lambda i: (i, 0))],
                core_axis_name=("c", "s"),
                dimension_semantics=(pltpu.PARALLEL,),
            )(i_hbm, o_hbm)

    return k(table_a, table_b, idx.reshape(1, m))


def _expert_kernel(be_ref, nv_ref, xa_ref, xb_ref, w1_ref, w3_ref, w2_ref, ya_ref, yb_ref):
    j = pl.program_id(0)
    nv = nv_ref[j]

    @pl.when(nv > 0)
    def _():
        valid = lax.broadcasted_iota(I32, xa_ref.shape, 0) < nv
        zero = jnp.zeros(xa_ref.shape, U32)
        parts = _unpack_pairs(jnp.where(valid, xa_ref[...], zero)) + \
            _unpack_pairs(jnp.where(valid, xb_ref[...], zero))
        x = jnp.concatenate([p.astype(BF16) for p in parts], axis=1)
        h1 = jnp.dot(x, w1_ref[0], preferred_element_type=F32)
        h3 = jnp.dot(x, w3_ref[0], preferred_element_type=F32)
        y = jnp.dot((_silu(h1) * h3).astype(BF16), w2_ref[0], preferred_element_type=F32)
        ya_ref[...] = _pack_pairs(y[:, 0:PACK_W], y[:, PACK_W:2 * PACK_W])
        yb_ref[...] = _pack_pairs(y[:, 2 * PACK_W:3 * PACK_W], y[:, 3 * PACK_W:4 * PACK_W])

    @pl.when(nv == 0)
    def _():
        ya_ref[...] = jnp.zeros_like(ya_ref)
        yb_ref[...] = jnp.zeros_like(yb_ref)


def _experts(block_exp, n_valid, xa, xb, w1, w3, w2):
    n_slots = xa.shape[0]
    n_blocks = n_slots // MOE_BLK
    d, de = w1.shape[1], w1.shape[2]
    slot = pl.BlockSpec((MOE_BLK, PACK_W), lambda j, be, nv: (j, 0))
    grid_spec = pltpu.PrefetchScalarGridSpec(
        num_scalar_prefetch=2,
        grid=(n_blocks,),
        in_specs=[slot, slot,
                  pl.BlockSpec((1, d, de), lambda j, be, nv: (be[j], 0, 0)),
                  pl.BlockSpec((1, d, de), lambda j, be, nv: (be[j], 0, 0)),
                  pl.BlockSpec((1, de, d), lambda j, be, nv: (be[j], 0, 0))],
        out_specs=(slot, slot),
    )
    return pl.pallas_call(
        _expert_kernel,
        out_shape=(jax.ShapeDtypeStruct((n_slots, PACK_W), U32),
                   jax.ShapeDtypeStruct((n_slots, PACK_W), U32)),
        grid_spec=grid_spec,
        compiler_params=_cparams("parallel"),
        name="experts",
    )(block_exp, n_valid, xa, xb, w1, w3, w2)


def _final_kernel(alpha, x1_ref, a0_ref, b0_ref, a1_ref, b1_ref, w_ref, g2_ref, lng_ref, lnb_ref, o_ref):
    w = w_ref[...].T
    w0 = w[:, 0:1]
    w1 = w[:, 1:2]
    parts0 = _unpack_pairs(a0_ref[...]) + _unpack_pairs(b0_ref[...])
    parts1 = _unpack_pairs(a1_ref[...]) + _unpack_pairs(b1_ref[...])
    f = jnp.concatenate([w0 * p0 + w1 * p1 for p0, p1 in zip(parts0, parts1)], axis=1)
    o_ref[0] = _layer_norm(alpha * x1_ref[0] + g2_ref[0] * f) * lng_ref[...] + lnb_ref[...]


def _final(alpha, x1, ya, yb, w, g2, lng, lnb):
    B, T, D = x1.shape
    tm = min(FINAL_TM, T)
    per_b = T // tm
    n_tiles = B * per_b

    def rows(k):
        return pl.BlockSpec((tm, PACK_W), lambda b, i: (k * n_tiles + b * per_b + i, 0))

    return pl.pallas_call(
        functools.partial(_final_kernel, alpha),
        out_shape=jax.ShapeDtypeStruct((B, T, D), F32),
        grid=(B, per_b),
        in_specs=[pl.BlockSpec((1, tm, D), lambda b, i: (b, i, 0)),
                  rows(0), rows(0), rows(1), rows(1),
                  pl.BlockSpec((8, tm), lambda b, i: (0, b * per_b + i)),
                  pl.BlockSpec((1, 1, D), lambda b, i: (b, 0, 0)),
                  pl.BlockSpec((1, D), lambda b, i: (0, 0)),
                  pl.BlockSpec((1, D), lambda b, i: (0, 0))],
        out_specs=pl.BlockSpec((1, tm, D), lambda b, i: (b, i, 0)),
        compiler_params=_cparams("parallel", "parallel"),
        name="final",
    )(x1, ya, yb, ya, yb, w, g2, lng, lnb)


def _rotary_tables(T):
    quarter = HEAD_W // 4
    freqs = ROPE_BASE ** (-jnp.arange(quarter, dtype=F32) / quarter)
    t = jnp.arange(T)
    ang_r = (t // GRID_W).astype(F32)[:, None] * freqs[None, :]
    ang_c = (t % GRID_W).astype(F32)[:, None] * freqs[None, :]
    cos = jnp.concatenate([jnp.cos(ang_r)] * 2 + [jnp.cos(ang_c)] * 2, axis=1)
    sin = jnp.concatenate([-jnp.sin(ang_r), jnp.sin(ang_r), -jnp.sin(ang_c), jnp.sin(ang_c)], axis=1)
    return cos, sin


def _per_head_gates(gt):
    B, _, T = gt.shape
    n_chunks = T // SCAN_L
    gth = gt.reshape(B, N_GK, HEADS, n_chunks, SCAN_L).transpose(0, 2, 1, 3, 4)
    return jnp.pad(gth, ((0, 0), (0, 0), (0, 0), (0, 8 - n_chunks), (0, 0)))


def _table_lookup(table, idx):
    sel = idx[..., None] == jnp.arange(table.shape[0], dtype=idx.dtype)
    return jnp.sum(jnp.where(sel, table, 0), axis=-1)


def kernel(x, c, ctx, c_ctx, w_ada, b_ada, w_in, b_mgate, ml_conv_w, ml_conv_b, ret_decay_logit, w_ret_branch, w_ml_branch, w_out, ln1_g, ln1_b, w_rg, b_rg, w_re, b_re, w_e1, w_e3, w_e2, ln2_g, ln2_b):
    B, T, D = x.shape
    depth = w_ada.shape[0]
    assert depth == 1 and D == BRANCH_W and T % GRID_W == 0
    alpha = (2 * depth) ** 0.25
    n_tok = B * T

    n_rows = -(-(B + 1) // 8) * 8
    cs = jnp.zeros((n_rows, D), F32).at[:B].set(c).at[B].set(c_ctx)
    mod = _ada(cs, w_ada[0], b_ada[0][None, :])
    sh1, sc1, g1, sh2, sc2, g2 = [mod[:B, None, i * D:(i + 1) * D] for i in range(6)]
    csh1 = mod[B, 0 * D:1 * D].reshape(1, 1, D)
    csc1 = mod[B, 1 * D:2 * D].reshape(1, 1, D)

    w = w_in[0]
    g_lo = 8 * BRANCH_W
    w_gate_t = w[:, g_lo:g_lo + N_GATES].T.astype(BF16)
    b_gate = b_mgate[0][:, None]
    w_head = w[:, :g_lo].astype(BF16)
    w_tail = w[:, g_lo + N_GATES:].astype(BF16)
    sec_lat = tuple(("h", s) for s in range(8)) + (("t", 0), ("t", 1))
    sec_ctx = (("h", 1), ("h", 2), ("h", 5), ("h", 6))
    kinds_lat = ("rot", "rot_scale") + ("plain",) * 8
    kinds_ctx = ("scale", "plain", "plain", "plain")
    p_lat, gt_lat, _ = _proj(x, sh1, sc1, w_head, w_tail, sec_lat, w_gate_t, b_gate, kinds_lat,
                             _rotary_tables(T))
    Tc = ctx.shape[1]
    p_ctx, gt_ctx, (w_rb, w_mb, w_ob) = _proj(
        ctx.reshape(1, B * Tc, D), csh1, csc1, w_head, w_tail, sec_ctx, w_gate_t, b_gate, kinds_ctx,
        side_cast=(w_ret_branch[0], w_ml_branch[0], w_out[0]))
    p_ctx = p_ctx.reshape(B, Tc, -1)
    gt_ctx = gt_ctx.reshape(N_GATES, B, Tc).transpose(1, 0, 2)

    ret, mls = _scans(ret_decay_logit[0], p_lat, p_ctx, _per_head_gates(gt_lat), _per_head_gates(gt_ctx),
                      ml_conv_w[0], ml_conv_b[0][None, :], (0, 1, 2, 3), (0, 1), (4, 5, 6, 7), (2, 3))

    wrt = jnp.zeros((ROUTE_ROWS, D), F32).at[:N_GROUPS].set(w_rg[0].T).at[8:8 + N_EXPERTS].set(w_re[0].T)
    brt = jnp.zeros((ROUTE_ROWS, 1), F32).at[:N_GROUPS, 0].set(b_rg[0]).at[8:8 + N_EXPERTS, 0].set(b_re[0])
    x1, ua, ub, ri, rw, cnt, we1, we3, we2 = _merge(
        alpha, ret, mls, p_lat, (8, 9), x, g1, sh2, sc2, ln1_g[0][None, :], ln1_b[0][None, :],
        w_rb, w_mb, w_ob, wrt.astype(BF16), brt, (w_e1[0], w_e3[0], w_e2[0]))

    counts = cnt[:, 0]
    padded = (counts + MOE_BLK - 1) // MOE_BLK * MOE_BLK
    pad_end = jnp.cumsum(padded)
    pad_off = pad_end - padded
    dest = _table_lookup(pad_off, ri[0:2]) + ri[2:4]
    n_blocks = (2 * n_tok) // MOE_BLK + N_EXPERTS
    n_slots = n_blocks * MOE_BLK
    block_start = jnp.arange(n_blocks, dtype=I32) * MOE_BLK
    block_exp = jnp.minimum((block_start[:, None] >= pad_end[None, :]).sum(1), N_EXPERTS - 1).astype(I32)
    n_valid = jnp.clip(_table_lookup(counts, block_exp) - (block_start - _table_lookup(pad_off, block_exp)),
                       0, MOE_BLK).astype(I32)

    xa, xb = _sc_scatter2(ua.reshape(n_tok, PACK_W), ub.reshape(n_tok, PACK_W), dest[0], dest[1], n_slots)
    ya, yb = _experts(block_exp, n_valid, xa, xb, we1, we3, we2)
    ga, gb = _sc_gather(ya, yb, dest.reshape(2 * n_tok))
    return _final(alpha, x1, ga, gb, rw, g2, ln2_g[0][None, :], ln2_b[0][None, :])
```
